```python
import jax, jax.numpy as jnp
from jax import lax
import numpy as np

D_MODEL = 1024
BATCH = 8
SEQ = 8192
DEPTH = 1

CONV_CH = D_MODEL // 2
N_CONV_GROUPS = 8
HEAD_DIM = 64
N_HEADS = (D_MODEL // 2) // HEAD_DIM
ATTN_W = N_HEADS * HEAD_DIM
MIX_W = CONV_CH + ATTN_W
CONV_K = 3
D_FF = ((8 * D_MODEL) // 3 + 255) // 256 * 256
Q_BLOCK = 128
EPS = 1e-6
FORGET_BIAS_INIT = 3.0
IN_COLS = 3 * CONV_CH + 3 * ATTN_W + N_HEADS

kernel_name = "hymba_conv_fox_convffn_layer"


def rmsnorm(x, g):
    xf = x.astype(jnp.float32)
    y = xf * lax.rsqrt(jnp.mean(xf * xf, axis=-1, keepdims=True) + EPS)
    return (y * g.astype(jnp.float32)).astype(x.dtype)


def causal_dwconv(x, w):
    s = x.shape[1]
    xp = jnp.pad(x, ((0, 0), (CONV_K - 1, 0), (0, 0)))
    return sum(xp[:, j:j + s, :] * w[j] for j in range(CONV_K))


def fox_attention(q, k, v, log_f):
    b, s, h, dh = q.shape
    nblk = s // Q_BLOCK
    scale = 1.0 / np.sqrt(dh).astype(np.float32)
    F = jnp.transpose(jnp.cumsum(log_f, axis=1), (0, 2, 1))
    q_blocks = jnp.transpose(q.reshape(b, nblk, Q_BLOCK, h, dh), (1, 0, 2, 3, 4))
    f_blocks = jnp.transpose(F.reshape(b, h, nblk, Q_BLOCK), (2, 0, 1, 3))
    kpos = jnp.arange(s)

    def one_block(args):
        qb, fqb, i = args
        qpos = i * Q_BLOCK + jnp.arange(Q_BLOCK)
        sc = jnp.einsum('bqhd,bkhd->bhqk', qb, k,
                        preferred_element_type=jnp.float32) * scale
        logits = sc + fqb[..., None] - F[:, :, None, :]
        mask = kpos[None, :] <= qpos[:, None]
        logits = jnp.where(mask[None, None], logits, -jnp.inf)
        p = jax.nn.softmax(logits, axis=-1)
        return jnp.einsum('bhqk,bkhd->bqhd', p.astype(v.dtype), v)

    out = lax.map(one_block, (q_blocks, f_blocks, jnp.arange(nblk)))
    return jnp.transpose(out, (1, 0, 2, 3, 4)).reshape(b, s, h * dh)


def _fwd_setup_inputs(seed: int = 0) -> dict:
    key = jax.random.key(seed)
    ks = jax.random.split(key, 16)
    nrm = lambda k, shape, sc: jax.random.normal(k, shape, jnp.float32) * sc
    gain = lambda k, shape: 1.0 + 0.02 * jax.random.normal(k, shape, jnp.float32)
    return {
        "x": jax.random.normal(ks[0], (BATCH, SEQ, D_MODEL), jnp.float32),
        "g_mix": gain(ks[1], (DEPTH, D_MODEL)),
        "w_in": nrm(ks[2], (DEPTH, D_MODEL, IN_COLS), D_MODEL ** -0.5),
        "b_f": FORGET_BIAS_INIT + 0.1 * jax.random.normal(ks[3], (DEPTH, N_HEADS), jnp.float32),
        "w_conv": nrm(ks[4], (DEPTH, CONV_K, CONV_CH), CONV_K ** -0.5),
        "g_conv_out": gain(ks[5], (DEPTH, CONV_CH)),
        "g_attn_out": gain(ks[6], (DEPTH, ATTN_W)),
        "w_o": nrm(ks[7], (DEPTH, MIX_W, D_MODEL), MIX_W ** -0.5),
        "g_ffn": gain(ks[8], (DEPTH, D_MODEL)),
        "w_up": nrm(ks[9], (DEPTH, D_MODEL, 2 * D_FF), D_MODEL ** -0.5),
        "w_ffn_conv": nrm(ks[10], (DEPTH, CONV_K, 2 * D_FF), CONV_K ** -0.5),
        "w_down": nrm(ks[11], (DEPTH, D_FF, D_MODEL), D_FF ** -0.5),
        "g_final": gain(ks[12], (D_MODEL,)),
    }


def _fwd_reference(x, g_mix, w_in, b_f, w_conv, g_conv_out, g_attn_out, w_o,
              g_ffn, w_up, w_ffn_conv, w_down, g_final):
    b, s, _ = x.shape
    o1 = CONV_CH
    o2 = 2 * CONV_CH
    o3 = 3 * CONV_CH
    o4 = o3 + ATTN_W
    o5 = o4 + ATTN_W
    o6 = o5 + ATTN_W
    for l in range(DEPTH):
        h = rmsnorm(x, g_mix[l])
        z = h @ w_in[l]
        gb, gc, xc = z[..., :o1], z[..., o1:o2], z[..., o2:o3]
        q = z[..., o3:o4].reshape(b, s, N_HEADS, HEAD_DIM)
        k = z[..., o4:o5].reshape(b, s, N_HEADS, HEAD_DIM)
        v = z[..., o5:o6].reshape(b, s, N_HEADS, HEAD_DIM)
        f_logit = z[..., o6:]
        y_conv = gb * causal_dwconv(gc * xc, w_conv[l])
        log_f = jax.nn.log_sigmoid((f_logit + b_f[l]).astype(jnp.float32))
        y_attn = fox_attention(q, k, v, log_f)
        mix = jnp.concatenate([rmsnorm(y_conv, g_conv_out[l]),
                               rmsnorm(y_attn, g_attn_out[l])], axis=-1)
        x = x + mix @ w_o[l]
        h = rmsnorm(x, g_ffn[l])
        u = causal_dwconv(h @ w_up[l], w_ffn_conv[l])
        a, g = u[..., :D_FF], u[..., D_FF:]
        x = x + (jax.nn.silu(g) * a) @ w_down[l]
    return rmsnorm(x, g_final)


import jax as _jax
import jax.numpy as _jnp

TWIN_FORMAT = 'train_step'
FWD_PARAMS = ['x', 'g_mix', 'w_in', 'b_f', 'w_conv', 'g_conv_out', 'g_attn_out', 'w_o', 'g_ffn', 'w_up', 'w_ffn_conv', 'w_down', 'g_final']
TWIN_WEIGHTS = ['g_mix', 'w_in', 'b_f', 'w_conv', 'g_conv_out', 'g_attn_out', 'w_o', 'g_ffn', 'w_up', 'w_ffn_conv', 'w_down', 'g_final']
TWIN_DIFF_INPUT = 'x'
TWIN_INPUTS = ['x', 'g_mix', 'w_in', 'b_f', 'w_conv', 'g_conv_out', 'g_attn_out', 'w_o', 'g_ffn', 'w_up', 'w_ffn_conv', 'w_down', 'g_final', 'loss_target', 'm_g_mix', 'm_w_in', 'm_b_f', 'm_w_conv', 'm_g_conv_out', 'm_g_attn_out', 'm_w_o', 'm_g_ffn', 'm_w_up', 'm_w_ffn_conv', 'm_w_down', 'm_g_final', 'v_g_mix', 'v_w_in', 'v_b_f', 'v_w_conv', 'v_g_conv_out', 'v_g_attn_out', 'v_w_o', 'v_g_ffn', 'v_w_up', 'v_w_ffn_conv', 'v_w_down', 'v_g_final']
TWIN_OUTPUTS = ['loss', 'grad_x', 'grad_g_mix', 'grad_w_in', 'grad_b_f', 'grad_w_conv', 'grad_g_conv_out', 'grad_g_attn_out', 'grad_w_o', 'grad_g_ffn', 'grad_w_up', 'grad_w_ffn_conv', 'grad_w_down', 'grad_g_final', 'delta_g_mix', 'delta_w_in', 'delta_b_f', 'delta_w_conv', 'delta_g_conv_out', 'delta_g_attn_out', 'delta_w_o', 'delta_g_ffn', 'delta_w_up', 'delta_w_ffn_conv', 'delta_w_down', 'delta_g_final', 'new_m_g_mix', 'new_m_w_in', 'new_m_b_f', 'new_m_w_conv', 'new_m_g_conv_out', 'new_m_g_attn_out', 'new_m_w_o', 'new_m_g_ffn', 'new_m_w_up', 'new_m_w_ffn_conv', 'new_m_w_down', 'new_m_g_final', 'new_v_g_mix', 'new_v_w_in', 'new_v_b_f', 'new_v_w_conv', 'new_v_g_conv_out', 'new_v_g_attn_out', 'new_v_w_o', 'new_v_g_ffn', 'new_v_w_up', 'new_v_w_ffn_conv', 'new_v_w_down', 'new_v_g_final']
TWIN_LEAF_KINDS = {'loss': 'loss', 'grad_x': 'grad_x', 'grad_g_mix': 'grad_w', 'grad_w_in': 'grad_w', 'grad_b_f': 'grad_w', 'grad_w_conv': 'grad_w', 'grad_g_conv_out': 'grad_w', 'grad_g_attn_out': 'grad_w', 'grad_w_o': 'grad_w', 'grad_g_ffn': 'grad_w', 'grad_w_up': 'grad_w', 'grad_w_ffn_conv': 'grad_w', 'grad_w_down': 'grad_w', 'grad_g_final': 'grad_w', 'delta_g_mix': 'delta_w', 'delta_w_in': 'delta_w', 'delta_b_f': 'delta_w', 'delta_w_conv': 'delta_w', 'delta_g_conv_out': 'delta_w', 'delta_g_attn_out': 'delta_w', 'delta_w_o': 'delta_w', 'delta_g_ffn': 'delta_w', 'delta_w_up': 'delta_w', 'delta_w_ffn_conv': 'delta_w', 'delta_w_down': 'delta_w', 'delta_g_final': 'delta_w', 'new_m_g_mix': 'new_m', 'new_m_w_in': 'new_m', 'new_m_b_f': 'new_m', 'new_m_w_conv': 'new_m', 'new_m_g_conv_out': 'new_m', 'new_m_g_attn_out': 'new_m', 'new_m_w_o': 'new_m', 'new_m_g_ffn': 'new_m', 'new_m_w_up': 'new_m', 'new_m_w_ffn_conv': 'new_m', 'new_m_w_down': 'new_m', 'new_m_g_final': 'new_m', 'new_v_g_mix': 'new_v', 'new_v_w_in': 'new_v', 'new_v_b_f': 'new_v', 'new_v_w_conv': 'new_v', 'new_v_g_conv_out': 'new_v', 'new_v_g_attn_out': 'new_v', 'new_v_w_o': 'new_v', 'new_v_g_ffn': 'new_v', 'new_v_w_up': 'new_v', 'new_v_w_ffn_conv': 'new_v', 'new_v_w_down': 'new_v', 'new_v_g_final': 'new_v'}


def _forward(args):
    return _fwd_reference(*[args[k] for k in FWD_PARAMS])


def _output_shape():
    def fwd():
        inp = _fwd_setup_inputs(0)
        return _fwd_reference(*[inp[k] for k in FWD_PARAMS])
    out = _jax.eval_shape(fwd)
    return out.shape, out.dtype

N_MICROBATCH = 1
ADAM_LR = 0.001
ADAM_B1 = 0.9
ADAM_B2 = 0.999
ADAM_EPS = 1e-08
ADAM_WD = 0.01
ADAM_STEP = 10
PER_EXAMPLE_BATCH_AXIS = {'x': 0, 'loss_target': 0}
SHARED_INPUTS = []
_WEIGHT_DTYPES = {'g_mix': _jnp.float32, 'w_in': _jnp.float32, 'b_f': _jnp.float32, 'w_conv': _jnp.float32, 'g_conv_out': _jnp.float32, 'g_attn_out': _jnp.float32, 'w_o': _jnp.float32, 'g_ffn': _jnp.float32, 'w_up': _jnp.float32, 'w_ffn_conv': _jnp.float32, 'w_down': _jnp.float32, 'g_final': _jnp.float32}
MOMENT_SCALE = {'g_mix': 3.208921e-01, 'w_in': 1.848222e-01, 'b_f': 1.187530e+00, 'w_conv': 1.951571e-01, 'g_conv_out': 1.970812e-01, 'g_attn_out': 1.987705e-01, 'w_o': 1.925950e-01, 'g_ffn': 1.484169e-01, 'w_up': 6.106905e-02, 'w_ffn_conv': 6.132374e-02, 'w_down': 9.994006e-02, 'g_final': 6.404161e+01}


def _to_microbatches(a, axis):
    t = _jnp.moveaxis(a, axis, 0)
    t = t.reshape((N_MICROBATCH, t.shape[0] // N_MICROBATCH) + t.shape[1:])
    return _jnp.moveaxis(t, 1, axis + 1)


def setup_inputs(seed: int = 0) -> dict:
    inp = _fwd_setup_inputs(seed)
    key = _jax.random.fold_in(_jax.random.key(seed), 7919)
    shape, _ = _output_shape()
    out = dict(inp)
    out["loss_target"] = _jax.random.normal(_jax.random.fold_in(key, 0), shape, _jnp.float32)
    for i, name in enumerate(TWIN_WEIGHTS):
        w = inp[name].astype(_jnp.float32)
        if MOMENT_SCALE is None:
            s = _jnp.sqrt(_jnp.mean(_jnp.square(w)) + 1e-30)
        else:
            s = MOMENT_SCALE[name]
        km, kv = _jax.random.split(_jax.random.fold_in(key, i + 1))
        out[name] = w
        out["m_" + name] = s * _jax.random.normal(km, w.shape, _jnp.float32)
        out["v_" + name] = (s * s) * _jax.random.uniform(kv, w.shape, _jnp.float32, 0.5, 1.5)
    if N_MICROBATCH > 1:
        for name, axis in PER_EXAMPLE_BATCH_AXIS.items():
            out[name] = _to_microbatches(out[name], axis)
    return {'x': out['x'], 'g_mix': out['g_mix'], 'w_in': out['w_in'], 'b_f': out['b_f'], 'w_conv': out['w_conv'], 'g_conv_out': out['g_conv_out'], 'g_attn_out': out['g_attn_out'], 'w_o': out['w_o'], 'g_ffn': out['g_ffn'], 'w_up': out['w_up'], 'w_ffn_conv': out['w_ffn_conv'], 'w_down': out['w_down'], 'g_final': out['g_final'], 'loss_target': out['loss_target'], 'm_g_mix': out['m_g_mix'], 'm_w_in': out['m_w_in'], 'm_b_f': out['m_b_f'], 'm_w_conv': out['m_w_conv'], 'm_g_conv_out': out['m_g_conv_out'], 'm_g_attn_out': out['m_g_attn_out'], 'm_w_o': out['m_w_o'], 'm_g_ffn': out['m_g_ffn'], 'm_w_up': out['m_w_up'], 'm_w_ffn_conv': out['m_w_ffn_conv'], 'm_w_down': out['m_w_down'], 'm_g_final': out['m_g_final'], 'v_g_mix': out['v_g_mix'], 'v_w_in': out['v_w_in'], 'v_b_f': out['v_b_f'], 'v_w_conv': out['v_w_conv'], 'v_g_conv_out': out['v_g_conv_out'], 'v_g_attn_out': out['v_g_attn_out'], 'v_w_o': out['v_w_o'], 'v_g_ffn': out['v_g_ffn'], 'v_w_up': out['v_w_up'], 'v_w_ffn_conv': out['v_w_ffn_conv'], 'v_w_down': out['v_w_down'], 'v_g_final': out['v_g_final']}


def _loss(weights, diff, rest, loss_target):
    with _jax.named_scope("forward"):
        args = {**rest, TWIN_DIFF_INPUT: diff, **{k: w.astype(_WEIGHT_DTYPES[k]) for k, w in weights.items()}}
        y = _forward(args)
    with _jax.named_scope("loss_head"):
        err = _jnp.square(y.astype(_jnp.float32) - loss_target)
        return 0.5 * _jnp.sum(_jnp.mean(err, axis=-1)) if err.ndim else 0.5 * err


def _adamw(w, g, m, v):
    m = ADAM_B1 * m + (1.0 - ADAM_B1) * g
    v = ADAM_B2 * v + (1.0 - ADAM_B2) * _jnp.square(g)
    m_hat = m / (1.0 - ADAM_B1 ** ADAM_STEP)
    v_hat = v / (1.0 - ADAM_B2 ** ADAM_STEP)
    delta = -ADAM_LR * (m_hat / (_jnp.sqrt(v_hat) + ADAM_EPS) + ADAM_WD * w)
    return delta, m, v


def reference(x, g_mix, w_in, b_f, w_conv, g_conv_out, g_attn_out, w_o, g_ffn, w_up, w_ffn_conv, w_down, g_final, loss_target, m_g_mix, m_w_in, m_b_f, m_w_conv, m_g_conv_out, m_g_attn_out, m_w_o, m_g_ffn, m_w_up, m_w_ffn_conv, m_w_down, m_g_final, v_g_mix, v_w_in, v_b_f, v_w_conv, v_g_conv_out, v_g_attn_out, v_w_o, v_g_ffn, v_w_up, v_w_ffn_conv, v_w_down, v_g_final):
    given = dict(x=x, g_mix=g_mix, w_in=w_in, b_f=b_f, w_conv=w_conv, g_conv_out=g_conv_out, g_attn_out=g_attn_out, w_o=w_o, g_ffn=g_ffn, w_up=w_up, w_ffn_conv=w_ffn_conv, w_down=w_down, g_final=g_final, loss_target=loss_target, m_g_mix=m_g_mix, m_w_in=m_w_in, m_b_f=m_b_f, m_w_conv=m_w_conv, m_g_conv_out=m_g_conv_out, m_g_attn_out=m_g_attn_out, m_w_o=m_w_o, m_g_ffn=m_g_ffn, m_w_up=m_w_up, m_w_ffn_conv=m_w_ffn_conv, m_w_down=m_w_down, m_g_final=m_g_final, v_g_mix=v_g_mix, v_w_in=v_w_in, v_b_f=v_b_f, v_w_conv=v_w_conv, v_g_conv_out=v_g_conv_out, v_g_attn_out=v_g_attn_out, v_w_o=v_w_o, v_g_ffn=v_g_ffn, v_w_up=v_w_up, v_w_ffn_conv=v_w_ffn_conv, v_w_down=v_w_down, v_g_final=v_g_final)
    weights = {n: given[n] for n in TWIN_WEIGHTS}
    shared = {n: given[n] for n in SHARED_INPUTS}
    per_example = {n: given[n] for n in ['x']}
    grad_fn = _jax.value_and_grad(_loss, argnums=(0, 1))

    def one_microbatch(ex, loss_target):
        ex = dict(ex)
        diff = ex.pop(TWIN_DIFF_INPUT)
        return grad_fn(weights, diff, {**shared, **ex}, loss_target)

    if N_MICROBATCH == 1:
        loss, (grad_w, grad_x) = one_microbatch(per_example, given["loss_target"])
    else:
        def body(carry, xs):
            loss_sum, grad_sum = carry
            l_k, (gw_k, gx_k) = one_microbatch(xs[0], xs[1])
            with _jax.named_scope("update"):
                return (loss_sum + l_k, _jax.tree.map(_jnp.add, grad_sum, gw_k)), gx_k

        init = (_jnp.zeros((), _jnp.float32), _jax.tree.map(_jnp.zeros_like, weights))
        (loss, grad_w), grad_x = _jax.lax.scan(body, init, (per_example, given["loss_target"]))
    with _jax.named_scope("update"):
        delta_w, new_m, new_v = {}, {}, {}
        for n in TWIN_WEIGHTS:
            delta_w[n], new_m[n], new_v[n] = _adamw(weights[n], grad_w[n], given["m_" + n], given["v_" + n])
    return (loss, grad_x, *[grad_w[n] for n in TWIN_WEIGHTS], *[delta_w[n] for n in TWIN_WEIGHTS],
            *[new_m[n] for n in TWIN_WEIGHTS], *[new_v[n] for n in TWIN_WEIGHTS])
```

```python
import functools

import jax
import jax.numpy as jnp
from jax import lax
from jax.experimental import pallas as pl
from jax.experimental.pallas import tpu as pltpu

F32 = jnp.float32
BF = jnp.bfloat16
HI = lax.Precision.HIGHEST
MESH = pl.DeviceIdType.MESH

D_MODEL = 1024
CONV_CH = 512
ATTN_W = 512
N_HEADS = 8
HEAD_DIM = 64
D_FF = 2816
FF_CHUNK = 256
N_FF_CHUNKS = D_FF // FF_CHUNK
IN_COLS = 3080
EPS = 1e-6
NEG = -1e30
LANES = 128
VMEM_BIG = 56 * 1024 * 1024
VMEM_MID = 40 * 1024 * 1024

ADAM_LR = 0.001
ADAM_B1 = 0.9
ADAM_B2 = 0.999
ADAM_EPS = 1e-08
ADAM_WD = 0.01
ADAM_STEP = 10

NT = (((1,), (1,)), ((), ()))
TN = (((0,), (0,)), ((), ()))


def _dot(a, b):
    return jnp.dot(a, b, preferred_element_type=F32)


def _dot_nt(a, b):
    return lax.dot_general(a, b, NT, preferred_element_type=F32)


def _dot_tn(a, b):
    return lax.dot_general(a, b, TN, preferred_element_type=F32)


def _params(vmem=None, parallel=False):
    return pltpu.CompilerParams(
        dimension_semantics=None, vmem_limit_bytes=vmem)


def _rows(shape):
    return lax.broadcasted_iota(jnp.int32, shape, 0)


def _cols(shape):
    return lax.broadcasted_iota(jnp.int32, shape, 1)


def _shift_down(u, prev, k):
    n = prev.shape[0]
    out = pltpu.roll(u, k, 0)
    row = _rows(u.shape)
    for r in range(k):
        out = jnp.where(row == r, prev[n - k + r:n - k + r + 1, :].astype(u.dtype), out)
    return out


def _shift_up(u, nxt, k):
    tm = u.shape[0]
    out = pltpu.roll(u, tm - k, 0)
    row = _rows(u.shape)
    for r in range(k):
        out = jnp.where(row == tm - k + r, nxt[r:r + 1, :], out)
    return out


def _conv3(u, prev, w):
    u1 = _shift_down(u, prev, 1)
    u2 = _shift_down(u, prev, 2)
    return w[0:1, :] * u2 + w[1:2, :] * u1 + w[2:3, :] * u, u1, u2


def _conv3_bwd(d, nxt, w):
    return w[2:3, :] * d + w[1:2, :] * _shift_up(d, nxt, 1) + w[0:1, :] * _shift_up(d, nxt, 2)


def _rms(x):
    return lax.rsqrt(jnp.mean(x * x, axis=-1, keepdims=True) + EPS)


def _rms_bwd(x, r, dyg):
    return r * dyg - x * (r * r * r) * jnp.mean(dyg * x, axis=-1, keepdims=True)


def _full(shape):
    nd = len(shape)
    return pl.BlockSpec(shape, lambda i, _n=nd: (0,) * _n)


def _fwd_in(x, g_mix, w_zc, w_qkv, w_f, b_f, w_conv, g_conv_out, tm):
    s = x.shape[0]
    nb = s // tm

    def body(x_ref, gm_ref, wzc_ref, wqkv_ref, wf_ref, bf_ref, wc_ref, gco_ref,
             h1_ref, zc_ref, qkv_ref, fpre_ref, fcol_ref, frow_ref, nc_ref, cu_ref, cf_ref):
        i = pl.program_id(0)

        @pl.when(i == 0)
        def _():
            cu_ref[...] = jnp.zeros_like(cu_ref)
            cf_ref[...] = jnp.zeros_like(cf_ref)

        xv = x_ref[...]
        hb = (xv * _rms(xv) * gm_ref[...]).astype(BF)
        h1_ref[...] = hb
        zc = _dot(hb, wzc_ref[...])
        zc_ref[...] = zc
        qkv = _dot(hb, wqkv_ref[...])
        qkv = jnp.where(_cols(qkv.shape) < ATTN_W, qkv * 0.125, qkv)
        qkv_ref[...] = qkv.astype(BF)

        gb, gc, xc = zc[:, :CONV_CH], zc[:, CONV_CH:2 * CONV_CH], zc[:, 2 * CONV_CH:]
        u = gc * xc
        cv, _, _ = _conv3(u, cu_ref[...], wc_ref[...])
        cu_ref[...] = u[tm - 8:, :]
        y = gb * cv
        nc_ref[...] = (y * _rms(y) * gco_ref[...]).astype(BF)

        fpre = _dot(hb, wf_ref[...]) + bf_ref[...]
        fpre_ref[...] = fpre
        logf = jnp.minimum(fpre, 0.0) - jnp.log1p(jnp.exp(-jnp.abs(fpre)))
        logf = jnp.where(_cols(logf.shape) < N_HEADS, logf, 0.0)
        tri = (_rows((tm, tm)) >= _cols((tm, tm))).astype(F32)
        fcol = jnp.dot(tri, logf, precision=HI, preferred_element_type=F32) + cf_ref[...]
        cf_ref[...] = fcol[tm - 1:tm, :]
        fcol_ref[...] = fcol
        frow_ref[0] = fcol.T[:N_HEADS, :]

    blk = lambda c: pl.BlockSpec((tm, c), lambda i: (i, 0))
    return pl.pallas_call(
        body, name="fwd_in", grid=(nb,),
        in_specs=[blk(D_MODEL), _full((1, D_MODEL)), _full(w_zc.shape), _full(w_qkv.shape), _full(w_f.shape),
                  _full((1, LANES)), _full((8, CONV_CH)), _full((1, CONV_CH))],
        out_specs=[blk(D_MODEL), blk(3 * CONV_CH), blk(3 * ATTN_W), blk(LANES), blk(LANES),
                   pl.BlockSpec((1, N_HEADS, tm), lambda i: (i, 0, 0)), blk(CONV_CH)],
        out_shape=[jax.ShapeDtypeStruct((s, D_MODEL), BF), jax.ShapeDtypeStruct((s, 3 * CONV_CH), F32),
                   jax.ShapeDtypeStruct((s, 3 * ATTN_W), BF), jax.ShapeDtypeStruct((s, LANES), F32),
                   jax.ShapeDtypeStruct((s, LANES), F32), jax.ShapeDtypeStruct((nb, N_HEADS, tm), F32),
                   jax.ShapeDtypeStruct((s, CONV_CH), BF)],
        scratch_shapes=[pltpu.VMEM((8, CONV_CH), F32), pltpu.VMEM((1, LANES), F32)],
        compiler_params=_params(VMEM_MID),
    )(x, g_mix, w_zc, w_qkv, w_f, b_f, w_conv, g_conv_out)


def _head_mask(shape, hh):
    lane = _cols(shape)
    return (lane >= HEAD_DIM * hh) & (lane < HEAD_DIM * (hh + 1))


def _attn_fwd(qkv, fcol, frow, t):
    s = qkv.shape[0]
    nb = s // t

    def body(q_ref, fcol_ref, frow_ref, qkv_any, o_ref, l_ref, k_scr, v_scr, m_scr, s_scr, acc_scr, sem):
        i = pl.program_id(0)

        @pl.when(i == 0)
        def _():
            ck = pltpu.make_async_copy(qkv_any.at[:, pl.ds(ATTN_W, ATTN_W)], k_scr, sem.at[0])
            cv = pltpu.make_async_copy(qkv_any.at[:, pl.ds(2 * ATTN_W, ATTN_W)], v_scr, sem.at[1])
            ck.start()
            cv.start()
            ck.wait()
            cv.wait()

        causal = _cols((t, t)) <= _rows((t, t))
        lane = _cols((t, LANES))
        lcol = jnp.zeros((t, LANES), F32)
        for hp in range(N_HEADS // 2):
            cs = slice(hp * LANES, (hp + 1) * LANES)
            q2 = q_ref[:, cs]
            o_pair = None
            for hh in range(2):
                h = 2 * hp + hh
                qh = jnp.where(_head_mask(q2.shape, hh), q2, jnp.zeros_like(q2))
                fq = fcol_ref[:, h:h + 1]
                m_scr[...] = jnp.full(m_scr.shape, NEG, F32)
                s_scr[...] = jnp.zeros(s_scr.shape, F32)
                acc_scr[...] = jnp.zeros(acc_scr.shape, F32)

                def tile(kb, masked, cs=cs, qh=qh, fq=fq, h=h):
                    rs = pl.ds(pl.multiple_of(kb * t, t), t)
                    k2 = k_scr[rs, cs]
                    v2 = v_scr[rs, cs]
                    e = _dot_nt(qh, k2) + (fq - frow_ref[kb, h:h + 1, :])
                    if masked:
                        e = jnp.where(causal, e, NEG)
                    m_old = m_scr[...]
                    m_new = jnp.maximum(m_old, jnp.max(e, axis=-1, keepdims=True))
                    p = jnp.exp(e - m_new)
                    alpha = jnp.exp(m_old - m_new)
                    s_scr[...] = alpha * s_scr[...] + jnp.sum(p, axis=-1, keepdims=True)
                    acc_scr[...] = alpha * acc_scr[...] + _dot(p.astype(BF), v2)
                    m_scr[...] = m_new

                def loop_body(kb, carry, tile=tile):
                    tile(kb, False)
                    return carry

                lax.fori_loop(0, i, loop_body, 0)
                tile(i, True)
                denom = s_scr[...]
                o_h = acc_scr[...] / denom
                lcol = jnp.where(lane == h, m_scr[...] + jnp.log(denom), lcol)
                o_pair = o_h if hh == 0 else jnp.where(lane < HEAD_DIM, o_pair, o_h)
            o_ref[:, cs] = o_pair
        l_ref[...] = lcol

    return pl.pallas_call(
        body, name="attn_fwd", grid=(nb,),
        in_specs=[pl.BlockSpec((t, ATTN_W), lambda i: (i, 0)), pl.BlockSpec((t, LANES), lambda i: (i, 0)),
                  _full(frow.shape), pl.BlockSpec(memory_space=pl.ANY)],
        out_specs=[pl.BlockSpec((t, ATTN_W), lambda i: (i, 0)), pl.BlockSpec((t, LANES), lambda i: (i, 0))],
        out_shape=[jax.ShapeDtypeStruct((s, ATTN_W), F32), jax.ShapeDtypeStruct((s, LANES), F32)],
        scratch_shapes=[pltpu.VMEM((s, ATTN_W), BF), pltpu.VMEM((s, ATTN_W), BF), pltpu.VMEM((t, 1), F32),
                        pltpu.VMEM((t, 1), F32), pltpu.VMEM((t, LANES), F32), pltpu.SemaphoreType.DMA((2,))],
        compiler_params=_params(VMEM_MID),
    )(qkv, fcol, frow, qkv)


def _fwd_out(x, nc, o, g_attn_out, w_o, g_ffn, tm):
    s = x.shape[0]

    def body(x_ref, nc_ref, o_ref, ga_ref, wo_ref, gf_ref, x2_ref, h2_ref, mix_ref):
        ov = o_ref[...]
        na = (ov * _rms(ov) * ga_ref[...]).astype(BF)
        ncv = nc_ref[...]
        mix_ref[:, :CONV_CH] = ncv
        mix_ref[:, CONV_CH:] = na
        x2 = x_ref[...] + _dot(ncv, wo_ref[:CONV_CH, :]) + _dot(na, wo_ref[CONV_CH:, :])
        x2_ref[...] = x2
        h2_ref[...] = (x2 * _rms(x2) * gf_ref[...]).astype(BF)

    blk = lambda c: pl.BlockSpec((tm, c), lambda i: (i, 0))
    return pl.pallas_call(
        body, name="fwd_out", grid=(s // tm,),
        in_specs=[blk(D_MODEL), blk(CONV_CH), blk(ATTN_W), _full((1, ATTN_W)), _full(w_o.shape), _full((1, D_MODEL))],
        out_specs=[blk(D_MODEL), blk(D_MODEL), blk(D_MODEL)],
        out_shape=[jax.ShapeDtypeStruct((s, D_MODEL), F32), jax.ShapeDtypeStruct((s, D_MODEL), BF),
                   jax.ShapeDtypeStruct((s, D_MODEL), BF)],
        compiler_params=_params(VMEM_MID),
    )(x, nc, o, g_attn_out, w_o, g_ffn)


def _load_weights(i, pairs, sem):
    @pl.when(i == 0)
    def _():
        cps = [pltpu.make_async_copy(src, dst, sem.at[n]) for n, (src, dst) in enumerate(pairs)]
        for cp in cps:
            cp.start()
        for cp in cps:
            cp.wait()


def _fwd_ffn(h2, x2, tgt, w_up, w_ffn_conv, w_dn, g_final, tm):
    s = x2.shape[0]

    def body(h2_ref, x2_ref, tgt_ref, wfc_ref, gfin_ref, wup_any, wdn_any,
             u0_ref, act_ref, dx3_ref, dx3b_ref, loss_ref, dgfin_ref,
             wup, wdn, carry, acc, sem):
        i = pl.program_id(0)
        _load_weights(i, [(wup_any, wup), (wdn_any, wdn)], sem)

        @pl.when(i == 0)
        def _():
            carry[...] = jnp.zeros_like(carry)
            loss_ref[...] = jnp.zeros_like(loss_ref)
            dgfin_ref[...] = jnp.zeros_like(dgfin_ref)

        hb = h2_ref[...]
        acc[...] = jnp.zeros_like(acc)
        for j in range(N_FF_CHUNKS):
            ca = slice(j * FF_CHUNK, (j + 1) * FF_CHUNK)
            cg = slice(D_FF + j * FF_CHUNK, D_FF + (j + 1) * FF_CHUNK)
            parts = []
            for cc in (ca, cg):
                u0 = _dot(hb, wup[:, cc])
                u0_ref[:, cc] = u0.astype(BF)
                uu, _, _ = _conv3(u0, carry[:, cc], wfc_ref[:, cc])
                carry[:, cc] = u0[tm - 8:, :]
                parts.append(uu)
            ua, ug = parts
            act = (ug * jax.nn.sigmoid(ug) * ua).astype(BF)
            act_ref[:, ca] = act
            acc[...] += _dot(act, wdn[ca, :])

        x3 = x2_ref[...] + acc[...]
        r3 = _rms(x3)
        gfin = gfin_ref[...]
        xn = x3 * r3
        diff = xn * gfin - tgt_ref[...]
        loss_ref[...] += jnp.sum(jnp.sum(diff * diff, axis=-1, keepdims=True), axis=0, keepdims=True) * (0.5 / D_MODEL)
        dy = diff * (1.0 / D_MODEL)
        dgfin_ref[...] += jnp.sum(dy * xn, axis=0, keepdims=True)
        dx3 = _rms_bwd(x3, r3, dy * gfin)
        dx3_ref[...] = dx3
        dx3b_ref[...] = dx3.astype(BF)

    blk = lambda c: pl.BlockSpec((tm, c), lambda i: (i, 0))
    any_ = pl.BlockSpec(memory_space=pl.ANY)
    return pl.pallas_call(
        body, name="fwd_ffn", grid=(s // tm,),
        in_specs=[blk(D_MODEL), blk(D_MODEL), blk(D_MODEL), _full((8, 2 * D_FF)), _full((1, D_MODEL)), any_, any_],
        out_specs=[blk(2 * D_FF), blk(D_FF), blk(D_MODEL), blk(D_MODEL), _full((1, 1)), _full((1, D_MODEL))],
        out_shape=[jax.ShapeDtypeStruct((s, 2 * D_FF), BF), jax.ShapeDtypeStruct((s, D_FF), BF),
                   jax.ShapeDtypeStruct((s, D_MODEL), F32), jax.ShapeDtypeStruct((s, D_MODEL), BF),
                   jax.ShapeDtypeStruct((1, 1), F32), jax.ShapeDtypeStruct((1, D_MODEL), F32)],
        scratch_shapes=[pltpu.VMEM(w_up.shape, BF), pltpu.VMEM(w_dn.shape, BF), pltpu.VMEM((8, 2 * D_FF), F32),
                        pltpu.VMEM((tm, D_MODEL), F32), pltpu.SemaphoreType.DMA((2,))],
        compiler_params=_params(VMEM_BIG),
    )(h2, x2, tgt, w_ffn_conv, g_final, w_up, w_dn)


def _bwd_ffn(dx3, dx3b, u0, x2, w_up, w_ffn_conv, w_dn, g_ffn, tm):
    s = x2.shape[0]
    nb = s // tm
    hr = 16

    def body(dx3_ref, dx3b_ref, u0_ref, halo_ref, x2_ref, wfc_ref, gf_ref, wup_any, wdn_any,
             du0_ref, dx2_ref, dx2b_ref, dwfc_ref, dgf_ref,
             wup, wdn, carry, acc, sem):
        i = pl.program_id(0)
        rb = nb - 1 - i
        _load_weights(i, [(wup_any, wup), (wdn_any, wdn)], sem)

        @pl.when(i == 0)
        def _():
            carry[...] = jnp.zeros_like(carry)
            dwfc_ref[...] = jnp.zeros_like(dwfc_ref)
            dgf_ref[...] = jnp.zeros_like(dgf_ref)

        db = dx3b_ref[...]
        live = (rb > 0).astype(F32)
        acc[...] = jnp.zeros_like(acc)
        for j in range(N_FF_CHUNKS):
            ca = slice(j * FF_CHUNK, (j + 1) * FF_CHUNK)
            cg = slice(D_FF + j * FF_CHUNK, D_FF + (j + 1) * FF_CHUNK)
            fw = []
            for cc in (ca, cg):
                u0 = u0_ref[:, cc].astype(F32)
                prev = halo_ref[:, cc].astype(F32) * live
                fw.append((u0,) + _conv3(u0, prev, wfc_ref[:, cc]))
            (u0a, ua, u1a, u2a), (u0g, ug, u1g, u2g) = fw
            sg = jax.nn.sigmoid(ug)
            dact = _dot_nt(db, wdn[ca, :])
            da = dact * (ug * sg)
            dg = dact * ua * (sg * (1.0 + ug * (1.0 - sg)))
            for cc, d, u0c, u1c, u2c in ((ca, da, u0a, u1a, u2a), (cg, dg, u0g, u1g, u2g)):
                dwfc_ref[0:1, cc] += jnp.sum(d * u2c, axis=0, keepdims=True)
                dwfc_ref[1:2, cc] += jnp.sum(d * u1c, axis=0, keepdims=True)
                dwfc_ref[2:3, cc] += jnp.sum(d * u0c, axis=0, keepdims=True)
                du0 = _conv3_bwd(d, carry[:, cc], wfc_ref[:, cc]).astype(BF)
                carry[:, cc] = d[:8, :]
                du0_ref[:, cc] = du0
                acc[...] += _dot_nt(du0, wup[:, cc])

        x2v = x2_ref[...]
        r2 = _rms(x2v)
        dh2 = acc[...]
        dgf_ref[...] += jnp.sum(dh2 * (x2v * r2), axis=0, keepdims=True)
        dx2 = dx3_ref[...] + _rms_bwd(x2v, r2, dh2 * gf_ref[...])
        dx2_ref[...] = dx2
        dx2b_ref[...] = dx2.astype(BF)

    blk = lambda c: pl.BlockSpec((tm, c), lambda i: (nb - 1 - i, 0))
    halo = pl.BlockSpec((hr, 2 * D_FF), lambda i: (jnp.maximum((nb - 1 - i) * (tm // hr) - 1, 0), 0))
    any_ = pl.BlockSpec(memory_space=pl.ANY)
    return pl.pallas_call(
        body, name="bwd_ffn", grid=(nb,),
        in_specs=[blk(D_MODEL), blk(D_MODEL), blk(2 * D_FF), halo, blk(D_MODEL), _full((8, 2 * D_FF)),
                  _full((1, D_MODEL)), any_, any_],
        out_specs=[blk(2 * D_FF), blk(D_MODEL), blk(D_MODEL), _full((8, 2 * D_FF)), _full((1, D_MODEL))],
        out_shape=[jax.ShapeDtypeStruct((s, 2 * D_FF), BF), jax.ShapeDtypeStruct((s, D_MODEL), F32),
                   jax.ShapeDtypeStruct((s, D_MODEL), BF), jax.ShapeDtypeStruct((8, 2 * D_FF), F32),
                   jax.ShapeDtypeStruct((1, D_MODEL), F32)],
        scratch_shapes=[pltpu.VMEM(w_up.shape, BF), pltpu.VMEM(w_dn.shape, BF), pltpu.VMEM((8, 2 * D_FF), F32),
                        pltpu.VMEM((tm, D_MODEL), F32), pltpu.SemaphoreType.DMA((2,))],
        compiler_params=_params(VMEM_BIG),
    )(dx3, dx3b, u0, u0, x2, w_ffn_conv, g_ffn, w_up, w_dn)


def _bwd_out(dx2b, o, zc, fcol, lcol, w_o, g_attn_out, g_conv_out, w_conv, tm):
    s = o.shape[0]
    nb = s // tm

    def body(dx2b_ref, o_ref, zc_ref, halo_ref, fcol_ref, lcol_ref, wo_ref, ga_ref, gco_ref, wc_ref,
             dzc_ref, dob_ref, dcol_ref, ccol_ref, drow_ref, crow_ref, dwc_ref, dga_ref, dgco_ref, carry):
        i = pl.program_id(0)
        rb = nb - 1 - i

        @pl.when(i == 0)
        def _():
            carry[...] = jnp.zeros_like(carry)
            dwc_ref[...] = jnp.zeros_like(dwc_ref)
            dga_ref[...] = jnp.zeros_like(dga_ref)
            dgco_ref[...] = jnp.zeros_like(dgco_ref)

        dmix = _dot_nt(dx2b_ref[...], wo_ref[...])
        dnc, dna = dmix[:, :CONV_CH], dmix[:, CONV_CH:]

        ov = o_ref[...]
        ra = _rms(ov)
        dga_ref[...] += jnp.sum(dna * (ov * ra), axis=0, keepdims=True)
        do = _rms_bwd(ov, ra, dna * ga_ref[...])
        dob_ref[...] = do.astype(BF)
        sel = (_rows((ATTN_W, LANES)) // HEAD_DIM == _cols((ATTN_W, LANES))).astype(F32)
        delta = jnp.dot(do * ov, sel, precision=HI, preferred_element_type=F32)
        cc = fcol_ref[...] - lcol_ref[...]
        dcol_ref[...] = delta
        ccol_ref[...] = cc
        drow_ref[0] = delta.T[:N_HEADS, :]
        crow_ref[0] = cc.T[:N_HEADS, :]

        zc_v = zc_ref[...]
        gb, gc, xc = zc_v[:, :CONV_CH], zc_v[:, CONV_CH:2 * CONV_CH], zc_v[:, 2 * CONV_CH:]
        hal = halo_ref[...] * (rb > 0).astype(F32)
        u = gc * xc
        prev = hal[:, CONV_CH:2 * CONV_CH] * hal[:, 2 * CONV_CH:]
        wc = wc_ref[...]
        cv, u1, u2 = _conv3(u, prev, wc)
        y = gb * cv
        rc = _rms(y)
        dgco_ref[...] += jnp.sum(dnc * (y * rc), axis=0, keepdims=True)
        dy = _rms_bwd(y, rc, dnc * gco_ref[...])
        dcv = dy * gb
        dwc_ref[0:1, :] += jnp.sum(dcv * u2, axis=0, keepdims=True)
        dwc_ref[1:2, :] += jnp.sum(dcv * u1, axis=0, keepdims=True)
        dwc_ref[2:3, :] += jnp.sum(dcv * u, axis=0, keepdims=True)
        du = _conv3_bwd(dcv, carry[...], wc)
        carry[...] = dcv[:8, :]
        dzc_ref[:, :CONV_CH] = (dy * cv).astype(BF)
        dzc_ref[:, CONV_CH:2 * CONV_CH] = (du * xc).astype(BF)
        dzc_ref[:, 2 * CONV_CH:] = (du * gc).astype(BF)

    blk = lambda c: pl.BlockSpec((tm, c), lambda i: (nb - 1 - i, 0))
    halo = pl.BlockSpec((8, 3 * CONV_CH), lambda i: (jnp.maximum((nb - 1 - i) * (tm // 8) - 1, 0), 0))
    row = pl.BlockSpec((1, N_HEADS, tm), lambda i: (nb - 1 - i, 0, 0))
    return pl.pallas_call(
        body, name="bwd_out", grid=(nb,),
        in_specs=[blk(D_MODEL), blk(ATTN_W), blk(3 * CONV_CH), halo, blk(LANES), blk(LANES), _full(w_o.shape),
                  _full((1, ATTN_W)), _full((1, CONV_CH)), _full((8, CONV_CH))],
        out_specs=[blk(3 * CONV_CH), blk(ATTN_W), blk(LANES), blk(LANES), row, row,
                   _full((8, CONV_CH)), _full((1, ATTN_W)), _full((1, CONV_CH))],
        out_shape=[jax.ShapeDtypeStruct((s, 3 * CONV_CH), BF), jax.ShapeDtypeStruct((s, ATTN_W), BF),
                   jax.ShapeDtypeStruct((s, LANES), F32), jax.ShapeDtypeStruct((s, LANES), F32),
                   jax.ShapeDtypeStruct((nb, N_HEADS, tm), F32), jax.ShapeDtypeStruct((nb, N_HEADS, tm), F32),
                   jax.ShapeDtypeStruct((8, CONV_CH), F32), jax.ShapeDtypeStruct((1, ATTN_W), F32),
                   jax.ShapeDtypeStruct((1, CONV_CH), F32)],
        scratch_shapes=[pltpu.VMEM((8, CONV_CH), F32)],
        compiler_params=_params(VMEM_MID),
    )(dx2b, o, zc, zc, fcol, lcol, w_o, g_attn_out, g_conv_out, w_conv)


def _attn_bwd_dq(qkv, dob, ccol, dcol, frow, t):
    s = qkv.shape[0]
    nb = s // t

    def body(q_ref, do_ref, ccol_ref, dcol_ref, frow_ref, qkv_any, dq_ref, dfq_ref, k_scr, v_scr, acc_scr, sem):
        i = pl.program_id(0)

        @pl.when(i == 0)
        def _():
            ck = pltpu.make_async_copy(qkv_any.at[:, pl.ds(ATTN_W, ATTN_W)], k_scr, sem.at[0])
            cv = pltpu.make_async_copy(qkv_any.at[:, pl.ds(2 * ATTN_W, ATTN_W)], v_scr, sem.at[1])
            ck.start()
            cv.start()
            ck.wait()
            cv.wait()

        causal = _cols((t, t)) <= _rows((t, t))
        lane = _cols((t, LANES))
        dfq = jnp.zeros((t, LANES), F32)
        for hp in range(N_HEADS // 2):
            cs = slice(hp * LANES, (hp + 1) * LANES)
            q2 = q_ref[:, cs]
            do2 = do_ref[:, cs]
            dq_pair = None
            for hh in range(2):
                h = 2 * hp + hh
                hm = _head_mask(q2.shape, hh)
                qh = jnp.where(hm, q2, jnp.zeros_like(q2))
                doh = jnp.where(hm, do2, jnp.zeros_like(do2))
                cq = ccol_ref[:, h:h + 1]
                dl = dcol_ref[:, h:h + 1]
                aux = HEAD_DIM * (1 - hh)
                acc_scr[...] = jnp.zeros(acc_scr.shape, F32)

                def tile(kb, masked, cs=cs, qh=qh, doh=doh, cq=cq, dl=dl, h=h, aux=aux):
                    rs = pl.ds(pl.multiple_of(kb * t, t), t)
                    k2 = k_scr[rs, cs]
                    v2 = v_scr[rs, cs]
                    e = _dot_nt(qh, k2) + (cq - frow_ref[kb, h:h + 1, :])
                    if masked:
                        e = jnp.where(causal, e, NEG)
                    p = jnp.exp(e)
                    ds = p * (_dot_nt(doh, v2) - dl)
                    acc_scr[...] += _dot(ds.astype(BF), jnp.where(lane == aux, jnp.ones_like(k2), k2))

                def loop_body(kb, carry, tile=tile):
                    tile(kb, False)
                    return carry

                lax.fori_loop(0, i, loop_body, 0)
                tile(i, True)
                aq = acc_scr[...]
                dfq = jnp.where(lane == h, aq[:, aux:aux + 1], dfq)
                dq_h = aq * 0.125
                dq_pair = dq_h if hh == 0 else jnp.where(lane < HEAD_DIM, dq_pair, dq_h)
            dq_ref[:, cs] = dq_pair.astype(BF)
        dfq_ref[...] = dfq

    return pl.pallas_call(
        body, name="attn_bwd_dq", grid=(nb,),
        in_specs=[pl.BlockSpec((t, ATTN_W), lambda i: (i, 0)), pl.BlockSpec((t, ATTN_W), lambda i: (i, 0)),
                  pl.BlockSpec((t, LANES), lambda i: (i, 0)), pl.BlockSpec((t, LANES), lambda i: (i, 0)),
                  _full(frow.shape), pl.BlockSpec(memory_space=pl.ANY)],
        out_specs=[pl.BlockSpec((t, ATTN_W), lambda i: (i, 0)), pl.BlockSpec((t, LANES), lambda i: (i, 0))],
        out_shape=[jax.ShapeDtypeStruct((s, ATTN_W), BF), jax.ShapeDtypeStruct((s, LANES), F32)],
        scratch_shapes=[pltpu.VMEM((s, ATTN_W), BF), pltpu.VMEM((s, ATTN_W), BF), pltpu.VMEM((t, LANES), F32),
                        pltpu.SemaphoreType.DMA((2,))],
        compiler_params=_params(VMEM_MID),
    )(qkv, dob, ccol, dcol, frow, qkv)


def _attn_bwd_dkv(qkv, dob, crow, drow, fcol, t):
    s = qkv.shape[0]
    nb = s // t

    def body(k_ref, v_ref, fcol_ref, crow_ref, drow_ref, qkv_any, do_any,
             dk_ref, dv_ref, dfk_ref, q_scr, do_scr, acck, accv, sem):
        j = pl.program_id(0)

        @pl.when(j == 0)
        def _():
            cq = pltpu.make_async_copy(qkv_any.at[:, pl.ds(0, ATTN_W)], q_scr, sem.at[0])
            cd = pltpu.make_async_copy(do_any, do_scr, sem.at[1])
            cq.start()
            cd.start()
            cq.wait()
            cd.wait()

        causal_t = _rows((t, t)) <= _cols((t, t))
        lane = _cols((t, LANES))
        dfk = jnp.zeros((t, LANES), F32)
        for hp in range(N_HEADS // 2):
            cs = slice(hp * LANES, (hp + 1) * LANES)
            k2 = k_ref[:, cs]
            v2 = v_ref[:, cs]
            dk_pair = dv_pair = None
            for hh in range(2):
                h = 2 * hp + hh
                hm = _head_mask(k2.shape, hh)
                kh = jnp.where(hm, k2, jnp.zeros_like(k2))
                vh = jnp.where(hm, v2, jnp.zeros_like(v2))
                fk = fcol_ref[:, h:h + 1]
                aux = HEAD_DIM * (1 - hh)
                acck[...] = jnp.zeros(acck.shape, F32)
                accv[...] = jnp.zeros(accv.shape, F32)

                def tile(qb, masked, cs=cs, kh=kh, vh=vh, fk=fk, hm=hm, aux=aux, h=h):
                    rs = pl.ds(pl.multiple_of(qb * t, t), t)
                    q2 = q_scr[rs, cs]
                    do2 = do_scr[rs, cs]
                    e = _dot_nt(kh, q2) + (crow_ref[qb, h:h + 1, :] - fk)
                    if masked:
                        e = jnp.where(causal_t, e, NEG)
                    p = jnp.exp(e)
                    accv[...] += _dot(p.astype(BF), do2)
                    ds = p * (_dot_nt(vh, do2) - drow_ref[qb, h:h + 1, :])
                    qa = jnp.where(lane == aux, jnp.ones_like(q2), jnp.where(hm, q2, jnp.zeros_like(q2)))
                    acck[...] += _dot(ds.astype(BF), qa)

                def loop_body(qb, carry, tile=tile):
                    tile(qb, False)
                    return carry

                tile(j, True)
                lax.fori_loop(j + 1, nb, loop_body, 0)
                ak = acck[...]
                av = accv[...]
                dfk = jnp.where(lane == h, -ak[:, aux:aux + 1], dfk)
                dk_pair = ak if hh == 0 else jnp.where(lane < HEAD_DIM, dk_pair, ak)
                dv_pair = av if hh == 0 else jnp.where(lane < HEAD_DIM, dv_pair, av)
            dk_ref[:, cs] = dk_pair.astype(BF)
            dv_ref[:, cs] = dv_pair.astype(BF)
        dfk_ref[...] = dfk

    any_ = pl.BlockSpec(memory_space=pl.ANY)
    return pl.pallas_call(
        body, name="attn_bwd_dkv", grid=(nb,),
        in_specs=[pl.BlockSpec((t, ATTN_W), lambda j: (j, 1)), pl.BlockSpec((t, ATTN_W), lambda j: (j, 2)),
                  pl.BlockSpec((t, LANES), lambda j: (j, 0)), _full(crow.shape), _full(drow.shape), any_, any_],
        out_specs=[pl.BlockSpec((t, ATTN_W), lambda j: (j, 0)), pl.BlockSpec((t, ATTN_W), lambda j: (j, 0)),
                   pl.BlockSpec((t, LANES), lambda j: (j, 0))],
        out_shape=[jax.ShapeDtypeStruct((s, ATTN_W), BF), jax.ShapeDtypeStruct((s, ATTN_W), BF),
                   jax.ShapeDtypeStruct((s, LANES), F32)],
        scratch_shapes=[pltpu.VMEM((s, ATTN_W), BF), pltpu.VMEM((s, ATTN_W), BF), pltpu.VMEM((t, LANES), F32),
                        pltpu.VMEM((t, LANES), F32), pltpu.SemaphoreType.DMA((2,))],
        compiler_params=_params(VMEM_MID),
    )(qkv, qkv, fcol, crow, drow, qkv, dob)


def _bwd_in(x, dx2, dzc, dq, dk, dv, dfq, dfk, fpre, w_zc, w_qkv, w_f, g_mix, tm):
    s = x.shape[0]
    nb = s // tm

    def body(x_ref, dx2_ref, dzc_ref, dq_ref, dk_ref, dv_ref, dfq_ref, dfk_ref, fpre_ref, wzc_ref, wqkv_ref, wf_ref,
             gm_ref, gx_ref, dfb_ref, dgm_ref, dbf_ref, carry):
        i = pl.program_id(0)

        @pl.when(i == 0)
        def _():
            carry[...] = jnp.zeros_like(carry)
            dgm_ref[...] = jnp.zeros_like(dgm_ref)
            dbf_ref[...] = jnp.zeros_like(dbf_ref)

        triu = (_rows((tm, tm)) <= _cols((tm, tm))).astype(F32)
        df_cum = dfq_ref[...] + dfk_ref[...]
        dlogf = jnp.dot(triu, df_cum, precision=HI, preferred_element_type=F32) + carry[...]
        carry[...] = dlogf[0:1, :]
        fpre = fpre_ref[...]
        df = jnp.where(_cols(fpre.shape) < N_HEADS, dlogf / (1.0 + jnp.exp(fpre)), 0.0)
        dbf_ref[...] += jnp.sum(df, axis=0, keepdims=True)
        dfb = df.astype(BF)
        dfb_ref[...] = dfb

        dh1 = _dot_nt(dzc_ref[...], wzc_ref[...])
        dh1 += _dot_nt(dq_ref[...], wqkv_ref[:, :ATTN_W])
        dh1 += _dot_nt(dk_ref[...], wqkv_ref[:, ATTN_W:2 * ATTN_W])
        dh1 += _dot_nt(dv_ref[...], wqkv_ref[:, 2 * ATTN_W:])
        dh1 += _dot_nt(dfb, wf_ref[...])
        xv = x_ref[...]
        r1 = _rms(xv)
        dgm_ref[...] += jnp.sum(dh1 * (xv * r1), axis=0, keepdims=True)
        gx_ref[...] = dx2_ref[...] + _rms_bwd(xv, r1, dh1 * gm_ref[...])

    blk = lambda c: pl.BlockSpec((tm, c), lambda i: (nb - 1 - i, 0))
    return pl.pallas_call(
        body, name="bwd_in", grid=(nb,),
        in_specs=[blk(D_MODEL), blk(D_MODEL), blk(3 * CONV_CH), blk(ATTN_W), blk(ATTN_W), blk(ATTN_W), blk(LANES),
                  blk(LANES), blk(LANES), _full(w_zc.shape), _full(w_qkv.shape), _full(w_f.shape),
                  _full((1, D_MODEL))],
        out_specs=[blk(D_MODEL), blk(LANES), _full((1, D_MODEL)), _full((1, LANES))],
        out_shape=[jax.ShapeDtypeStruct((s, D_MODEL), F32), jax.ShapeDtypeStruct((s, LANES), BF),
                   jax.ShapeDtypeStruct((1, D_MODEL), F32), jax.ShapeDtypeStruct((1, LANES), F32)],
        scratch_shapes=[pltpu.VMEM((1, LANES), F32)],
        compiler_params=_params(VMEM_MID),
    )(x, dx2, dzc, dq, dk, dv, dfq, dfk, fpre, w_zc, w_qkv, w_f, g_mix)


def _matmul_tn(a, b, bn, bk, name):
    s, m = a.shape
    n = b.shape[1]

    def body(a_ref, b_ref, o_ref):
        @pl.when(pl.program_id(1) == 0)
        def _():
            o_ref[...] = jnp.zeros_like(o_ref)

        o_ref[...] += _dot_tn(a_ref[...], b_ref[...])

    return pl.pallas_call(
        body, name=name, grid=(n // bn, s // bk),
        in_specs=[pl.BlockSpec((bk, m), lambda jn, k: (k, 0)), pl.BlockSpec((bk, bn), lambda jn, k: (k, jn))],
        out_specs=pl.BlockSpec((m, bn), lambda jn, k: (0, jn)),
        out_shape=jax.ShapeDtypeStruct((m, n), F32),
        compiler_params=_params(VMEM_MID),
    )(a, b)


def _flip(v, bit):
    return 1 - v if bit else v


def _all_gather(shards):
    n = len(shards)

    def body(*refs):
        src, out = refs[:n], refs[n:2 * n]
        send_sems, recv_sems, loc_sems = refs[2 * n:]
        x, y, c = lax.axis_index("x"), lax.axis_index("y"), lax.axis_index("c")
        me, sibling = (x, y, c), (x, y, 1 - c)
        chips = [(1 - x, y), (x, 1 - y), (1 - x, 1 - y)]

        def slot(a, px, py, pc):
            return out[a].at[4 * px + 2 * py + pc]

        def copy(a, k, block, to, from_src=False):
            return pltpu.make_async_remote_copy(
                src_ref=src[a] if from_src else slot(a, *block), dst_ref=slot(a, *block),
                send_sem=send_sems.at[7 * a + k], recv_sem=recv_sems.at[7 * a + k],
                device_id=to, device_id_type=MESH)

        started = []
        mine = []
        for a in range(n):
            cp = pltpu.make_async_copy(src[a], slot(a, *me), loc_sems.at[a])
            cp.start()
            mine.append(cp)
            first = [copy(a, 0, me, sibling, True)]
            first += [copy(a, 1 + j, me, (*chip, c), True) for j, chip in enumerate(chips)]
            for cp in first:
                cp.start()
            started += first
        for a in range(n):
            for j, chip in enumerate(chips):
                copy(a, 1 + j, (*chip, c), me).wait_recv()
                fwd = copy(a, 4 + j, (*chip, c), sibling)
                fwd.start()
                started.append(fwd)
        for a in range(n):
            copy(a, 0, sibling, me).wait_recv()
            for j, chip in enumerate(chips):
                copy(a, 4 + j, (*chip, 1 - c), me).wait_recv()
        for cp in started:
            cp.wait_send()
        for cp in mine:
            cp.wait()

    any_ = pl.BlockSpec(memory_space=pl.ANY)
    return pl.pallas_call(
        body, name="all_gather_weights",
        in_specs=[any_] * n, out_specs=[any_] * n,
        out_shape=[jax.ShapeDtypeStruct((8,) + sh.shape, sh.dtype) for sh in shards],
        scratch_shapes=[pltpu.SemaphoreType.DMA((7 * n,)), pltpu.SemaphoreType.DMA((7 * n,)),
                        pltpu.SemaphoreType.DMA((n,))],
    )(*shards)


def _all_to_all(blocks, name):
    n = len(blocks)
    masks = [((k >> 2) & 1, (k >> 1) & 1, k & 1) for k in range(1, 8)]

    def body(*refs):
        src, out = refs[:n], refs[n:2 * n]
        send_sems, recv_sems, loc_sems = refs[2 * n:]
        x, y, c = lax.axis_index("x"), lax.axis_index("y"), lax.axis_index("c")
        me = 4 * x + 2 * y + c
        started = []
        for a in range(n):
            cp = pltpu.make_async_copy(src[a].at[me], out[a].at[me], loc_sems.at[a])
            cp.start()
            started.append(cp)
            for k, (mx, my, mc) in enumerate(masks):
                px, py, pc = _flip(x, mx), _flip(y, my), _flip(c, mc)
                cp = pltpu.make_async_remote_copy(
                    src_ref=src[a].at[4 * px + 2 * py + pc], dst_ref=out[a].at[me],
                    send_sem=send_sems.at[7 * a + k], recv_sem=recv_sems.at[7 * a + k],
                    device_id=(px, py, pc), device_id_type=MESH)
                cp.start()
                started.append(cp)
        for cp in started:
            cp.wait()

    any_ = pl.BlockSpec(memory_space=pl.ANY)
    return pl.pallas_call(
        body, name=name,
        in_specs=[any_] * n, out_specs=[any_] * n,
        out_shape=[jax.ShapeDtypeStruct(b.shape, b.dtype) for b in blocks],
        scratch_shapes=[pltpu.SemaphoreType.DMA((7 * n,)), pltpu.SemaphoreType.DMA((7 * n,)),
                        pltpu.SemaphoreType.DMA((n,))],
    )(*blocks)


def _adamw(parts, w, m, v, br, name):
    g8, r, c = parts.shape
    c1 = 1.0 - ADAM_B1 ** ADAM_STEP
    c2 = 1.0 - ADAM_B2 ** ADAM_STEP

    def body(p_ref, w_ref, m_ref, v_ref, g_ref, d_ref, m2_ref, v2_ref):
        g = p_ref[0]
        for d in range(1, g8):
            g = g + p_ref[d]
        m2 = ADAM_B1 * m_ref[...] + (1.0 - ADAM_B1) * g
        v2 = ADAM_B2 * v_ref[...] + (1.0 - ADAM_B2) * (g * g)
        g_ref[...] = g
        m2_ref[...] = m2
        v2_ref[...] = v2
        d_ref[...] = -ADAM_LR * ((m2 / c1) / (jnp.sqrt(v2 / c2) + ADAM_EPS) + ADAM_WD * w_ref[...])

    blk = pl.BlockSpec((br, c), lambda i: (i, 0))
    out = jax.ShapeDtypeStruct((r, c), F32)
    return pl.pallas_call(
        body, name=name, grid=(r // br,),
        in_specs=[pl.BlockSpec((g8, br, c), lambda i: (0, i, 0)), blk, blk, blk],
        out_specs=[blk] * 4, out_shape=[out] * 4,
        compiler_params=_params(VMEM_MID),
    )(parts, w, m, v)


def _pad_rows(a, rows):
    return jnp.pad(a, ((0, rows - a.shape[0]), (0, 0)))


def _local_step(x, tgt, g_mix, w_in, b_f, w_conv, g_conv_out, g_attn_out, w_o, g_ffn, w_up, w_ffn_conv, w_dn,
                g_final, tm=512, t=512, tf=256):
    assert tm == t and x.shape[0] % tm == 0 and x.shape[0] % tf == 0
    w_zc = w_in[:, :3 * CONV_CH]
    w_qkv = w_in[:, 3 * CONV_CH:3 * CONV_CH + 3 * ATTN_W]
    w_f = jnp.pad(w_in[:, 3 * CONV_CH + 3 * ATTN_W:], ((0, 0), (0, LANES - N_HEADS)))
    b_f_p = jnp.pad(b_f, ((0, 0), (0, LANES - N_HEADS)))
    w_conv_p = _pad_rows(w_conv, 8)
    w_fc_p = _pad_rows(w_ffn_conv, 8)
    g_final2 = g_final.reshape(1, D_MODEL)

    h1, zc, qkv, fpre, fcol, frow, nc = _fwd_in(x, g_mix, w_zc, w_qkv, w_f, b_f_p, w_conv_p, g_conv_out, tm)
    o, lcol = _attn_fwd(qkv, fcol, frow, t)
    x2, h2, mix = _fwd_out(x, nc, o, g_attn_out, w_o, g_ffn, tm)
    u0, act, dx3, dx3b, loss, dg_final = _fwd_ffn(h2, x2, tgt, w_up, w_fc_p, w_dn, g_final2, tf)

    du0, dx2, dx2b, dw_fc, dg_ffn = _bwd_ffn(dx3, dx3b, u0, x2, w_up, w_fc_p, w_dn, g_ffn, tf)
    dzc, dob, dcol, ccol, drow, crow, dw_conv, dg_attn, dg_conv = _bwd_out(
        dx2b, o, zc, fcol, lcol, w_o, g_attn_out, g_conv_out, w_conv_p, t)
    dq, dfq = _attn_bwd_dq(qkv, dob, ccol, dcol, frow, t)
    dk, dv, dfk = _attn_bwd_dkv(qkv, dob, crow, drow, fcol, t)
    gx, dfb, dg_mix, db_f = _bwd_in(x, dx2, dzc, dq, dk, dv, dfq, dfk, fpre, w_zc, w_qkv, w_f, g_mix, tm)

    bk = 1024 if x.shape[0] % 1024 == 0 else 512
    dw_up = _matmul_tn(h2, du0, 1408, bk, "dw_up")
    dw_dn = _matmul_tn(act, dx3b, 512, bk, "dw_down")
    dw_o = _matmul_tn(mix, dx2b, 512, bk, "dw_o")
    dw_zc = _matmul_tn(h1, dzc, 3 * CONV_CH, bk, "dw_in_conv")
    dw_q = _matmul_tn(h1, dq, ATTN_W, bk, "dw_in_q")
    dw_k = _matmul_tn(h1, dk, ATTN_W, bk, "dw_in_k")
    dw_v = _matmul_tn(h1, dv, ATTN_W, bk, "dw_in_v")
    dw_f = _matmul_tn(h1, dfb, LANES, bk, "dw_in_f")
    dw_in = jnp.concatenate([dw_zc, dw_q, dw_k, dw_v, dw_f[:, :N_HEADS]], axis=1)

    grads = dict(g_mix=dg_mix, w_in=dw_in, b_f=db_f[:, :N_HEADS], w_conv=dw_conv[:3], g_conv_out=dg_conv,
                 g_attn_out=dg_attn, w_o=dw_o, g_ffn=dg_ffn, w_up=dw_up, w_ffn_conv=dw_fc[:3], w_down=dw_dn,
                 g_final=dg_final)
    return loss[0, 0], gx, grads


_SMALL = (("g_mix", 1024), ("b_f", 8), ("g_conv_out", 512), ("g_attn_out", 512), ("g_ffn", 1024), ("g_final", 1024))
_SMALL_ROWS = 40


def _pack_small(vals):
    flat = jnp.concatenate([vals[k].reshape(-1) for k, _ in _SMALL])
    return jnp.pad(flat, (0, _SMALL_ROWS * LANES - flat.shape[0])).reshape(_SMALL_ROWS, LANES)


def _unpack_small(packed, shapes):
    flat = packed.reshape(-1)
    out, off = {}, 0
    for k, n in _SMALL:
        out[k] = flat[off:off + n].reshape(shapes[k])
        off += n
    return out


def _col_blocks(a):
    r, c8 = a.shape
    return jnp.transpose(a.reshape(r, 8, c8 // 8), (1, 0, 2))


def _from_col_blocks(a):
    g, r, c = a.shape
    return jnp.transpose(a, (1, 0, 2)).reshape(r, g * c)


def kernel(x, g_mix, w_in, b_f, w_conv, g_conv_out, g_attn_out, w_o, g_ffn, w_up, w_ffn_conv, w_down, g_final, loss_target, m_g_mix, m_w_in, m_b_f, m_w_conv, m_g_conv_out, m_g_attn_out, m_w_o, m_g_ffn, m_w_up, m_w_ffn_conv, m_w_down, m_g_final, v_g_mix, v_w_in, v_b_f, v_w_conv, v_g_conv_out, v_g_attn_out, v_w_o, v_g_ffn, v_w_up, v_w_ffn_conv, v_w_down, v_g_final):
    w = dict(g_mix=g_mix, w_in=w_in[0], b_f=b_f, w_conv=w_conv[0], g_conv_out=g_conv_out, g_attn_out=g_attn_out,
             w_o=w_o[0], g_ffn=g_ffn, w_up=w_up[0], w_ffn_conv=w_ffn_conv[0], w_down=w_down[0], g_final=g_final)
    m = dict(g_mix=m_g_mix, w_in=m_w_in[0], b_f=m_b_f, w_conv=m_w_conv[0], g_conv_out=m_g_conv_out,
             g_attn_out=m_g_attn_out, w_o=m_w_o[0], g_ffn=m_g_ffn, w_up=m_w_up[0], w_ffn_conv=m_w_ffn_conv[0],
             w_down=m_w_down[0], g_final=m_g_final)
    v = dict(g_mix=v_g_mix, w_in=v_w_in[0], b_f=v_b_f, w_conv=v_w_conv[0], g_conv_out=v_g_conv_out,
             g_attn_out=v_g_attn_out, w_o=v_w_o[0], g_ffn=v_g_ffn, w_up=v_w_up[0], w_ffn_conv=v_w_ffn_conv[0],
             w_down=v_w_down[0], g_final=v_g_final)
    shapes = dict(g_mix=g_mix.shape, w_in=w_in.shape, b_f=b_f.shape, w_conv=w_conv.shape,
                  g_conv_out=g_conv_out.shape, g_attn_out=g_attn_out.shape, w_o=w_o.shape, g_ffn=g_ffn.shape,
                  w_up=w_up.shape, w_ffn_conv=w_ffn_conv.shape, w_down=w_down.shape, g_final=g_final.shape)

    gathered = _all_gather([w["w_in"].astype(BF), w["w_o"].astype(BF), w["w_up"].astype(BF), w["w_down"].astype(BF),
                            w["w_conv"], w["w_ffn_conv"]])
    w_in_f = _from_col_blocks(gathered[0])
    w_o_f = gathered[1].reshape(D_MODEL, D_MODEL)
    w_up_f = _from_col_blocks(gathered[2])
    w_dn_f = gathered[3].reshape(D_FF, D_MODEL)
    w_conv_f = _from_col_blocks(gathered[4])
    w_fc_f = _from_col_blocks(gathered[5])

    loss, gx, grads = _local_step(x[0], loss_target[0], g_mix, w_in_f, b_f, w_conv_f, g_conv_out, g_attn_out, w_o_f,
                                  g_ffn, w_up_f, w_fc_f, w_dn_f, g_final)
    loss = lax.psum(loss, ("x", "y", "c"))

    big = ("w_in", "w_o", "w_up", "w_down", "w_conv", "w_ffn_conv")
    send = [_col_blocks(grads["w_in"]), grads["w_o"].reshape(8, D_MODEL // 8, D_MODEL), _col_blocks(grads["w_up"]),
            grads["w_down"].reshape(8, D_FF // 8, D_MODEL), _col_blocks(grads["w_conv"]),
            _col_blocks(grads["w_ffn_conv"]),
            jnp.broadcast_to(_pack_small(grads), (8, _SMALL_ROWS, LANES))]
    recv = _all_to_all(send, "all_to_all_grads")

    rows = dict(w_in=256, w_o=128, w_up=256, w_down=176, w_conv=3, w_ffn_conv=3)
    g_out, d_out, m_out, v_out = {}, {}, {}, {}
    for a, k in enumerate(big):
        res = _adamw(recv[a], w[k], m[k], v[k], rows[k], "adamw_" + k)
        g_out[k], d_out[k], m_out[k], v_out[k] = [r.reshape(shapes[k]) for r in res]
    res = _adamw(recv[6], _pack_small(w), _pack_small(m), _pack_small(v), _SMALL_ROWS, "adamw_gains")
    for dst, packed in zip((g_out, d_out, m_out, v_out), res):
        dst.update(_unpack_small(packed, shapes))

    order = ("g_mix", "w_in", "b_f", "w_conv", "g_conv_out", "g_attn_out", "w_o", "g_ffn", "w_up", "w_ffn_conv",
             "w_down", "g_final")
    return (loss, gx.reshape(x.shape), *[g_out[k] for k in order], *[d_out[k] for k in order],
            *[m_out[k] for k in order], *[v_out[k] for k in order])
```

```python
import functools

import jax
import jax.numpy as jnp
import numpy as np
from jax import lax
from jax.experimental import pallas as pl
from jax.experimental.pallas import tpu as pltpu

F32 = jnp.float32
BF = jnp.bfloat16
HI = lax.Precision.HIGHEST
MESH = pl.DeviceIdType.MESH

D_MODEL = 1024
CONV_CH = 512
ATTN_W = 512
N_HEADS = 8
HEAD_DIM = 64
D_FF = 2816
FF_CHUNK = 256
N_FF_CHUNKS = D_FF // FF_CHUNK
IN_COLS = 3080
EPS = 1e-6
NEG = -1e30
LANES = 128
VMEM_BIG = 56 * 1024 * 1024
VMEM_MID = 40 * 1024 * 1024

ADAM_LR = 0.001
ADAM_B1 = 0.9
ADAM_B2 = 0.999
ADAM_EPS = 1e-08
ADAM_WD = 0.01
ADAM_STEP = 10

NT = (((1,), (1,)), ((), ()))
TN = (((0,), (0,)), ((), ()))


def _dot(a, b):
    return jnp.dot(a, b, preferred_element_type=F32)


def _dot_nt(a, b):
    return lax.dot_general(a, b, NT, preferred_element_type=F32)


def _dot_tn(a, b):
    return lax.dot_general(a, b, TN, preferred_element_type=F32)


def _params(vmem=None, parallel=False):
    return pltpu.CompilerParams(
        dimension_semantics=None, vmem_limit_bytes=vmem)


def _rows(shape):
    return lax.broadcasted_iota(jnp.int32, shape, 0)


def _cols(shape):
    return lax.broadcasted_iota(jnp.int32, shape, 1)


def _shift_down(u, prev, k):
    n = prev.shape[0]
    out = pltpu.roll(u, k, 0)
    row = _rows(u.shape)
    for r in range(k):
        out = jnp.where(row == r, prev[n - k + r:n - k + r + 1, :].astype(u.dtype), out)
    return out


def _shift_up(u, nxt, k):
    tm = u.shape[0]
    out = pltpu.roll(u, tm - k, 0)
    row = _rows(u.shape)
    for r in range(k):
        out = jnp.where(row == tm - k + r, nxt[r:r + 1, :], out)
    return out


def _conv3(u, prev, w):
    u1 = _shift_down(u, prev, 1)
    u2 = _shift_down(u, prev, 2)
    return w[0:1, :] * u2 + w[1:2, :] * u1 + w[2:3, :] * u, u1, u2


def _conv3_bwd(d, nxt, w):
    return w[2:3, :] * d + w[1:2, :] * _shift_up(d, nxt, 1) + w[0:1, :] * _shift_up(d, nxt, 2)


def _rms(x):
    return lax.rsqrt(jnp.mean(x * x, axis=-1, keepdims=True) + EPS)


def _rms_bwd(x, r, dyg):
    return r * dyg - x * (r * r * r) * jnp.mean(dyg * x, axis=-1, keepdims=True)


def _full(shape):
    nd = len(shape)
    return pl.BlockSpec(shape, lambda i, _n=nd: (0,) * _n)


ONES_LANE = 24


def _bias_scatter():
    sc = np.zeros((LANES, 2 * N_HEADS * LANES), np.float32)
    koff = N_HEADS * LANES
    for h in range(N_HEADS):
        aux = HEAD_DIM * (1 - h % 2)
        for j in range(3):
            sc[8 * j + h, LANES * h + aux + j] = 1.0
            sc[ONES_LANE, koff + LANES * h + aux + j] = 1.0
            sc[ONES_LANE, LANES * h + aux + 3 + j] = 1.0
            sc[8 * j + h, koff + LANES * h + aux + 3 + j] = -1.0
    return jnp.asarray(sc, BF)


def _fwd_in(x, g_mix, w_zc, w_qkv, w_f, b_f, w_conv, g_conv_out, tm):
    s = x.shape[0]
    nb = s // tm
    aw = N_HEADS * LANES

    def body(x_ref, gm_ref, wzc_ref, wqkv_ref, wf_ref, bf_ref, wc_ref, gco_ref, sc_ref,
             h1_ref, zc_ref, qkv_ref, fpre_ref, fcol_ref, frow_ref, nc_ref, qa_ref, ka_ref, vt_ref, cu_ref, cf_ref):
        i = pl.program_id(0)

        @pl.when(i == 0)
        def _():
            cu_ref[...] = jnp.zeros_like(cu_ref)
            cf_ref[...] = jnp.zeros_like(cf_ref)

        xv = x_ref[...]
        hb = (xv * _rms(xv) * gm_ref[...]).astype(BF)
        h1_ref[...] = hb
        zc = _dot(hb, wzc_ref[...])
        zc_ref[...] = zc
        qkv = _dot(hb, wqkv_ref[...])
        qkv = jnp.where(_cols(qkv.shape) < ATTN_W, qkv * 0.125, qkv)
        qkvb = qkv.astype(BF)
        qkv_ref[...] = qkvb
        vt_ref[0] = qkv[:, 2 * ATTN_W:].T.astype(BF)

        gb, gc, xc = zc[:, :CONV_CH], zc[:, CONV_CH:2 * CONV_CH], zc[:, 2 * CONV_CH:]
        u = gc * xc
        cv, _, _ = _conv3(u, cu_ref[...], wc_ref[...])
        cu_ref[...] = u[tm - 8:, :]
        y = gb * cv
        nc_ref[...] = (y * _rms(y) * gco_ref[...]).astype(BF)

        fpre = _dot(hb, wf_ref[...]) + bf_ref[...]
        fpre_ref[...] = fpre
        logf = jnp.minimum(fpre, 0.0) - jnp.log1p(jnp.exp(-jnp.abs(fpre)))
        logf = jnp.where(_cols(logf.shape) < N_HEADS, logf, 0.0)
        tri = (_rows((tm, tm)) >= _cols((tm, tm))).astype(F32)
        fcol = jnp.dot(tri, logf, precision=HI, preferred_element_type=F32) + cf_ref[...]
        cf_ref[...] = fcol[tm - 1:tm, :]
        fcol_ref[...] = fcol
        frow_ref[0] = fcol.T[:N_HEADS, :]

        hi = fcol.astype(BF).astype(F32)
        rest = fcol - hi
        mid = rest.astype(BF).astype(F32)
        lo = (rest - mid).astype(BF).astype(F32)
        parts = hi + pltpu.roll(mid, 8, 1) + pltpu.roll(lo, 16, 1)
        parts = jnp.where(_cols(parts.shape) == ONES_LANE, 1.0, parts).astype(BF)
        feat = _dot(parts, sc_ref[...]).astype(BF)
        lane = _cols((tm, LANES))
        for h in range(N_HEADS):
            hp, hh = divmod(h, 2)
            own = (lane >= HEAD_DIM * hh) & (lane < HEAD_DIM * (hh + 1))
            hs = slice(LANES * h, LANES * (h + 1))
            qa_ref[:, hs] = jnp.where(own, qkvb[:, LANES * hp:LANES * (hp + 1)], feat[:, hs])
            ka_ref[:, hs] = jnp.where(own, qkvb[:, ATTN_W + LANES * hp:ATTN_W + LANES * (hp + 1)],
                                      feat[:, aw + LANES * h:aw + LANES * (h + 1)])

    blk = lambda c: pl.BlockSpec((tm, c), lambda i: (i, 0))
    return pl.pallas_call(
        body, name="fwd_in", grid=(nb,),
        in_specs=[blk(D_MODEL), _full((1, D_MODEL)), _full(w_zc.shape), _full(w_qkv.shape), _full(w_f.shape),
                  _full((1, LANES)), _full((8, CONV_CH)), _full((1, CONV_CH)), _full((LANES, 2 * aw))],
        out_specs=[blk(D_MODEL), blk(3 * CONV_CH), blk(3 * ATTN_W), blk(LANES), blk(LANES),
                   pl.BlockSpec((1, N_HEADS, tm), lambda i: (i, 0, 0)), blk(CONV_CH), blk(aw), blk(aw),
                   pl.BlockSpec((1, ATTN_W, tm), lambda i: (i, 0, 0))],
        out_shape=[jax.ShapeDtypeStruct((s, D_MODEL), BF), jax.ShapeDtypeStruct((s, 3 * CONV_CH), F32),
                   jax.ShapeDtypeStruct((s, 3 * ATTN_W), BF), jax.ShapeDtypeStruct((s, LANES), F32),
                   jax.ShapeDtypeStruct((s, LANES), F32), jax.ShapeDtypeStruct((nb, N_HEADS, tm), F32),
                   jax.ShapeDtypeStruct((s, CONV_CH), BF), jax.ShapeDtypeStruct((s, aw), BF),
                   jax.ShapeDtypeStruct((s, aw), BF), jax.ShapeDtypeStruct((nb, ATTN_W, tm), BF)],
        scratch_shapes=[pltpu.VMEM((8, CONV_CH), F32), pltpu.VMEM((1, LANES), F32)],
        compiler_params=_params(VMEM_MID),
    )(x, g_mix, w_zc, w_qkv, w_f, b_f, w_conv, g_conv_out, _bias_scatter())


def _head_mask(shape, hh):
    lane = _cols(shape)
    return (lane >= HEAD_DIM * hh) & (lane < HEAD_DIM * (hh + 1))


def _attn_fwd(qa, ka, vt, t):
    s = qa.shape[0]
    nb = s // t

    def body(qa_ref, ka_any, vt_any, o_ref, l_ref, ka_scr, vt_scr, m_scr, acc_scr, sem):
        i = pl.program_id(0)
        _load_weights(i, [(ka_any, ka_scr), (vt_any, vt_scr)], sem)

        causal_t = _rows((t, t)) <= _cols((t, t))
        row = _rows((LANES, t))
        lrows = jnp.zeros((LANES, t), F32)
        for hp in range(N_HEADS // 2):
            ps = slice(hp * LANES, (hp + 1) * LANES)
            o_pair = None
            for hh in range(2):
                h = 2 * hp + hh
                hs = slice(h * LANES, (h + 1) * LANES)
                aux = HEAD_DIM * (1 - hh)
                own = (row >= HEAD_DIM * hh) & (row < HEAD_DIM * (hh + 1))
                fill = jnp.where(row == aux, 1.0, 0.0).astype(BF)
                qh = qa_ref[:, hs]
                m_scr[...] = jnp.full(m_scr.shape, NEG, F32)
                acc_scr[...] = jnp.zeros(acc_scr.shape, F32)

                def tile(kb, masked, ps=ps, hs=hs, qh=qh, own=own, fill=fill):
                    rs = pl.ds(pl.multiple_of(kb * t, t), t)
                    e = _dot_nt(ka_scr[rs, hs], qh)
                    if masked:
                        e = jnp.where(causal_t, e, NEG)
                    m_old = m_scr[...]
                    m_new = jnp.maximum(m_old, jnp.max(e, axis=0, keepdims=True))
                    p = jnp.exp(e - m_new).astype(BF)
                    vta = jnp.where(own, vt_scr[kb, ps, :], fill)
                    acc_scr[...] = jnp.exp(m_old - m_new) * acc_scr[...] + _dot(vta, p)
                    m_scr[...] = m_new

                def loop_body(kb, carry, tile=tile):
                    tile(kb, False)
                    return carry

                lax.fori_loop(0, i, loop_body, 0)
                tile(i, True)
                acc = acc_scr[...]
                denom = acc[aux:aux + 1, :]
                o_h = acc / denom
                lrows = jnp.where(row == h, m_scr[...] + jnp.log(denom), lrows)
                o_pair = o_h if hh == 0 else jnp.where(row < HEAD_DIM, o_pair, o_h)
            o_ref[:, ps] = o_pair.T
        l_ref[...] = lrows.T

    any_ = pl.BlockSpec(memory_space=pl.ANY)
    return pl.pallas_call(
        body, name="attn_fwd", grid=(nb,),
        in_specs=[pl.BlockSpec((t, N_HEADS * LANES), lambda i: (i, 0)), any_, any_],
        out_specs=[pl.BlockSpec((t, ATTN_W), lambda i: (i, 0)), pl.BlockSpec((t, LANES), lambda i: (i, 0))],
        out_shape=[jax.ShapeDtypeStruct((s, ATTN_W), F32), jax.ShapeDtypeStruct((s, LANES), F32)],
        scratch_shapes=[pltpu.VMEM(ka.shape, BF), pltpu.VMEM(vt.shape, BF), pltpu.VMEM((1, t), F32),
                        pltpu.VMEM((LANES, t), F32), pltpu.SemaphoreType.DMA((2,))],
        compiler_params=_params(VMEM_BIG),
    )(qa, ka, vt)


def _fwd_out(x, nc, o, g_attn_out, w_o, g_ffn, tm):
    s = x.shape[0]

    def body(x_ref, nc_ref, o_ref, ga_ref, wo_ref, gf_ref, x2_ref, h2_ref, mix_ref):
        ov = o_ref[...]
        na = (ov * _rms(ov) * ga_ref[...]).astype(BF)
        ncv = nc_ref[...]
        mix_ref[:, :CONV_CH] = ncv
        mix_ref[:, CONV_CH:] = na
        x2 = x_ref[...] + _dot(ncv, wo_ref[:CONV_CH, :]) + _dot(na, wo_ref[CONV_CH:, :])
        x2_ref[...] = x2
        h2_ref[...] = (x2 * _rms(x2) * gf_ref[...]).astype(BF)

    blk = lambda c: pl.BlockSpec((tm, c), lambda i: (i, 0))
    return pl.pallas_call(
        body, name="fwd_out", grid=(s // tm,),
        in_specs=[blk(D_MODEL), blk(CONV_CH), blk(ATTN_W), _full((1, ATTN_W)), _full(w_o.shape), _full((1, D_MODEL))],
        out_specs=[blk(D_MODEL), blk(D_MODEL), blk(D_MODEL)],
        out_shape=[jax.ShapeDtypeStruct((s, D_MODEL), F32), jax.ShapeDtypeStruct((s, D_MODEL), BF),
                   jax.ShapeDtypeStruct((s, D_MODEL), BF)],
        compiler_params=_params(VMEM_MID),
    )(x, nc, o, g_attn_out, w_o, g_ffn)


def _load_weights(i, pairs, sem):
    @pl.when(i == 0)
    def _():
        cps = [pltpu.make_async_copy(src, dst, sem.at[n]) for n, (src, dst) in enumerate(pairs)]
        for cp in cps:
            cp.start()
        for cp in cps:
            cp.wait()


def _fwd_ffn(h2, x2, tgt, w_up, w_ffn_conv, w_dn, g_final, tm):
    s = x2.shape[0]

    def body(h2_ref, x2_ref, tgt_ref, wfc_ref, gfin_ref, wup_any, wdn_any,
             u0_ref, act_ref, dx3_ref, dx3b_ref, loss_ref, dgfin_ref,
             wup, wdn, carry, acc, sem):
        i = pl.program_id(0)
        _load_weights(i, [(wup_any, wup), (wdn_any, wdn)], sem)

        @pl.when(i == 0)
        def _():
            carry[...] = jnp.zeros_like(carry)
            loss_ref[...] = jnp.zeros_like(loss_ref)
            dgfin_ref[...] = jnp.zeros_like(dgfin_ref)

        hb = h2_ref[...]
        acc[...] = jnp.zeros_like(acc)
        for j in range(N_FF_CHUNKS):
            ca = slice(j * FF_CHUNK, (j + 1) * FF_CHUNK)
            cg = slice(D_FF + j * FF_CHUNK, D_FF + (j + 1) * FF_CHUNK)
            parts = []
            for cc in (ca, cg):
                u0 = _dot(hb, wup[:, cc])
                u0_ref[:, cc] = u0.astype(BF)
                uu, _, _ = _conv3(u0, carry[:, cc], wfc_ref[:, cc])
                carry[:, cc] = u0[tm - 8:, :]
                parts.append(uu)
            ua, ug = parts
            act = (ug * jax.nn.sigmoid(ug) * ua).astype(BF)
            act_ref[:, ca] = act
            acc[...] += _dot(act, wdn[ca, :])

        x3 = x2_ref[...] + acc[...]
        r3 = _rms(x3)
        gfin = gfin_ref[...]
        xn = x3 * r3
        diff = xn * gfin - tgt_ref[...]
        loss_ref[...] += jnp.sum(jnp.sum(diff * diff, axis=-1, keepdims=True), axis=0, keepdims=True) * (0.5 / D_MODEL)
        dy = diff * (1.0 / D_MODEL)
        dgfin_ref[...] += jnp.sum(dy * xn, axis=0, keepdims=True)
        dx3 = _rms_bwd(x3, r3, dy * gfin)
        dx3_ref[...] = dx3
        dx3b_ref[...] = dx3.astype(BF)

    blk = lambda c: pl.BlockSpec((tm, c), lambda i: (i, 0))
    any_ = pl.BlockSpec(memory_space=pl.ANY)
    return pl.pallas_call(
        body, name="fwd_ffn", grid=(s // tm,),
        in_specs=[blk(D_MODEL), blk(D_MODEL), blk(D_MODEL), _full((8, 2 * D_FF)), _full((1, D_MODEL)), any_, any_],
        out_specs=[blk(2 * D_FF), blk(D_FF), blk(D_MODEL), blk(D_MODEL), _full((1, 1)), _full((1, D_MODEL))],
        out_shape=[jax.ShapeDtypeStruct((s, 2 * D_FF), BF), jax.ShapeDtypeStruct((s, D_FF), BF),
                   jax.ShapeDtypeStruct((s, D_MODEL), F32), jax.ShapeDtypeStruct((s, D_MODEL), BF),
                   jax.ShapeDtypeStruct((1, 1), F32), jax.ShapeDtypeStruct((1, D_MODEL), F32)],
        scratch_shapes=[pltpu.VMEM(w_up.shape, BF), pltpu.VMEM(w_dn.shape, BF), pltpu.VMEM((8, 2 * D_FF), F32),
                        pltpu.VMEM((tm, D_MODEL), F32), pltpu.SemaphoreType.DMA((2,))],
        compiler_params=_params(VMEM_BIG),
    )(h2, x2, tgt, w_ffn_conv, g_final, w_up, w_dn)


def _bwd_ffn(dx3, dx3b, u0, x2, w_up, w_ffn_conv, w_dn, g_ffn, tm):
    s = x2.shape[0]
    nb = s // tm
    hr = 16

    def body(dx3_ref, dx3b_ref, u0_ref, halo_ref, x2_ref, wfc_ref, gf_ref, wup_any, wdn_any,
             du0_ref, dx2_ref, dx2b_ref, dwfc_ref, dgf_ref,
             wup, wdn, carry, acc, sem):
        i = pl.program_id(0)
        rb = nb - 1 - i
        _load_weights(i, [(wup_any, wup), (wdn_any, wdn)], sem)

        @pl.when(i == 0)
        def _():
            carry[...] = jnp.zeros_like(carry)
            dwfc_ref[...] = jnp.zeros_like(dwfc_ref)
            dgf_ref[...] = jnp.zeros_like(dgf_ref)

        db = dx3b_ref[...]
        live = (rb > 0).astype(F32)
        acc[...] = jnp.zeros_like(acc)
        for j in range(N_FF_CHUNKS):
            ca = slice(j * FF_CHUNK, (j + 1) * FF_CHUNK)
            cg = slice(D_FF + j * FF_CHUNK, D_FF + (j + 1) * FF_CHUNK)
            fw = []
            for cc in (ca, cg):
                u0 = u0_ref[:, cc].astype(F32)
                prev = halo_ref[:, cc].astype(F32) * live
                fw.append((u0,) + _conv3(u0, prev, wfc_ref[:, cc]))
            (u0a, ua, u1a, u2a), (u0g, ug, u1g, u2g) = fw
            sg = jax.nn.sigmoid(ug)
            dact = _dot_nt(db, wdn[ca, :])
            da = dact * (ug * sg)
            dg = dact * ua * (sg * (1.0 + ug * (1.0 - sg)))
            for cc, d, u0c, u1c, u2c in ((ca, da, u0a, u1a, u2a), (cg, dg, u0g, u1g, u2g)):
                dwfc_ref[0:1, cc] += jnp.sum(d * u2c, axis=0, keepdims=True)
                dwfc_ref[1:2, cc] += jnp.sum(d * u1c, axis=0, keepdims=True)
                dwfc_ref[2:3, cc] += jnp.sum(d * u0c, axis=0, keepdims=True)
                du0 = _conv3_bwd(d, carry[:, cc], wfc_ref[:, cc]).astype(BF)
                carry[:, cc] = d[:8, :]
                du0_ref[:, cc] = du0
                acc[...] += _dot_nt(du0, wup[:, cc])

        x2v = x2_ref[...]
        r2 = _rms(x2v)
        dh2 = acc[...]
        dgf_ref[...] += jnp.sum(dh2 * (x2v * r2), axis=0, keepdims=True)
        dx2 = dx3_ref[...] + _rms_bwd(x2v, r2, dh2 * gf_ref[...])
        dx2_ref[...] = dx2
        dx2b_ref[...] = dx2.astype(BF)

    blk = lambda c: pl.BlockSpec((tm, c), lambda i: (nb - 1 - i, 0))
    halo = pl.BlockSpec((hr, 2 * D_FF), lambda i: (jnp.maximum((nb - 1 - i) * (tm // hr) - 1, 0), 0))
    any_ = pl.BlockSpec(memory_space=pl.ANY)
    return pl.pallas_call(
        body, name="bwd_ffn", grid=(nb,),
        in_specs=[blk(D_MODEL), blk(D_MODEL), blk(2 * D_FF), halo, blk(D_MODEL), _full((8, 2 * D_FF)),
                  _full((1, D_MODEL)), any_, any_],
        out_specs=[blk(2 * D_FF), blk(D_MODEL), blk(D_MODEL), _full((8, 2 * D_FF)), _full((1, D_MODEL))],
        out_shape=[jax.ShapeDtypeStruct((s, 2 * D_FF), BF), jax.ShapeDtypeStruct((s, D_MODEL), F32),
                   jax.ShapeDtypeStruct((s, D_MODEL), BF), jax.ShapeDtypeStruct((8, 2 * D_FF), F32),
                   jax.ShapeDtypeStruct((1, D_MODEL), F32)],
        scratch_shapes=[pltpu.VMEM(w_up.shape, BF), pltpu.VMEM(w_dn.shape, BF), pltpu.VMEM((8, 2 * D_FF), F32),
                        pltpu.VMEM((tm, D_MODEL), F32), pltpu.SemaphoreType.DMA((2,))],
        compiler_params=_params(VMEM_BIG),
    )(dx3, dx3b, u0, u0, x2, w_ffn_conv, g_ffn, w_up, w_dn)


def _bwd_out(dx2b, o, zc, fcol, lcol, w_o, g_attn_out, g_conv_out, w_conv, tm):
    s = o.shape[0]
    nb = s // tm

    def body(dx2b_ref, o_ref, zc_ref, halo_ref, fcol_ref, lcol_ref, wo_ref, ga_ref, gco_ref, wc_ref,
             dzc_ref, dob_ref, dcol_ref, ccol_ref, drow_ref, crow_ref, dwc_ref, dga_ref, dgco_ref, carry):
        i = pl.program_id(0)
        rb = nb - 1 - i

        @pl.when(i == 0)
        def _():
            carry[...] = jnp.zeros_like(carry)
            dwc_ref[...] = jnp.zeros_like(dwc_ref)
            dga_ref[...] = jnp.zeros_like(dga_ref)
            dgco_ref[...] = jnp.zeros_like(dgco_ref)

        dmix = _dot_nt(dx2b_ref[...], wo_ref[...])
        dnc, dna = dmix[:, :CONV_CH], dmix[:, CONV_CH:]

        ov = o_ref[...]
        ra = _rms(ov)
        dga_ref[...] += jnp.sum(dna * (ov * ra), axis=0, keepdims=True)
        do = _rms_bwd(ov, ra, dna * ga_ref[...])
        dob_ref[...] = do.astype(BF)
        sel = (_rows((ATTN_W, LANES)) // HEAD_DIM == _cols((ATTN_W, LANES))).astype(F32)
        delta = jnp.dot(do * ov, sel, precision=HI, preferred_element_type=F32)
        cc = fcol_ref[...] - lcol_ref[...]
        dcol_ref[...] = delta
        ccol_ref[...] = cc
        drow_ref[0] = delta.T[:N_HEADS, :]
        crow_ref[0] = cc.T[:N_HEADS, :]

        zc_v = zc_ref[...]
        gb, gc, xc = zc_v[:, :CONV_CH], zc_v[:, CONV_CH:2 * CONV_CH], zc_v[:, 2 * CONV_CH:]
        hal = halo_ref[...] * (rb > 0).astype(F32)
        u = gc * xc
        prev = hal[:, CONV_CH:2 * CONV_CH] * hal[:, 2 * CONV_CH:]
        wc = wc_ref[...]
        cv, u1, u2 = _conv3(u, prev, wc)
        y = gb * cv
        rc = _rms(y)
        dgco_ref[...] += jnp.sum(dnc * (y * rc), axis=0, keepdims=True)
        dy = _rms_bwd(y, rc, dnc * gco_ref[...])
        dcv = dy * gb
        dwc_ref[0:1, :] += jnp.sum(dcv * u2, axis=0, keepdims=True)
        dwc_ref[1:2, :] += jnp.sum(dcv * u1, axis=0, keepdims=True)
        dwc_ref[2:3, :] += jnp.sum(dcv * u, axis=0, keepdims=True)
        du = _conv3_bwd(dcv, carry[...], wc)
        carry[...] = dcv[:8, :]
        dzc_ref[:, :CONV_CH] = (dy * cv).astype(BF)
        dzc_ref[:, CONV_CH:2 * CONV_CH] = (du * xc).astype(BF)
        dzc_ref[:, 2 * CONV_CH:] = (du * gc).astype(BF)

    blk = lambda c: pl.BlockSpec((tm, c), lambda i: (nb - 1 - i, 0))
    halo = pl.BlockSpec((8, 3 * CONV_CH), lambda i: (jnp.maximum((nb - 1 - i) * (tm // 8) - 1, 0), 0))
    row = pl.BlockSpec((1, N_HEADS, tm), lambda i: (nb - 1 - i, 0, 0))
    return pl.pallas_call(
        body, name="bwd_out", grid=(nb,),
        in_specs=[blk(D_MODEL), blk(ATTN_W), blk(3 * CONV_CH), halo, blk(LANES), blk(LANES), _full(w_o.shape),
                  _full((1, ATTN_W)), _full((1, CONV_CH)), _full((8, CONV_CH))],
        out_specs=[blk(3 * CONV_CH), blk(ATTN_W), blk(LANES), blk(LANES), row, row,
                   _full((8, CONV_CH)), _full((1, ATTN_W)), _full((1, CONV_CH))],
        out_shape=[jax.ShapeDtypeStruct((s, 3 * CONV_CH), BF), jax.ShapeDtypeStruct((s, ATTN_W), BF),
                   jax.ShapeDtypeStruct((s, LANES), F32), jax.ShapeDtypeStruct((s, LANES), F32),
                   jax.ShapeDtypeStruct((nb, N_HEADS, tm), F32), jax.ShapeDtypeStruct((nb, N_HEADS, tm), F32),
                   jax.ShapeDtypeStruct((8, CONV_CH), F32), jax.ShapeDtypeStruct((1, ATTN_W), F32),
                   jax.ShapeDtypeStruct((1, CONV_CH), F32)],
        scratch_shapes=[pltpu.VMEM((8, CONV_CH), F32)],
        compiler_params=_params(VMEM_MID),
    )(dx2b, o, zc, zc, fcol, lcol, w_o, g_attn_out, g_conv_out, w_conv)


def _attn_bwd_dq(qkv, dob, ccol, dcol, frow, t):
    s = qkv.shape[0]
    nb = s // t

    def body(q_ref, do_ref, ccol_ref, dcol_ref, frow_ref, qkv_any, dq_ref, dfq_ref, k_scr, v_scr, acc_scr, sem):
        i = pl.program_id(0)

        @pl.when(i == 0)
        def _():
            ck = pltpu.make_async_copy(qkv_any.at[:, pl.ds(ATTN_W, ATTN_W)], k_scr, sem.at[0])
            cv = pltpu.make_async_copy(qkv_any.at[:, pl.ds(2 * ATTN_W, ATTN_W)], v_scr, sem.at[1])
            ck.start()
            cv.start()
            ck.wait()
            cv.wait()

        causal = _cols((t, t)) <= _rows((t, t))
        lane = _cols((t, LANES))
        dfq = jnp.zeros((t, LANES), F32)
        for hp in range(N_HEADS // 2):
            cs = slice(hp * LANES, (hp + 1) * LANES)
            q2 = q_ref[:, cs]
            do2 = do_ref[:, cs]
            dq_pair = None
            for hh in range(2):
                h = 2 * hp + hh
                hm = _head_mask(q2.shape, hh)
                qh = jnp.where(hm, q2, jnp.zeros_like(q2))
                doh = jnp.where(hm, do2, jnp.zeros_like(do2))
                cq = ccol_ref[:, h:h + 1]
                dl = dcol_ref[:, h:h + 1]
                aux = HEAD_DIM * (1 - hh)
                acc_scr[...] = jnp.zeros(acc_scr.shape, F32)

                def tile(kb, masked, cs=cs, qh=qh, doh=doh, cq=cq, dl=dl, h=h, aux=aux):
                    rs = pl.ds(pl.multiple_of(kb * t, t), t)
                    k2 = k_scr[rs, cs]
                    v2 = v_scr[rs, cs]
                    e = _dot_nt(qh, k2) + (cq - frow_ref[kb, h:h + 1, :])
                    if masked:
                        e = jnp.where(causal, e, NEG)
                    p = jnp.exp(e)
                    ds = p * (_dot_nt(doh, v2) - dl)
                    acc_scr[...] += _dot(ds.astype(BF), jnp.where(lane == aux, jnp.ones_like(k2), k2))

                def loop_body(kb, carry, tile=tile):
                    tile(kb, False)
                    return carry

                lax.fori_loop(0, i, loop_body, 0)
                tile(i, True)
                aq = acc_scr[...]
                dfq = jnp.where(lane == h, aq[:, aux:aux + 1], dfq)
                dq_h = aq * 0.125
                dq_pair = dq_h if hh == 0 else jnp.where(lane < HEAD_DIM, dq_pair, dq_h)
            dq_ref[:, cs] = dq_pair.astype(BF)
        dfq_ref[...] = dfq

    return pl.pallas_call(
        body, name="attn_bwd_dq", grid=(nb,),
        in_specs=[pl.BlockSpec((t, ATTN_W), lambda i: (i, 0)), pl.BlockSpec((t, ATTN_W), lambda i: (i, 0)),
                  pl.BlockSpec((t, LANES), lambda i: (i, 0)), pl.BlockSpec((t, LANES), lambda i: (i, 0)),
                  _full(frow.shape), pl.BlockSpec(memory_space=pl.ANY)],
        out_specs=[pl.BlockSpec((t, ATTN_W), lambda i: (i, 0)), pl.BlockSpec((t, LANES), lambda i: (i, 0))],
        out_shape=[jax.ShapeDtypeStruct((s, ATTN_W), BF), jax.ShapeDtypeStruct((s, LANES), F32)],
        scratch_shapes=[pltpu.VMEM((s, ATTN_W), BF), pltpu.VMEM((s, ATTN_W), BF), pltpu.VMEM((t, LANES), F32),
                        pltpu.SemaphoreType.DMA((2,))],
        compiler_params=_params(VMEM_MID),
    )(qkv, dob, ccol, dcol, frow, qkv)


def _attn_bwd_dkv(qkv, dob, crow, drow, fcol, t):
    s = qkv.shape[0]
    nb = s // t

    def body(k_ref, v_ref, fcol_ref, crow_ref, drow_ref, qkv_any, do_any,
             dk_ref, dv_ref, dfk_ref, q_scr, do_scr, acck, accv, sem):
        j = pl.program_id(0)

        @pl.when(j == 0)
        def _():
            cq = pltpu.make_async_copy(qkv_any.at[:, pl.ds(0, ATTN_W)], q_scr, sem.at[0])
            cd = pltpu.make_async_copy(do_any, do_scr, sem.at[1])
            cq.start()
            cd.start()
            cq.wait()
            cd.wait()

        causal_t = _rows((t, t)) <= _cols((t, t))
        lane = _cols((t, LANES))
        dfk = jnp.zeros((t, LANES), F32)
        for hp in range(N_HEADS // 2):
            cs = slice(hp * LANES, (hp + 1) * LANES)
            k2 = k_ref[:, cs]
            v2 = v_ref[:, cs]
            dk_pair = dv_pair = None
            for hh in range(2):
                h = 2 * hp + hh
                hm = _head_mask(k2.shape, hh)
                kh = jnp.where(hm, k2, jnp.zeros_like(k2))
                vh = jnp.where(hm, v2, jnp.zeros_like(v2))
                fk = fcol_ref[:, h:h + 1]
                aux = HEAD_DIM * (1 - hh)
                acck[...] = jnp.zeros(acck.shape, F32)
                accv[...] = jnp.zeros(accv.shape, F32)

                def tile(qb, masked, cs=cs, kh=kh, vh=vh, fk=fk, hm=hm, aux=aux, h=h):
                    rs = pl.ds(pl.multiple_of(qb * t, t), t)
                    q2 = q_scr[rs, cs]
                    do2 = do_scr[rs, cs]
                    e = _dot_nt(kh, q2) + (crow_ref[qb, h:h + 1, :] - fk)
                    if masked:
                        e = jnp.where(causal_t, e, NEG)
                    p = jnp.exp(e)
                    accv[...] += _dot(p.astype(BF), do2)
                    ds = p * (_dot_nt(vh, do2) - drow_ref[qb, h:h + 1, :])
                    qa = jnp.where(lane == aux, jnp.ones_like(q2), jnp.where(hm, q2, jnp.zeros_like(q2)))
                    acck[...] += _dot(ds.astype(BF), qa)

                def loop_body(qb, carry, tile=tile):
                    tile(qb, False)
                    return carry

                tile(j, True)
                lax.fori_loop(j + 1, nb, loop_body, 0)
                ak = acck[...]
                av = accv[...]
                dfk = jnp.where(lane == h, -ak[:, aux:aux + 1], dfk)
                dk_pair = ak if hh == 0 else jnp.where(lane < HEAD_DIM, dk_pair, ak)
                dv_pair = av if hh == 0 else jnp.where(lane < HEAD_DIM, dv_pair, av)
            dk_ref[:, cs] = dk_pair.astype(BF)
            dv_ref[:, cs] = dv_pair.astype(BF)
        dfk_ref[...] = dfk

    any_ = pl.BlockSpec(memory_space=pl.ANY)
    return pl.pallas_call(
        body, name="attn_bwd_dkv", grid=(nb,),
        in_specs=[pl.BlockSpec((t, ATTN_W), lambda j: (j, 1)), pl.BlockSpec((t, ATTN_W), lambda j: (j, 2)),
                  pl.BlockSpec((t, LANES), lambda j: (j, 0)), _full(crow.shape), _full(drow.shape), any_, any_],
        out_specs=[pl.BlockSpec((t, ATTN_W), lambda j: (j, 0)), pl.BlockSpec((t, ATTN_W), lambda j: (j, 0)),
                   pl.BlockSpec((t, LANES), lambda j: (j, 0))],
        out_shape=[jax.ShapeDtypeStruct((s, ATTN_W), BF), jax.ShapeDtypeStruct((s, ATTN_W), BF),
                   jax.ShapeDtypeStruct((s, LANES), F32)],
        scratch_shapes=[pltpu.VMEM((s, ATTN_W), BF), pltpu.VMEM((s, ATTN_W), BF), pltpu.VMEM((t, LANES), F32),
                        pltpu.VMEM((t, LANES), F32), pltpu.SemaphoreType.DMA((2,))],
        compiler_params=_params(VMEM_MID),
    )(qkv, qkv, fcol, crow, drow, qkv, dob)


def _bwd_in(x, dx2, dzc, dq, dk, dv, dfq, dfk, fpre, w_zc, w_qkv, w_f, g_mix, tm):
    s = x.shape[0]
    nb = s // tm

    def body(x_ref, dx2_ref, dzc_ref, dq_ref, dk_ref, dv_ref, dfq_ref, dfk_ref, fpre_ref, wzc_ref, wqkv_ref, wf_ref,
             gm_ref, gx_ref, dfb_ref, dgm_ref, dbf_ref, carry):
        i = pl.program_id(0)

        @pl.when(i == 0)
        def _():
            carry[...] = jnp.zeros_like(carry)
            dgm_ref[...] = jnp.zeros_like(dgm_ref)
            dbf_ref[...] = jnp.zeros_like(dbf_ref)

        triu = (_rows((tm, tm)) <= _cols((tm, tm))).astype(F32)
        df_cum = dfq_ref[...] + dfk_ref[...]
        dlogf = jnp.dot(triu, df_cum, precision=HI, preferred_element_type=F32) + carry[...]
        carry[...] = dlogf[0:1, :]
        fpre = fpre_ref[...]
        df = jnp.where(_cols(fpre.shape) < N_HEADS, dlogf / (1.0 + jnp.exp(fpre)), 0.0)
        dbf_ref[...] += jnp.sum(df, axis=0, keepdims=True)
        dfb = df.astype(BF)
        dfb_ref[...] = dfb

        dh1 = _dot_nt(dzc_ref[...], wzc_ref[...])
        dh1 += _dot_nt(dq_ref[...], wqkv_ref[:, :ATTN_W])
        dh1 += _dot_nt(dk_ref[...], wqkv_ref[:, ATTN_W:2 * ATTN_W])
        dh1 += _dot_nt(dv_ref[...], wqkv_ref[:, 2 * ATTN_W:])
        dh1 += _dot_nt(dfb, wf_ref[...])
        xv = x_ref[...]
        r1 = _rms(xv)
        dgm_ref[...] += jnp.sum(dh1 * (xv * r1), axis=0, keepdims=True)
        gx_ref[...] = dx2_ref[...] + _rms_bwd(xv, r1, dh1 * gm_ref[...])

    blk = lambda c: pl.BlockSpec((tm, c), lambda i: (nb - 1 - i, 0))
    return pl.pallas_call(
        body, name="bwd_in", grid=(nb,),
        in_specs=[blk(D_MODEL), blk(D_MODEL), blk(3 * CONV_CH), blk(ATTN_W), blk(ATTN_W), blk(ATTN_W), blk(LANES),
                  blk(LANES), blk(LANES), _full(w_zc.shape), _full(w_qkv.shape), _full(w_f.shape),
                  _full((1, D_MODEL))],
        out_specs=[blk(D_MODEL), blk(LANES), _full((1, D_MODEL)), _full((1, LANES))],
        out_shape=[jax.ShapeDtypeStruct((s, D_MODEL), F32), jax.ShapeDtypeStruct((s, LANES), BF),
                   jax.ShapeDtypeStruct((1, D_MODEL), F32), jax.ShapeDtypeStruct((1, LANES), F32)],
        scratch_shapes=[pltpu.VMEM((1, LANES), F32)],
        compiler_params=_params(VMEM_MID),
    )(x, dx2, dzc, dq, dk, dv, dfq, dfk, fpre, w_zc, w_qkv, w_f, g_mix)


def _matmul_tn(a, b, bn, bk, name):
    s, m = a.shape
    n = b.shape[1]

    def body(a_ref, b_ref, o_ref):
        @pl.when(pl.program_id(1) == 0)
        def _():
            o_ref[...] = jnp.zeros_like(o_ref)

        o_ref[...] += _dot_tn(a_ref[...], b_ref[...])

    return pl.pallas_call(
        body, name=name, grid=(n // bn, s // bk),
        in_specs=[pl.BlockSpec((bk, m), lambda jn, k: (k, 0)), pl.BlockSpec((bk, bn), lambda jn, k: (k, jn))],
        out_specs=pl.BlockSpec((m, bn), lambda jn, k: (0, jn)),
        out_shape=jax.ShapeDtypeStruct((m, n), F32),
        compiler_params=_params(VMEM_MID),
    )(a, b)


def _flip(v, bit):
    return 1 - v if bit else v


def _all_gather(shards):
    n = len(shards)

    def body(*refs):
        src, out = refs[:n], refs[n:2 * n]
        send_sems, recv_sems, loc_sems = refs[2 * n:]
        x, y, c = lax.axis_index("x"), lax.axis_index("y"), lax.axis_index("c")
        me, sibling = (x, y, c), (x, y, 1 - c)
        chips = [(1 - x, y), (x, 1 - y), (1 - x, 1 - y)]

        def slot(a, px, py, pc):
            return out[a].at[4 * px + 2 * py + pc]

        def copy(a, k, block, to, from_src=False):
            return pltpu.make_async_remote_copy(
                src_ref=src[a] if from_src else slot(a, *block), dst_ref=slot(a, *block),
                send_sem=send_sems.at[7 * a + k], recv_sem=recv_sems.at[7 * a + k],
                device_id=to, device_id_type=MESH)

        started = []
        mine = []
        for a in range(n):
            cp = pltpu.make_async_copy(src[a], slot(a, *me), loc_sems.at[a])
            cp.start()
            mine.append(cp)
            first = [copy(a, 0, me, sibling, True)]
            first += [copy(a, 1 + j, me, (*chip, c), True) for j, chip in enumerate(chips)]
            for cp in first:
                cp.start()
            started += first
        for a in range(n):
            for j, chip in enumerate(chips):
                copy(a, 1 + j, (*chip, c), me).wait_recv()
                fwd = copy(a, 4 + j, (*chip, c), sibling)
                fwd.start()
                started.append(fwd)
        for a in range(n):
            copy(a, 0, sibling, me).wait_recv()
            for j, chip in enumerate(chips):
                copy(a, 4 + j, (*chip, 1 - c), me).wait_recv()
        for cp in started:
            cp.wait_send()
        for cp in mine:
            cp.wait()

    any_ = pl.BlockSpec(memory_space=pl.ANY)
    return pl.pallas_call(
        body, name="all_gather_weights",
        in_specs=[any_] * n, out_specs=[any_] * n,
        out_shape=[jax.ShapeDtypeStruct((8,) + sh.shape, sh.dtype) for sh in shards],
        scratch_shapes=[pltpu.SemaphoreType.DMA((7 * n,)), pltpu.SemaphoreType.DMA((7 * n,)),
                        pltpu.SemaphoreType.DMA((n,))],
    )(*shards)


def _all_to_all(blocks, name):
    n = len(blocks)
    masks = [((k >> 2) & 1, (k >> 1) & 1, k & 1) for k in range(1, 8)]

    def body(*refs):
        src, out = refs[:n], refs[n:2 * n]
        send_sems, recv_sems, loc_sems = refs[2 * n:]
        x, y, c = lax.axis_index("x"), lax.axis_index("y"), lax.axis_index("c")
        me = 4 * x + 2 * y + c
        started = []
        for a in range(n):
            cp = pltpu.make_async_copy(src[a].at[me], out[a].at[me], loc_sems.at[a])
            cp.start()
            started.append(cp)
            for k, (mx, my, mc) in enumerate(masks):
                px, py, pc = _flip(x, mx), _flip(y, my), _flip(c, mc)
                cp = pltpu.make_async_remote_copy(
                    src_ref=src[a].at[4 * px + 2 * py + pc], dst_ref=out[a].at[me],
                    send_sem=send_sems.at[7 * a + k], recv_sem=recv_sems.at[7 * a + k],
                    device_id=(px, py, pc), device_id_type=MESH)
                cp.start()
                started.append(cp)
        for cp in started:
            cp.wait()

    any_ = pl.BlockSpec(memory_space=pl.ANY)
    return pl.pallas_call(
        body, name=name,
        in_specs=[any_] * n, out_specs=[any_] * n,
        out_shape=[jax.ShapeDtypeStruct(b.shape, b.dtype) for b in blocks],
        scratch_shapes=[pltpu.SemaphoreType.DMA((7 * n,)), pltpu.SemaphoreType.DMA((7 * n,)),
                        pltpu.SemaphoreType.DMA((n,))],
    )(*blocks)


def _adamw(parts, w, m, v, br, name):
    g8, r, c = parts.shape
    c1 = 1.0 - ADAM_B1 ** ADAM_STEP
    c2 = 1.0 - ADAM_B2 ** ADAM_STEP

    def body(p_ref, w_ref, m_ref, v_ref, g_ref, d_ref, m2_ref, v2_ref):
        g = p_ref[0]
        for d in range(1, g8):
            g = g + p_ref[d]
        m2 = ADAM_B1 * m_ref[...] + (1.0 - ADAM_B1) * g
        v2 = ADAM_B2 * v_ref[...] + (1.0 - ADAM_B2) * (g * g)
        g_ref[...] = g
        m2_ref[...] = m2
        v2_ref[...] = v2
        d_ref[...] = -ADAM_LR * ((m2 / c1) / (jnp.sqrt(v2 / c2) + ADAM_EPS) + ADAM_WD * w_ref[...])

    blk = pl.BlockSpec((br, c), lambda i: (i, 0))
    out = jax.ShapeDtypeStruct((r, c), F32)
    return pl.pallas_call(
        body, name=name, grid=(r // br,),
        in_specs=[pl.BlockSpec((g8, br, c), lambda i: (0, i, 0)), blk, blk, blk],
        out_specs=[blk] * 4, out_shape=[out] * 4,
        compiler_params=_params(VMEM_MID),
    )(parts, w, m, v)


def _pad_rows(a, rows):
    return jnp.pad(a, ((0, rows - a.shape[0]), (0, 0)))


def _local_step(x, tgt, g_mix, w_in, b_f, w_conv, g_conv_out, g_attn_out, w_o, g_ffn, w_up, w_ffn_conv, w_dn,
                g_final, tm=512, t=512, tf=256):
    assert tm == t and x.shape[0] % tm == 0 and x.shape[0] % tf == 0
    w_zc = w_in[:, :3 * CONV_CH]
    w_qkv = w_in[:, 3 * CONV_CH:3 * CONV_CH + 3 * ATTN_W]
    w_f = jnp.pad(w_in[:, 3 * CONV_CH + 3 * ATTN_W:], ((0, 0), (0, LANES - N_HEADS)))
    b_f_p = jnp.pad(b_f, ((0, 0), (0, LANES - N_HEADS)))
    w_conv_p = _pad_rows(w_conv, 8)
    w_fc_p = _pad_rows(w_ffn_conv, 8)
    g_final2 = g_final.reshape(1, D_MODEL)

    h1, zc, qkv, fpre, fcol, frow, nc, qa, ka, vt = _fwd_in(
        x, g_mix, w_zc, w_qkv, w_f, b_f_p, w_conv_p, g_conv_out, tm)
    o, lcol = _attn_fwd(qa, ka, vt, t)
    x2, h2, mix = _fwd_out(x, nc, o, g_attn_out, w_o, g_ffn, tm)
    u0, act, dx3, dx3b, loss, dg_final = _fwd_ffn(h2, x2, tgt, w_up, w_fc_p, w_dn, g_final2, tf)

    du0, dx2, dx2b, dw_fc, dg_ffn = _bwd_ffn(dx3, dx3b, u0, x2, w_up, w_fc_p, w_dn, g_ffn, tf)
    dzc, dob, dcol, ccol, drow, crow, dw_conv, dg_attn, dg_conv = _bwd_out(
        dx2b, o, zc, fcol, lcol, w_o, g_attn_out, g_conv_out, w_conv_p, t)
    dq, dfq = _attn_bwd_dq(qkv, dob, ccol, dcol, frow, t)
    dk, dv, dfk = _attn_bwd_dkv(qkv, dob, crow, drow, fcol, t)
    gx, dfb, dg_mix, db_f = _bwd_in(x, dx2, dzc, dq, dk, dv, dfq, dfk, fpre, w_zc, w_qkv, w_f, g_mix, tm)

    bk = 1024 if x.shape[0] % 1024 == 0 else 512
    dw_up = _matmul_tn(h2, du0, 1408, bk, "dw_up")
    dw_dn = _matmul_tn(act, dx3b, 512, bk, "dw_down")
    dw_o = _matmul_tn(mix, dx2b, 512, bk, "dw_o")
    dw_zc = _matmul_tn(h1, dzc, 3 * CONV_CH, bk, "dw_in_conv")
    dw_q = _matmul_tn(h1, dq, ATTN_W, bk, "dw_in_q")
    dw_k = _matmul_tn(h1, dk, ATTN_W, bk, "dw_in_k")
    dw_v = _matmul_tn(h1, dv, ATTN_W, bk, "dw_in_v")
    dw_f = _matmul_tn(h1, dfb, LANES, bk, "dw_in_f")
    dw_in = jnp.concatenate([dw_zc, dw_q, dw_k, dw_v, dw_f[:, :N_HEADS]], axis=1)

    grads = dict(g_mix=dg_mix, w_in=dw_in, b_f=db_f[:, :N_HEADS], w_conv=dw_conv[:3], g_conv_out=dg_conv,
                 g_attn_out=dg_attn, w_o=dw_o, g_ffn=dg_ffn, w_up=dw_up, w_ffn_conv=dw_fc[:3], w_down=dw_dn,
                 g_final=dg_final)
    return loss[0, 0], gx, grads


_SMALL = (("g_mix", 1024), ("b_f", 8), ("g_conv_out", 512), ("g_attn_out", 512), ("g_ffn", 1024), ("g_final", 1024))
_SMALL_ROWS = 40


def _pack_small(vals):
    flat = jnp.concatenate([vals[k].reshape(-1) for k, _ in _SMALL])
    return jnp.pad(flat, (0, _SMALL_ROWS * LANES - flat.shape[0])).reshape(_SMALL_ROWS, LANES)


def _unpack_small(packed, shapes):
    flat = packed.reshape(-1)
    out, off = {}, 0
    for k, n in _SMALL:
        out[k] = flat[off:off + n].reshape(shapes[k])
        off += n
    return out


def _col_blocks(a):
    r, c8 = a.shape
    return jnp.transpose(a.reshape(r, 8, c8 // 8), (1, 0, 2))


def _from_col_blocks(a):
    g, r, c = a.shape
    return jnp.transpose(a, (1, 0, 2)).reshape(r, g * c)


def kernel(x, g_mix, w_in, b_f, w_conv, g_conv_out, g_attn_out, w_o, g_ffn, w_up, w_ffn_conv, w_down, g_final, loss_target, m_g_mix, m_w_in, m_b_f, m_w_conv, m_g_conv_out, m_g_attn_out, m_w_o, m_g_ffn, m_w_up, m_w_ffn_conv, m_w_down, m_g_final, v_g_mix, v_w_in, v_b_f, v_w_conv, v_g_conv_out, v_g_attn_out, v_w_o, v_g_ffn, v_w_up, v_w_ffn_conv, v_w_down, v_g_final):
    w = dict(g_mix=g_mix, w_in=w_in[0], b_f=b_f, w_conv=w_conv[0], g_conv_out=g_conv_out, g_attn_out=g_attn_out,
             w_o=w_o[0], g_ffn=g_ffn, w_up=w_up[0], w_ffn_conv=w_ffn_conv[0], w_down=w_down[0], g_final=g_final)
    m = dict(g_mix=m_g_mix, w_in=m_w_in[0], b_f=m_b_f, w_conv=m_w_conv[0], g_conv_out=m_g_conv_out,
             g_attn_out=m_g_attn_out, w_o=m_w_o[0], g_ffn=m_g_ffn, w_up=m_w_up[0], w_ffn_conv=m_w_ffn_conv[0],
             w_down=m_w_down[0], g_final=m_g_final)
    v = dict(g_mix=v_g_mix, w_in=v_w_in[0], b_f=v_b_f, w_conv=v_w_conv[0], g_conv_out=v_g_conv_out,
             g_attn_out=v_g_attn_out, w_o=v_w_o[0], g_ffn=v_g_ffn, w_up=v_w_up[0], w_ffn_conv=v_w_ffn_conv[0],
             w_down=v_w_down[0], g_final=v_g_final)
    shapes = dict(g_mix=g_mix.shape, w_in=w_in.shape, b_f=b_f.shape, w_conv=w_conv.shape,
                  g_conv_out=g_conv_out.shape, g_attn_out=g_attn_out.shape, w_o=w_o.shape, g_ffn=g_ffn.shape,
                  w_up=w_up.shape, w_ffn_conv=w_ffn_conv.shape, w_down=w_down.shape, g_final=g_final.shape)

    gathered = _all_gather([w["w_in"].astype(BF), w["w_o"].astype(BF), w["w_up"].astype(BF), w["w_down"].astype(BF),
                            w["w_conv"], w["w_ffn_conv"]])
    w_in_f = _from_col_blocks(gathered[0])
    w_o_f = gathered[1].reshape(D_MODEL, D_MODEL)
    w_up_f = _from_col_blocks(gathered[2])
    w_dn_f = gathered[3].reshape(D_FF, D_MODEL)
    w_conv_f = _from_col_blocks(gathered[4])
    w_fc_f = _from_col_blocks(gathered[5])

    loss, gx, grads = _local_step(x[0], loss_target[0], g_mix, w_in_f, b_f, w_conv_f, g_conv_out, g_attn_out, w_o_f,
                                  g_ffn, w_up_f, w_fc_f, w_dn_f, g_final)
    loss = lax.psum(loss, ("x", "y", "c"))

    big = ("w_in", "w_o", "w_up", "w_down", "w_conv", "w_ffn_conv")
    send = [_col_blocks(grads["w_in"]), grads["w_o"].reshape(8, D_MODEL // 8, D_MODEL), _col_blocks(grads["w_up"]),
            grads["w_down"].reshape(8, D_FF // 8, D_MODEL), _col_blocks(grads["w_conv"]),
            _col_blocks(grads["w_ffn_conv"]),
            jnp.broadcast_to(_pack_small(grads), (8, _SMALL_ROWS, LANES))]
    recv = _all_to_all(send, "all_to_all_grads")

    rows = dict(w_in=256, w_o=128, w_up=256, w_down=176, w_conv=3, w_ffn_conv=3)
    g_out, d_out, m_out, v_out = {}, {}, {}, {}
    for a, k in enumerate(big):
        res = _adamw(recv[a], w[k], m[k], v[k], rows[k], "adamw_" + k)
        g_out[k], d_out[k], m_out[k], v_out[k] = [r.reshape(shapes[k]) for r in res]
    res = _adamw(recv[6], _pack_small(w), _pack_small(m), _pack_small(v), _SMALL_ROWS, "adamw_gains")
    for dst, packed in zip((g_out, d_out, m_out, v_out), res):
        dst.update(_unpack_small(packed, shapes))

    order = ("g_mix", "w_in", "b_f", "w_conv", "g_conv_out", "g_attn_out", "w_o", "g_ffn", "w_up", "w_ffn_conv",
             "w_down", "g_final")
    return (loss, gx.reshape(x.shape), *[g_out[k] for k in order], *[d_out[k] for k in order],
            *[m_out[k] for k in order], *[v_out[k] for k in order])
```

```python
import functools

import jax
import jax.numpy as jnp
import numpy as np
from jax import lax
from jax.experimental import pallas as pl
from jax.experimental.pallas import tpu as pltpu

F32 = jnp.float32
BF = jnp.bfloat16
HI = lax.Precision.HIGHEST
MESH = pl.DeviceIdType.MESH

D_MODEL = 1024
CONV_CH = 512
ATTN_W = 512
N_HEADS = 8
HEAD_DIM = 64
D_FF = 2816
FF_CHUNK = 256
N_FF_CHUNKS = D_FF // FF_CHUNK
IN_COLS = 3080
EPS = 1e-6
NEG = -1e30
LANES = 128
VMEM_BIG = 56 * 1024 * 1024
VMEM_MID = 40 * 1024 * 1024

ADAM_LR = 0.001
ADAM_B1 = 0.9
ADAM_B2 = 0.999
ADAM_EPS = 1e-08
ADAM_WD = 0.01
ADAM_STEP = 10

NT = (((1,), (1,)), ((), ()))
TN = (((0,), (0,)), ((), ()))


def _dot(a, b):
    return jnp.dot(a, b, preferred_element_type=F32)


def _dot_nt(a, b):
    return lax.dot_general(a, b, NT, preferred_element_type=F32)


def _dot_tn(a, b):
    return lax.dot_general(a, b, TN, preferred_element_type=F32)


def _params(vmem=None, parallel=False):
    return pltpu.CompilerParams(
        dimension_semantics=None, vmem_limit_bytes=vmem)


def _rows(shape):
    return lax.broadcasted_iota(jnp.int32, shape, 0)


def _cols(shape):
    return lax.broadcasted_iota(jnp.int32, shape, 1)


def _shift_down(u, prev, k):
    n = prev.shape[0]
    out = pltpu.roll(u, k, 0)
    row = _rows(u.shape)
    for r in range(k):
        out = jnp.where(row == r, prev[n - k + r:n - k + r + 1, :].astype(u.dtype), out)
    return out


def _shift_up(u, nxt, k):
    tm = u.shape[0]
    out = pltpu.roll(u, tm - k, 0)
    row = _rows(u.shape)
    for r in range(k):
        out = jnp.where(row == tm - k + r, nxt[r:r + 1, :], out)
    return out


def _conv3(u, prev, w):
    u1 = _shift_down(u, prev, 1)
    u2 = _shift_down(u, prev, 2)
    return w[0:1, :] * u2 + w[1:2, :] * u1 + w[2:3, :] * u, u1, u2


def _conv3_bwd(d, nxt, w):
    return w[2:3, :] * d + w[1:2, :] * _shift_up(d, nxt, 1) + w[0:1, :] * _shift_up(d, nxt, 2)


def _rms(x):
    return lax.rsqrt(jnp.mean(x * x, axis=-1, keepdims=True) + EPS)


def _rms_bwd(x, r, dyg):
    return r * dyg - x * (r * r * r) * jnp.mean(dyg * x, axis=-1, keepdims=True)


def _full(shape):
    nd = len(shape)
    return pl.BlockSpec(shape, lambda i, _n=nd: (0,) * _n)


ONES_LANE = 24


def _bias_scatter():
    sc = np.zeros((LANES, 2 * N_HEADS * LANES), np.float32)
    koff = N_HEADS * LANES
    for h in range(N_HEADS):
        aux = HEAD_DIM * (1 - h % 2)
        for j in range(3):
            sc[8 * j + h, LANES * h + aux + j] = 1.0
            sc[ONES_LANE, koff + LANES * h + aux + j] = 1.0
            sc[ONES_LANE, LANES * h + aux + 3 + j] = 1.0
            sc[8 * j + h, koff + LANES * h + aux + 3 + j] = -1.0
    return jnp.asarray(sc, BF)


def _fwd_in(x, g_mix, w_zc, w_qkv, w_f, b_f, w_conv, g_conv_out, tm):
    s = x.shape[0]
    nb = s // tm
    aw = N_HEADS * LANES

    def body(x_ref, gm_ref, wzc_ref, wqkv_ref, wf_ref, bf_ref, wc_ref, gco_ref, sc_ref,
             h1_ref, zc_ref, qkv_ref, fpre_ref, fcol_ref, frow_ref, nc_ref, qa_ref, ka_ref, vt_ref, cu_ref, cf_ref):
        i = pl.program_id(0)

        @pl.when(i == 0)
        def _():
            cu_ref[...] = jnp.zeros_like(cu_ref)
            cf_ref[...] = jnp.zeros_like(cf_ref)

        xv = x_ref[...]
        hb = (xv * _rms(xv) * gm_ref[...]).astype(BF)
        h1_ref[...] = hb
        zc = _dot(hb, wzc_ref[...])
        zc_ref[...] = zc
        qkv = _dot(hb, wqkv_ref[...])
        qkv = jnp.where(_cols(qkv.shape) < ATTN_W, qkv * 0.125, qkv)
        qkvb = qkv.astype(BF)
        qkv_ref[...] = qkvb
        vt_ref[0] = qkv[:, 2 * ATTN_W:].T.astype(BF)

        gb, gc, xc = zc[:, :CONV_CH], zc[:, CONV_CH:2 * CONV_CH], zc[:, 2 * CONV_CH:]
        u = gc * xc
        cv, _, _ = _conv3(u, cu_ref[...], wc_ref[...])
        cu_ref[...] = u[tm - 8:, :]
        y = gb * cv
        nc_ref[...] = (y * _rms(y) * gco_ref[...]).astype(BF)

        fpre = _dot(hb, wf_ref[...]) + bf_ref[...]
        fpre_ref[...] = fpre
        logf = jnp.minimum(fpre, 0.0) - jnp.log1p(jnp.exp(-jnp.abs(fpre)))
        logf = jnp.where(_cols(logf.shape) < N_HEADS, logf, 0.0)
        tri = (_rows((tm, tm)) >= _cols((tm, tm))).astype(F32)
        fcol = jnp.dot(tri, logf, precision=HI, preferred_element_type=F32) + cf_ref[...]
        cf_ref[...] = fcol[tm - 1:tm, :]
        fcol_ref[...] = fcol
        frow_ref[0] = fcol.T[:N_HEADS, :]

        hi = fcol.astype(BF).astype(F32)
        rest = fcol - hi
        mid = rest.astype(BF).astype(F32)
        lo = (rest - mid).astype(BF).astype(F32)
        parts = hi + pltpu.roll(mid, 8, 1) + pltpu.roll(lo, 16, 1)
        parts = jnp.where(_cols(parts.shape) == ONES_LANE, 1.0, parts).astype(BF)
        feat = _dot(parts, sc_ref[...]).astype(BF)
        lane = _cols((tm, LANES))
        for h in range(N_HEADS):
            hp, hh = divmod(h, 2)
            own = (lane >= HEAD_DIM * hh) & (lane < HEAD_DIM * (hh + 1))
            hs = slice(LANES * h, LANES * (h + 1))
            qa_ref[:, hs] = jnp.where(own, qkvb[:, LANES * hp:LANES * (hp + 1)], feat[:, hs])
            ka_ref[:, hs] = jnp.where(own, qkvb[:, ATTN_W + LANES * hp:ATTN_W + LANES * (hp + 1)],
                                      feat[:, aw + LANES * h:aw + LANES * (h + 1)])

    blk = lambda c: pl.BlockSpec((tm, c), lambda i: (i, 0))
    return pl.pallas_call(
        body, name="fwd_in", grid=(nb,),
        in_specs=[blk(D_MODEL), _full((1, D_MODEL)), _full(w_zc.shape), _full(w_qkv.shape), _full(w_f.shape),
                  _full((1, LANES)), _full((8, CONV_CH)), _full((1, CONV_CH)), _full((LANES, 2 * aw))],
        out_specs=[blk(D_MODEL), blk(3 * CONV_CH), blk(3 * ATTN_W), blk(LANES), blk(LANES),
                   pl.BlockSpec((1, N_HEADS, tm), lambda i: (i, 0, 0)), blk(CONV_CH), blk(aw), blk(aw),
                   pl.BlockSpec((1, ATTN_W, tm), lambda i: (i, 0, 0))],
        out_shape=[jax.ShapeDtypeStruct((s, D_MODEL), BF), jax.ShapeDtypeStruct((s, 3 * CONV_CH), F32),
                   jax.ShapeDtypeStruct((s, 3 * ATTN_W), BF), jax.ShapeDtypeStruct((s, LANES), F32),
                   jax.ShapeDtypeStruct((s, LANES), F32), jax.ShapeDtypeStruct((nb, N_HEADS, tm), F32),
                   jax.ShapeDtypeStruct((s, CONV_CH), BF), jax.ShapeDtypeStruct((s, aw), BF),
                   jax.ShapeDtypeStruct((s, aw), BF), jax.ShapeDtypeStruct((nb, ATTN_W, tm), BF)],
        scratch_shapes=[pltpu.VMEM((8, CONV_CH), F32), pltpu.VMEM((1, LANES), F32)],
        compiler_params=_params(VMEM_MID),
    )(x, g_mix, w_zc, w_qkv, w_f, b_f, w_conv, g_conv_out, _bias_scatter())


def _head_mask(shape, hh):
    lane = _cols(shape)
    return (lane >= HEAD_DIM * hh) & (lane < HEAD_DIM * (hh + 1))


def _attn_fwd(qa, ka, vt, t, shards):
    s = qa.shape[0]
    nb = s // t
    n = len(shards)

    def body(qa_ref, ka_any, vt_any, *refs):
        shard_refs, (o_ref, l_ref), gathered = refs[:n], refs[n:n + 2], refs[n + 2:2 * n + 2]
        ka_scr, vt_scr, m_scr, acc_scr, sem = refs[2 * n + 2:2 * n + 7]
        i = pl.program_id(0)
        start, forward, finish = _gather_phases(shard_refs, gathered, *refs[2 * n + 7:])
        pl.when(i == 0)(start)
        pl.when(i == nb // 2)(forward)
        _load_weights(i, [(ka_any, ka_scr), (vt_any, vt_scr)], sem)

        causal_t = _rows((t, t)) <= _cols((t, t))
        row = _rows((LANES, t))
        lrows = jnp.zeros((LANES, t), F32)
        for hp in range(N_HEADS // 2):
            ps = slice(hp * LANES, (hp + 1) * LANES)
            o_pair = None
            for hh in range(2):
                h = 2 * hp + hh
                hs = slice(h * LANES, (h + 1) * LANES)
                aux = HEAD_DIM * (1 - hh)
                own = (row >= HEAD_DIM * hh) & (row < HEAD_DIM * (hh + 1))
                fill = jnp.where(row == aux, 1.0, 0.0).astype(BF)
                qh = qa_ref[:, hs]
                m_scr[...] = jnp.full(m_scr.shape, NEG, F32)
                acc_scr[...] = jnp.zeros(acc_scr.shape, F32)

                def tile(kb, masked, ps=ps, hs=hs, qh=qh, own=own, fill=fill):
                    rs = pl.ds(pl.multiple_of(kb * t, t), t)
                    e = _dot_nt(ka_scr[rs, hs], qh)
                    if masked:
                        e = jnp.where(causal_t, e, NEG)
                    m_old = m_scr[...]
                    m_new = jnp.maximum(m_old, jnp.max(e, axis=0, keepdims=True))
                    p = jnp.exp(e - m_new).astype(BF)
                    vta = jnp.where(own, vt_scr[kb, ps, :], fill)
                    acc_scr[...] = jnp.exp(m_old - m_new) * acc_scr[...] + _dot(vta, p)
                    m_scr[...] = m_new

                def loop_body(kb, carry, tile=tile):
                    tile(kb, False)
                    return carry

                lax.fori_loop(0, i, loop_body, 0)
                tile(i, True)
                acc = acc_scr[...]
                denom = acc[aux:aux + 1, :]
                o_h = acc / denom
                lrows = jnp.where(row == h, m_scr[...] + jnp.log(denom), lrows)
                o_pair = o_h if hh == 0 else jnp.where(row < HEAD_DIM, o_pair, o_h)
            o_ref[:, ps] = o_pair.T
        l_ref[...] = lrows.T
        pl.when(i == nb - 1)(finish)

    any_ = pl.BlockSpec(memory_space=pl.ANY)
    return pl.pallas_call(
        body, name="attn_fwd", grid=(nb,),
        in_specs=[pl.BlockSpec((t, N_HEADS * LANES), lambda i: (i, 0)), any_, any_] + [any_] * n,
        out_specs=[pl.BlockSpec((t, ATTN_W), lambda i: (i, 0)), pl.BlockSpec((t, LANES), lambda i: (i, 0))]
        + [any_] * n,
        out_shape=[jax.ShapeDtypeStruct((s, ATTN_W), F32), jax.ShapeDtypeStruct((s, LANES), F32)]
        + _gathered_shapes(shards),
        scratch_shapes=[pltpu.VMEM(ka.shape, BF), pltpu.VMEM(vt.shape, BF), pltpu.VMEM((1, t), F32),
                        pltpu.VMEM((LANES, t), F32), pltpu.SemaphoreType.DMA((2,))] + _exchange_sems(n),
        compiler_params=_params(VMEM_BIG),
    )(qa, ka, vt, *shards)


def _fwd_out(x, nc, o, g_attn_out, w_o, g_ffn, tm):
    s = x.shape[0]

    def body(x_ref, nc_ref, o_ref, ga_ref, wo_ref, gf_ref, x2_ref, h2_ref, mix_ref):
        ov = o_ref[...]
        na = (ov * _rms(ov) * ga_ref[...]).astype(BF)
        ncv = nc_ref[...]
        mix_ref[:, :CONV_CH] = ncv
        mix_ref[:, CONV_CH:] = na
        x2 = x_ref[...] + _dot(ncv, wo_ref[:CONV_CH, :]) + _dot(na, wo_ref[CONV_CH:, :])
        x2_ref[...] = x2
        h2_ref[...] = (x2 * _rms(x2) * gf_ref[...]).astype(BF)

    blk = lambda c: pl.BlockSpec((tm, c), lambda i: (i, 0))
    return pl.pallas_call(
        body, name="fwd_out", grid=(s // tm,),
        in_specs=[blk(D_MODEL), blk(CONV_CH), blk(ATTN_W), _full((1, ATTN_W)), _full(w_o.shape), _full((1, D_MODEL))],
        out_specs=[blk(D_MODEL), blk(D_MODEL), blk(D_MODEL)],
        out_shape=[jax.ShapeDtypeStruct((s, D_MODEL), F32), jax.ShapeDtypeStruct((s, D_MODEL), BF),
                   jax.ShapeDtypeStruct((s, D_MODEL), BF)],
        compiler_params=_params(VMEM_MID),
    )(x, nc, o, g_attn_out, w_o, g_ffn)


def _load_weights(i, pairs, sem):
    @pl.when(i == 0)
    def _():
        cps = [pltpu.make_async_copy(src, dst, sem.at[n]) for n, (src, dst) in enumerate(pairs)]
        for cp in cps:
            cp.start()
        for cp in cps:
            cp.wait()


def _fwd_ffn(h2, x2, tgt, w_up, w_ffn_conv, w_dn, g_final, tm):
    s = x2.shape[0]

    def body(h2_ref, x2_ref, tgt_ref, wfc_ref, gfin_ref, wup_any, wdn_any,
             u0_ref, act_ref, dx3_ref, dx3b_ref, loss_ref, dgfin_ref,
             wup, wdn, carry, acc, sem):
        i = pl.program_id(0)
        _load_weights(i, [(wup_any, wup), (wdn_any, wdn)], sem)

        @pl.when(i == 0)
        def _():
            carry[...] = jnp.zeros_like(carry)
            loss_ref[...] = jnp.zeros_like(loss_ref)
            dgfin_ref[...] = jnp.zeros_like(dgfin_ref)

        hb = h2_ref[...]
        acc[...] = jnp.zeros_like(acc)
        for j in range(N_FF_CHUNKS):
            ca = slice(j * FF_CHUNK, (j + 1) * FF_CHUNK)
            cg = slice(D_FF + j * FF_CHUNK, D_FF + (j + 1) * FF_CHUNK)
            parts = []
            for cc in (ca, cg):
                u0 = _dot(hb, wup[:, cc])
                u0_ref[:, cc] = u0.astype(BF)
                uu, _, _ = _conv3(u0, carry[:, cc], wfc_ref[:, cc])
                carry[:, cc] = u0[tm - 8:, :]
                parts.append(uu)
            ua, ug = parts
            act = (ug * jax.nn.sigmoid(ug) * ua).astype(BF)
            act_ref[:, ca] = act
            acc[...] += _dot(act, wdn[ca, :])

        x3 = x2_ref[...] + acc[...]
        r3 = _rms(x3)
        gfin = gfin_ref[...]
        xn = x3 * r3
        diff = xn * gfin - tgt_ref[...]
        loss_ref[...] += jnp.sum(jnp.sum(diff * diff, axis=-1, keepdims=True), axis=0, keepdims=True) * (0.5 / D_MODEL)
        dy = diff * (1.0 / D_MODEL)
        dgfin_ref[...] += jnp.sum(dy * xn, axis=0, keepdims=True)
        dx3 = _rms_bwd(x3, r3, dy * gfin)
        dx3_ref[...] = dx3
        dx3b_ref[...] = dx3.astype(BF)

    blk = lambda c: pl.BlockSpec((tm, c), lambda i: (i, 0))
    any_ = pl.BlockSpec(memory_space=pl.ANY)
    return pl.pallas_call(
        body, name="fwd_ffn", grid=(s // tm,),
        in_specs=[blk(D_MODEL), blk(D_MODEL), blk(D_MODEL), _full((8, 2 * D_FF)), _full((1, D_MODEL)), any_, any_],
        out_specs=[blk(2 * D_FF), blk(D_FF), blk(D_MODEL), blk(D_MODEL), _full((1, 1)), _full((1, D_MODEL))],
        out_shape=[jax.ShapeDtypeStruct((s, 2 * D_FF), BF), jax.ShapeDtypeStruct((s, D_FF), BF),
                   jax.ShapeDtypeStruct((s, D_MODEL), F32), jax.ShapeDtypeStruct((s, D_MODEL), BF),
                   jax.ShapeDtypeStruct((1, 1), F32), jax.ShapeDtypeStruct((1, D_MODEL), F32)],
        scratch_shapes=[pltpu.VMEM(w_up.shape, BF), pltpu.VMEM(w_dn.shape, BF), pltpu.VMEM((8, 2 * D_FF), F32),
                        pltpu.VMEM((tm, D_MODEL), F32), pltpu.SemaphoreType.DMA((2,))],
        compiler_params=_params(VMEM_BIG),
    )(h2, x2, tgt, w_ffn_conv, g_final, w_up, w_dn)


def _bwd_ffn(dx3, dx3b, u0, x2, w_up, w_ffn_conv, w_dn, g_ffn, tm):
    s = x2.shape[0]
    nb = s // tm
    hr = 16

    def body(dx3_ref, dx3b_ref, u0_ref, halo_ref, x2_ref, wfc_ref, gf_ref, wup_any, wdn_any,
             du0_ref, dx2_ref, dx2b_ref, dwfc_ref, dgf_ref,
             wup, wdn, carry, acc, sem):
        i = pl.program_id(0)
        rb = nb - 1 - i
        _load_weights(i, [(wup_any, wup), (wdn_any, wdn)], sem)

        @pl.when(i == 0)
        def _():
            carry[...] = jnp.zeros_like(carry)
            dwfc_ref[...] = jnp.zeros_like(dwfc_ref)
            dgf_ref[...] = jnp.zeros_like(dgf_ref)

        db = dx3b_ref[...]
        live = (rb > 0).astype(F32)
        acc[...] = jnp.zeros_like(acc)
        for j in range(N_FF_CHUNKS):
            ca = slice(j * FF_CHUNK, (j + 1) * FF_CHUNK)
            cg = slice(D_FF + j * FF_CHUNK, D_FF + (j + 1) * FF_CHUNK)
            fw = []
            for cc in (ca, cg):
                u0 = u0_ref[:, cc].astype(F32)
                prev = halo_ref[:, cc].astype(F32) * live
                fw.append((u0,) + _conv3(u0, prev, wfc_ref[:, cc]))
            (u0a, ua, u1a, u2a), (u0g, ug, u1g, u2g) = fw
            sg = jax.nn.sigmoid(ug)
            dact = _dot_nt(db, wdn[ca, :])
            da = dact * (ug * sg)
            dg = dact * ua * (sg * (1.0 + ug * (1.0 - sg)))
            for cc, d, u0c, u1c, u2c in ((ca, da, u0a, u1a, u2a), (cg, dg, u0g, u1g, u2g)):
                dwfc_ref[0:1, cc] += jnp.sum(d * u2c, axis=0, keepdims=True)
                dwfc_ref[1:2, cc] += jnp.sum(d * u1c, axis=0, keepdims=True)
                dwfc_ref[2:3, cc] += jnp.sum(d * u0c, axis=0, keepdims=True)
                du0 = _conv3_bwd(d, carry[:, cc], wfc_ref[:, cc]).astype(BF)
                carry[:, cc] = d[:8, :]
                du0_ref[:, cc] = du0
                acc[...] += _dot_nt(du0, wup[:, cc])

        x2v = x2_ref[...]
        r2 = _rms(x2v)
        dh2 = acc[...]
        dgf_ref[...] += jnp.sum(dh2 * (x2v * r2), axis=0, keepdims=True)
        dx2 = dx3_ref[...] + _rms_bwd(x2v, r2, dh2 * gf_ref[...])
        dx2_ref[...] = dx2
        dx2b_ref[...] = dx2.astype(BF)

    blk = lambda c: pl.BlockSpec((tm, c), lambda i: (nb - 1 - i, 0))
    halo = pl.BlockSpec((hr, 2 * D_FF), lambda i: (jnp.maximum((nb - 1 - i) * (tm // hr) - 1, 0), 0))
    any_ = pl.BlockSpec(memory_space=pl.ANY)
    return pl.pallas_call(
        body, name="bwd_ffn", grid=(nb,),
        in_specs=[blk(D_MODEL), blk(D_MODEL), blk(2 * D_FF), halo, blk(D_MODEL), _full((8, 2 * D_FF)),
                  _full((1, D_MODEL)), any_, any_],
        out_specs=[blk(2 * D_FF), blk(D_MODEL), blk(D_MODEL), _full((8, 2 * D_FF)), _full((1, D_MODEL))],
        out_shape=[jax.ShapeDtypeStruct((s, 2 * D_FF), BF), jax.ShapeDtypeStruct((s, D_MODEL), F32),
                   jax.ShapeDtypeStruct((s, D_MODEL), BF), jax.ShapeDtypeStruct((8, 2 * D_FF), F32),
                   jax.ShapeDtypeStruct((1, D_MODEL), F32)],
        scratch_shapes=[pltpu.VMEM(w_up.shape, BF), pltpu.VMEM(w_dn.shape, BF), pltpu.VMEM((8, 2 * D_FF), F32),
                        pltpu.VMEM((tm, D_MODEL), F32), pltpu.SemaphoreType.DMA((2,))],
        compiler_params=_params(VMEM_BIG),
    )(dx3, dx3b, u0, u0, x2, w_ffn_conv, g_ffn, w_up, w_dn)


def _bwd_out(dx2b, o, zc, fcol, lcol, w_o, g_attn_out, g_conv_out, w_conv, tm):
    s = o.shape[0]
    nb = s // tm

    def body(dx2b_ref, o_ref, zc_ref, halo_ref, fcol_ref, lcol_ref, wo_ref, ga_ref, gco_ref, wc_ref,
             dzc_ref, dob_ref, dcol_ref, ccol_ref, drow_ref, crow_ref, dwc_ref, dga_ref, dgco_ref, carry):
        i = pl.program_id(0)
        rb = nb - 1 - i

        @pl.when(i == 0)
        def _():
            carry[...] = jnp.zeros_like(carry)
            dwc_ref[...] = jnp.zeros_like(dwc_ref)
            dga_ref[...] = jnp.zeros_like(dga_ref)
            dgco_ref[...] = jnp.zeros_like(dgco_ref)

        dmix = _dot_nt(dx2b_ref[...], wo_ref[...])
        dnc, dna = dmix[:, :CONV_CH], dmix[:, CONV_CH:]

        ov = o_ref[...]
        ra = _rms(ov)
        dga_ref[...] += jnp.sum(dna * (ov * ra), axis=0, keepdims=True)
        do = _rms_bwd(ov, ra, dna * ga_ref[...])
        dob_ref[...] = do.astype(BF)
        sel = (_rows((ATTN_W, LANES)) // HEAD_DIM == _cols((ATTN_W, LANES))).astype(F32)
        delta = jnp.dot(do * ov, sel, precision=HI, preferred_element_type=F32)
        cc = fcol_ref[...] - lcol_ref[...]
        dcol_ref[...] = delta
        ccol_ref[...] = cc
        drow_ref[0] = delta.T[:N_HEADS, :]
        crow_ref[0] = cc.T[:N_HEADS, :]

        zc_v = zc_ref[...]
        gb, gc, xc = zc_v[:, :CONV_CH], zc_v[:, CONV_CH:2 * CONV_CH], zc_v[:, 2 * CONV_CH:]
        hal = halo_ref[...] * (rb > 0).astype(F32)
        u = gc * xc
        prev = hal[:, CONV_CH:2 * CONV_CH] * hal[:, 2 * CONV_CH:]
        wc = wc_ref[...]
        cv, u1, u2 = _conv3(u, prev, wc)
        y = gb * cv
        rc = _rms(y)
        dgco_ref[...] += jnp.sum(dnc * (y * rc), axis=0, keepdims=True)
        dy = _rms_bwd(y, rc, dnc * gco_ref[...])
        dcv = dy * gb
        dwc_ref[0:1, :] += jnp.sum(dcv * u2, axis=0, keepdims=True)
        dwc_ref[1:2, :] += jnp.sum(dcv * u1, axis=0, keepdims=True)
        dwc_ref[2:3, :] += jnp.sum(dcv * u, axis=0, keepdims=True)
        du = _conv3_bwd(dcv, carry[...], wc)
        carry[...] = dcv[:8, :]
        dzc_ref[:, :CONV_CH] = (dy * cv).astype(BF)
        dzc_ref[:, CONV_CH:2 * CONV_CH] = (du * xc).astype(BF)
        dzc_ref[:, 2 * CONV_CH:] = (du * gc).astype(BF)

    blk = lambda c: pl.BlockSpec((tm, c), lambda i: (nb - 1 - i, 0))
    halo = pl.BlockSpec((8, 3 * CONV_CH), lambda i: (jnp.maximum((nb - 1 - i) * (tm // 8) - 1, 0), 0))
    row = pl.BlockSpec((1, N_HEADS, tm), lambda i: (nb - 1 - i, 0, 0))
    return pl.pallas_call(
        body, name="bwd_out", grid=(nb,),
        in_specs=[blk(D_MODEL), blk(ATTN_W), blk(3 * CONV_CH), halo, blk(LANES), blk(LANES), _full(w_o.shape),
                  _full((1, ATTN_W)), _full((1, CONV_CH)), _full((8, CONV_CH))],
        out_specs=[blk(3 * CONV_CH), blk(ATTN_W), blk(LANES), blk(LANES), row, row,
                   _full((8, CONV_CH)), _full((1, ATTN_W)), _full((1, CONV_CH))],
        out_shape=[jax.ShapeDtypeStruct((s, 3 * CONV_CH), BF), jax.ShapeDtypeStruct((s, ATTN_W), BF),
                   jax.ShapeDtypeStruct((s, LANES), F32), jax.ShapeDtypeStruct((s, LANES), F32),
                   jax.ShapeDtypeStruct((nb, N_HEADS, tm), F32), jax.ShapeDtypeStruct((nb, N_HEADS, tm), F32),
                   jax.ShapeDtypeStruct((8, CONV_CH), F32), jax.ShapeDtypeStruct((1, ATTN_W), F32),
                   jax.ShapeDtypeStruct((1, CONV_CH), F32)],
        scratch_shapes=[pltpu.VMEM((8, CONV_CH), F32)],
        compiler_params=_params(VMEM_MID),
    )(dx2b, o, zc, zc, fcol, lcol, w_o, g_attn_out, g_conv_out, w_conv)


def _attn_bwd_dq(qkv, dob, ccol, dcol, frow, t, blocks):
    s = qkv.shape[0]
    nb = s // t
    n = len(blocks)

    def body(q_ref, do_ref, ccol_ref, dcol_ref, frow_ref, qkv_any, *refs):
        block_refs, (dq_ref, dfq_ref), received = refs[:n], refs[n:n + 2], refs[n + 2:2 * n + 2]
        k_scr, v_scr, acc_scr, sem = refs[2 * n + 2:2 * n + 6]
        i = pl.program_id(0)
        start, finish = _scatter_phases(block_refs, received, *refs[2 * n + 6:])
        pl.when(i == 0)(start)

        @pl.when(i == 0)
        def _():
            ck = pltpu.make_async_copy(qkv_any.at[:, pl.ds(ATTN_W, ATTN_W)], k_scr, sem.at[0])
            cv = pltpu.make_async_copy(qkv_any.at[:, pl.ds(2 * ATTN_W, ATTN_W)], v_scr, sem.at[1])
            ck.start()
            cv.start()
            ck.wait()
            cv.wait()

        causal = _cols((t, t)) <= _rows((t, t))
        lane = _cols((t, LANES))
        dfq = jnp.zeros((t, LANES), F32)
        for hp in range(N_HEADS // 2):
            cs = slice(hp * LANES, (hp + 1) * LANES)
            q2 = q_ref[:, cs]
            do2 = do_ref[:, cs]
            dq_pair = None
            for hh in range(2):
                h = 2 * hp + hh
                hm = _head_mask(q2.shape, hh)
                qh = jnp.where(hm, q2, jnp.zeros_like(q2))
                doh = jnp.where(hm, do2, jnp.zeros_like(do2))
                cq = ccol_ref[:, h:h + 1]
                dl = dcol_ref[:, h:h + 1]
                aux = HEAD_DIM * (1 - hh)
                acc_scr[...] = jnp.zeros(acc_scr.shape, F32)

                def tile(kb, masked, cs=cs, qh=qh, doh=doh, cq=cq, dl=dl, h=h, aux=aux):
                    rs = pl.ds(pl.multiple_of(kb * t, t), t)
                    k2 = k_scr[rs, cs]
                    v2 = v_scr[rs, cs]
                    e = _dot_nt(qh, k2) + (cq - frow_ref[kb, h:h + 1, :])
                    if masked:
                        e = jnp.where(causal, e, NEG)
                    p = jnp.exp(e)
                    ds = p * (_dot_nt(doh, v2) - dl)
                    acc_scr[...] += _dot(ds.astype(BF), jnp.where(lane == aux, jnp.ones_like(k2), k2))

                def loop_body(kb, carry, tile=tile):
                    tile(kb, False)
                    return carry

                lax.fori_loop(0, i, loop_body, 0)
                tile(i, True)
                aq = acc_scr[...]
                dfq = jnp.where(lane == h, aq[:, aux:aux + 1], dfq)
                dq_h = aq * 0.125
                dq_pair = dq_h if hh == 0 else jnp.where(lane < HEAD_DIM, dq_pair, dq_h)
            dq_ref[:, cs] = dq_pair.astype(BF)
        dfq_ref[...] = dfq
        pl.when(i == nb - 1)(finish)

    any_ = pl.BlockSpec(memory_space=pl.ANY)
    return pl.pallas_call(
        body, name="attn_bwd_dq", grid=(nb,),
        in_specs=[pl.BlockSpec((t, ATTN_W), lambda i: (i, 0)), pl.BlockSpec((t, ATTN_W), lambda i: (i, 0)),
                  pl.BlockSpec((t, LANES), lambda i: (i, 0)), pl.BlockSpec((t, LANES), lambda i: (i, 0)),
                  _full(frow.shape), any_] + [any_] * n,
        out_specs=[pl.BlockSpec((t, ATTN_W), lambda i: (i, 0)), pl.BlockSpec((t, LANES), lambda i: (i, 0))]
        + [any_] * n,
        out_shape=[jax.ShapeDtypeStruct((s, ATTN_W), BF), jax.ShapeDtypeStruct((s, LANES), F32)]
        + [jax.ShapeDtypeStruct(b.shape, b.dtype) for b in blocks],
        scratch_shapes=[pltpu.VMEM((s, ATTN_W), BF), pltpu.VMEM((s, ATTN_W), BF), pltpu.VMEM((t, LANES), F32),
                        pltpu.SemaphoreType.DMA((2,))] + _exchange_sems(n),
        compiler_params=_params(VMEM_MID),
    )(qkv, dob, ccol, dcol, frow, qkv, *blocks)


def _attn_bwd_dkv(qkv, dob, crow, drow, fcol, t):
    s = qkv.shape[0]
    nb = s // t

    def body(k_ref, v_ref, fcol_ref, crow_ref, drow_ref, qkv_any, do_any,
             dk_ref, dv_ref, dfk_ref, q_scr, do_scr, acck, accv, sem):
        j = pl.program_id(0)

        @pl.when(j == 0)
        def _():
            cq = pltpu.make_async_copy(qkv_any.at[:, pl.ds(0, ATTN_W)], q_scr, sem.at[0])
            cd = pltpu.make_async_copy(do_any, do_scr, sem.at[1])
            cq.start()
            cd.start()
            cq.wait()
            cd.wait()

        causal_t = _rows((t, t)) <= _cols((t, t))
        lane = _cols((t, LANES))
        dfk = jnp.zeros((t, LANES), F32)
        for hp in range(N_HEADS // 2):
            cs = slice(hp * LANES, (hp + 1) * LANES)
            k2 = k_ref[:, cs]
            v2 = v_ref[:, cs]
            dk_pair = dv_pair = None
            for hh in range(2):
                h = 2 * hp + hh
                hm = _head_mask(k2.shape, hh)
                kh = jnp.where(hm, k2, jnp.zeros_like(k2))
                vh = jnp.where(hm, v2, jnp.zeros_like(v2))
                fk = fcol_ref[:, h:h + 1]
                aux = HEAD_DIM * (1 - hh)
                acck[...] = jnp.zeros(acck.shape, F32)
                accv[...] = jnp.zeros(accv.shape, F32)

                def tile(qb, masked, cs=cs, kh=kh, vh=vh, fk=fk, hm=hm, aux=aux, h=h):
                    rs = pl.ds(pl.multiple_of(qb * t, t), t)
                    q2 = q_scr[rs, cs]
                    do2 = do_scr[rs, cs]
                    e = _dot_nt(kh, q2) + (crow_ref[qb, h:h + 1, :] - fk)
                    if masked:
                        e = jnp.where(causal_t, e, NEG)
                    p = jnp.exp(e)
                    accv[...] += _dot(p.astype(BF), do2)
                    ds = p * (_dot_nt(vh, do2) - drow_ref[qb, h:h + 1, :])
                    qa = jnp.where(lane == aux, jnp.ones_like(q2), jnp.where(hm, q2, jnp.zeros_like(q2)))
                    acck[...] += _dot(ds.astype(BF), qa)

                def loop_body(qb, carry, tile=tile):
                    tile(qb, False)
                    return carry

                tile(j, True)
                lax.fori_loop(j + 1, nb, loop_body, 0)
                ak = acck[...]
                av = accv[...]
                dfk = jnp.where(lane == h, -ak[:, aux:aux + 1], dfk)
                dk_pair = ak if hh == 0 else jnp.where(lane < HEAD_DIM, dk_pair, ak)
                dv_pair = av if hh == 0 else jnp.where(lane < HEAD_DIM, dv_pair, av)
            dk_ref[:, cs] = dk_pair.astype(BF)
            dv_ref[:, cs] = dv_pair.astype(BF)
        dfk_ref[...] = dfk

    any_ = pl.BlockSpec(memory_space=pl.ANY)
    return pl.pallas_call(
        body, name="attn_bwd_dkv", grid=(nb,),
        in_specs=[pl.BlockSpec((t, ATTN_W), lambda j: (j, 1)), pl.BlockSpec((t, ATTN_W), lambda j: (j, 2)),
                  pl.BlockSpec((t, LANES), lambda j: (j, 0)), _full(crow.shape), _full(drow.shape), any_, any_],
        out_specs=[pl.BlockSpec((t, ATTN_W), lambda j: (j, 0)), pl.BlockSpec((t, ATTN_W), lambda j: (j, 0)),
                   pl.BlockSpec((t, LANES), lambda j: (j, 0))],
        out_shape=[jax.ShapeDtypeStruct((s, ATTN_W), BF), jax.ShapeDtypeStruct((s, ATTN_W), BF),
                   jax.ShapeDtypeStruct((s, LANES), F32)],
        scratch_shapes=[pltpu.VMEM((s, ATTN_W), BF), pltpu.VMEM((s, ATTN_W), BF), pltpu.VMEM((t, LANES), F32),
                        pltpu.VMEM((t, LANES), F32), pltpu.SemaphoreType.DMA((2,))],
        compiler_params=_params(VMEM_MID),
    )(qkv, qkv, fcol, crow, drow, qkv, dob)


def _bwd_in(x, dx2, dzc, dq, dk, dv, dfq, dfk, fpre, w_zc, w_qkv, w_f, g_mix, tm):
    s = x.shape[0]
    nb = s // tm

    def body(x_ref, dx2_ref, dzc_ref, dq_ref, dk_ref, dv_ref, dfq_ref, dfk_ref, fpre_ref, wzc_ref, wqkv_ref, wf_ref,
             gm_ref, gx_ref, dfb_ref, dgm_ref, dbf_ref, carry):
        i = pl.program_id(0)

        @pl.when(i == 0)
        def _():
            carry[...] = jnp.zeros_like(carry)
            dgm_ref[...] = jnp.zeros_like(dgm_ref)
            dbf_ref[...] = jnp.zeros_like(dbf_ref)

        triu = (_rows((tm, tm)) <= _cols((tm, tm))).astype(F32)
        df_cum = dfq_ref[...] + dfk_ref[...]
        dlogf = jnp.dot(triu, df_cum, precision=HI, preferred_element_type=F32) + carry[...]
        carry[...] = dlogf[0:1, :]
        fpre = fpre_ref[...]
        df = jnp.where(_cols(fpre.shape) < N_HEADS, dlogf / (1.0 + jnp.exp(fpre)), 0.0)
        dbf_ref[...] += jnp.sum(df, axis=0, keepdims=True)
        dfb = df.astype(BF)
        dfb_ref[...] = dfb

        dh1 = _dot_nt(dzc_ref[...], wzc_ref[...])
        dh1 += _dot_nt(dq_ref[...], wqkv_ref[:, :ATTN_W])
        dh1 += _dot_nt(dk_ref[...], wqkv_ref[:, ATTN_W:2 * ATTN_W])
        dh1 += _dot_nt(dv_ref[...], wqkv_ref[:, 2 * ATTN_W:])
        dh1 += _dot_nt(dfb, wf_ref[...])
        xv = x_ref[...]
        r1 = _rms(xv)
        dgm_ref[...] += jnp.sum(dh1 * (xv * r1), axis=0, keepdims=True)
        gx_ref[...] = dx2_ref[...] + _rms_bwd(xv, r1, dh1 * gm_ref[...])

    blk = lambda c: pl.BlockSpec((tm, c), lambda i: (nb - 1 - i, 0))
    return pl.pallas_call(
        body, name="bwd_in", grid=(nb,),
        in_specs=[blk(D_MODEL), blk(D_MODEL), blk(3 * CONV_CH), blk(ATTN_W), blk(ATTN_W), blk(ATTN_W), blk(LANES),
                  blk(LANES), blk(LANES), _full(w_zc.shape), _full(w_qkv.shape), _full(w_f.shape),
                  _full((1, D_MODEL))],
        out_specs=[blk(D_MODEL), blk(LANES), _full((1, D_MODEL)), _full((1, LANES))],
        out_shape=[jax.ShapeDtypeStruct((s, D_MODEL), F32), jax.ShapeDtypeStruct((s, LANES), BF),
                   jax.ShapeDtypeStruct((1, D_MODEL), F32), jax.ShapeDtypeStruct((1, LANES), F32)],
        scratch_shapes=[pltpu.VMEM((1, LANES), F32)],
        compiler_params=_params(VMEM_MID),
    )(x, dx2, dzc, dq, dk, dv, dfq, dfk, fpre, w_zc, w_qkv, w_f, g_mix)


def _matmul_tn(a, b, bn, bk, name):
    s, m = a.shape
    n = b.shape[1]

    def body(a_ref, b_ref, o_ref):
        @pl.when(pl.program_id(1) == 0)
        def _():
            o_ref[...] = jnp.zeros_like(o_ref)

        o_ref[...] += _dot_tn(a_ref[...], b_ref[...])

    return pl.pallas_call(
        body, name=name, grid=(n // bn, s // bk),
        in_specs=[pl.BlockSpec((bk, m), lambda jn, k: (k, 0)), pl.BlockSpec((bk, bn), lambda jn, k: (k, jn))],
        out_specs=pl.BlockSpec((m, bn), lambda jn, k: (0, jn)),
        out_shape=jax.ShapeDtypeStruct((m, n), F32),
        compiler_params=_params(VMEM_MID),
    )(a, b)


def _flip(v, bit):
    return 1 - v if bit else v


def _all_gather(shards):
    n = len(shards)

    def body(*refs):
        start, forward, finish = _gather_phases(refs[:n], refs[n:2 * n], *refs[2 * n:])
        start()
        forward()
        finish()

    any_ = pl.BlockSpec(memory_space=pl.ANY)
    return pl.pallas_call(
        body, name="all_gather_weights",
        in_specs=[any_] * n, out_specs=[any_] * n,
        out_shape=_gathered_shapes(shards), scratch_shapes=_exchange_sems(n),
    )(*shards)


def _gathered_shapes(shards):
    return [jax.ShapeDtypeStruct((8,) + sh.shape, sh.dtype) for sh in shards]


def _exchange_sems(n):
    return [pltpu.SemaphoreType.DMA((7 * n,)), pltpu.SemaphoreType.DMA((7 * n,)), pltpu.SemaphoreType.DMA((n,))]


def _gather_phases(src, out, send_sems, recv_sems, loc_sems):
    n = len(src)
    x, y, c = lax.axis_index("x"), lax.axis_index("y"), lax.axis_index("c")
    me, sibling = (x, y, c), (x, y, 1 - c)
    chips = [(1 - x, y), (x, 1 - y), (1 - x, 1 - y)]

    def slot(a, px, py, pc):
        return out[a].at[4 * px + 2 * py + pc]

    def copy(a, k, block, to, from_src=False):
        return pltpu.make_async_remote_copy(
            src_ref=src[a] if from_src else slot(a, *block), dst_ref=slot(a, *block),
            send_sem=send_sems.at[7 * a + k], recv_sem=recv_sems.at[7 * a + k],
            device_id=to, device_id_type=MESH)

    def local(a):
        return pltpu.make_async_copy(src[a], slot(a, *me), loc_sems.at[a])

    def first(a):
        return [copy(a, 0, me, sibling, True)] + [copy(a, 1 + j, me, (*chip, c), True)
                                                  for j, chip in enumerate(chips)]

    def passed(a, j):
        return copy(a, 4 + j, (*chips[j], c), sibling)

    def start():
        for a in range(n):
            local(a).start()
            for cp in first(a):
                cp.start()

    def forward():
        for a in range(n):
            for j, chip in enumerate(chips):
                copy(a, 1 + j, (*chip, c), me).wait_recv()
                passed(a, j).start()

    def finish():
        for a in range(n):
            copy(a, 0, sibling, me).wait_recv()
            for j, chip in enumerate(chips):
                copy(a, 4 + j, (*chip, 1 - c), me).wait_recv()
        for a in range(n):
            for cp in first(a) + [passed(a, j) for j in range(3)]:
                cp.wait_send()
            local(a).wait()

    return start, forward, finish


def _scatter_phases(src, out, send_sems, recv_sems, loc_sems):
    n = len(src)
    masks = [((k >> 2) & 1, (k >> 1) & 1, k & 1) for k in range(1, 8)]
    x, y, c = lax.axis_index("x"), lax.axis_index("y"), lax.axis_index("c")
    me = 4 * x + 2 * y + c

    def copies():
        cps = []
        for a in range(n):
            cps.append(pltpu.make_async_copy(src[a].at[me], out[a].at[me], loc_sems.at[a]))
            for k, (mx, my, mc) in enumerate(masks):
                px, py, pc = _flip(x, mx), _flip(y, my), _flip(c, mc)
                cps.append(pltpu.make_async_remote_copy(
                    src_ref=src[a].at[4 * px + 2 * py + pc], dst_ref=out[a].at[me],
                    send_sem=send_sems.at[7 * a + k], recv_sem=recv_sems.at[7 * a + k],
                    device_id=(px, py, pc), device_id_type=MESH))
        return cps

    def start():
        for cp in copies():
            cp.start()

    def finish():
        for cp in copies():
            cp.wait()

    return start, finish


def _all_to_all(blocks, name):
    n = len(blocks)

    def body(*refs):
        start, finish = _scatter_phases(refs[:n], refs[n:2 * n], *refs[2 * n:])
        start()
        finish()

    any_ = pl.BlockSpec(memory_space=pl.ANY)
    return pl.pallas_call(
        body, name=name,
        in_specs=[any_] * n, out_specs=[any_] * n,
        out_shape=[jax.ShapeDtypeStruct(b.shape, b.dtype) for b in blocks], scratch_shapes=_exchange_sems(n),
    )(*blocks)


def _adamw(parts, w, m, v, br, name):
    g8, r, c = parts.shape
    c1 = 1.0 - ADAM_B1 ** ADAM_STEP
    c2 = 1.0 - ADAM_B2 ** ADAM_STEP

    def body(p_ref, w_ref, m_ref, v_ref, g_ref, d_ref, m2_ref, v2_ref):
        g = p_ref[0]
        for d in range(1, g8):
            g = g + p_ref[d]
        m2 = ADAM_B1 * m_ref[...] + (1.0 - ADAM_B1) * g
        v2 = ADAM_B2 * v_ref[...] + (1.0 - ADAM_B2) * (g * g)
        g_ref[...] = g
        m2_ref[...] = m2
        v2_ref[...] = v2
        d_ref[...] = -ADAM_LR * ((m2 / c1) / (jnp.sqrt(v2 / c2) + ADAM_EPS) + ADAM_WD * w_ref[...])

    blk = pl.BlockSpec((br, c), lambda i: (i, 0))
    out = jax.ShapeDtypeStruct((r, c), F32)
    return pl.pallas_call(
        body, name=name, grid=(r // br,),
        in_specs=[pl.BlockSpec((g8, br, c), lambda i: (0, i, 0)), blk, blk, blk],
        out_specs=[blk] * 4, out_shape=[out] * 4,
        compiler_params=_params(VMEM_MID),
    )(parts, w, m, v)


def _pad_rows(a, rows):
    return jnp.pad(a, ((0, rows - a.shape[0]), (0, 0)))


_SMALL =(("g_mix", 1024), ("b_f", 8), ("g_conv_out", 512), ("g_attn_out", 512), ("g_ffn", 1024), ("g_final", 1024))
_SMALL_ROWS = 40


def _pack_small(vals):
    flat = jnp.concatenate([vals[k].reshape(-1) for k, _ in _SMALL])
    return jnp.pad(flat, (0, _SMALL_ROWS * LANES - flat.shape[0])).reshape(_SMALL_ROWS, LANES)


def _unpack_small(packed, shapes):
    flat = packed.reshape(-1)
    out, off = {}, 0
    for k, n in _SMALL:
        out[k] = flat[off:off + n].reshape(shapes[k])
        off += n
    return out


def _col_blocks(a):
    r, c8 = a.shape
    return jnp.transpose(a.reshape(r, 8, c8 // 8), (1, 0, 2))


def _from_col_blocks(a):
    g, r, c = a.shape
    return jnp.transpose(a, (1, 0, 2)).reshape(r, g * c)


def kernel(x, g_mix, w_in, b_f, w_conv, g_conv_out, g_attn_out, w_o, g_ffn, w_up, w_ffn_conv, w_down, g_final, loss_target, m_g_mix, m_w_in, m_b_f, m_w_conv, m_g_conv_out, m_g_attn_out, m_w_o, m_g_ffn, m_w_up, m_w_ffn_conv, m_w_down, m_g_final, v_g_mix, v_w_in, v_b_f, v_w_conv, v_g_conv_out, v_g_attn_out, v_w_o, v_g_ffn, v_w_up, v_w_ffn_conv, v_w_down, v_g_final):
    w = dict(g_mix=g_mix, w_in=w_in[0], b_f=b_f, w_conv=w_conv[0], g_conv_out=g_conv_out, g_attn_out=g_attn_out,
             w_o=w_o[0], g_ffn=g_ffn, w_up=w_up[0], w_ffn_conv=w_ffn_conv[0], w_down=w_down[0], g_final=g_final)
    m = dict(g_mix=m_g_mix, w_in=m_w_in[0], b_f=m_b_f, w_conv=m_w_conv[0], g_conv_out=m_g_conv_out,
             g_attn_out=m_g_attn_out, w_o=m_w_o[0], g_ffn=m_g_ffn, w_up=m_w_up[0], w_ffn_conv=m_w_ffn_conv[0],
             w_down=m_w_down[0], g_final=m_g_final)
    v = dict(g_mix=v_g_mix, w_in=v_w_in[0], b_f=v_b_f, w_conv=v_w_conv[0], g_conv_out=v_g_conv_out,
             g_attn_out=v_g_attn_out, w_o=v_w_o[0], g_ffn=v_g_ffn, w_up=v_w_up[0], w_ffn_conv=v_w_ffn_conv[0],
             w_down=v_w_down[0], g_final=v_g_final)
    shapes = dict(g_mix=g_mix.shape, w_in=w_in.shape, b_f=b_f.shape, w_conv=w_conv.shape,
                  g_conv_out=g_conv_out.shape, g_attn_out=g_attn_out.shape, w_o=w_o.shape, g_ffn=g_ffn.shape,
                  w_up=w_up.shape, w_ffn_conv=w_ffn_conv.shape, w_down=w_down.shape, g_final=g_final.shape)

    tm = t = 512
    tf = 256
    xs, tgt = x[0], loss_target[0]
    seq = xs.shape[0]
    assert seq % tm == 0 and seq % tf == 0
    bk = 1024 if seq % 1024 == 0 else 512

    g_in, g_conv = _all_gather([w["w_in"].astype(BF), w["w_conv"]])
    w_in_f = _from_col_blocks(g_in)
    w_zc = w_in_f[:, :3 * CONV_CH]
    w_qkv = w_in_f[:, 3 * CONV_CH:3 * CONV_CH + 3 * ATTN_W]
    w_f = jnp.pad(w_in_f[:, 3 * CONV_CH + 3 * ATTN_W:], ((0, 0), (0, LANES - N_HEADS)))
    b_f_p = jnp.pad(b_f, ((0, 0), (0, LANES - N_HEADS)))
    w_conv_p = _pad_rows(_from_col_blocks(g_conv), 8)

    h1, zc, qkv, fpre, fcol, frow, nc, qa, ka, vt = _fwd_in(
        xs, g_mix, w_zc, w_qkv, w_f, b_f_p, w_conv_p, g_conv_out, tm)
    o, lcol, g_o, g_up, g_dn, g_fc = _attn_fwd(
        qa, ka, vt, t, [w["w_o"].astype(BF), w["w_up"].astype(BF), w["w_down"].astype(BF), w["w_ffn_conv"]])
    w_o_f = g_o.reshape(D_MODEL, D_MODEL)
    w_up_f = _from_col_blocks(g_up)
    w_dn_f = g_dn.reshape(D_FF, D_MODEL)
    w_fc_p = _pad_rows(_from_col_blocks(g_fc), 8)
    x2, h2, mix = _fwd_out(xs, nc, o, g_attn_out, w_o_f, g_ffn, tm)
    u0, act, dx3, dx3b, loss, dg_final = _fwd_ffn(h2, x2, tgt, w_up_f, w_fc_p, w_dn_f, g_final.reshape(1, D_MODEL), tf)
    loss = lax.psum(loss[0, 0], ("x", "y", "c"))

    du0, dx2, dx2b, dw_fc, dg_ffn = _bwd_ffn(dx3, dx3b, u0, x2, w_up_f, w_fc_p, w_dn_f, g_ffn, tf)
    dw_up = _matmul_tn(h2, du0, 1408, bk, "dw_up")
    dw_dn = _matmul_tn(act, dx3b, 512, bk, "dw_down")
    dw_o = _matmul_tn(mix, dx2b, 512, bk, "dw_o")
    dzc, dob, dcol, ccol, drow, crow, dw_conv, dg_attn, dg_conv = _bwd_out(
        dx2b, o, zc, fcol, lcol, w_o_f, g_attn_out, g_conv_out, w_conv_p, t)
    dq, dfq, r_up, r_dn, r_fc, r_o = _attn_bwd_dq(
        qkv, dob, ccol, dcol, frow, t,
        [_col_blocks(dw_up), dw_dn.reshape(8, D_FF // 8, D_MODEL), _col_blocks(dw_fc[:3]),
         dw_o.reshape(8, D_MODEL // 8, D_MODEL)])
    dk, dv, dfk = _attn_bwd_dkv(qkv, dob, crow, drow, fcol, t)
    gx, dfb, dg_mix, db_f = _bwd_in(xs, dx2, dzc, dq, dk, dv, dfq, dfk, fpre, w_zc, w_qkv, w_f, g_mix, tm)
    dw_zc = _matmul_tn(h1, dzc, 3 * CONV_CH, bk, "dw_in_conv")
    dw_q = _matmul_tn(h1, dq, ATTN_W, bk, "dw_in_q")
    dw_k = _matmul_tn(h1, dk, ATTN_W, bk, "dw_in_k")
    dw_v = _matmul_tn(h1, dv, ATTN_W, bk, "dw_in_v")
    dw_f = _matmul_tn(h1, dfb, LANES, bk, "dw_in_f")
    dw_in = jnp.concatenate([dw_zc, dw_q, dw_k, dw_v, dw_f[:, :N_HEADS]], axis=1)
    small = dict(g_mix=dg_mix, b_f=db_f[:, :N_HEADS], g_conv_out=dg_conv, g_attn_out=dg_attn, g_ffn=dg_ffn,
                 g_final=dg_final)
    r_in, r_conv, r_small = _all_to_all(
        [_col_blocks(dw_in), _col_blocks(dw_conv[:3]), jnp.broadcast_to(_pack_small(small), (8, _SMALL_ROWS, LANES))],
        "all_to_all_grads")
    big = ("w_in", "w_o", "w_up", "w_down", "w_conv", "w_ffn_conv")
    recv = [r_in, r_o, r_up, r_dn, r_conv, r_fc, r_small]

    rows = dict(w_in=256, w_o=128, w_up=256, w_down=176, w_conv=3, w_ffn_conv=3)
    g_out, d_out, m_out, v_out = {}, {}, {}, {}
    for a, k in enumerate(big):
        res = _adamw(recv[a], w[k], m[k], v[k], rows[k], "adamw_" + k)
        g_out[k], d_out[k], m_out[k], v_out[k] = [r.reshape(shapes[k]) for r in res]
    res = _adamw(recv[6], _pack_small(w), _pack_small(m), _pack_small(v), _SMALL_ROWS, "adamw_gains")
    for dst, packed in zip((g_out, d_out, m_out, v_out), res):
        dst.update(_unpack_small(packed, shapes))

    order = ("g_mix", "w_in", "b_f", "w_conv", "g_conv_out", "g_attn_out", "w_o", "g_ffn", "w_up", "w_ffn_conv",
             "w_down", "g_final")
    return (loss, gx.reshape(x.shape), *[g_out[k] for k in order], *[d_out[k] for k in order],
            *[m_out[k] for k in order], *[v_out[k] for k in order])
```

```python
import functools

import jax
import jax.numpy as jnp
import numpy as np
from jax import lax
from jax.experimental import pallas as pl
from jax.experimental.pallas import tpu as pltpu

F32 = jnp.float32
BF = jnp.bfloat16
HI = lax.Precision.HIGHEST
MESH = pl.DeviceIdType.MESH

D_MODEL = 1024
CONV_CH = 512
ATTN_W = 512
N_HEADS = 8
HEAD_DIM = 64
D_FF = 2816
FF_CHUNK = 256
N_FF_CHUNKS = D_FF // FF_CHUNK
IN_COLS = 3080
EPS = 1e-6
NEG = -1e30
LANES = 128
VMEM_BIG = 56 * 1024 * 1024
VMEM_MID = 40 * 1024 * 1024

ADAM_LR = 0.001
ADAM_B1 = 0.9
ADAM_B2 = 0.999
ADAM_EPS = 1e-08
ADAM_WD = 0.01
ADAM_STEP = 10

NT = (((1,), (1,)), ((), ()))
TN = (((0,), (0,)), ((), ()))


def _dot(a, b):
    return jnp.dot(a, b, preferred_element_type=F32)


def _dot_nt(a, b):
    return lax.dot_general(a, b, NT, preferred_element_type=F32)


def _dot_tn(a, b):
    return lax.dot_general(a, b, TN, preferred_element_type=F32)


def _params(vmem=None, parallel=False):
    return pltpu.CompilerParams(
        dimension_semantics=None, vmem_limit_bytes=vmem)


def _rows(shape):
    return lax.broadcasted_iota(jnp.int32, shape, 0)


def _cols(shape):
    return lax.broadcasted_iota(jnp.int32, shape, 1)


def _shift_down(u, prev, k):
    n = prev.shape[0]
    out = pltpu.roll(u, k, 0)
    row = _rows(u.shape)
    for r in range(k):
        out = jnp.where(row == r, prev[n - k + r:n - k + r + 1, :].astype(u.dtype), out)
    return out


def _shift_up(u, nxt, k):
    tm = u.shape[0]
    out = pltpu.roll(u, tm - k, 0)
    row = _rows(u.shape)
    for r in range(k):
        out = jnp.where(row == tm - k + r, nxt[r:r + 1, :], out)
    return out


def _conv3(u, prev, w):
    u1 = _shift_down(u, prev, 1)
    u2 = _shift_down(u, prev, 2)
    return w[0:1, :] * u2 + w[1:2, :] * u1 + w[2:3, :] * u, u1, u2


def _conv3_bwd(d, nxt, w):
    return w[2:3, :] * d + w[1:2, :] * _shift_up(d, nxt, 1) + w[0:1, :] * _shift_up(d, nxt, 2)


def _rms(x):
    return lax.rsqrt(jnp.mean(x * x, axis=-1, keepdims=True) + EPS)


def _rms_bwd(x, r, dyg):
    return r * dyg - x * (r * r * r) * jnp.mean(dyg * x, axis=-1, keepdims=True)


def _full(shape):
    nd = len(shape)
    return pl.BlockSpec(shape, lambda i, _n=nd: (0,) * _n)


ONES_LANE = 24


def _bias_scatter():
    sc = np.zeros((LANES, 2 * N_HEADS * LANES), np.float32)
    koff = N_HEADS * LANES
    for h in range(N_HEADS):
        aux = HEAD_DIM * (1 - h % 2)
        for j in range(3):
            sc[8 * j + h, LANES * h + aux + j] = 1.0
            sc[ONES_LANE, koff + LANES * h + aux + j] = 1.0
            sc[ONES_LANE, LANES * h + aux + 3 + j] = 1.0
            sc[8 * j + h, koff + LANES * h + aux + 3 + j] = -1.0
    return jnp.asarray(sc, BF)


def _split_parts(v, one):
    hi = v.astype(BF).astype(F32)
    rest = v - hi
    mid = rest.astype(BF).astype(F32)
    lo = (rest - mid).astype(BF).astype(F32)
    parts = hi + pltpu.roll(mid, 8, 1) + pltpu.roll(lo, 16, 1)
    return jnp.where(_cols(parts.shape) == ONES_LANE, one, parts).astype(BF)


def _fwd_in(x, g_mix, w_zc, w_qkv, w_f, b_f, w_conv, g_conv_out, tm):
    s = x.shape[0]
    nb = s // tm
    aw = N_HEADS * LANES

    def body(x_ref, gm_ref, wzc_ref, wqkv_ref, wf_ref, bf_ref, wc_ref, gco_ref, sc_ref,
             h1_ref, zc_ref, qkv_ref, fpre_ref, fcol_ref, nc_ref, qa_ref, ka_ref, qt_ref, kt_ref, vt_ref,
             cu_ref, cf_ref):
        i = pl.program_id(0)

        @pl.when(i == 0)
        def _():
            cu_ref[...] = jnp.zeros_like(cu_ref)
            cf_ref[...] = jnp.zeros_like(cf_ref)

        xv = x_ref[...]
        hb = (xv * _rms(xv) * gm_ref[...]).astype(BF)
        h1_ref[...] = hb
        zc = _dot(hb, wzc_ref[...])
        zc_ref[...] = zc
        qkv = _dot(hb, wqkv_ref[...])
        qkv = jnp.where(_cols(qkv.shape) < ATTN_W, qkv * 0.125, qkv)
        qkvb = qkv.astype(BF)
        qkv_ref[...] = qkvb
        qt_ref[0] = qkv[:, :ATTN_W].T.astype(BF)
        kt_ref[0] = qkv[:, ATTN_W:2 * ATTN_W].T.astype(BF)
        vt_ref[0] = qkv[:, 2 * ATTN_W:].T.astype(BF)

        gb, gc, xc = zc[:, :CONV_CH], zc[:, CONV_CH:2 * CONV_CH], zc[:, 2 * CONV_CH:]
        u = gc * xc
        cv, _, _ = _conv3(u, cu_ref[...], wc_ref[...])
        cu_ref[...] = u[tm - 8:, :]
        y = gb * cv
        nc_ref[...] = (y * _rms(y) * gco_ref[...]).astype(BF)

        fpre = _dot(hb, wf_ref[...]) + bf_ref[...]
        fpre_ref[...] = fpre
        logf = jnp.minimum(fpre, 0.0) - jnp.log1p(jnp.exp(-jnp.abs(fpre)))
        logf = jnp.where(_cols(logf.shape) < N_HEADS, logf, 0.0)
        tri = (_rows((tm, tm)) >= _cols((tm, tm))).astype(F32)
        fcol = jnp.dot(tri, logf, precision=HI, preferred_element_type=F32) + cf_ref[...]
        cf_ref[...] = fcol[tm - 1:tm, :]
        fcol_ref[...] = fcol

        feat = _dot(_split_parts(fcol, 1.0), sc_ref[...]).astype(BF)
        lane = _cols((tm, LANES))
        for h in range(N_HEADS):
            hp, hh = divmod(h, 2)
            own = (lane >= HEAD_DIM * hh) & (lane < HEAD_DIM * (hh + 1))
            hs = slice(LANES * h, LANES * (h + 1))
            qa_ref[:, hs] = jnp.where(own, qkvb[:, LANES * hp:LANES * (hp + 1)], feat[:, hs])
            ka_ref[:, hs] = jnp.where(own, qkvb[:, ATTN_W + LANES * hp:ATTN_W + LANES * (hp + 1)],
                                      feat[:, aw + LANES * h:aw + LANES * (h + 1)])

    blk = lambda c: pl.BlockSpec((tm, c), lambda i: (i, 0))
    return pl.pallas_call(
        body, name="fwd_in", grid=(nb,),
        in_specs=[blk(D_MODEL), _full((1, D_MODEL)), _full(w_zc.shape), _full(w_qkv.shape), _full(w_f.shape),
                  _full((1, LANES)), _full((8, CONV_CH)), _full((1, CONV_CH)), _full((LANES, 2 * aw))],
        out_specs=[blk(D_MODEL), blk(3 * CONV_CH), blk(3 * ATTN_W), blk(LANES), blk(LANES),
                   blk(CONV_CH), blk(aw), blk(aw)] + [pl.BlockSpec((1, ATTN_W, tm), lambda i: (i, 0, 0))] * 3,
        out_shape=[jax.ShapeDtypeStruct((s, D_MODEL), BF), jax.ShapeDtypeStruct((s, 3 * CONV_CH), F32),
                   jax.ShapeDtypeStruct((s, 3 * ATTN_W), BF), jax.ShapeDtypeStruct((s, LANES), F32),
                   jax.ShapeDtypeStruct((s, LANES), F32),
                   jax.ShapeDtypeStruct((s, CONV_CH), BF), jax.ShapeDtypeStruct((s, aw), BF),
                   jax.ShapeDtypeStruct((s, aw), BF)] + [jax.ShapeDtypeStruct((nb, ATTN_W, tm), BF)] * 3,
        scratch_shapes=[pltpu.VMEM((8, CONV_CH), F32), pltpu.VMEM((1, LANES), F32)],
        compiler_params=_params(VMEM_MID),
    )(x, g_mix, w_zc, w_qkv, w_f, b_f, w_conv, g_conv_out, _bias_scatter())


def _head_mask(shape, hh):
    lane = _cols(shape)
    return (lane >= HEAD_DIM * hh) & (lane < HEAD_DIM * (hh + 1))


def _attn_fwd(qa, ka, vt, t, shards):
    s = qa.shape[0]
    nb = s // t
    n = len(shards)

    def body(qa_ref, ka_any, vt_any, *refs):
        shard_refs, (o_ref, l_ref), gathered = refs[:n], refs[n:n + 2], refs[n + 2:2 * n + 2]
        ka_scr, vt_scr, m_scr, acc_scr, sem = refs[2 * n + 2:2 * n + 7]
        i = pl.program_id(0)
        start, forward, finish = _gather_phases(shard_refs, gathered, *refs[2 * n + 7:])
        pl.when(i == 0)(start)
        pl.when(i == nb // 2)(forward)
        _load_weights(i, [(ka_any, ka_scr), (vt_any, vt_scr)], sem)

        causal_t = _rows((t, t)) <= _cols((t, t))
        row = _rows((LANES, t))
        lrows = jnp.zeros((LANES, t), F32)
        for hp in range(N_HEADS // 2):
            ps = slice(hp * LANES, (hp + 1) * LANES)
            m_scr[...] = jnp.full(m_scr.shape, NEG, F32)
            acc_scr[...] = jnp.zeros(acc_scr.shape, F32)
            heads = [dict(hh=hh, hs=slice((2 * hp + hh) * LANES, (2 * hp + hh + 1) * LANES),
                          aux=HEAD_DIM * (1 - hh), own=(row >= HEAD_DIM * hh) & (row < HEAD_DIM * (hh + 1)),
                          fill=jnp.where(row == HEAD_DIM * (1 - hh), 1.0, 0.0).astype(BF),
                          qh=qa_ref[:, (2 * hp + hh) * LANES:(2 * hp + hh + 1) * LANES]) for hh in range(2)]

            def tile(kb, masked, ps=ps, heads=heads):
                rs = pl.ds(pl.multiple_of(kb * t, t), t)
                vt2 = vt_scr[kb, ps, :]
                es = [_dot_nt(ka_scr[rs, hd["hs"]], hd["qh"]) for hd in heads]
                m_olds = [m_scr[hd["hh"]:hd["hh"] + 1, :] for hd in heads]
                accs = [acc_scr[hd["hh"]] for hd in heads]
                m_news, acc_news = [], []
                for hd, e, m_old, acc in zip(heads, es, m_olds, accs):
                    if masked:
                        e = jnp.where(causal_t, e, NEG)
                    m_new = jnp.maximum(m_old, jnp.max(e, axis=0, keepdims=True))
                    p = jnp.exp(e - m_new).astype(BF)
                    vta = jnp.where(hd["own"], vt2, hd["fill"])
                    acc_news.append(jnp.exp(m_old - m_new) * acc + _dot(vta, p))
                    m_news.append(m_new)
                for hd, m_new, acc in zip(heads, m_news, acc_news):
                    acc_scr[hd["hh"]] = acc
                    m_scr[hd["hh"]:hd["hh"] + 1, :] = m_new

            def loop_body(kb, carry, tile=tile):
                tile(kb, False)
                return carry

            lax.fori_loop(0, i, loop_body, 0)
            tile(i, True)
            o_pair = None
            for hd in heads:
                hh = hd["hh"]
                acc = acc_scr[hh]
                denom = acc[hd["aux"]:hd["aux"] + 1, :]
                o_h = acc / denom
                lrows = jnp.where(row == 2 * hp + hh, m_scr[hh:hh + 1, :] + jnp.log(denom), lrows)
                o_pair = o_h if hh == 0 else jnp.where(row < HEAD_DIM, o_pair, o_h)
            o_ref[:, ps] = o_pair.T
        l_ref[...] = lrows.T
        pl.when(i == nb - 1)(finish)

    any_ = pl.BlockSpec(memory_space=pl.ANY)
    return pl.pallas_call(
        body, name="attn_fwd", grid=(nb,),
        in_specs=[pl.BlockSpec((t, N_HEADS * LANES), lambda i: (i, 0)), any_, any_] + [any_] * n,
        out_specs=[pl.BlockSpec((t, ATTN_W), lambda i: (i, 0)), pl.BlockSpec((t, LANES), lambda i: (i, 0))]
        + [any_] * n,
        out_shape=[jax.ShapeDtypeStruct((s, ATTN_W), F32), jax.ShapeDtypeStruct((s, LANES), F32)]
        + _gathered_shapes(shards),
        scratch_shapes=[pltpu.VMEM(ka.shape, BF), pltpu.VMEM(vt.shape, BF), pltpu.VMEM((2, t), F32),
                        pltpu.VMEM((2, LANES, t), F32), pltpu.SemaphoreType.DMA((2,))] + _exchange_sems(n),
        compiler_params=_params(VMEM_BIG),
    )(qa, ka, vt, *shards)


def _fwd_out(x, nc, o, g_attn_out, w_o, g_ffn, tm):
    s = x.shape[0]

    def body(x_ref, nc_ref, o_ref, ga_ref, wo_ref, gf_ref, x2_ref, h2_ref, mix_ref):
        ov = o_ref[...]
        na = (ov * _rms(ov) * ga_ref[...]).astype(BF)
        ncv = nc_ref[...]
        mix_ref[:, :CONV_CH] = ncv
        mix_ref[:, CONV_CH:] = na
        x2 = x_ref[...] + _dot(ncv, wo_ref[:CONV_CH, :]) + _dot(na, wo_ref[CONV_CH:, :])
        x2_ref[...] = x2
        h2_ref[...] = (x2 * _rms(x2) * gf_ref[...]).astype(BF)

    blk = lambda c: pl.BlockSpec((tm, c), lambda i: (i, 0))
    return pl.pallas_call(
        body, name="fwd_out", grid=(s // tm,),
        in_specs=[blk(D_MODEL), blk(CONV_CH), blk(ATTN_W), _full((1, ATTN_W)), _full(w_o.shape), _full((1, D_MODEL))],
        out_specs=[blk(D_MODEL), blk(D_MODEL), blk(D_MODEL)],
        out_shape=[jax.ShapeDtypeStruct((s, D_MODEL), F32), jax.ShapeDtypeStruct((s, D_MODEL), BF),
                   jax.ShapeDtypeStruct((s, D_MODEL), BF)],
        compiler_params=_params(VMEM_MID),
    )(x, nc, o, g_attn_out, w_o, g_ffn)


def _load_weights(i, pairs, sem):
    @pl.when(i == 0)
    def _():
        cps = [pltpu.make_async_copy(src, dst, sem.at[n]) for n, (src, dst) in enumerate(pairs)]
        for cp in cps:
            cp.start()
        for cp in cps:
            cp.wait()


def _fwd_ffn(h2, x2, tgt, w_up, w_ffn_conv, w_dn, g_final, tm):
    s = x2.shape[0]

    def body(h2_ref, x2_ref, tgt_ref, wfc_ref, gfin_ref, wup_any, wdn_any,
             u0_ref, act_ref, dx3_ref, dx3b_ref, loss_ref, dgfin_ref,
             wup, wdn, carry, acc, sem):
        i = pl.program_id(0)
        _load_weights(i, [(wup_any, wup), (wdn_any, wdn)], sem)

        @pl.when(i == 0)
        def _():
            carry[...] = jnp.zeros_like(carry)
            loss_ref[...] = jnp.zeros_like(loss_ref)
            dgfin_ref[...] = jnp.zeros_like(dgfin_ref)

        hb = h2_ref[...]
        acc[...] = jnp.zeros_like(acc)
        for j in range(N_FF_CHUNKS):
            ca = slice(j * FF_CHUNK, (j + 1) * FF_CHUNK)
            cg = slice(D_FF + j * FF_CHUNK, D_FF + (j + 1) * FF_CHUNK)
            parts = []
            for cc in (ca, cg):
                u0 = _dot(hb, wup[:, cc])
                u0_ref[:, cc] = u0.astype(BF)
                uu, _, _ = _conv3(u0, carry[:, cc], wfc_ref[:, cc])
                carry[:, cc] = u0[tm - 8:, :]
                parts.append(uu)
            ua, ug = parts
            act = (ug * jax.nn.sigmoid(ug) * ua).astype(BF)
            act_ref[:, ca] = act
            acc[...] += _dot(act, wdn[ca, :])

        x3 = x2_ref[...] + acc[...]
        r3 = _rms(x3)
        gfin = gfin_ref[...]
        xn = x3 * r3
        diff = xn * gfin - tgt_ref[...]
        loss_ref[...] += jnp.sum(jnp.sum(diff * diff, axis=-1, keepdims=True), axis=0, keepdims=True) * (0.5 / D_MODEL)
        dy = diff * (1.0 / D_MODEL)
        dgfin_ref[...] += jnp.sum(dy * xn, axis=0, keepdims=True)
        dx3 = _rms_bwd(x3, r3, dy * gfin)
        dx3_ref[...] = dx3
        dx3b_ref[...] = dx3.astype(BF)

    blk = lambda c: pl.BlockSpec((tm, c), lambda i: (i, 0))
    any_ = pl.BlockSpec(memory_space=pl.ANY)
    return pl.pallas_call(
        body, name="fwd_ffn", grid=(s // tm,),
        in_specs=[blk(D_MODEL), blk(D_MODEL), blk(D_MODEL), _full((8, 2 * D_FF)), _full((1, D_MODEL)), any_, any_],
        out_specs=[blk(2 * D_FF), blk(D_FF), blk(D_MODEL), blk(D_MODEL), _full((1, 1)), _full((1, D_MODEL))],
        out_shape=[jax.ShapeDtypeStruct((s, 2 * D_FF), BF), jax.ShapeDtypeStruct((s, D_FF), BF),
                   jax.ShapeDtypeStruct((s, D_MODEL), F32), jax.ShapeDtypeStruct((s, D_MODEL), BF),
                   jax.ShapeDtypeStruct((1, 1), F32), jax.ShapeDtypeStruct((1, D_MODEL), F32)],
        scratch_shapes=[pltpu.VMEM(w_up.shape, BF), pltpu.VMEM(w_dn.shape, BF), pltpu.VMEM((8, 2 * D_FF), F32),
                        pltpu.VMEM((tm, D_MODEL), F32), pltpu.SemaphoreType.DMA((2,))],
        compiler_params=_params(VMEM_BIG),
    )(h2, x2, tgt, w_ffn_conv, g_final, w_up, w_dn)


def _bwd_ffn(dx3, dx3b, u0, x2, w_up, w_ffn_conv, w_dn, g_ffn, tm):
    s = x2.shape[0]
    nb = s // tm
    hr = 16

    def body(dx3_ref, dx3b_ref, u0_ref, halo_ref, x2_ref, wfc_ref, gf_ref, wup_any, wdn_any,
             du0_ref, dx2_ref, dx2b_ref, dwfc_ref, dgf_ref,
             wup, wdn, carry, acc, sem):
        i = pl.program_id(0)
        rb = nb - 1 - i
        _load_weights(i, [(wup_any, wup), (wdn_any, wdn)], sem)

        @pl.when(i == 0)
        def _():
            carry[...] = jnp.zeros_like(carry)
            dwfc_ref[...] = jnp.zeros_like(dwfc_ref)
            dgf_ref[...] = jnp.zeros_like(dgf_ref)

        db = dx3b_ref[...]
        live = (rb > 0).astype(F32)
        acc[...] = jnp.zeros_like(acc)
        for j in range(N_FF_CHUNKS):
            ca = slice(j * FF_CHUNK, (j + 1) * FF_CHUNK)
            cg = slice(D_FF + j * FF_CHUNK, D_FF + (j + 1) * FF_CHUNK)
            fw = []
            for cc in (ca, cg):
                u0 = u0_ref[:, cc].astype(F32)
                prev = halo_ref[:, cc].astype(F32) * live
                fw.append((u0,) + _conv3(u0, prev, wfc_ref[:, cc]))
            (u0a, ua, u1a, u2a), (u0g, ug, u1g, u2g) = fw
            sg = jax.nn.sigmoid(ug)
            dact = _dot_nt(db, wdn[ca, :])
            da = dact * (ug * sg)
            dg = dact * ua * (sg * (1.0 + ug * (1.0 - sg)))
            for cc, d, u0c, u1c, u2c in ((ca, da, u0a, u1a, u2a), (cg, dg, u0g, u1g, u2g)):
                dwfc_ref[0:1, cc] += jnp.sum(d * u2c, axis=0, keepdims=True)
                dwfc_ref[1:2, cc] += jnp.sum(d * u1c, axis=0, keepdims=True)
                dwfc_ref[2:3, cc] += jnp.sum(d * u0c, axis=0, keepdims=True)
                du0 = _conv3_bwd(d, carry[:, cc], wfc_ref[:, cc]).astype(BF)
                carry[:, cc] = d[:8, :]
                du0_ref[:, cc] = du0
                acc[...] += _dot_nt(du0, wup[:, cc])

        x2v = x2_ref[...]
        r2 = _rms(x2v)
        dh2 = acc[...]
        dgf_ref[...] += jnp.sum(dh2 * (x2v * r2), axis=0, keepdims=True)
        dx2 = dx3_ref[...] + _rms_bwd(x2v, r2, dh2 * gf_ref[...])
        dx2_ref[...] = dx2
        dx2b_ref[...] = dx2.astype(BF)

    blk = lambda c: pl.BlockSpec((tm, c), lambda i: (nb - 1 - i, 0))
    halo = pl.BlockSpec((hr, 2 * D_FF), lambda i: (jnp.maximum((nb - 1 - i) * (tm // hr) - 1, 0), 0))
    any_ = pl.BlockSpec(memory_space=pl.ANY)
    return pl.pallas_call(
        body, name="bwd_ffn", grid=(nb,),
        in_specs=[blk(D_MODEL), blk(D_MODEL), blk(2 * D_FF), halo, blk(D_MODEL), _full((8, 2 * D_FF)),
                  _full((1, D_MODEL)), any_, any_],
        out_specs=[blk(2 * D_FF), blk(D_MODEL), blk(D_MODEL), _full((8, 2 * D_FF)), _full((1, D_MODEL))],
        out_shape=[jax.ShapeDtypeStruct((s, 2 * D_FF), BF), jax.ShapeDtypeStruct((s, D_MODEL), F32),
                   jax.ShapeDtypeStruct((s, D_MODEL), BF), jax.ShapeDtypeStruct((8, 2 * D_FF), F32),
                   jax.ShapeDtypeStruct((1, D_MODEL), F32)],
        scratch_shapes=[pltpu.VMEM(w_up.shape, BF), pltpu.VMEM(w_dn.shape, BF), pltpu.VMEM((8, 2 * D_FF), F32),
                        pltpu.VMEM((tm, D_MODEL), F32), pltpu.SemaphoreType.DMA((2,))],
        compiler_params=_params(VMEM_BIG),
    )(dx3, dx3b, u0, u0, x2, w_ffn_conv, g_ffn, w_up, w_dn)


def _bwd_out(dx2b, o, zc, fcol, lcol, qa, w_o, g_attn_out, g_conv_out, w_conv, tm):
    s = o.shape[0]
    nb = s // tm
    aw = N_HEADS * LANES

    def body(dx2b_ref, o_ref, zc_ref, halo_ref, fcol_ref, lcol_ref, qa_ref, wo_ref, ga_ref, gco_ref, wc_ref, sc_ref,
             dzc_ref, dob_ref, doa_ref, dot_ref, qab_ref, drow_ref, dwc_ref, dga_ref, dgco_ref, carry):
        i = pl.program_id(0)
        rb = nb - 1 - i

        @pl.when(i == 0)
        def _():
            carry[...] = jnp.zeros_like(carry)
            dwc_ref[...] = jnp.zeros_like(dwc_ref)
            dga_ref[...] = jnp.zeros_like(dga_ref)
            dgco_ref[...] = jnp.zeros_like(dgco_ref)

        dmix = _dot_nt(dx2b_ref[...], wo_ref[...])
        dnc, dna = dmix[:, :CONV_CH], dmix[:, CONV_CH:]

        ov = o_ref[...]
        ra = _rms(ov)
        dga_ref[...] += jnp.sum(dna * (ov * ra), axis=0, keepdims=True)
        do = _rms_bwd(ov, ra, dna * ga_ref[...])
        dob = do.astype(BF)
        dob_ref[...] = dob
        dot_ref[0] = do.T.astype(BF)
        sel = (_rows((ATTN_W, LANES)) // HEAD_DIM == _cols((ATTN_W, LANES))).astype(F32)
        delta = jnp.dot(do * ov, sel, precision=HI, preferred_element_type=F32)
        drow_ref[0] = delta.T[:N_HEADS, :]
        featc = _dot(_split_parts(fcol_ref[...] - lcol_ref[...], 1.0), sc_ref[...]).astype(BF)
        featd = _dot(_split_parts(-delta, 0.0), sc_ref[...]).astype(BF)
        lane = _cols((tm, LANES))
        for h in range(N_HEADS):
            hp, hh = divmod(h, 2)
            own = (lane >= HEAD_DIM * hh) & (lane < HEAD_DIM * (hh + 1))
            hs = slice(LANES * h, LANES * (h + 1))
            qab_ref[:, hs] = jnp.where(own, qa_ref[:, hs], featc[:, hs])
            doa_ref[:, hs] = jnp.where(own, dob[:, LANES * hp:LANES * (hp + 1)], featd[:, hs])

        zc_v = zc_ref[...]
        gb, gc, xc = zc_v[:, :CONV_CH], zc_v[:, CONV_CH:2 * CONV_CH], zc_v[:, 2 * CONV_CH:]
        hal = halo_ref[...] * (rb > 0).astype(F32)
        u = gc * xc
        prev = hal[:, CONV_CH:2 * CONV_CH] * hal[:, 2 * CONV_CH:]
        wc = wc_ref[...]
        cv, u1, u2 = _conv3(u, prev, wc)
        y = gb * cv
        rc = _rms(y)
        dgco_ref[...] += jnp.sum(dnc * (y * rc), axis=0, keepdims=True)
        dy = _rms_bwd(y, rc, dnc * gco_ref[...])
        dcv = dy * gb
        dwc_ref[0:1, :] += jnp.sum(dcv * u2, axis=0, keepdims=True)
        dwc_ref[1:2, :] += jnp.sum(dcv * u1, axis=0, keepdims=True)
        dwc_ref[2:3, :] += jnp.sum(dcv * u, axis=0, keepdims=True)
        du = _conv3_bwd(dcv, carry[...], wc)
        carry[...] = dcv[:8, :]
        dzc_ref[:, :CONV_CH] = (dy * cv).astype(BF)
        dzc_ref[:, CONV_CH:2 * CONV_CH] = (du * xc).astype(BF)
        dzc_ref[:, 2 * CONV_CH:] = (du * gc).astype(BF)

    blk = lambda c: pl.BlockSpec((tm, c), lambda i: (nb - 1 - i, 0))
    halo = pl.BlockSpec((8, 3 * CONV_CH), lambda i: (jnp.maximum((nb - 1 - i) * (tm // 8) - 1, 0), 0))
    row = pl.BlockSpec((1, N_HEADS, tm), lambda i: (nb - 1 - i, 0, 0))
    tr = pl.BlockSpec((1, ATTN_W, tm), lambda i: (nb - 1 - i, 0, 0))
    return pl.pallas_call(
        body, name="bwd_out", grid=(nb,),
        in_specs=[blk(D_MODEL), blk(ATTN_W), blk(3 * CONV_CH), halo, blk(LANES), blk(LANES), blk(aw),
                  _full(w_o.shape), _full((1, ATTN_W)), _full((1, CONV_CH)), _full((8, CONV_CH)), _full((LANES, aw))],
        out_specs=[blk(3 * CONV_CH), blk(ATTN_W), blk(aw), tr, blk(aw), row,
                   _full((8, CONV_CH)), _full((1, ATTN_W)), _full((1, CONV_CH))],
        out_shape=[jax.ShapeDtypeStruct((s, 3 * CONV_CH), BF), jax.ShapeDtypeStruct((s, ATTN_W), BF),
                   jax.ShapeDtypeStruct((s, aw), BF), jax.ShapeDtypeStruct((nb, ATTN_W, tm), BF),
                   jax.ShapeDtypeStruct((s, aw), BF), jax.ShapeDtypeStruct((nb, N_HEADS, tm), F32),
                   jax.ShapeDtypeStruct((8, CONV_CH), F32), jax.ShapeDtypeStruct((1, ATTN_W), F32),
                   jax.ShapeDtypeStruct((1, CONV_CH), F32)],
        scratch_shapes=[pltpu.VMEM((8, CONV_CH), F32)],
        compiler_params=_params(VMEM_MID),
    )(dx2b, o, zc, zc, fcol, lcol, qa, w_o, g_attn_out, g_conv_out, w_conv, _bias_scatter()[:, :aw])


def _attn_bwd_dq(qab, ka, qkv, kt, dob, drow, t, blocks):
    s = qab.shape[0]
    nb = s // t
    n = len(blocks)

    def body(qab_ref, do_ref, drow_ref, ka_any, qkv_any, kt_any, *refs):
        block_refs, (dq_ref, dfq_ref), received = refs[:n], refs[n:n + 2], refs[n + 2:2 * n + 2]
        ka_scr, v_scr, kt_scr, acc_scr, sem = refs[2 * n + 2:2 * n + 7]
        i = pl.program_id(0)
        start, finish = _scatter_phases(block_refs, received, *refs[2 * n + 7:])
        pl.when(i == 0)(start)
        _load_weights(i, [(ka_any, ka_scr), (qkv_any.at[:, pl.ds(2 * ATTN_W, ATTN_W)], v_scr), (kt_any, kt_scr)], sem)

        causal_t = _rows((t, t)) <= _cols((t, t))
        row = _rows((LANES, t))
        lane = _cols((t, LANES))
        dfrows = jnp.zeros((LANES, t), F32)
        for hp in range(N_HEADS // 2):
            ps = slice(hp * LANES, (hp + 1) * LANES)
            acc_scr[...] = jnp.zeros(acc_scr.shape, F32)
            do2 = do_ref[:, ps]
            heads = [dict(hh=hh, hs=slice((2 * hp + hh) * LANES, (2 * hp + hh + 1) * LANES),
                          aux=HEAD_DIM * (1 - hh), own=(row >= HEAD_DIM * hh) & (row < HEAD_DIM * (hh + 1)),
                          ownl=(lane >= HEAD_DIM * hh) & (lane < HEAD_DIM * (hh + 1)),
                          fill=jnp.where(row == HEAD_DIM * (1 - hh), 1.0, 0.0).astype(BF),
                          qh=qab_ref[:, (2 * hp + hh) * LANES:(2 * hp + hh + 1) * LANES],
                          dl=drow_ref[0, 2 * hp + hh:2 * hp + hh + 1, :]) for hh in range(2)]

            def tile(kb, masked, ps=ps, heads=heads, do2=do2):
                rs = pl.ds(pl.multiple_of(kb * t, t), t)
                v2 = v_scr[rs, ps]
                kt2 = kt_scr[kb, ps, :]
                es = [_dot_nt(ka_scr[rs, hd["hs"]], hd["qh"]) for hd in heads]
                dps = [_dot_nt(jnp.where(hd["ownl"], v2, jnp.zeros_like(v2)), do2) for hd in heads]
                accs = [acc_scr[hd["hh"]] for hd in heads]
                outs = []
                for hd, e, dp, acc in zip(heads, es, dps, accs):
                    if masked:
                        e = jnp.where(causal_t, e, NEG)
                    ds = (jnp.exp(e) * (dp - hd["dl"])).astype(BF)
                    outs.append(acc + _dot(jnp.where(hd["own"], kt2, hd["fill"]), ds))
                for hd, acc in zip(heads, outs):
                    acc_scr[hd["hh"]] = acc

            def loop_body(kb, carry, tile=tile):
                tile(kb, False)
                return carry

            lax.fori_loop(0, i, loop_body, 0)
            tile(i, True)
            dq_pair = None
            for hd in heads:
                acc = acc_scr[hd["hh"]]
                dfrows = jnp.where(row == 2 * hp + hd["hh"], acc[hd["aux"]:hd["aux"] + 1, :], dfrows)
                dq_pair = acc if hd["hh"] == 0 else jnp.where(row < HEAD_DIM, dq_pair, acc)
            dq_ref[:, ps] = (dq_pair * 0.125).T.astype(BF)
        dfq_ref[...] = dfrows.T
        pl.when(i == nb - 1)(finish)

    any_ = pl.BlockSpec(memory_space=pl.ANY)
    return pl.pallas_call(
        body, name="attn_bwd_dq", grid=(nb,),
        in_specs=[pl.BlockSpec((t, N_HEADS * LANES), lambda i: (i, 0)), pl.BlockSpec((t, ATTN_W), lambda i: (i, 0)),
                  pl.BlockSpec((1, N_HEADS, t), lambda i: (i, 0, 0)), any_, any_, any_] + [any_] * n,
        out_specs=[pl.BlockSpec((t, ATTN_W), lambda i: (i, 0)), pl.BlockSpec((t, LANES), lambda i: (i, 0))]
        + [any_] * n,
        out_shape=[jax.ShapeDtypeStruct((s, ATTN_W), BF), jax.ShapeDtypeStruct((s, LANES), F32)]
        + [jax.ShapeDtypeStruct(b.shape, b.dtype) for b in blocks],
        scratch_shapes=[pltpu.VMEM(ka.shape, BF), pltpu.VMEM((s, ATTN_W), BF), pltpu.VMEM(kt.shape, BF),
                        pltpu.VMEM((2, LANES, t), F32), pltpu.SemaphoreType.DMA((3,))] + _exchange_sems(n),
        compiler_params=_params(VMEM_BIG),
    )(qab, dob, drow, ka, qkv, kt, *blocks)


def _attn_bwd_dkv(qab, doa, ka, qkv, qt, dot, t):
    s = qab.shape[0]
    nb = s // t
    npair = N_HEADS // 2

    def body(ka_ref, v_ref, qab_ref, doa_ref, qt_ref, dot_ref, dk_ref, dv_ref, dfk_ref, acck, accv):
        hp = pl.program_id(0)
        j = pl.program_id(1)
        causal = _cols((t, t)) <= _rows((t, t))
        row = _rows((LANES, t))
        lane = _cols((t, LANES))
        acck[...] = jnp.zeros(acck.shape, F32)
        accv[...] = jnp.zeros(accv.shape, F32)
        v2 = v_ref[...]
        heads = []
        for hh in range(2):
            aux = HEAD_DIM * (1 - hh)
            ownl = (lane >= HEAD_DIM * hh) & (lane < HEAD_DIM * (hh + 1))
            ones3 = jnp.where((lane >= aux) & (lane < aux + 3), 1.0, 0.0).astype(BF)
            heads.append(dict(hh=hh, hs=slice(hh * LANES, (hh + 1) * LANES), aux=aux,
                              own=(row >= HEAD_DIM * hh) & (row < HEAD_DIM * (hh + 1)),
                              fill=jnp.where(row == aux, 1.0, 0.0).astype(BF),
                              kh=ka_ref[:, hh * LANES:(hh + 1) * LANES], vh=jnp.where(ownl, v2, ones3)))

        def tile(qb, masked):
            rs = pl.ds(pl.multiple_of(qb * t, t), t)
            qt2 = qt_ref[qb]
            dot2 = dot_ref[qb]
            es = [_dot_nt(qab_ref[rs, hd["hs"]], hd["kh"]) for hd in heads]
            dps = [_dot_nt(doa_ref[rs, hd["hs"]], hd["vh"]) for hd in heads]
            aks = [acck[hd["hh"]] for hd in heads]
            avs = [accv[hd["hh"]] for hd in heads]
            outs = []
            for hd, e, dp, ak, av in zip(heads, es, dps, aks, avs):
                if masked:
                    e = jnp.where(causal, e, NEG)
                p = jnp.exp(e)
                ds = (p * dp).astype(BF)
                outs.append((ak + _dot(jnp.where(hd["own"], qt2, hd["fill"]), ds), av + _dot(dot2, p.astype(BF))))
            for hd, (ak, av) in zip(heads, outs):
                acck[hd["hh"]] = ak
                accv[hd["hh"]] = av

        def loop_body(qb, carry):
            tile(qb, False)
            return carry

        tile(j, True)
        lax.fori_loop(j + 1, nb, loop_body, 0)
        dk_pair = dv_pair = None
        dfrows = jnp.zeros((LANES, t), F32)
        for hd in heads:
            ak, av = acck[hd["hh"]], accv[hd["hh"]]
            dfrows = jnp.where(row == 2 * hp + hd["hh"], -ak[hd["aux"]:hd["aux"] + 1, :], dfrows)
            dk_pair = ak if hd["hh"] == 0 else jnp.where(row < HEAD_DIM, dk_pair, ak)
            dv_pair = av if hd["hh"] == 0 else jnp.where(row < HEAD_DIM, dv_pair, av)
        dk_ref[...] = dk_pair.T.astype(BF)
        dv_ref[...] = dv_pair.T.astype(BF)
        dfk_ref[0] = dfrows.T

    pair_cols = pl.BlockSpec((s, 2 * LANES), lambda hp, j: (0, hp))
    pair_rows = pl.BlockSpec((nb, LANES, t), lambda hp, j: (0, hp, 0))
    return pl.pallas_call(
        body, name="attn_bwd_dkv", grid=(npair, nb),
        in_specs=[pl.BlockSpec((t, 2 * LANES), lambda hp, j: (j, hp)),
                  pl.BlockSpec((t, LANES), lambda hp, j: (j, 2 * npair + hp)),
                  pair_cols, pair_cols, pair_rows, pair_rows],
        out_specs=[pl.BlockSpec((t, LANES), lambda hp, j: (j, hp)), pl.BlockSpec((t, LANES), lambda hp, j: (j, hp)),
                   pl.BlockSpec((1, t, LANES), lambda hp, j: (hp, j, 0))],
        out_shape=[jax.ShapeDtypeStruct((s, ATTN_W), BF), jax.ShapeDtypeStruct((s, ATTN_W), BF),
                   jax.ShapeDtypeStruct((npair, s, LANES), F32)],
        scratch_shapes=[pltpu.VMEM((2, LANES, t), F32), pltpu.VMEM((2, LANES, t), F32)],
        compiler_params=_params(VMEM_BIG),
    )(ka, qkv, qab, doa, qt, dot)


def _bwd_in(x, dx2, dzc, dq, dk, dv, dfq, dfk, fpre, w_zc, w_qkv, w_f, g_mix, tm):
    s = x.shape[0]
    nb = s // tm

    def body(x_ref, dx2_ref, dzc_ref, dq_ref, dk_ref, dv_ref, dfq_ref, dfk_ref, fpre_ref, wzc_ref, wqkv_ref, wf_ref,
             gm_ref, gx_ref, dfb_ref, dgm_ref, dbf_ref, carry):
        i = pl.program_id(0)

        @pl.when(i == 0)
        def _():
            carry[...] = jnp.zeros_like(carry)
            dgm_ref[...] = jnp.zeros_like(dgm_ref)
            dbf_ref[...] = jnp.zeros_like(dbf_ref)

        triu = (_rows((tm, tm)) <= _cols((tm, tm))).astype(F32)
        df_cum = dfq_ref[...] + ((dfk_ref[0] + dfk_ref[1]) + (dfk_ref[2] + dfk_ref[3]))
        dlogf = jnp.dot(triu, df_cum, precision=HI, preferred_element_type=F32) + carry[...]
        carry[...] = dlogf[0:1, :]
        fpre = fpre_ref[...]
        df = jnp.where(_cols(fpre.shape) < N_HEADS, dlogf / (1.0 + jnp.exp(fpre)), 0.0)
        dbf_ref[...] += jnp.sum(df, axis=0, keepdims=True)
        dfb = df.astype(BF)
        dfb_ref[...] = dfb

        dh1 = _dot_nt(dzc_ref[...], wzc_ref[...])
        dh1 += _dot_nt(dq_ref[...], wqkv_ref[:, :ATTN_W])
        dh1 += _dot_nt(dk_ref[...], wqkv_ref[:, ATTN_W:2 * ATTN_W])
        dh1 += _dot_nt(dv_ref[...], wqkv_ref[:, 2 * ATTN_W:])
        dh1 += _dot_nt(dfb, wf_ref[...])
        xv = x_ref[...]
        r1 = _rms(xv)
        dgm_ref[...] += jnp.sum(dh1 * (xv * r1), axis=0, keepdims=True)
        gx_ref[...] = dx2_ref[...] + _rms_bwd(xv, r1, dh1 * gm_ref[...])

    blk = lambda c: pl.BlockSpec((tm, c), lambda i: (nb - 1 - i, 0))
    return pl.pallas_call(
        body, name="bwd_in", grid=(nb,),
        in_specs=[blk(D_MODEL), blk(D_MODEL), blk(3 * CONV_CH), blk(ATTN_W), blk(ATTN_W), blk(ATTN_W), blk(LANES),
                  pl.BlockSpec((N_HEADS // 2, tm, LANES), lambda i: (0, nb - 1 - i, 0)), blk(LANES),
                  _full(w_zc.shape), _full(w_qkv.shape), _full(w_f.shape), _full((1, D_MODEL))],
        out_specs=[blk(D_MODEL), blk(LANES), _full((1, D_MODEL)), _full((1, LANES))],
        out_shape=[jax.ShapeDtypeStruct((s, D_MODEL), F32), jax.ShapeDtypeStruct((s, LANES), BF),
                   jax.ShapeDtypeStruct((1, D_MODEL), F32), jax.ShapeDtypeStruct((1, LANES), F32)],
        scratch_shapes=[pltpu.VMEM((1, LANES), F32)],
        compiler_params=_params(VMEM_MID),
    )(x, dx2, dzc, dq, dk, dv, dfq, dfk, fpre, w_zc, w_qkv, w_f, g_mix)


def _matmul_tn(a, b, bn, bk, name):
    s, m = a.shape
    n = b.shape[1]

    def body(a_ref, b_ref, o_ref):
        @pl.when(pl.program_id(1) == 0)
        def _():
            o_ref[...] = jnp.zeros_like(o_ref)

        o_ref[...] += _dot_tn(a_ref[...], b_ref[...])

    return pl.pallas_call(
        body, name=name, grid=(n // bn, s // bk),
        in_specs=[pl.BlockSpec((bk, m), lambda jn, k: (k, 0)), pl.BlockSpec((bk, bn), lambda jn, k: (k, jn))],
        out_specs=pl.BlockSpec((m, bn), lambda jn, k: (0, jn)),
        out_shape=jax.ShapeDtypeStruct((m, n), F32),
        compiler_params=_params(VMEM_MID),
    )(a, b)


def _flip(v, bit):
    return 1 - v if bit else v


def _all_gather(shards):
    n = len(shards)

    def body(*refs):
        start, forward, finish = _gather_phases(refs[:n], refs[n:2 * n], *refs[2 * n:])
        start()
        forward()
        finish()

    any_ = pl.BlockSpec(memory_space=pl.ANY)
    return pl.pallas_call(
        body, name="all_gather_weights",
        in_specs=[any_] * n, out_specs=[any_] * n,
        out_shape=_gathered_shapes(shards), scratch_shapes=_exchange_sems(n),
    )(*shards)


def _gathered_shapes(shards):
    return [jax.ShapeDtypeStruct((8,) + sh.shape, sh.dtype) for sh in shards]


def _exchange_sems(n):
    return [pltpu.SemaphoreType.DMA((7 * n,)), pltpu.SemaphoreType.DMA((7 * n,)), pltpu.SemaphoreType.DMA((n,))]


def _gather_phases(src, out, send_sems, recv_sems, loc_sems):
    n = len(src)
    x, y, c = lax.axis_index("x"), lax.axis_index("y"), lax.axis_index("c")
    me, sibling = (x, y, c), (x, y, 1 - c)
    chips = [(1 - x, y), (x, 1 - y), (1 - x, 1 - y)]

    def slot(a, px, py, pc):
        return out[a].at[4 * px + 2 * py + pc]

    def copy(a, k, block, to, from_src=False):
        return pltpu.make_async_remote_copy(
            src_ref=src[a] if from_src else slot(a, *block), dst_ref=slot(a, *block),
            send_sem=send_sems.at[7 * a + k], recv_sem=recv_sems.at[7 * a + k],
            device_id=to, device_id_type=MESH)

    def local(a):
        return pltpu.make_async_copy(src[a], slot(a, *me), loc_sems.at[a])

    def first(a):
        return [copy(a, 0, me, sibling, True)] + [copy(a, 1 + j, me, (*chip, c), True)
                                                  for j, chip in enumerate(chips)]

    def passed(a, j):
        return copy(a, 4 + j, (*chips[j], c), sibling)

    def start():
        for a in range(n):
            local(a).start()
            for cp in first(a):
                cp.start()

    def forward():
        for a in range(n):
            for j, chip in enumerate(chips):
                copy(a, 1 + j, (*chip, c), me).wait_recv()
                passed(a, j).start()

    def finish():
        for a in range(n):
            copy(a, 0, sibling, me).wait_recv()
            for j, chip in enumerate(chips):
                copy(a, 4 + j, (*chip, 1 - c), me).wait_recv()
        for a in range(n):
            for cp in first(a) + [passed(a, j) for j in range(3)]:
                cp.wait_send()
            local(a).wait()

    return start, forward, finish


def _scatter_phases(src, out, send_sems, recv_sems, loc_sems):
    n = len(src)
    masks = [((k >> 2) & 1, (k >> 1) & 1, k & 1) for k in range(1, 8)]
    x, y, c = lax.axis_index("x"), lax.axis_index("y"), lax.axis_index("c")
    me = 4 * x + 2 * y + c

    def copies():
        cps = []
        for a in range(n):
            cps.append(pltpu.make_async_copy(src[a].at[me], out[a].at[me], loc_sems.at[a]))
            for k, (mx, my, mc) in enumerate(masks):
                px, py, pc = _flip(x, mx), _flip(y, my), _flip(c, mc)
                cps.append(pltpu.make_async_remote_copy(
                    src_ref=src[a].at[4 * px + 2 * py + pc], dst_ref=out[a].at[me],
                    send_sem=send_sems.at[7 * a + k], recv_sem=recv_sems.at[7 * a + k],
                    device_id=(px, py, pc), device_id_type=MESH))
        return cps

    def start():
        for cp in copies():
            cp.start()

    def finish():
        for cp in copies():
            cp.wait()

    return start, finish


def _all_to_all(blocks, name):
    n = len(blocks)

    def body(*refs):
        start, finish = _scatter_phases(refs[:n], refs[n:2 * n], *refs[2 * n:])
        start()
        finish()

    any_ = pl.BlockSpec(memory_space=pl.ANY)
    return pl.pallas_call(
        body, name=name,
        in_specs=[any_] * n, out_specs=[any_] * n,
        out_shape=[jax.ShapeDtypeStruct(b.shape, b.dtype) for b in blocks], scratch_shapes=_exchange_sems(n),
    )(*blocks)


def _adamw(parts, w, m, v, br, name):
    g8, r, c = parts.shape
    c1 = 1.0 - ADAM_B1 ** ADAM_STEP
    c2 = 1.0 - ADAM_B2 ** ADAM_STEP

    def body(p_ref, w_ref, m_ref, v_ref, g_ref, d_ref, m2_ref, v2_ref):
        g = p_ref[0]
        for d in range(1, g8):
            g = g + p_ref[d]
        m2 = ADAM_B1 * m_ref[...] + (1.0 - ADAM_B1) * g
        v2 = ADAM_B2 * v_ref[...] + (1.0 - ADAM_B2) * (g * g)
        g_ref[...] = g
        m2_ref[...] = m2
        v2_ref[...] = v2
        d_ref[...] = -ADAM_LR * ((m2 / c1) / (jnp.sqrt(v2 / c2) + ADAM_EPS) + ADAM_WD * w_ref[...])

    blk = pl.BlockSpec((br, c), lambda i: (i, 0))
    out = jax.ShapeDtypeStruct((r, c), F32)
    return pl.pallas_call(
        body, name=name, grid=(r // br,),
        in_specs=[pl.BlockSpec((g8, br, c), lambda i: (0, i, 0)), blk, blk, blk],
        out_specs=[blk] * 4, out_shape=[out] * 4,
        compiler_params=_params(VMEM_MID),
    )(parts, w, m, v)


def _pad_rows(a, rows):
    return jnp.pad(a, ((0, rows - a.shape[0]), (0, 0)))


_SMALL =(("g_mix", 1024), ("b_f", 8), ("g_conv_out", 512), ("g_attn_out", 512), ("g_ffn", 1024), ("g_final", 1024))
_SMALL_ROWS = 40


def _pack_small(vals):
    flat = jnp.concatenate([vals[k].reshape(-1) for k, _ in _SMALL])
    return jnp.pad(flat, (0, _SMALL_ROWS * LANES - flat.shape[0])).reshape(_SMALL_ROWS, LANES)


def _unpack_small(packed, shapes):
    flat = packed.reshape(-1)
    out, off = {}, 0
    for k, n in _SMALL:
        out[k] = flat[off:off + n].reshape(shapes[k])
        off += n
    return out


def _col_blocks(a):
    r, c8 = a.shape
    return jnp.transpose(a.reshape(r, 8, c8 // 8), (1, 0, 2))


def _from_col_blocks(a):
    g, r, c = a.shape
    return jnp.transpose(a, (1, 0, 2)).reshape(r, g * c)


def kernel(x, g_mix, w_in, b_f, w_conv, g_conv_out, g_attn_out, w_o, g_ffn, w_up, w_ffn_conv, w_down, g_final, loss_target, m_g_mix, m_w_in, m_b_f, m_w_conv, m_g_conv_out, m_g_attn_out, m_w_o, m_g_ffn, m_w_up, m_w_ffn_conv, m_w_down, m_g_final, v_g_mix, v_w_in, v_b_f, v_w_conv, v_g_conv_out, v_g_attn_out, v_w_o, v_g_ffn, v_w_up, v_w_ffn_conv, v_w_down, v_g_final):
    w = dict(g_mix=g_mix, w_in=w_in[0], b_f=b_f, w_conv=w_conv[0], g_conv_out=g_conv_out, g_attn_out=g_attn_out,
             w_o=w_o[0], g_ffn=g_ffn, w_up=w_up[0], w_ffn_conv=w_ffn_conv[0], w_down=w_down[0], g_final=g_final)
    m = dict(g_mix=m_g_mix, w_in=m_w_in[0], b_f=m_b_f, w_conv=m_w_conv[0], g_conv_out=m_g_conv_out,
             g_attn_out=m_g_attn_out, w_o=m_w_o[0], g_ffn=m_g_ffn, w_up=m_w_up[0], w_ffn_conv=m_w_ffn_conv[0],
             w_down=m_w_down[0], g_final=m_g_final)
    v = dict(g_mix=v_g_mix, w_in=v_w_in[0], b_f=v_b_f, w_conv=v_w_conv[0], g_conv_out=v_g_conv_out,
             g_attn_out=v_g_attn_out, w_o=v_w_o[0], g_ffn=v_g_ffn, w_up=v_w_up[0], w_ffn_conv=v_w_ffn_conv[0],
             w_down=v_w_down[0], g_final=v_g_final)
    shapes = dict(g_mix=g_mix.shape, w_in=w_in.shape, b_f=b_f.shape, w_conv=w_conv.shape,
                  g_conv_out=g_conv_out.shape, g_attn_out=g_attn_out.shape, w_o=w_o.shape, g_ffn=g_ffn.shape,
                  w_up=w_up.shape, w_ffn_conv=w_ffn_conv.shape, w_down=w_down.shape, g_final=g_final.shape)

    tm = t = 512
    tf = 256
    xs, tgt = x[0], loss_target[0]
    seq = xs.shape[0]
    assert seq % tm == 0 and seq % tf == 0
    bk = 1024 if seq % 1024 == 0 else 512

    g_in, g_conv = _all_gather([w["w_in"].astype(BF), w["w_conv"]])
    w_in_f = _from_col_blocks(g_in)
    w_zc = w_in_f[:, :3 * CONV_CH]
    w_qkv = w_in_f[:, 3 * CONV_CH:3 * CONV_CH + 3 * ATTN_W]
    w_f = jnp.pad(w_in_f[:, 3 * CONV_CH + 3 * ATTN_W:], ((0, 0), (0, LANES - N_HEADS)))
    b_f_p = jnp.pad(b_f, ((0, 0), (0, LANES - N_HEADS)))
    w_conv_p = _pad_rows(_from_col_blocks(g_conv), 8)

    h1, zc, qkv, fpre, fcol, nc, qa, ka, qt, kt, vt = _fwd_in(
        xs, g_mix, w_zc, w_qkv, w_f, b_f_p, w_conv_p, g_conv_out, tm)
    o, lcol, g_o, g_up, g_dn, g_fc = _attn_fwd(
        qa, ka, vt, t, [w["w_o"].astype(BF), w["w_up"].astype(BF), w["w_down"].astype(BF), w["w_ffn_conv"]])
    w_o_f = g_o.reshape(D_MODEL, D_MODEL)
    w_up_f = _from_col_blocks(g_up)
    w_dn_f = g_dn.reshape(D_FF, D_MODEL)
    w_fc_p = _pad_rows(_from_col_blocks(g_fc), 8)
    x2, h2, mix = _fwd_out(xs, nc, o, g_attn_out, w_o_f, g_ffn, tm)
    u0, act, dx3, dx3b, loss, dg_final = _fwd_ffn(h2, x2, tgt, w_up_f, w_fc_p, w_dn_f, g_final.reshape(1, D_MODEL), tf)
    loss = lax.psum(loss[0, 0], ("x", "y", "c"))

    du0, dx2, dx2b, dw_fc, dg_ffn = _bwd_ffn(dx3, dx3b, u0, x2, w_up_f, w_fc_p, w_dn_f, g_ffn, tf)
    dw_up = _matmul_tn(h2, du0, 1408, bk, "dw_up")
    dw_dn = _matmul_tn(act, dx3b, 512, bk, "dw_down")
    dw_o = _matmul_tn(mix, dx2b, 512, bk, "dw_o")
    dzc, dob, doa, dot, qab, drow, dw_conv, dg_attn, dg_conv = _bwd_out(
        dx2b, o, zc, fcol, lcol, qa, w_o_f, g_attn_out, g_conv_out, w_conv_p, t)
    dq, dfq, r_up, r_dn, r_fc, r_o = _attn_bwd_dq(
        qab, ka, qkv, kt, dob, drow, t,
        [_col_blocks(dw_up), dw_dn.reshape(8, D_FF // 8, D_MODEL), _col_blocks(dw_fc[:3]),
         dw_o.reshape(8, D_MODEL // 8, D_MODEL)])
    dk, dv, dfk = _attn_bwd_dkv(qab, doa, ka, qkv, qt, dot, t)
    gx, dfb, dg_mix, db_f = _bwd_in(xs, dx2, dzc, dq, dk, dv, dfq, dfk, fpre, w_zc, w_qkv, w_f, g_mix, tm)
    dw_zc = _matmul_tn(h1, dzc, 3 * CONV_CH, bk, "dw_in_conv")
    dw_q = _matmul_tn(h1, dq, ATTN_W, bk, "dw_in_q")
    dw_k = _matmul_tn(h1, dk, ATTN_W, bk, "dw_in_k")
    dw_v = _matmul_tn(h1, dv, ATTN_W, bk, "dw_in_v")
    dw_f = _matmul_tn(h1, dfb, LANES, bk, "dw_in_f")
    dw_in = jnp.concatenate([dw_zc, dw_q, dw_k, dw_v, dw_f[:, :N_HEADS]], axis=1)
    small = dict(g_mix=dg_mix, b_f=db_f[:, :N_HEADS], g_conv_out=dg_conv, g_attn_out=dg_attn, g_ffn=dg_ffn,
                 g_final=dg_final)
    r_in, r_conv, r_small = _all_to_all(
        [_col_blocks(dw_in), _col_blocks(dw_conv[:3]), jnp.broadcast_to(_pack_small(small), (8, _SMALL_ROWS, LANES))],
        "all_to_all_grads")
    big = ("w_in", "w_o", "w_up", "w_down", "w_conv", "w_ffn_conv")
    recv = [r_in, r_o, r_up, r_dn, r_conv, r_fc, r_small]

    rows = dict(w_in=256, w_o=128, w_up=256, w_down=176, w_conv=3, w_ffn_conv=3)
    g_out, d_out, m_out, v_out = {}, {}, {}, {}
    for a, k in enumerate(big):
        res = _adamw(recv[a], w[k], m[k], v[k], rows[k], "adamw_" + k)
        g_out[k], d_out[k], m_out[k], v_out[k] = [r.reshape(shapes[k]) for r in res]
    res = _adamw(recv[6], _pack_small(w), _pack_small(m), _pack_small(v), _SMALL_ROWS, "adamw_gains")
    for dst, packed in zip((g_out, d_out, m_out, v_out), res):
        dst.update(_unpack_small(packed, shapes))

    order = ("g_mix", "w_in", "b_f", "w_conv", "g_conv_out", "g_attn_out", "w_o", "g_ffn", "w_up", "w_ffn_conv",
             "w_down", "g_final")
    return (loss, gx.reshape(x.shape), *[g_out[k] for k in order], *[d_out[k] for k in order],
            *[m_out[k] for k in order], *[v_out[k] for k in order])
```

```python
import functools

import jax
import jax.numpy as jnp
import numpy as np
from jax import lax
from jax.experimental import pallas as pl
from jax.experimental.pallas import tpu as pltpu

F32 = jnp.float32
BF = jnp.bfloat16
HI = lax.Precision.HIGHEST
MESH = pl.DeviceIdType.MESH

D_MODEL = 1024
CONV_CH = 512
ATTN_W = 512
N_HEADS = 8
HEAD_DIM = 64
D_FF = 2816
FF_CHUNK = 256
N_FF_CHUNKS = D_FF // FF_CHUNK
IN_COLS = 3080
EPS = 1e-6
NEG = -1e30
LANES = 128
VMEM_BIG = 56 * 1024 * 1024
VMEM_MID = 40 * 1024 * 1024

ADAM_LR = 0.001
ADAM_B1 = 0.9
ADAM_B2 = 0.999
ADAM_EPS = 1e-08
ADAM_WD = 0.01
ADAM_STEP = 10

NT = (((1,), (1,)), ((), ()))
TN = (((0,), (0,)), ((), ()))


def _dot(a, b):
    return jnp.dot(a, b, preferred_element_type=F32)


def _dot_nt(a, b):
    return lax.dot_general(a, b, NT, preferred_element_type=F32)


def _dot_tn(a, b):
    return lax.dot_general(a, b, TN, preferred_element_type=F32)


def _params(vmem=None, parallel=False):
    return pltpu.CompilerParams(
        dimension_semantics=None, vmem_limit_bytes=vmem)


def _rows(shape):
    return lax.broadcasted_iota(jnp.int32, shape, 0)


def _cols(shape):
    return lax.broadcasted_iota(jnp.int32, shape, 1)


def _shift_down(u, prev, k):
    n = prev.shape[0]
    out = pltpu.roll(u, k, 0)
    row = _rows(u.shape)
    for r in range(k):
        out = jnp.where(row == r, prev[n - k + r:n - k + r + 1, :].astype(u.dtype), out)
    return out


def _shift_up(u, nxt, k):
    tm = u.shape[0]
    out = pltpu.roll(u, tm - k, 0)
    row = _rows(u.shape)
    for r in range(k):
        out = jnp.where(row == tm - k + r, nxt[r:r + 1, :], out)
    return out


def _conv3(u, prev, w):
    u1 = _shift_down(u, prev, 1)
    u2 = _shift_down(u, prev, 2)
    return w[0:1, :] * u2 + w[1:2, :] * u1 + w[2:3, :] * u, u1, u2


def _conv3_bwd(d, nxt, w):
    return w[2:3, :] * d + w[1:2, :] * _shift_up(d, nxt, 1) + w[0:1, :] * _shift_up(d, nxt, 2)


def _rms(x):
    return lax.rsqrt(jnp.mean(x * x, axis=-1, keepdims=True) + EPS)


def _rms_bwd(x, r, dyg):
    return r * dyg - x * (r * r * r) * jnp.mean(dyg * x, axis=-1, keepdims=True)


def _full(shape):
    nd = len(shape)
    return pl.BlockSpec(shape, lambda i, _n=nd: (0,) * _n)


ONES_LANE = 24


def _bias_scatter():
    sc = np.zeros((LANES, 2 * N_HEADS * LANES), np.float32)
    koff = N_HEADS * LANES
    for h in range(N_HEADS):
        aux = HEAD_DIM * (1 - h % 2)
        for j in range(3):
            sc[8 * j + h, LANES * h + aux + j] = 1.0
            sc[ONES_LANE, koff + LANES * h + aux + j] = 1.0
            sc[ONES_LANE, LANES * h + aux + 3 + j] = 1.0
            sc[8 * j + h, koff + LANES * h + aux + 3 + j] = -1.0
    return jnp.asarray(sc, BF)


def _split_parts(v, one):
    hi = v.astype(BF).astype(F32)
    rest = v - hi
    mid = rest.astype(BF).astype(F32)
    lo = (rest - mid).astype(BF).astype(F32)
    parts = hi + pltpu.roll(mid, 8, 1) + pltpu.roll(lo, 16, 1)
    return jnp.where(_cols(parts.shape) == ONES_LANE, one, parts).astype(BF)


def _fwd_in(x, g_mix, w_zc, w_qkv, w_f, b_f, w_conv, g_conv_out, tm):
    s = x.shape[0]
    nb = s // tm
    aw = N_HEADS * LANES

    def body(x_ref, gm_ref, wzc_ref, wqkv_ref, wf_ref, bf_ref, wc_ref, gco_ref, sc_ref,
             h1_ref, zc_ref, qkv_ref, fpre_ref, fcol_ref, nc_ref, qa_ref, ka_ref, qt_ref, kt_ref, vt_ref,
             cu_ref, cf_ref):
        i = pl.program_id(0)

        @pl.when(i == 0)
        def _():
            cu_ref[...] = jnp.zeros_like(cu_ref)
            cf_ref[...] = jnp.zeros_like(cf_ref)

        xv = x_ref[...]
        hb = (xv * _rms(xv) * gm_ref[...]).astype(BF)
        h1_ref[...] = hb
        zc = _dot(hb, wzc_ref[...])
        zc_ref[...] = zc
        qkv = _dot(hb, wqkv_ref[...])
        qkv = jnp.where(_cols(qkv.shape) < ATTN_W, qkv * 0.125, qkv)
        qkvb = qkv.astype(BF)
        qkv_ref[...] = qkvb
        qt_ref[0] = qkv[:, :ATTN_W].T.astype(BF)
        kt_ref[0] = qkv[:, ATTN_W:2 * ATTN_W].T.astype(BF)
        vt_ref[0] = qkv[:, 2 * ATTN_W:].T.astype(BF)

        gb, gc, xc = zc[:, :CONV_CH], zc[:, CONV_CH:2 * CONV_CH], zc[:, 2 * CONV_CH:]
        u = gc * xc
        cv, _, _ = _conv3(u, cu_ref[...], wc_ref[...])
        cu_ref[...] = u[tm - 8:, :]
        y = gb * cv
        nc_ref[...] = (y * _rms(y) * gco_ref[...]).astype(BF)

        fpre = _dot(hb, wf_ref[...]) + bf_ref[...]
        fpre_ref[...] = fpre
        logf = jnp.minimum(fpre, 0.0) - jnp.log1p(jnp.exp(-jnp.abs(fpre)))
        logf = jnp.where(_cols(logf.shape) < N_HEADS, logf, 0.0)
        tri = (_rows((tm, tm)) >= _cols((tm, tm))).astype(F32)
        fcol = jnp.dot(tri, logf, precision=HI, preferred_element_type=F32) + cf_ref[...]
        cf_ref[...] = fcol[tm - 1:tm, :]
        fcol_ref[...] = fcol

        feat = _dot(_split_parts(fcol, 1.0), sc_ref[...]).astype(BF)
        lane = _cols((tm, LANES))
        for h in range(N_HEADS):
            hp, hh = divmod(h, 2)
            own = (lane >= HEAD_DIM * hh) & (lane < HEAD_DIM * (hh + 1))
            hs = slice(LANES * h, LANES * (h + 1))
            qa_ref[:, hs] = jnp.where(own, qkvb[:, LANES * hp:LANES * (hp + 1)], feat[:, hs])
            ka_ref[:, hs] = jnp.where(own, qkvb[:, ATTN_W + LANES * hp:ATTN_W + LANES * (hp + 1)],
                                      feat[:, aw + LANES * h:aw + LANES * (h + 1)])

    blk = lambda c: pl.BlockSpec((tm, c), lambda i: (i, 0))
    return pl.pallas_call(
        body, name="fwd_in", grid=(nb,),
        in_specs=[blk(D_MODEL), _full((1, D_MODEL)), _full(w_zc.shape), _full(w_qkv.shape), _full(w_f.shape),
                  _full((1, LANES)), _full((8, CONV_CH)), _full((1, CONV_CH)), _full((LANES, 2 * aw))],
        out_specs=[blk(D_MODEL), blk(3 * CONV_CH), blk(3 * ATTN_W), blk(LANES), blk(LANES),
                   blk(CONV_CH), blk(aw), blk(aw)] + [pl.BlockSpec((1, ATTN_W, tm), lambda i: (i, 0, 0))] * 3,
        out_shape=[jax.ShapeDtypeStruct((s, D_MODEL), BF), jax.ShapeDtypeStruct((s, 3 * CONV_CH), F32),
                   jax.ShapeDtypeStruct((s, 3 * ATTN_W), BF), jax.ShapeDtypeStruct((s, LANES), F32),
                   jax.ShapeDtypeStruct((s, LANES), F32),
                   jax.ShapeDtypeStruct((s, CONV_CH), BF), jax.ShapeDtypeStruct((s, aw), BF),
                   jax.ShapeDtypeStruct((s, aw), BF)] + [jax.ShapeDtypeStruct((nb, ATTN_W, tm), BF)] * 3,
        scratch_shapes=[pltpu.VMEM((8, CONV_CH), F32), pltpu.VMEM((1, LANES), F32)],
        compiler_params=_params(VMEM_MID),
    )(x, g_mix, w_zc, w_qkv, w_f, b_f, w_conv, g_conv_out, _bias_scatter())


def _head_mask(shape, hh):
    lane = _cols(shape)
    return (lane >= HEAD_DIM * hh) & (lane < HEAD_DIM * (hh + 1))


def _attn_fwd(qa, ka, vt, t, shards):
    s = qa.shape[0]
    nb = s // t
    n = len(shards)

    def body(qa_ref, ka_any, vt_any, *refs):
        shard_refs, (o_ref, l_ref), gathered = refs[:n], refs[n:n + 2], refs[n + 2:2 * n + 2]
        ka_scr, vt_scr, m_scr, acc_scr, sem = refs[2 * n + 2:2 * n + 7]
        i = pl.program_id(0)
        start, forward, finish = _gather_phases(shard_refs, gathered, *refs[2 * n + 7:])
        pl.when(i == 0)(start)
        pl.when(i == nb // 2)(forward)
        _load_weights(i, [(ka_any, ka_scr), (vt_any, vt_scr)], sem)

        causal_t = _rows((t, t)) <= _cols((t, t))
        row = _rows((LANES, t))
        lrows = jnp.zeros((LANES, t), F32)
        for hp in range(N_HEADS // 2):
            ps = slice(hp * LANES, (hp + 1) * LANES)
            m_scr[...] = jnp.full(m_scr.shape, NEG, F32)
            acc_scr[...] = jnp.zeros(acc_scr.shape, F32)
            heads = [dict(hh=hh, hs=slice((2 * hp + hh) * LANES, (2 * hp + hh + 1) * LANES),
                          aux=HEAD_DIM * (1 - hh), own=(row >= HEAD_DIM * hh) & (row < HEAD_DIM * (hh + 1)),
                          fill=jnp.where(row == HEAD_DIM * (1 - hh), 1.0, 0.0).astype(BF),
                          qh=qa_ref[:, (2 * hp + hh) * LANES:(2 * hp + hh + 1) * LANES]) for hh in range(2)]

            def tile(kb, masked, ps=ps, heads=heads):
                rs = pl.ds(pl.multiple_of(kb * t, t), t)
                vt2 = vt_scr[kb, ps, :]
                es = [_dot_nt(ka_scr[rs, hd["hs"]], hd["qh"]) for hd in heads]
                m_olds = [m_scr[hd["hh"]:hd["hh"] + 1, :] for hd in heads]
                accs = [acc_scr[hd["hh"]] for hd in heads]
                m_news, acc_news = [], []
                for hd, e, m_old, acc in zip(heads, es, m_olds, accs):
                    if masked:
                        e = jnp.where(causal_t, e, NEG)
                    m_new = jnp.maximum(m_old, jnp.max(e, axis=0, keepdims=True))
                    p = jnp.exp(e - m_new).astype(BF)
                    vta = jnp.where(hd["own"], vt2, hd["fill"])
                    acc_news.append(jnp.exp(m_old - m_new) * acc + _dot(vta, p))
                    m_news.append(m_new)
                for hd, m_new, acc in zip(heads, m_news, acc_news):
                    acc_scr[hd["hh"]] = acc
                    m_scr[hd["hh"]:hd["hh"] + 1, :] = m_new

            def loop_body(kb, carry, tile=tile):
                tile(kb, False)
                return carry

            lax.fori_loop(0, i, loop_body, 0)
            tile(i, True)
            o_pair = None
            for hd in heads:
                hh = hd["hh"]
                acc = acc_scr[hh]
                denom = acc[hd["aux"]:hd["aux"] + 1, :]
                o_h = acc / denom
                lrows = jnp.where(row == 2 * hp + hh, m_scr[hh:hh + 1, :] + jnp.log(denom), lrows)
                o_pair = o_h if hh == 0 else jnp.where(row < HEAD_DIM, o_pair, o_h)
            o_ref[:, ps] = o_pair.T
        l_ref[...] = lrows.T
        pl.when(i == nb - 1)(finish)

    any_ = pl.BlockSpec(memory_space=pl.ANY)
    return pl.pallas_call(
        body, name="attn_fwd", grid=(nb,),
        in_specs=[pl.BlockSpec((t, N_HEADS * LANES), lambda i: (i, 0)), any_, any_] + [any_] * n,
        out_specs=[pl.BlockSpec((t, ATTN_W), lambda i: (i, 0)), pl.BlockSpec((t, LANES), lambda i: (i, 0))]
        + [any_] * n,
        out_shape=[jax.ShapeDtypeStruct((s, ATTN_W), F32), jax.ShapeDtypeStruct((s, LANES), F32)]
        + _gathered_shapes(shards),
        scratch_shapes=[pltpu.VMEM(ka.shape, BF), pltpu.VMEM(vt.shape, BF), pltpu.VMEM((2, t), F32),
                        pltpu.VMEM((2, LANES, t), F32), pltpu.SemaphoreType.DMA((2,))] + _exchange_sems(n),
        compiler_params=_params(VMEM_BIG),
    )(qa, ka, vt, *shards)


def _fwd_out(x, nc, o, g_attn_out, w_o, g_ffn, tm):
    s = x.shape[0]

    def body(x_ref, nc_ref, o_ref, ga_ref, wo_ref, gf_ref, x2_ref, h2_ref, mix_ref):
        ov = o_ref[...]
        na = (ov * _rms(ov) * ga_ref[...]).astype(BF)
        ncv = nc_ref[...]
        mix_ref[:, :CONV_CH] = ncv
        mix_ref[:, CONV_CH:] = na
        x2 = x_ref[...] + _dot(ncv, wo_ref[:CONV_CH, :]) + _dot(na, wo_ref[CONV_CH:, :])
        x2_ref[...] = x2
        h2_ref[...] = (x2 * _rms(x2) * gf_ref[...]).astype(BF)

    blk = lambda c: pl.BlockSpec((tm, c), lambda i: (i, 0))
    return pl.pallas_call(
        body, name="fwd_out", grid=(s // tm,),
        in_specs=[blk(D_MODEL), blk(CONV_CH), blk(ATTN_W), _full((1, ATTN_W)), _full(w_o.shape), _full((1, D_MODEL))],
        out_specs=[blk(D_MODEL), blk(D_MODEL), blk(D_MODEL)],
        out_shape=[jax.ShapeDtypeStruct((s, D_MODEL), F32), jax.ShapeDtypeStruct((s, D_MODEL), BF),
                   jax.ShapeDtypeStruct((s, D_MODEL), BF)],
        compiler_params=_params(VMEM_MID),
    )(x, nc, o, g_attn_out, w_o, g_ffn)


def _load_weights(i, pairs, sem):
    @pl.when(i == 0)
    def _():
        cps = [pltpu.make_async_copy(src, dst, sem.at[n]) for n, (src, dst) in enumerate(pairs)]
        for cp in cps:
            cp.start()
        for cp in cps:
            cp.wait()


def _ff_cols(j):
    return (slice(j * FF_CHUNK, (j + 1) * FF_CHUNK), slice(D_FF + j * FF_CHUNK, D_FF + (j + 1) * FF_CHUNK))


def _fwd_ffn(h2, x2, tgt, w_up, w_ffn_conv, w_dn, g_final, tm):
    s = x2.shape[0]

    def body(h2_ref, x2_ref, tgt_ref, wfc_ref, gfin_ref, wup_any, wdn_any,
             u0_ref, act_ref, dx3_ref, dx3b_ref, loss_ref, dgfin_ref,
             wup, wdn, carry, sem):
        i = pl.program_id(0)
        _load_weights(i, [(wup_any, wup), (wdn_any, wdn)], sem)

        @pl.when(i == 0)
        def _():
            carry[...] = jnp.zeros_like(carry)
            loss_ref[...] = jnp.zeros_like(loss_ref)
            dgfin_ref[...] = jnp.zeros_like(dgfin_ref)

        hb = h2_ref[...]

        def up(j):
            return [_dot(hb, wup[:, cc]) for cc in _ff_cols(j)]

        nxt = up(0)
        down = None
        for j in range(N_FF_CHUNKS):
            cur = nxt
            if j + 1 < N_FF_CHUNKS:
                nxt = up(j + 1)
            parts = []
            for cc, u0 in zip(_ff_cols(j), cur):
                u0_ref[:, cc] = u0.astype(BF)
                uu, _, _ = _conv3(u0, carry[:, cc], wfc_ref[:, cc])
                carry[:, cc] = u0[tm - 8:, :]
                parts.append(uu)
            ua, ug = parts
            act = (ug * jax.nn.sigmoid(ug) * ua).astype(BF)
            ca = _ff_cols(j)[0]
            act_ref[:, ca] = act
            part = _dot(act, wdn[ca, :])
            down = part if down is None else down + part

        x3 = x2_ref[...] + down
        r3 = _rms(x3)
        gfin = gfin_ref[...]
        xn = x3 * r3
        diff = xn * gfin - tgt_ref[...]
        loss_ref[...] += jnp.sum(jnp.sum(diff * diff, axis=-1, keepdims=True), axis=0, keepdims=True) * (0.5 / D_MODEL)
        dy = diff * (1.0 / D_MODEL)
        dgfin_ref[...] += jnp.sum(dy * xn, axis=0, keepdims=True)
        dx3 = _rms_bwd(x3, r3, dy * gfin)
        dx3_ref[...] = dx3
        dx3b_ref[...] = dx3.astype(BF)

    blk = lambda c: pl.BlockSpec((tm, c), lambda i: (i, 0))
    any_ = pl.BlockSpec(memory_space=pl.ANY)
    return pl.pallas_call(
        body, name="fwd_ffn", grid=(s // tm,),
        in_specs=[blk(D_MODEL), blk(D_MODEL), blk(D_MODEL), _full((8, 2 * D_FF)), _full((1, D_MODEL)), any_, any_],
        out_specs=[blk(2 * D_FF), blk(D_FF), blk(D_MODEL), blk(D_MODEL), _full((1, 1)), _full((1, D_MODEL))],
        out_shape=[jax.ShapeDtypeStruct((s, 2 * D_FF), BF), jax.ShapeDtypeStruct((s, D_FF), BF),
                   jax.ShapeDtypeStruct((s, D_MODEL), F32), jax.ShapeDtypeStruct((s, D_MODEL), BF),
                   jax.ShapeDtypeStruct((1, 1), F32), jax.ShapeDtypeStruct((1, D_MODEL), F32)],
        scratch_shapes=[pltpu.VMEM(w_up.shape, BF), pltpu.VMEM(w_dn.shape, BF), pltpu.VMEM((8, 2 * D_FF), F32),
                        pltpu.SemaphoreType.DMA((2,))],
        compiler_params=_params(VMEM_BIG),
    )(h2, x2, tgt, w_ffn_conv, g_final, w_up, w_dn)


def _bwd_ffn(dx3, dx3b, u0, x2, w_up, w_ffn_conv, w_dn, g_ffn, tm):
    s = x2.shape[0]
    nb = s // tm
    hr = 16

    def body(dx3_ref, dx3b_ref, u0_ref, halo_ref, x2_ref, wfc_ref, gf_ref, wup_any, wdn_any,
             du0_ref, dx2_ref, dx2b_ref, dwfc_ref, dgf_ref,
             wup, wdn, carry, sem):
        i = pl.program_id(0)
        rb = nb - 1 - i
        _load_weights(i, [(wup_any, wup), (wdn_any, wdn)], sem)

        @pl.when(i == 0)
        def _():
            carry[...] = jnp.zeros_like(carry)
            dwfc_ref[...] = jnp.zeros_like(dwfc_ref)
            dgf_ref[...] = jnp.zeros_like(dgf_ref)

        db = dx3b_ref[...]
        live = (rb > 0).astype(F32)

        def dact_of(j):
            return _dot(db, wdn[:, _ff_cols(j)[0]])

        nxt = dact_of(0)
        dh2 = None
        for j in range(N_FF_CHUNKS):
            ca, cg = _ff_cols(j)
            dact = nxt
            if j + 1 < N_FF_CHUNKS:
                nxt = dact_of(j + 1)
            fw = []
            for cc in (ca, cg):
                u0 = u0_ref[:, cc].astype(F32)
                prev = halo_ref[:, cc].astype(F32) * live
                fw.append((u0,) + _conv3(u0, prev, wfc_ref[:, cc]))
            (u0a, ua, u1a, u2a), (u0g, ug, u1g, u2g) = fw
            sg = jax.nn.sigmoid(ug)
            da = dact * (ug * sg)
            dg = dact * ua * (sg * (1.0 + ug * (1.0 - sg)))
            for cc, d, u0c, u1c, u2c in ((ca, da, u0a, u1a, u2a), (cg, dg, u0g, u1g, u2g)):
                dwfc_ref[0:1, cc] += jnp.sum(d * u2c, axis=0, keepdims=True)
                dwfc_ref[1:2, cc] += jnp.sum(d * u1c, axis=0, keepdims=True)
                dwfc_ref[2:3, cc] += jnp.sum(d * u0c, axis=0, keepdims=True)
                du0 = _conv3_bwd(d, carry[:, cc], wfc_ref[:, cc]).astype(BF)
                carry[:, cc] = d[:8, :]
                du0_ref[:, cc] = du0
                part = _dot(du0, wup[cc, :])
                dh2 = part if dh2 is None else dh2 + part

        x2v = x2_ref[...]
        r2 = _rms(x2v)
        dgf_ref[...] += jnp.sum(dh2 * (x2v * r2), axis=0, keepdims=True)
        dx2 = dx3_ref[...] + _rms_bwd(x2v, r2, dh2 * gf_ref[...])
        dx2_ref[...] = dx2
        dx2b_ref[...] = dx2.astype(BF)

    blk = lambda c: pl.BlockSpec((tm, c), lambda i: (nb - 1 - i, 0))
    halo = pl.BlockSpec((hr, 2 * D_FF), lambda i: (jnp.maximum((nb - 1 - i) * (tm // hr) - 1, 0), 0))
    any_ = pl.BlockSpec(memory_space=pl.ANY)
    return pl.pallas_call(
        body, name="bwd_ffn", grid=(nb,),
        in_specs=[blk(D_MODEL), blk(D_MODEL), blk(2 * D_FF), halo, blk(D_MODEL), _full((8, 2 * D_FF)),
                  _full((1, D_MODEL)), any_, any_],
        out_specs=[blk(2 * D_FF), blk(D_MODEL), blk(D_MODEL), _full((8, 2 * D_FF)), _full((1, D_MODEL))],
        out_shape=[jax.ShapeDtypeStruct((s, 2 * D_FF), BF), jax.ShapeDtypeStruct((s, D_MODEL), F32),
                   jax.ShapeDtypeStruct((s, D_MODEL), BF), jax.ShapeDtypeStruct((8, 2 * D_FF), F32),
                   jax.ShapeDtypeStruct((1, D_MODEL), F32)],
        scratch_shapes=[pltpu.VMEM(w_up.shape, BF), pltpu.VMEM(w_dn.shape, BF), pltpu.VMEM((8, 2 * D_FF), F32),
                        pltpu.SemaphoreType.DMA((2,))],
        compiler_params=_params(VMEM_BIG),
    )(dx3, dx3b, u0, u0, x2, w_ffn_conv, g_ffn, w_up, w_dn)


def _bwd_out(dx2b, o, zc, fcol, lcol, qa, w_o, g_attn_out, g_conv_out, w_conv, tm):
    s = o.shape[0]
    nb = s // tm
    aw = N_HEADS * LANES

    def body(dx2b_ref, o_ref, zc_ref, halo_ref, fcol_ref, lcol_ref, qa_ref, wo_ref, ga_ref, gco_ref, wc_ref, sc_ref,
             dzc_ref, dob_ref, doa_ref, dot_ref, qab_ref, drow_ref, dwc_ref, dga_ref, dgco_ref, carry):
        i = pl.program_id(0)
        rb = nb - 1 - i

        @pl.when(i == 0)
        def _():
            carry[...] = jnp.zeros_like(carry)
            dwc_ref[...] = jnp.zeros_like(dwc_ref)
            dga_ref[...] = jnp.zeros_like(dga_ref)
            dgco_ref[...] = jnp.zeros_like(dgco_ref)

        dmix = _dot_nt(dx2b_ref[...], wo_ref[...])
        dnc, dna = dmix[:, :CONV_CH], dmix[:, CONV_CH:]

        ov = o_ref[...]
        ra = _rms(ov)
        dga_ref[...] += jnp.sum(dna * (ov * ra), axis=0, keepdims=True)
        do = _rms_bwd(ov, ra, dna * ga_ref[...])
        dob = do.astype(BF)
        dob_ref[...] = dob
        dot_ref[0] = do.T.astype(BF)
        sel = (_rows((ATTN_W, LANES)) // HEAD_DIM == _cols((ATTN_W, LANES))).astype(F32)
        delta = jnp.dot(do * ov, sel, precision=HI, preferred_element_type=F32)
        drow_ref[0] = delta.T[:N_HEADS, :]
        featc = _dot(_split_parts(fcol_ref[...] - lcol_ref[...], 1.0), sc_ref[...]).astype(BF)
        featd = _dot(_split_parts(-delta, 0.0), sc_ref[...]).astype(BF)
        lane = _cols((tm, LANES))
        for h in range(N_HEADS):
            hp, hh = divmod(h, 2)
            own = (lane >= HEAD_DIM * hh) & (lane < HEAD_DIM * (hh + 1))
            hs = slice(LANES * h, LANES * (h + 1))
            qab_ref[:, hs] = jnp.where(own, qa_ref[:, hs], featc[:, hs])
            doa_ref[:, hs] = jnp.where(own, dob[:, LANES * hp:LANES * (hp + 1)], featd[:, hs])

        zc_v = zc_ref[...]
        gb, gc, xc = zc_v[:, :CONV_CH], zc_v[:, CONV_CH:2 * CONV_CH], zc_v[:, 2 * CONV_CH:]
        hal = halo_ref[...] * (rb > 0).astype(F32)
        u = gc * xc
        prev = hal[:, CONV_CH:2 * CONV_CH] * hal[:, 2 * CONV_CH:]
        wc = wc_ref[...]
        cv, u1, u2 = _conv3(u, prev, wc)
        y = gb * cv
        rc = _rms(y)
        dgco_ref[...] += jnp.sum(dnc * (y * rc), axis=0, keepdims=True)
        dy = _rms_bwd(y, rc, dnc * gco_ref[...])
        dcv = dy * gb
        dwc_ref[0:1, :] += jnp.sum(dcv * u2, axis=0, keepdims=True)
        dwc_ref[1:2, :] += jnp.sum(dcv * u1, axis=0, keepdims=True)
        dwc_ref[2:3, :] += jnp.sum(dcv * u, axis=0, keepdims=True)
        du = _conv3_bwd(dcv, carry[...], wc)
        carry[...] = dcv[:8, :]
        dzc_ref[:, :CONV_CH] = (dy * cv).astype(BF)
        dzc_ref[:, CONV_CH:2 * CONV_CH] = (du * xc).astype(BF)
        dzc_ref[:, 2 * CONV_CH:] = (du * gc).astype(BF)

    blk = lambda c: pl.BlockSpec((tm, c), lambda i: (nb - 1 - i, 0))
    halo = pl.BlockSpec((8, 3 * CONV_CH), lambda i: (jnp.maximum((nb - 1 - i) * (tm // 8) - 1, 0), 0))
    row = pl.BlockSpec((1, N_HEADS, tm), lambda i: (nb - 1 - i, 0, 0))
    tr = pl.BlockSpec((1, ATTN_W, tm), lambda i: (nb - 1 - i, 0, 0))
    return pl.pallas_call(
        body, name="bwd_out", grid=(nb,),
        in_specs=[blk(D_MODEL), blk(ATTN_W), blk(3 * CONV_CH), halo, blk(LANES), blk(LANES), blk(aw),
                  _full(w_o.shape), _full((1, ATTN_W)), _full((1, CONV_CH)), _full((8, CONV_CH)), _full((LANES, aw))],
        out_specs=[blk(3 * CONV_CH), blk(ATTN_W), blk(aw), tr, blk(aw), row,
                   _full((8, CONV_CH)), _full((1, ATTN_W)), _full((1, CONV_CH))],
        out_shape=[jax.ShapeDtypeStruct((s, 3 * CONV_CH), BF), jax.ShapeDtypeStruct((s, ATTN_W), BF),
                   jax.ShapeDtypeStruct((s, aw), BF), jax.ShapeDtypeStruct((nb, ATTN_W, tm), BF),
                   jax.ShapeDtypeStruct((s, aw), BF), jax.ShapeDtypeStruct((nb, N_HEADS, tm), F32),
                   jax.ShapeDtypeStruct((8, CONV_CH), F32), jax.ShapeDtypeStruct((1, ATTN_W), F32),
                   jax.ShapeDtypeStruct((1, CONV_CH), F32)],
        scratch_shapes=[pltpu.VMEM((8, CONV_CH), F32)],
        compiler_params=_params(VMEM_MID),
    )(dx2b, o, zc, zc, fcol, lcol, qa, w_o, g_attn_out, g_conv_out, w_conv, _bias_scatter()[:, :aw])


def _attn_bwd_dq(qab, ka, qkv, kt, dob, drow, t, blocks):
    s = qab.shape[0]
    nb = s // t
    n = len(blocks)

    def body(qab_ref, do_ref, drow_ref, ka_any, qkv_any, kt_any, *refs):
        block_refs, (dq_ref, dfq_ref), received = refs[:n], refs[n:n + 2], refs[n + 2:2 * n + 2]
        ka_scr, v_scr, kt_scr, acc_scr, sem = refs[2 * n + 2:2 * n + 7]
        i = pl.program_id(0)
        start, finish = _scatter_phases(block_refs, received, *refs[2 * n + 7:])
        pl.when(i == 0)(start)
        _load_weights(i, [(ka_any, ka_scr), (qkv_any.at[:, pl.ds(2 * ATTN_W, ATTN_W)], v_scr), (kt_any, kt_scr)], sem)

        causal_t = _rows((t, t)) <= _cols((t, t))
        row = _rows((LANES, t))
        lane = _cols((t, LANES))
        dfrows = jnp.zeros((LANES, t), F32)
        for hp in range(N_HEADS // 2):
            ps = slice(hp * LANES, (hp + 1) * LANES)
            acc_scr[...] = jnp.zeros(acc_scr.shape, F32)
            do2 = do_ref[:, ps]
            heads = [dict(hh=hh, hs=slice((2 * hp + hh) * LANES, (2 * hp + hh + 1) * LANES),
                          aux=HEAD_DIM * (1 - hh), own=(row >= HEAD_DIM * hh) & (row < HEAD_DIM * (hh + 1)),
                          ownl=(lane >= HEAD_DIM * hh) & (lane < HEAD_DIM * (hh + 1)),
                          fill=jnp.where(row == HEAD_DIM * (1 - hh), 1.0, 0.0).astype(BF),
                          qh=qab_ref[:, (2 * hp + hh) * LANES:(2 * hp + hh + 1) * LANES],
                          dl=drow_ref[0, 2 * hp + hh:2 * hp + hh + 1, :]) for hh in range(2)]

            def tile(kb, masked, ps=ps, heads=heads, do2=do2):
                rs = pl.ds(pl.multiple_of(kb * t, t), t)
                v2 = v_scr[rs, ps]
                kt2 = kt_scr[kb, ps, :]
                es = [_dot_nt(ka_scr[rs, hd["hs"]], hd["qh"]) for hd in heads]
                dps = [_dot_nt(jnp.where(hd["ownl"], v2, jnp.zeros_like(v2)), do2) for hd in heads]
                accs = [acc_scr[hd["hh"]] for hd in heads]
                outs = []
                for hd, e, dp, acc in zip(heads, es, dps, accs):
                    if masked:
                        e = jnp.where(causal_t, e, NEG)
                    ds = (jnp.exp(e) * (dp - hd["dl"])).astype(BF)
                    outs.append(acc + _dot(jnp.where(hd["own"], kt2, hd["fill"]), ds))
                for hd, acc in zip(heads, outs):
                    acc_scr[hd["hh"]] = acc

            def loop_body(kb, carry, tile=tile):
                tile(kb, False)
                return carry

            lax.fori_loop(0, i, loop_body, 0)
            tile(i, True)
            dq_pair = None
            for hd in heads:
                acc = acc_scr[hd["hh"]]
                dfrows = jnp.where(row == 2 * hp + hd["hh"], acc[hd["aux"]:hd["aux"] + 1, :], dfrows)
                dq_pair = acc if hd["hh"] == 0 else jnp.where(row < HEAD_DIM, dq_pair, acc)
            dq_ref[:, ps] = (dq_pair * 0.125).T.astype(BF)
        dfq_ref[...] = dfrows.T
        pl.when(i == nb - 1)(finish)

    any_ = pl.BlockSpec(memory_space=pl.ANY)
    return pl.pallas_call(
        body, name="attn_bwd_dq", grid=(nb,),
        in_specs=[pl.BlockSpec((t, N_HEADS * LANES), lambda i: (i, 0)), pl.BlockSpec((t, ATTN_W), lambda i: (i, 0)),
                  pl.BlockSpec((1, N_HEADS, t), lambda i: (i, 0, 0)), any_, any_, any_] + [any_] * n,
        out_specs=[pl.BlockSpec((t, ATTN_W), lambda i: (i, 0)), pl.BlockSpec((t, LANES), lambda i: (i, 0))]
        + [any_] * n,
        out_shape=[jax.ShapeDtypeStruct((s, ATTN_W), BF), jax.ShapeDtypeStruct((s, LANES), F32)]
        + [jax.ShapeDtypeStruct(b.shape, b.dtype) for b in blocks],
        scratch_shapes=[pltpu.VMEM(ka.shape, BF), pltpu.VMEM((s, ATTN_W), BF), pltpu.VMEM(kt.shape, BF),
                        pltpu.VMEM((2, LANES, t), F32), pltpu.SemaphoreType.DMA((3,))] + _exchange_sems(n),
        compiler_params=_params(VMEM_BIG),
    )(qab, dob, drow, ka, qkv, kt, *blocks)


def _attn_bwd_dkv(qab, doa, ka, qkv, qt, dot, t):
    s = qab.shape[0]
    nb = s // t
    npair = N_HEADS // 2

    def body(ka_ref, v_ref, qab_ref, doa_ref, qt_ref, dot_ref, dk_ref, dv_ref, dfk_ref, acck, accv):
        hp = pl.program_id(0)
        j = pl.program_id(1)
        causal = _cols((t, t)) <= _rows((t, t))
        row = _rows((LANES, t))
        lane = _cols((t, LANES))
        acck[...] = jnp.zeros(acck.shape, F32)
        accv[...] = jnp.zeros(accv.shape, F32)
        v2 = v_ref[...]
        heads = []
        for hh in range(2):
            aux = HEAD_DIM * (1 - hh)
            ownl = (lane >= HEAD_DIM * hh) & (lane < HEAD_DIM * (hh + 1))
            ones3 = jnp.where((lane >= aux) & (lane < aux + 3), 1.0, 0.0).astype(BF)
            heads.append(dict(hh=hh, hs=slice(hh * LANES, (hh + 1) * LANES), aux=aux,
                              own=(row >= HEAD_DIM * hh) & (row < HEAD_DIM * (hh + 1)),
                              fill=jnp.where(row == aux, 1.0, 0.0).astype(BF),
                              kh=ka_ref[:, hh * LANES:(hh + 1) * LANES], vh=jnp.where(ownl, v2, ones3)))

        def tile(qb, masked):
            rs = pl.ds(pl.multiple_of(qb * t, t), t)
            qt2 = qt_ref[qb]
            dot2 = dot_ref[qb]
            es = [_dot_nt(qab_ref[rs, hd["hs"]], hd["kh"]) for hd in heads]
            dps = [_dot_nt(doa_ref[rs, hd["hs"]], hd["vh"]) for hd in heads]
            aks = [acck[hd["hh"]] for hd in heads]
            avs = [accv[hd["hh"]] for hd in heads]
            outs = []
            for hd, e, dp, ak, av in zip(heads, es, dps, aks, avs):
                if masked:
                    e = jnp.where(causal, e, NEG)
                p = jnp.exp(e)
                ds = (p * dp).astype(BF)
                outs.append((ak + _dot(jnp.where(hd["own"], qt2, hd["fill"]), ds), av + _dot(dot2, p.astype(BF))))
            for hd, (ak, av) in zip(heads, outs):
                acck[hd["hh"]] = ak
                accv[hd["hh"]] = av

        def loop_body(qb, carry):
            tile(qb, False)
            return carry

        tile(j, True)
        lax.fori_loop(j + 1, nb, loop_body, 0)
        dk_pair = dv_pair = None
        dfrows = jnp.zeros((LANES, t), F32)
        for hd in heads:
            ak, av = acck[hd["hh"]], accv[hd["hh"]]
            dfrows = jnp.where(row == 2 * hp + hd["hh"], -ak[hd["aux"]:hd["aux"] + 1, :], dfrows)
            dk_pair = ak if hd["hh"] == 0 else jnp.where(row < HEAD_DIM, dk_pair, ak)
            dv_pair = av if hd["hh"] == 0 else jnp.where(row < HEAD_DIM, dv_pair, av)
        dk_ref[...] = dk_pair.T.astype(BF)
        dv_ref[...] = dv_pair.T.astype(BF)
        dfk_ref[0] = dfrows.T

    pair_cols = pl.BlockSpec((s, 2 * LANES), lambda hp, j: (0, hp))
    pair_rows = pl.BlockSpec((nb, LANES, t), lambda hp, j: (0, hp, 0))
    return pl.pallas_call(
        body, name="attn_bwd_dkv", grid=(npair, nb),
        in_specs=[pl.BlockSpec((t, 2 * LANES), lambda hp, j: (j, hp)),
                  pl.BlockSpec((t, LANES), lambda hp, j: (j, 2 * npair + hp)),
                  pair_cols, pair_cols, pair_rows, pair_rows],
        out_specs=[pl.BlockSpec((t, LANES), lambda hp, j: (j, hp)), pl.BlockSpec((t, LANES), lambda hp, j: (j, hp)),
                   pl.BlockSpec((1, t, LANES), lambda hp, j: (hp, j, 0))],
        out_shape=[jax.ShapeDtypeStruct((s, ATTN_W), BF), jax.ShapeDtypeStruct((s, ATTN_W), BF),
                   jax.ShapeDtypeStruct((npair, s, LANES), F32)],
        scratch_shapes=[pltpu.VMEM((2, LANES, t), F32), pltpu.VMEM((2, LANES, t), F32)],
        compiler_params=_params(VMEM_BIG),
    )(ka, qkv, qab, doa, qt, dot)


def _bwd_in(x, dx2, dzc, dq, dk, dv, dfq, dfk, fpre, w_zc, w_qkv, w_f, g_mix, tm):
    s = x.shape[0]
    nb = s // tm

    def body(x_ref, dx2_ref, dzc_ref, dq_ref, dk_ref, dv_ref, dfq_ref, dfk_ref, fpre_ref, wzc_ref, wqkv_ref, wf_ref,
             gm_ref, gx_ref, dfb_ref, dgm_ref, dbf_ref, carry):
        i = pl.program_id(0)

        @pl.when(i == 0)
        def _():
            carry[...] = jnp.zeros_like(carry)
            dgm_ref[...] = jnp.zeros_like(dgm_ref)
            dbf_ref[...] = jnp.zeros_like(dbf_ref)

        triu = (_rows((tm, tm)) <= _cols((tm, tm))).astype(F32)
        df_cum = dfq_ref[...] + ((dfk_ref[0] + dfk_ref[1]) + (dfk_ref[2] + dfk_ref[3]))
        dlogf = jnp.dot(triu, df_cum, precision=HI, preferred_element_type=F32) + carry[...]
        carry[...] = dlogf[0:1, :]
        fpre = fpre_ref[...]
        df = jnp.where(_cols(fpre.shape) < N_HEADS, dlogf / (1.0 + jnp.exp(fpre)), 0.0)
        dbf_ref[...] += jnp.sum(df, axis=0, keepdims=True)
        dfb = df.astype(BF)
        dfb_ref[...] = dfb

        dh1 = _dot_nt(dzc_ref[...], wzc_ref[...])
        dh1 += _dot_nt(dq_ref[...], wqkv_ref[:, :ATTN_W])
        dh1 += _dot_nt(dk_ref[...], wqkv_ref[:, ATTN_W:2 * ATTN_W])
        dh1 += _dot_nt(dv_ref[...], wqkv_ref[:, 2 * ATTN_W:])
        dh1 += _dot_nt(dfb, wf_ref[...])
        xv = x_ref[...]
        r1 = _rms(xv)
        dgm_ref[...] += jnp.sum(dh1 * (xv * r1), axis=0, keepdims=True)
        gx_ref[...] = dx2_ref[...] + _rms_bwd(xv, r1, dh1 * gm_ref[...])

    blk = lambda c: pl.BlockSpec((tm, c), lambda i: (nb - 1 - i, 0))
    return pl.pallas_call(
        body, name="bwd_in", grid=(nb,),
        in_specs=[blk(D_MODEL), blk(D_MODEL), blk(3 * CONV_CH), blk(ATTN_W), blk(ATTN_W), blk(ATTN_W), blk(LANES),
                  pl.BlockSpec((N_HEADS // 2, tm, LANES), lambda i: (0, nb - 1 - i, 0)), blk(LANES),
                  _full(w_zc.shape), _full(w_qkv.shape), _full(w_f.shape), _full((1, D_MODEL))],
        out_specs=[blk(D_MODEL), blk(LANES), _full((1, D_MODEL)), _full((1, LANES))],
        out_shape=[jax.ShapeDtypeStruct((s, D_MODEL), F32), jax.ShapeDtypeStruct((s, LANES), BF),
                   jax.ShapeDtypeStruct((1, D_MODEL), F32), jax.ShapeDtypeStruct((1, LANES), F32)],
        scratch_shapes=[pltpu.VMEM((1, LANES), F32)],
        compiler_params=_params(VMEM_MID),
    )(x, dx2, dzc, dq, dk, dv, dfq, dfk, fpre, w_zc, w_qkv, w_f, g_mix)


def _matmul_tn(a, b, bn, bk, name):
    s, m = a.shape
    n = b.shape[1]

    def body(a_ref, b_ref, o_ref):
        @pl.when(pl.program_id(1) == 0)
        def _():
            o_ref[...] = jnp.zeros_like(o_ref)

        o_ref[...] += _dot_tn(a_ref[...], b_ref[...])

    return pl.pallas_call(
        body, name=name, grid=(n // bn, s // bk),
        in_specs=[pl.BlockSpec((bk, m), lambda jn, k: (k, 0)), pl.BlockSpec((bk, bn), lambda jn, k: (k, jn))],
        out_specs=pl.BlockSpec((m, bn), lambda jn, k: (0, jn)),
        out_shape=jax.ShapeDtypeStruct((m, n), F32),
        compiler_params=_params(VMEM_MID),
    )(a, b)


def _flip(v, bit):
    return 1 - v if bit else v


def _all_gather(shards):
    n = len(shards)

    def body(*refs):
        start, forward, finish = _gather_phases(refs[:n], refs[n:2 * n], *refs[2 * n:])
        start()
        forward()
        finish()

    any_ = pl.BlockSpec(memory_space=pl.ANY)
    return pl.pallas_call(
        body, name="all_gather_weights",
        in_specs=[any_] * n, out_specs=[any_] * n,
        out_shape=_gathered_shapes(shards), scratch_shapes=_exchange_sems(n),
    )(*shards)


def _gathered_shapes(shards):
    return [jax.ShapeDtypeStruct((8,) + sh.shape, sh.dtype) for sh in shards]


def _exchange_sems(n):
    return [pltpu.SemaphoreType.DMA((7 * n,)), pltpu.SemaphoreType.DMA((7 * n,)), pltpu.SemaphoreType.DMA((n,))]


def _gather_phases(src, out, send_sems, recv_sems, loc_sems):
    n = len(src)
    x, y, c = lax.axis_index("x"), lax.axis_index("y"), lax.axis_index("c")
    me, sibling = (x, y, c), (x, y, 1 - c)
    chips = [(1 - x, y), (x, 1 - y), (1 - x, 1 - y)]

    def slot(a, px, py, pc):
        return out[a].at[4 * px + 2 * py + pc]

    def copy(a, k, block, to, from_src=False):
        return pltpu.make_async_remote_copy(
            src_ref=src[a] if from_src else slot(a, *block), dst_ref=slot(a, *block),
            send_sem=send_sems.at[7 * a + k], recv_sem=recv_sems.at[7 * a + k],
            device_id=to, device_id_type=MESH)

    def local(a):
        return pltpu.make_async_copy(src[a], slot(a, *me), loc_sems.at[a])

    def first(a):
        return [copy(a, 0, me, sibling, True)] + [copy(a, 1 + j, me, (*chip, c), True)
                                                  for j, chip in enumerate(chips)]

    def passed(a, j):
        return copy(a, 4 + j, (*chips[j], c), sibling)

    def start():
        for a in range(n):
            local(a).start()
            for cp in first(a):
                cp.start()

    def forward():
        for a in range(n):
            for j, chip in enumerate(chips):
                copy(a, 1 + j, (*chip, c), me).wait_recv()
                passed(a, j).start()

    def finish():
        for a in range(n):
            copy(a, 0, sibling, me).wait_recv()
            for j, chip in enumerate(chips):
                copy(a, 4 + j, (*chip, 1 - c), me).wait_recv()
        for a in range(n):
            for cp in first(a) + [passed(a, j) for j in range(3)]:
                cp.wait_send()
            local(a).wait()

    return start, forward, finish


def _scatter_phases(src, out, send_sems, recv_sems, loc_sems):
    n = len(src)
    masks = [((k >> 2) & 1, (k >> 1) & 1, k & 1) for k in range(1, 8)]
    x, y, c = lax.axis_index("x"), lax.axis_index("y"), lax.axis_index("c")
    me = 4 * x + 2 * y + c

    def copies():
        cps = []
        for a in range(n):
            cps.append(pltpu.make_async_copy(src[a].at[me], out[a].at[me], loc_sems.at[a]))
            for k, (mx, my, mc) in enumerate(masks):
                px, py, pc = _flip(x, mx), _flip(y, my), _flip(c, mc)
                cps.append(pltpu.make_async_remote_copy(
                    src_ref=src[a].at[4 * px + 2 * py + pc], dst_ref=out[a].at[me],
                    send_sem=send_sems.at[7 * a + k], recv_sem=recv_sems.at[7 * a + k],
                    device_id=(px, py, pc), device_id_type=MESH))
        return cps

    def start():
        for cp in copies():
            cp.start()

    def finish():
        for cp in copies():
            cp.wait()

    return start, finish


def _all_to_all(blocks, name):
    n = len(blocks)

    def body(*refs):
        start, finish = _scatter_phases(refs[:n], refs[n:2 * n], *refs[2 * n:])
        start()
        finish()

    any_ = pl.BlockSpec(memory_space=pl.ANY)
    return pl.pallas_call(
        body, name=name,
        in_specs=[any_] * n, out_specs=[any_] * n,
        out_shape=[jax.ShapeDtypeStruct(b.shape, b.dtype) for b in blocks], scratch_shapes=_exchange_sems(n),
    )(*blocks)


def _adamw(parts, w, m, v, br, name):
    g8, r, c = parts.shape
    c1 = 1.0 - ADAM_B1 ** ADAM_STEP
    c2 = 1.0 - ADAM_B2 ** ADAM_STEP

    def body(p_ref, w_ref, m_ref, v_ref, g_ref, d_ref, m2_ref, v2_ref):
        g = p_ref[0]
        for d in range(1, g8):
            g = g + p_ref[d]
        m2 = ADAM_B1 * m_ref[...] + (1.0 - ADAM_B1) * g
        v2 = ADAM_B2 * v_ref[...] + (1.0 - ADAM_B2) * (g * g)
        g_ref[...] = g
        m2_ref[...] = m2
        v2_ref[...] = v2
        d_ref[...] = -ADAM_LR * ((m2 / c1) / (jnp.sqrt(v2 / c2) + ADAM_EPS) + ADAM_WD * w_ref[...])

    blk = pl.BlockSpec((br, c), lambda i: (i, 0))
    out = jax.ShapeDtypeStruct((r, c), F32)
    return pl.pallas_call(
        body, name=name, grid=(r // br,),
        in_specs=[pl.BlockSpec((g8, br, c), lambda i: (0, i, 0)), blk, blk, blk],
        out_specs=[blk] * 4, out_shape=[out] * 4,
        compiler_params=_params(VMEM_MID),
    )(parts, w, m, v)


def _pad_rows(a, rows):
    return jnp.pad(a, ((0, rows - a.shape[0]), (0, 0)))


_SMALL =(("g_mix", 1024), ("b_f", 8), ("g_conv_out", 512), ("g_attn_out", 512), ("g_ffn", 1024), ("g_final", 1024))
_SMALL_ROWS = 40


def _pack_small(vals):
    flat = jnp.concatenate([vals[k].reshape(-1) for k, _ in _SMALL])
    return jnp.pad(flat, (0, _SMALL_ROWS * LANES - flat.shape[0])).reshape(_SMALL_ROWS, LANES)


def _unpack_small(packed, shapes):
    flat = packed.reshape(-1)
    out, off = {}, 0
    for k, n in _SMALL:
        out[k] = flat[off:off + n].reshape(shapes[k])
        off += n
    return out


def _col_blocks(a):
    r, c8 = a.shape
    return jnp.transpose(a.reshape(r, 8, c8 // 8), (1, 0, 2))


def _from_col_blocks(a):
    g, r, c = a.shape
    return jnp.transpose(a, (1, 0, 2)).reshape(r, g * c)


def kernel(x, g_mix, w_in, b_f, w_conv, g_conv_out, g_attn_out, w_o, g_ffn, w_up, w_ffn_conv, w_down, g_final, loss_target, m_g_mix, m_w_in, m_b_f, m_w_conv, m_g_conv_out, m_g_attn_out, m_w_o, m_g_ffn, m_w_up, m_w_ffn_conv, m_w_down, m_g_final, v_g_mix, v_w_in, v_b_f, v_w_conv, v_g_conv_out, v_g_attn_out, v_w_o, v_g_ffn, v_w_up, v_w_ffn_conv, v_w_down, v_g_final):
    w = dict(g_mix=g_mix, w_in=w_in[0], b_f=b_f, w_conv=w_conv[0], g_conv_out=g_conv_out, g_attn_out=g_attn_out,
             w_o=w_o[0], g_ffn=g_ffn, w_up=w_up[0], w_ffn_conv=w_ffn_conv[0], w_down=w_down[0], g_final=g_final)
    m = dict(g_mix=m_g_mix, w_in=m_w_in[0], b_f=m_b_f, w_conv=m_w_conv[0], g_conv_out=m_g_conv_out,
             g_attn_out=m_g_attn_out, w_o=m_w_o[0], g_ffn=m_g_ffn, w_up=m_w_up[0], w_ffn_conv=m_w_ffn_conv[0],
             w_down=m_w_down[0], g_final=m_g_final)
    v = dict(g_mix=v_g_mix, w_in=v_w_in[0], b_f=v_b_f, w_conv=v_w_conv[0], g_conv_out=v_g_conv_out,
             g_attn_out=v_g_attn_out, w_o=v_w_o[0], g_ffn=v_g_ffn, w_up=v_w_up[0], w_ffn_conv=v_w_ffn_conv[0],
             w_down=v_w_down[0], g_final=v_g_final)
    shapes = dict(g_mix=g_mix.shape, w_in=w_in.shape, b_f=b_f.shape, w_conv=w_conv.shape,
                  g_conv_out=g_conv_out.shape, g_attn_out=g_attn_out.shape, w_o=w_o.shape, g_ffn=g_ffn.shape,
                  w_up=w_up.shape, w_ffn_conv=w_ffn_conv.shape, w_down=w_down.shape, g_final=g_final.shape)

    tm = t = 512
    tf = 256
    xs, tgt = x[0], loss_target[0]
    seq = xs.shape[0]
    assert seq % tm == 0 and seq % tf == 0
    bk = 1024 if seq % 1024 == 0 else 512

    g_in, g_conv = _all_gather([w["w_in"].astype(BF), w["w_conv"]])
    w_in_f = _from_col_blocks(g_in)
    w_zc = w_in_f[:, :3 * CONV_CH]
    w_qkv = w_in_f[:, 3 * CONV_CH:3 * CONV_CH + 3 * ATTN_W]
    w_f = jnp.pad(w_in_f[:, 3 * CONV_CH + 3 * ATTN_W:], ((0, 0), (0, LANES - N_HEADS)))
    b_f_p = jnp.pad(b_f, ((0, 0), (0, LANES - N_HEADS)))
    w_conv_p = _pad_rows(_from_col_blocks(g_conv), 8)

    h1, zc, qkv, fpre, fcol, nc, qa, ka, qt, kt, vt = _fwd_in(
        xs, g_mix, w_zc, w_qkv, w_f, b_f_p, w_conv_p, g_conv_out, tm)
    o, lcol, g_o, g_up, g_dn, g_fc = _attn_fwd(
        qa, ka, vt, t, [w["w_o"].astype(BF), w["w_up"].astype(BF), w["w_down"].astype(BF), w["w_ffn_conv"]])
    w_o_f = g_o.reshape(D_MODEL, D_MODEL)
    w_up_f = _from_col_blocks(g_up)
    w_dn_f = g_dn.reshape(D_FF, D_MODEL)
    w_fc_p = _pad_rows(_from_col_blocks(g_fc), 8)
    x2, h2, mix = _fwd_out(xs, nc, o, g_attn_out, w_o_f, g_ffn, tm)
    u0, act, dx3, dx3b, loss, dg_final = _fwd_ffn(h2, x2, tgt, w_up_f, w_fc_p, w_dn_f, g_final.reshape(1, D_MODEL), tf)
    loss = lax.psum(loss[0, 0], ("x", "y", "c"))

    du0, dx2, dx2b, dw_fc, dg_ffn = _bwd_ffn(dx3, dx3b, u0, x2, w_up_f.T, w_fc_p, w_dn_f.T, g_ffn, tf)
    dw_up = _matmul_tn(h2, du0, 1408, bk, "dw_up")
    dw_dn = _matmul_tn(act, dx3b, 512, bk, "dw_down")
    dw_o = _matmul_tn(mix, dx2b, 512, bk, "dw_o")
    dzc, dob, doa, dot, qab, drow, dw_conv, dg_attn, dg_conv = _bwd_out(
        dx2b, o, zc, fcol, lcol, qa, w_o_f, g_attn_out, g_conv_out, w_conv_p, t)
    dq, dfq, r_up, r_dn, r_fc, r_o = _attn_bwd_dq(
        qab, ka, qkv, kt, dob, drow, t,
        [_col_blocks(dw_up), dw_dn.reshape(8, D_FF // 8, D_MODEL), _col_blocks(dw_fc[:3]),
         dw_o.reshape(8, D_MODEL // 8, D_MODEL)])
    dk, dv, dfk = _attn_bwd_dkv(qab, doa, ka, qkv, qt, dot, t)
    gx, dfb, dg_mix, db_f = _bwd_in(xs, dx2, dzc, dq, dk, dv, dfq, dfk, fpre, w_zc, w_qkv, w_f, g_mix, tm)
    dw_zc = _matmul_tn(h1, dzc, 3 * CONV_CH, bk, "dw_in_conv")
    dw_q = _matmul_tn(h1, dq, ATTN_W, bk, "dw_in_q")
    dw_k = _matmul_tn(h1, dk, ATTN_W, bk, "dw_in_k")
    dw_v = _matmul_tn(h1, dv, ATTN_W, bk, "dw_in_v")
    dw_f = _matmul_tn(h1, dfb, LANES, bk, "dw_in_f")
    dw_in = jnp.concatenate([dw_zc, dw_q, dw_k, dw_v, dw_f[:, :N_HEADS]], axis=1)
    small = dict(g_mix=dg_mix, b_f=db_f[:, :N_HEADS], g_conv_out=dg_conv, g_attn_out=dg_attn, g_ffn=dg_ffn,
                 g_final=dg_final)
    r_in, r_conv, r_small = _all_to_all(
        [_col_blocks(dw_in), _col_blocks(dw_conv[:3]), jnp.broadcast_to(_pack_small(small), (8, _SMALL_ROWS, LANES))],
        "all_to_all_grads")
    big = ("w_in", "w_o", "w_up", "w_down", "w_conv", "w_ffn_conv")
    recv = [r_in, r_o, r_up, r_dn, r_conv, r_fc, r_small]

    rows = dict(w_in=256, w_o=128, w_up=256, w_down=176, w_conv=3, w_ffn_conv=3)
    g_out, d_out, m_out, v_out = {}, {}, {}, {}
    for a, k in enumerate(big):
        res = _adamw(recv[a], w[k], m[k], v[k], rows[k], "adamw_" + k)
        g_out[k], d_out[k], m_out[k], v_out[k] = [r.reshape(shapes[k]) for r in res]
    res = _adamw(recv[6], _pack_small(w), _pack_small(m), _pack_small(v), _SMALL_ROWS, "adamw_gains")
    for dst, packed in zip((g_out, d_out, m_out, v_out), res):
        dst.update(_unpack_small(packed, shapes))

    order = ("g_mix", "w_in", "b_f", "w_conv", "g_conv_out", "g_attn_out", "w_o", "g_ffn", "w_up", "w_ffn_conv",
             "w_down", "g_final")
    return (loss, gx.reshape(x.shape), *[g_out[k] for k in order], *[d_out[k] for k in order],
            *[m_out[k] for k in order], *[v_out[k] for k in order])
```

```python
import functools

import jax
import jax.numpy as jnp
import numpy as np
from jax import lax
from jax.experimental import pallas as pl
from jax.experimental.pallas import tpu as pltpu

F32 = jnp.float32
BF = jnp.bfloat16
HI = lax.Precision.HIGHEST
MESH = pl.DeviceIdType.MESH

D_MODEL = 1024
CONV_CH = 512
ATTN_W = 512
N_HEADS = 8
HEAD_DIM = 64
D_FF = 2816
FF_CHUNK = 256
N_FF_CHUNKS = D_FF // FF_CHUNK
IN_COLS = 3080
EPS = 1e-6
NEG = -1e30
LANES = 128
VMEM_BIG = 56 * 1024 * 1024
VMEM_MID = 40 * 1024 * 1024

ADAM_LR = 0.001
ADAM_B1 = 0.9
ADAM_B2 = 0.999
ADAM_EPS = 1e-08
ADAM_WD = 0.01
ADAM_STEP = 10

NT = (((1,), (1,)), ((), ()))
TN = (((0,), (0,)), ((), ()))


def _dot(a, b):
    return jnp.dot(a, b, preferred_element_type=F32)


def _dot_nt(a, b):
    return lax.dot_general(a, b, NT, preferred_element_type=F32)


def _dot_tn(a, b):
    return lax.dot_general(a, b, TN, preferred_element_type=F32)


def _params(vmem=None, parallel=False):
    return pltpu.CompilerParams(
        dimension_semantics=None, vmem_limit_bytes=vmem)


def _rows(shape):
    return lax.broadcasted_iota(jnp.int32, shape, 0)


def _cols(shape):
    return lax.broadcasted_iota(jnp.int32, shape, 1)


def _shift_down(u, prev, k):
    n = prev.shape[0]
    out = pltpu.roll(u, k, 0)
    row = _rows(u.shape)
    for r in range(k):
        out = jnp.where(row == r, prev[n - k + r:n - k + r + 1, :].astype(u.dtype), out)
    return out


def _shift_up(u, nxt, k):
    tm = u.shape[0]
    out = pltpu.roll(u, tm - k, 0)
    row = _rows(u.shape)
    for r in range(k):
        out = jnp.where(row == tm - k + r, nxt[r:r + 1, :], out)
    return out


def _conv3(u, prev, w):
    u1 = _shift_down(u, prev, 1)
    u2 = _shift_down(u, prev, 2)
    return w[0:1, :] * u2 + w[1:2, :] * u1 + w[2:3, :] * u, u1, u2


def _conv3_bwd(d, nxt, w):
    return w[2:3, :] * d + w[1:2, :] * _shift_up(d, nxt, 1) + w[0:1, :] * _shift_up(d, nxt, 2)


def _rms(x):
    return lax.rsqrt(jnp.mean(x * x, axis=-1, keepdims=True) + EPS)


def _rms_bwd(x, r, dyg):
    return r * dyg - x * (r * r * r) * jnp.mean(dyg * x, axis=-1, keepdims=True)


def _full(shape):
    nd = len(shape)
    return pl.BlockSpec(shape, lambda i, _n=nd: (0,) * _n)


ONES_LANE = 24


def _bias_scatter():
    sc = np.zeros((LANES, 2 * N_HEADS * LANES), np.float32)
    koff = N_HEADS * LANES
    for h in range(N_HEADS):
        aux = HEAD_DIM * (1 - h % 2)
        for j in range(3):
            sc[8 * j + h, LANES * h + aux + j] = 1.0
            sc[ONES_LANE, koff + LANES * h + aux + j] = 1.0
            sc[ONES_LANE, LANES * h + aux + 3 + j] = 1.0
            sc[8 * j + h, koff + LANES * h + aux + 3 + j] = -1.0
    return jnp.asarray(sc, BF)


def _split_parts(v, one):
    hi = v.astype(BF).astype(F32)
    rest = v - hi
    mid = rest.astype(BF).astype(F32)
    lo = (rest - mid).astype(BF).astype(F32)
    parts = hi + pltpu.roll(mid, 8, 1) + pltpu.roll(lo, 16, 1)
    return jnp.where(_cols(parts.shape) == ONES_LANE, one, parts).astype(BF)


def _fwd_in(x, g_mix, w_zc, w_qkv, w_f, b_f, w_conv, g_conv_out, tm):
    s = x.shape[0]
    nb = s // tm
    aw = N_HEADS * LANES

    def body(x_ref, gm_ref, wzc_ref, wqkv_ref, wf_ref, bf_ref, wc_ref, gco_ref, sc_ref,
             h1_ref, zc_ref, qkv_ref, fpre_ref, fcol_ref, nc_ref, qa_ref, ka_ref, qt_ref, kt_ref, vt_ref,
             cu_ref, cf_ref):
        i = pl.program_id(0)

        @pl.when(i == 0)
        def _():
            cu_ref[...] = jnp.zeros_like(cu_ref)
            cf_ref[...] = jnp.zeros_like(cf_ref)

        xv = x_ref[...]
        hb = (xv * _rms(xv) * gm_ref[...]).astype(BF)
        h1_ref[...] = hb
        zc = _dot(hb, wzc_ref[...])
        zc_ref[...] = zc
        qkv = _dot(hb, wqkv_ref[...])
        qkv = jnp.where(_cols(qkv.shape) < ATTN_W, qkv * 0.125, qkv)
        qkvb = qkv.astype(BF)
        qkv_ref[...] = qkvb
        qt_ref[0] = qkv[:, :ATTN_W].T.astype(BF)
        kt_ref[0] = qkv[:, ATTN_W:2 * ATTN_W].T.astype(BF)
        vt_ref[0] = qkv[:, 2 * ATTN_W:].T.astype(BF)

        gb, gc, xc = zc[:, :CONV_CH], zc[:, CONV_CH:2 * CONV_CH], zc[:, 2 * CONV_CH:]
        u = gc * xc
        cv, _, _ = _conv3(u, cu_ref[...], wc_ref[...])
        cu_ref[...] = u[tm - 8:, :]
        y = gb * cv
        nc_ref[...] = (y * _rms(y) * gco_ref[...]).astype(BF)

        fpre = _dot(hb, wf_ref[...]) + bf_ref[...]
        fpre_ref[...] = fpre
        logf = jnp.minimum(fpre, 0.0) - jnp.log1p(jnp.exp(-jnp.abs(fpre)))
        logf = jnp.where(_cols(logf.shape) < N_HEADS, logf, 0.0)
        tri = (_rows((tm, tm)) >= _cols((tm, tm))).astype(F32)
        fcol = jnp.dot(tri, logf, precision=HI, preferred_element_type=F32) + cf_ref[...]
        cf_ref[...] = fcol[tm - 1:tm, :]
        fcol_ref[...] = fcol

        feat = _dot(_split_parts(fcol, 1.0), sc_ref[...]).astype(BF)
        lane = _cols((tm, LANES))
        for h in range(N_HEADS):
            hp, hh = divmod(h, 2)
            own = (lane >= HEAD_DIM * hh) & (lane < HEAD_DIM * (hh + 1))
            hs = slice(LANES * h, LANES * (h + 1))
            qa_ref[:, hs] = jnp.where(own, qkvb[:, LANES * hp:LANES * (hp + 1)], feat[:, hs])
            ka_ref[:, hs] = jnp.where(own, qkvb[:, ATTN_W + LANES * hp:ATTN_W + LANES * (hp + 1)],
                                      feat[:, aw + LANES * h:aw + LANES * (h + 1)])

    blk = lambda c: pl.BlockSpec((tm, c), lambda i: (i, 0))
    return pl.pallas_call(
        body, name="fwd_in", grid=(nb,),
        in_specs=[blk(D_MODEL), _full((1, D_MODEL)), _full(w_zc.shape), _full(w_qkv.shape), _full(w_f.shape),
                  _full((1, LANES)), _full((8, CONV_CH)), _full((1, CONV_CH)), _full((LANES, 2 * aw))],
        out_specs=[blk(D_MODEL), blk(3 * CONV_CH), blk(3 * ATTN_W), blk(LANES), blk(LANES),
                   blk(CONV_CH), blk(aw), blk(aw)] + [pl.BlockSpec((1, ATTN_W, tm), lambda i: (i, 0, 0))] * 3,
        out_shape=[jax.ShapeDtypeStruct((s, D_MODEL), BF), jax.ShapeDtypeStruct((s, 3 * CONV_CH), F32),
                   jax.ShapeDtypeStruct((s, 3 * ATTN_W), BF), jax.ShapeDtypeStruct((s, LANES), F32),
                   jax.ShapeDtypeStruct((s, LANES), F32),
                   jax.ShapeDtypeStruct((s, CONV_CH), BF), jax.ShapeDtypeStruct((s, aw), BF),
                   jax.ShapeDtypeStruct((s, aw), BF)] + [jax.ShapeDtypeStruct((nb, ATTN_W, tm), BF)] * 3,
        scratch_shapes=[pltpu.VMEM((8, CONV_CH), F32), pltpu.VMEM((1, LANES), F32)],
        compiler_params=_params(VMEM_MID),
    )(x, g_mix, w_zc, w_qkv, w_f, b_f, w_conv, g_conv_out, _bias_scatter())


def _head_mask(shape, hh):
    lane = _cols(shape)
    return (lane >= HEAD_DIM * hh) & (lane < HEAD_DIM * (hh + 1))


def _attn_fwd(qa, ka, vt, t, shards):
    s = qa.shape[0]
    nb = s // t
    n = len(shards)

    def body(qa_ref, ka_any, vt_any, *refs):
        shard_refs, (o_ref, l_ref), gathered = refs[:n], refs[n:n + 2], refs[n + 2:2 * n + 2]
        ka_scr, vt_scr, m_scr, acc_scr, ea_scr, eb_scr, sem = refs[2 * n + 2:2 * n + 9]
        i = pl.program_id(0)
        start, forward, finish = _gather_phases(shard_refs, gathered, *refs[2 * n + 9:])
        pl.when(i == 0)(start)
        pl.when(i == nb // 2)(forward)
        _load_weights(i, [(ka_any, ka_scr), (vt_any, vt_scr)], sem)

        causal_t = _rows((t, t)) <= _cols((t, t))
        row = _rows((LANES, t))
        lrows = jnp.zeros((LANES, t), F32)
        for hp in range(N_HEADS // 2):
            ps = slice(hp * LANES, (hp + 1) * LANES)
            m_scr[...] = jnp.full(m_scr.shape, NEG, F32)
            acc_scr[...] = jnp.zeros(acc_scr.shape, F32)
            heads = [dict(hh=hh, hs=slice((2 * hp + hh) * LANES, (2 * hp + hh + 1) * LANES),
                          aux=HEAD_DIM * (1 - hh), own=(row >= HEAD_DIM * hh) & (row < HEAD_DIM * (hh + 1)),
                          fill=jnp.where(row == HEAD_DIM * (1 - hh), 1.0, 0.0).astype(BF),
                          qh=qa_ref[:, (2 * hp + hh) * LANES:(2 * hp + hh + 1) * LANES]) for hh in range(2)]

            def scores(kb, dst, heads=heads):
                rs = pl.ds(pl.multiple_of(kb * t, t), t)
                for hd in heads:
                    dst[hd["hh"]] = _dot_nt(ka_scr[rs, hd["hs"]], hd["qh"])

            def consume(kb, src, masked, ps=ps, heads=heads):
                vt2 = vt_scr[kb, ps, :]
                m_olds = [m_scr[hd["hh"]:hd["hh"] + 1, :] for hd in heads]
                accs = [acc_scr[hd["hh"]] for hd in heads]
                m_news, acc_news = [], []
                for hd, m_old, acc in zip(heads, m_olds, accs):
                    e = src[hd["hh"]]
                    if masked:
                        e = jnp.where(causal_t, e, NEG)
                    m_new = jnp.maximum(m_old, jnp.max(e, axis=0, keepdims=True))
                    p = jnp.exp(e - m_new).astype(BF)
                    vta = jnp.where(hd["own"], vt2, hd["fill"])
                    acc_news.append(jnp.exp(m_old - m_new) * acc + _dot(vta, p))
                    m_news.append(m_new)
                for hd, m_new, acc in zip(heads, m_news, acc_news):
                    acc_scr[hd["hh"]] = acc
                    m_scr[hd["hh"]:hd["hh"] + 1, :] = m_new

            scores(0, ea_scr)

            def two_blocks(j, carry, scores=scores, consume=consume):
                kb = 2 * j
                scores(kb + 1, eb_scr)
                consume(kb, ea_scr, False)
                scores(kb + 2, ea_scr)
                consume(kb + 1, eb_scr, False)
                return carry

            lax.fori_loop(0, i // 2, two_blocks, 0)

            @pl.when(i % 2 == 0)
            def _(consume=consume):
                consume(i, ea_scr, True)

            @pl.when(i % 2 == 1)
            def _(scores=scores, consume=consume):
                scores(i, eb_scr)
                consume(i - 1, ea_scr, False)
                consume(i, eb_scr, True)

            o_pair = None
            for hd in heads:
                hh = hd["hh"]
                acc = acc_scr[hh]
                denom = acc[hd["aux"]:hd["aux"] + 1, :]
                o_h = acc / denom
                lrows = jnp.where(row == 2 * hp + hh, m_scr[hh:hh + 1, :] + jnp.log(denom), lrows)
                o_pair = o_h if hh == 0 else jnp.where(row < HEAD_DIM, o_pair, o_h)
            o_ref[:, ps] = o_pair.T
        l_ref[...] = lrows.T
        pl.when(i == nb - 1)(finish)

    any_ = pl.BlockSpec(memory_space=pl.ANY)
    return pl.pallas_call(
        body, name="attn_fwd", grid=(nb,),
        in_specs=[pl.BlockSpec((t, N_HEADS * LANES), lambda i: (i, 0)), any_, any_] + [any_] * n,
        out_specs=[pl.BlockSpec((t, ATTN_W), lambda i: (i, 0)), pl.BlockSpec((t, LANES), lambda i: (i, 0))]
        + [any_] * n,
        out_shape=[jax.ShapeDtypeStruct((s, ATTN_W), F32), jax.ShapeDtypeStruct((s, LANES), F32)]
        + _gathered_shapes(shards),
        scratch_shapes=[pltpu.VMEM(ka.shape, BF), pltpu.VMEM(vt.shape, BF), pltpu.VMEM((2, t), F32),
                        pltpu.VMEM((2, LANES, t), F32), pltpu.VMEM((2, t, t), F32), pltpu.VMEM((2, t, t), F32),
                        pltpu.SemaphoreType.DMA((2,))] + _exchange_sems(n),
        compiler_params=_params(VMEM_BIG),
    )(qa, ka, vt, *shards)


def _fwd_out(x, nc, o, g_attn_out, w_o, g_ffn, tm):
    s = x.shape[0]

    def body(x_ref, nc_ref, o_ref, ga_ref, wo_ref, gf_ref, x2_ref, h2_ref, mix_ref):
        ov = o_ref[...]
        na = (ov * _rms(ov) * ga_ref[...]).astype(BF)
        ncv = nc_ref[...]
        mix_ref[:, :CONV_CH] = ncv
        mix_ref[:, CONV_CH:] = na
        x2 = x_ref[...] + _dot(ncv, wo_ref[:CONV_CH, :]) + _dot(na, wo_ref[CONV_CH:, :])
        x2_ref[...] = x2
        h2_ref[...] = (x2 * _rms(x2) * gf_ref[...]).astype(BF)

    blk = lambda c: pl.BlockSpec((tm, c), lambda i: (i, 0))
    return pl.pallas_call(
        body, name="fwd_out", grid=(s // tm,),
        in_specs=[blk(D_MODEL), blk(CONV_CH), blk(ATTN_W), _full((1, ATTN_W)), _full(w_o.shape), _full((1, D_MODEL))],
        out_specs=[blk(D_MODEL), blk(D_MODEL), blk(D_MODEL)],
        out_shape=[jax.ShapeDtypeStruct((s, D_MODEL), F32), jax.ShapeDtypeStruct((s, D_MODEL), BF),
                   jax.ShapeDtypeStruct((s, D_MODEL), BF)],
        compiler_params=_params(VMEM_MID),
    )(x, nc, o, g_attn_out, w_o, g_ffn)


def _load_weights(i, pairs, sem):
    @pl.when(i == 0)
    def _():
        cps = [pltpu.make_async_copy(src, dst, sem.at[n]) for n, (src, dst) in enumerate(pairs)]
        for cp in cps:
            cp.start()
        for cp in cps:
            cp.wait()


def _ff_cols(j):
    return (slice(j * FF_CHUNK, (j + 1) * FF_CHUNK), slice(D_FF + j * FF_CHUNK, D_FF + (j + 1) * FF_CHUNK))


def _fwd_ffn(h2, x2, tgt, w_up, w_ffn_conv, w_dn, g_final, tm):
    s = x2.shape[0]

    def body(h2_ref, x2_ref, tgt_ref, wfc_ref, gfin_ref, wup_any, wdn_any,
             u0_ref, act_ref, dx3_ref, dx3b_ref, loss_ref, dgfin_ref,
             wup, wdn, carry, sem):
        i = pl.program_id(0)
        _load_weights(i, [(wup_any, wup), (wdn_any, wdn)], sem)

        @pl.when(i == 0)
        def _():
            carry[...] = jnp.zeros_like(carry)
            loss_ref[...] = jnp.zeros_like(loss_ref)
            dgfin_ref[...] = jnp.zeros_like(dgfin_ref)

        hb = h2_ref[...]

        def up(j):
            return [_dot(hb, wup[:, cc]) for cc in _ff_cols(j)]

        nxt = up(0)
        down = None
        for j in range(N_FF_CHUNKS):
            cur = nxt
            if j + 1 < N_FF_CHUNKS:
                nxt = up(j + 1)
            parts = []
            for cc, u0 in zip(_ff_cols(j), cur):
                u0_ref[:, cc] = u0.astype(BF)
                uu, _, _ = _conv3(u0, carry[:, cc], wfc_ref[:, cc])
                carry[:, cc] = u0[tm - 8:, :]
                parts.append(uu)
            ua, ug = parts
            act = (ug * jax.nn.sigmoid(ug) * ua).astype(BF)
            ca = _ff_cols(j)[0]
            act_ref[:, ca] = act
            part = _dot(act, wdn[ca, :])
            down = part if down is None else down + part

        x3 = x2_ref[...] + down
        r3 = _rms(x3)
        gfin = gfin_ref[...]
        xn = x3 * r3
        diff = xn * gfin - tgt_ref[...]
        loss_ref[...] += jnp.sum(jnp.sum(diff * diff, axis=-1, keepdims=True), axis=0, keepdims=True) * (0.5 / D_MODEL)
        dy = diff * (1.0 / D_MODEL)
        dgfin_ref[...] += jnp.sum(dy * xn, axis=0, keepdims=True)
        dx3 = _rms_bwd(x3, r3, dy * gfin)
        dx3_ref[...] = dx3
        dx3b_ref[...] = dx3.astype(BF)

    blk = lambda c: pl.BlockSpec((tm, c), lambda i: (i, 0))
    any_ = pl.BlockSpec(memory_space=pl.ANY)
    return pl.pallas_call(
        body, name="fwd_ffn", grid=(s // tm,),
        in_specs=[blk(D_MODEL), blk(D_MODEL), blk(D_MODEL), _full((8, 2 * D_FF)), _full((1, D_MODEL)), any_, any_],
        out_specs=[blk(2 * D_FF), blk(D_FF), blk(D_MODEL), blk(D_MODEL), _full((1, 1)), _full((1, D_MODEL))],
        out_shape=[jax.ShapeDtypeStruct((s, 2 * D_FF), BF), jax.ShapeDtypeStruct((s, D_FF), BF),
                   jax.ShapeDtypeStruct((s, D_MODEL), F32), jax.ShapeDtypeStruct((s, D_MODEL), BF),
                   jax.ShapeDtypeStruct((1, 1), F32), jax.ShapeDtypeStruct((1, D_MODEL), F32)],
        scratch_shapes=[pltpu.VMEM(w_up.shape, BF), pltpu.VMEM(w_dn.shape, BF), pltpu.VMEM((8, 2 * D_FF), F32),
                        pltpu.SemaphoreType.DMA((2,))],
        compiler_params=_params(VMEM_BIG),
    )(h2, x2, tgt, w_ffn_conv, g_final, w_up, w_dn)


def _bwd_ffn(dx3, dx3b, u0, x2, w_up, w_ffn_conv, w_dn, g_ffn, tm):
    s = x2.shape[0]
    nb = s // tm
    hr = 16

    def body(dx3_ref, dx3b_ref, u0_ref, halo_ref, x2_ref, wfc_ref, gf_ref, wup_any, wdn_any,
             du0_ref, dx2_ref, dx2b_ref, dwfc_ref, dgf_ref,
             wup, wdn, carry, sem):
        i = pl.program_id(0)
        rb = nb - 1 - i
        _load_weights(i, [(wup_any, wup), (wdn_any, wdn)], sem)

        @pl.when(i == 0)
        def _():
            carry[...] = jnp.zeros_like(carry)
            dwfc_ref[...] = jnp.zeros_like(dwfc_ref)
            dgf_ref[...] = jnp.zeros_like(dgf_ref)

        db = dx3b_ref[...]
        live = (rb > 0).astype(F32)

        def dact_of(j):
            return _dot(db, wdn[:, _ff_cols(j)[0]])

        nxt = dact_of(0)
        dh2 = None
        for j in range(N_FF_CHUNKS):
            ca, cg = _ff_cols(j)
            dact = nxt
            if j + 1 < N_FF_CHUNKS:
                nxt = dact_of(j + 1)
            fw = []
            for cc in (ca, cg):
                u0 = u0_ref[:, cc].astype(F32)
                prev = halo_ref[:, cc].astype(F32) * live
                fw.append((u0,) + _conv3(u0, prev, wfc_ref[:, cc]))
            (u0a, ua, u1a, u2a), (u0g, ug, u1g, u2g) = fw
            sg = jax.nn.sigmoid(ug)
            da = dact * (ug * sg)
            dg = dact * ua * (sg * (1.0 + ug * (1.0 - sg)))
            for cc, d, u0c, u1c, u2c in ((ca, da, u0a, u1a, u2a), (cg, dg, u0g, u1g, u2g)):
                dwfc_ref[0:1, cc] += jnp.sum(d * u2c, axis=0, keepdims=True)
                dwfc_ref[1:2, cc] += jnp.sum(d * u1c, axis=0, keepdims=True)
                dwfc_ref[2:3, cc] += jnp.sum(d * u0c, axis=0, keepdims=True)
                du0 = _conv3_bwd(d, carry[:, cc], wfc_ref[:, cc]).astype(BF)
                carry[:, cc] = d[:8, :]
                du0_ref[:, cc] = du0
                part = _dot(du0, wup[cc, :])
                dh2 = part if dh2 is None else dh2 + part

        x2v = x2_ref[...]
        r2 = _rms(x2v)
        dgf_ref[...] += jnp.sum(dh2 * (x2v * r2), axis=0, keepdims=True)
        dx2 = dx3_ref[...] + _rms_bwd(x2v, r2, dh2 * gf_ref[...])
        dx2_ref[...] = dx2
        dx2b_ref[...] = dx2.astype(BF)

    blk = lambda c: pl.BlockSpec((tm, c), lambda i: (nb - 1 - i, 0))
    halo = pl.BlockSpec((hr, 2 * D_FF), lambda i: (jnp.maximum((nb - 1 - i) * (tm // hr) - 1, 0), 0))
    any_ = pl.BlockSpec(memory_space=pl.ANY)
    return pl.pallas_call(
        body, name="bwd_ffn", grid=(nb,),
        in_specs=[blk(D_MODEL), blk(D_MODEL), blk(2 * D_FF), halo, blk(D_MODEL), _full((8, 2 * D_FF)),
                  _full((1, D_MODEL)), any_, any_],
        out_specs=[blk(2 * D_FF), blk(D_MODEL), blk(D_MODEL), _full((8, 2 * D_FF)), _full((1, D_MODEL))],
        out_shape=[jax.ShapeDtypeStruct((s, 2 * D_FF), BF), jax.ShapeDtypeStruct((s, D_MODEL), F32),
                   jax.ShapeDtypeStruct((s, D_MODEL), BF), jax.ShapeDtypeStruct((8, 2 * D_FF), F32),
                   jax.ShapeDtypeStruct((1, D_MODEL), F32)],
        scratch_shapes=[pltpu.VMEM(w_up.shape, BF), pltpu.VMEM(w_dn.shape, BF), pltpu.VMEM((8, 2 * D_FF), F32),
                        pltpu.SemaphoreType.DMA((2,))],
        compiler_params=_params(VMEM_BIG),
    )(dx3, dx3b, u0, u0, x2, w_ffn_conv, g_ffn, w_up, w_dn)


def _bwd_out(dx2b, o, zc, fcol, lcol, qa, w_o, g_attn_out, g_conv_out, w_conv, tm):
    s = o.shape[0]
    nb = s // tm
    aw = N_HEADS * LANES

    def body(dx2b_ref, o_ref, zc_ref, halo_ref, fcol_ref, lcol_ref, qa_ref, wo_ref, ga_ref, gco_ref, wc_ref, sc_ref,
             dzc_ref, dob_ref, doa_ref, dot_ref, qab_ref, drow_ref, dwc_ref, dga_ref, dgco_ref, carry):
        i = pl.program_id(0)
        rb = nb - 1 - i

        @pl.when(i == 0)
        def _():
            carry[...] = jnp.zeros_like(carry)
            dwc_ref[...] = jnp.zeros_like(dwc_ref)
            dga_ref[...] = jnp.zeros_like(dga_ref)
            dgco_ref[...] = jnp.zeros_like(dgco_ref)

        dmix = _dot_nt(dx2b_ref[...], wo_ref[...])
        dnc, dna = dmix[:, :CONV_CH], dmix[:, CONV_CH:]

        ov = o_ref[...]
        ra = _rms(ov)
        dga_ref[...] += jnp.sum(dna * (ov * ra), axis=0, keepdims=True)
        do = _rms_bwd(ov, ra, dna * ga_ref[...])
        dob = do.astype(BF)
        dob_ref[...] = dob
        dot_ref[0] = do.T.astype(BF)
        sel = (_rows((ATTN_W, LANES)) // HEAD_DIM == _cols((ATTN_W, LANES))).astype(F32)
        delta = jnp.dot(do * ov, sel, precision=HI, preferred_element_type=F32)
        drow_ref[0] = delta.T[:N_HEADS, :]
        featc = _dot(_split_parts(fcol_ref[...] - lcol_ref[...], 1.0), sc_ref[...]).astype(BF)
        featd = _dot(_split_parts(-delta, 0.0), sc_ref[...]).astype(BF)
        lane = _cols((tm, LANES))
        for h in range(N_HEADS):
            hp, hh = divmod(h, 2)
            own = (lane >= HEAD_DIM * hh) & (lane < HEAD_DIM * (hh + 1))
            hs = slice(LANES * h, LANES * (h + 1))
            qab_ref[:, hs] = jnp.where(own, qa_ref[:, hs], featc[:, hs])
            doa_ref[:, hs] = jnp.where(own, dob[:, LANES * hp:LANES * (hp + 1)], featd[:, hs])

        zc_v = zc_ref[...]
        gb, gc, xc = zc_v[:, :CONV_CH], zc_v[:, CONV_CH:2 * CONV_CH], zc_v[:, 2 * CONV_CH:]
        hal = halo_ref[...] * (rb > 0).astype(F32)
        u = gc * xc
        prev = hal[:, CONV_CH:2 * CONV_CH] * hal[:, 2 * CONV_CH:]
        wc = wc_ref[...]
        cv, u1, u2 = _conv3(u, prev, wc)
        y = gb * cv
        rc = _rms(y)
        dgco_ref[...] += jnp.sum(dnc * (y * rc), axis=0, keepdims=True)
        dy = _rms_bwd(y, rc, dnc * gco_ref[...])
        dcv = dy * gb
        dwc_ref[0:1, :] += jnp.sum(dcv * u2, axis=0, keepdims=True)
        dwc_ref[1:2, :] += jnp.sum(dcv * u1, axis=0, keepdims=True)
        dwc_ref[2:3, :] += jnp.sum(dcv * u, axis=0, keepdims=True)
        du = _conv3_bwd(dcv, carry[...], wc)
        carry[...] = dcv[:8, :]
        dzc_ref[:, :CONV_CH] = (dy * cv).astype(BF)
        dzc_ref[:, CONV_CH:2 * CONV_CH] = (du * xc).astype(BF)
        dzc_ref[:, 2 * CONV_CH:] = (du * gc).astype(BF)

    blk = lambda c: pl.BlockSpec((tm, c), lambda i: (nb - 1 - i, 0))
    halo = pl.BlockSpec((8, 3 * CONV_CH), lambda i: (jnp.maximum((nb - 1 - i) * (tm // 8) - 1, 0), 0))
    row = pl.BlockSpec((1, N_HEADS, tm), lambda i: (nb - 1 - i, 0, 0))
    tr = pl.BlockSpec((1, ATTN_W, tm), lambda i: (nb - 1 - i, 0, 0))
    return pl.pallas_call(
        body, name="bwd_out", grid=(nb,),
        in_specs=[blk(D_MODEL), blk(ATTN_W), blk(3 * CONV_CH), halo, blk(LANES), blk(LANES), blk(aw),
                  _full(w_o.shape), _full((1, ATTN_W)), _full((1, CONV_CH)), _full((8, CONV_CH)), _full((LANES, aw))],
        out_specs=[blk(3 * CONV_CH), blk(ATTN_W), blk(aw), tr, blk(aw), row,
                   _full((8, CONV_CH)), _full((1, ATTN_W)), _full((1, CONV_CH))],
        out_shape=[jax.ShapeDtypeStruct((s, 3 * CONV_CH), BF), jax.ShapeDtypeStruct((s, ATTN_W), BF),
                   jax.ShapeDtypeStruct((s, aw), BF), jax.ShapeDtypeStruct((nb, ATTN_W, tm), BF),
                   jax.ShapeDtypeStruct((s, aw), BF), jax.ShapeDtypeStruct((nb, N_HEADS, tm), F32),
                   jax.ShapeDtypeStruct((8, CONV_CH), F32), jax.ShapeDtypeStruct((1, ATTN_W), F32),
                   jax.ShapeDtypeStruct((1, CONV_CH), F32)],
        scratch_shapes=[pltpu.VMEM((8, CONV_CH), F32)],
        compiler_params=_params(VMEM_MID),
    )(dx2b, o, zc, zc, fcol, lcol, qa, w_o, g_attn_out, g_conv_out, w_conv, _bias_scatter()[:, :aw])


def _attn_bwd_dq(qab, ka, qkv, kt, dob, drow, t, blocks):
    s = qab.shape[0]
    nb = s // t
    n = len(blocks)

    def body(qab_ref, do_ref, drow_ref, ka_any, qkv_any, kt_any, *refs):
        block_refs, (dq_ref, dfq_ref), received = refs[:n], refs[n:n + 2], refs[n + 2:2 * n + 2]
        ka_scr, v_scr, kt_scr, acc_scr, sem = refs[2 * n + 2:2 * n + 7]
        i = pl.program_id(0)
        start, finish = _scatter_phases(block_refs, received, *refs[2 * n + 7:])
        pl.when(i == 0)(start)
        _load_weights(i, [(ka_any, ka_scr), (qkv_any.at[:, pl.ds(2 * ATTN_W, ATTN_W)], v_scr), (kt_any, kt_scr)], sem)

        causal_t = _rows((t, t)) <= _cols((t, t))
        row = _rows((LANES, t))
        lane = _cols((t, LANES))
        dfrows = jnp.zeros((LANES, t), F32)
        for hp in range(N_HEADS // 2):
            ps = slice(hp * LANES, (hp + 1) * LANES)
            acc_scr[...] = jnp.zeros(acc_scr.shape, F32)
            do2 = do_ref[:, ps]
            heads = [dict(hh=hh, hs=slice((2 * hp + hh) * LANES, (2 * hp + hh + 1) * LANES),
                          aux=HEAD_DIM * (1 - hh), own=(row >= HEAD_DIM * hh) & (row < HEAD_DIM * (hh + 1)),
                          ownl=(lane >= HEAD_DIM * hh) & (lane < HEAD_DIM * (hh + 1)),
                          fill=jnp.where(row == HEAD_DIM * (1 - hh), 1.0, 0.0).astype(BF),
                          qh=qab_ref[:, (2 * hp + hh) * LANES:(2 * hp + hh + 1) * LANES],
                          dl=drow_ref[0, 2 * hp + hh:2 * hp + hh + 1, :]) for hh in range(2)]

            def tile(kb, masked, ps=ps, heads=heads, do2=do2):
                rs = pl.ds(pl.multiple_of(kb * t, t), t)
                v2 = v_scr[rs, ps]
                kt2 = kt_scr[kb, ps, :]
                es = [_dot_nt(ka_scr[rs, hd["hs"]], hd["qh"]) for hd in heads]
                dps = [_dot_nt(jnp.where(hd["ownl"], v2, jnp.zeros_like(v2)), do2) for hd in heads]
                accs = [acc_scr[hd["hh"]] for hd in heads]
                outs = []
                for hd, e, dp, acc in zip(heads, es, dps, accs):
                    if masked:
                        e = jnp.where(causal_t, e, NEG)
                    ds = (jnp.exp(e) * (dp - hd["dl"])).astype(BF)
                    outs.append(acc + _dot(jnp.where(hd["own"], kt2, hd["fill"]), ds))
                for hd, acc in zip(heads, outs):
                    acc_scr[hd["hh"]] = acc

            def loop_body(kb, carry, tile=tile):
                tile(kb, False)
                return carry

            lax.fori_loop(0, i, loop_body, 0)
            tile(i, True)
            dq_pair = None
            for hd in heads:
                acc = acc_scr[hd["hh"]]
                dfrows = jnp.where(row == 2 * hp + hd["hh"], acc[hd["aux"]:hd["aux"] + 1, :], dfrows)
                dq_pair = acc if hd["hh"] == 0 else jnp.where(row < HEAD_DIM, dq_pair, acc)
            dq_ref[:, ps] = (dq_pair * 0.125).T.astype(BF)
        dfq_ref[...] = dfrows.T
        pl.when(i == nb - 1)(finish)

    any_ = pl.BlockSpec(memory_space=pl.ANY)
    return pl.pallas_call(
        body, name="attn_bwd_dq", grid=(nb,),
        in_specs=[pl.BlockSpec((t, N_HEADS * LANES), lambda i: (i, 0)), pl.BlockSpec((t, ATTN_W), lambda i: (i, 0)),
                  pl.BlockSpec((1, N_HEADS, t), lambda i: (i, 0, 0)), any_, any_, any_] + [any_] * n,
        out_specs=[pl.BlockSpec((t, ATTN_W), lambda i: (i, 0)), pl.BlockSpec((t, LANES), lambda i: (i, 0))]
        + [any_] * n,
        out_shape=[jax.ShapeDtypeStruct((s, ATTN_W), BF), jax.ShapeDtypeStruct((s, LANES), F32)]
        + [jax.ShapeDtypeStruct(b.shape, b.dtype) for b in blocks],
        scratch_shapes=[pltpu.VMEM(ka.shape, BF), pltpu.VMEM((s, ATTN_W), BF), pltpu.VMEM(kt.shape, BF),
                        pltpu.VMEM((2, LANES, t), F32), pltpu.SemaphoreType.DMA((3,))] + _exchange_sems(n),
        compiler_params=_params(VMEM_BIG),
    )(qab, dob, drow, ka, qkv, kt, *blocks)


def _attn_bwd_dkv(qab, doa, ka, qkv, qt, dot, t):
    s = qab.shape[0]
    nb = s // t
    npair = N_HEADS // 2

    def body(ka_ref, v_ref, qab_ref, doa_ref, qt_ref, dot_ref, dk_ref, dv_ref, dfk_ref, acck, accv):
        hp = pl.program_id(0)
        j = pl.program_id(1)
        causal = _cols((t, t)) <= _rows((t, t))
        row = _rows((LANES, t))
        lane = _cols((t, LANES))
        acck[...] = jnp.zeros(acck.shape, F32)
        accv[...] = jnp.zeros(accv.shape, F32)
        v2 = v_ref[...]
        heads = []
        for hh in range(2):
            aux = HEAD_DIM * (1 - hh)
            ownl = (lane >= HEAD_DIM * hh) & (lane < HEAD_DIM * (hh + 1))
            ones3 = jnp.where((lane >= aux) & (lane < aux + 3), 1.0, 0.0).astype(BF)
            heads.append(dict(hh=hh, hs=slice(hh * LANES, (hh + 1) * LANES), aux=aux,
                              own=(row >= HEAD_DIM * hh) & (row < HEAD_DIM * (hh + 1)),
                              fill=jnp.where(row == aux, 1.0, 0.0).astype(BF),
                              kh=ka_ref[:, hh * LANES:(hh + 1) * LANES], vh=jnp.where(ownl, v2, ones3)))

        def tile(qb, masked):
            rs = pl.ds(pl.multiple_of(qb * t, t), t)
            qt2 = qt_ref[qb]
            dot2 = dot_ref[qb]
            es = [_dot_nt(qab_ref[rs, hd["hs"]], hd["kh"]) for hd in heads]
            dps = [_dot_nt(doa_ref[rs, hd["hs"]], hd["vh"]) for hd in heads]
            aks = [acck[hd["hh"]] for hd in heads]
            avs = [accv[hd["hh"]] for hd in heads]
            outs = []
            for hd, e, dp, ak, av in zip(heads, es, dps, aks, avs):
                if masked:
                    e = jnp.where(causal, e, NEG)
                p = jnp.exp(e)
                ds = (p * dp).astype(BF)
                outs.append((ak + _dot(jnp.where(hd["own"], qt2, hd["fill"]), ds), av + _dot(dot2, p.astype(BF))))
            for hd, (ak, av) in zip(heads, outs):
                acck[hd["hh"]] = ak
                accv[hd["hh"]] = av

        def loop_body(qb, carry):
            tile(qb, False)
            return carry

        tile(j, True)
        lax.fori_loop(j + 1, nb, loop_body, 0)
        dk_pair = dv_pair = None
        dfrows = jnp.zeros((LANES, t), F32)
        for hd in heads:
            ak, av = acck[hd["hh"]], accv[hd["hh"]]
            dfrows = jnp.where(row == 2 * hp + hd["hh"], -ak[hd["aux"]:hd["aux"] + 1, :], dfrows)
            dk_pair = ak if hd["hh"] == 0 else jnp.where(row < HEAD_DIM, dk_pair, ak)
            dv_pair = av if hd["hh"] == 0 else jnp.where(row < HEAD_DIM, dv_pair, av)
        dk_ref[...] = dk_pair.T.astype(BF)
        dv_ref[...] = dv_pair.T.astype(BF)
        dfk_ref[0] = dfrows.T

    pair_cols = pl.BlockSpec((s, 2 * LANES), lambda hp, j: (0, hp))
    pair_rows = pl.BlockSpec((nb, LANES, t), lambda hp, j: (0, hp, 0))
    return pl.pallas_call(
        body, name="attn_bwd_dkv", grid=(npair, nb),
        in_specs=[pl.BlockSpec((t, 2 * LANES), lambda hp, j: (j, hp)),
                  pl.BlockSpec((t, LANES), lambda hp, j: (j, 2 * npair + hp)),
                  pair_cols, pair_cols, pair_rows, pair_rows],
        out_specs=[pl.BlockSpec((t, LANES), lambda hp, j: (j, hp)), pl.BlockSpec((t, LANES), lambda hp, j: (j, hp)),
                   pl.BlockSpec((1, t, LANES), lambda hp, j: (hp, j, 0))],
        out_shape=[jax.ShapeDtypeStruct((s, ATTN_W), BF), jax.ShapeDtypeStruct((s, ATTN_W), BF),
                   jax.ShapeDtypeStruct((npair, s, LANES), F32)],
        scratch_shapes=[pltpu.VMEM((2, LANES, t), F32), pltpu.VMEM((2, LANES, t), F32)],
        compiler_params=_params(VMEM_BIG),
    )(ka, qkv, qab, doa, qt, dot)


def _bwd_in(x, dx2, dzc, dq, dk, dv, dfq, dfk, fpre, w_zc, w_qkv, w_f, g_mix, tm):
    s = x.shape[0]
    nb = s // tm

    def body(x_ref, dx2_ref, dzc_ref, dq_ref, dk_ref, dv_ref, dfq_ref, dfk_ref, fpre_ref, wzc_ref, wqkv_ref, wf_ref,
             gm_ref, gx_ref, dfb_ref, dgm_ref, dbf_ref, carry):
        i = pl.program_id(0)

        @pl.when(i == 0)
        def _():
            carry[...] = jnp.zeros_like(carry)
            dgm_ref[...] = jnp.zeros_like(dgm_ref)
            dbf_ref[...] = jnp.zeros_like(dbf_ref)

        triu = (_rows((tm, tm)) <= _cols((tm, tm))).astype(F32)
        df_cum = dfq_ref[...] + ((dfk_ref[0] + dfk_ref[1]) + (dfk_ref[2] + dfk_ref[3]))
        dlogf = jnp.dot(triu, df_cum, precision=HI, preferred_element_type=F32) + carry[...]
        carry[...] = dlogf[0:1, :]
        fpre = fpre_ref[...]
        df = jnp.where(_cols(fpre.shape) < N_HEADS, dlogf / (1.0 + jnp.exp(fpre)), 0.0)
        dbf_ref[...] += jnp.sum(df, axis=0, keepdims=True)
        dfb = df.astype(BF)
        dfb_ref[...] = dfb

        dh1 = _dot_nt(dzc_ref[...], wzc_ref[...])
        dh1 += _dot_nt(dq_ref[...], wqkv_ref[:, :ATTN_W])
        dh1 += _dot_nt(dk_ref[...], wqkv_ref[:, ATTN_W:2 * ATTN_W])
        dh1 += _dot_nt(dv_ref[...], wqkv_ref[:, 2 * ATTN_W:])
        dh1 += _dot_nt(dfb, wf_ref[...])
        xv = x_ref[...]
        r1 = _rms(xv)
        dgm_ref[...] += jnp.sum(dh1 * (xv * r1), axis=0, keepdims=True)
        gx_ref[...] = dx2_ref[...] + _rms_bwd(xv, r1, dh1 * gm_ref[...])

    blk = lambda c: pl.BlockSpec((tm, c), lambda i: (nb - 1 - i, 0))
    return pl.pallas_call(
        body, name="bwd_in", grid=(nb,),
        in_specs=[blk(D_MODEL), blk(D_MODEL), blk(3 * CONV_CH), blk(ATTN_W), blk(ATTN_W), blk(ATTN_W), blk(LANES),
                  pl.BlockSpec((N_HEADS // 2, tm, LANES), lambda i: (0, nb - 1 - i, 0)), blk(LANES),
                  _full(w_zc.shape), _full(w_qkv.shape), _full(w_f.shape), _full((1, D_MODEL))],
        out_specs=[blk(D_MODEL), blk(LANES), _full((1, D_MODEL)), _full((1, LANES))],
        out_shape=[jax.ShapeDtypeStruct((s, D_MODEL), F32), jax.ShapeDtypeStruct((s, LANES), BF),
                   jax.ShapeDtypeStruct((1, D_MODEL), F32), jax.ShapeDtypeStruct((1, LANES), F32)],
        scratch_shapes=[pltpu.VMEM((1, LANES), F32)],
        compiler_params=_params(VMEM_MID),
    )(x, dx2, dzc, dq, dk, dv, dfq, dfk, fpre, w_zc, w_qkv, w_f, g_mix)


def _matmul_tn(a, b, bn, bk, name):
    s, m = a.shape
    n = b.shape[1]

    def body(a_ref, b_ref, o_ref):
        @pl.when(pl.program_id(1) == 0)
        def _():
            o_ref[...] = jnp.zeros_like(o_ref)

        o_ref[...] += _dot_tn(a_ref[...], b_ref[...])

    return pl.pallas_call(
        body, name=name, grid=(n // bn, s // bk),
        in_specs=[pl.BlockSpec((bk, m), lambda jn, k: (k, 0)), pl.BlockSpec((bk, bn), lambda jn, k: (k, jn))],
        out_specs=pl.BlockSpec((m, bn), lambda jn, k: (0, jn)),
        out_shape=jax.ShapeDtypeStruct((m, n), F32),
        compiler_params=_params(VMEM_MID),
    )(a, b)


def _flip(v, bit):
    return 1 - v if bit else v


def _all_gather(shards):
    n = len(shards)

    def body(*refs):
        start, forward, finish = _gather_phases(refs[:n], refs[n:2 * n], *refs[2 * n:])
        start()
        forward()
        finish()

    any_ = pl.BlockSpec(memory_space=pl.ANY)
    return pl.pallas_call(
        body, name="all_gather_weights",
        in_specs=[any_] * n, out_specs=[any_] * n,
        out_shape=_gathered_shapes(shards), scratch_shapes=_exchange_sems(n),
    )(*shards)


def _gathered_shapes(shards):
    return [jax.ShapeDtypeStruct((8,) + sh.shape, sh.dtype) for sh in shards]


def _exchange_sems(n):
    return [pltpu.SemaphoreType.DMA((7 * n,)), pltpu.SemaphoreType.DMA((7 * n,)), pltpu.SemaphoreType.DMA((n,))]


def _gather_phases(src, out, send_sems, recv_sems, loc_sems):
    n = len(src)
    x, y, c = lax.axis_index("x"), lax.axis_index("y"), lax.axis_index("c")
    me, sibling = (x, y, c), (x, y, 1 - c)
    chips = [(1 - x, y), (x, 1 - y), (1 - x, 1 - y)]

    def slot(a, px, py, pc):
        return out[a].at[4 * px + 2 * py + pc]

    def copy(a, k, block, to, from_src=False):
        return pltpu.make_async_remote_copy(
            src_ref=src[a] if from_src else slot(a, *block), dst_ref=slot(a, *block),
            send_sem=send_sems.at[7 * a + k], recv_sem=recv_sems.at[7 * a + k],
            device_id=to, device_id_type=MESH)

    def local(a):
        return pltpu.make_async_copy(src[a], slot(a, *me), loc_sems.at[a])

    def first(a):
        return [copy(a, 0, me, sibling, True)] + [copy(a, 1 + j, me, (*chip, c), True)
                                                  for j, chip in enumerate(chips)]

    def passed(a, j):
        return copy(a, 4 + j, (*chips[j], c), sibling)

    def start():
        for a in range(n):
            local(a).start()
            for cp in first(a):
                cp.start()

    def forward():
        for a in range(n):
            for j, chip in enumerate(chips):
                copy(a, 1 + j, (*chip, c), me).wait_recv()
                passed(a, j).start()

    def finish():
        for a in range(n):
            copy(a, 0, sibling, me).wait_recv()
            for j, chip in enumerate(chips):
                copy(a, 4 + j, (*chip, 1 - c), me).wait_recv()
        for a in range(n):
            for cp in first(a) + [passed(a, j) for j in range(3)]:
                cp.wait_send()
            local(a).wait()

    return start, forward, finish


def _scatter_phases(src, out, send_sems, recv_sems, loc_sems):
    n = len(src)
    masks = [((k >> 2) & 1, (k >> 1) & 1, k & 1) for k in range(1, 8)]
    x, y, c = lax.axis_index("x"), lax.axis_index("y"), lax.axis_index("c")
    me = 4 * x + 2 * y + c

    def copies():
        cps = []
        for a in range(n):
            cps.append(pltpu.make_async_copy(src[a].at[me], out[a].at[me], loc_sems.at[a]))
            for k, (mx, my, mc) in enumerate(masks):
                px, py, pc = _flip(x, mx), _flip(y, my), _flip(c, mc)
                cps.append(pltpu.make_async_remote_copy(
                    src_ref=src[a].at[4 * px + 2 * py + pc], dst_ref=out[a].at[me],
                    send_sem=send_sems.at[7 * a + k], recv_sem=recv_sems.at[7 * a + k],
                    device_id=(px, py, pc), device_id_type=MESH))
        return cps

    def start():
        for cp in copies():
            cp.start()

    def finish():
        for cp in copies():
            cp.wait()

    return start, finish


def _all_to_all(blocks, name):
    n = len(blocks)

    def body(*refs):
        start, finish = _scatter_phases(refs[:n], refs[n:2 * n], *refs[2 * n:])
        start()
        finish()

    any_ = pl.BlockSpec(memory_space=pl.ANY)
    return pl.pallas_call(
        body, name=name,
        in_specs=[any_] * n, out_specs=[any_] * n,
        out_shape=[jax.ShapeDtypeStruct(b.shape, b.dtype) for b in blocks], scratch_shapes=_exchange_sems(n),
    )(*blocks)


def _adamw(parts, w, m, v, br, name):
    g8, r, c = parts.shape
    c1 = 1.0 - ADAM_B1 ** ADAM_STEP
    c2 = 1.0 - ADAM_B2 ** ADAM_STEP

    def body(p_ref, w_ref, m_ref, v_ref, g_ref, d_ref, m2_ref, v2_ref):
        g = p_ref[0]
        for d in range(1, g8):
            g = g + p_ref[d]
        m2 = ADAM_B1 * m_ref[...] + (1.0 - ADAM_B1) * g
        v2 = ADAM_B2 * v_ref[...] + (1.0 - ADAM_B2) * (g * g)
        g_ref[...] = g
        m2_ref[...] = m2
        v2_ref[...] = v2
        d_ref[...] = -ADAM_LR * ((m2 / c1) / (jnp.sqrt(v2 / c2) + ADAM_EPS) + ADAM_WD * w_ref[...])

    blk = pl.BlockSpec((br, c), lambda i: (i, 0))
    out = jax.ShapeDtypeStruct((r, c), F32)
    return pl.pallas_call(
        body, name=name, grid=(r // br,),
        in_specs=[pl.BlockSpec((g8, br, c), lambda i: (0, i, 0)), blk, blk, blk],
        out_specs=[blk] * 4, out_shape=[out] * 4,
        compiler_params=_params(VMEM_MID),
    )(parts, w, m, v)


def _pad_rows(a, rows):
    return jnp.pad(a, ((0, rows - a.shape[0]), (0, 0)))


_SMALL =(("g_mix", 1024), ("b_f", 8), ("g_conv_out", 512), ("g_attn_out", 512), ("g_ffn", 1024), ("g_final", 1024))
_SMALL_ROWS = 40


def _pack_small(vals):
    flat = jnp.concatenate([vals[k].reshape(-1) for k, _ in _SMALL])
    return jnp.pad(flat, (0, _SMALL_ROWS * LANES - flat.shape[0])).reshape(_SMALL_ROWS, LANES)


def _unpack_small(packed, shapes):
    flat = packed.reshape(-1)
    out, off = {}, 0
    for k, n in _SMALL:
        out[k] = flat[off:off + n].reshape(shapes[k])
        off += n
    return out


def _col_blocks(a):
    r, c8 = a.shape
    return jnp.transpose(a.reshape(r, 8, c8 // 8), (1, 0, 2))


def _from_col_blocks(a):
    g, r, c = a.shape
    return jnp.transpose(a, (1, 0, 2)).reshape(r, g * c)


def kernel(x, g_mix, w_in, b_f, w_conv, g_conv_out, g_attn_out, w_o, g_ffn, w_up, w_ffn_conv, w_down, g_final, loss_target, m_g_mix, m_w_in, m_b_f, m_w_conv, m_g_conv_out, m_g_attn_out, m_w_o, m_g_ffn, m_w_up, m_w_ffn_conv, m_w_down, m_g_final, v_g_mix, v_w_in, v_b_f, v_w_conv, v_g_conv_out, v_g_attn_out, v_w_o, v_g_ffn, v_w_up, v_w_ffn_conv, v_w_down, v_g_final):
    w = dict(g_mix=g_mix, w_in=w_in[0], b_f=b_f, w_conv=w_conv[0], g_conv_out=g_conv_out, g_attn_out=g_attn_out,
             w_o=w_o[0], g_ffn=g_ffn, w_up=w_up[0], w_ffn_conv=w_ffn_conv[0], w_down=w_down[0], g_final=g_final)
    m = dict(g_mix=m_g_mix, w_in=m_w_in[0], b_f=m_b_f, w_conv=m_w_conv[0], g_conv_out=m_g_conv_out,
             g_attn_out=m_g_attn_out, w_o=m_w_o[0], g_ffn=m_g_ffn, w_up=m_w_up[0], w_ffn_conv=m_w_ffn_conv[0],
             w_down=m_w_down[0], g_final=m_g_final)
    v = dict(g_mix=v_g_mix, w_in=v_w_in[0], b_f=v_b_f, w_conv=v_w_conv[0], g_conv_out=v_g_conv_out,
             g_attn_out=v_g_attn_out, w_o=v_w_o[0], g_ffn=v_g_ffn, w_up=v_w_up[0], w_ffn_conv=v_w_ffn_conv[0],
             w_down=v_w_down[0], g_final=v_g_final)
    shapes = dict(g_mix=g_mix.shape, w_in=w_in.shape, b_f=b_f.shape, w_conv=w_conv.shape,
                  g_conv_out=g_conv_out.shape, g_attn_out=g_attn_out.shape, w_o=w_o.shape, g_ffn=g_ffn.shape,
                  w_up=w_up.shape, w_ffn_conv=w_ffn_conv.shape, w_down=w_down.shape, g_final=g_final.shape)

    tm = t = 512
    tf = 256
    xs, tgt = x[0], loss_target[0]
    seq = xs.shape[0]
    assert seq % tm == 0 and seq % tf == 0
    bk = 1024 if seq % 1024 == 0 else 512

    g_in, g_conv = _all_gather([w["w_in"].astype(BF), w["w_conv"]])
    w_in_f = _from_col_blocks(g_in)
    w_zc = w_in_f[:, :3 * CONV_CH]
    w_qkv = w_in_f[:, 3 * CONV_CH:3 * CONV_CH + 3 * ATTN_W]
    w_f = jnp.pad(w_in_f[:, 3 * CONV_CH + 3 * ATTN_W:], ((0, 0), (0, LANES - N_HEADS)))
    b_f_p = jnp.pad(b_f, ((0, 0), (0, LANES - N_HEADS)))
    w_conv_p = _pad_rows(_from_col_blocks(g_conv), 8)

    h1, zc, qkv, fpre, fcol, nc, qa, ka, qt, kt, vt = _fwd_in(
        xs, g_mix, w_zc, w_qkv, w_f, b_f_p, w_conv_p, g_conv_out, tm)
    o, lcol, g_o, g_up, g_dn, g_fc = _attn_fwd(
        qa, ka, vt, t, [w["w_o"].astype(BF), w["w_up"].astype(BF), w["w_down"].astype(BF), w["w_ffn_conv"]])
    w_o_f = g_o.reshape(D_MODEL, D_MODEL)
    w_up_f = _from_col_blocks(g_up)
    w_dn_f = g_dn.reshape(D_FF, D_MODEL)
    w_fc_p = _pad_rows(_from_col_blocks(g_fc), 8)
    x2, h2, mix = _fwd_out(xs, nc, o, g_attn_out, w_o_f, g_ffn, tm)
    u0, act, dx3, dx3b, loss, dg_final = _fwd_ffn(h2, x2, tgt, w_up_f, w_fc_p, w_dn_f, g_final.reshape(1, D_MODEL), tf)
    loss = lax.psum(loss[0, 0], ("x", "y", "c"))

    du0, dx2, dx2b, dw_fc, dg_ffn = _bwd_ffn(dx3, dx3b, u0, x2, w_up_f.T, w_fc_p, w_dn_f.T, g_ffn, tf)
    dw_up = _matmul_tn(h2, du0, 1408, bk, "dw_up")
    dw_dn = _matmul_tn(act, dx3b, 512, bk, "dw_down")
    dw_o = _matmul_tn(mix, dx2b, 512, bk, "dw_o")
    dzc, dob, doa, dot, qab, drow, dw_conv, dg_attn, dg_conv = _bwd_out(
        dx2b, o, zc, fcol, lcol, qa, w_o_f, g_attn_out, g_conv_out, w_conv_p, t)
    dq, dfq, r_up, r_dn, r_fc, r_o = _attn_bwd_dq(
        qab, ka, qkv, kt, dob, drow, t,
        [_col_blocks(dw_up), dw_dn.reshape(8, D_FF // 8, D_MODEL), _col_blocks(dw_fc[:3]),
         dw_o.reshape(8, D_MODEL // 8, D_MODEL)])
    dk, dv, dfk = _attn_bwd_dkv(qab, doa, ka, qkv, qt, dot, t)
    gx, dfb, dg_mix, db_f = _bwd_in(xs, dx2, dzc, dq, dk, dv, dfq, dfk, fpre, w_zc, w_qkv, w_f, g_mix, tm)
    dw_zc = _matmul_tn(h1, dzc, 3 * CONV_CH, bk, "dw_in_conv")
    dw_q = _matmul_tn(h1, dq, ATTN_W, bk, "dw_in_q")
    dw_k = _matmul_tn(h1, dk, ATTN_W, bk, "dw_in_k")
    dw_v = _matmul_tn(h1, dv, ATTN_W, bk, "dw_in_v")
    dw_f = _matmul_tn(h1, dfb, LANES, bk, "dw_in_f")
    dw_in = jnp.concatenate([dw_zc, dw_q, dw_k, dw_v, dw_f[:, :N_HEADS]], axis=1)
    small = dict(g_mix=dg_mix, b_f=db_f[:, :N_HEADS], g_conv_out=dg_conv, g_attn_out=dg_attn, g_ffn=dg_ffn,
                 g_final=dg_final)
    r_in, r_conv, r_small = _all_to_all(
        [_col_blocks(dw_in), _col_blocks(dw_conv[:3]), jnp.broadcast_to(_pack_small(small), (8, _SMALL_ROWS, LANES))],
        "all_to_all_grads")
    big = ("w_in", "w_o", "w_up", "w_down", "w_conv", "w_ffn_conv")
    recv = [r_in, r_o, r_up, r_dn, r_conv, r_fc, r_small]

    rows = dict(w_in=256, w_o=128, w_up=256, w_down=176, w_conv=3, w_ffn_conv=3)
    g_out, d_out, m_out, v_out = {}, {}, {}, {}
    for a, k in enumerate(big):
        res = _adamw(recv[a], w[k], m[k], v[k], rows[k], "adamw_" + k)
        g_out[k], d_out[k], m_out[k], v_out[k] = [r.reshape(shapes[k]) for r in res]
    res = _adamw(recv[6], _pack_small(w), _pack_small(m), _pack_small(v), _SMALL_ROWS, "adamw_gains")
    for dst, packed in zip((g_out, d_out, m_out, v_out), res):
        dst.update(_unpack_small(packed, shapes))

    order = ("g_mix", "w_in", "b_f", "w_conv", "g_conv_out", "g_attn_out", "w_o", "g_ffn", "w_up", "w_ffn_conv",
             "w_down", "g_final")
    return (loss, gx.reshape(x.shape), *[g_out[k] for k in order], *[d_out[k] for k in order],
            *[m_out[k] for k in order], *[v_out[k] for k in order])
```

```python
import functools

import jax
import jax.numpy as jnp
import numpy as np
from jax import lax
from jax.experimental import pallas as pl
from jax.experimental.pallas import tpu as pltpu

F32 = jnp.float32
BF = jnp.bfloat16
HI = lax.Precision.HIGHEST
MESH = pl.DeviceIdType.MESH

D_MODEL = 1024
CONV_CH = 512
ATTN_W = 512
N_HEADS = 8
HEAD_DIM = 64
D_FF = 2816
FF_CHUNK = 256
N_FF_CHUNKS = D_FF // FF_CHUNK
IN_COLS = 3080
EPS = 1e-6
NEG = -1e30
LANES = 128
VMEM_BIG = 56 * 1024 * 1024
VMEM_MID = 40 * 1024 * 1024

ADAM_LR = 0.001
ADAM_B1 = 0.9
ADAM_B2 = 0.999
ADAM_EPS = 1e-08
ADAM_WD = 0.01
ADAM_STEP = 10

NT = (((1,), (1,)), ((), ()))
TN = (((0,), (0,)), ((), ()))


def _dot(a, b):
    return jnp.dot(a, b, preferred_element_type=F32)


def _dot_nt(a, b):
    return lax.dot_general(a, b, NT, preferred_element_type=F32)


def _dot_tn(a, b):
    return lax.dot_general(a, b, TN, preferred_element_type=F32)


def _params(vmem=None, parallel=False):
    return pltpu.CompilerParams(
        dimension_semantics=None, vmem_limit_bytes=vmem)


def _rows(shape):
    return lax.broadcasted_iota(jnp.int32, shape, 0)


def _cols(shape):
    return lax.broadcasted_iota(jnp.int32, shape, 1)


def _shift_down(u, prev, k):
    n = prev.shape[0]
    out = pltpu.roll(u, k, 0)
    row = _rows(u.shape)
    for r in range(k):
        out = jnp.where(row == r, prev[n - k + r:n - k + r + 1, :].astype(u.dtype), out)
    return out


def _shift_up(u, nxt, k):
    tm = u.shape[0]
    out = pltpu.roll(u, tm - k, 0)
    row = _rows(u.shape)
    for r in range(k):
        out = jnp.where(row == tm - k + r, nxt[r:r + 1, :], out)
    return out


def _conv3(u, prev, w):
    u1 = _shift_down(u, prev, 1)
    u2 = _shift_down(u, prev, 2)
    return w[0:1, :] * u2 + w[1:2, :] * u1 + w[2:3, :] * u, u1, u2


def _conv3_bwd(d, nxt, w):
    return w[2:3, :] * d + w[1:2, :] * _shift_up(d, nxt, 1) + w[0:1, :] * _shift_up(d, nxt, 2)


def _rms(x):
    return lax.rsqrt(jnp.mean(x * x, axis=-1, keepdims=True) + EPS)


def _rms_bwd(x, r, dyg):
    return r * dyg - x * (r * r * r) * jnp.mean(dyg * x, axis=-1, keepdims=True)


def _full(shape):
    nd = len(shape)
    return pl.BlockSpec(shape, lambda i, _n=nd: (0,) * _n)


ONES_LANE = 24


def _bias_scatter():
    sc = np.zeros((LANES, 2 * N_HEADS * LANES), np.float32)
    koff = N_HEADS * LANES
    for h in range(N_HEADS):
        aux = HEAD_DIM * (1 - h % 2)
        for j in range(3):
            sc[8 * j + h, LANES * h + aux + j] = 1.0
            sc[ONES_LANE, koff + LANES * h + aux + j] = 1.0
            sc[ONES_LANE, LANES * h + aux + 3 + j] = 1.0
            sc[8 * j + h, koff + LANES * h + aux + 3 + j] = -1.0
    return jnp.asarray(sc, BF)


def _split_parts(v, one):
    hi = v.astype(BF).astype(F32)
    rest = v - hi
    mid = rest.astype(BF).astype(F32)
    lo = (rest - mid).astype(BF).astype(F32)
    parts = hi + pltpu.roll(mid, 8, 1) + pltpu.roll(lo, 16, 1)
    return jnp.where(_cols(parts.shape) == ONES_LANE, one, parts).astype(BF)


def _fwd_in(x, g_mix, w_zc, w_qkv, w_f, b_f, w_conv, g_conv_out, tm):
    s = x.shape[0]
    nb = s // tm
    aw = N_HEADS * LANES

    def body(x_ref, gm_ref, wzc_ref, wqkv_ref, wf_ref, bf_ref, wc_ref, gco_ref, sc_ref,
             h1_ref, zc_ref, qkv_ref, fpre_ref, fcol_ref, nc_ref, qa_ref, ka_ref, qt_ref, vt_ref, cu_ref, cf_ref):
        i = pl.program_id(0)

        @pl.when(i == 0)
        def _():
            cu_ref[...] = jnp.zeros_like(cu_ref)
            cf_ref[...] = jnp.zeros_like(cf_ref)

        xv = x_ref[...]
        hb = (xv * _rms(xv) * gm_ref[...]).astype(BF)
        h1_ref[...] = hb
        zc = _dot(hb, wzc_ref[...])
        zc_ref[...] = zc
        qkv = _dot(hb, wqkv_ref[...])
        qkv = jnp.where(_cols(qkv.shape) < ATTN_W, qkv * 0.125, qkv)
        qkvb = qkv.astype(BF)
        qkv_ref[...] = qkvb
        qt_ref[0] = qkv[:, :ATTN_W].T.astype(BF)
        vt_ref[0] = qkv[:, 2 * ATTN_W:].T.astype(BF)

        gb, gc, xc = zc[:, :CONV_CH], zc[:, CONV_CH:2 * CONV_CH], zc[:, 2 * CONV_CH:]
        u = gc * xc
        cv, _, _ = _conv3(u, cu_ref[...], wc_ref[...])
        cu_ref[...] = u[tm - 8:, :]
        y = gb * cv
        nc_ref[...] = (y * _rms(y) * gco_ref[...]).astype(BF)

        fpre = _dot(hb, wf_ref[...]) + bf_ref[...]
        fpre_ref[...] = fpre
        logf = jnp.minimum(fpre, 0.0) - jnp.log1p(jnp.exp(-jnp.abs(fpre)))
        logf = jnp.where(_cols(logf.shape) < N_HEADS, logf, 0.0)
        tri = (_rows((tm, tm)) >= _cols((tm, tm))).astype(F32)
        fcol = jnp.dot(tri, logf, precision=HI, preferred_element_type=F32) + cf_ref[...]
        cf_ref[...] = fcol[tm - 1:tm, :]
        fcol_ref[...] = fcol

        feat = _dot(_split_parts(fcol, 1.0), sc_ref[...]).astype(BF)
        lane = _cols((tm, LANES))
        for h in range(N_HEADS):
            hp, hh = divmod(h, 2)
            own = (lane >= HEAD_DIM * hh) & (lane < HEAD_DIM * (hh + 1))
            hs = slice(LANES * h, LANES * (h + 1))
            qa_ref[:, hs] = jnp.where(own, qkvb[:, LANES * hp:LANES * (hp + 1)], feat[:, hs])
            ka_ref[:, hs] = jnp.where(own, qkvb[:, ATTN_W + LANES * hp:ATTN_W + LANES * (hp + 1)],
                                      feat[:, aw + LANES * h:aw + LANES * (h + 1)])

    blk = lambda c: pl.BlockSpec((tm, c), lambda i: (i, 0))
    return pl.pallas_call(
        body, name="fwd_in", grid=(nb,),
        in_specs=[blk(D_MODEL), _full((1, D_MODEL)), _full(w_zc.shape), _full(w_qkv.shape), _full(w_f.shape),
                  _full((1, LANES)), _full((8, CONV_CH)), _full((1, CONV_CH)), _full((LANES, 2 * aw))],
        out_specs=[blk(D_MODEL), blk(3 * CONV_CH), blk(3 * ATTN_W), blk(LANES), blk(LANES),
                   blk(CONV_CH), blk(aw), blk(aw)] + [pl.BlockSpec((1, ATTN_W, tm), lambda i: (i, 0, 0))] * 2,
        out_shape=[jax.ShapeDtypeStruct((s, D_MODEL), BF), jax.ShapeDtypeStruct((s, 3 * CONV_CH), F32),
                   jax.ShapeDtypeStruct((s, 3 * ATTN_W), BF), jax.ShapeDtypeStruct((s, LANES), F32),
                   jax.ShapeDtypeStruct((s, LANES), F32),
                   jax.ShapeDtypeStruct((s, CONV_CH), BF), jax.ShapeDtypeStruct((s, aw), BF),
                   jax.ShapeDtypeStruct((s, aw), BF)] + [jax.ShapeDtypeStruct((nb, ATTN_W, tm), BF)] * 2,
        scratch_shapes=[pltpu.VMEM((8, CONV_CH), F32), pltpu.VMEM((1, LANES), F32)],
        compiler_params=_params(VMEM_MID),
    )(x, g_mix, w_zc, w_qkv, w_f, b_f, w_conv, g_conv_out, _bias_scatter())


def _head_mask(shape, hh):
    lane = _cols(shape)
    return (lane >= HEAD_DIM * hh) & (lane < HEAD_DIM * (hh + 1))


def _pipeline_masked_last(last, produce, consume, buf_a, buf_b):
    produce(0, buf_a)

    def two_blocks(j, carry):
        blk = 2 * j
        produce(blk + 1, buf_b)
        consume(blk, buf_a, False)
        produce(blk + 2, buf_a)
        consume(blk + 1, buf_b, False)
        return carry

    lax.fori_loop(0, last // 2, two_blocks, 0)

    @pl.when(last % 2 == 0)
    def _():
        consume(last, buf_a, True)

    @pl.when(last % 2 == 1)
    def _():
        produce(last, buf_b)
        consume(last - 1, buf_a, False)
        consume(last, buf_b, True)


def _attn_fwd(qa, ka, vt, t, shards):
    s = qa.shape[0]
    nb = s // t
    n = len(shards)

    def body(qa_ref, ka_any, vt_any, *refs):
        shard_refs, (o_ref, l_ref), gathered = refs[:n], refs[n:n + 2], refs[n + 2:2 * n + 2]
        ka_scr, vt_scr, m_scr, acc_scr, ea_scr, eb_scr, sem = refs[2 * n + 2:2 * n + 9]
        i = pl.program_id(0)
        start, forward, finish = _gather_phases(shard_refs, gathered, *refs[2 * n + 9:])
        pl.when(i == 0)(start)
        pl.when(i == nb // 2)(forward)
        _load_weights(i, [(ka_any, ka_scr), (vt_any, vt_scr)], sem)

        causal_t = _rows((t, t)) <= _cols((t, t))
        row = _rows((LANES, t))
        lrows = jnp.zeros((LANES, t), F32)
        for hp in range(N_HEADS // 2):
            ps = slice(hp * LANES, (hp + 1) * LANES)
            m_scr[...] = jnp.full(m_scr.shape, NEG, F32)
            acc_scr[...] = jnp.zeros(acc_scr.shape, F32)
            heads = [dict(hh=hh, hs=slice((2 * hp + hh) * LANES, (2 * hp + hh + 1) * LANES),
                          aux=HEAD_DIM * (1 - hh), own=(row >= HEAD_DIM * hh) & (row < HEAD_DIM * (hh + 1)),
                          fill=jnp.where(row == HEAD_DIM * (1 - hh), 1.0, 0.0).astype(BF),
                          qh=qa_ref[:, (2 * hp + hh) * LANES:(2 * hp + hh + 1) * LANES]) for hh in range(2)]

            def scores(kb, dst, heads=heads):
                rs = pl.ds(pl.multiple_of(kb * t, t), t)
                for hd in heads:
                    dst[hd["hh"]] = _dot_nt(ka_scr[rs, hd["hs"]], hd["qh"])

            def consume(kb, src, masked, ps=ps, heads=heads):
                vt2 = vt_scr[kb, ps, :]
                m_olds = [m_scr[hd["hh"]:hd["hh"] + 1, :] for hd in heads]
                accs = [acc_scr[hd["hh"]] for hd in heads]
                m_news, acc_news = [], []
                for hd, m_old, acc in zip(heads, m_olds, accs):
                    e = src[hd["hh"]]
                    if masked:
                        e = jnp.where(causal_t, e, NEG)
                    m_new = jnp.maximum(m_old, jnp.max(e, axis=0, keepdims=True))
                    p = jnp.exp(e - m_new).astype(BF)
                    vta = jnp.where(hd["own"], vt2, hd["fill"])
                    acc_news.append(jnp.exp(m_old - m_new) * acc + _dot(vta, p))
                    m_news.append(m_new)
                for hd, m_new, acc in zip(heads, m_news, acc_news):
                    acc_scr[hd["hh"]] = acc
                    m_scr[hd["hh"]:hd["hh"] + 1, :] = m_new

            _pipeline_masked_last(i, scores, consume, ea_scr, eb_scr)

            o_pair = None
            for hd in heads:
                hh = hd["hh"]
                acc = acc_scr[hh]
                denom = acc[hd["aux"]:hd["aux"] + 1, :]
                o_h = acc / denom
                lrows = jnp.where(row == 2 * hp + hh, m_scr[hh:hh + 1, :] + jnp.log(denom), lrows)
                o_pair = o_h if hh == 0 else jnp.where(row < HEAD_DIM, o_pair, o_h)
            o_ref[:, ps] = o_pair.T
        l_ref[...] = lrows.T
        pl.when(i == nb - 1)(finish)

    any_ = pl.BlockSpec(memory_space=pl.ANY)
    return pl.pallas_call(
        body, name="attn_fwd", grid=(nb,),
        in_specs=[pl.BlockSpec((t, N_HEADS * LANES), lambda i: (i, 0)), any_, any_] + [any_] * n,
        out_specs=[pl.BlockSpec((t, ATTN_W), lambda i: (i, 0)), pl.BlockSpec((t, LANES), lambda i: (i, 0))]
        + [any_] * n,
        out_shape=[jax.ShapeDtypeStruct((s, ATTN_W), F32), jax.ShapeDtypeStruct((s, LANES), F32)]
        + _gathered_shapes(shards),
        scratch_shapes=[pltpu.VMEM(ka.shape, BF), pltpu.VMEM(vt.shape, BF), pltpu.VMEM((2, t), F32),
                        pltpu.VMEM((2, LANES, t), F32), pltpu.VMEM((2, t, t), F32), pltpu.VMEM((2, t, t), F32),
                        pltpu.SemaphoreType.DMA((2,))] + _exchange_sems(n),
        compiler_params=_params(VMEM_BIG),
    )(qa, ka, vt, *shards)


def _fwd_out(x, nc, o, g_attn_out, w_o, g_ffn, tm):
    s = x.shape[0]

    def body(x_ref, nc_ref, o_ref, ga_ref, wo_ref, gf_ref, x2_ref, h2_ref, mix_ref):
        ov = o_ref[...]
        na = (ov * _rms(ov) * ga_ref[...]).astype(BF)
        ncv = nc_ref[...]
        mix_ref[:, :CONV_CH] = ncv
        mix_ref[:, CONV_CH:] = na
        x2 = x_ref[...] + _dot(ncv, wo_ref[:CONV_CH, :]) + _dot(na, wo_ref[CONV_CH:, :])
        x2_ref[...] = x2
        h2_ref[...] = (x2 * _rms(x2) * gf_ref[...]).astype(BF)

    blk = lambda c: pl.BlockSpec((tm, c), lambda i: (i, 0))
    return pl.pallas_call(
        body, name="fwd_out", grid=(s // tm,),
        in_specs=[blk(D_MODEL), blk(CONV_CH), blk(ATTN_W), _full((1, ATTN_W)), _full(w_o.shape), _full((1, D_MODEL))],
        out_specs=[blk(D_MODEL), blk(D_MODEL), blk(D_MODEL)],
        out_shape=[jax.ShapeDtypeStruct((s, D_MODEL), F32), jax.ShapeDtypeStruct((s, D_MODEL), BF),
                   jax.ShapeDtypeStruct((s, D_MODEL), BF)],
        compiler_params=_params(VMEM_MID),
    )(x, nc, o, g_attn_out, w_o, g_ffn)


def _load_weights(i, pairs, sem):
    @pl.when(i == 0)
    def _():
        cps = [pltpu.make_async_copy(src, dst, sem.at[n]) for n, (src, dst) in enumerate(pairs)]
        for cp in cps:
            cp.start()
        for cp in cps:
            cp.wait()


def _ff_cols(j):
    return (slice(j * FF_CHUNK, (j + 1) * FF_CHUNK), slice(D_FF + j * FF_CHUNK, D_FF + (j + 1) * FF_CHUNK))


def _fwd_ffn(h2, x2, tgt, w_up, w_ffn_conv, w_dn, g_final, tm):
    s = x2.shape[0]

    def body(h2_ref, x2_ref, tgt_ref, wfc_ref, gfin_ref, wup_any, wdn_any,
             u0_ref, act_ref, dx3_ref, dx3b_ref, loss_ref, dgfin_ref,
             wup, wdn, carry, sem):
        i = pl.program_id(0)
        _load_weights(i, [(wup_any, wup), (wdn_any, wdn)], sem)

        @pl.when(i == 0)
        def _():
            carry[...] = jnp.zeros_like(carry)
            loss_ref[...] = jnp.zeros_like(loss_ref)
            dgfin_ref[...] = jnp.zeros_like(dgfin_ref)

        hb = h2_ref[...]

        def up(j):
            return [_dot(hb, wup[:, cc]) for cc in _ff_cols(j)]

        nxt = up(0)
        down = None
        for j in range(N_FF_CHUNKS):
            cur = nxt
            if j + 1 < N_FF_CHUNKS:
                nxt = up(j + 1)
            parts = []
            for cc, u0 in zip(_ff_cols(j), cur):
                u0_ref[:, cc] = u0.astype(BF)
                uu, _, _ = _conv3(u0, carry[:, cc], wfc_ref[:, cc])
                carry[:, cc] = u0[tm - 8:, :]
                parts.append(uu)
            ua, ug = parts
            act = (ug * jax.nn.sigmoid(ug) * ua).astype(BF)
            ca = _ff_cols(j)[0]
            act_ref[:, ca] = act
            part = _dot(act, wdn[ca, :])
            down = part if down is None else down + part

        x3 = x2_ref[...] + down
        r3 = _rms(x3)
        gfin = gfin_ref[...]
        xn = x3 * r3
        diff = xn * gfin - tgt_ref[...]
        loss_ref[...] += jnp.sum(jnp.sum(diff * diff, axis=-1, keepdims=True), axis=0, keepdims=True) * (0.5 / D_MODEL)
        dy = diff * (1.0 / D_MODEL)
        dgfin_ref[...] += jnp.sum(dy * xn, axis=0, keepdims=True)
        dx3 = _rms_bwd(x3, r3, dy * gfin)
        dx3_ref[...] = dx3
        dx3b_ref[...] = dx3.astype(BF)

    blk = lambda c: pl.BlockSpec((tm, c), lambda i: (i, 0))
    any_ = pl.BlockSpec(memory_space=pl.ANY)
    return pl.pallas_call(
        body, name="fwd_ffn", grid=(s // tm,),
        in_specs=[blk(D_MODEL), blk(D_MODEL), blk(D_MODEL), _full((8, 2 * D_FF)), _full((1, D_MODEL)), any_, any_],
        out_specs=[blk(2 * D_FF), blk(D_FF), blk(D_MODEL), blk(D_MODEL), _full((1, 1)), _full((1, D_MODEL))],
        out_shape=[jax.ShapeDtypeStruct((s, 2 * D_FF), BF), jax.ShapeDtypeStruct((s, D_FF), BF),
                   jax.ShapeDtypeStruct((s, D_MODEL), F32), jax.ShapeDtypeStruct((s, D_MODEL), BF),
                   jax.ShapeDtypeStruct((1, 1), F32), jax.ShapeDtypeStruct((1, D_MODEL), F32)],
        scratch_shapes=[pltpu.VMEM(w_up.shape, BF), pltpu.VMEM(w_dn.shape, BF), pltpu.VMEM((8, 2 * D_FF), F32),
                        pltpu.SemaphoreType.DMA((2,))],
        compiler_params=_params(VMEM_BIG),
    )(h2, x2, tgt, w_ffn_conv, g_final, w_up, w_dn)


def _bwd_ffn(dx3, dx3b, u0, x2, w_up, w_ffn_conv, w_dn, g_ffn, tm):
    s = x2.shape[0]
    nb = s // tm
    hr = 16

    def body(dx3_ref, dx3b_ref, u0_ref, halo_ref, x2_ref, wfc_ref, gf_ref, wup_any, wdn_any,
             du0_ref, dx2_ref, dx2b_ref, dwfc_ref, dgf_ref,
             wup, wdn, carry, sem):
        i = pl.program_id(0)
        rb = nb - 1 - i
        _load_weights(i, [(wup_any, wup), (wdn_any, wdn)], sem)

        @pl.when(i == 0)
        def _():
            carry[...] = jnp.zeros_like(carry)
            dwfc_ref[...] = jnp.zeros_like(dwfc_ref)
            dgf_ref[...] = jnp.zeros_like(dgf_ref)

        db = dx3b_ref[...]
        live = (rb > 0).astype(F32)

        def dact_of(j):
            return _dot(db, wdn[:, _ff_cols(j)[0]])

        nxt = dact_of(0)
        dh2 = None
        for j in range(N_FF_CHUNKS):
            ca, cg = _ff_cols(j)
            dact = nxt
            if j + 1 < N_FF_CHUNKS:
                nxt = dact_of(j + 1)
            fw = []
            for cc in (ca, cg):
                u0 = u0_ref[:, cc].astype(F32)
                prev = halo_ref[:, cc].astype(F32) * live
                fw.append((u0,) + _conv3(u0, prev, wfc_ref[:, cc]))
            (u0a, ua, u1a, u2a), (u0g, ug, u1g, u2g) = fw
            sg = jax.nn.sigmoid(ug)
            da = dact * (ug * sg)
            dg = dact * ua * (sg * (1.0 + ug * (1.0 - sg)))
            for cc, d, u0c, u1c, u2c in ((ca, da, u0a, u1a, u2a), (cg, dg, u0g, u1g, u2g)):
                dwfc_ref[0:1, cc] += jnp.sum(d * u2c, axis=0, keepdims=True)
                dwfc_ref[1:2, cc] += jnp.sum(d * u1c, axis=0, keepdims=True)
                dwfc_ref[2:3, cc] += jnp.sum(d * u0c, axis=0, keepdims=True)
                du0 = _conv3_bwd(d, carry[:, cc], wfc_ref[:, cc]).astype(BF)
                carry[:, cc] = d[:8, :]
                du0_ref[:, cc] = du0
                part = _dot(du0, wup[cc, :])
                dh2 = part if dh2 is None else dh2 + part

        x2v = x2_ref[...]
        r2 = _rms(x2v)
        dgf_ref[...] += jnp.sum(dh2 * (x2v * r2), axis=0, keepdims=True)
        dx2 = dx3_ref[...] + _rms_bwd(x2v, r2, dh2 * gf_ref[...])
        dx2_ref[...] = dx2
        dx2b_ref[...] = dx2.astype(BF)

    blk = lambda c: pl.BlockSpec((tm, c), lambda i: (nb - 1 - i, 0))
    halo = pl.BlockSpec((hr, 2 * D_FF), lambda i: (jnp.maximum((nb - 1 - i) * (tm // hr) - 1, 0), 0))
    any_ = pl.BlockSpec(memory_space=pl.ANY)
    return pl.pallas_call(
        body, name="bwd_ffn", grid=(nb,),
        in_specs=[blk(D_MODEL), blk(D_MODEL), blk(2 * D_FF), halo, blk(D_MODEL), _full((8, 2 * D_FF)),
                  _full((1, D_MODEL)), any_, any_],
        out_specs=[blk(2 * D_FF), blk(D_MODEL), blk(D_MODEL), _full((8, 2 * D_FF)), _full((1, D_MODEL))],
        out_shape=[jax.ShapeDtypeStruct((s, 2 * D_FF), BF), jax.ShapeDtypeStruct((s, D_MODEL), F32),
                   jax.ShapeDtypeStruct((s, D_MODEL), BF), jax.ShapeDtypeStruct((8, 2 * D_FF), F32),
                   jax.ShapeDtypeStruct((1, D_MODEL), F32)],
        scratch_shapes=[pltpu.VMEM(w_up.shape, BF), pltpu.VMEM(w_dn.shape, BF), pltpu.VMEM((8, 2 * D_FF), F32),
                        pltpu.SemaphoreType.DMA((2,))],
        compiler_params=_params(VMEM_BIG),
    )(dx3, dx3b, u0, u0, x2, w_ffn_conv, g_ffn, w_up, w_dn)


def _bwd_out(dx2b, o, zc, fcol, lcol, qa, w_o, g_attn_out, g_conv_out, w_conv, tm):
    s = o.shape[0]
    nb = s // tm
    aw = N_HEADS * LANES

    def body(dx2b_ref, o_ref, zc_ref, halo_ref, fcol_ref, lcol_ref, qa_ref, wo_ref, ga_ref, gco_ref, wc_ref, sc_ref,
             dzc_ref, doa_ref, dot_ref, qab_ref, dwc_ref, dga_ref, dgco_ref, carry):
        i = pl.program_id(0)
        rb = nb - 1 - i

        @pl.when(i == 0)
        def _():
            carry[...] = jnp.zeros_like(carry)
            dwc_ref[...] = jnp.zeros_like(dwc_ref)
            dga_ref[...] = jnp.zeros_like(dga_ref)
            dgco_ref[...] = jnp.zeros_like(dgco_ref)

        dmix = _dot_nt(dx2b_ref[...], wo_ref[...])
        dnc, dna = dmix[:, :CONV_CH], dmix[:, CONV_CH:]

        ov = o_ref[...]
        ra = _rms(ov)
        dga_ref[...] += jnp.sum(dna * (ov * ra), axis=0, keepdims=True)
        do = _rms_bwd(ov, ra, dna * ga_ref[...])
        dob = do.astype(BF)
        dot_ref[0] = do.T.astype(BF)
        sel = (_rows((ATTN_W, LANES)) // HEAD_DIM == _cols((ATTN_W, LANES))).astype(F32)
        delta = jnp.dot(do * ov, sel, precision=HI, preferred_element_type=F32)
        featc = _dot(_split_parts(fcol_ref[...] - lcol_ref[...], 1.0), sc_ref[...]).astype(BF)
        featd = _dot(_split_parts(-delta, 0.0), sc_ref[...]).astype(BF)
        lane = _cols((tm, LANES))
        for h in range(N_HEADS):
            hp, hh = divmod(h, 2)
            own = (lane >= HEAD_DIM * hh) & (lane < HEAD_DIM * (hh + 1))
            hs = slice(LANES * h, LANES * (h + 1))
            qab_ref[:, hs] = jnp.where(own, qa_ref[:, hs], featc[:, hs])
            doa_ref[:, hs] = jnp.where(own, dob[:, LANES * hp:LANES * (hp + 1)], featd[:, hs])

        zc_v = zc_ref[...]
        gb, gc, xc = zc_v[:, :CONV_CH], zc_v[:, CONV_CH:2 * CONV_CH], zc_v[:, 2 * CONV_CH:]
        hal = halo_ref[...] * (rb > 0).astype(F32)
        u = gc * xc
        prev = hal[:, CONV_CH:2 * CONV_CH] * hal[:, 2 * CONV_CH:]
        wc = wc_ref[...]
        cv, u1, u2 = _conv3(u, prev, wc)
        y = gb * cv
        rc = _rms(y)
        dgco_ref[...] += jnp.sum(dnc * (y * rc), axis=0, keepdims=True)
        dy = _rms_bwd(y, rc, dnc * gco_ref[...])
        dcv = dy * gb
        dwc_ref[0:1, :] += jnp.sum(dcv * u2, axis=0, keepdims=True)
        dwc_ref[1:2, :] += jnp.sum(dcv * u1, axis=0, keepdims=True)
        dwc_ref[2:3, :] += jnp.sum(dcv * u, axis=0, keepdims=True)
        du = _conv3_bwd(dcv, carry[...], wc)
        carry[...] = dcv[:8, :]
        dzc_ref[:, :CONV_CH] = (dy * cv).astype(BF)
        dzc_ref[:, CONV_CH:2 * CONV_CH] = (du * xc).astype(BF)
        dzc_ref[:, 2 * CONV_CH:] = (du * gc).astype(BF)

    blk = lambda c: pl.BlockSpec((tm, c), lambda i: (nb - 1 - i, 0))
    halo = pl.BlockSpec((8, 3 * CONV_CH), lambda i: (jnp.maximum((nb - 1 - i) * (tm // 8) - 1, 0), 0))
    tr = pl.BlockSpec((1, ATTN_W, tm), lambda i: (nb - 1 - i, 0, 0))
    return pl.pallas_call(
        body, name="bwd_out", grid=(nb,),
        in_specs=[blk(D_MODEL), blk(ATTN_W), blk(3 * CONV_CH), halo, blk(LANES), blk(LANES), blk(aw),
                  _full(w_o.shape), _full((1, ATTN_W)), _full((1, CONV_CH)), _full((8, CONV_CH)), _full((LANES, aw))],
        out_specs=[blk(3 * CONV_CH), blk(aw), tr, blk(aw),
                   _full((8, CONV_CH)), _full((1, ATTN_W)), _full((1, CONV_CH))],
        out_shape=[jax.ShapeDtypeStruct((s, 3 * CONV_CH), BF),
                   jax.ShapeDtypeStruct((s, aw), BF), jax.ShapeDtypeStruct((nb, ATTN_W, tm), BF),
                   jax.ShapeDtypeStruct((s, aw), BF),
                   jax.ShapeDtypeStruct((8, CONV_CH), F32), jax.ShapeDtypeStruct((1, ATTN_W), F32),
                   jax.ShapeDtypeStruct((1, CONV_CH), F32)],
        scratch_shapes=[pltpu.VMEM((8, CONV_CH), F32)],
        compiler_params=_params(VMEM_MID),
    )(dx2b, o, zc, zc, fcol, lcol, qa, w_o, g_attn_out, g_conv_out, w_conv, _bias_scatter()[:, :aw])


def _attn_bwd(qab, doa, ka, qkv, qt, dot, t, blocks):
    s = qab.shape[0]
    nb = s // t
    npair = N_HEADS // 2
    n = len(blocks)

    def body(ka_ref, v_ref, qab_ref, doa_ref, qt_ref, dot_ref, *refs):
        block_refs, (dk_ref, dv_ref, dfk_ref, dqa_ref), received = refs[:n], refs[n:n + 4], refs[n + 4:2 * n + 4]
        acck, accv = refs[2 * n + 4:2 * n + 6]
        hp = pl.program_id(0)
        j = pl.program_id(1)
        start, finish = _scatter_phases(block_refs, received, *refs[2 * n + 6:])
        pl.when((hp == 0) & (j == 0))(start)

        @pl.when(j == 0)
        def _():
            dqa_ref[...] = jnp.zeros_like(dqa_ref)

        causal = _cols((t, t)) <= _rows((t, t))
        row = _rows((LANES, t))
        lane = _cols((t, LANES))
        acck[...] = jnp.zeros(acck.shape, F32)
        accv[...] = jnp.zeros(accv.shape, F32)
        v2 = v_ref[...]
        heads = []
        for hh in range(2):
            aux = HEAD_DIM * (1 - hh)
            ownl = (lane >= HEAD_DIM * hh) & (lane < HEAD_DIM * (hh + 1))
            ones3 = jnp.where((lane >= aux) & (lane < aux + 3), 1.0, 0.0).astype(BF)
            heads.append(dict(hh=hh, hs=slice(hh * LANES, (hh + 1) * LANES), aux=aux,
                              own=(row >= HEAD_DIM * hh) & (row < HEAD_DIM * (hh + 1)),
                              fill=jnp.where(row == aux, 1.0, 0.0).astype(BF),
                              kh=ka_ref[:, hh * LANES:(hh + 1) * LANES], vh=jnp.where(ownl, v2, ones3)))

        def tile(qb, masked):
            rs = pl.ds(pl.multiple_of(qb * t, t), t)
            qt2 = qt_ref[qb]
            dot2 = dot_ref[qb]
            es = [_dot_nt(qab_ref[rs, hd["hs"]], hd["kh"]) for hd in heads]
            dps = [_dot_nt(doa_ref[rs, hd["hs"]], hd["vh"]) for hd in heads]
            aks = [acck[hd["hh"]] for hd in heads]
            avs = [accv[hd["hh"]] for hd in heads]
            dqs = [dqa_ref[rs, hd["hs"]] for hd in heads]
            outs = []
            for hd, e, dp, ak, av, dq in zip(heads, es, dps, aks, avs, dqs):
                if masked:
                    e = jnp.where(causal, e, NEG)
                p = jnp.exp(e)
                ds = (p * dp).astype(BF)
                outs.append((ak + _dot(jnp.where(hd["own"], qt2, hd["fill"]), ds), av + _dot(dot2, p.astype(BF)),
                             dq + _dot(ds, hd["kh"])))
            for hd, (ak, av, dq) in zip(heads, outs):
                acck[hd["hh"]] = ak
                accv[hd["hh"]] = av
                dqa_ref[rs, hd["hs"]] = dq

        def loop_body(qb, carry):
            tile(qb, False)
            return carry

        tile(j, True)
        lax.fori_loop(j + 1, nb, loop_body, 0)
        dk_pair = dv_pair = None
        dfrows = jnp.zeros((LANES, t), F32)
        for hd in heads:
            ak, av = acck[hd["hh"]], accv[hd["hh"]]
            dfrows = jnp.where(row == 2 * hp + hd["hh"], -ak[hd["aux"]:hd["aux"] + 1, :], dfrows)
            dk_pair = ak if hd["hh"] == 0 else jnp.where(row < HEAD_DIM, dk_pair, ak)
            dv_pair = av if hd["hh"] == 0 else jnp.where(row < HEAD_DIM, dv_pair, av)
        dk_ref[...] = dk_pair.T.astype(BF)
        dv_ref[...] = dv_pair.T.astype(BF)
        dfk_ref[0] = dfrows.T
        pl.when((hp == npair - 1) & (j == nb - 1))(finish)

    pair_cols = pl.BlockSpec((s, 2 * LANES), lambda hp, j: (0, hp))
    pair_rows = pl.BlockSpec((nb, LANES, t), lambda hp, j: (0, hp, 0))
    any_ = pl.BlockSpec(memory_space=pl.ANY)
    return pl.pallas_call(
        body, name="attn_bwd", grid=(npair, nb),
        in_specs=[pl.BlockSpec((t, 2 * LANES), lambda hp, j: (j, hp)),
                  pl.BlockSpec((t, LANES), lambda hp, j: (j, 2 * npair + hp)),
                  pair_cols, pair_cols, pair_rows, pair_rows] + [any_] * n,
        out_specs=[pl.BlockSpec((t, LANES), lambda hp, j: (j, hp)), pl.BlockSpec((t, LANES), lambda hp, j: (j, hp)),
                   pl.BlockSpec((1, t, LANES), lambda hp, j: (hp, j, 0)), pair_cols] + [any_] * n,
        out_shape=[jax.ShapeDtypeStruct((s, ATTN_W), BF), jax.ShapeDtypeStruct((s, ATTN_W), BF),
                   jax.ShapeDtypeStruct((npair, s, LANES), F32), jax.ShapeDtypeStruct((s, N_HEADS * LANES), F32)]
        + [jax.ShapeDtypeStruct(b.shape, b.dtype) for b in blocks],
        scratch_shapes=[pltpu.VMEM((2, LANES, t), F32), pltpu.VMEM((2, LANES, t), F32)] + _exchange_sems(n),
        compiler_params=_params(VMEM_BIG),
    )(ka, qkv, qab, doa, qt, dot, *blocks)


def _bwd_in(x, dx2, dzc, dqa, dk, dv, dfk, fpre, w_zc, w_qkv, w_f, g_mix, tm):
    s = x.shape[0]
    nb = s // tm

    def body(x_ref, dx2_ref, dzc_ref, dqa_ref, dk_ref, dv_ref, dfk_ref, fpre_ref, wzc_ref, wqkv_ref, wf_ref,
             gm_ref, gx_ref, dqb_ref, dfb_ref, dgm_ref, dbf_ref, carry):
        i = pl.program_id(0)

        @pl.when(i == 0)
        def _():
            carry[...] = jnp.zeros_like(carry)
            dgm_ref[...] = jnp.zeros_like(dgm_ref)
            dbf_ref[...] = jnp.zeros_like(dbf_ref)

        lane = _cols((tm, LANES))
        dfq = jnp.zeros((tm, LANES), F32)
        for hp in range(N_HEADS // 2):
            a = dqa_ref[:, 2 * hp * LANES:(2 * hp + 1) * LANES]
            b = dqa_ref[:, (2 * hp + 1) * LANES:(2 * hp + 2) * LANES]
            dqb_ref[:, hp * LANES:(hp + 1) * LANES] = (jnp.where(lane < HEAD_DIM, a, b) * 0.125).astype(BF)
            dfq = jnp.where(lane == 2 * hp, a[:, HEAD_DIM:HEAD_DIM + 1], dfq)
            dfq = jnp.where(lane == 2 * hp + 1, b[:, 0:1], dfq)

        triu = (_rows((tm, tm)) <= _cols((tm, tm))).astype(F32)
        df_cum = dfq + ((dfk_ref[0] + dfk_ref[1]) + (dfk_ref[2] + dfk_ref[3]))
        dlogf = jnp.dot(triu, df_cum, precision=HI, preferred_element_type=F32) + carry[...]
        carry[...] = dlogf[0:1, :]
        fpre = fpre_ref[...]
        df = jnp.where(_cols(fpre.shape) < N_HEADS, dlogf / (1.0 + jnp.exp(fpre)), 0.0)
        dbf_ref[...] += jnp.sum(df, axis=0, keepdims=True)
        dfb = df.astype(BF)
        dfb_ref[...] = dfb

        dh1 = _dot_nt(dzc_ref[...], wzc_ref[...])
        dh1 += _dot_nt(dqb_ref[...], wqkv_ref[:, :ATTN_W])
        dh1 += _dot_nt(dk_ref[...], wqkv_ref[:, ATTN_W:2 * ATTN_W])
        dh1 += _dot_nt(dv_ref[...], wqkv_ref[:, 2 * ATTN_W:])
        dh1 += _dot_nt(dfb, wf_ref[...])
        xv = x_ref[...]
        r1 = _rms(xv)
        dgm_ref[...] += jnp.sum(dh1 * (xv * r1), axis=0, keepdims=True)
        gx_ref[...] = dx2_ref[...] + _rms_bwd(xv, r1, dh1 * gm_ref[...])

    blk = lambda c: pl.BlockSpec((tm, c), lambda i: (nb - 1 - i, 0))
    return pl.pallas_call(
        body, name="bwd_in", grid=(nb,),
        in_specs=[blk(D_MODEL), blk(D_MODEL), blk(3 * CONV_CH), blk(N_HEADS * LANES), blk(ATTN_W), blk(ATTN_W),
                  pl.BlockSpec((N_HEADS // 2, tm, LANES), lambda i: (0, nb - 1 - i, 0)), blk(LANES),
                  _full(w_zc.shape), _full(w_qkv.shape), _full(w_f.shape), _full((1, D_MODEL))],
        out_specs=[blk(D_MODEL), blk(ATTN_W), blk(LANES), _full((1, D_MODEL)), _full((1, LANES))],
        out_shape=[jax.ShapeDtypeStruct((s, D_MODEL), F32), jax.ShapeDtypeStruct((s, ATTN_W), BF),
                   jax.ShapeDtypeStruct((s, LANES), BF),
                   jax.ShapeDtypeStruct((1, D_MODEL), F32), jax.ShapeDtypeStruct((1, LANES), F32)],
        scratch_shapes=[pltpu.VMEM((1, LANES), F32)],
        compiler_params=_params(VMEM_MID),
    )(x, dx2, dzc, dqa, dk, dv, dfk, fpre, w_zc, w_qkv, w_f, g_mix)


def _matmul_tn(a, b, bn, bk, name):
    s, m = a.shape
    n = b.shape[1]

    def body(a_ref, b_ref, o_ref):
        @pl.when(pl.program_id(1) == 0)
        def _():
            o_ref[...] = jnp.zeros_like(o_ref)

        o_ref[...] += _dot_tn(a_ref[...], b_ref[...])

    return pl.pallas_call(
        body, name=name, grid=(n // bn, s // bk),
        in_specs=[pl.BlockSpec((bk, m), lambda jn, k: (k, 0)), pl.BlockSpec((bk, bn), lambda jn, k: (k, jn))],
        out_specs=pl.BlockSpec((m, bn), lambda jn, k: (0, jn)),
        out_shape=jax.ShapeDtypeStruct((m, n), F32),
        compiler_params=_params(VMEM_MID),
    )(a, b)


def _flip(v, bit):
    return 1 - v if bit else v


def _all_gather(shards):
    n = len(shards)

    def body(*refs):
        start, forward, finish = _gather_phases(refs[:n], refs[n:2 * n], *refs[2 * n:])
        start()
        forward()
        finish()

    any_ = pl.BlockSpec(memory_space=pl.ANY)
    return pl.pallas_call(
        body, name="all_gather_weights",
        in_specs=[any_] * n, out_specs=[any_] * n,
        out_shape=_gathered_shapes(shards), scratch_shapes=_exchange_sems(n),
    )(*shards)


def _gathered_shapes(shards):
    return [jax.ShapeDtypeStruct((8,) + sh.shape, sh.dtype) for sh in shards]


def _exchange_sems(n):
    return [pltpu.SemaphoreType.DMA((7 * n,)), pltpu.SemaphoreType.DMA((7 * n,)), pltpu.SemaphoreType.DMA((n,))]


def _gather_phases(src, out, send_sems, recv_sems, loc_sems):
    n = len(src)
    x, y, c = lax.axis_index("x"), lax.axis_index("y"), lax.axis_index("c")
    me, sibling = (x, y, c), (x, y, 1 - c)
    chips = [(1 - x, y), (x, 1 - y), (1 - x, 1 - y)]

    def slot(a, px, py, pc):
        return out[a].at[4 * px + 2 * py + pc]

    def copy(a, k, block, to, from_src=False):
        return pltpu.make_async_remote_copy(
            src_ref=src[a] if from_src else slot(a, *block), dst_ref=slot(a, *block),
            send_sem=send_sems.at[7 * a + k], recv_sem=recv_sems.at[7 * a + k],
            device_id=to, device_id_type=MESH)

    def local(a):
        return pltpu.make_async_copy(src[a], slot(a, *me), loc_sems.at[a])

    def first(a):
        return [copy(a, 0, me, sibling, True)] + [copy(a, 1 + j, me, (*chip, c), True)
                                                  for j, chip in enumerate(chips)]

    def passed(a, j):
        return copy(a, 4 + j, (*chips[j], c), sibling)

    def start():
        for a in range(n):
            local(a).start()
            for cp in first(a):
                cp.start()

    def forward():
        for a in range(n):
            for j, chip in enumerate(chips):
                copy(a, 1 + j, (*chip, c), me).wait_recv()
                passed(a, j).start()

    def finish():
        for a in range(n):
            copy(a, 0, sibling, me).wait_recv()
            for j, chip in enumerate(chips):
                copy(a, 4 + j, (*chip, 1 - c), me).wait_recv()
        for a in range(n):
            for cp in first(a) + [passed(a, j) for j in range(3)]:
                cp.wait_send()
            local(a).wait()

    return start, forward, finish


def _scatter_phases(src, out, send_sems, recv_sems, loc_sems):
    n = len(src)
    masks = [((k >> 2) & 1, (k >> 1) & 1, k & 1) for k in range(1, 8)]
    x, y, c = lax.axis_index("x"), lax.axis_index("y"), lax.axis_index("c")
    me = 4 * x + 2 * y + c

    def copies():
        cps = []
        for a in range(n):
            cps.append(pltpu.make_async_copy(src[a].at[me], out[a].at[me], loc_sems.at[a]))
            for k, (mx, my, mc) in enumerate(masks):
                px, py, pc = _flip(x, mx), _flip(y, my), _flip(c, mc)
                cps.append(pltpu.make_async_remote_copy(
                    src_ref=src[a].at[4 * px + 2 * py + pc], dst_ref=out[a].at[me],
                    send_sem=send_sems.at[7 * a + k], recv_sem=recv_sems.at[7 * a + k],
                    device_id=(px, py, pc), device_id_type=MESH))
        return cps

    def start():
        for cp in copies():
            cp.start()

    def finish():
        for cp in copies():
            cp.wait()

    return start, finish


def _all_to_all(blocks, name):
    n = len(blocks)

    def body(*refs):
        start, finish = _scatter_phases(refs[:n], refs[n:2 * n], *refs[2 * n:])
        start()
        finish()

    any_ = pl.BlockSpec(memory_space=pl.ANY)
    return pl.pallas_call(
        body, name=name,
        in_specs=[any_] * n, out_specs=[any_] * n,
        out_shape=[jax.ShapeDtypeStruct(b.shape, b.dtype) for b in blocks], scratch_shapes=_exchange_sems(n),
    )(*blocks)


def _adamw(parts, w, m, v, br, name):
    g8, r, c = parts.shape
    c1 = 1.0 - ADAM_B1 ** ADAM_STEP
    c2 = 1.0 - ADAM_B2 ** ADAM_STEP

    def body(p_ref, w_ref, m_ref, v_ref, g_ref, d_ref, m2_ref, v2_ref):
        g = p_ref[0]
        for d in range(1, g8):
            g = g + p_ref[d]
        m2 = ADAM_B1 * m_ref[...] + (1.0 - ADAM_B1) * g
        v2 = ADAM_B2 * v_ref[...] + (1.0 - ADAM_B2) * (g * g)
        g_ref[...] = g
        m2_ref[...] = m2
        v2_ref[...] = v2
        d_ref[...] = -ADAM_LR * ((m2 / c1) / (jnp.sqrt(v2 / c2) + ADAM_EPS) + ADAM_WD * w_ref[...])

    blk = pl.BlockSpec((br, c), lambda i: (i, 0))
    out = jax.ShapeDtypeStruct((r, c), F32)
    return pl.pallas_call(
        body, name=name, grid=(r // br,),
        in_specs=[pl.BlockSpec((g8, br, c), lambda i: (0, i, 0)), blk, blk, blk],
        out_specs=[blk] * 4, out_shape=[out] * 4,
        compiler_params=_params(VMEM_MID),
    )(parts, w, m, v)


def _pad_rows(a, rows):
    return jnp.pad(a, ((0, rows - a.shape[0]), (0, 0)))


_SMALL =(("g_mix", 1024), ("b_f", 8), ("g_conv_out", 512), ("g_attn_out", 512), ("g_ffn", 1024), ("g_final", 1024))
_SMALL_ROWS = 40


def _pack_small(vals):
    flat = jnp.concatenate([vals[k].reshape(-1) for k, _ in _SMALL])
    return jnp.pad(flat, (0, _SMALL_ROWS * LANES - flat.shape[0])).reshape(_SMALL_ROWS, LANES)


def _unpack_small(packed, shapes):
    flat = packed.reshape(-1)
    out, off = {}, 0
    for k, n in _SMALL:
        out[k] = flat[off:off + n].reshape(shapes[k])
        off += n
    return out


def _col_blocks(a):
    r, c8 = a.shape
    return jnp.transpose(a.reshape(r, 8, c8 // 8), (1, 0, 2))


def _from_col_blocks(a):
    g, r, c = a.shape
    return jnp.transpose(a, (1, 0, 2)).reshape(r, g * c)


def kernel(x, g_mix, w_in, b_f, w_conv, g_conv_out, g_attn_out, w_o, g_ffn, w_up, w_ffn_conv, w_down, g_final, loss_target, m_g_mix, m_w_in, m_b_f, m_w_conv, m_g_conv_out, m_g_attn_out, m_w_o, m_g_ffn, m_w_up, m_w_ffn_conv, m_w_down, m_g_final, v_g_mix, v_w_in, v_b_f, v_w_conv, v_g_conv_out, v_g_attn_out, v_w_o, v_g_ffn, v_w_up, v_w_ffn_conv, v_w_down, v_g_final):
    w = dict(g_mix=g_mix, w_in=w_in[0], b_f=b_f, w_conv=w_conv[0], g_conv_out=g_conv_out, g_attn_out=g_attn_out,
             w_o=w_o[0], g_ffn=g_ffn, w_up=w_up[0], w_ffn_conv=w_ffn_conv[0], w_down=w_down[0], g_final=g_final)
    m = dict(g_mix=m_g_mix, w_in=m_w_in[0], b_f=m_b_f, w_conv=m_w_conv[0], g_conv_out=m_g_conv_out,
             g_attn_out=m_g_attn_out, w_o=m_w_o[0], g_ffn=m_g_ffn, w_up=m_w_up[0], w_ffn_conv=m_w_ffn_conv[0],
             w_down=m_w_down[0], g_final=m_g_final)
    v = dict(g_mix=v_g_mix, w_in=v_w_in[0], b_f=v_b_f, w_conv=v_w_conv[0], g_conv_out=v_g_conv_out,
             g_attn_out=v_g_attn_out, w_o=v_w_o[0], g_ffn=v_g_ffn, w_up=v_w_up[0], w_ffn_conv=v_w_ffn_conv[0],
             w_down=v_w_down[0], g_final=v_g_final)
    shapes = dict(g_mix=g_mix.shape, w_in=w_in.shape, b_f=b_f.shape, w_conv=w_conv.shape,
                  g_conv_out=g_conv_out.shape, g_attn_out=g_attn_out.shape, w_o=w_o.shape, g_ffn=g_ffn.shape,
                  w_up=w_up.shape, w_ffn_conv=w_ffn_conv.shape, w_down=w_down.shape, g_final=g_final.shape)

    tm = t = 512
    tf = 256
    xs, tgt = x[0], loss_target[0]
    seq = xs.shape[0]
    assert seq % tm == 0 and seq % tf == 0
    bk = 1024 if seq % 1024 == 0 else 512

    g_in, g_conv = _all_gather([w["w_in"].astype(BF), w["w_conv"]])
    w_in_f = _from_col_blocks(g_in)
    w_zc = w_in_f[:, :3 * CONV_CH]
    w_qkv = w_in_f[:, 3 * CONV_CH:3 * CONV_CH + 3 * ATTN_W]
    w_f = jnp.pad(w_in_f[:, 3 * CONV_CH + 3 * ATTN_W:], ((0, 0), (0, LANES - N_HEADS)))
    b_f_p = jnp.pad(b_f, ((0, 0), (0, LANES - N_HEADS)))
    w_conv_p = _pad_rows(_from_col_blocks(g_conv), 8)

    h1, zc, qkv, fpre, fcol, nc, qa, ka, qt, vt = _fwd_in(
        xs, g_mix, w_zc, w_qkv, w_f, b_f_p, w_conv_p, g_conv_out, tm)
    o, lcol, g_o, g_up, g_dn, g_fc = _attn_fwd(
        qa, ka, vt, t, [w["w_o"].astype(BF), w["w_up"].astype(BF), w["w_down"].astype(BF), w["w_ffn_conv"]])
    w_o_f = g_o.reshape(D_MODEL, D_MODEL)
    w_up_f = _from_col_blocks(g_up)
    w_dn_f = g_dn.reshape(D_FF, D_MODEL)
    w_fc_p = _pad_rows(_from_col_blocks(g_fc), 8)
    x2, h2, mix = _fwd_out(xs, nc, o, g_attn_out, w_o_f, g_ffn, tm)
    u0, act, dx3, dx3b, loss, dg_final = _fwd_ffn(h2, x2, tgt, w_up_f, w_fc_p, w_dn_f, g_final.reshape(1, D_MODEL), tf)
    loss = lax.psum(loss[0, 0], ("x", "y", "c"))

    du0, dx2, dx2b, dw_fc, dg_ffn = _bwd_ffn(dx3, dx3b, u0, x2, w_up_f.T, w_fc_p, w_dn_f.T, g_ffn, tf)
    dw_up = _matmul_tn(h2, du0, 1408, bk, "dw_up")
    dw_dn = _matmul_tn(act, dx3b, 512, bk, "dw_down")
    dw_o = _matmul_tn(mix, dx2b, 512, bk, "dw_o")
    dzc, doa, dot, qab, dw_conv, dg_attn, dg_conv = _bwd_out(
        dx2b, o, zc, fcol, lcol, qa, w_o_f, g_attn_out, g_conv_out, w_conv_p, t)
    dk, dv, dfk, dqa, r_up, r_dn, r_fc, r_o = _attn_bwd(
        qab, doa, ka, qkv, qt, dot, t,
        [_col_blocks(dw_up), dw_dn.reshape(8, D_FF // 8, D_MODEL), _col_blocks(dw_fc[:3]),
         dw_o.reshape(8, D_MODEL // 8, D_MODEL)])
    gx, dq, dfb, dg_mix, db_f = _bwd_in(xs, dx2, dzc, dqa, dk, dv, dfk, fpre, w_zc, w_qkv, w_f, g_mix, tm)
    dw_zc = _matmul_tn(h1, dzc, 3 * CONV_CH, bk, "dw_in_conv")
    dw_q = _matmul_tn(h1, dq, ATTN_W, bk, "dw_in_q")
    dw_k = _matmul_tn(h1, dk, ATTN_W, bk, "dw_in_k")
    dw_v = _matmul_tn(h1, dv, ATTN_W, bk, "dw_in_v")
    dw_f = _matmul_tn(h1, dfb, LANES, bk, "dw_in_f")
    dw_in = jnp.concatenate([dw_zc, dw_q, dw_k, dw_v, dw_f[:, :N_HEADS]], axis=1)
    small = dict(g_mix=dg_mix, b_f=db_f[:, :N_HEADS], g_conv_out=dg_conv, g_attn_out=dg_attn, g_ffn=dg_ffn,
                 g_final=dg_final)
    r_in, r_conv, r_small = _all_to_all(
        [_col_blocks(dw_in), _col_blocks(dw_conv[:3]), jnp.broadcast_to(_pack_small(small), (8, _SMALL_ROWS, LANES))],
        "all_to_all_grads")
    big = ("w_in", "w_o", "w_up", "w_down", "w_conv", "w_ffn_conv")
    recv = [r_in, r_o, r_up, r_dn, r_conv, r_fc, r_small]

    rows = dict(w_in=256, w_o=128, w_up=256, w_down=176, w_conv=3, w_ffn_conv=3)
    g_out, d_out, m_out, v_out = {}, {}, {}, {}
    for a, k in enumerate(big):
        res = _adamw(recv[a], w[k], m[k], v[k], rows[k], "adamw_" + k)
        g_out[k], d_out[k], m_out[k], v_out[k] = [r.reshape(shapes[k]) for r in res]
    res = _adamw(recv[6], _pack_small(w), _pack_small(m), _pack_small(v), _SMALL_ROWS, "adamw_gains")
    for dst, packed in zip((g_out, d_out, m_out, v_out), res):
        dst.update(_unpack_small(packed, shapes))

    order = ("g_mix", "w_in", "b_f", "w_conv", "g_conv_out", "g_attn_out", "w_o", "g_ffn", "w_up", "w_ffn_conv",
             "w_down", "g_final")
    return (loss, gx.reshape(x.shape), *[g_out[k] for k in order], *[d_out[k] for k in order],
            *[m_out[k] for k in order], *[v_out[k] for k in order])
```

```python
import jax
import jax.numpy as jnp
import numpy as np
from jax import lax
from jax.experimental import pallas as pl
from jax.experimental.pallas import tpu as pltpu

F32 = jnp.float32
BF = jnp.bfloat16
HI = lax.Precision.HIGHEST
MESH = pl.DeviceIdType.MESH

D_MODEL = 1024
CONV_CH = 512
ATTN_W = 512
N_HEADS = 8
HEAD_DIM = 64
D_FF = 2816
FF_CHUNK = 256
N_FF_CHUNKS = D_FF // FF_CHUNK
EPS = 1e-6
NEG = -1e30
LANES = 128
VMEM_BIG = 56 * 1024 * 1024
VMEM_MID = 40 * 1024 * 1024

ADAM_LR = 0.001
ADAM_B1 = 0.9
ADAM_B2 = 0.999
ADAM_EPS = 1e-08
ADAM_WD = 0.01
ADAM_STEP = 10

NT = (((1,), (1,)), ((), ()))
TN = (((0,), (0,)), ((), ()))


def _dot(a, b):
    return jnp.dot(a, b, preferred_element_type=F32)


def _dot_nt(a, b):
    return lax.dot_general(a, b, NT, preferred_element_type=F32)


def _dot_tn(a, b):
    return lax.dot_general(a, b, TN, preferred_element_type=F32)


def _params(vmem):
    return pltpu.CompilerParams(vmem_limit_bytes=vmem)


def _rows(shape):
    return lax.broadcasted_iota(jnp.int32, shape, 0)


def _cols(shape):
    return lax.broadcasted_iota(jnp.int32, shape, 1)


def _shift_down(u, prev, k):
    n = prev.shape[0]
    out = pltpu.roll(u, k, 0)
    row = _rows(u.shape)
    for r in range(k):
        out = jnp.where(row == r, prev[n - k + r:n - k + r + 1, :].astype(u.dtype), out)
    return out


def _shift_up(u, nxt, k):
    tm = u.shape[0]
    out = pltpu.roll(u, tm - k, 0)
    row = _rows(u.shape)
    for r in range(k):
        out = jnp.where(row == tm - k + r, nxt[r:r + 1, :], out)
    return out


def _conv3(u, prev, w):
    u1 = _shift_down(u, prev, 1)
    u2 = _shift_down(u, prev, 2)
    return w[0:1, :] * u2 + w[1:2, :] * u1 + w[2:3, :] * u, u1, u2


def _conv3_bwd(d, nxt, w):
    return w[2:3, :] * d + w[1:2, :] * _shift_up(d, nxt, 1) + w[0:1, :] * _shift_up(d, nxt, 2)


def _rms(x):
    return lax.rsqrt(jnp.mean(x * x, axis=-1, keepdims=True) + EPS)


def _rms_bwd(x, r, dyg):
    return r * dyg - x * (r * r * r) * jnp.mean(dyg * x, axis=-1, keepdims=True)


def _full(shape):
    nd = len(shape)
    return pl.BlockSpec(shape, lambda i, _n=nd: (0,) * _n)


ONES_LANE = 24


def _bias_scatter():
    sc = np.zeros((LANES, 2 * N_HEADS * LANES), np.float32)
    koff = N_HEADS * LANES
    for h in range(N_HEADS):
        aux = HEAD_DIM * (1 - h % 2)
        for j in range(3):
            sc[8 * j + h, LANES * h + aux + j] = 1.0
            sc[ONES_LANE, koff + LANES * h + aux + j] = 1.0
            sc[ONES_LANE, LANES * h + aux + 3 + j] = 1.0
            sc[8 * j + h, koff + LANES * h + aux + 3 + j] = -1.0
    return jnp.asarray(sc, BF)


def _split_parts(v, one):
    hi = v.astype(BF).astype(F32)
    rest = v - hi
    mid = rest.astype(BF).astype(F32)
    lo = (rest - mid).astype(BF).astype(F32)
    parts = hi + pltpu.roll(mid, 8, 1) + pltpu.roll(lo, 16, 1)
    return jnp.where(_cols(parts.shape) == ONES_LANE, one, parts).astype(BF)


def _fwd_in(x, g_mix, w_zc, w_qkv, w_f, b_f, w_conv, g_conv_out, tm):
    s = x.shape[0]
    nb = s // tm
    aw = N_HEADS * LANES

    def body(x_ref, gm_ref, wzc_ref, wqkv_ref, wf_ref, bf_ref, wc_ref, gco_ref, sc_ref,
             h1_ref, zc_ref, qkv_ref, fpre_ref, fcol_ref, nc_ref, qa_ref, ka_ref, qt_ref, vt_ref, cu_ref, cf_ref):
        i = pl.program_id(0)

        @pl.when(i == 0)
        def _():
            cu_ref[...] = jnp.zeros_like(cu_ref)
            cf_ref[...] = jnp.zeros_like(cf_ref)

        xv = x_ref[...]
        hb = (xv * _rms(xv) * gm_ref[...]).astype(BF)
        h1_ref[...] = hb
        zc = _dot(hb, wzc_ref[...])
        zc_ref[...] = zc
        qkv = _dot(hb, wqkv_ref[...])
        qkv = jnp.where(_cols(qkv.shape) < ATTN_W, qkv * 0.125, qkv)
        qkvb = qkv.astype(BF)
        qkv_ref[...] = qkvb
        qt_ref[0] = qkv[:, :ATTN_W].T.astype(BF)
        vt_ref[0] = qkv[:, 2 * ATTN_W:].T.astype(BF)

        gb, gc, xc = zc[:, :CONV_CH], zc[:, CONV_CH:2 * CONV_CH], zc[:, 2 * CONV_CH:]
        u = gc * xc
        cv, _, _ = _conv3(u, cu_ref[...], wc_ref[...])
        cu_ref[...] = u[tm - 8:, :]
        y = gb * cv
        nc_ref[...] = (y * _rms(y) * gco_ref[...]).astype(BF)

        fpre = _dot(hb, wf_ref[...]) + bf_ref[...]
        fpre_ref[...] = fpre
        logf = jnp.minimum(fpre, 0.0) - jnp.log1p(jnp.exp(-jnp.abs(fpre)))
        logf = jnp.where(_cols(logf.shape) < N_HEADS, logf, 0.0)
        tri = (_rows((tm, tm)) >= _cols((tm, tm))).astype(F32)
        fcol = jnp.dot(tri, logf, precision=HI, preferred_element_type=F32) + cf_ref[...]
        cf_ref[...] = fcol[tm - 1:tm, :]
        fcol_ref[...] = fcol

        feat = _dot(_split_parts(fcol, 1.0), sc_ref[...]).astype(BF)
        lane = _cols((tm, LANES))
        for h in range(N_HEADS):
            hp, hh = divmod(h, 2)
            own = (lane >= HEAD_DIM * hh) & (lane < HEAD_DIM * (hh + 1))
            hs = slice(LANES * h, LANES * (h + 1))
            qa_ref[:, hs] = jnp.where(own, qkvb[:, LANES * hp:LANES * (hp + 1)], feat[:, hs])
            ka_ref[:, hs] = jnp.where(own, qkvb[:, ATTN_W + LANES * hp:ATTN_W + LANES * (hp + 1)],
                                      feat[:, aw + LANES * h:aw + LANES * (h + 1)])

    blk = lambda c: pl.BlockSpec((tm, c), lambda i: (i, 0))
    return pl.pallas_call(
        body, name="fwd_in", grid=(nb,),
        in_specs=[blk(D_MODEL), _full((1, D_MODEL)), _full(w_zc.shape), _full(w_qkv.shape), _full(w_f.shape),
                  _full((1, LANES)), _full((8, CONV_CH)), _full((1, CONV_CH)), _full((LANES, 2 * aw))],
        out_specs=[blk(D_MODEL), blk(3 * CONV_CH), blk(3 * ATTN_W), blk(LANES), blk(LANES),
                   blk(CONV_CH), blk(aw), blk(aw)] + [pl.BlockSpec((1, ATTN_W, tm), lambda i: (i, 0, 0))] * 2,
        out_shape=[jax.ShapeDtypeStruct((s, D_MODEL), BF), jax.ShapeDtypeStruct((s, 3 * CONV_CH), F32),
                   jax.ShapeDtypeStruct((s, 3 * ATTN_W), BF), jax.ShapeDtypeStruct((s, LANES), F32),
                   jax.ShapeDtypeStruct((s, LANES), F32),
                   jax.ShapeDtypeStruct((s, CONV_CH), BF), jax.ShapeDtypeStruct((s, aw), BF),
                   jax.ShapeDtypeStruct((s, aw), BF)] + [jax.ShapeDtypeStruct((nb, ATTN_W, tm), BF)] * 2,
        scratch_shapes=[pltpu.VMEM((8, CONV_CH), F32), pltpu.VMEM((1, LANES), F32)],
        compiler_params=_params(VMEM_MID),
    )(x, g_mix, w_zc, w_qkv, w_f, b_f, w_conv, g_conv_out, _bias_scatter())


def _pipeline_masked_last(last, produce, consume, buf_a, buf_b):
    produce(0, buf_a)

    def two_blocks(j, carry):
        blk = 2 * j
        produce(blk + 1, buf_b)
        consume(blk, buf_a, False)
        produce(blk + 2, buf_a)
        consume(blk + 1, buf_b, False)
        return carry

    lax.fori_loop(0, last // 2, two_blocks, 0)

    @pl.when(last % 2 == 0)
    def _():
        consume(last, buf_a, True)

    @pl.when(last % 2 == 1)
    def _():
        produce(last, buf_b)
        consume(last - 1, buf_a, False)
        consume(last, buf_b, True)


def _attn_fwd(qa, ka, vt, t, shards):
    s = qa.shape[0]
    nb = s // t
    n = len(shards)

    def body(qa_ref, ka_any, vt_any, *refs):
        shard_refs, (o_ref, l_ref), gathered = refs[:n], refs[n:n + 2], refs[n + 2:2 * n + 2]
        ka_scr, vt_scr, m_scr, acc_scr, ea_scr, eb_scr, sem = refs[2 * n + 2:2 * n + 9]
        i = pl.program_id(0)
        start, forward, finish = _gather_phases(shard_refs, gathered, *refs[2 * n + 9:])
        pl.when(i == 0)(start)
        pl.when(i == nb // 2)(forward)
        _load_weights(i, [(ka_any, ka_scr), (vt_any, vt_scr)], sem)

        causal_t = _rows((t, t)) <= _cols((t, t))
        row = _rows((LANES, t))
        lrows = jnp.zeros((LANES, t), F32)
        for hp in range(N_HEADS // 2):
            ps = slice(hp * LANES, (hp + 1) * LANES)
            m_scr[...] = jnp.full(m_scr.shape, NEG, F32)
            acc_scr[...] = jnp.zeros(acc_scr.shape, F32)
            heads = [dict(hh=hh, hs=slice((2 * hp + hh) * LANES, (2 * hp + hh + 1) * LANES),
                          aux=HEAD_DIM * (1 - hh), own=(row >= HEAD_DIM * hh) & (row < HEAD_DIM * (hh + 1)),
                          fill=jnp.where(row == HEAD_DIM * (1 - hh), 1.0, 0.0).astype(BF),
                          qh=qa_ref[:, (2 * hp + hh) * LANES:(2 * hp + hh + 1) * LANES]) for hh in range(2)]

            def scores(kb, dst, heads=heads):
                rs = pl.ds(pl.multiple_of(kb * t, t), t)
                for hd in heads:
                    dst[hd["hh"]] = _dot_nt(ka_scr[rs, hd["hs"]], hd["qh"])

            def consume(kb, src, masked, ps=ps, heads=heads):
                vt2 = vt_scr[kb, ps, :]
                m_olds = [m_scr[hd["hh"]:hd["hh"] + 1, :] for hd in heads]
                accs = [acc_scr[hd["hh"]] for hd in heads]
                m_news, acc_news = [], []
                for hd, m_old, acc in zip(heads, m_olds, accs):
                    e = src[hd["hh"]]
                    if masked:
                        e = jnp.where(causal_t, e, NEG)
                    m_new = jnp.maximum(m_old, jnp.max(e, axis=0, keepdims=True))
                    p = jnp.exp(e - m_new).astype(BF)
                    vta = jnp.where(hd["own"], vt2, hd["fill"])
                    acc_news.append(jnp.exp(m_old - m_new) * acc + _dot(vta, p))
                    m_news.append(m_new)
                for hd, m_new, acc in zip(heads, m_news, acc_news):
                    acc_scr[hd["hh"]] = acc
                    m_scr[hd["hh"]:hd["hh"] + 1, :] = m_new

            _pipeline_masked_last(i, scores, consume, ea_scr, eb_scr)

            o_pair = None
            for hd in heads:
                hh = hd["hh"]
                acc = acc_scr[hh]
                denom = acc[hd["aux"]:hd["aux"] + 1, :]
                o_h = acc / denom
                lrows = jnp.where(row == 2 * hp + hh, m_scr[hh:hh + 1, :] + jnp.log(denom), lrows)
                o_pair = o_h if hh == 0 else jnp.where(row < HEAD_DIM, o_pair, o_h)
            o_ref[:, ps] = o_pair.T
        l_ref[...] = lrows.T
        pl.when(i == nb - 1)(finish)

    any_ = pl.BlockSpec(memory_space=pl.ANY)
    return pl.pallas_call(
        body, name="attn_fwd", grid=(nb,),
        in_specs=[pl.BlockSpec((t, N_HEADS * LANES), lambda i: (i, 0)), any_, any_] + [any_] * n,
        out_specs=[pl.BlockSpec((t, ATTN_W), lambda i: (i, 0)), pl.BlockSpec((t, LANES), lambda i: (i, 0))]
        + [any_] * n,
        out_shape=[jax.ShapeDtypeStruct((s, ATTN_W), F32), jax.ShapeDtypeStruct((s, LANES), F32)]
        + _gathered_shapes(shards),
        scratch_shapes=[pltpu.VMEM(ka.shape, BF), pltpu.VMEM(vt.shape, BF), pltpu.VMEM((2, t), F32),
                        pltpu.VMEM((2, LANES, t), F32), pltpu.VMEM((2, t, t), F32), pltpu.VMEM((2, t, t), F32),
                        pltpu.SemaphoreType.DMA((2,))] + _exchange_sems(n),
        compiler_params=_params(VMEM_BIG),
    )(qa, ka, vt, *shards)


def _fwd_out(x, nc, o, g_attn_out, w_o, g_ffn, tm):
    s = x.shape[0]

    def body(x_ref, nc_ref, o_ref, ga_ref, wo_ref, gf_ref, x2_ref, h2_ref, mix_ref):
        ov = o_ref[...]
        na = (ov * _rms(ov) * ga_ref[...]).astype(BF)
        ncv = nc_ref[...]
        mix_ref[:, :CONV_CH] = ncv
        mix_ref[:, CONV_CH:] = na
        x2 = x_ref[...] + _dot(ncv, wo_ref[:CONV_CH, :]) + _dot(na, wo_ref[CONV_CH:, :])
        x2_ref[...] = x2
        h2_ref[...] = (x2 * _rms(x2) * gf_ref[...]).astype(BF)

    blk = lambda c: pl.BlockSpec((tm, c), lambda i: (i, 0))
    return pl.pallas_call(
        body, name="fwd_out", grid=(s // tm,),
        in_specs=[blk(D_MODEL), blk(CONV_CH), blk(ATTN_W), _full((1, ATTN_W)), _full(w_o.shape), _full((1, D_MODEL))],
        out_specs=[blk(D_MODEL), blk(D_MODEL), blk(D_MODEL)],
        out_shape=[jax.ShapeDtypeStruct((s, D_MODEL), F32), jax.ShapeDtypeStruct((s, D_MODEL), BF),
                   jax.ShapeDtypeStruct((s, D_MODEL), BF)],
        compiler_params=_params(VMEM_MID),
    )(x, nc, o, g_attn_out, w_o, g_ffn)


def _load_weights(i, pairs, sem):
    @pl.when(i == 0)
    def _():
        cps = [pltpu.make_async_copy(src, dst, sem.at[n]) for n, (src, dst) in enumerate(pairs)]
        for cp in cps:
            cp.start()
        for cp in cps:
            cp.wait()


def _ff_cols(j):
    return (slice(j * FF_CHUNK, (j + 1) * FF_CHUNK), slice(D_FF + j * FF_CHUNK, D_FF + (j + 1) * FF_CHUNK))


def _fwd_ffn(h2, x2, tgt, w_up, w_ffn_conv, w_dn, g_final, tm):
    s = x2.shape[0]

    def body(h2_ref, x2_ref, tgt_ref, wfc_ref, gfin_ref, wup_any, wdn_any,
             u0_ref, uc_ref, act_ref, dx3_ref, dx3b_ref, loss_ref, dgfin_ref,
             wup, wdn, carry, sem):
        i = pl.program_id(0)
        _load_weights(i, [(wup_any, wup), (wdn_any, wdn)], sem)

        @pl.when(i == 0)
        def _():
            carry[...] = jnp.zeros_like(carry)
            loss_ref[...] = jnp.zeros_like(loss_ref)
            dgfin_ref[...] = jnp.zeros_like(dgfin_ref)

        hb = h2_ref[...]

        def up(j):
            return [_dot(hb, wup[:, cc]) for cc in _ff_cols(j)]

        nxt = up(0)
        down = None
        for j in range(N_FF_CHUNKS):
            cur = nxt
            if j + 1 < N_FF_CHUNKS:
                nxt = up(j + 1)
            parts = []
            for cc, u0 in zip(_ff_cols(j), cur):
                u0_ref[:, cc] = u0.astype(BF)
                uu, _, _ = _conv3(u0, carry[:, cc], wfc_ref[:, cc])
                carry[:, cc] = u0[tm - 8:, :]
                uc_ref[:, cc] = uu.astype(BF)
                parts.append(uu)
            ua, ug = parts
            act = (ug * jax.nn.sigmoid(ug) * ua).astype(BF)
            ca = _ff_cols(j)[0]
            act_ref[:, ca] = act
            part = _dot(act, wdn[ca, :])
            down = part if down is None else down + part

        x3 = x2_ref[...] + down
        r3 = _rms(x3)
        gfin = gfin_ref[...]
        xn = x3 * r3
        diff = xn * gfin - tgt_ref[...]
        loss_ref[...] += jnp.sum(jnp.sum(diff * diff, axis=-1, keepdims=True), axis=0, keepdims=True) * (0.5 / D_MODEL)
        dy = diff * (1.0 / D_MODEL)
        dgfin_ref[...] += jnp.sum(dy * xn, axis=0, keepdims=True)
        dx3 = _rms_bwd(x3, r3, dy * gfin)
        dx3_ref[...] = dx3
        dx3b_ref[...] = dx3.astype(BF)

    blk = lambda c: pl.BlockSpec((tm, c), lambda i: (i, 0))
    any_ = pl.BlockSpec(memory_space=pl.ANY)
    return pl.pallas_call(
        body, name="fwd_ffn", grid=(s // tm,),
        in_specs=[blk(D_MODEL), blk(D_MODEL), blk(D_MODEL), _full((8, 2 * D_FF)), _full((1, D_MODEL)), any_, any_],
        out_specs=[blk(2 * D_FF), blk(2 * D_FF), blk(D_FF), blk(D_MODEL), blk(D_MODEL), _full((1, 1)),
                   _full((1, D_MODEL))],
        out_shape=[jax.ShapeDtypeStruct((s, 2 * D_FF), BF), jax.ShapeDtypeStruct((s, 2 * D_FF), BF),
                   jax.ShapeDtypeStruct((s, D_FF), BF),
                   jax.ShapeDtypeStruct((s, D_MODEL), F32), jax.ShapeDtypeStruct((s, D_MODEL), BF),
                   jax.ShapeDtypeStruct((1, 1), F32), jax.ShapeDtypeStruct((1, D_MODEL), F32)],
        scratch_shapes=[pltpu.VMEM(w_up.shape, BF), pltpu.VMEM(w_dn.shape, BF), pltpu.VMEM((8, 2 * D_FF), F32),
                        pltpu.SemaphoreType.DMA((2,))],
        compiler_params=_params(VMEM_BIG),
    )(h2, x2, tgt, w_ffn_conv, g_final, w_up, w_dn)


def _bwd_ffn(dx3, dx3b, u0, uc, x2, w_up, w_ffn_conv, w_dn, g_ffn, tm):
    s = x2.shape[0]
    nb = s // tm

    def body(dx3_ref, dx3b_ref, u0_ref, uc_ref, x2_ref, wfc_ref, gf_ref, wup_any, wdn_any,
             du0_ref, dx2_ref, dx2b_ref, dwfc_ref, dgf_ref,
             wup, wdn, carry, sem):
        i = pl.program_id(0)
        _load_weights(i, [(wup_any, wup), (wdn_any, wdn)], sem)

        @pl.when(i == 0)
        def _():
            carry[...] = jnp.zeros_like(carry)
            dwfc_ref[...] = jnp.zeros_like(dwfc_ref)
            dgf_ref[...] = jnp.zeros_like(dgf_ref)

        db = dx3b_ref[...]

        def dact_of(j):
            return _dot(db, wdn[:, _ff_cols(j)[0]])

        nxt = dact_of(0)
        dh2 = None
        for j in range(N_FF_CHUNKS):
            ca, cg = _ff_cols(j)
            dact = nxt
            if j + 1 < N_FF_CHUNKS:
                nxt = dact_of(j + 1)
            ua = uc_ref[:, ca].astype(F32)
            ug = uc_ref[:, cg].astype(F32)
            sg = jax.nn.sigmoid(ug)
            da = dact * (ug * sg)
            dg = dact * ua * (sg * (1.0 + ug * (1.0 - sg)))
            for cc, d in ((ca, da), (cg, dg)):
                nxt_rows = carry[:, cc]
                d1 = _shift_up(d, nxt_rows, 1)
                d2 = _shift_up(d, nxt_rows, 2)
                u0c = u0_ref[:, cc].astype(F32)
                w = wfc_ref[:, cc]
                dwfc_ref[0:1, cc] += jnp.sum(d2 * u0c, axis=0, keepdims=True)
                dwfc_ref[1:2, cc] += jnp.sum(d1 * u0c, axis=0, keepdims=True)
                dwfc_ref[2:3, cc] += jnp.sum(d * u0c, axis=0, keepdims=True)
                du0 = (w[2:3, :] * d + w[1:2, :] * d1 + w[0:1, :] * d2).astype(BF)
                carry[:, cc] = d[:8, :]
                du0_ref[:, cc] = du0
                part = _dot(du0, wup[cc, :])
                dh2 = part if dh2 is None else dh2 + part

        x2v = x2_ref[...]
        r2 = _rms(x2v)
        dgf_ref[...] += jnp.sum(dh2 * (x2v * r2), axis=0, keepdims=True)
        dx2 = dx3_ref[...] + _rms_bwd(x2v, r2, dh2 * gf_ref[...])
        dx2_ref[...] = dx2
        dx2b_ref[...] = dx2.astype(BF)

    blk = lambda c: pl.BlockSpec((tm, c), lambda i: (nb - 1 - i, 0))
    any_ = pl.BlockSpec(memory_space=pl.ANY)
    return pl.pallas_call(
        body, name="bwd_ffn", grid=(nb,),
        in_specs=[blk(D_MODEL), blk(D_MODEL), blk(2 * D_FF), blk(2 * D_FF), blk(D_MODEL), _full((8, 2 * D_FF)),
                  _full((1, D_MODEL)), any_, any_],
        out_specs=[blk(2 * D_FF), blk(D_MODEL), blk(D_MODEL), _full((8, 2 * D_FF)), _full((1, D_MODEL))],
        out_shape=[jax.ShapeDtypeStruct((s, 2 * D_FF), BF), jax.ShapeDtypeStruct((s, D_MODEL), F32),
                   jax.ShapeDtypeStruct((s, D_MODEL), BF), jax.ShapeDtypeStruct((8, 2 * D_FF), F32),
                   jax.ShapeDtypeStruct((1, D_MODEL), F32)],
        scratch_shapes=[pltpu.VMEM(w_up.shape, BF), pltpu.VMEM(w_dn.shape, BF), pltpu.VMEM((8, 2 * D_FF), F32),
                        pltpu.SemaphoreType.DMA((2,))],
        compiler_params=_params(VMEM_BIG),
    )(dx3, dx3b, u0, uc, x2, w_ffn_conv, g_ffn, w_up, w_dn)


def _bwd_out(dx2b, o, zc, fcol, lcol, qa, w_o, g_attn_out, g_conv_out, w_conv, tm):
    s = o.shape[0]
    nb = s // tm
    aw = N_HEADS * LANES

    def body(dx2b_ref, o_ref, zc_ref, halo_ref, fcol_ref, lcol_ref, qa_ref, wo_ref, ga_ref, gco_ref, wc_ref, sc_ref,
             dzc_ref, doa_ref, dot_ref, qab_ref, dwc_ref, dga_ref, dgco_ref, carry):
        i = pl.program_id(0)
        rb = nb - 1 - i

        @pl.when(i == 0)
        def _():
            carry[...] = jnp.zeros_like(carry)
            dwc_ref[...] = jnp.zeros_like(dwc_ref)
            dga_ref[...] = jnp.zeros_like(dga_ref)
            dgco_ref[...] = jnp.zeros_like(dgco_ref)

        dmix = _dot_nt(dx2b_ref[...], wo_ref[...])
        dnc, dna = dmix[:, :CONV_CH], dmix[:, CONV_CH:]

        ov = o_ref[...]
        ra = _rms(ov)
        dga_ref[...] += jnp.sum(dna * (ov * ra), axis=0, keepdims=True)
        do = _rms_bwd(ov, ra, dna * ga_ref[...])
        dob = do.astype(BF)
        dot_ref[0] = do.T.astype(BF)
        sel = (_rows((ATTN_W, LANES)) // HEAD_DIM == _cols((ATTN_W, LANES))).astype(F32)
        delta = jnp.dot(do * ov, sel, precision=HI, preferred_element_type=F32)
        featc = _dot(_split_parts(fcol_ref[...] - lcol_ref[...], 1.0), sc_ref[...]).astype(BF)
        featd = _dot(_split_parts(-delta, 0.0), sc_ref[...]).astype(BF)
        lane = _cols((tm, LANES))
        for h in range(N_HEADS):
            hp, hh = divmod(h, 2)
            own = (lane >= HEAD_DIM * hh) & (lane < HEAD_DIM * (hh + 1))
            hs = slice(LANES * h, LANES * (h + 1))
            qab_ref[:, hs] = jnp.where(own, qa_ref[:, hs], featc[:, hs])
            doa_ref[:, hs] = jnp.where(own, dob[:, LANES * hp:LANES * (hp + 1)], featd[:, hs])

        zc_v = zc_ref[...]
        gb, gc, xc = zc_v[:, :CONV_CH], zc_v[:, CONV_CH:2 * CONV_CH], zc_v[:, 2 * CONV_CH:]
        hal = halo_ref[...] * (rb > 0).astype(F32)
        u = gc * xc
        prev = hal[:, CONV_CH:2 * CONV_CH] * hal[:, 2 * CONV_CH:]
        wc = wc_ref[...]
        cv, u1, u2 = _conv3(u, prev, wc)
        y = gb * cv
        rc = _rms(y)
        dgco_ref[...] += jnp.sum(dnc * (y * rc), axis=0, keepdims=True)
        dy = _rms_bwd(y, rc, dnc * gco_ref[...])
        dcv = dy * gb
        dwc_ref[0:1, :] += jnp.sum(dcv * u2, axis=0, keepdims=True)
        dwc_ref[1:2, :] += jnp.sum(dcv * u1, axis=0, keepdims=True)
        dwc_ref[2:3, :] += jnp.sum(dcv * u, axis=0, keepdims=True)
        du = _conv3_bwd(dcv, carry[...], wc)
        carry[...] = dcv[:8, :]
        dzc_ref[:, :CONV_CH] = (dy * cv).astype(BF)
        dzc_ref[:, CONV_CH:2 * CONV_CH] = (du * xc).astype(BF)
        dzc_ref[:, 2 * CONV_CH:] = (du * gc).astype(BF)

    blk = lambda c: pl.BlockSpec((tm, c), lambda i: (nb - 1 - i, 0))
    halo = pl.BlockSpec((8, 3 * CONV_CH), lambda i: (jnp.maximum((nb - 1 - i) * (tm // 8) - 1, 0), 0))
    tr = pl.BlockSpec((1, ATTN_W, tm), lambda i: (nb - 1 - i, 0, 0))
    return pl.pallas_call(
        body, name="bwd_out", grid=(nb,),
        in_specs=[blk(D_MODEL), blk(ATTN_W), blk(3 * CONV_CH), halo, blk(LANES), blk(LANES), blk(aw),
                  _full(w_o.shape), _full((1, ATTN_W)), _full((1, CONV_CH)), _full((8, CONV_CH)), _full((LANES, aw))],
        out_specs=[blk(3 * CONV_CH), blk(aw), tr, blk(aw),
                   _full((8, CONV_CH)), _full((1, ATTN_W)), _full((1, CONV_CH))],
        out_shape=[jax.ShapeDtypeStruct((s, 3 * CONV_CH), BF),
                   jax.ShapeDtypeStruct((s, aw), BF), jax.ShapeDtypeStruct((nb, ATTN_W, tm), BF),
                   jax.ShapeDtypeStruct((s, aw), BF),
                   jax.ShapeDtypeStruct((8, CONV_CH), F32), jax.ShapeDtypeStruct((1, ATTN_W), F32),
                   jax.ShapeDtypeStruct((1, CONV_CH), F32)],
        scratch_shapes=[pltpu.VMEM((8, CONV_CH), F32)],
        compiler_params=_params(VMEM_MID),
    )(dx2b, o, zc, zc, fcol, lcol, qa, w_o, g_attn_out, g_conv_out, w_conv, _bias_scatter()[:, :aw])


def _attn_bwd(qab, doa, ka, qkv, qt, dot, t, blocks):
    s = qab.shape[0]
    nb = s // t
    npair = N_HEADS // 2
    n = len(blocks)

    def body(ka_ref, v_ref, qab_ref, doa_ref, qt_ref, dot_ref, *refs):
        block_refs, (dk_ref, dv_ref, dfk_ref, dqa_ref), received = refs[:n], refs[n:n + 4], refs[n + 4:2 * n + 4]
        acck, accv = refs[2 * n + 4:2 * n + 6]
        hp = pl.program_id(0)
        j = pl.program_id(1)
        start, finish = _scatter_phases(block_refs, received, *refs[2 * n + 6:])
        pl.when((hp == 0) & (j == 0))(start)

        @pl.when(j == 0)
        def _():
            dqa_ref[...] = jnp.zeros_like(dqa_ref)

        causal = _cols((t, t)) <= _rows((t, t))
        row = _rows((LANES, t))
        lane = _cols((t, LANES))
        acck[...] = jnp.zeros(acck.shape, F32)
        accv[...] = jnp.zeros(accv.shape, F32)
        v2 = v_ref[...]
        heads = []
        for hh in range(2):
            aux = HEAD_DIM * (1 - hh)
            ownl = (lane >= HEAD_DIM * hh) & (lane < HEAD_DIM * (hh + 1))
            ones3 = jnp.where((lane >= aux) & (lane < aux + 3), 1.0, 0.0).astype(BF)
            heads.append(dict(hh=hh, hs=slice(hh * LANES, (hh + 1) * LANES), aux=aux,
                              own=(row >= HEAD_DIM * hh) & (row < HEAD_DIM * (hh + 1)),
                              fill=jnp.where(row == aux, 1.0, 0.0).astype(BF),
                              kh=ka_ref[:, hh * LANES:(hh + 1) * LANES], vh=jnp.where(ownl, v2, ones3)))

        def tile(qb, masked):
            rs = pl.ds(pl.multiple_of(qb * t, t), t)
            qt2 = qt_ref[qb]
            dot2 = dot_ref[qb]
            es = [_dot_nt(qab_ref[rs, hd["hs"]], hd["kh"]) for hd in heads]
            dps = [_dot_nt(doa_ref[rs, hd["hs"]], hd["vh"]) for hd in heads]
            aks = [acck[hd["hh"]] for hd in heads]
            avs = [accv[hd["hh"]] for hd in heads]
            dqs = [dqa_ref[rs, hd["hs"]] for hd in heads]
            outs = []
            for hd, e, dp, ak, av, dq in zip(heads, es, dps, aks, avs, dqs):
                if masked:
                    e = jnp.where(causal, e, NEG)
                p = jnp.exp(e)
                ds = (p * dp).astype(BF)
                outs.append((ak + _dot(jnp.where(hd["own"], qt2, hd["fill"]), ds), av + _dot(dot2, p.astype(BF)),
                             dq + _dot(ds, hd["kh"])))
            for hd, (ak, av, dq) in zip(heads, outs):
                acck[hd["hh"]] = ak
                accv[hd["hh"]] = av
                dqa_ref[rs, hd["hs"]] = dq

        def loop_body(qb, carry):
            tile(qb, False)
            return carry

        tile(j, True)
        lax.fori_loop(j + 1, nb, loop_body, 0)
        dk_pair = dv_pair = None
        dfrows = jnp.zeros((LANES, t), F32)
        for hd in heads:
            ak, av = acck[hd["hh"]], accv[hd["hh"]]
            dfrows = jnp.where(row == 2 * hp + hd["hh"], -ak[hd["aux"]:hd["aux"] + 1, :], dfrows)
            dk_pair = ak if hd["hh"] == 0 else jnp.where(row < HEAD_DIM, dk_pair, ak)
            dv_pair = av if hd["hh"] == 0 else jnp.where(row < HEAD_DIM, dv_pair, av)
        dk_ref[...] = dk_pair.T.astype(BF)
        dv_ref[...] = dv_pair.T.astype(BF)
        dfk_ref[0] = dfrows.T
        pl.when((hp == npair - 1) & (j == nb - 1))(finish)

    pair_cols = pl.BlockSpec((s, 2 * LANES), lambda hp, j: (0, hp))
    pair_rows = pl.BlockSpec((nb, LANES, t), lambda hp, j: (0, hp, 0))
    any_ = pl.BlockSpec(memory_space=pl.ANY)
    return pl.pallas_call(
        body, name="attn_bwd", grid=(npair, nb),
        in_specs=[pl.BlockSpec((t, 2 * LANES), lambda hp, j: (j, hp)),
                  pl.BlockSpec((t, LANES), lambda hp, j: (j, 2 * npair + hp)),
                  pair_cols, pair_cols, pair_rows, pair_rows] + [any_] * n,
        out_specs=[pl.BlockSpec((t, LANES), lambda hp, j: (j, hp)), pl.BlockSpec((t, LANES), lambda hp, j: (j, hp)),
                   pl.BlockSpec((1, t, LANES), lambda hp, j: (hp, j, 0)), pair_cols] + [any_] * n,
        out_shape=[jax.ShapeDtypeStruct((s, ATTN_W), BF), jax.ShapeDtypeStruct((s, ATTN_W), BF),
                   jax.ShapeDtypeStruct((npair, s, LANES), F32), jax.ShapeDtypeStruct((s, N_HEADS * LANES), F32)]
        + [jax.ShapeDtypeStruct(b.shape, b.dtype) for b in blocks],
        scratch_shapes=[pltpu.VMEM((2, LANES, t), F32), pltpu.VMEM((2, LANES, t), F32)] + _exchange_sems(n),
        compiler_params=_params(VMEM_BIG),
    )(ka, qkv, qab, doa, qt, dot, *blocks)


def _bwd_in(x, dx2, dzc, dqa, dk, dv, dfk, fpre, w_zc, w_qkv, w_f, g_mix, tm):
    s = x.shape[0]
    nb = s // tm

    def body(x_ref, dx2_ref, dzc_ref, dqa_ref, dk_ref, dv_ref, dfk_ref, fpre_ref, wzc_ref, wqkv_ref, wf_ref,
             gm_ref, gx_ref, dqb_ref, dfb_ref, dgm_ref, dbf_ref, carry):
        i = pl.program_id(0)

        @pl.when(i == 0)
        def _():
            carry[...] = jnp.zeros_like(carry)
            dgm_ref[...] = jnp.zeros_like(dgm_ref)
            dbf_ref[...] = jnp.zeros_like(dbf_ref)

        lane = _cols((tm, LANES))
        dfq = jnp.zeros((tm, LANES), F32)
        for hp in range(N_HEADS // 2):
            a = dqa_ref[:, 2 * hp * LANES:(2 * hp + 1) * LANES]
            b = dqa_ref[:, (2 * hp + 1) * LANES:(2 * hp + 2) * LANES]
            dqb_ref[:, hp * LANES:(hp + 1) * LANES] = (jnp.where(lane < HEAD_DIM, a, b) * 0.125).astype(BF)
            dfq = jnp.where(lane == 2 * hp, a[:, HEAD_DIM:HEAD_DIM + 1], dfq)
            dfq = jnp.where(lane == 2 * hp + 1, b[:, 0:1], dfq)

        triu = (_rows((tm, tm)) <= _cols((tm, tm))).astype(F32)
        df_cum = dfq + ((dfk_ref[0] + dfk_ref[1]) + (dfk_ref[2] + dfk_ref[3]))
        dlogf = jnp.dot(triu, df_cum, precision=HI, preferred_element_type=F32) + carry[...]
        carry[...] = dlogf[0:1, :]
        fpre = fpre_ref[...]
        df = jnp.where(_cols(fpre.shape) < N_HEADS, dlogf / (1.0 + jnp.exp(fpre)), 0.0)
        dbf_ref[...] += jnp.sum(df, axis=0, keepdims=True)
        dfb = df.astype(BF)
        dfb_ref[...] = dfb

        dh1 = _dot_nt(dzc_ref[...], wzc_ref[...])
        dh1 += _dot_nt(dqb_ref[...], wqkv_ref[:, :ATTN_W])
        dh1 += _dot_nt(dk_ref[...], wqkv_ref[:, ATTN_W:2 * ATTN_W])
        dh1 += _dot_nt(dv_ref[...], wqkv_ref[:, 2 * ATTN_W:])
        dh1 += _dot_nt(dfb, wf_ref[...])
        xv = x_ref[...]
        r1 = _rms(xv)
        dgm_ref[...] += jnp.sum(dh1 * (xv * r1), axis=0, keepdims=True)
        gx_ref[...] = dx2_ref[...] + _rms_bwd(xv, r1, dh1 * gm_ref[...])

    blk = lambda c: pl.BlockSpec((tm, c), lambda i: (nb - 1 - i, 0))
    return pl.pallas_call(
        body, name="bwd_in", grid=(nb,),
        in_specs=[blk(D_MODEL), blk(D_MODEL), blk(3 * CONV_CH), blk(N_HEADS * LANES), blk(ATTN_W), blk(ATTN_W),
                  pl.BlockSpec((N_HEADS // 2, tm, LANES), lambda i: (0, nb - 1 - i, 0)), blk(LANES),
                  _full(w_zc.shape), _full(w_qkv.shape), _full(w_f.shape), _full((1, D_MODEL))],
        out_specs=[blk(D_MODEL), blk(ATTN_W), blk(LANES), _full((1, D_MODEL)), _full((1, LANES))],
        out_shape=[jax.ShapeDtypeStruct((s, D_MODEL), F32), jax.ShapeDtypeStruct((s, ATTN_W), BF),
                   jax.ShapeDtypeStruct((s, LANES), BF),
                   jax.ShapeDtypeStruct((1, D_MODEL), F32), jax.ShapeDtypeStruct((1, LANES), F32)],
        scratch_shapes=[pltpu.VMEM((1, LANES), F32)],
        compiler_params=_params(VMEM_MID),
    )(x, dx2, dzc, dqa, dk, dv, dfk, fpre, w_zc, w_qkv, w_f, g_mix)


def _matmul_tn(a, b, bn, bk, name):
    s, m = a.shape
    n = b.shape[1]

    def body(a_ref, b_ref, o_ref):
        @pl.when(pl.program_id(1) == 0)
        def _():
            o_ref[...] = jnp.zeros_like(o_ref)

        o_ref[...] += _dot_tn(a_ref[...], b_ref[...])

    return pl.pallas_call(
        body, name=name, grid=(n // bn, s // bk),
        in_specs=[pl.BlockSpec((bk, m), lambda jn, k: (k, 0)), pl.BlockSpec((bk, bn), lambda jn, k: (k, jn))],
        out_specs=pl.BlockSpec((m, bn), lambda jn, k: (0, jn)),
        out_shape=jax.ShapeDtypeStruct((m, n), F32),
        compiler_params=_params(VMEM_MID),
    )(a, b)


def _flip(v, bit):
    return 1 - v if bit else v


def _all_gather(shards):
    n = len(shards)

    def body(*refs):
        start, forward, finish = _gather_phases(refs[:n], refs[n:2 * n], *refs[2 * n:])
        start()
        forward()
        finish()

    any_ = pl.BlockSpec(memory_space=pl.ANY)
    return pl.pallas_call(
        body, name="all_gather_weights",
        in_specs=[any_] * n, out_specs=[any_] * n,
        out_shape=_gathered_shapes(shards), scratch_shapes=_exchange_sems(n),
    )(*shards)


def _gathered_shapes(shards):
    return [jax.ShapeDtypeStruct((8,) + sh.shape, sh.dtype) for sh in shards]


def _exchange_sems(n):
    return [pltpu.SemaphoreType.DMA((7 * n,)), pltpu.SemaphoreType.DMA((7 * n,)), pltpu.SemaphoreType.DMA((n,))]


def _gather_phases(src, out, send_sems, recv_sems, loc_sems):
    n = len(src)
    x, y, c = lax.axis_index("x"), lax.axis_index("y"), lax.axis_index("c")
    me, sibling = (x, y, c), (x, y, 1 - c)
    chips = [(1 - x, y), (x, 1 - y), (1 - x, 1 - y)]

    def slot(a, px, py, pc):
        return out[a].at[4 * px + 2 * py + pc]

    def copy(a, k, block, to, from_src=False):
        return pltpu.make_async_remote_copy(
            src_ref=src[a] if from_src else slot(a, *block), dst_ref=slot(a, *block),
            send_sem=send_sems.at[7 * a + k], recv_sem=recv_sems.at[7 * a + k],
            device_id=to, device_id_type=MESH)

    def local(a):
        return pltpu.make_async_copy(src[a], slot(a, *me), loc_sems.at[a])

    def first(a):
        return [copy(a, 0, me, sibling, True)] + [copy(a, 1 + j, me, (*chip, c), True)
                                                  for j, chip in enumerate(chips)]

    def passed(a, j):
        return copy(a, 4 + j, (*chips[j], c), sibling)

    def start():
        for a in range(n):
            local(a).start()
            for cp in first(a):
                cp.start()

    def forward():
        for a in range(n):
            for j, chip in enumerate(chips):
                copy(a, 1 + j, (*chip, c), me).wait_recv()
                passed(a, j).start()

    def finish():
        for a in range(n):
            copy(a, 0, sibling, me).wait_recv()
            for j, chip in enumerate(chips):
                copy(a, 4 + j, (*chip, 1 - c), me).wait_recv()
        for a in range(n):
            for cp in first(a) + [passed(a, j) for j in range(3)]:
                cp.wait_send()
            local(a).wait()

    return start, forward, finish


def _scatter_phases(src, out, send_sems, recv_sems, loc_sems):
    n = len(src)
    masks = [((k >> 2) & 1, (k >> 1) & 1, k & 1) for k in range(1, 8)]
    x, y, c = lax.axis_index("x"), lax.axis_index("y"), lax.axis_index("c")
    me = 4 * x + 2 * y + c

    def copies():
        cps = []
        for a in range(n):
            cps.append(pltpu.make_async_copy(src[a].at[me], out[a].at[me], loc_sems.at[a]))
            for k, (mx, my, mc) in enumerate(masks):
                px, py, pc = _flip(x, mx), _flip(y, my), _flip(c, mc)
                cps.append(pltpu.make_async_remote_copy(
                    src_ref=src[a].at[4 * px + 2 * py + pc], dst_ref=out[a].at[me],
                    send_sem=send_sems.at[7 * a + k], recv_sem=recv_sems.at[7 * a + k],
                    device_id=(px, py, pc), device_id_type=MESH))
        return cps

    def start():
        for cp in copies():
            cp.start()

    def finish():
        for cp in copies():
            cp.wait()

    return start, finish


def _all_to_all(blocks, name):
    n = len(blocks)

    def body(*refs):
        start, finish = _scatter_phases(refs[:n], refs[n:2 * n], *refs[2 * n:])
        start()
        finish()

    any_ = pl.BlockSpec(memory_space=pl.ANY)
    return pl.pallas_call(
        body, name=name,
        in_specs=[any_] * n, out_specs=[any_] * n,
        out_shape=[jax.ShapeDtypeStruct(b.shape, b.dtype) for b in blocks], scratch_shapes=_exchange_sems(n),
    )(*blocks)


def _adamw(parts, w, m, v, br, name):
    g8, r, c = parts.shape
    c1 = 1.0 - ADAM_B1 ** ADAM_STEP
    c2 = 1.0 - ADAM_B2 ** ADAM_STEP

    def body(p_ref, w_ref, m_ref, v_ref, g_ref, d_ref, m2_ref, v2_ref):
        g = p_ref[0]
        for d in range(1, g8):
            g = g + p_ref[d]
        m2 = ADAM_B1 * m_ref[...] + (1.0 - ADAM_B1) * g
        v2 = ADAM_B2 * v_ref[...] + (1.0 - ADAM_B2) * (g * g)
        g_ref[...] = g
        m2_ref[...] = m2
        v2_ref[...] = v2
        d_ref[...] = -ADAM_LR * ((m2 / c1) / (jnp.sqrt(v2 / c2) + ADAM_EPS) + ADAM_WD * w_ref[...])

    blk = pl.BlockSpec((br, c), lambda i: (i, 0))
    out = jax.ShapeDtypeStruct((r, c), F32)
    return pl.pallas_call(
        body, name=name, grid=(r // br,),
        in_specs=[pl.BlockSpec((g8, br, c), lambda i: (0, i, 0)), blk, blk, blk],
        out_specs=[blk] * 4, out_shape=[out] * 4,
        compiler_params=_params(VMEM_MID),
    )(parts, w, m, v)


def _pad_rows(a, rows):
    return jnp.pad(a, ((0, rows - a.shape[0]), (0, 0)))


_SMALL =(("g_mix", 1024), ("b_f", 8), ("g_conv_out", 512), ("g_attn_out", 512), ("g_ffn", 1024), ("g_final", 1024))
_SMALL_ROWS = 40


def _pack_small(vals):
    flat = jnp.concatenate([vals[k].reshape(-1) for k, _ in _SMALL])
    return jnp.pad(flat, (0, _SMALL_ROWS * LANES - flat.shape[0])).reshape(_SMALL_ROWS, LANES)


def _unpack_small(packed, shapes):
    flat = packed.reshape(-1)
    out, off = {}, 0
    for k, n in _SMALL:
        out[k] = flat[off:off + n].reshape(shapes[k])
        off += n
    return out


def _col_blocks(a):
    r, c8 = a.shape
    return jnp.transpose(a.reshape(r, 8, c8 // 8), (1, 0, 2))


def _from_col_blocks(a):
    g, r, c = a.shape
    return jnp.transpose(a, (1, 0, 2)).reshape(r, g * c)


def kernel(x, g_mix, w_in, b_f, w_conv, g_conv_out, g_attn_out, w_o, g_ffn, w_up, w_ffn_conv, w_down, g_final, loss_target, m_g_mix, m_w_in, m_b_f, m_w_conv, m_g_conv_out, m_g_attn_out, m_w_o, m_g_ffn, m_w_up, m_w_ffn_conv, m_w_down, m_g_final, v_g_mix, v_w_in, v_b_f, v_w_conv, v_g_conv_out, v_g_attn_out, v_w_o, v_g_ffn, v_w_up, v_w_ffn_conv, v_w_down, v_g_final):
    w = dict(g_mix=g_mix, w_in=w_in[0], b_f=b_f, w_conv=w_conv[0], g_conv_out=g_conv_out, g_attn_out=g_attn_out,
             w_o=w_o[0], g_ffn=g_ffn, w_up=w_up[0], w_ffn_conv=w_ffn_conv[0], w_down=w_down[0], g_final=g_final)
    m = dict(g_mix=m_g_mix, w_in=m_w_in[0], b_f=m_b_f, w_conv=m_w_conv[0], g_conv_out=m_g_conv_out,
             g_attn_out=m_g_attn_out, w_o=m_w_o[0], g_ffn=m_g_ffn, w_up=m_w_up[0], w_ffn_conv=m_w_ffn_conv[0],
             w_down=m_w_down[0], g_final=m_g_final)
    v = dict(g_mix=v_g_mix, w_in=v_w_in[0], b_f=v_b_f, w_conv=v_w_conv[0], g_conv_out=v_g_conv_out,
             g_attn_out=v_g_attn_out, w_o=v_w_o[0], g_ffn=v_g_ffn, w_up=v_w_up[0], w_ffn_conv=v_w_ffn_conv[0],
             w_down=v_w_down[0], g_final=v_g_final)
    shapes = dict(g_mix=g_mix.shape, w_in=w_in.shape, b_f=b_f.shape, w_conv=w_conv.shape,
                  g_conv_out=g_conv_out.shape, g_attn_out=g_attn_out.shape, w_o=w_o.shape, g_ffn=g_ffn.shape,
                  w_up=w_up.shape, w_ffn_conv=w_ffn_conv.shape, w_down=w_down.shape, g_final=g_final.shape)

    tm = t = 512
    tf = 256
    xs, tgt = x[0], loss_target[0]
    seq = xs.shape[0]
    assert seq % tm == 0 and seq % tf == 0
    bk = 1024 if seq % 1024 == 0 else 512

    g_in, g_conv = _all_gather([w["w_in"].astype(BF), w["w_conv"]])
    w_in_f = _from_col_blocks(g_in)
    w_zc = w_in_f[:, :3 * CONV_CH]
    w_qkv = w_in_f[:, 3 * CONV_CH:3 * CONV_CH + 3 * ATTN_W]
    w_f = jnp.pad(w_in_f[:, 3 * CONV_CH + 3 * ATTN_W:], ((0, 0), (0, LANES - N_HEADS)))
    b_f_p = jnp.pad(b_f, ((0, 0), (0, LANES - N_HEADS)))
    w_conv_p = _pad_rows(_from_col_blocks(g_conv), 8)

    h1, zc, qkv, fpre, fcol, nc, qa, ka, qt, vt = _fwd_in(
        xs, g_mix, w_zc, w_qkv, w_f, b_f_p, w_conv_p, g_conv_out, tm)
    o, lcol, g_o, g_up, g_dn, g_fc = _attn_fwd(
        qa, ka, vt, t, [w["w_o"].astype(BF), w["w_up"].astype(BF), w["w_down"].astype(BF), w["w_ffn_conv"]])
    w_o_f = g_o.reshape(D_MODEL, D_MODEL)
    w_up_f = _from_col_blocks(g_up)
    w_dn_f = g_dn.reshape(D_FF, D_MODEL)
    w_fc_p = _pad_rows(_from_col_blocks(g_fc), 8)
    x2, h2, mix = _fwd_out(xs, nc, o, g_attn_out, w_o_f, g_ffn, tm)
    u0, uc, act, dx3, dx3b, loss, dg_final = _fwd_ffn(
        h2, x2, tgt, w_up_f, w_fc_p, w_dn_f, g_final.reshape(1, D_MODEL), tf)
    loss = lax.psum(loss[0, 0], ("x", "y", "c"))

    du0, dx2, dx2b, dw_fc, dg_ffn = _bwd_ffn(dx3, dx3b, u0, uc, x2, w_up_f.T, w_fc_p, w_dn_f.T, g_ffn, tf)
    dw_up = _matmul_tn(h2, du0, 1408, bk, "dw_up")
    dw_dn = _matmul_tn(act, dx3b, 512, bk, "dw_down")
    dw_o = _matmul_tn(mix, dx2b, 512, bk, "dw_o")
    dzc, doa, dot, qab, dw_conv, dg_attn, dg_conv = _bwd_out(
        dx2b, o, zc, fcol, lcol, qa, w_o_f, g_attn_out, g_conv_out, w_conv_p, t)
    dk, dv, dfk, dqa, r_up, r_dn, r_fc, r_o = _attn_bwd(
        qab, doa, ka, qkv, qt, dot, t,
        [_col_blocks(dw_up), dw_dn.reshape(8, D_FF // 8, D_MODEL), _col_blocks(dw_fc[:3]),
         dw_o.reshape(8, D_MODEL // 8, D_MODEL)])
    gx, dq, dfb, dg_mix, db_f = _bwd_in(xs, dx2, dzc, dqa, dk, dv, dfk, fpre, w_zc, w_qkv, w_f, g_mix, tm)
    dw_zc = _matmul_tn(h1, dzc, 3 * CONV_CH, bk, "dw_in_conv")
    dw_q = _matmul_tn(h1, dq, ATTN_W, bk, "dw_in_q")
    dw_k = _matmul_tn(h1, dk, ATTN_W, bk, "dw_in_k")
    dw_v = _matmul_tn(h1, dv, ATTN_W, bk, "dw_in_v")
    dw_f = _matmul_tn(h1, dfb, LANES, bk, "dw_in_f")
    dw_in = jnp.concatenate([dw_zc, dw_q, dw_k, dw_v, dw_f[:, :N_HEADS]], axis=1)
    small = dict(g_mix=dg_mix, b_f=db_f[:, :N_HEADS], g_conv_out=dg_conv, g_attn_out=dg_attn, g_ffn=dg_ffn,
                 g_final=dg_final)
    r_in, r_conv, r_small = _all_to_all(
        [_col_blocks(dw_in), _col_blocks(dw_conv[:3]), jnp.broadcast_to(_pack_small(small), (8, _SMALL_ROWS, LANES))],
        "all_to_all_grads")
    big = ("w_in", "w_o", "w_up", "w_down", "w_conv", "w_ffn_conv")
    recv = [r_in, r_o, r_up, r_dn, r_conv, r_fc, r_small]

    rows = dict(w_in=256, w_o=128, w_up=256, w_down=176, w_conv=3, w_ffn_conv=3)
    g_out, d_out, m_out, v_out = {}, {}, {}, {}
    for a, k in enumerate(big):
        res = _adamw(recv[a], w[k], m[k], v[k], rows[k], "adamw_" + k)
        g_out[k], d_out[k], m_out[k], v_out[k] = [r.reshape(shapes[k]) for r in res]
    res = _adamw(recv[6], _pack_small(w), _pack_small(m), _pack_small(v), _SMALL_ROWS, "adamw_gains")
    for dst, packed in zip((g_out, d_out, m_out, v_out), res):
        dst.update(_unpack_small(packed, shapes))

    order = ("g_mix", "w_in", "b_f", "w_conv", "g_conv_out", "g_attn_out", "w_o", "g_ffn", "w_up", "w_ffn_conv",
             "w_down", "g_final")
    return (loss, gx.reshape(x.shape), *[g_out[k] for k in order], *[d_out[k] for k in order],
            *[m_out[k] for k in order], *[v_out[k] for k in order])
```

```python
import jax
import jax.numpy as jnp
import numpy as np
from jax import lax
from jax.experimental import pallas as pl
from jax.experimental.pallas import tpu as pltpu

F32 = jnp.float32
BF = jnp.bfloat16
HI = lax.Precision.HIGHEST
MESH = pl.DeviceIdType.MESH

D_MODEL = 1024
CONV_CH = 512
ATTN_W = 512
N_HEADS = 8
HEAD_DIM = 64
D_FF = 2816
FF_CHUNK = 256
N_FF_CHUNKS = D_FF // FF_CHUNK
EPS = 1e-6
NEG = -1e30
LANES = 128
VMEM_BIG = 56 * 1024 * 1024
VMEM_MID = 40 * 1024 * 1024

ADAM_LR = 0.001
ADAM_B1 = 0.9
ADAM_B2 = 0.999
ADAM_EPS = 1e-08
ADAM_WD = 0.01
ADAM_STEP = 10

NT = (((1,), (1,)), ((), ()))
TN = (((0,), (0,)), ((), ()))


def _dot(a, b):
    return jnp.dot(a, b, preferred_element_type=F32)


def _dot_nt(a, b):
    return lax.dot_general(a, b, NT, preferred_element_type=F32)


def _dot_tn(a, b):
    return lax.dot_general(a, b, TN, preferred_element_type=F32)


def _params(vmem):
    return pltpu.CompilerParams(vmem_limit_bytes=vmem)


def _rows(shape):
    return lax.broadcasted_iota(jnp.int32, shape, 0)


def _cols(shape):
    return lax.broadcasted_iota(jnp.int32, shape, 1)


def _shift_down(u, prev, k):
    n = prev.shape[0]
    out = pltpu.roll(u, k, 0)
    row = _rows(u.shape)
    for r in range(k):
        out = jnp.where(row == r, prev[n - k + r:n - k + r + 1, :].astype(u.dtype), out)
    return out


def _shift_up(u, nxt, k):
    tm = u.shape[0]
    out = pltpu.roll(u, tm - k, 0)
    row = _rows(u.shape)
    for r in range(k):
        out = jnp.where(row == tm - k + r, nxt[r:r + 1, :], out)
    return out


def _conv3(u, prev, w):
    u1 = _shift_down(u, prev, 1)
    u2 = _shift_down(u, prev, 2)
    return w[0:1, :] * u2 + w[1:2, :] * u1 + w[2:3, :] * u, u1, u2


def _conv3_bwd(d, nxt, w):
    return w[2:3, :] * d + w[1:2, :] * _shift_up(d, nxt, 1) + w[0:1, :] * _shift_up(d, nxt, 2)


def _rms(x):
    return lax.rsqrt(jnp.mean(x * x, axis=-1, keepdims=True) + EPS)


def _rms_bwd(x, r, dyg):
    return r * dyg - x * (r * r * r) * jnp.mean(dyg * x, axis=-1, keepdims=True)


def _full(shape):
    nd = len(shape)
    return pl.BlockSpec(shape, lambda i, _n=nd: (0,) * _n)


ONES_LANE = 24


def _bias_scatter():
    sc = np.zeros((LANES, 2 * N_HEADS * LANES), np.float32)
    koff = N_HEADS * LANES
    for h in range(N_HEADS):
        aux = HEAD_DIM * (1 - h % 2)
        for j in range(3):
            sc[8 * j + h, LANES * h + aux + j] = 1.0
            sc[ONES_LANE, koff + LANES * h + aux + j] = 1.0
            sc[ONES_LANE, LANES * h + aux + 3 + j] = 1.0
            sc[8 * j + h, koff + LANES * h + aux + 3 + j] = -1.0
    return jnp.asarray(sc, BF)


def _split_parts(v, one):
    hi = v.astype(BF).astype(F32)
    rest = v - hi
    mid = rest.astype(BF).astype(F32)
    lo = (rest - mid).astype(BF).astype(F32)
    parts = hi + pltpu.roll(mid, 8, 1) + pltpu.roll(lo, 16, 1)
    return jnp.where(_cols(parts.shape) == ONES_LANE, one, parts).astype(BF)


def _fwd_in(x, g_mix, w_zc, w_qkv, w_f, b_f, w_conv, g_conv_out, tm):
    s = x.shape[0]
    nb = s // tm
    aw = N_HEADS * LANES

    def body(x_ref, gm_ref, wzc_ref, wqkv_ref, wf_ref, bf_ref, wc_ref, gco_ref, sc_ref,
             h1_ref, zc_ref, qkv_ref, fpre_ref, fcol_ref, nc_ref, qa_ref, ka_ref, qt_ref, vt_ref, cu_ref, cf_ref):
        i = pl.program_id(0)

        @pl.when(i == 0)
        def _():
            cu_ref[...] = jnp.zeros_like(cu_ref)
            cf_ref[...] = jnp.zeros_like(cf_ref)

        xv = x_ref[...]
        hb = (xv * _rms(xv) * gm_ref[...]).astype(BF)
        h1_ref[...] = hb
        zc = _dot(hb, wzc_ref[...])
        zc_ref[...] = zc
        qkv = _dot(hb, wqkv_ref[...])
        qkv = jnp.where(_cols(qkv.shape) < ATTN_W, qkv * 0.125, qkv)
        qkvb = qkv.astype(BF)
        qkv_ref[...] = qkvb
        qt_ref[0] = qkv[:, :ATTN_W].T.astype(BF)
        vt_ref[0] = qkv[:, 2 * ATTN_W:].T.astype(BF)

        gb, gc, xc = zc[:, :CONV_CH], zc[:, CONV_CH:2 * CONV_CH], zc[:, 2 * CONV_CH:]
        u = gc * xc
        cv, _, _ = _conv3(u, cu_ref[...], wc_ref[...])
        cu_ref[...] = u[tm - 8:, :]
        y = gb * cv
        nc_ref[...] = (y * _rms(y) * gco_ref[...]).astype(BF)

        fpre = _dot(hb, wf_ref[...]) + bf_ref[...]
        fpre_ref[...] = fpre
        logf = jnp.minimum(fpre, 0.0) - jnp.log1p(jnp.exp(-jnp.abs(fpre)))
        logf = jnp.where(_cols(logf.shape) < N_HEADS, logf, 0.0)
        tri = (_rows((tm, tm)) >= _cols((tm, tm))).astype(F32)
        fcol = jnp.dot(tri, logf, precision=HI, preferred_element_type=F32) + cf_ref[...]
        cf_ref[...] = fcol[tm - 1:tm, :]
        fcol_ref[...] = fcol

        feat = _dot(_split_parts(fcol, 1.0), sc_ref[...]).astype(BF)
        lane = _cols((tm, LANES))
        for h in range(N_HEADS):
            hp, hh = divmod(h, 2)
            own = (lane >= HEAD_DIM * hh) & (lane < HEAD_DIM * (hh + 1))
            hs = slice(LANES * h, LANES * (h + 1))
            qa_ref[:, hs] = jnp.where(own, qkvb[:, LANES * hp:LANES * (hp + 1)], feat[:, hs])
            ka_ref[:, hs] = jnp.where(own, qkvb[:, ATTN_W + LANES * hp:ATTN_W + LANES * (hp + 1)],
                                      feat[:, aw + LANES * h:aw + LANES * (h + 1)])

    blk = lambda c: pl.BlockSpec((tm, c), lambda i: (i, 0))
    return pl.pallas_call(
        body, name="fwd_in", grid=(nb,),
        in_specs=[blk(D_MODEL), _full((1, D_MODEL)), _full(w_zc.shape), _full(w_qkv.shape), _full(w_f.shape),
                  _full((1, LANES)), _full((8, CONV_CH)), _full((1, CONV_CH)), _full((LANES, 2 * aw))],
        out_specs=[blk(D_MODEL), blk(3 * CONV_CH), blk(3 * ATTN_W), blk(LANES), blk(LANES),
                   blk(CONV_CH), blk(aw), blk(aw)] + [pl.BlockSpec((1, ATTN_W, tm), lambda i: (i, 0, 0))] * 2,
        out_shape=[jax.ShapeDtypeStruct((s, D_MODEL), BF), jax.ShapeDtypeStruct((s, 3 * CONV_CH), F32),
                   jax.ShapeDtypeStruct((s, 3 * ATTN_W), BF), jax.ShapeDtypeStruct((s, LANES), F32),
                   jax.ShapeDtypeStruct((s, LANES), F32),
                   jax.ShapeDtypeStruct((s, CONV_CH), BF), jax.ShapeDtypeStruct((s, aw), BF),
                   jax.ShapeDtypeStruct((s, aw), BF)] + [jax.ShapeDtypeStruct((nb, ATTN_W, tm), BF)] * 2,
        scratch_shapes=[pltpu.VMEM((8, CONV_CH), F32), pltpu.VMEM((1, LANES), F32)],
        compiler_params=_params(VMEM_MID),
    )(x, g_mix, w_zc, w_qkv, w_f, b_f, w_conv, g_conv_out, _bias_scatter())


def _pipeline_masked_last(last, produce, consume, buf_a, buf_b):
    produce(0, buf_a)

    def two_blocks(j, carry):
        blk = 2 * j
        produce(blk + 1, buf_b)
        consume(blk, buf_a, False)
        produce(blk + 2, buf_a)
        consume(blk + 1, buf_b, False)
        return carry

    lax.fori_loop(0, last // 2, two_blocks, 0)

    @pl.when(last % 2 == 0)
    def _():
        consume(last, buf_a, True)

    @pl.when(last % 2 == 1)
    def _():
        produce(last, buf_b)
        consume(last - 1, buf_a, False)
        consume(last, buf_b, True)


def _attn_fwd(qa, ka, vt, t, shards):
    s = qa.shape[0]
    nb = s // t
    n = len(shards)

    def body(qa_ref, ka_any, vt_any, *refs):
        shard_refs, (o_ref, l_ref), gathered = refs[:n], refs[n:n + 2], refs[n + 2:2 * n + 2]
        ka_scr, vt_scr, m_scr, acc_scr, ea_scr, eb_scr, sem = refs[2 * n + 2:2 * n + 9]
        i = pl.program_id(0)
        start, forward, finish = _gather_phases(shard_refs, gathered, *refs[2 * n + 9:])
        pl.when(i == 0)(start)
        pl.when(i == nb // 2)(forward)
        _load_weights(i, [(ka_any, ka_scr), (vt_any, vt_scr)], sem)

        causal_t = _rows((t, t)) <= _cols((t, t))
        row = _rows((LANES, t))
        lrows = jnp.zeros((LANES, t), F32)
        for hp in range(N_HEADS // 2):
            ps = slice(hp * LANES, (hp + 1) * LANES)
            m_scr[...] = jnp.full(m_scr.shape, NEG, F32)
            acc_scr[...] = jnp.zeros(acc_scr.shape, F32)
            heads = [dict(hh=hh, hs=slice((2 * hp + hh) * LANES, (2 * hp + hh + 1) * LANES),
                          aux=HEAD_DIM * (1 - hh), own=(row >= HEAD_DIM * hh) & (row < HEAD_DIM * (hh + 1)),
                          fill=jnp.where(row == HEAD_DIM * (1 - hh), 1.0, 0.0).astype(BF),
                          qh=qa_ref[:, (2 * hp + hh) * LANES:(2 * hp + hh + 1) * LANES]) for hh in range(2)]

            def scores(kb, dst, heads=heads):
                rs = pl.ds(pl.multiple_of(kb * t, t), t)
                for hd in heads:
                    dst[hd["hh"]] = _dot_nt(ka_scr[rs, hd["hs"]], hd["qh"])

            def consume(kb, src, masked, ps=ps, heads=heads):
                vt2 = vt_scr[kb, ps, :]
                m_olds = [m_scr[hd["hh"]:hd["hh"] + 1, :] for hd in heads]
                accs = [acc_scr[hd["hh"]] for hd in heads]
                m_news, acc_news = [], []
                for hd, m_old, acc in zip(heads, m_olds, accs):
                    e = src[hd["hh"]]
                    if masked:
                        e = jnp.where(causal_t, e, NEG)
                    m_new = jnp.maximum(m_old, jnp.max(e, axis=0, keepdims=True))
                    p = jnp.exp(e - m_new).astype(BF)
                    vta = jnp.where(hd["own"], vt2, hd["fill"])
                    acc_news.append(jnp.exp(m_old - m_new) * acc + _dot(vta, p))
                    m_news.append(m_new)
                for hd, m_new, acc in zip(heads, m_news, acc_news):
                    acc_scr[hd["hh"]] = acc
                    m_scr[hd["hh"]:hd["hh"] + 1, :] = m_new

            _pipeline_masked_last(i, scores, consume, ea_scr, eb_scr)

            o_pair = None
            for hd in heads:
                hh = hd["hh"]
                acc = acc_scr[hh]
                denom = acc[hd["aux"]:hd["aux"] + 1, :]
                o_h = acc / denom
                lrows = jnp.where(row == 2 * hp + hh, m_scr[hh:hh + 1, :] + jnp.log(denom), lrows)
                o_pair = o_h if hh == 0 else jnp.where(row < HEAD_DIM, o_pair, o_h)
            o_ref[:, ps] = o_pair.T
        l_ref[...] = lrows.T
        pl.when(i == nb - 1)(finish)

    any_ = pl.BlockSpec(memory_space=pl.ANY)
    return pl.pallas_call(
        body, name="attn_fwd", grid=(nb,),
        in_specs=[pl.BlockSpec((t, N_HEADS * LANES), lambda i: (i, 0)), any_, any_] + [any_] * n,
        out_specs=[pl.BlockSpec((t, ATTN_W), lambda i: (i, 0)), pl.BlockSpec((t, LANES), lambda i: (i, 0))]
        + [any_] * n,
        out_shape=[jax.ShapeDtypeStruct((s, ATTN_W), F32), jax.ShapeDtypeStruct((s, LANES), F32)]
        + _gathered_shapes(shards),
        scratch_shapes=[pltpu.VMEM(ka.shape, BF), pltpu.VMEM(vt.shape, BF), pltpu.VMEM((2, t), F32),
                        pltpu.VMEM((2, LANES, t), F32), pltpu.VMEM((2, t, t), F32), pltpu.VMEM((2, t, t), F32),
                        pltpu.SemaphoreType.DMA((2,))] + _exchange_sems(n),
        compiler_params=_params(VMEM_BIG),
    )(qa, ka, vt, *shards)


def _fwd_out(x, nc, o, g_attn_out, w_o, g_ffn, tm):
    s = x.shape[0]

    def body(x_ref, nc_ref, o_ref, ga_ref, wo_ref, gf_ref, x2_ref, h2_ref, mix_ref):
        ov = o_ref[...]
        na = (ov * _rms(ov) * ga_ref[...]).astype(BF)
        ncv = nc_ref[...]
        mix_ref[:, :CONV_CH] = ncv
        mix_ref[:, CONV_CH:] = na
        x2 = x_ref[...] + _dot(ncv, wo_ref[:CONV_CH, :]) + _dot(na, wo_ref[CONV_CH:, :])
        x2_ref[...] = x2
        h2_ref[...] = (x2 * _rms(x2) * gf_ref[...]).astype(BF)

    blk = lambda c: pl.BlockSpec((tm, c), lambda i: (i, 0))
    return pl.pallas_call(
        body, name="fwd_out", grid=(s // tm,),
        in_specs=[blk(D_MODEL), blk(CONV_CH), blk(ATTN_W), _full((1, ATTN_W)), _full(w_o.shape), _full((1, D_MODEL))],
        out_specs=[blk(D_MODEL), blk(D_MODEL), blk(D_MODEL)],
        out_shape=[jax.ShapeDtypeStruct((s, D_MODEL), F32), jax.ShapeDtypeStruct((s, D_MODEL), BF),
                   jax.ShapeDtypeStruct((s, D_MODEL), BF)],
        compiler_params=_params(VMEM_MID),
    )(x, nc, o, g_attn_out, w_o, g_ffn)


def _load_weights(i, pairs, sem):
    @pl.when(i == 0)
    def _():
        cps = [pltpu.make_async_copy(src, dst, sem.at[n]) for n, (src, dst) in enumerate(pairs)]
        for cp in cps:
            cp.start()
        for cp in cps:
            cp.wait()


def _ff_cols(j):
    return (slice(j * FF_CHUNK, (j + 1) * FF_CHUNK), slice(D_FF + j * FF_CHUNK, D_FF + (j + 1) * FF_CHUNK))


def _fwd_ffn(h2, x2, tgt, w_up, w_ffn_conv, w_dn, g_final, tm):
    s = x2.shape[0]

    def body(h2_ref, x2_ref, tgt_ref, wfc_ref, gfin_ref, wup_any, wdn_any,
             u0_ref, uc_ref, act_ref, dx3_ref, dx3b_ref, loss_ref, dgfin_ref,
             wup, wdn, carry, sem):
        i = pl.program_id(0)
        _load_weights(i, [(wup_any, wup), (wdn_any, wdn)], sem)

        @pl.when(i == 0)
        def _():
            carry[...] = jnp.zeros_like(carry)
            loss_ref[...] = jnp.zeros_like(loss_ref)
            dgfin_ref[...] = jnp.zeros_like(dgfin_ref)

        hb = h2_ref[...]

        def up(j):
            return [_dot(hb, wup[:, cc]) for cc in _ff_cols(j)]

        nxt = up(0)
        down = None
        for j in range(N_FF_CHUNKS):
            cur = nxt
            if j + 1 < N_FF_CHUNKS:
                nxt = up(j + 1)
            parts = []
            for cc, u0 in zip(_ff_cols(j), cur):
                u0_ref[:, cc] = u0.astype(BF)
                uu, _, _ = _conv3(u0, carry[:, cc], wfc_ref[:, cc])
                carry[:, cc] = u0[tm - 8:, :]
                uc_ref[:, cc] = uu.astype(BF)
                parts.append(uu)
            ua, ug = parts
            act = (ug * jax.nn.sigmoid(ug) * ua).astype(BF)
            ca = _ff_cols(j)[0]
            act_ref[:, ca] = act
            part = _dot(act, wdn[ca, :])
            down = part if down is None else down + part

        x3 = x2_ref[...] + down
        r3 = _rms(x3)
        gfin = gfin_ref[...]
        xn = x3 * r3
        diff = xn * gfin - tgt_ref[...]
        loss_ref[...] += jnp.sum(jnp.sum(diff * diff, axis=-1, keepdims=True), axis=0, keepdims=True) * (0.5 / D_MODEL)
        dy = diff * (1.0 / D_MODEL)
        dgfin_ref[...] += jnp.sum(dy * xn, axis=0, keepdims=True)
        dx3 = _rms_bwd(x3, r3, dy * gfin)
        dx3_ref[...] = dx3
        dx3b_ref[...] = dx3.astype(BF)

    blk = lambda c: pl.BlockSpec((tm, c), lambda i: (i, 0))
    any_ = pl.BlockSpec(memory_space=pl.ANY)
    return pl.pallas_call(
        body, name="fwd_ffn", grid=(s // tm,),
        in_specs=[blk(D_MODEL), blk(D_MODEL), blk(D_MODEL), _full((8, 2 * D_FF)), _full((1, D_MODEL)), any_, any_],
        out_specs=[blk(2 * D_FF), blk(2 * D_FF), blk(D_FF), blk(D_MODEL), blk(D_MODEL), _full((1, 1)),
                   _full((1, D_MODEL))],
        out_shape=[jax.ShapeDtypeStruct((s, 2 * D_FF), BF), jax.ShapeDtypeStruct((s, 2 * D_FF), BF),
                   jax.ShapeDtypeStruct((s, D_FF), BF),
                   jax.ShapeDtypeStruct((s, D_MODEL), F32), jax.ShapeDtypeStruct((s, D_MODEL), BF),
                   jax.ShapeDtypeStruct((1, 1), F32), jax.ShapeDtypeStruct((1, D_MODEL), F32)],
        scratch_shapes=[pltpu.VMEM(w_up.shape, BF), pltpu.VMEM(w_dn.shape, BF), pltpu.VMEM((8, 2 * D_FF), F32),
                        pltpu.SemaphoreType.DMA((2,))],
        compiler_params=_params(VMEM_BIG),
    )(h2, x2, tgt, w_ffn_conv, g_final, w_up, w_dn)


def _bwd_ffn(dx3, dx3b, u0, uc, x2, w_up, w_ffn_conv, w_dn, g_ffn, tm):
    s = x2.shape[0]
    nb = s // tm

    def body(dx3_ref, dx3b_ref, u0_ref, uc_ref, x2_ref, wfc_ref, gf_ref, wup_any, wdn_any,
             du0_ref, dx2_ref, dx2b_ref, dwfc_ref, dgf_ref,
             wup, wdn, carry, sem):
        i = pl.program_id(0)
        _load_weights(i, [(wup_any, wup), (wdn_any, wdn)], sem)

        @pl.when(i == 0)
        def _():
            carry[...] = jnp.zeros_like(carry)
            dwfc_ref[...] = jnp.zeros_like(dwfc_ref)
            dgf_ref[...] = jnp.zeros_like(dgf_ref)

        db = dx3b_ref[...]

        def dact_of(j):
            return _dot(db, wdn[:, _ff_cols(j)[0]])

        nxt = dact_of(0)
        dh2 = None
        for j in range(N_FF_CHUNKS):
            ca, cg = _ff_cols(j)
            dact = nxt
            if j + 1 < N_FF_CHUNKS:
                nxt = dact_of(j + 1)
            ua = uc_ref[:, ca].astype(F32)
            ug = uc_ref[:, cg].astype(F32)
            sg = jax.nn.sigmoid(ug)
            da = dact * (ug * sg)
            dg = dact * ua * (sg * (1.0 + ug * (1.0 - sg)))
            for cc, d in ((ca, da), (cg, dg)):
                nxt_rows = carry[:, cc]
                d1 = _shift_up(d, nxt_rows, 1)
                d2 = _shift_up(d, nxt_rows, 2)
                u0c = u0_ref[:, cc].astype(F32)
                w = wfc_ref[:, cc]
                dwfc_ref[0:1, cc] += jnp.sum(d2 * u0c, axis=0, keepdims=True)
                dwfc_ref[1:2, cc] += jnp.sum(d1 * u0c, axis=0, keepdims=True)
                dwfc_ref[2:3, cc] += jnp.sum(d * u0c, axis=0, keepdims=True)
                du0 = (w[2:3, :] * d + w[1:2, :] * d1 + w[0:1, :] * d2).astype(BF)
                carry[:, cc] = d[:8, :]
                du0_ref[:, cc] = du0
                part = _dot(du0, wup[cc, :])
                dh2 = part if dh2 is None else dh2 + part

        x2v = x2_ref[...]
        r2 = _rms(x2v)
        dgf_ref[...] += jnp.sum(dh2 * (x2v * r2), axis=0, keepdims=True)
        dx2 = dx3_ref[...] + _rms_bwd(x2v, r2, dh2 * gf_ref[...])
        dx2_ref[...] = dx2
        dx2b_ref[...] = dx2.astype(BF)

    blk = lambda c: pl.BlockSpec((tm, c), lambda i: (nb - 1 - i, 0))
    any_ = pl.BlockSpec(memory_space=pl.ANY)
    return pl.pallas_call(
        body, name="bwd_ffn", grid=(nb,),
        in_specs=[blk(D_MODEL), blk(D_MODEL), blk(2 * D_FF), blk(2 * D_FF), blk(D_MODEL), _full((8, 2 * D_FF)),
                  _full((1, D_MODEL)), any_, any_],
        out_specs=[blk(2 * D_FF), blk(D_MODEL), blk(D_MODEL), _full((8, 2 * D_FF)), _full((1, D_MODEL))],
        out_shape=[jax.ShapeDtypeStruct((s, 2 * D_FF), BF), jax.ShapeDtypeStruct((s, D_MODEL), F32),
                   jax.ShapeDtypeStruct((s, D_MODEL), BF), jax.ShapeDtypeStruct((8, 2 * D_FF), F32),
                   jax.ShapeDtypeStruct((1, D_MODEL), F32)],
        scratch_shapes=[pltpu.VMEM(w_up.shape, BF), pltpu.VMEM(w_dn.shape, BF), pltpu.VMEM((8, 2 * D_FF), F32),
                        pltpu.SemaphoreType.DMA((2,))],
        compiler_params=_params(VMEM_BIG),
    )(dx3, dx3b, u0, uc, x2, w_ffn_conv, g_ffn, w_up, w_dn)


def _bwd_out(dx2b, o, zc, fcol, lcol, qa, w_o, g_attn_out, g_conv_out, w_conv, tm):
    s = o.shape[0]
    nb = s // tm
    aw = N_HEADS * LANES

    def body(dx2b_ref, o_ref, zc_ref, halo_ref, fcol_ref, lcol_ref, qa_ref, wo_ref, ga_ref, gco_ref, wc_ref, sc_ref,
             dzc_ref, doa_ref, dot_ref, qab_ref, dwc_ref, dga_ref, dgco_ref, carry):
        i = pl.program_id(0)
        rb = nb - 1 - i

        @pl.when(i == 0)
        def _():
            carry[...] = jnp.zeros_like(carry)
            dwc_ref[...] = jnp.zeros_like(dwc_ref)
            dga_ref[...] = jnp.zeros_like(dga_ref)
            dgco_ref[...] = jnp.zeros_like(dgco_ref)

        dmix = _dot_nt(dx2b_ref[...], wo_ref[...])
        dnc, dna = dmix[:, :CONV_CH], dmix[:, CONV_CH:]

        ov = o_ref[...]
        ra = _rms(ov)
        dga_ref[...] += jnp.sum(dna * (ov * ra), axis=0, keepdims=True)
        do = _rms_bwd(ov, ra, dna * ga_ref[...])
        dob = do.astype(BF)
        dot_ref[0] = do.T.astype(BF)
        sel = (_rows((ATTN_W, LANES)) // HEAD_DIM == _cols((ATTN_W, LANES))).astype(F32)
        delta = jnp.dot(do * ov, sel, precision=HI, preferred_element_type=F32)
        featc = _dot(_split_parts(fcol_ref[...] - lcol_ref[...], 1.0), sc_ref[...]).astype(BF)
        featd = _dot(_split_parts(-delta, 0.0), sc_ref[...]).astype(BF)
        lane = _cols((tm, LANES))
        for h in range(N_HEADS):
            hp, hh = divmod(h, 2)
            own = (lane >= HEAD_DIM * hh) & (lane < HEAD_DIM * (hh + 1))
            hs = slice(LANES * h, LANES * (h + 1))
            qab_ref[:, hs] = jnp.where(own, qa_ref[:, hs], featc[:, hs])
            doa_ref[:, hs] = jnp.where(own, dob[:, LANES * hp:LANES * (hp + 1)], featd[:, hs])

        zc_v = zc_ref[...]
        gb, gc, xc = zc_v[:, :CONV_CH], zc_v[:, CONV_CH:2 * CONV_CH], zc_v[:, 2 * CONV_CH:]
        hal = halo_ref[...] * (rb > 0).astype(F32)
        u = gc * xc
        prev = hal[:, CONV_CH:2 * CONV_CH] * hal[:, 2 * CONV_CH:]
        wc = wc_ref[...]
        cv, u1, u2 = _conv3(u, prev, wc)
        y = gb * cv
        rc = _rms(y)
        dgco_ref[...] += jnp.sum(dnc * (y * rc), axis=0, keepdims=True)
        dy = _rms_bwd(y, rc, dnc * gco_ref[...])
        dcv = dy * gb
        dwc_ref[0:1, :] += jnp.sum(dcv * u2, axis=0, keepdims=True)
        dwc_ref[1:2, :] += jnp.sum(dcv * u1, axis=0, keepdims=True)
        dwc_ref[2:3, :] += jnp.sum(dcv * u, axis=0, keepdims=True)
        du = _conv3_bwd(dcv, carry[...], wc)
        carry[...] = dcv[:8, :]
        dzc_ref[:, :CONV_CH] = (dy * cv).astype(BF)
        dzc_ref[:, CONV_CH:2 * CONV_CH] = (du * xc).astype(BF)
        dzc_ref[:, 2 * CONV_CH:] = (du * gc).astype(BF)

    blk = lambda c: pl.BlockSpec((tm, c), lambda i: (nb - 1 - i, 0))
    halo = pl.BlockSpec((8, 3 * CONV_CH), lambda i: (jnp.maximum((nb - 1 - i) * (tm // 8) - 1, 0), 0))
    tr = pl.BlockSpec((1, ATTN_W, tm), lambda i: (nb - 1 - i, 0, 0))
    return pl.pallas_call(
        body, name="bwd_out", grid=(nb,),
        in_specs=[blk(D_MODEL), blk(ATTN_W), blk(3 * CONV_CH), halo, blk(LANES), blk(LANES), blk(aw),
                  _full(w_o.shape), _full((1, ATTN_W)), _full((1, CONV_CH)), _full((8, CONV_CH)), _full((LANES, aw))],
        out_specs=[blk(3 * CONV_CH), blk(aw), tr, blk(aw),
                   _full((8, CONV_CH)), _full((1, ATTN_W)), _full((1, CONV_CH))],
        out_shape=[jax.ShapeDtypeStruct((s, 3 * CONV_CH), BF),
                   jax.ShapeDtypeStruct((s, aw), BF), jax.ShapeDtypeStruct((nb, ATTN_W, tm), BF),
                   jax.ShapeDtypeStruct((s, aw), BF),
                   jax.ShapeDtypeStruct((8, CONV_CH), F32), jax.ShapeDtypeStruct((1, ATTN_W), F32),
                   jax.ShapeDtypeStruct((1, CONV_CH), F32)],
        scratch_shapes=[pltpu.VMEM((8, CONV_CH), F32)],
        compiler_params=_params(VMEM_MID),
    )(dx2b, o, zc, zc, fcol, lcol, qa, w_o, g_attn_out, g_conv_out, w_conv, _bias_scatter()[:, :aw])


def _attn_bwd(qab, doa, ka, qkv, qt, dot, t, blocks):
    s = qab.shape[0]
    nb = s // t
    npair = N_HEADS // 2
    n = len(blocks)

    def body(ka_ref, v_ref, qab_ref, doa_ref, qt_ref, dot_ref, *refs):
        block_refs, (dk_ref, dv_ref, dfk_ref, dqa_ref, dq_ref) = refs[:n], refs[n:n + 5]
        received = refs[n + 5:2 * n + 5]
        acck, accv = refs[2 * n + 5:2 * n + 7]
        hp = pl.program_id(0)
        j = pl.program_id(1)
        start, finish = _scatter_phases(block_refs, received, *refs[2 * n + 7:])
        pl.when((hp == 0) & (j == 0))(start)

        @pl.when(j == 0)
        def _():
            dqa_ref[...] = jnp.zeros_like(dqa_ref)

        causal = _cols((t, t)) <= _rows((t, t))
        row = _rows((LANES, t))
        lane = _cols((t, LANES))
        acck[...] = jnp.zeros(acck.shape, F32)
        accv[...] = jnp.zeros(accv.shape, F32)
        v2 = v_ref[...]
        heads = []
        for hh in range(2):
            aux = HEAD_DIM * (1 - hh)
            ownl = (lane >= HEAD_DIM * hh) & (lane < HEAD_DIM * (hh + 1))
            ones3 = jnp.where((lane >= aux) & (lane < aux + 3), 1.0, 0.0).astype(BF)
            heads.append(dict(hh=hh, hs=slice(hh * LANES, (hh + 1) * LANES), aux=aux,
                              own=(row >= HEAD_DIM * hh) & (row < HEAD_DIM * (hh + 1)),
                              fill=jnp.where(row == aux, 1.0, 0.0).astype(BF),
                              kh=ka_ref[:, hh * LANES:(hh + 1) * LANES], vh=jnp.where(ownl, v2, ones3)))

        def tile(qb, masked):
            rs = pl.ds(pl.multiple_of(qb * t, t), t)
            qt2 = qt_ref[qb]
            dot2 = dot_ref[qb]
            es = [_dot_nt(qab_ref[rs, hd["hs"]], hd["kh"]) for hd in heads]
            dps = [_dot_nt(doa_ref[rs, hd["hs"]], hd["vh"]) for hd in heads]
            aks = [acck[hd["hh"]] for hd in heads]
            avs = [accv[hd["hh"]] for hd in heads]
            dqs = [dqa_ref[rs, hd["hs"]] for hd in heads]
            outs = []
            for hd, e, dp, ak, av, dq in zip(heads, es, dps, aks, avs, dqs):
                if masked:
                    e = jnp.where(causal, e, NEG)
                p = jnp.exp(e)
                ds = (p * dp).astype(BF)
                outs.append((ak + _dot(jnp.where(hd["own"], qt2, hd["fill"]), ds), av + _dot(dot2, p.astype(BF)),
                             dq + _dot(ds, hd["kh"])))
            for hd, (ak, av, dq) in zip(heads, outs):
                acck[hd["hh"]] = ak
                accv[hd["hh"]] = av
                dqa_ref[rs, hd["hs"]] = dq

        def loop_body(qb, carry):
            tile(qb, False)
            return carry

        tile(j, True)
        lax.fori_loop(j + 1, nb, loop_body, 0)
        dk_pair = dv_pair = None
        dfrows = jnp.zeros((LANES, t), F32)
        for hd in heads:
            ak, av = acck[hd["hh"]], accv[hd["hh"]]
            dfrows = jnp.where(row == 2 * hp + hd["hh"], -ak[hd["aux"]:hd["aux"] + 1, :], dfrows)
            dk_pair = ak if hd["hh"] == 0 else jnp.where(row < HEAD_DIM, dk_pair, ak)
            dv_pair = av if hd["hh"] == 0 else jnp.where(row < HEAD_DIM, dv_pair, av)
        dk_ref[...] = dk_pair.T.astype(BF)
        dv_ref[...] = dv_pair.T.astype(BF)
        dfk_ref[0] = dfrows.T

        @pl.when(j == nb - 1)
        def _():
            for r in range(nb):
                rows = slice(r * t, (r + 1) * t)
                dq_ref[rows, :] = (jnp.where(lane < HEAD_DIM, dqa_ref[rows, :LANES], dqa_ref[rows, LANES:])
                                   * 0.125).astype(BF)

        pl.when((hp == npair - 1) & (j == nb - 1))(finish)

    pair_cols = pl.BlockSpec((s, 2 * LANES), lambda hp, j: (0, hp))
    pair_rows = pl.BlockSpec((nb, LANES, t), lambda hp, j: (0, hp, 0))
    any_ = pl.BlockSpec(memory_space=pl.ANY)
    return pl.pallas_call(
        body, name="attn_bwd", grid=(npair, nb),
        in_specs=[pl.BlockSpec((t, 2 * LANES), lambda hp, j: (j, hp)),
                  pl.BlockSpec((t, LANES), lambda hp, j: (j, 2 * npair + hp)),
                  pair_cols, pair_cols, pair_rows, pair_rows] + [any_] * n,
        out_specs=[pl.BlockSpec((t, LANES), lambda hp, j: (j, hp)), pl.BlockSpec((t, LANES), lambda hp, j: (j, hp)),
                   pl.BlockSpec((1, t, LANES), lambda hp, j: (hp, j, 0)), pair_cols,
                   pl.BlockSpec((s, LANES), lambda hp, j: (0, hp))] + [any_] * n,
        out_shape=[jax.ShapeDtypeStruct((s, ATTN_W), BF), jax.ShapeDtypeStruct((s, ATTN_W), BF),
                   jax.ShapeDtypeStruct((npair, s, LANES), F32), jax.ShapeDtypeStruct((s, N_HEADS * LANES), F32),
                   jax.ShapeDtypeStruct((s, ATTN_W), BF)]
        + [jax.ShapeDtypeStruct(b.shape, b.dtype) for b in blocks],
        scratch_shapes=[pltpu.VMEM((2, LANES, t), F32), pltpu.VMEM((2, LANES, t), F32)] + _exchange_sems(n),
        compiler_params=_params(VMEM_BIG),
    )(ka, qkv, qab, doa, qt, dot, *blocks)


def _bwd_in(x, dx2, dzc, dq, dqa, dk, dv, dfk, fpre, w_zc, w_qkv, w_f, g_mix, tm, blocks):
    s = x.shape[0]
    nb = s // tm
    n = len(blocks)

    def body(x_ref, dx2_ref, dzc_ref, dq_ref, dqa_ref, dk_ref, dv_ref, dfk_ref, fpre_ref, wzc_ref, wqkv_ref, wf_ref,
             gm_ref, *refs):
        block_refs, (gx_ref, dfb_ref, dgm_ref, dbf_ref), received = refs[:n], refs[n:n + 4], refs[n + 4:2 * n + 4]
        carry = refs[2 * n + 4]
        i = pl.program_id(0)
        start, finish = _scatter_phases(block_refs, received, *refs[2 * n + 5:])
        pl.when(i == 0)(start)

        @pl.when(i == 0)
        def _():
            carry[...] = jnp.zeros_like(carry)
            dgm_ref[...] = jnp.zeros_like(dgm_ref)
            dbf_ref[...] = jnp.zeros_like(dbf_ref)

        lane = _cols((tm, LANES))
        dfq = jnp.zeros((tm, LANES), F32)
        for h in range(N_HEADS):
            aux = LANES * h + HEAD_DIM * (1 - h % 2)
            dfq = jnp.where(lane == h, dqa_ref[:, aux:aux + 1], dfq)

        triu = (_rows((tm, tm)) <= _cols((tm, tm))).astype(F32)
        df_cum = dfq + ((dfk_ref[0] + dfk_ref[1]) + (dfk_ref[2] + dfk_ref[3]))
        dlogf = jnp.dot(triu, df_cum, precision=HI, preferred_element_type=F32) + carry[...]
        carry[...] = dlogf[0:1, :]
        fpre = fpre_ref[...]
        df = jnp.where(_cols(fpre.shape) < N_HEADS, dlogf / (1.0 + jnp.exp(fpre)), 0.0)
        dbf_ref[...] += jnp.sum(df, axis=0, keepdims=True)
        dfb = df.astype(BF)
        dfb_ref[...] = dfb

        dh1 = _dot_nt(dzc_ref[...], wzc_ref[...])
        dh1 += _dot_nt(dq_ref[...], wqkv_ref[:, :ATTN_W])
        dh1 += _dot_nt(dk_ref[...], wqkv_ref[:, ATTN_W:2 * ATTN_W])
        dh1 += _dot_nt(dv_ref[...], wqkv_ref[:, 2 * ATTN_W:])
        dh1 += _dot_nt(dfb, wf_ref[...])
        xv = x_ref[...]
        r1 = _rms(xv)
        dgm_ref[...] += jnp.sum(dh1 * (xv * r1), axis=0, keepdims=True)
        gx_ref[...] = dx2_ref[...] + _rms_bwd(xv, r1, dh1 * gm_ref[...])
        pl.when(i == nb - 1)(finish)

    blk = lambda c: pl.BlockSpec((tm, c), lambda i: (nb - 1 - i, 0))
    any_ = pl.BlockSpec(memory_space=pl.ANY)
    return pl.pallas_call(
        body, name="bwd_in", grid=(nb,),
        in_specs=[blk(D_MODEL), blk(D_MODEL), blk(3 * CONV_CH), blk(ATTN_W), blk(N_HEADS * LANES), blk(ATTN_W),
                  blk(ATTN_W), pl.BlockSpec((N_HEADS // 2, tm, LANES), lambda i: (0, nb - 1 - i, 0)), blk(LANES),
                  _full(w_zc.shape), _full(w_qkv.shape), _full(w_f.shape), _full((1, D_MODEL))] + [any_] * n,
        out_specs=[blk(D_MODEL), blk(LANES), _full((1, D_MODEL)), _full((1, LANES))] + [any_] * n,
        out_shape=[jax.ShapeDtypeStruct((s, D_MODEL), F32), jax.ShapeDtypeStruct((s, LANES), BF),
                   jax.ShapeDtypeStruct((1, D_MODEL), F32), jax.ShapeDtypeStruct((1, LANES), F32)]
        + [jax.ShapeDtypeStruct(b.shape, b.dtype) for b in blocks],
        scratch_shapes=[pltpu.VMEM((1, LANES), F32)] + _exchange_sems(n),
        compiler_params=_params(VMEM_MID),
    )(x, dx2, dzc, dq, dqa, dk, dv, dfk, fpre, w_zc, w_qkv, w_f, g_mix, *blocks)


def _matmul_tn(a, b, bn, bk, name):
    s, m = a.shape
    n = b.shape[1]

    def body(a_ref, b_ref, o_ref):
        @pl.when(pl.program_id(1) == 0)
        def _():
            o_ref[...] = jnp.zeros_like(o_ref)

        o_ref[...] += _dot_tn(a_ref[...], b_ref[...])

    return pl.pallas_call(
        body, name=name, grid=(n // bn, s // bk),
        in_specs=[pl.BlockSpec((bk, m), lambda jn, k: (k, 0)), pl.BlockSpec((bk, bn), lambda jn, k: (k, jn))],
        out_specs=pl.BlockSpec((m, bn), lambda jn, k: (0, jn)),
        out_shape=jax.ShapeDtypeStruct((m, n), F32),
        compiler_params=_params(VMEM_MID),
    )(a, b)


def _flip(v, bit):
    return 1 - v if bit else v


def _all_gather(shards):
    n = len(shards)

    def body(*refs):
        start, forward, finish = _gather_phases(refs[:n], refs[n:2 * n], *refs[2 * n:])
        start()
        forward()
        finish()

    any_ = pl.BlockSpec(memory_space=pl.ANY)
    return pl.pallas_call(
        body, name="all_gather_weights",
        in_specs=[any_] * n, out_specs=[any_] * n,
        out_shape=_gathered_shapes(shards), scratch_shapes=_exchange_sems(n),
    )(*shards)


def _gathered_shapes(shards):
    return [jax.ShapeDtypeStruct((8,) + sh.shape, sh.dtype) for sh in shards]


def _exchange_sems(n):
    return [pltpu.SemaphoreType.DMA((7 * n,)), pltpu.SemaphoreType.DMA((7 * n,)), pltpu.SemaphoreType.DMA((n,))]


def _gather_phases(src, out, send_sems, recv_sems, loc_sems):
    n = len(src)
    x, y, c = lax.axis_index("x"), lax.axis_index("y"), lax.axis_index("c")
    me, sibling = (x, y, c), (x, y, 1 - c)
    chips = [(1 - x, y), (x, 1 - y), (1 - x, 1 - y)]

    def slot(a, px, py, pc):
        return out[a].at[4 * px + 2 * py + pc]

    def copy(a, k, block, to, from_src=False):
        return pltpu.make_async_remote_copy(
            src_ref=src[a] if from_src else slot(a, *block), dst_ref=slot(a, *block),
            send_sem=send_sems.at[7 * a + k], recv_sem=recv_sems.at[7 * a + k],
            device_id=to, device_id_type=MESH)

    def local(a):
        return pltpu.make_async_copy(src[a], slot(a, *me), loc_sems.at[a])

    def first(a):
        return [copy(a, 0, me, sibling, True)] + [copy(a, 1 + j, me, (*chip, c), True)
                                                  for j, chip in enumerate(chips)]

    def passed(a, j):
        return copy(a, 4 + j, (*chips[j], c), sibling)

    def start():
        for a in range(n):
            local(a).start()
            for cp in first(a):
                cp.start()

    def forward():
        for a in range(n):
            for j, chip in enumerate(chips):
                copy(a, 1 + j, (*chip, c), me).wait_recv()
                passed(a, j).start()

    def finish():
        for a in range(n):
            copy(a, 0, sibling, me).wait_recv()
            for j, chip in enumerate(chips):
                copy(a, 4 + j, (*chip, 1 - c), me).wait_recv()
        for a in range(n):
            for cp in first(a) + [passed(a, j) for j in range(3)]:
                cp.wait_send()
            local(a).wait()

    return start, forward, finish


def _scatter_phases(src, out, send_sems, recv_sems, loc_sems):
    n = len(src)
    masks = [((k >> 2) & 1, (k >> 1) & 1, k & 1) for k in range(1, 8)]
    x, y, c = lax.axis_index("x"), lax.axis_index("y"), lax.axis_index("c")
    me = 4 * x + 2 * y + c

    def copies():
        cps = []
        for a in range(n):
            cps.append(pltpu.make_async_copy(src[a].at[me], out[a].at[me], loc_sems.at[a]))
            for k, (mx, my, mc) in enumerate(masks):
                px, py, pc = _flip(x, mx), _flip(y, my), _flip(c, mc)
                cps.append(pltpu.make_async_remote_copy(
                    src_ref=src[a].at[4 * px + 2 * py + pc], dst_ref=out[a].at[me],
                    send_sem=send_sems.at[7 * a + k], recv_sem=recv_sems.at[7 * a + k],
                    device_id=(px, py, pc), device_id_type=MESH))
        return cps

    def start():
        for cp in copies():
            cp.start()

    def finish():
        for cp in copies():
            cp.wait()

    return start, finish


def _all_to_all(blocks, name):
    n = len(blocks)

    def body(*refs):
        start, finish = _scatter_phases(refs[:n], refs[n:2 * n], *refs[2 * n:])
        start()
        finish()

    any_ = pl.BlockSpec(memory_space=pl.ANY)
    return pl.pallas_call(
        body, name=name,
        in_specs=[any_] * n, out_specs=[any_] * n,
        out_shape=[jax.ShapeDtypeStruct(b.shape, b.dtype) for b in blocks], scratch_shapes=_exchange_sems(n),
    )(*blocks)


def _adamw(parts, w, m, v, br, name):
    g8, r, c = parts.shape
    c1 = 1.0 - ADAM_B1 ** ADAM_STEP
    c2 = 1.0 - ADAM_B2 ** ADAM_STEP

    def body(p_ref, w_ref, m_ref, v_ref, g_ref, d_ref, m2_ref, v2_ref):
        g = p_ref[0]
        for d in range(1, g8):
            g = g + p_ref[d]
        m2 = ADAM_B1 * m_ref[...] + (1.0 - ADAM_B1) * g
        v2 = ADAM_B2 * v_ref[...] + (1.0 - ADAM_B2) * (g * g)
        g_ref[...] = g
        m2_ref[...] = m2
        v2_ref[...] = v2
        d_ref[...] = -ADAM_LR * ((m2 / c1) / (jnp.sqrt(v2 / c2) + ADAM_EPS) + ADAM_WD * w_ref[...])

    blk = pl.BlockSpec((br, c), lambda i: (i, 0))
    out = jax.ShapeDtypeStruct((r, c), F32)
    return pl.pallas_call(
        body, name=name, grid=(r // br,),
        in_specs=[pl.BlockSpec((g8, br, c), lambda i: (0, i, 0)), blk, blk, blk],
        out_specs=[blk] * 4, out_shape=[out] * 4,
        compiler_params=_params(VMEM_MID),
    )(parts, w, m, v)


def _pad_rows(a, rows):
    return jnp.pad(a, ((0, rows - a.shape[0]), (0, 0)))


_SMALL = (("g_mix", 1024), ("b_f", 8), ("g_conv_out", 512), ("g_attn_out", 512), ("g_ffn", 1024), ("g_final", 1024))
_SMALL_LEN = sum(n for _, n in _SMALL)
_SMALL_ROWS = 40


def _pack_small(vals, extra=None):
    parts = [vals[k].reshape(-1) for k, _ in _SMALL]
    if extra is not None:
        parts.append(extra.reshape(-1))
    flat = jnp.concatenate(parts)
    return jnp.pad(flat, (0, _SMALL_ROWS * LANES - flat.shape[0])).reshape(_SMALL_ROWS, LANES)


def _unpack_small(packed, shapes):
    flat = packed.reshape(-1)
    out, off = {}, 0
    for k, n in _SMALL:
        out[k] = flat[off:off + n].reshape(shapes[k])
        off += n
    return out


def _col_blocks(a):
    r, c8 = a.shape
    return jnp.transpose(a.reshape(r, 8, c8 // 8), (1, 0, 2))


def _from_col_blocks(a):
    g, r, c = a.shape
    return jnp.transpose(a, (1, 0, 2)).reshape(r, g * c)


def kernel(x, g_mix, w_in, b_f, w_conv, g_conv_out, g_attn_out, w_o, g_ffn, w_up, w_ffn_conv, w_down, g_final, loss_target, m_g_mix, m_w_in, m_b_f, m_w_conv, m_g_conv_out, m_g_attn_out, m_w_o, m_g_ffn, m_w_up, m_w_ffn_conv, m_w_down, m_g_final, v_g_mix, v_w_in, v_b_f, v_w_conv, v_g_conv_out, v_g_attn_out, v_w_o, v_g_ffn, v_w_up, v_w_ffn_conv, v_w_down, v_g_final):
    w = dict(g_mix=g_mix, w_in=w_in[0], b_f=b_f, w_conv=w_conv[0], g_conv_out=g_conv_out, g_attn_out=g_attn_out,
             w_o=w_o[0], g_ffn=g_ffn, w_up=w_up[0], w_ffn_conv=w_ffn_conv[0], w_down=w_down[0], g_final=g_final)
    m = dict(g_mix=m_g_mix, w_in=m_w_in[0], b_f=m_b_f, w_conv=m_w_conv[0], g_conv_out=m_g_conv_out,
             g_attn_out=m_g_attn_out, w_o=m_w_o[0], g_ffn=m_g_ffn, w_up=m_w_up[0], w_ffn_conv=m_w_ffn_conv[0],
             w_down=m_w_down[0], g_final=m_g_final)
    v = dict(g_mix=v_g_mix, w_in=v_w_in[0], b_f=v_b_f, w_conv=v_w_conv[0], g_conv_out=v_g_conv_out,
             g_attn_out=v_g_attn_out, w_o=v_w_o[0], g_ffn=v_g_ffn, w_up=v_w_up[0], w_ffn_conv=v_w_ffn_conv[0],
             w_down=v_w_down[0], g_final=v_g_final)
    shapes = dict(g_mix=g_mix.shape, w_in=w_in.shape, b_f=b_f.shape, w_conv=w_conv.shape,
                  g_conv_out=g_conv_out.shape, g_attn_out=g_attn_out.shape, w_o=w_o.shape, g_ffn=g_ffn.shape,
                  w_up=w_up.shape, w_ffn_conv=w_ffn_conv.shape, w_down=w_down.shape, g_final=g_final.shape)

    tm = t = 512
    tf = 256
    xs, tgt = x[0], loss_target[0]
    seq = xs.shape[0]
    assert seq % tm == 0 and seq % tf == 0
    bk = 1024 if seq % 1024 == 0 else 512

    g_in, g_conv = _all_gather([w["w_in"].astype(BF), w["w_conv"]])
    w_in_f = _from_col_blocks(g_in)
    w_zc = w_in_f[:, :3 * CONV_CH]
    w_qkv = w_in_f[:, 3 * CONV_CH:3 * CONV_CH + 3 * ATTN_W]
    w_f = jnp.pad(w_in_f[:, 3 * CONV_CH + 3 * ATTN_W:], ((0, 0), (0, LANES - N_HEADS)))
    b_f_p = jnp.pad(b_f, ((0, 0), (0, LANES - N_HEADS)))
    w_conv_p = _pad_rows(_from_col_blocks(g_conv), 8)

    h1, zc, qkv, fpre, fcol, nc, qa, ka, qt, vt = _fwd_in(
        xs, g_mix, w_zc, w_qkv, w_f, b_f_p, w_conv_p, g_conv_out, tm)
    o, lcol, g_o, g_up, g_dn, g_fc = _attn_fwd(
        qa, ka, vt, t, [w["w_o"].astype(BF), w["w_up"].astype(BF), w["w_down"].astype(BF), w["w_ffn_conv"]])
    w_o_f = g_o.reshape(D_MODEL, D_MODEL)
    w_up_f = _from_col_blocks(g_up)
    w_dn_f = g_dn.reshape(D_FF, D_MODEL)
    w_fc_p = _pad_rows(_from_col_blocks(g_fc), 8)
    x2, h2, mix = _fwd_out(xs, nc, o, g_attn_out, w_o_f, g_ffn, tm)
    u0, uc, act, dx3, dx3b, loss, dg_final = _fwd_ffn(
        h2, x2, tgt, w_up_f, w_fc_p, w_dn_f, g_final.reshape(1, D_MODEL), tf)
    loss_part = loss

    du0, dx2, dx2b, dw_fc, dg_ffn = _bwd_ffn(dx3, dx3b, u0, uc, x2, w_up_f.T, w_fc_p, w_dn_f.T, g_ffn, tf)
    dw_up = _matmul_tn(h2, du0, 1408, bk, "dw_up")
    dw_dn = _matmul_tn(act, dx3b, 512, bk, "dw_down")
    dw_o = _matmul_tn(mix, dx2b, 512, bk, "dw_o")
    dzc, doa, dot, qab, dw_conv, dg_attn, dg_conv = _bwd_out(
        dx2b, o, zc, fcol, lcol, qa, w_o_f, g_attn_out, g_conv_out, w_conv_p, t)
    dk, dv, dfk, dqa, dq, r_up, r_dn, r_fc, r_o = _attn_bwd(
        qab, doa, ka, qkv, qt, dot, t,
        [_col_blocks(dw_up), dw_dn.reshape(8, D_FF // 8, D_MODEL), _col_blocks(dw_fc[:3]),
         dw_o.reshape(8, D_MODEL // 8, D_MODEL)])
    dw_zc = _matmul_tn(h1, dzc, 3 * CONV_CH, bk, "dw_in_conv")
    dw_q = _matmul_tn(h1, dq, ATTN_W, bk, "dw_in_q")
    dw_k = _matmul_tn(h1, dk, ATTN_W, bk, "dw_in_k")
    dw_v = _matmul_tn(h1, dv, ATTN_W, bk, "dw_in_v")
    dw_main = jnp.concatenate([dw_zc, dw_q, dw_k, dw_v, jnp.zeros((D_MODEL, N_HEADS), F32)], axis=1)
    gx, dfb, dg_mix, db_f, r_in = _bwd_in(xs, dx2, dzc, dq, dqa, dk, dv, dfk, fpre, w_zc, w_qkv, w_f, g_mix, tm,
                                          [_col_blocks(dw_main)])
    dw_f = _matmul_tn(h1, dfb, LANES, bk, "dw_in_f")
    to_last = jnp.zeros((8, D_MODEL, N_HEADS), F32).at[7].set(dw_f[:, :N_HEADS])
    small = dict(g_mix=dg_mix, b_f=db_f[:, :N_HEADS], g_conv_out=dg_conv, g_attn_out=dg_attn, g_ffn=dg_ffn,
                 g_final=dg_final)
    r_f, r_conv, r_small = _all_to_all(
        [to_last, _col_blocks(dw_conv[:3]),
         jnp.broadcast_to(_pack_small(small, loss_part), (8, _SMALL_ROWS, LANES))], "all_to_all_grads")
    r_in = r_in.at[:, :, -N_HEADS:].add(r_f)
    big = ("w_in", "w_o", "w_up", "w_down", "w_conv", "w_ffn_conv")
    recv = [r_in, r_o, r_up, r_dn, r_conv, r_fc, r_small]

    rows = dict(w_in=256, w_o=128, w_up=256, w_down=176, w_conv=3, w_ffn_conv=3)
    g_out, d_out, m_out, v_out = {}, {}, {}, {}
    for a, k in enumerate(big):
        res = _adamw(recv[a], w[k], m[k], v[k], rows[k], "adamw_" + k)
        g_out[k], d_out[k], m_out[k], v_out[k] = [r.reshape(shapes[k]) for r in res]
    res = _adamw(recv[6], _pack_small(w), _pack_small(m), _pack_small(v), _SMALL_ROWS, "adamw_gains")
    for dst, packed in zip((g_out, d_out, m_out, v_out), res):
        dst.update(_unpack_small(packed, shapes))
    loss = res[0].reshape(-1)[_SMALL_LEN]

    order = ("g_mix", "w_in", "b_f", "w_conv", "g_conv_out", "g_attn_out", "w_o", "g_ffn", "w_up", "w_ffn_conv",
             "w_down", "g_final")
    return (loss, gx.reshape(x.shape), *[g_out[k] for k in order], *[d_out[k] for k in order],
            *[m_out[k] for k in order], *[v_out[k] for k in order])
```

```python
import jax
import jax.numpy as jnp
import numpy as np
from jax import lax
from jax.experimental import pallas as pl
from jax.experimental.pallas import tpu as pltpu

F32 = jnp.float32
BF = jnp.bfloat16
HI = lax.Precision.HIGHEST
MESH = pl.DeviceIdType.MESH

D_MODEL = 1024
CONV_CH = 512
ATTN_W = 512
N_HEADS = 8
HEAD_DIM = 64
D_FF = 2816
FF_CHUNK = 256
N_FF_CHUNKS = D_FF // FF_CHUNK
EPS = 1e-6
NEG = -1e30
LANES = 128
VMEM_BIG = 56 * 1024 * 1024
VMEM_MID = 40 * 1024 * 1024

ADAM_LR = 0.001
ADAM_B1 = 0.9
ADAM_B2 = 0.999
ADAM_EPS = 1e-08
ADAM_WD = 0.01
ADAM_STEP = 10

NT = (((1,), (1,)), ((), ()))
TN = (((0,), (0,)), ((), ()))


def _dot(a, b):
    return jnp.dot(a, b, preferred_element_type=F32)


def _dot_nt(a, b):
    return lax.dot_general(a, b, NT, preferred_element_type=F32)


def _dot_tn(a, b):
    return lax.dot_general(a, b, TN, preferred_element_type=F32)


def _params(vmem):
    return pltpu.CompilerParams(vmem_limit_bytes=vmem)


def _rows(shape):
    return lax.broadcasted_iota(jnp.int32, shape, 0)


def _cols(shape):
    return lax.broadcasted_iota(jnp.int32, shape, 1)


def _shift_down(u, prev, k):
    n = prev.shape[0]
    out = pltpu.roll(u, k, 0)
    row = _rows(u.shape)
    for r in range(k):
        out = jnp.where(row == r, prev[n - k + r:n - k + r + 1, :].astype(u.dtype), out)
    return out


def _shift_up(u, nxt, k):
    tm = u.shape[0]
    out = pltpu.roll(u, tm - k, 0)
    row = _rows(u.shape)
    for r in range(k):
        out = jnp.where(row == tm - k + r, nxt[r:r + 1, :], out)
    return out


def _conv3(u, prev, w):
    u1 = _shift_down(u, prev, 1)
    u2 = _shift_down(u, prev, 2)
    return w[0:1, :] * u2 + w[1:2, :] * u1 + w[2:3, :] * u, u1, u2


def _conv3_bwd(d, nxt, w):
    return w[2:3, :] * d + w[1:2, :] * _shift_up(d, nxt, 1) + w[0:1, :] * _shift_up(d, nxt, 2)


def _rms(x):
    return lax.rsqrt(jnp.mean(x * x, axis=-1, keepdims=True) + EPS)


def _rms_bwd(x, r, dyg):
    return r * dyg - x * (r * r * r) * jnp.mean(dyg * x, axis=-1, keepdims=True)


def _full(shape):
    nd = len(shape)
    return pl.BlockSpec(shape, lambda i, _n=nd: (0,) * _n)


ONES_LANE = 24


def _bias_scatter():
    sc = np.zeros((LANES, 2 * N_HEADS * LANES), np.float32)
    koff = N_HEADS * LANES
    for h in range(N_HEADS):
        aux = HEAD_DIM * (1 - h % 2)
        for j in range(3):
            sc[8 * j + h, LANES * h + aux + j] = 1.0
            sc[ONES_LANE, koff + LANES * h + aux + j] = 1.0
            sc[ONES_LANE, LANES * h + aux + 3 + j] = 1.0
            sc[8 * j + h, koff + LANES * h + aux + 3 + j] = -1.0
    return jnp.asarray(sc, BF)


def _split_parts(v, one):
    hi = v.astype(BF).astype(F32)
    rest = v - hi
    mid = rest.astype(BF).astype(F32)
    lo = (rest - mid).astype(BF).astype(F32)
    parts = hi + pltpu.roll(mid, 8, 1) + pltpu.roll(lo, 16, 1)
    return jnp.where(_cols(parts.shape) == ONES_LANE, one, parts).astype(BF)


def _fwd_in(x, g_mix, w_zc, w_qkv, w_f, b_f, w_conv, g_conv_out, tm):
    s = x.shape[0]
    nb = s // tm
    aw = N_HEADS * LANES

    def body(x_ref, gm_ref, wzc_ref, wqkv_ref, wf_ref, bf_ref, wc_ref, gco_ref, sc_ref,
             h1_ref, zc_ref, qkv_ref, fpre_ref, fcol_ref, nc_ref, qa_ref, ka_ref, qt_ref, vt_ref, cu_ref, cf_ref):
        i = pl.program_id(0)

        @pl.when(i == 0)
        def _():
            cu_ref[...] = jnp.zeros_like(cu_ref)
            cf_ref[...] = jnp.zeros_like(cf_ref)

        xv = x_ref[...]
        hb = (xv * _rms(xv) * gm_ref[...]).astype(BF)
        h1_ref[...] = hb
        zc = _dot(hb, wzc_ref[...])
        zc_ref[...] = zc
        qkv = _dot(hb, wqkv_ref[...])
        qkv = jnp.where(_cols(qkv.shape) < ATTN_W, qkv * 0.125, qkv)
        qkvb = qkv.astype(BF)
        qkv_ref[...] = qkvb
        qt_ref[0] = qkv[:, :ATTN_W].T.astype(BF)
        vt_ref[0] = qkv[:, 2 * ATTN_W:].T.astype(BF)

        gb, gc, xc = zc[:, :CONV_CH], zc[:, CONV_CH:2 * CONV_CH], zc[:, 2 * CONV_CH:]
        u = gc * xc
        cv, _, _ = _conv3(u, cu_ref[...], wc_ref[...])
        cu_ref[...] = u[tm - 8:, :]
        y = gb * cv
        nc_ref[...] = (y * _rms(y) * gco_ref[...]).astype(BF)

        fpre = _dot(hb, wf_ref[...]) + bf_ref[...]
        fpre_ref[...] = fpre
        logf = jnp.minimum(fpre, 0.0) - jnp.log1p(jnp.exp(-jnp.abs(fpre)))
        logf = jnp.where(_cols(logf.shape) < N_HEADS, logf, 0.0)
        tri = (_rows((tm, tm)) >= _cols((tm, tm))).astype(F32)
        fcol = jnp.dot(tri, logf, precision=HI, preferred_element_type=F32) + cf_ref[...]
        cf_ref[...] = fcol[tm - 1:tm, :]
        fcol_ref[...] = fcol

        feat = _dot(_split_parts(fcol, 1.0), sc_ref[...]).astype(BF)
        lane = _cols((tm, LANES))
        for h in range(N_HEADS):
            hp, hh = divmod(h, 2)
            own = (lane >= HEAD_DIM * hh) & (lane < HEAD_DIM * (hh + 1))
            hs = slice(LANES * h, LANES * (h + 1))
            qa_ref[:, hs] = jnp.where(own, qkvb[:, LANES * hp:LANES * (hp + 1)], feat[:, hs])
            ka_ref[:, hs] = jnp.where(own, qkvb[:, ATTN_W + LANES * hp:ATTN_W + LANES * (hp + 1)],
                                      feat[:, aw + LANES * h:aw + LANES * (h + 1)])

    blk = lambda c: pl.BlockSpec((tm, c), lambda i: (i, 0))
    return pl.pallas_call(
        body, name="fwd_in", grid=(nb,),
        in_specs=[blk(D_MODEL), _full((1, D_MODEL)), _full(w_zc.shape), _full(w_qkv.shape), _full(w_f.shape),
                  _full((1, LANES)), _full((8, CONV_CH)), _full((1, CONV_CH)), _full((LANES, 2 * aw))],
        out_specs=[blk(D_MODEL), blk(3 * CONV_CH), blk(3 * ATTN_W), blk(LANES), blk(LANES),
                   blk(CONV_CH), blk(aw), blk(aw)] + [pl.BlockSpec((1, ATTN_W, tm), lambda i: (i, 0, 0))] * 2,
        out_shape=[jax.ShapeDtypeStruct((s, D_MODEL), BF), jax.ShapeDtypeStruct((s, 3 * CONV_CH), F32),
                   jax.ShapeDtypeStruct((s, 3 * ATTN_W), BF), jax.ShapeDtypeStruct((s, LANES), F32),
                   jax.ShapeDtypeStruct((s, LANES), F32),
                   jax.ShapeDtypeStruct((s, CONV_CH), BF), jax.ShapeDtypeStruct((s, aw), BF),
                   jax.ShapeDtypeStruct((s, aw), BF)] + [jax.ShapeDtypeStruct((nb, ATTN_W, tm), BF)] * 2,
        scratch_shapes=[pltpu.VMEM((8, CONV_CH), F32), pltpu.VMEM((1, LANES), F32)],
        compiler_params=_params(VMEM_MID),
    )(x, g_mix, w_zc, w_qkv, w_f, b_f, w_conv, g_conv_out, _bias_scatter())


def _pipeline_masked_last(last, produce, consume, buf_a, buf_b):
    produce(0, buf_a)

    def two_blocks(j, carry):
        blk = 2 * j
        produce(blk + 1, buf_b)
        consume(blk, buf_a, False)
        produce(blk + 2, buf_a)
        consume(blk + 1, buf_b, False)
        return carry

    lax.fori_loop(0, last // 2, two_blocks, 0)

    @pl.when(last % 2 == 0)
    def _():
        consume(last, buf_a, True)

    @pl.when(last % 2 == 1)
    def _():
        produce(last, buf_b)
        consume(last - 1, buf_a, False)
        consume(last, buf_b, True)


def _attn_fwd(qa, ka, vt, t, shards):
    s = qa.shape[0]
    nb = s // t
    n = len(shards)

    def body(qa_ref, ka_any, vt_any, *refs):
        shard_refs, (o_ref, l_ref), gathered = refs[:n], refs[n:n + 2], refs[n + 2:2 * n + 2]
        ka_scr, vt_scr, m_scr, acc_scr, ea_scr, eb_scr, sem = refs[2 * n + 2:2 * n + 9]
        i = pl.program_id(0)
        start, forward, finish = _gather_phases(shard_refs, gathered, *refs[2 * n + 9:])
        pl.when(i == 0)(start)
        pl.when(i == nb // 2)(forward)
        _load_weights(i, [(ka_any, ka_scr), (vt_any, vt_scr)], sem)

        causal_t = _rows((t, t)) <= _cols((t, t))
        row = _rows((LANES, t))
        lrows = jnp.zeros((LANES, t), F32)
        for hp in range(N_HEADS // 2):
            ps = slice(hp * LANES, (hp + 1) * LANES)
            m_scr[...] = jnp.full(m_scr.shape, NEG, F32)
            acc_scr[...] = jnp.zeros(acc_scr.shape, F32)
            heads = [dict(hh=hh, hs=slice((2 * hp + hh) * LANES, (2 * hp + hh + 1) * LANES),
                          aux=HEAD_DIM * (1 - hh), own=(row >= HEAD_DIM * hh) & (row < HEAD_DIM * (hh + 1)),
                          fill=jnp.where(row == HEAD_DIM * (1 - hh), 1.0, 0.0).astype(BF),
                          qh=qa_ref[:, (2 * hp + hh) * LANES:(2 * hp + hh + 1) * LANES]) for hh in range(2)]

            def scores(kb, dst, heads=heads):
                rs = pl.ds(pl.multiple_of(kb * t, t), t)
                for hd in heads:
                    dst[hd["hh"]] = _dot_nt(ka_scr[rs, hd["hs"]], hd["qh"])

            def consume(kb, src, masked, ps=ps, heads=heads):
                vt2 = vt_scr[kb, ps, :]
                m_olds = [m_scr[hd["hh"]:hd["hh"] + 1, :] for hd in heads]
                accs = [acc_scr[hd["hh"]] for hd in heads]
                m_news, acc_news = [], []
                for hd, m_old, acc in zip(heads, m_olds, accs):
                    e = src[hd["hh"]]
                    if masked:
                        e = jnp.where(causal_t, e, NEG)
                    m_new = jnp.maximum(m_old, jnp.max(e, axis=0, keepdims=True))
                    p = jnp.exp(e - m_new).astype(BF)
                    vta = jnp.where(hd["own"], vt2, hd["fill"])
                    acc_news.append(jnp.exp(m_old - m_new) * acc + _dot(vta, p))
                    m_news.append(m_new)
                for hd, m_new, acc in zip(heads, m_news, acc_news):
                    acc_scr[hd["hh"]] = acc
                    m_scr[hd["hh"]:hd["hh"] + 1, :] = m_new

            _pipeline_masked_last(i, scores, consume, ea_scr, eb_scr)

            o_pair = None
            for hd in heads:
                hh = hd["hh"]
                acc = acc_scr[hh]
                denom = acc[hd["aux"]:hd["aux"] + 1, :]
                o_h = acc / denom
                lrows = jnp.where(row == 2 * hp + hh, m_scr[hh:hh + 1, :] + jnp.log(denom), lrows)
                o_pair = o_h if hh == 0 else jnp.where(row < HEAD_DIM, o_pair, o_h)
            o_ref[:, ps] = o_pair.T
        l_ref[...] = lrows.T
        pl.when(i == nb - 1)(finish)

    any_ = pl.BlockSpec(memory_space=pl.ANY)
    return pl.pallas_call(
        body, name="attn_fwd", grid=(nb,),
        in_specs=[pl.BlockSpec((t, N_HEADS * LANES), lambda i: (i, 0)), any_, any_] + [any_] * n,
        out_specs=[pl.BlockSpec((t, ATTN_W), lambda i: (i, 0)), pl.BlockSpec((t, LANES), lambda i: (i, 0))]
        + [any_] * n,
        out_shape=[jax.ShapeDtypeStruct((s, ATTN_W), F32), jax.ShapeDtypeStruct((s, LANES), F32)]
        + _gathered_shapes(shards),
        scratch_shapes=[pltpu.VMEM(ka.shape, BF), pltpu.VMEM(vt.shape, BF), pltpu.VMEM((2, t), F32),
                        pltpu.VMEM((2, LANES, t), F32), pltpu.VMEM((2, t, t), F32), pltpu.VMEM((2, t, t), F32),
                        pltpu.SemaphoreType.DMA((2,))] + _exchange_sems(n),
        compiler_params=_params(VMEM_BIG),
    )(qa, ka, vt, *shards)


def _fwd_out(x, nc, o, g_attn_out, w_o, g_ffn, tm):
    s = x.shape[0]

    def body(x_ref, nc_ref, o_ref, ga_ref, wo_ref, gf_ref, x2_ref, h2_ref, mix_ref):
        ov = o_ref[...]
        na = (ov * _rms(ov) * ga_ref[...]).astype(BF)
        ncv = nc_ref[...]
        mix_ref[:, :CONV_CH] = ncv
        mix_ref[:, CONV_CH:] = na
        x2 = x_ref[...] + _dot(ncv, wo_ref[:CONV_CH, :]) + _dot(na, wo_ref[CONV_CH:, :])
        x2_ref[...] = x2
        h2_ref[...] = (x2 * _rms(x2) * gf_ref[...]).astype(BF)

    blk = lambda c: pl.BlockSpec((tm, c), lambda i: (i, 0))
    return pl.pallas_call(
        body, name="fwd_out", grid=(s // tm,),
        in_specs=[blk(D_MODEL), blk(CONV_CH), blk(ATTN_W), _full((1, ATTN_W)), _full(w_o.shape), _full((1, D_MODEL))],
        out_specs=[blk(D_MODEL), blk(D_MODEL), blk(D_MODEL)],
        out_shape=[jax.ShapeDtypeStruct((s, D_MODEL), F32), jax.ShapeDtypeStruct((s, D_MODEL), BF),
                   jax.ShapeDtypeStruct((s, D_MODEL), BF)],
        compiler_params=_params(VMEM_MID),
    )(x, nc, o, g_attn_out, w_o, g_ffn)


def _load_weights(i, pairs, sem):
    @pl.when(i == 0)
    def _():
        cps = [pltpu.make_async_copy(src, dst, sem.at[n]) for n, (src, dst) in enumerate(pairs)]
        for cp in cps:
            cp.start()
        for cp in cps:
            cp.wait()


def _ff_cols(j):
    return (slice(j * FF_CHUNK, (j + 1) * FF_CHUNK), slice(D_FF + j * FF_CHUNK, D_FF + (j + 1) * FF_CHUNK))


def _fwd_ffn(h2, x2, tgt, w_up, w_ffn_conv, w_dn, g_final, tm):
    s = x2.shape[0]

    def body(h2_ref, x2_ref, tgt_ref, wfc_ref, gfin_ref, wup_any, wdn_any,
             u0_ref, uc_ref, act_ref, dx3_ref, dx3b_ref, loss_ref, dgfin_ref,
             wup, wdn, carry, sem):
        i = pl.program_id(0)
        _load_weights(i, [(wup_any, wup), (wdn_any, wdn)], sem)

        @pl.when(i == 0)
        def _():
            carry[...] = jnp.zeros_like(carry)
            loss_ref[...] = jnp.zeros_like(loss_ref)
            dgfin_ref[...] = jnp.zeros_like(dgfin_ref)

        hb = h2_ref[...]

        def up(j):
            return [_dot(hb, wup[:, cc]) for cc in _ff_cols(j)]

        nxt = up(0)
        down = None
        for j in range(N_FF_CHUNKS):
            cur = nxt
            if j + 1 < N_FF_CHUNKS:
                nxt = up(j + 1)
            parts = []
            for cc, u0 in zip(_ff_cols(j), cur):
                u0_ref[:, cc] = u0.astype(BF)
                uu, _, _ = _conv3(u0, carry[:, cc], wfc_ref[:, cc])
                carry[:, cc] = u0[tm - 8:, :]
                uc_ref[:, cc] = uu.astype(BF)
                parts.append(uu)
            ua, ug = parts
            act = (ug * jax.nn.sigmoid(ug) * ua).astype(BF)
            ca = _ff_cols(j)[0]
            act_ref[:, ca] = act
            part = _dot(act, wdn[ca, :])
            down = part if down is None else down + part

        x3 = x2_ref[...] + down
        r3 = _rms(x3)
        gfin = gfin_ref[...]
        xn = x3 * r3
        diff = xn * gfin - tgt_ref[...]
        loss_ref[...] += jnp.sum(jnp.sum(diff * diff, axis=-1, keepdims=True), axis=0, keepdims=True) * (0.5 / D_MODEL)
        dy = diff * (1.0 / D_MODEL)
        dgfin_ref[...] += jnp.sum(dy * xn, axis=0, keepdims=True)
        dx3 = _rms_bwd(x3, r3, dy * gfin)
        dx3_ref[...] = dx3
        dx3b_ref[...] = dx3.astype(BF)

    blk = lambda c: pl.BlockSpec((tm, c), lambda i: (i, 0))
    any_ = pl.BlockSpec(memory_space=pl.ANY)
    return pl.pallas_call(
        body, name="fwd_ffn", grid=(s // tm,),
        in_specs=[blk(D_MODEL), blk(D_MODEL), blk(D_MODEL), _full((8, 2 * D_FF)), _full((1, D_MODEL)), any_, any_],
        out_specs=[blk(2 * D_FF), blk(2 * D_FF), blk(D_FF), blk(D_MODEL), blk(D_MODEL), _full((1, 1)),
                   _full((1, D_MODEL))],
        out_shape=[jax.ShapeDtypeStruct((s, 2 * D_FF), BF), jax.ShapeDtypeStruct((s, 2 * D_FF), BF),
                   jax.ShapeDtypeStruct((s, D_FF), BF),
                   jax.ShapeDtypeStruct((s, D_MODEL), F32), jax.ShapeDtypeStruct((s, D_MODEL), BF),
                   jax.ShapeDtypeStruct((1, 1), F32), jax.ShapeDtypeStruct((1, D_MODEL), F32)],
        scratch_shapes=[pltpu.VMEM(w_up.shape, BF), pltpu.VMEM(w_dn.shape, BF), pltpu.VMEM((8, 2 * D_FF), F32),
                        pltpu.SemaphoreType.DMA((2,))],
        compiler_params=_params(VMEM_BIG),
    )(h2, x2, tgt, w_ffn_conv, g_final, w_up, w_dn)


def _bwd_ffn(dx3, dx3b, u0, uc, x2, w_up, w_ffn_conv, w_dn, g_ffn, tm):
    s = x2.shape[0]
    nb = s // tm

    def body(dx3_ref, dx3b_ref, u0_ref, uc_ref, x2_ref, wfc_ref, gf_ref, wup_any, wdn_any,
             du0_ref, dx2_ref, dx2b_ref, dwfc_ref, dgf_ref,
             wup, wdn, carry, sem):
        i = pl.program_id(0)
        _load_weights(i, [(wup_any, wup), (wdn_any, wdn)], sem)

        @pl.when(i == 0)
        def _():
            carry[...] = jnp.zeros_like(carry)
            dwfc_ref[...] = jnp.zeros_like(dwfc_ref)
            dgf_ref[...] = jnp.zeros_like(dgf_ref)

        db = dx3b_ref[...]

        def dact_of(j):
            return _dot(db, wdn[:, _ff_cols(j)[0]])

        nxt = dact_of(0)
        dh2 = None
        for j in range(N_FF_CHUNKS):
            ca, cg = _ff_cols(j)
            dact = nxt
            if j + 1 < N_FF_CHUNKS:
                nxt = dact_of(j + 1)
            ua = uc_ref[:, ca].astype(F32)
            ug = uc_ref[:, cg].astype(F32)
            sg = jax.nn.sigmoid(ug)
            da = dact * (ug * sg)
            dg = dact * ua * (sg * (1.0 + ug * (1.0 - sg)))
            for cc, d in ((ca, da), (cg, dg)):
                nxt_rows = carry[:, cc]
                d1 = _shift_up(d, nxt_rows, 1)
                d2 = _shift_up(d, nxt_rows, 2)
                u0c = u0_ref[:, cc].astype(F32)
                w = wfc_ref[:, cc]
                dwfc_ref[0:1, cc] += jnp.sum(d2 * u0c, axis=0, keepdims=True)
                dwfc_ref[1:2, cc] += jnp.sum(d1 * u0c, axis=0, keepdims=True)
                dwfc_ref[2:3, cc] += jnp.sum(d * u0c, axis=0, keepdims=True)
                du0 = (w[2:3, :] * d + w[1:2, :] * d1 + w[0:1, :] * d2).astype(BF)
                carry[:, cc] = d[:8, :]
                du0_ref[:, cc] = du0
                part = _dot(du0, wup[cc, :])
                dh2 = part if dh2 is None else dh2 + part

        x2v = x2_ref[...]
        r2 = _rms(x2v)
        dgf_ref[...] += jnp.sum(dh2 * (x2v * r2), axis=0, keepdims=True)
        dx2 = dx3_ref[...] + _rms_bwd(x2v, r2, dh2 * gf_ref[...])
        dx2_ref[...] = dx2
        dx2b_ref[...] = dx2.astype(BF)

    blk = lambda c: pl.BlockSpec((tm, c), lambda i: (nb - 1 - i, 0))
    any_ = pl.BlockSpec(memory_space=pl.ANY)
    return pl.pallas_call(
        body, name="bwd_ffn", grid=(nb,),
        in_specs=[blk(D_MODEL), blk(D_MODEL), blk(2 * D_FF), blk(2 * D_FF), blk(D_MODEL), _full((8, 2 * D_FF)),
                  _full((1, D_MODEL)), any_, any_],
        out_specs=[blk(2 * D_FF), blk(D_MODEL), blk(D_MODEL), _full((8, 2 * D_FF)), _full((1, D_MODEL))],
        out_shape=[jax.ShapeDtypeStruct((s, 2 * D_FF), BF), jax.ShapeDtypeStruct((s, D_MODEL), F32),
                   jax.ShapeDtypeStruct((s, D_MODEL), BF), jax.ShapeDtypeStruct((8, 2 * D_FF), F32),
                   jax.ShapeDtypeStruct((1, D_MODEL), F32)],
        scratch_shapes=[pltpu.VMEM(w_up.shape, BF), pltpu.VMEM(w_dn.shape, BF), pltpu.VMEM((8, 2 * D_FF), F32),
                        pltpu.SemaphoreType.DMA((2,))],
        compiler_params=_params(VMEM_BIG),
    )(dx3, dx3b, u0, uc, x2, w_ffn_conv, g_ffn, w_up, w_dn)


def _bwd_out(dx2b, o, zc, fcol, lcol, qa, w_o, g_attn_out, g_conv_out, w_conv, tm):
    s = o.shape[0]
    nb = s // tm
    aw = N_HEADS * LANES

    def body(dx2b_ref, o_ref, zc_ref, halo_ref, fcol_ref, lcol_ref, qa_ref, wo_ref, ga_ref, gco_ref, wc_ref, sc_ref,
             dzc_ref, doa_ref, dot_ref, qab_ref, dwc_ref, dga_ref, dgco_ref, carry):
        i = pl.program_id(0)
        rb = nb - 1 - i

        @pl.when(i == 0)
        def _():
            carry[...] = jnp.zeros_like(carry)
            dwc_ref[...] = jnp.zeros_like(dwc_ref)
            dga_ref[...] = jnp.zeros_like(dga_ref)
            dgco_ref[...] = jnp.zeros_like(dgco_ref)

        dmix = _dot_nt(dx2b_ref[...], wo_ref[...])
        dnc, dna = dmix[:, :CONV_CH], dmix[:, CONV_CH:]

        ov = o_ref[...]
        ra = _rms(ov)
        dga_ref[...] += jnp.sum(dna * (ov * ra), axis=0, keepdims=True)
        do = _rms_bwd(ov, ra, dna * ga_ref[...])
        dob = do.astype(BF)
        dot_ref[0] = do.T.astype(BF)
        sel = (_rows((ATTN_W, LANES)) // HEAD_DIM == _cols((ATTN_W, LANES))).astype(F32)
        delta = jnp.dot(do * ov, sel, precision=HI, preferred_element_type=F32)
        featc = _dot(_split_parts(fcol_ref[...] - lcol_ref[...], 1.0), sc_ref[...]).astype(BF)
        featd = _dot(_split_parts(-delta, 0.0), sc_ref[...]).astype(BF)
        lane = _cols((tm, LANES))
        for h in range(N_HEADS):
            hp, hh = divmod(h, 2)
            own = (lane >= HEAD_DIM * hh) & (lane < HEAD_DIM * (hh + 1))
            hs = slice(LANES * h, LANES * (h + 1))
            qab_ref[:, hs] = jnp.where(own, qa_ref[:, hs], featc[:, hs])
            doa_ref[:, hs] = jnp.where(own, dob[:, LANES * hp:LANES * (hp + 1)], featd[:, hs])

        zc_v = zc_ref[...]
        gb, gc, xc = zc_v[:, :CONV_CH], zc_v[:, CONV_CH:2 * CONV_CH], zc_v[:, 2 * CONV_CH:]
        hal = halo_ref[...] * (rb > 0).astype(F32)
        u = gc * xc
        prev = hal[:, CONV_CH:2 * CONV_CH] * hal[:, 2 * CONV_CH:]
        wc = wc_ref[...]
        cv, u1, u2 = _conv3(u, prev, wc)
        y = gb * cv
        rc = _rms(y)
        dgco_ref[...] += jnp.sum(dnc * (y * rc), axis=0, keepdims=True)
        dy = _rms_bwd(y, rc, dnc * gco_ref[...])
        dcv = dy * gb
        dwc_ref[0:1, :] += jnp.sum(dcv * u2, axis=0, keepdims=True)
        dwc_ref[1:2, :] += jnp.sum(dcv * u1, axis=0, keepdims=True)
        dwc_ref[2:3, :] += jnp.sum(dcv * u, axis=0, keepdims=True)
        du = _conv3_bwd(dcv, carry[...], wc)
        carry[...] = dcv[:8, :]
        dzc_ref[:, :CONV_CH] = (dy * cv).astype(BF)
        dzc_ref[:, CONV_CH:2 * CONV_CH] = (du * xc).astype(BF)
        dzc_ref[:, 2 * CONV_CH:] = (du * gc).astype(BF)

    blk = lambda c: pl.BlockSpec((tm, c), lambda i: (nb - 1 - i, 0))
    halo = pl.BlockSpec((8, 3 * CONV_CH), lambda i: (jnp.maximum((nb - 1 - i) * (tm // 8) - 1, 0), 0))
    tr = pl.BlockSpec((1, ATTN_W, tm), lambda i: (nb - 1 - i, 0, 0))
    return pl.pallas_call(
        body, name="bwd_out", grid=(nb,),
        in_specs=[blk(D_MODEL), blk(ATTN_W), blk(3 * CONV_CH), halo, blk(LANES), blk(LANES), blk(aw),
                  _full(w_o.shape), _full((1, ATTN_W)), _full((1, CONV_CH)), _full((8, CONV_CH)), _full((LANES, aw))],
        out_specs=[blk(3 * CONV_CH), blk(aw), tr, blk(aw),
                   _full((8, CONV_CH)), _full((1, ATTN_W)), _full((1, CONV_CH))],
        out_shape=[jax.ShapeDtypeStruct((s, 3 * CONV_CH), BF),
                   jax.ShapeDtypeStruct((s, aw), BF), jax.ShapeDtypeStruct((nb, ATTN_W, tm), BF),
                   jax.ShapeDtypeStruct((s, aw), BF),
                   jax.ShapeDtypeStruct((8, CONV_CH), F32), jax.ShapeDtypeStruct((1, ATTN_W), F32),
                   jax.ShapeDtypeStruct((1, CONV_CH), F32)],
        scratch_shapes=[pltpu.VMEM((8, CONV_CH), F32)],
        compiler_params=_params(VMEM_MID),
    )(dx2b, o, zc, zc, fcol, lcol, qa, w_o, g_attn_out, g_conv_out, w_conv, _bias_scatter()[:, :aw])


def _attn_bwd(qab, doa, ka, qkv, qt, dot, t, blocks):
    s = qab.shape[0]
    nb = s // t
    npair = N_HEADS // 2
    n = len(blocks)

    def body(ka_ref, v_ref, qab_ref, doa_ref, qt_ref, dot_ref, *refs):
        block_refs, (dk_ref, dv_ref, dfk_ref, dqa_ref, dq_ref) = refs[:n], refs[n:n + 5]
        received = refs[n + 5:2 * n + 5]
        acck, accv = refs[2 * n + 5:2 * n + 7]
        hp = pl.program_id(0)
        j = pl.program_id(1)
        start, finish = _scatter_phases(block_refs, received, *refs[2 * n + 7:])
        pl.when((hp == 0) & (j == 0))(start)

        @pl.when(j == 0)
        def _():
            dqa_ref[...] = jnp.zeros_like(dqa_ref)

        causal = _cols((t, t)) <= _rows((t, t))
        row = _rows((LANES, t))
        lane = _cols((t, LANES))
        acck[...] = jnp.zeros(acck.shape, F32)
        accv[...] = jnp.zeros(accv.shape, F32)
        v2 = v_ref[...]
        heads = []
        for hh in range(2):
            aux = HEAD_DIM * (1 - hh)
            ownl = (lane >= HEAD_DIM * hh) & (lane < HEAD_DIM * (hh + 1))
            ones3 = jnp.where((lane >= aux) & (lane < aux + 3), 1.0, 0.0).astype(BF)
            heads.append(dict(hh=hh, hs=slice(hh * LANES, (hh + 1) * LANES), aux=aux,
                              own=(row >= HEAD_DIM * hh) & (row < HEAD_DIM * (hh + 1)),
                              fill=jnp.where(row == aux, 1.0, 0.0).astype(BF),
                              kh=ka_ref[:, hh * LANES:(hh + 1) * LANES], vh=jnp.where(ownl, v2, ones3)))

        def tile(qb, masked):
            rs = pl.ds(pl.multiple_of(qb * t, t), t)
            qt2 = qt_ref[qb]
            dot2 = dot_ref[qb]
            es = [_dot_nt(qab_ref[rs, hd["hs"]], hd["kh"]) for hd in heads]
            dps = [_dot_nt(doa_ref[rs, hd["hs"]], hd["vh"]) for hd in heads]
            aks = [acck[hd["hh"]] for hd in heads]
            avs = [accv[hd["hh"]] for hd in heads]
            dqs = [dqa_ref[rs, hd["hs"]] for hd in heads]
            outs = []
            for hd, e, dp, ak, av, dq in zip(heads, es, dps, aks, avs, dqs):
                if masked:
                    e = jnp.where(causal, e, NEG)
                p = jnp.exp(e)
                ds = (p * dp).astype(BF)
                outs.append((ak + _dot(jnp.where(hd["own"], qt2, hd["fill"]), ds), av + _dot(dot2, p.astype(BF)),
                             dq + _dot(ds, hd["kh"])))
            for hd, (ak, av, dq) in zip(heads, outs):
                acck[hd["hh"]] = ak
                accv[hd["hh"]] = av
                dqa_ref[rs, hd["hs"]] = dq

        def loop_body(qb, carry):
            tile(qb, False)
            return carry

        tile(j, True)
        lax.fori_loop(j + 1, nb, loop_body, 0)
        dk_pair = dv_pair = None
        dfrows = jnp.zeros((LANES, t), F32)
        for hd in heads:
            ak, av = acck[hd["hh"]], accv[hd["hh"]]
            dfrows = jnp.where(row == 2 * hp + hd["hh"], -ak[hd["aux"]:hd["aux"] + 1, :], dfrows)
            dk_pair = ak if hd["hh"] == 0 else jnp.where(row < HEAD_DIM, dk_pair, ak)
            dv_pair = av if hd["hh"] == 0 else jnp.where(row < HEAD_DIM, dv_pair, av)
        dk_ref[...] = dk_pair.T.astype(BF)
        dv_ref[...] = dv_pair.T.astype(BF)
        dfk_ref[0] = dfrows.T

        @pl.when(j == nb - 1)
        def _():
            for r in range(nb):
                rows = slice(r * t, (r + 1) * t)
                dq_ref[rows, :] = (jnp.where(lane < HEAD_DIM, dqa_ref[rows, :LANES], dqa_ref[rows, LANES:])
                                   * 0.125).astype(BF)

        pl.when((hp == npair - 1) & (j == nb - 1))(finish)

    pair_cols = pl.BlockSpec((s, 2 * LANES), lambda hp, j: (0, hp))
    pair_rows = pl.BlockSpec((nb, LANES, t), lambda hp, j: (0, hp, 0))
    any_ = pl.BlockSpec(memory_space=pl.ANY)
    return pl.pallas_call(
        body, name="attn_bwd", grid=(npair, nb),
        in_specs=[pl.BlockSpec((t, 2 * LANES), lambda hp, j: (j, hp)),
                  pl.BlockSpec((t, LANES), lambda hp, j: (j, 2 * npair + hp)),
                  pair_cols, pair_cols, pair_rows, pair_rows] + [any_] * n,
        out_specs=[pl.BlockSpec((t, LANES), lambda hp, j: (j, hp)), pl.BlockSpec((t, LANES), lambda hp, j: (j, hp)),
                   pl.BlockSpec((1, t, LANES), lambda hp, j: (hp, j, 0)), pair_cols,
                   pl.BlockSpec((s, LANES), lambda hp, j: (0, hp))] + [any_] * n,
        out_shape=[jax.ShapeDtypeStruct((s, ATTN_W), BF), jax.ShapeDtypeStruct((s, ATTN_W), BF),
                   jax.ShapeDtypeStruct((npair, s, LANES), F32), jax.ShapeDtypeStruct((s, N_HEADS * LANES), F32),
                   jax.ShapeDtypeStruct((s, ATTN_W), BF)]
        + [jax.ShapeDtypeStruct(b.shape, b.dtype) for b in blocks],
        scratch_shapes=[pltpu.VMEM((2, LANES, t), F32), pltpu.VMEM((2, LANES, t), F32)] + _exchange_sems(n),
        compiler_params=_params(VMEM_BIG),
    )(ka, qkv, qab, doa, qt, dot, *blocks)


def _bwd_in(x, dx2, dzc, dq, dqa, dk, dv, dfk, fpre, w_zc, w_qkv, w_f, g_mix, tm, blocks):
    s = x.shape[0]
    nb = s // tm
    n = len(blocks)

    def body(x_ref, dx2_ref, dzc_ref, dq_ref, dqa_ref, dk_ref, dv_ref, dfk_ref, fpre_ref, wzc_ref, wqkv_ref, wf_ref,
             gm_ref, *refs):
        block_refs, (gx_ref, dfb_ref, dgm_ref, dbf_ref), received = refs[:n], refs[n:n + 4], refs[n + 4:2 * n + 4]
        carry = refs[2 * n + 4]
        i = pl.program_id(0)
        start, finish = _scatter_phases(block_refs, received, *refs[2 * n + 5:])
        pl.when(i == 0)(start)

        @pl.when(i == 0)
        def _():
            carry[...] = jnp.zeros_like(carry)
            dgm_ref[...] = jnp.zeros_like(dgm_ref)
            dbf_ref[...] = jnp.zeros_like(dbf_ref)

        lane = _cols((tm, LANES))
        dfq = jnp.zeros((tm, LANES), F32)
        for h in range(N_HEADS):
            aux = LANES * h + HEAD_DIM * (1 - h % 2)
            dfq = jnp.where(lane == h, dqa_ref[:, aux:aux + 1], dfq)

        triu = (_rows((tm, tm)) <= _cols((tm, tm))).astype(F32)
        df_cum = dfq + ((dfk_ref[0] + dfk_ref[1]) + (dfk_ref[2] + dfk_ref[3]))
        dlogf = jnp.dot(triu, df_cum, precision=HI, preferred_element_type=F32) + carry[...]
        carry[...] = dlogf[0:1, :]
        fpre = fpre_ref[...]
        df = jnp.where(_cols(fpre.shape) < N_HEADS, dlogf / (1.0 + jnp.exp(fpre)), 0.0)
        dbf_ref[...] += jnp.sum(df, axis=0, keepdims=True)
        dfb = df.astype(BF)
        dfb_ref[...] = dfb

        dh1 = _dot_nt(dzc_ref[...], wzc_ref[...])
        dh1 += _dot_nt(dq_ref[...], wqkv_ref[:, :ATTN_W])
        dh1 += _dot_nt(dk_ref[...], wqkv_ref[:, ATTN_W:2 * ATTN_W])
        dh1 += _dot_nt(dv_ref[...], wqkv_ref[:, 2 * ATTN_W:])
        dh1 += _dot_nt(dfb, wf_ref[...])
        xv = x_ref[...]
        r1 = _rms(xv)
        dgm_ref[...] += jnp.sum(dh1 * (xv * r1), axis=0, keepdims=True)
        gx_ref[...] = dx2_ref[...] + _rms_bwd(xv, r1, dh1 * gm_ref[...])
        pl.when(i == nb - 1)(finish)

    blk = lambda c: pl.BlockSpec((tm, c), lambda i: (nb - 1 - i, 0))
    any_ = pl.BlockSpec(memory_space=pl.ANY)
    return pl.pallas_call(
        body, name="bwd_in", grid=(nb,),
        in_specs=[blk(D_MODEL), blk(D_MODEL), blk(3 * CONV_CH), blk(ATTN_W), blk(N_HEADS * LANES), blk(ATTN_W),
                  blk(ATTN_W), pl.BlockSpec((N_HEADS // 2, tm, LANES), lambda i: (0, nb - 1 - i, 0)), blk(LANES),
                  _full(w_zc.shape), _full(w_qkv.shape), _full(w_f.shape), _full((1, D_MODEL))] + [any_] * n,
        out_specs=[blk(D_MODEL), blk(LANES), _full((1, D_MODEL)), _full((1, LANES))] + [any_] * n,
        out_shape=[jax.ShapeDtypeStruct((s, D_MODEL), F32), jax.ShapeDtypeStruct((s, LANES), BF),
                   jax.ShapeDtypeStruct((1, D_MODEL), F32), jax.ShapeDtypeStruct((1, LANES), F32)]
        + [jax.ShapeDtypeStruct(b.shape, b.dtype) for b in blocks],
        scratch_shapes=[pltpu.VMEM((1, LANES), F32)] + _exchange_sems(n),
        compiler_params=_params(VMEM_MID),
    )(x, dx2, dzc, dq, dqa, dk, dv, dfk, fpre, w_zc, w_qkv, w_f, g_mix, *blocks)


def _matmul_tn(a, b, bn, bk, name):
    s, m = a.shape
    n = b.shape[1]

    def body(a_ref, b_ref, o_ref):
        @pl.when(pl.program_id(1) == 0)
        def _():
            o_ref[...] = jnp.zeros_like(o_ref)

        o_ref[...] += _dot_tn(a_ref[...], b_ref[...])

    return pl.pallas_call(
        body, name=name, grid=(n // bn, s // bk),
        in_specs=[pl.BlockSpec((bk, m), lambda jn, k: (k, 0)), pl.BlockSpec((bk, bn), lambda jn, k: (k, jn))],
        out_specs=pl.BlockSpec((m, bn), lambda jn, k: (0, jn)),
        out_shape=jax.ShapeDtypeStruct((m, n), F32),
        compiler_params=_params(VMEM_MID),
    )(a, b)


def _flip(v, bit):
    return 1 - v if bit else v


def _all_gather(shards):
    n = len(shards)

    def body(*refs):
        start, forward, finish = _gather_phases(refs[:n], refs[n:2 * n], *refs[2 * n:])
        start()
        forward()
        finish()

    any_ = pl.BlockSpec(memory_space=pl.ANY)
    return pl.pallas_call(
        body, name="all_gather_weights",
        in_specs=[any_] * n, out_specs=[any_] * n,
        out_shape=_gathered_shapes(shards), scratch_shapes=_exchange_sems(n),
    )(*shards)


def _gathered_shapes(shards):
    return [jax.ShapeDtypeStruct((8,) + sh.shape, sh.dtype) for sh in shards]


def _exchange_sems(n):
    return [pltpu.SemaphoreType.DMA((7 * n,)), pltpu.SemaphoreType.DMA((7 * n,)), pltpu.SemaphoreType.DMA((n,))]


def _gather_phases(src, out, send_sems, recv_sems, loc_sems):
    n = len(src)
    x, y, c = lax.axis_index("x"), lax.axis_index("y"), lax.axis_index("c")
    me, sibling = (x, y, c), (x, y, 1 - c)
    chips = [(1 - x, y), (x, 1 - y), (1 - x, 1 - y)]

    def slot(a, px, py, pc):
        return out[a].at[4 * px + 2 * py + pc]

    def copy(a, k, block, to, from_src=False):
        return pltpu.make_async_remote_copy(
            src_ref=src[a] if from_src else slot(a, *block), dst_ref=slot(a, *block),
            send_sem=send_sems.at[7 * a + k], recv_sem=recv_sems.at[7 * a + k],
            device_id=to, device_id_type=MESH)

    def local(a):
        return pltpu.make_async_copy(src[a], slot(a, *me), loc_sems.at[a])

    def first(a):
        return [copy(a, 0, me, sibling, True)] + [copy(a, 1 + j, me, (*chip, c), True)
                                                  for j, chip in enumerate(chips)]

    def passed(a, j):
        return copy(a, 4 + j, (*chips[j], c), sibling)

    def start():
        for a in range(n):
            local(a).start()
            for cp in first(a):
                cp.start()

    def forward():
        for a in range(n):
            for j, chip in enumerate(chips):
                copy(a, 1 + j, (*chip, c), me).wait_recv()
                passed(a, j).start()

    def finish():
        for a in range(n):
            copy(a, 0, sibling, me).wait_recv()
            for j, chip in enumerate(chips):
                copy(a, 4 + j, (*chip, 1 - c), me).wait_recv()
        for a in range(n):
            for cp in first(a) + [passed(a, j) for j in range(3)]:
                cp.wait_send()
            local(a).wait()

    return start, forward, finish


def _scatter_phases(src, out, send_sems, recv_sems, loc_sems):
    n = len(src)
    masks = [((k >> 2) & 1, (k >> 1) & 1, k & 1) for k in range(1, 8)]
    x, y, c = lax.axis_index("x"), lax.axis_index("y"), lax.axis_index("c")
    me = 4 * x + 2 * y + c

    def copies():
        cps = []
        for a in range(n):
            cps.append(pltpu.make_async_copy(src[a].at[me], out[a].at[me], loc_sems.at[a]))
            for k, (mx, my, mc) in enumerate(masks):
                px, py, pc = _flip(x, mx), _flip(y, my), _flip(c, mc)
                cps.append(pltpu.make_async_remote_copy(
                    src_ref=src[a].at[4 * px + 2 * py + pc], dst_ref=out[a].at[me],
                    send_sem=send_sems.at[7 * a + k], recv_sem=recv_sems.at[7 * a + k],
                    device_id=(px, py, pc), device_id_type=MESH))
        return cps

    def start():
        for cp in copies():
            cp.start()

    def finish():
        for cp in copies():
            cp.wait()

    return start, finish


def _all_to_all(blocks, name):
    n = len(blocks)

    def body(*refs):
        start, finish = _scatter_phases(refs[:n], refs[n:2 * n], *refs[2 * n:])
        start()
        finish()

    any_ = pl.BlockSpec(memory_space=pl.ANY)
    return pl.pallas_call(
        body, name=name,
        in_specs=[any_] * n, out_specs=[any_] * n,
        out_shape=[jax.ShapeDtypeStruct(b.shape, b.dtype) for b in blocks], scratch_shapes=_exchange_sems(n),
    )(*blocks)


def _adamw(parts, w, m, v, br, name, tail=None):
    g8, r, c = parts.shape
    c1 = 1.0 - ADAM_B1 ** ADAM_STEP
    c2 = 1.0 - ADAM_B2 ** ADAM_STEP
    extra = [] if tail is None else [tail]

    def body(p_ref, w_ref, m_ref, v_ref, *refs):
        g_ref, d_ref, m2_ref, v2_ref = refs[len(extra):]
        g = p_ref[0]
        for d in range(1, g8):
            g = g + p_ref[d]
        if tail is not None:
            k = tail.shape[1]
            place = (_rows((k, c)) + (c - k) == _cols((k, c))).astype(F32)
            g = g + jnp.dot(refs[0][...], place, precision=HI, preferred_element_type=F32)
        m2 = ADAM_B1 * m_ref[...] + (1.0 - ADAM_B1) * g
        v2 = ADAM_B2 * v_ref[...] + (1.0 - ADAM_B2) * (g * g)
        g_ref[...] = g
        m2_ref[...] = m2
        v2_ref[...] = v2
        d_ref[...] = -ADAM_LR * ((m2 / c1) / (jnp.sqrt(v2 / c2) + ADAM_EPS) + ADAM_WD * w_ref[...])

    blk = pl.BlockSpec((br, c), lambda i: (i, 0))
    out = jax.ShapeDtypeStruct((r, c), F32)
    return pl.pallas_call(
        body, name=name, grid=(r // br,),
        in_specs=[pl.BlockSpec((g8, br, c), lambda i: (0, i, 0)), blk, blk, blk]
        + [pl.BlockSpec((br, e.shape[1]), lambda i: (i, 0)) for e in extra],
        out_specs=[blk] * 4, out_shape=[out] * 4,
        compiler_params=_params(VMEM_MID),
    )(parts, w, m, v, *extra)


def _pad_rows(a, rows):
    return jnp.pad(a, ((0, rows - a.shape[0]), (0, 0)))


_SMALL = (("g_mix", 1024), ("b_f", 8), ("g_conv_out", 512), ("g_attn_out", 512), ("g_ffn", 1024), ("g_final", 1024))
_SMALL_LEN = sum(n for _, n in _SMALL)
_SMALL_ROWS = 104


def _pack_small(vals, extra=None):
    parts = [vals[k].reshape(-1) for k, _ in _SMALL]
    if extra is not None:
        parts.append(extra.reshape(-1))
    flat = jnp.concatenate(parts)
    return jnp.pad(flat, (0, _SMALL_ROWS * LANES - flat.shape[0])).reshape(_SMALL_ROWS, LANES)


def _unpack_small(packed, shapes):
    flat = packed.reshape(-1)
    out, off = {}, 0
    for k, n in _SMALL:
        out[k] = flat[off:off + n].reshape(shapes[k])
        off += n
    return out


def _col_blocks(a):
    r, c8 = a.shape
    return jnp.transpose(a.reshape(r, 8, c8 // 8), (1, 0, 2))


def _from_col_blocks(a):
    g, r, c = a.shape
    return jnp.transpose(a, (1, 0, 2)).reshape(r, g * c)


def kernel(x, g_mix, w_in, b_f, w_conv, g_conv_out, g_attn_out, w_o, g_ffn, w_up, w_ffn_conv, w_down, g_final, loss_target, m_g_mix, m_w_in, m_b_f, m_w_conv, m_g_conv_out, m_g_attn_out, m_w_o, m_g_ffn, m_w_up, m_w_ffn_conv, m_w_down, m_g_final, v_g_mix, v_w_in, v_b_f, v_w_conv, v_g_conv_out, v_g_attn_out, v_w_o, v_g_ffn, v_w_up, v_w_ffn_conv, v_w_down, v_g_final):
    w = dict(g_mix=g_mix, w_in=w_in[0], b_f=b_f, w_conv=w_conv[0], g_conv_out=g_conv_out, g_attn_out=g_attn_out,
             w_o=w_o[0], g_ffn=g_ffn, w_up=w_up[0], w_ffn_conv=w_ffn_conv[0], w_down=w_down[0], g_final=g_final)
    m = dict(g_mix=m_g_mix, w_in=m_w_in[0], b_f=m_b_f, w_conv=m_w_conv[0], g_conv_out=m_g_conv_out,
             g_attn_out=m_g_attn_out, w_o=m_w_o[0], g_ffn=m_g_ffn, w_up=m_w_up[0], w_ffn_conv=m_w_ffn_conv[0],
             w_down=m_w_down[0], g_final=m_g_final)
    v = dict(g_mix=v_g_mix, w_in=v_w_in[0], b_f=v_b_f, w_conv=v_w_conv[0], g_conv_out=v_g_conv_out,
             g_attn_out=v_g_attn_out, w_o=v_w_o[0], g_ffn=v_g_ffn, w_up=v_w_up[0], w_ffn_conv=v_w_ffn_conv[0],
             w_down=v_w_down[0], g_final=v_g_final)
    shapes = dict(g_mix=g_mix.shape, w_in=w_in.shape, b_f=b_f.shape, w_conv=w_conv.shape,
                  g_conv_out=g_conv_out.shape, g_attn_out=g_attn_out.shape, w_o=w_o.shape, g_ffn=g_ffn.shape,
                  w_up=w_up.shape, w_ffn_conv=w_ffn_conv.shape, w_down=w_down.shape, g_final=g_final.shape)

    tm = t = 512
    tf = 256
    xs, tgt = x[0], loss_target[0]
    seq = xs.shape[0]
    assert seq % tm == 0 and seq % tf == 0
    bk = 1024 if seq % 1024 == 0 else 512

    g_in, g_conv = _all_gather([w["w_in"].astype(BF), w["w_conv"]])
    w_in_f = _from_col_blocks(g_in)
    w_zc = w_in_f[:, :3 * CONV_CH]
    w_qkv = w_in_f[:, 3 * CONV_CH:3 * CONV_CH + 3 * ATTN_W]
    w_f = jnp.pad(w_in_f[:, 3 * CONV_CH + 3 * ATTN_W:], ((0, 0), (0, LANES - N_HEADS)))
    b_f_p = jnp.pad(b_f, ((0, 0), (0, LANES - N_HEADS)))
    w_conv_p = _pad_rows(_from_col_blocks(g_conv), 8)

    h1, zc, qkv, fpre, fcol, nc, qa, ka, qt, vt = _fwd_in(
        xs, g_mix, w_zc, w_qkv, w_f, b_f_p, w_conv_p, g_conv_out, tm)
    o, lcol, g_o, g_up, g_dn, g_fc = _attn_fwd(
        qa, ka, vt, t, [w["w_o"].astype(BF), w["w_up"].astype(BF), w["w_down"].astype(BF), w["w_ffn_conv"]])
    w_o_f = g_o.reshape(D_MODEL, D_MODEL)
    w_up_f = _from_col_blocks(g_up)
    w_dn_f = g_dn.reshape(D_FF, D_MODEL)
    w_fc_p = _pad_rows(_from_col_blocks(g_fc), 8)
    x2, h2, mix = _fwd_out(xs, nc, o, g_attn_out, w_o_f, g_ffn, tm)
    u0, uc, act, dx3, dx3b, loss, dg_final = _fwd_ffn(
        h2, x2, tgt, w_up_f, w_fc_p, w_dn_f, g_final.reshape(1, D_MODEL), tf)
    loss_part = loss

    du0, dx2, dx2b, dw_fc, dg_ffn = _bwd_ffn(dx3, dx3b, u0, uc, x2, w_up_f.T, w_fc_p, w_dn_f.T, g_ffn, tf)
    dw_up = _matmul_tn(h2, du0, 1408, bk, "dw_up")
    dw_dn = _matmul_tn(act, dx3b, 512, bk, "dw_down")
    dw_o = _matmul_tn(mix, dx2b, 512, bk, "dw_o")
    dzc, doa, dot, qab, dw_conv, dg_attn, dg_conv = _bwd_out(
        dx2b, o, zc, fcol, lcol, qa, w_o_f, g_attn_out, g_conv_out, w_conv_p, t)
    dk, dv, dfk, dqa, dq, r_up, r_dn, r_fc, r_o = _attn_bwd(
        qab, doa, ka, qkv, qt, dot, t,
        [_col_blocks(dw_up), dw_dn.reshape(8, D_FF // 8, D_MODEL), _col_blocks(dw_fc[:3]),
         dw_o.reshape(8, D_MODEL // 8, D_MODEL)])
    dw_zc = _matmul_tn(h1, dzc, 3 * CONV_CH, bk, "dw_in_conv")
    dw_q = _matmul_tn(h1, dq, ATTN_W, bk, "dw_in_q")
    dw_k = _matmul_tn(h1, dk, ATTN_W, bk, "dw_in_k")
    dw_v = _matmul_tn(h1, dv, ATTN_W, bk, "dw_in_v")
    dw_main = jnp.concatenate([dw_zc, dw_q, dw_k, dw_v, jnp.zeros((D_MODEL, N_HEADS), F32)], axis=1)
    gx, dfb, dg_mix, db_f, r_in = _bwd_in(xs, dx2, dzc, dq, dqa, dk, dv, dfk, fpre, w_zc, w_qkv, w_f, g_mix, tm,
                                          [_col_blocks(dw_main)])
    dw_f = _matmul_tn(h1, dfb, LANES, bk, "dw_in_f")
    small = dict(g_mix=dg_mix, b_f=db_f[:, :N_HEADS], g_conv_out=dg_conv, g_attn_out=dg_attn, g_ffn=dg_ffn,
                 g_final=dg_final)
    riders = jnp.concatenate([loss_part.reshape(-1), dw_f[:, :N_HEADS].reshape(-1)])
    r_conv, r_small = _all_to_all(
        [_col_blocks(dw_conv[:3]), jnp.broadcast_to(_pack_small(small, riders), (8, _SMALL_ROWS, LANES))],
        "all_to_all_grads")

    g_out, d_out, m_out, v_out = {}, {}, {}, {}
    res = _adamw(r_small, _pack_small(w), _pack_small(m), _pack_small(v), _SMALL_ROWS, "adamw_gains")
    for dst, packed in zip((g_out, d_out, m_out, v_out), res):
        dst.update(_unpack_small(packed, shapes))
    summed = res[0].reshape(-1)
    loss = summed[_SMALL_LEN]
    is_last = (4 * lax.axis_index("x") + 2 * lax.axis_index("y") + lax.axis_index("c") == 7).astype(F32)
    dw_f_sum = summed[_SMALL_LEN + 1:_SMALL_LEN + 1 + D_MODEL * N_HEADS].reshape(D_MODEL, N_HEADS) * is_last

    big = ("w_in", "w_o", "w_up", "w_down", "w_conv", "w_ffn_conv")
    recv = [r_in, r_o, r_up, r_dn, r_conv, r_fc]
    rows = dict(w_in=256, w_o=128, w_up=256, w_down=176, w_conv=3, w_ffn_conv=3)
    for a, k in enumerate(big):
        res = _adamw(recv[a], w[k], m[k], v[k], rows[k], "adamw_" + k, tail=dw_f_sum if k == "w_in" else None)
        g_out[k], d_out[k], m_out[k], v_out[k] = [r.reshape(shapes[k]) for r in res]

    order = ("g_mix", "w_in", "b_f", "w_conv", "g_conv_out", "g_attn_out", "w_o", "g_ffn", "w_up", "w_ffn_conv",
             "w_down", "g_final")
    return (loss, gx.reshape(x.shape), *[g_out[k] for k in order], *[d_out[k] for k in order],
            *[m_out[k] for k in order], *[v_out[k] for k in order])
```

```python
import jax
import jax.numpy as jnp
import numpy as np
from jax import lax
from jax.experimental import pallas as pl
from jax.experimental.pallas import tpu as pltpu

F32 = jnp.float32
BF = jnp.bfloat16
HI = lax.Precision.HIGHEST
MESH = pl.DeviceIdType.MESH

D_MODEL = 1024
CONV_CH = 512
ATTN_W = 512
N_HEADS = 8
HEAD_DIM = 64
D_FF = 2816
FF_CHUNK = 256
N_FF_CHUNKS = D_FF // FF_CHUNK
EPS = 1e-6
NEG = -1e30
LANES = 128
VMEM_BIG = 56 * 1024 * 1024
VMEM_MID = 40 * 1024 * 1024

ADAM_LR = 0.001
ADAM_B1 = 0.9
ADAM_B2 = 0.999
ADAM_EPS = 1e-08
ADAM_WD = 0.01
ADAM_STEP = 10

NT = (((1,), (1,)), ((), ()))
TN = (((0,), (0,)), ((), ()))


def _dot(a, b):
    return jnp.dot(a, b, preferred_element_type=F32)


def _dot_nt(a, b):
    return lax.dot_general(a, b, NT, preferred_element_type=F32)


def _dot_tn(a, b):
    return lax.dot_general(a, b, TN, preferred_element_type=F32)


def _params(vmem):
    return pltpu.CompilerParams(vmem_limit_bytes=vmem)


def _rows(shape):
    return lax.broadcasted_iota(jnp.int32, shape, 0)


def _cols(shape):
    return lax.broadcasted_iota(jnp.int32, shape, 1)


def _shift_down(u, prev, k):
    n = prev.shape[0]
    out = pltpu.roll(u, k, 0)
    row = _rows(u.shape)
    for r in range(k):
        out = jnp.where(row == r, prev[n - k + r:n - k + r + 1, :].astype(u.dtype), out)
    return out


def _shift_up(u, nxt, k):
    tm = u.shape[0]
    out = pltpu.roll(u, tm - k, 0)
    row = _rows(u.shape)
    for r in range(k):
        out = jnp.where(row == tm - k + r, nxt[r:r + 1, :], out)
    return out


def _conv3(u, prev, w):
    u1 = _shift_down(u, prev, 1)
    u2 = _shift_down(u, prev, 2)
    return w[0:1, :] * u2 + w[1:2, :] * u1 + w[2:3, :] * u, u1, u2


def _conv3_bwd(d, nxt, w):
    return w[2:3, :] * d + w[1:2, :] * _shift_up(d, nxt, 1) + w[0:1, :] * _shift_up(d, nxt, 2)


def _rms(x):
    return lax.rsqrt(jnp.mean(x * x, axis=-1, keepdims=True) + EPS)


def _rms_bwd(x, r, dyg):
    return r * dyg - x * (r * r * r) * jnp.mean(dyg * x, axis=-1, keepdims=True)


def _full(shape):
    nd = len(shape)
    return pl.BlockSpec(shape, lambda i, _n=nd: (0,) * _n)


ONES_LANE = 24


def _bias_scatter():
    sc = np.zeros((LANES, 2 * N_HEADS * LANES), np.float32)
    koff = N_HEADS * LANES
    for h in range(N_HEADS):
        aux = HEAD_DIM * (1 - h % 2)
        for j in range(3):
            sc[8 * j + h, LANES * h + aux + j] = 1.0
            sc[ONES_LANE, koff + LANES * h + aux + j] = 1.0
            sc[ONES_LANE, LANES * h + aux + 3 + j] = 1.0
            sc[8 * j + h, koff + LANES * h + aux + 3 + j] = -1.0
    return jnp.asarray(sc, BF)


def _split_parts(v, one):
    hi = v.astype(BF).astype(F32)
    rest = v - hi
    mid = rest.astype(BF).astype(F32)
    lo = (rest - mid).astype(BF).astype(F32)
    parts = hi + pltpu.roll(mid, 8, 1) + pltpu.roll(lo, 16, 1)
    return jnp.where(_cols(parts.shape) == ONES_LANE, one, parts).astype(BF)


def _fwd_in(x, g_mix, w_zc, w_qkv, w_f, b_f, w_conv, g_conv_out, tm):
    s = x.shape[0]
    nb = s // tm
    aw = N_HEADS * LANES

    def body(x_ref, gm_ref, wzc_ref, wqkv_ref, wf_ref, bf_ref, wc_ref, gco_ref, sc_ref,
             h1_ref, zc_ref, qkv_ref, fpre_ref, fcol_ref, nc_ref, qa_ref, ka_ref, qt_ref, vt_ref, cu_ref, cf_ref):
        i = pl.program_id(0)

        @pl.when(i == 0)
        def _():
            cu_ref[...] = jnp.zeros_like(cu_ref)
            cf_ref[...] = jnp.zeros_like(cf_ref)

        xv = x_ref[...]
        hb = (xv * _rms(xv) * gm_ref[...]).astype(BF)
        h1_ref[...] = hb
        zc = _dot(hb, wzc_ref[...])
        zc_ref[...] = zc
        qkv = _dot(hb, wqkv_ref[...])
        qkv = jnp.where(_cols(qkv.shape) < ATTN_W, qkv * 0.125, qkv)
        qkvb = qkv.astype(BF)
        qkv_ref[...] = qkvb
        qt_ref[0] = qkv[:, :ATTN_W].T.astype(BF)
        vt_ref[0] = qkv[:, 2 * ATTN_W:].T.astype(BF)

        gb, gc, xc = zc[:, :CONV_CH], zc[:, CONV_CH:2 * CONV_CH], zc[:, 2 * CONV_CH:]
        u = gc * xc
        cv, _, _ = _conv3(u, cu_ref[...], wc_ref[...])
        cu_ref[...] = u[tm - 8:, :]
        y = gb * cv
        nc_ref[...] = (y * _rms(y) * gco_ref[...]).astype(BF)

        fpre = _dot(hb, wf_ref[...]) + bf_ref[...]
        fpre_ref[...] = fpre
        logf = jnp.minimum(fpre, 0.0) - jnp.log1p(jnp.exp(-jnp.abs(fpre)))
        logf = jnp.where(_cols(logf.shape) < N_HEADS, logf, 0.0)
        tri = (_rows((tm, tm)) >= _cols((tm, tm))).astype(F32)
        fcol = jnp.dot(tri, logf, precision=HI, preferred_element_type=F32) + cf_ref[...]
        cf_ref[...] = fcol[tm - 1:tm, :]
        fcol_ref[...] = fcol

        feat = _dot(_split_parts(fcol, 1.0), sc_ref[...]).astype(BF)
        lane = _cols((tm, LANES))
        for h in range(N_HEADS):
            hp, hh = divmod(h, 2)
            own = (lane >= HEAD_DIM * hh) & (lane < HEAD_DIM * (hh + 1))
            hs = slice(LANES * h, LANES * (h + 1))
            qa_ref[:, hs] = jnp.where(own, qkvb[:, LANES * hp:LANES * (hp + 1)], feat[:, hs])
            ka_ref[:, hs] = jnp.where(own, qkvb[:, ATTN_W + LANES * hp:ATTN_W + LANES * (hp + 1)],
                                      feat[:, aw + LANES * h:aw + LANES * (h + 1)])

    blk = lambda c: pl.BlockSpec((tm, c), lambda i: (i, 0))
    return pl.pallas_call(
        body, name="fwd_in", grid=(nb,),
        in_specs=[blk(D_MODEL), _full((1, D_MODEL)), _full(w_zc.shape), _full(w_qkv.shape), _full(w_f.shape),
                  _full((1, LANES)), _full((8, CONV_CH)), _full((1, CONV_CH)), _full((LANES, 2 * aw))],
        out_specs=[blk(D_MODEL), blk(3 * CONV_CH), blk(3 * ATTN_W), blk(LANES), blk(LANES),
                   blk(CONV_CH), blk(aw), blk(aw)] + [pl.BlockSpec((1, ATTN_W, tm), lambda i: (i, 0, 0))] * 2,
        out_shape=[jax.ShapeDtypeStruct((s, D_MODEL), BF), jax.ShapeDtypeStruct((s, 3 * CONV_CH), F32),
                   jax.ShapeDtypeStruct((s, 3 * ATTN_W), BF), jax.ShapeDtypeStruct((s, LANES), F32),
                   jax.ShapeDtypeStruct((s, LANES), F32),
                   jax.ShapeDtypeStruct((s, CONV_CH), BF), jax.ShapeDtypeStruct((s, aw), BF),
                   jax.ShapeDtypeStruct((s, aw), BF)] + [jax.ShapeDtypeStruct((nb, ATTN_W, tm), BF)] * 2,
        scratch_shapes=[pltpu.VMEM((8, CONV_CH), F32), pltpu.VMEM((1, LANES), F32)],
        compiler_params=_params(VMEM_MID),
    )(x, g_mix, w_zc, w_qkv, w_f, b_f, w_conv, g_conv_out, _bias_scatter())


def _pipeline_masked_last(last, produce, consume, buf_a, buf_b):
    produce(0, buf_a)

    def two_blocks(j, carry):
        blk = 2 * j
        produce(blk + 1, buf_b)
        consume(blk, buf_a, False)
        produce(blk + 2, buf_a)
        consume(blk + 1, buf_b, False)
        return carry

    lax.fori_loop(0, last // 2, two_blocks, 0)

    @pl.when(last % 2 == 0)
    def _():
        consume(last, buf_a, True)

    @pl.when(last % 2 == 1)
    def _():
        produce(last, buf_b)
        consume(last - 1, buf_a, False)
        consume(last, buf_b, True)


def _attn_fwd(qa, ka, vt, t, shards):
    s = qa.shape[0]
    nb = s // t
    n = len(shards)

    def body(qa_ref, ka_any, vt_any, *refs):
        shard_refs, (o_ref, l_ref), gathered = refs[:n], refs[n:n + 2], refs[n + 2:2 * n + 2]
        ka_scr, vt_scr, m_scr, acc_scr, ea_scr, eb_scr, sem = refs[2 * n + 2:2 * n + 9]
        i = pl.program_id(0)
        start, forward, finish = _gather_phases(shard_refs, gathered, *refs[2 * n + 9:])
        pl.when(i == 0)(start)
        pl.when(i == nb // 2)(forward)
        _load_weights(i, [(ka_any, ka_scr), (vt_any, vt_scr)], sem)

        causal_t = _rows((t, t)) <= _cols((t, t))
        row = _rows((LANES, t))
        lrows = jnp.zeros((LANES, t), F32)
        for hp in range(N_HEADS // 2):
            ps = slice(hp * LANES, (hp + 1) * LANES)
            m_scr[...] = jnp.full(m_scr.shape, NEG, F32)
            acc_scr[...] = jnp.zeros(acc_scr.shape, F32)
            heads = [dict(hh=hh, hs=slice((2 * hp + hh) * LANES, (2 * hp + hh + 1) * LANES),
                          aux=HEAD_DIM * (1 - hh), own=(row >= HEAD_DIM * hh) & (row < HEAD_DIM * (hh + 1)),
                          fill=jnp.where(row == HEAD_DIM * (1 - hh), 1.0, 0.0).astype(BF),
                          qh=qa_ref[:, (2 * hp + hh) * LANES:(2 * hp + hh + 1) * LANES]) for hh in range(2)]

            def scores(kb, dst, heads=heads):
                rs = pl.ds(pl.multiple_of(kb * t, t), t)
                for hd in heads:
                    dst[hd["hh"]] = _dot_nt(ka_scr[rs, hd["hs"]], hd["qh"])

            def consume(kb, src, masked, ps=ps, heads=heads):
                vt2 = vt_scr[kb, ps, :]
                m_olds = [m_scr[hd["hh"]:hd["hh"] + 1, :] for hd in heads]
                accs = [acc_scr[hd["hh"]] for hd in heads]
                m_news, acc_news = [], []
                for hd, m_old, acc in zip(heads, m_olds, accs):
                    e = src[hd["hh"]]
                    if masked:
                        e = jnp.where(causal_t, e, NEG)
                    m_new = jnp.maximum(m_old, jnp.max(e, axis=0, keepdims=True))
                    p = jnp.exp(e - m_new).astype(BF)
                    vta = jnp.where(hd["own"], vt2, hd["fill"])
                    acc_news.append(jnp.exp(m_old - m_new) * acc + _dot(vta, p))
                    m_news.append(m_new)
                for hd, m_new, acc in zip(heads, m_news, acc_news):
                    acc_scr[hd["hh"]] = acc
                    m_scr[hd["hh"]:hd["hh"] + 1, :] = m_new

            _pipeline_masked_last(i, scores, consume, ea_scr, eb_scr)

            o_pair = None
            for hd in heads:
                hh = hd["hh"]
                acc = acc_scr[hh]
                denom = acc[hd["aux"]:hd["aux"] + 1, :]
                o_h = acc / denom
                lrows = jnp.where(row == 2 * hp + hh, m_scr[hh:hh + 1, :] + jnp.log(denom), lrows)
                o_pair = o_h if hh == 0 else jnp.where(row < HEAD_DIM, o_pair, o_h)
            o_ref[:, ps] = o_pair.T
        l_ref[...] = lrows.T
        pl.when(i == nb - 1)(finish)

    any_ = pl.BlockSpec(memory_space=pl.ANY)
    return pl.pallas_call(
        body, name="attn_fwd", grid=(nb,),
        in_specs=[pl.BlockSpec((t, N_HEADS * LANES), lambda i: (i, 0)), any_, any_] + [any_] * n,
        out_specs=[pl.BlockSpec((t, ATTN_W), lambda i: (i, 0)), pl.BlockSpec((t, LANES), lambda i: (i, 0))]
        + [any_] * n,
        out_shape=[jax.ShapeDtypeStruct((s, ATTN_W), F32), jax.ShapeDtypeStruct((s, LANES), F32)]
        + _gathered_shapes(shards),
        scratch_shapes=[pltpu.VMEM(ka.shape, BF), pltpu.VMEM(vt.shape, BF), pltpu.VMEM((2, t), F32),
                        pltpu.VMEM((2, LANES, t), F32), pltpu.VMEM((2, t, t), F32), pltpu.VMEM((2, t, t), F32),
                        pltpu.SemaphoreType.DMA((2,))] + _exchange_sems(n),
        compiler_params=_params(VMEM_BIG),
    )(qa, ka, vt, *shards)


def _fwd_out(x, nc, o, g_attn_out, w_o, g_ffn, tm):
    s = x.shape[0]

    def body(x_ref, nc_ref, o_ref, ga_ref, wo_ref, gf_ref, x2_ref, h2_ref, mix_ref):
        ov = o_ref[...]
        na = (ov * _rms(ov) * ga_ref[...]).astype(BF)
        ncv = nc_ref[...]
        mix_ref[:, :CONV_CH] = ncv
        mix_ref[:, CONV_CH:] = na
        x2 = x_ref[...] + _dot(ncv, wo_ref[:CONV_CH, :]) + _dot(na, wo_ref[CONV_CH:, :])
        x2_ref[...] = x2
        h2_ref[...] = (x2 * _rms(x2) * gf_ref[...]).astype(BF)

    blk = lambda c: pl.BlockSpec((tm, c), lambda i: (i, 0))
    return pl.pallas_call(
        body, name="fwd_out", grid=(s // tm,),
        in_specs=[blk(D_MODEL), blk(CONV_CH), blk(ATTN_W), _full((1, ATTN_W)), _full(w_o.shape), _full((1, D_MODEL))],
        out_specs=[blk(D_MODEL), blk(D_MODEL), blk(D_MODEL)],
        out_shape=[jax.ShapeDtypeStruct((s, D_MODEL), F32), jax.ShapeDtypeStruct((s, D_MODEL), BF),
                   jax.ShapeDtypeStruct((s, D_MODEL), BF)],
        compiler_params=_params(VMEM_MID),
    )(x, nc, o, g_attn_out, w_o, g_ffn)


def _load_weights(i, pairs, sem):
    @pl.when(i == 0)
    def _():
        cps = [pltpu.make_async_copy(src, dst, sem.at[n]) for n, (src, dst) in enumerate(pairs)]
        for cp in cps:
            cp.start()
        for cp in cps:
            cp.wait()


def _ff_cols(j):
    return (slice(j * FF_CHUNK, (j + 1) * FF_CHUNK), slice(D_FF + j * FF_CHUNK, D_FF + (j + 1) * FF_CHUNK))


def _fwd_ffn(h2, x2, tgt, w_up, w_ffn_conv, w_dn, g_final, tm):
    s = x2.shape[0]

    def body(h2_ref, x2_ref, tgt_ref, wfc_ref, gfin_ref, wup_any, wdn_any,
             u0_ref, uc_ref, act_ref, dx3_ref, dx3b_ref, loss_ref, dgfin_ref,
             wup, wdn, carry, sem):
        i = pl.program_id(0)
        _load_weights(i, [(wup_any, wup), (wdn_any, wdn)], sem)

        @pl.when(i == 0)
        def _():
            carry[...] = jnp.zeros_like(carry)
            loss_ref[...] = jnp.zeros_like(loss_ref)
            dgfin_ref[...] = jnp.zeros_like(dgfin_ref)

        hb = h2_ref[...]

        def up(j):
            return [_dot(hb, wup[:, cc]) for cc in _ff_cols(j)]

        nxt = up(0)
        down = None
        for j in range(N_FF_CHUNKS):
            cur = nxt
            if j + 1 < N_FF_CHUNKS:
                nxt = up(j + 1)
            parts = []
            for cc, u0 in zip(_ff_cols(j), cur):
                u0_ref[:, cc] = u0.astype(BF)
                uu, _, _ = _conv3(u0, carry[:, cc], wfc_ref[:, cc])
                carry[:, cc] = u0[tm - 8:, :]
                uc_ref[:, cc] = uu.astype(BF)
                parts.append(uu)
            ua, ug = parts
            act = (ug * jax.nn.sigmoid(ug) * ua).astype(BF)
            ca = _ff_cols(j)[0]
            act_ref[:, ca] = act
            part = _dot(act, wdn[ca, :])
            down = part if down is None else down + part

        x3 = x2_ref[...] + down
        r3 = _rms(x3)
        gfin = gfin_ref[...]
        xn = x3 * r3
        diff = xn * gfin - tgt_ref[...]
        loss_ref[...] += jnp.sum(jnp.sum(diff * diff, axis=-1, keepdims=True), axis=0, keepdims=True) * (0.5 / D_MODEL)
        dy = diff * (1.0 / D_MODEL)
        dgfin_ref[...] += jnp.sum(dy * xn, axis=0, keepdims=True)
        dx3 = _rms_bwd(x3, r3, dy * gfin)
        dx3_ref[...] = dx3
        dx3b_ref[...] = dx3.astype(BF)

    blk = lambda c: pl.BlockSpec((tm, c), lambda i: (i, 0))
    any_ = pl.BlockSpec(memory_space=pl.ANY)
    return pl.pallas_call(
        body, name="fwd_ffn", grid=(s // tm,),
        in_specs=[blk(D_MODEL), blk(D_MODEL), blk(D_MODEL), _full((8, 2 * D_FF)), _full((1, D_MODEL)), any_, any_],
        out_specs=[blk(2 * D_FF), blk(2 * D_FF), blk(D_FF), blk(D_MODEL), blk(D_MODEL), _full((1, 1)),
                   _full((1, D_MODEL))],
        out_shape=[jax.ShapeDtypeStruct((s, 2 * D_FF), BF), jax.ShapeDtypeStruct((s, 2 * D_FF), BF),
                   jax.ShapeDtypeStruct((s, D_FF), BF),
                   jax.ShapeDtypeStruct((s, D_MODEL), F32), jax.ShapeDtypeStruct((s, D_MODEL), BF),
                   jax.ShapeDtypeStruct((1, 1), F32), jax.ShapeDtypeStruct((1, D_MODEL), F32)],
        scratch_shapes=[pltpu.VMEM(w_up.shape, BF), pltpu.VMEM(w_dn.shape, BF), pltpu.VMEM((8, 2 * D_FF), F32),
                        pltpu.SemaphoreType.DMA((2,))],
        compiler_params=_params(VMEM_BIG),
    )(h2, x2, tgt, w_ffn_conv, g_final, w_up, w_dn)


def _bwd_ffn(dx3, dx3b, u0, uc, x2, w_up, w_ffn_conv, w_dn, g_ffn, tm):
    s = x2.shape[0]
    nb = s // tm

    def body(dx3_ref, dx3b_ref, u0_ref, uc_ref, x2_ref, wfc_ref, gf_ref, wup_any, wdn_any,
             du0_ref, dx2_ref, dx2b_ref, dwfc_ref, dgf_ref,
             wup, wdn, carry, sem):
        i = pl.program_id(0)
        _load_weights(i, [(wup_any, wup), (wdn_any, wdn)], sem)

        @pl.when(i == 0)
        def _():
            carry[...] = jnp.zeros_like(carry)
            dwfc_ref[...] = jnp.zeros_like(dwfc_ref)
            dgf_ref[...] = jnp.zeros_like(dgf_ref)

        db = dx3b_ref[...]

        def dact_of(j):
            return _dot(db, wdn[:, _ff_cols(j)[0]])

        nxt = dact_of(0)
        dh2 = None
        for j in range(N_FF_CHUNKS):
            ca, cg = _ff_cols(j)
            dact = nxt
            if j + 1 < N_FF_CHUNKS:
                nxt = dact_of(j + 1)
            ua = uc_ref[:, ca].astype(F32)
            ug = uc_ref[:, cg].astype(F32)
            sg = jax.nn.sigmoid(ug)
            da = dact * (ug * sg)
            dg = dact * ua * (sg * (1.0 + ug * (1.0 - sg)))
            for cc, d in ((ca, da), (cg, dg)):
                nxt_rows = carry[:, cc]
                d1 = _shift_up(d, nxt_rows, 1)
                d2 = _shift_up(d, nxt_rows, 2)
                u0c = u0_ref[:, cc].astype(F32)
                w = wfc_ref[:, cc]
                dwfc_ref[0:1, cc] += jnp.sum(d2 * u0c, axis=0, keepdims=True)
                dwfc_ref[1:2, cc] += jnp.sum(d1 * u0c, axis=0, keepdims=True)
                dwfc_ref[2:3, cc] += jnp.sum(d * u0c, axis=0, keepdims=True)
                du0 = (w[2:3, :] * d + w[1:2, :] * d1 + w[0:1, :] * d2).astype(BF)
                carry[:, cc] = d[:8, :]
                du0_ref[:, cc] = du0
                part = _dot(du0, wup[cc, :])
                dh2 = part if dh2 is None else dh2 + part

        x2v = x2_ref[...]
        r2 = _rms(x2v)
        dgf_ref[...] += jnp.sum(dh2 * (x2v * r2), axis=0, keepdims=True)
        dx2 = dx3_ref[...] + _rms_bwd(x2v, r2, dh2 * gf_ref[...])
        dx2_ref[...] = dx2
        dx2b_ref[...] = dx2.astype(BF)

    blk = lambda c: pl.BlockSpec((tm, c), lambda i: (nb - 1 - i, 0))
    any_ = pl.BlockSpec(memory_space=pl.ANY)
    return pl.pallas_call(
        body, name="bwd_ffn", grid=(nb,),
        in_specs=[blk(D_MODEL), blk(D_MODEL), blk(2 * D_FF), blk(2 * D_FF), blk(D_MODEL), _full((8, 2 * D_FF)),
                  _full((1, D_MODEL)), any_, any_],
        out_specs=[blk(2 * D_FF), blk(D_MODEL), blk(D_MODEL), _full((8, 2 * D_FF)), _full((1, D_MODEL))],
        out_shape=[jax.ShapeDtypeStruct((s, 2 * D_FF), BF), jax.ShapeDtypeStruct((s, D_MODEL), F32),
                   jax.ShapeDtypeStruct((s, D_MODEL), BF), jax.ShapeDtypeStruct((8, 2 * D_FF), F32),
                   jax.ShapeDtypeStruct((1, D_MODEL), F32)],
        scratch_shapes=[pltpu.VMEM(w_up.shape, BF), pltpu.VMEM(w_dn.shape, BF), pltpu.VMEM((8, 2 * D_FF), F32),
                        pltpu.SemaphoreType.DMA((2,))],
        compiler_params=_params(VMEM_BIG),
    )(dx3, dx3b, u0, uc, x2, w_ffn_conv, g_ffn, w_up, w_dn)


def _bwd_out(dx2b, o, zc, fcol, lcol, qa, w_o, g_attn_out, g_conv_out, w_conv, tm):
    s = o.shape[0]
    nb = s // tm
    aw = N_HEADS * LANES

    def body(dx2b_ref, o_ref, zc_ref, halo_ref, fcol_ref, lcol_ref, qa_ref, wo_ref, ga_ref, gco_ref, wc_ref, sc_ref,
             dzc_ref, doa_ref, dot_ref, qab_ref, dwc_ref, dga_ref, dgco_ref, carry):
        i = pl.program_id(0)
        rb = nb - 1 - i

        @pl.when(i == 0)
        def _():
            carry[...] = jnp.zeros_like(carry)
            dwc_ref[...] = jnp.zeros_like(dwc_ref)
            dga_ref[...] = jnp.zeros_like(dga_ref)
            dgco_ref[...] = jnp.zeros_like(dgco_ref)

        dmix = _dot_nt(dx2b_ref[...], wo_ref[...])
        dnc, dna = dmix[:, :CONV_CH], dmix[:, CONV_CH:]

        ov = o_ref[...]
        ra = _rms(ov)
        dga_ref[...] += jnp.sum(dna * (ov * ra), axis=0, keepdims=True)
        do = _rms_bwd(ov, ra, dna * ga_ref[...])
        dob = do.astype(BF)
        dot_ref[0] = do.T.astype(BF)
        sel = (_rows((ATTN_W, LANES)) // HEAD_DIM == _cols((ATTN_W, LANES))).astype(F32)
        delta = jnp.dot(do * ov, sel, precision=HI, preferred_element_type=F32)
        featc = _dot(_split_parts(fcol_ref[...] - lcol_ref[...], 1.0), sc_ref[...]).astype(BF)
        featd = _dot(_split_parts(-delta, 0.0), sc_ref[...]).astype(BF)
        lane = _cols((tm, LANES))
        for h in range(N_HEADS):
            hp, hh = divmod(h, 2)
            own = (lane >= HEAD_DIM * hh) & (lane < HEAD_DIM * (hh + 1))
            hs = slice(LANES * h, LANES * (h + 1))
            qab_ref[:, hs] = jnp.where(own, qa_ref[:, hs], featc[:, hs])
            doa_ref[:, hs] = jnp.where(own, dob[:, LANES * hp:LANES * (hp + 1)], featd[:, hs])

        zc_v = zc_ref[...]
        gb, gc, xc = zc_v[:, :CONV_CH], zc_v[:, CONV_CH:2 * CONV_CH], zc_v[:, 2 * CONV_CH:]
        hal = halo_ref[...] * (rb > 0).astype(F32)
        u = gc * xc
        prev = hal[:, CONV_CH:2 * CONV_CH] * hal[:, 2 * CONV_CH:]
        wc = wc_ref[...]
        cv, u1, u2 = _conv3(u, prev, wc)
        y = gb * cv
        rc = _rms(y)
        dgco_ref[...] += jnp.sum(dnc * (y * rc), axis=0, keepdims=True)
        dy = _rms_bwd(y, rc, dnc * gco_ref[...])
        dcv = dy * gb
        dwc_ref[0:1, :] += jnp.sum(dcv * u2, axis=0, keepdims=True)
        dwc_ref[1:2, :] += jnp.sum(dcv * u1, axis=0, keepdims=True)
        dwc_ref[2:3, :] += jnp.sum(dcv * u, axis=0, keepdims=True)
        du = _conv3_bwd(dcv, carry[...], wc)
        carry[...] = dcv[:8, :]
        dzc_ref[:, :CONV_CH] = (dy * cv).astype(BF)
        dzc_ref[:, CONV_CH:2 * CONV_CH] = (du * xc).astype(BF)
        dzc_ref[:, 2 * CONV_CH:] = (du * gc).astype(BF)

    blk = lambda c: pl.BlockSpec((tm, c), lambda i: (nb - 1 - i, 0))
    halo = pl.BlockSpec((8, 3 * CONV_CH), lambda i: (jnp.maximum((nb - 1 - i) * (tm // 8) - 1, 0), 0))
    tr = pl.BlockSpec((1, ATTN_W, tm), lambda i: (nb - 1 - i, 0, 0))
    return pl.pallas_call(
        body, name="bwd_out", grid=(nb,),
        in_specs=[blk(D_MODEL), blk(ATTN_W), blk(3 * CONV_CH), halo, blk(LANES), blk(LANES), blk(aw),
                  _full(w_o.shape), _full((1, ATTN_W)), _full((1, CONV_CH)), _full((8, CONV_CH)), _full((LANES, aw))],
        out_specs=[blk(3 * CONV_CH), blk(aw), tr, blk(aw),
                   _full((8, CONV_CH)), _full((1, ATTN_W)), _full((1, CONV_CH))],
        out_shape=[jax.ShapeDtypeStruct((s, 3 * CONV_CH), BF),
                   jax.ShapeDtypeStruct((s, aw), BF), jax.ShapeDtypeStruct((nb, ATTN_W, tm), BF),
                   jax.ShapeDtypeStruct((s, aw), BF),
                   jax.ShapeDtypeStruct((8, CONV_CH), F32), jax.ShapeDtypeStruct((1, ATTN_W), F32),
                   jax.ShapeDtypeStruct((1, CONV_CH), F32)],
        scratch_shapes=[pltpu.VMEM((8, CONV_CH), F32)],
        compiler_params=_params(VMEM_MID),
    )(dx2b, o, zc, zc, fcol, lcol, qa, w_o, g_attn_out, g_conv_out, w_conv, _bias_scatter()[:, :aw])


def _attn_bwd(qab, doa, ka, qkv, qt, dot, t, blocks):
    s = qab.shape[0]
    nb = s // t
    npair = N_HEADS // 2
    n = len(blocks)

    def body(ka_ref, v_ref, qab_ref, doa_ref, qt_ref, dot_ref, *refs):
        block_refs, (dk_ref, dv_ref, dfk_ref, dqa_ref, dq_ref) = refs[:n], refs[n:n + 5]
        received = refs[n + 5:2 * n + 5]
        acck, accv = refs[2 * n + 5:2 * n + 7]
        hp = pl.program_id(0)
        j = pl.program_id(1)
        start, finish = _scatter_phases(block_refs, received, *refs[2 * n + 7:])
        pl.when((hp == 0) & (j == 0))(start)

        @pl.when(j == 0)
        def _():
            dqa_ref[...] = jnp.zeros_like(dqa_ref)

        causal = _cols((t, t)) <= _rows((t, t))
        row = _rows((LANES, t))
        lane = _cols((t, LANES))
        acck[...] = jnp.zeros(acck.shape, F32)
        accv[...] = jnp.zeros(accv.shape, F32)
        v2 = v_ref[...]
        heads = []
        for hh in range(2):
            aux = HEAD_DIM * (1 - hh)
            ownl = (lane >= HEAD_DIM * hh) & (lane < HEAD_DIM * (hh + 1))
            ones3 = jnp.where((lane >= aux) & (lane < aux + 3), 1.0, 0.0).astype(BF)
            heads.append(dict(hh=hh, hs=slice(hh * LANES, (hh + 1) * LANES), aux=aux,
                              own=(row >= HEAD_DIM * hh) & (row < HEAD_DIM * (hh + 1)),
                              fill=jnp.where(row == aux, 1.0, 0.0).astype(BF),
                              kh=ka_ref[:, hh * LANES:(hh + 1) * LANES], vh=jnp.where(ownl, v2, ones3)))

        def tile(qb, masked):
            rs = pl.ds(pl.multiple_of(qb * t, t), t)
            qt2 = qt_ref[qb]
            dot2 = dot_ref[qb]
            es = [_dot_nt(qab_ref[rs, hd["hs"]], hd["kh"]) for hd in heads]
            dps = [_dot_nt(doa_ref[rs, hd["hs"]], hd["vh"]) for hd in heads]
            aks = [acck[hd["hh"]] for hd in heads]
            avs = [accv[hd["hh"]] for hd in heads]
            dqs = [dqa_ref[rs, hd["hs"]] for hd in heads]
            outs = []
            for hd, e, dp, ak, av, dq in zip(heads, es, dps, aks, avs, dqs):
                if masked:
                    e = jnp.where(causal, e, NEG)
                p = jnp.exp(e)
                ds = (p * dp).astype(BF)
                outs.append((ak + _dot(jnp.where(hd["own"], qt2, hd["fill"]), ds), av + _dot(dot2, p.astype(BF)),
                             dq + _dot(ds, hd["kh"])))
            for hd, (ak, av, dq) in zip(heads, outs):
                acck[hd["hh"]] = ak
                accv[hd["hh"]] = av
                dqa_ref[rs, hd["hs"]] = dq

        def loop_body(qb, carry):
            tile(qb, False)
            return carry

        tile(j, True)
        lax.fori_loop(j + 1, nb, loop_body, 0)
        dk_pair = dv_pair = None
        dfrows = jnp.zeros((LANES, t), F32)
        for hd in heads:
            ak, av = acck[hd["hh"]], accv[hd["hh"]]
            dfrows = jnp.where(row == 2 * hp + hd["hh"], -ak[hd["aux"]:hd["aux"] + 1, :], dfrows)
            dk_pair = ak if hd["hh"] == 0 else jnp.where(row < HEAD_DIM, dk_pair, ak)
            dv_pair = av if hd["hh"] == 0 else jnp.where(row < HEAD_DIM, dv_pair, av)
        dk_ref[...] = dk_pair.T.astype(BF)
        dv_ref[...] = dv_pair.T.astype(BF)
        dfk_ref[0] = dfrows.T

        @pl.when(j == nb - 1)
        def _():
            for r in range(nb):
                rows = slice(r * t, (r + 1) * t)
                dq_ref[rows, :] = (jnp.where(lane < HEAD_DIM, dqa_ref[rows, :LANES], dqa_ref[rows, LANES:])
                                   * 0.125).astype(BF)

        pl.when((hp == npair - 1) & (j == nb - 1))(finish)

    pair_cols = pl.BlockSpec((s, 2 * LANES), lambda hp, j: (0, hp))
    pair_rows = pl.BlockSpec((nb, LANES, t), lambda hp, j: (0, hp, 0))
    any_ = pl.BlockSpec(memory_space=pl.ANY)
    return pl.pallas_call(
        body, name="attn_bwd", grid=(npair, nb),
        in_specs=[pl.BlockSpec((t, 2 * LANES), lambda hp, j: (j, hp)),
                  pl.BlockSpec((t, LANES), lambda hp, j: (j, 2 * npair + hp)),
                  pair_cols, pair_cols, pair_rows, pair_rows] + [any_] * n,
        out_specs=[pl.BlockSpec((t, LANES), lambda hp, j: (j, hp)), pl.BlockSpec((t, LANES), lambda hp, j: (j, hp)),
                   pl.BlockSpec((1, t, LANES), lambda hp, j: (hp, j, 0)), pair_cols,
                   pl.BlockSpec((s, LANES), lambda hp, j: (0, hp))] + [any_] * n,
        out_shape=[jax.ShapeDtypeStruct((s, ATTN_W), BF), jax.ShapeDtypeStruct((s, ATTN_W), BF),
                   jax.ShapeDtypeStruct((npair, s, LANES), F32), jax.ShapeDtypeStruct((s, N_HEADS * LANES), F32),
                   jax.ShapeDtypeStruct((s, ATTN_W), BF)]
        + [jax.ShapeDtypeStruct(b.shape, b.dtype) for b in blocks],
        scratch_shapes=[pltpu.VMEM((2, LANES, t), F32), pltpu.VMEM((2, LANES, t), F32)] + _exchange_sems(n),
        compiler_params=_params(VMEM_BIG),
    )(ka, qkv, qab, doa, qt, dot, *blocks)


def _bwd_in(x, dx2, dzc, dq, dqa, dk, dv, dfk, fpre, w_zc, w_qkv, w_f, g_mix, tm, blocks):
    s = x.shape[0]
    nb = s // tm
    n = len(blocks)

    def body(x_ref, dx2_ref, dzc_ref, dq_ref, dqa_ref, dk_ref, dv_ref, dfk_ref, fpre_ref, wzc_ref, wqkv_ref, wf_ref,
             gm_ref, *refs):
        block_refs, (gx_ref, dfb_ref, dgm_ref, dbf_ref), received = refs[:n], refs[n:n + 4], refs[n + 4:2 * n + 4]
        carry = refs[2 * n + 4]
        i = pl.program_id(0)
        start, finish = _scatter_phases(block_refs, received, *refs[2 * n + 5:])
        pl.when(i == 0)(start)

        @pl.when(i == 0)
        def _():
            carry[...] = jnp.zeros_like(carry)
            dgm_ref[...] = jnp.zeros_like(dgm_ref)
            dbf_ref[...] = jnp.zeros_like(dbf_ref)

        lane = _cols((tm, LANES))
        dfq = jnp.zeros((tm, LANES), F32)
        for h in range(N_HEADS):
            aux = LANES * h + HEAD_DIM * (1 - h % 2)
            dfq = jnp.where(lane == h, dqa_ref[:, aux:aux + 1], dfq)

        triu = (_rows((tm, tm)) <= _cols((tm, tm))).astype(F32)
        df_cum = dfq + ((dfk_ref[0] + dfk_ref[1]) + (dfk_ref[2] + dfk_ref[3]))
        dlogf = jnp.dot(triu, df_cum, precision=HI, preferred_element_type=F32) + carry[...]
        carry[...] = dlogf[0:1, :]
        fpre = fpre_ref[...]
        df = jnp.where(_cols(fpre.shape) < N_HEADS, dlogf / (1.0 + jnp.exp(fpre)), 0.0)
        dbf_ref[...] += jnp.sum(df, axis=0, keepdims=True)
        dfb = df.astype(BF)
        dfb_ref[...] = dfb

        dh1 = _dot_nt(dzc_ref[...], wzc_ref[...])
        dh1 += _dot_nt(dq_ref[...], wqkv_ref[:, :ATTN_W])
        dh1 += _dot_nt(dk_ref[...], wqkv_ref[:, ATTN_W:2 * ATTN_W])
        dh1 += _dot_nt(dv_ref[...], wqkv_ref[:, 2 * ATTN_W:])
        dh1 += _dot_nt(dfb, wf_ref[...])
        xv = x_ref[...]
        r1 = _rms(xv)
        dgm_ref[...] += jnp.sum(dh1 * (xv * r1), axis=0, keepdims=True)
        gx_ref[...] = dx2_ref[...] + _rms_bwd(xv, r1, dh1 * gm_ref[...])
        pl.when(i == nb - 1)(finish)

    blk = lambda c: pl.BlockSpec((tm, c), lambda i: (nb - 1 - i, 0))
    any_ = pl.BlockSpec(memory_space=pl.ANY)
    return pl.pallas_call(
        body, name="bwd_in", grid=(nb,),
        in_specs=[blk(D_MODEL), blk(D_MODEL), blk(3 * CONV_CH), blk(ATTN_W), blk(N_HEADS * LANES), blk(ATTN_W),
                  blk(ATTN_W), pl.BlockSpec((N_HEADS // 2, tm, LANES), lambda i: (0, nb - 1 - i, 0)), blk(LANES),
                  _full(w_zc.shape), _full(w_qkv.shape), _full(w_f.shape), _full((1, D_MODEL))] + [any_] * n,
        out_specs=[blk(D_MODEL), blk(LANES), _full((1, D_MODEL)), _full((1, LANES))] + [any_] * n,
        out_shape=[jax.ShapeDtypeStruct((s, D_MODEL), F32), jax.ShapeDtypeStruct((s, LANES), BF),
                   jax.ShapeDtypeStruct((1, D_MODEL), F32), jax.ShapeDtypeStruct((1, LANES), F32)]
        + [jax.ShapeDtypeStruct(b.shape, b.dtype) for b in blocks],
        scratch_shapes=[pltpu.VMEM((1, LANES), F32)] + _exchange_sems(n),
        compiler_params=_params(VMEM_MID),
    )(x, dx2, dzc, dq, dqa, dk, dv, dfk, fpre, w_zc, w_qkv, w_f, g_mix, *blocks)


def _matmul_tn(a, b, bn, bk, name):
    s, m = a.shape
    n = b.shape[1]

    def body(a_ref, b_ref, o_ref):
        @pl.when(pl.program_id(1) == 0)
        def _():
            o_ref[...] = jnp.zeros_like(o_ref)

        o_ref[...] += _dot_tn(a_ref[...], b_ref[...])

    return pl.pallas_call(
        body, name=name, grid=(n // bn, s // bk),
        in_specs=[pl.BlockSpec((bk, m), lambda jn, k: (k, 0)), pl.BlockSpec((bk, bn), lambda jn, k: (k, jn))],
        out_specs=pl.BlockSpec((m, bn), lambda jn, k: (0, jn)),
        out_shape=jax.ShapeDtypeStruct((m, n), F32),
        compiler_params=_params(VMEM_MID),
    )(a, b)


def _matmul_tn_multi(a, bs, bk, name):
    s, m = a.shape
    nb = len(bs)

    def body(a_ref, *refs):
        b_refs, o_refs = refs[:nb], refs[nb:]

        @pl.when(pl.program_id(0) == 0)
        def _():
            for o_ref in o_refs:
                o_ref[...] = jnp.zeros_like(o_ref)

        at = a_ref[...].T
        for b_ref, o_ref in zip(b_refs, o_refs):
            o_ref[...] += _dot(at, b_ref[...])

    return pl.pallas_call(
        body, name=name, grid=(s // bk,),
        in_specs=[pl.BlockSpec((bk, m), lambda k: (k, 0))] + [pl.BlockSpec((bk, b.shape[1]), lambda k: (k, 0)) for b in bs],
        out_specs=[_full((m, b.shape[1])) for b in bs],
        out_shape=[jax.ShapeDtypeStruct((m, b.shape[1]), F32) for b in bs],
        compiler_params=_params(VMEM_BIG),
    )(a, *bs)


def _flip(v, bit):
    return 1 - v if bit else v


def _all_gather(shards):
    n = len(shards)

    def body(*refs):
        start, forward, finish = _gather_phases(refs[:n], refs[n:2 * n], *refs[2 * n:])
        start()
        forward()
        finish()

    any_ = pl.BlockSpec(memory_space=pl.ANY)
    return pl.pallas_call(
        body, name="all_gather_weights",
        in_specs=[any_] * n, out_specs=[any_] * n,
        out_shape=_gathered_shapes(shards), scratch_shapes=_exchange_sems(n),
    )(*shards)


def _gathered_shapes(shards):
    return [jax.ShapeDtypeStruct((8,) + sh.shape, sh.dtype) for sh in shards]


def _exchange_sems(n):
    return [pltpu.SemaphoreType.DMA((7 * n,)), pltpu.SemaphoreType.DMA((7 * n,)), pltpu.SemaphoreType.DMA((n,))]


def _gather_phases(src, out, send_sems, recv_sems, loc_sems):
    n = len(src)
    x, y, c = lax.axis_index("x"), lax.axis_index("y"), lax.axis_index("c")
    me, sibling = (x, y, c), (x, y, 1 - c)
    chips = [(1 - x, y), (x, 1 - y), (1 - x, 1 - y)]

    def slot(a, px, py, pc):
        return out[a].at[4 * px + 2 * py + pc]

    def copy(a, k, block, to, from_src=False):
        return pltpu.make_async_remote_copy(
            src_ref=src[a] if from_src else slot(a, *block), dst_ref=slot(a, *block),
            send_sem=send_sems.at[7 * a + k], recv_sem=recv_sems.at[7 * a + k],
            device_id=to, device_id_type=MESH)

    def local(a):
        return pltpu.make_async_copy(src[a], slot(a, *me), loc_sems.at[a])

    def first(a):
        return [copy(a, 0, me, sibling, True)] + [copy(a, 1 + j, me, (*chip, c), True)
                                                  for j, chip in enumerate(chips)]

    def passed(a, j):
        return copy(a, 4 + j, (*chips[j], c), sibling)

    def start():
        for a in range(n):
            local(a).start()
            for cp in first(a):
                cp.start()

    def forward():
        for a in range(n):
            for j, chip in enumerate(chips):
                copy(a, 1 + j, (*chip, c), me).wait_recv()
                passed(a, j).start()

    def finish():
        for a in range(n):
            copy(a, 0, sibling, me).wait_recv()
            for j, chip in enumerate(chips):
                copy(a, 4 + j, (*chip, 1 - c), me).wait_recv()
        for a in range(n):
            for cp in first(a) + [passed(a, j) for j in range(3)]:
                cp.wait_send()
            local(a).wait()

    return start, forward, finish


def _scatter_phases(src, out, send_sems, recv_sems, loc_sems):
    n = len(src)
    masks = [((k >> 2) & 1, (k >> 1) & 1, k & 1) for k in range(1, 8)]
    x, y, c = lax.axis_index("x"), lax.axis_index("y"), lax.axis_index("c")
    me = 4 * x + 2 * y + c

    def copies():
        cps = []
        for a in range(n):
            cps.append(pltpu.make_async_copy(src[a].at[me], out[a].at[me], loc_sems.at[a]))
            for k, (mx, my, mc) in enumerate(masks):
                px, py, pc = _flip(x, mx), _flip(y, my), _flip(c, mc)
                cps.append(pltpu.make_async_remote_copy(
                    src_ref=src[a].at[4 * px + 2 * py + pc], dst_ref=out[a].at[me],
                    send_sem=send_sems.at[7 * a + k], recv_sem=recv_sems.at[7 * a + k],
                    device_id=(px, py, pc), device_id_type=MESH))
        return cps

    def start():
        for cp in copies():
            cp.start()

    def finish():
        for cp in copies():
            cp.wait()

    return start, finish


def _all_to_all(blocks, name):
    n = len(blocks)

    def body(*refs):
        start, finish = _scatter_phases(refs[:n], refs[n:2 * n], *refs[2 * n:])
        start()
        finish()

    any_ = pl.BlockSpec(memory_space=pl.ANY)
    return pl.pallas_call(
        body, name=name,
        in_specs=[any_] * n, out_specs=[any_] * n,
        out_shape=[jax.ShapeDtypeStruct(b.shape, b.dtype) for b in blocks], scratch_shapes=_exchange_sems(n),
    )(*blocks)


def _adamw(parts, w, m, v, br, name, tail=None):
    g8, r, c = parts.shape
    c1 = 1.0 - ADAM_B1 ** ADAM_STEP
    c2 = 1.0 - ADAM_B2 ** ADAM_STEP
    extra = [] if tail is None else [tail]

    def body(p_ref, w_ref, m_ref, v_ref, *refs):
        g_ref, d_ref, m2_ref, v2_ref = refs[len(extra):]
        g = p_ref[0]
        for d in range(1, g8):
            g = g + p_ref[d]
        if tail is not None:
            k = tail.shape[1]
            place = (_rows((k, c)) + (c - k) == _cols((k, c))).astype(F32)
            g = g + jnp.dot(refs[0][...], place, precision=HI, preferred_element_type=F32)
        m2 = ADAM_B1 * m_ref[...] + (1.0 - ADAM_B1) * g
        v2 = ADAM_B2 * v_ref[...] + (1.0 - ADAM_B2) * (g * g)
        g_ref[...] = g
        m2_ref[...] = m2
        v2_ref[...] = v2
        d_ref[...] = -ADAM_LR * ((m2 / c1) / (jnp.sqrt(v2 / c2) + ADAM_EPS) + ADAM_WD * w_ref[...])

    blk = pl.BlockSpec((br, c), lambda i: (i, 0))
    out = jax.ShapeDtypeStruct((r, c), F32)
    return pl.pallas_call(
        body, name=name, grid=(r // br,),
        in_specs=[pl.BlockSpec((g8, br, c), lambda i: (0, i, 0)), blk, blk, blk]
        + [pl.BlockSpec((br, e.shape[1]), lambda i: (i, 0)) for e in extra],
        out_specs=[blk] * 4, out_shape=[out] * 4,
        compiler_params=_params(VMEM_MID),
    )(parts, w, m, v, *extra)


def _pad_rows(a, rows):
    return jnp.pad(a, ((0, rows - a.shape[0]), (0, 0)))


_SMALL = (("g_mix", 1024), ("b_f", 8), ("g_conv_out", 512), ("g_attn_out", 512), ("g_ffn", 1024), ("g_final", 1024))
_SMALL_LEN = sum(n for _, n in _SMALL)
_SMALL_ROWS = 104


def _pack_small(vals, extra=None):
    parts = [vals[k].reshape(-1) for k, _ in _SMALL]
    if extra is not None:
        parts.append(extra.reshape(-1))
    flat = jnp.concatenate(parts)
    return jnp.pad(flat, (0, _SMALL_ROWS * LANES - flat.shape[0])).reshape(_SMALL_ROWS, LANES)


def _unpack_small(packed, shapes):
    flat = packed.reshape(-1)
    out, off = {}, 0
    for k, n in _SMALL:
        out[k] = flat[off:off + n].reshape(shapes[k])
        off += n
    return out


def _col_blocks(a):
    r, c8 = a.shape
    return jnp.transpose(a.reshape(r, 8, c8 // 8), (1, 0, 2))


def _from_col_blocks(a):
    g, r, c = a.shape
    return jnp.transpose(a, (1, 0, 2)).reshape(r, g * c)


def kernel(x, g_mix, w_in, b_f, w_conv, g_conv_out, g_attn_out, w_o, g_ffn, w_up, w_ffn_conv, w_down, g_final, loss_target, m_g_mix, m_w_in, m_b_f, m_w_conv, m_g_conv_out, m_g_attn_out, m_w_o, m_g_ffn, m_w_up, m_w_ffn_conv, m_w_down, m_g_final, v_g_mix, v_w_in, v_b_f, v_w_conv, v_g_conv_out, v_g_attn_out, v_w_o, v_g_ffn, v_w_up, v_w_ffn_conv, v_w_down, v_g_final):
    w = dict(g_mix=g_mix, w_in=w_in[0], b_f=b_f, w_conv=w_conv[0], g_conv_out=g_conv_out, g_attn_out=g_attn_out,
             w_o=w_o[0], g_ffn=g_ffn, w_up=w_up[0], w_ffn_conv=w_ffn_conv[0], w_down=w_down[0], g_final=g_final)
    m = dict(g_mix=m_g_mix, w_in=m_w_in[0], b_f=m_b_f, w_conv=m_w_conv[0], g_conv_out=m_g_conv_out,
             g_attn_out=m_g_attn_out, w_o=m_w_o[0], g_ffn=m_g_ffn, w_up=m_w_up[0], w_ffn_conv=m_w_ffn_conv[0],
             w_down=m_w_down[0], g_final=m_g_final)
    v = dict(g_mix=v_g_mix, w_in=v_w_in[0], b_f=v_b_f, w_conv=v_w_conv[0], g_conv_out=v_g_conv_out,
             g_attn_out=v_g_attn_out, w_o=v_w_o[0], g_ffn=v_g_ffn, w_up=v_w_up[0], w_ffn_conv=v_w_ffn_conv[0],
             w_down=v_w_down[0], g_final=v_g_final)
    shapes = dict(g_mix=g_mix.shape, w_in=w_in.shape, b_f=b_f.shape, w_conv=w_conv.shape,
                  g_conv_out=g_conv_out.shape, g_attn_out=g_attn_out.shape, w_o=w_o.shape, g_ffn=g_ffn.shape,
                  w_up=w_up.shape, w_ffn_conv=w_ffn_conv.shape, w_down=w_down.shape, g_final=g_final.shape)

    tm = t = 512
    tf = 256
    xs, tgt = x[0], loss_target[0]
    seq = xs.shape[0]
    assert seq % tm == 0 and seq % tf == 0
    bk = 1024 if seq % 1024 == 0 else 512

    g_in, g_conv = _all_gather([w["w_in"].astype(BF), w["w_conv"]])
    w_in_f = _from_col_blocks(g_in)
    w_zc = w_in_f[:, :3 * CONV_CH]
    w_qkv = w_in_f[:, 3 * CONV_CH:3 * CONV_CH + 3 * ATTN_W]
    w_f = jnp.pad(w_in_f[:, 3 * CONV_CH + 3 * ATTN_W:], ((0, 0), (0, LANES - N_HEADS)))
    b_f_p = jnp.pad(b_f, ((0, 0), (0, LANES - N_HEADS)))
    w_conv_p = _pad_rows(_from_col_blocks(g_conv), 8)

    h1, zc, qkv, fpre, fcol, nc, qa, ka, qt, vt = _fwd_in(
        xs, g_mix, w_zc, w_qkv, w_f, b_f_p, w_conv_p, g_conv_out, tm)
    o, lcol, g_o, g_up, g_dn, g_fc = _attn_fwd(
        qa, ka, vt, t, [w["w_o"].astype(BF), w["w_up"].astype(BF), w["w_down"].astype(BF), w["w_ffn_conv"]])
    w_o_f = g_o.reshape(D_MODEL, D_MODEL)
    w_up_f = _from_col_blocks(g_up)
    w_dn_f = g_dn.reshape(D_FF, D_MODEL)
    w_fc_p = _pad_rows(_from_col_blocks(g_fc), 8)
    x2, h2, mix = _fwd_out(xs, nc, o, g_attn_out, w_o_f, g_ffn, tm)
    u0, uc, act, dx3, dx3b, loss, dg_final = _fwd_ffn(
        h2, x2, tgt, w_up_f, w_fc_p, w_dn_f, g_final.reshape(1, D_MODEL), tf)
    loss_part = loss

    du0, dx2, dx2b, dw_fc, dg_ffn = _bwd_ffn(dx3, dx3b, u0, uc, x2, w_up_f.T, w_fc_p, w_dn_f.T, g_ffn, tf)
    dw_up = _matmul_tn(h2, du0, 1408, 2 * bk if seq % (2 * bk) == 0 else bk, "dw_up")
    dw_dn = _matmul_tn(act, dx3b, 512, bk, "dw_down")
    dw_o = _matmul_tn(mix, dx2b, 512, bk, "dw_o")
    dzc, doa, dot, qab, dw_conv, dg_attn, dg_conv = _bwd_out(
        dx2b, o, zc, fcol, lcol, qa, w_o_f, g_attn_out, g_conv_out, w_conv_p, t)
    dk, dv, dfk, dqa, dq, r_up, r_dn, r_fc, r_o = _attn_bwd(
        qab, doa, ka, qkv, qt, dot, t,
        [_col_blocks(dw_up), dw_dn.reshape(8, D_FF // 8, D_MODEL), _col_blocks(dw_fc[:3]),
         dw_o.reshape(8, D_MODEL // 8, D_MODEL)])
    dw_zc, dw_q, dw_k, dw_v = _matmul_tn_multi(h1, [dzc, dq, dk, dv], bk, "dw_in")
    dw_main = jnp.concatenate([dw_zc, dw_q, dw_k, dw_v, jnp.zeros((D_MODEL, N_HEADS), F32)], axis=1)
    gx, dfb, dg_mix, db_f, r_in = _bwd_in(xs, dx2, dzc, dq, dqa, dk, dv, dfk, fpre, w_zc, w_qkv, w_f, g_mix, tm,
                                          [_col_blocks(dw_main)])
    dw_f = _matmul_tn(h1, dfb, LANES, bk, "dw_in_f")
    small = dict(g_mix=dg_mix, b_f=db_f[:, :N_HEADS], g_conv_out=dg_conv, g_attn_out=dg_attn, g_ffn=dg_ffn,
                 g_final=dg_final)
    riders = jnp.concatenate([loss_part.reshape(-1), dw_f[:, :N_HEADS].reshape(-1)])
    r_conv, r_small = _all_to_all(
        [_col_blocks(dw_conv[:3]), jnp.broadcast_to(_pack_small(small, riders), (8, _SMALL_ROWS, LANES))],
        "all_to_all_grads")

    g_out, d_out, m_out, v_out = {}, {}, {}, {}
    res = _adamw(r_small, _pack_small(w), _pack_small(m), _pack_small(v), _SMALL_ROWS, "adamw_gains")
    for dst, packed in zip((g_out, d_out, m_out, v_out), res):
        dst.update(_unpack_small(packed, shapes))
    summed = res[0].reshape(-1)
    loss = summed[_SMALL_LEN]
    is_last = (4 * lax.axis_index("x") + 2 * lax.axis_index("y") + lax.axis_index("c") == 7).astype(F32)
    dw_f_sum = summed[_SMALL_LEN + 1:_SMALL_LEN + 1 + D_MODEL * N_HEADS].reshape(D_MODEL, N_HEADS) * is_last

    big = ("w_in", "w_o", "w_up", "w_down", "w_conv", "w_ffn_conv")
    recv = [r_in, r_o, r_up, r_dn, r_conv, r_fc]
    rows = dict(w_in=256, w_o=128, w_up=256, w_down=176, w_conv=3, w_ffn_conv=3)
    for a, k in enumerate(big):
        res = _adamw(recv[a], w[k], m[k], v[k], rows[k], "adamw_" + k, tail=dw_f_sum if k == "w_in" else None)
        g_out[k], d_out[k], m_out[k], v_out[k] = [r.reshape(shapes[k]) for r in res]

    order = ("g_mix", "w_in", "b_f", "w_conv", "g_conv_out", "g_attn_out", "w_o", "g_ffn", "w_up", "w_ffn_conv",
             "w_down", "g_final")
    return (loss, gx.reshape(x.shape), *[g_out[k] for k in order], *[d_out[k] for k in order],
            *[m_out[k] for k in order], *[v_out[k] for k in order])
```

```python
import jax
import jax.numpy as jnp
import numpy as np
from jax import lax
from jax.experimental import pallas as pl
from jax.experimental.pallas import tpu as pltpu

F32 = jnp.float32
BF = jnp.bfloat16
HI = lax.Precision.HIGHEST
MESH = pl.DeviceIdType.MESH

D_MODEL = 1024
CONV_CH = 512
ATTN_W = 512
N_HEADS = 8
HEAD_DIM = 64
D_FF = 2816
FF_CHUNK = 256
N_FF_CHUNKS = D_FF // FF_CHUNK
EPS = 1e-6
NEG = -1e30
LANES = 128
VMEM_BIG = 56 * 1024 * 1024
VMEM_MID = 40 * 1024 * 1024

ADAM_LR = 0.001
ADAM_B1 = 0.9
ADAM_B2 = 0.999
ADAM_EPS = 1e-08
ADAM_WD = 0.01
ADAM_STEP = 10

NT = (((1,), (1,)), ((), ()))
TN = (((0,), (0,)), ((), ()))


def _dot(a, b):
    return jnp.dot(a, b, preferred_element_type=F32)


def _dot_nt(a, b):
    return lax.dot_general(a, b, NT, preferred_element_type=F32)


def _dot_tn(a, b):
    return lax.dot_general(a, b, TN, preferred_element_type=F32)


def _params(vmem):
    return pltpu.CompilerParams(vmem_limit_bytes=vmem)


def _rows(shape):
    return lax.broadcasted_iota(jnp.int32, shape, 0)


def _cols(shape):
    return lax.broadcasted_iota(jnp.int32, shape, 1)


def _shift_down(u, prev, k):
    n = prev.shape[0]
    out = pltpu.roll(u, k, 0)
    row = _rows(u.shape)
    for r in range(k):
        out = jnp.where(row == r, prev[n - k + r:n - k + r + 1, :].astype(u.dtype), out)
    return out


def _shift_up(u, nxt, k):
    tm = u.shape[0]
    out = pltpu.roll(u, tm - k, 0)
    row = _rows(u.shape)
    for r in range(k):
        out = jnp.where(row == tm - k + r, nxt[r:r + 1, :], out)
    return out


def _conv3(u, prev, w):
    u1 = _shift_down(u, prev, 1)
    u2 = _shift_down(u, prev, 2)
    return w[0:1, :] * u2 + w[1:2, :] * u1 + w[2:3, :] * u, u1, u2


def _conv3_bwd(d, nxt, w):
    return w[2:3, :] * d + w[1:2, :] * _shift_up(d, nxt, 1) + w[0:1, :] * _shift_up(d, nxt, 2)


def _rms(x):
    return lax.rsqrt(jnp.mean(x * x, axis=-1, keepdims=True) + EPS)


def _rms_bwd(x, r, dyg):
    return r * dyg - x * (r * r * r) * jnp.mean(dyg * x, axis=-1, keepdims=True)


def _full(shape):
    nd = len(shape)
    return pl.BlockSpec(shape, lambda i, _n=nd: (0,) * _n)


ONES_LANE = 24


def _bias_scatter():
    sc = np.zeros((LANES, 2 * N_HEADS * LANES), np.float32)
    koff = N_HEADS * LANES
    for h in range(N_HEADS):
        aux = HEAD_DIM * (1 - h % 2)
        for j in range(3):
            sc[8 * j + h, LANES * h + aux + j] = 1.0
            sc[ONES_LANE, koff + LANES * h + aux + j] = 1.0
            sc[ONES_LANE, LANES * h + aux + 3 + j] = 1.0
            sc[8 * j + h, koff + LANES * h + aux + 3 + j] = -1.0
    return jnp.asarray(sc, BF)


def _split_parts(v, one):
    hi = v.astype(BF).astype(F32)
    rest = v - hi
    mid = rest.astype(BF).astype(F32)
    lo = (rest - mid).astype(BF).astype(F32)
    parts = hi + pltpu.roll(mid, 8, 1) + pltpu.roll(lo, 16, 1)
    return jnp.where(_cols(parts.shape) == ONES_LANE, one, parts).astype(BF)


def _fwd_in(x, g_mix, w_zc, w_qkv, w_f, b_f, w_conv, g_conv_out, tm):
    s = x.shape[0]
    nb = s // tm
    aw = N_HEADS * LANES

    def body(x_ref, gm_ref, wzc_ref, wqkv_ref, wf_ref, bf_ref, wc_ref, gco_ref, sc_ref,
             h1_ref, zc_ref, qkv_ref, fpre_ref, fcol_ref, nc_ref, qa_ref, ka_ref, qt_ref, vt_ref, cu_ref, cf_ref):
        i = pl.program_id(0)

        @pl.when(i == 0)
        def _():
            cu_ref[...] = jnp.zeros_like(cu_ref)
            cf_ref[...] = jnp.zeros_like(cf_ref)

        xv = x_ref[...]
        hb = (xv * _rms(xv) * gm_ref[...]).astype(BF)
        h1_ref[...] = hb
        zc = _dot(hb, wzc_ref[...])
        zc_ref[...] = zc
        qkv = _dot(hb, wqkv_ref[...])
        qkv = jnp.where(_cols(qkv.shape) < ATTN_W, qkv * 0.125, qkv)
        qkvb = qkv.astype(BF)
        qkv_ref[...] = qkvb
        qt_ref[0] = qkv[:, :ATTN_W].T.astype(BF)
        vt_ref[0] = qkv[:, 2 * ATTN_W:].T.astype(BF)

        gb, gc, xc = zc[:, :CONV_CH], zc[:, CONV_CH:2 * CONV_CH], zc[:, 2 * CONV_CH:]
        u = gc * xc
        cv, _, _ = _conv3(u, cu_ref[...], wc_ref[...])
        cu_ref[...] = u[tm - 8:, :]
        y = gb * cv
        nc_ref[...] = (y * _rms(y) * gco_ref[...]).astype(BF)

        fpre = _dot(hb, wf_ref[...]) + bf_ref[...]
        fpre_ref[...] = fpre
        logf = jnp.minimum(fpre, 0.0) - jnp.log1p(jnp.exp(-jnp.abs(fpre)))
        logf = jnp.where(_cols(logf.shape) < N_HEADS, logf, 0.0)
        tri = (_rows((tm, tm)) >= _cols((tm, tm))).astype(F32)
        fcol = jnp.dot(tri, logf, precision=HI, preferred_element_type=F32) + cf_ref[...]
        cf_ref[...] = fcol[tm - 1:tm, :]
        fcol_ref[...] = fcol

        feat = _dot(_split_parts(fcol, 1.0), sc_ref[...]).astype(BF)
        lane = _cols((tm, LANES))
        for h in range(N_HEADS):
            hp, hh = divmod(h, 2)
            own = (lane >= HEAD_DIM * hh) & (lane < HEAD_DIM * (hh + 1))
            hs = slice(LANES * h, LANES * (h + 1))
            qa_ref[:, hs] = jnp.where(own, qkvb[:, LANES * hp:LANES * (hp + 1)], feat[:, hs])
            ka_ref[:, hs] = jnp.where(own, qkvb[:, ATTN_W + LANES * hp:ATTN_W + LANES * (hp + 1)],
                                      feat[:, aw + LANES * h:aw + LANES * (h + 1)])

    blk = lambda c: pl.BlockSpec((tm, c), lambda i: (i, 0))
    return pl.pallas_call(
        body, name="fwd_in", grid=(nb,),
        in_specs=[blk(D_MODEL), _full((1, D_MODEL)), _full(w_zc.shape), _full(w_qkv.shape), _full(w_f.shape),
                  _full((1, LANES)), _full((8, CONV_CH)), _full((1, CONV_CH)), _full((LANES, 2 * aw))],
        out_specs=[blk(D_MODEL), blk(3 * CONV_CH), blk(3 * ATTN_W), blk(LANES), blk(LANES),
                   blk(CONV_CH), blk(aw), blk(aw)] + [pl.BlockSpec((1, ATTN_W, tm), lambda i: (i, 0, 0))] * 2,
        out_shape=[jax.ShapeDtypeStruct((s, D_MODEL), BF), jax.ShapeDtypeStruct((s, 3 * CONV_CH), F32),
                   jax.ShapeDtypeStruct((s, 3 * ATTN_W), BF), jax.ShapeDtypeStruct((s, LANES), F32),
                   jax.ShapeDtypeStruct((s, LANES), F32),
                   jax.ShapeDtypeStruct((s, CONV_CH), BF), jax.ShapeDtypeStruct((s, aw), BF),
                   jax.ShapeDtypeStruct((s, aw), BF)] + [jax.ShapeDtypeStruct((nb, ATTN_W, tm), BF)] * 2,
        scratch_shapes=[pltpu.VMEM((8, CONV_CH), F32), pltpu.VMEM((1, LANES), F32)],
        compiler_params=_params(VMEM_MID),
    )(x, g_mix, w_zc, w_qkv, w_f, b_f, w_conv, g_conv_out, _bias_scatter())


def _pipeline_masked_last(last, produce, consume, buf_a, buf_b):
    produce(0, buf_a)

    def two_blocks(j, carry):
        blk = 2 * j
        produce(blk + 1, buf_b)
        consume(blk, buf_a, False)
        produce(blk + 2, buf_a)
        consume(blk + 1, buf_b, False)
        return carry

    lax.fori_loop(0, last // 2, two_blocks, 0)

    @pl.when(last % 2 == 0)
    def _():
        consume(last, buf_a, True)

    @pl.when(last % 2 == 1)
    def _():
        produce(last, buf_b)
        consume(last - 1, buf_a, False)
        consume(last, buf_b, True)


def _attn_fwd(qa, ka, vt, t, shards):
    s = qa.shape[0]
    nb = s // t
    n = len(shards)

    def body(qa_ref, ka_any, vt_any, *refs):
        shard_refs, (o_ref, l_ref), gathered = refs[:n], refs[n:n + 2], refs[n + 2:2 * n + 2]
        ka_scr, vt_scr, m_scr, acc_scr, ea_scr, eb_scr, sem = refs[2 * n + 2:2 * n + 9]
        i = pl.program_id(0)
        start, forward, finish = _gather_phases(shard_refs, gathered, *refs[2 * n + 9:])
        pl.when(i == 0)(start)
        pl.when(i == nb // 2)(forward)
        _load_weights(i, [(ka_any, ka_scr), (vt_any, vt_scr)], sem)

        causal_t = _rows((t, t)) <= _cols((t, t))
        row = _rows((LANES, t))
        lrows = jnp.zeros((LANES, t), F32)
        for hp in range(N_HEADS // 2):
            ps = slice(hp * LANES, (hp + 1) * LANES)
            m_scr[...] = jnp.full(m_scr.shape, NEG, F32)
            acc_scr[...] = jnp.zeros(acc_scr.shape, F32)
            heads = [dict(hh=hh, hs=slice((2 * hp + hh) * LANES, (2 * hp + hh + 1) * LANES),
                          aux=HEAD_DIM * (1 - hh), own=(row >= HEAD_DIM * hh) & (row < HEAD_DIM * (hh + 1)),
                          fill=jnp.where(row == HEAD_DIM * (1 - hh), 1.0, 0.0).astype(BF),
                          qh=qa_ref[:, (2 * hp + hh) * LANES:(2 * hp + hh + 1) * LANES]) for hh in range(2)]

            def scores(kb, dst, heads=heads):
                rs = pl.ds(pl.multiple_of(kb * t, t), t)
                for hd in heads:
                    dst[hd["hh"]] = _dot_nt(ka_scr[rs, hd["hs"]], hd["qh"])

            def consume(kb, src, masked, ps=ps, heads=heads):
                vt2 = vt_scr[kb, ps, :]
                m_olds = [m_scr[hd["hh"]:hd["hh"] + 1, :] for hd in heads]
                accs = [acc_scr[hd["hh"]] for hd in heads]
                m_news, acc_news = [], []
                for hd, m_old, acc in zip(heads, m_olds, accs):
                    e = src[hd["hh"]]
                    if masked:
                        e = jnp.where(causal_t, e, NEG)
                    m_new = jnp.maximum(m_old, jnp.max(e, axis=0, keepdims=True))
                    p = jnp.exp(e - m_new).astype(BF)
                    vta = jnp.where(hd["own"], vt2, hd["fill"])
                    acc_news.append(jnp.exp(m_old - m_new) * acc + _dot(vta, p))
                    m_news.append(m_new)
                for hd, m_new, acc in zip(heads, m_news, acc_news):
                    acc_scr[hd["hh"]] = acc
                    m_scr[hd["hh"]:hd["hh"] + 1, :] = m_new

            _pipeline_masked_last(i, scores, consume, ea_scr, eb_scr)

            o_pair = None
            for hd in heads:
                hh = hd["hh"]
                acc = acc_scr[hh]
                denom = acc[hd["aux"]:hd["aux"] + 1, :]
                o_h = acc / denom
                lrows = jnp.where(row == 2 * hp + hh, m_scr[hh:hh + 1, :] + jnp.log(denom), lrows)
                o_pair = o_h if hh == 0 else jnp.where(row < HEAD_DIM, o_pair, o_h)
            o_ref[:, ps] = o_pair.T
        l_ref[...] = lrows.T
        pl.when(i == nb - 1)(finish)

    any_ = pl.BlockSpec(memory_space=pl.ANY)
    return pl.pallas_call(
        body, name="attn_fwd", grid=(nb,),
        in_specs=[pl.BlockSpec((t, N_HEADS * LANES), lambda i: (i, 0)), any_, any_] + [any_] * n,
        out_specs=[pl.BlockSpec((t, ATTN_W), lambda i: (i, 0)), pl.BlockSpec((t, LANES), lambda i: (i, 0))]
        + [any_] * n,
        out_shape=[jax.ShapeDtypeStruct((s, ATTN_W), F32), jax.ShapeDtypeStruct((s, LANES), F32)]
        + _gathered_shapes(shards),
        scratch_shapes=[pltpu.VMEM(ka.shape, BF), pltpu.VMEM(vt.shape, BF), pltpu.VMEM((2, t), F32),
                        pltpu.VMEM((2, LANES, t), F32), pltpu.VMEM((2, t, t), F32), pltpu.VMEM((2, t, t), F32),
                        pltpu.SemaphoreType.DMA((2,))] + _exchange_sems(n),
        compiler_params=_params(VMEM_BIG),
    )(qa, ka, vt, *shards)


def _fwd_out(x, nc, o, g_attn_out, w_o, g_ffn, tm):
    s = x.shape[0]

    def body(x_ref, nc_ref, o_ref, ga_ref, wo_ref, gf_ref, x2_ref, h2_ref, mix_ref):
        ov = o_ref[...]
        na = (ov * _rms(ov) * ga_ref[...]).astype(BF)
        ncv = nc_ref[...]
        mix_ref[:, :CONV_CH] = ncv
        mix_ref[:, CONV_CH:] = na
        x2 = x_ref[...] + _dot(ncv, wo_ref[:CONV_CH, :]) + _dot(na, wo_ref[CONV_CH:, :])
        x2_ref[...] = x2
        h2_ref[...] = (x2 * _rms(x2) * gf_ref[...]).astype(BF)

    blk = lambda c: pl.BlockSpec((tm, c), lambda i: (i, 0))
    return pl.pallas_call(
        body, name="fwd_out", grid=(s // tm,),
        in_specs=[blk(D_MODEL), blk(CONV_CH), blk(ATTN_W), _full((1, ATTN_W)), _full(w_o.shape), _full((1, D_MODEL))],
        out_specs=[blk(D_MODEL), blk(D_MODEL), blk(D_MODEL)],
        out_shape=[jax.ShapeDtypeStruct((s, D_MODEL), F32), jax.ShapeDtypeStruct((s, D_MODEL), BF),
                   jax.ShapeDtypeStruct((s, D_MODEL), BF)],
        compiler_params=_params(VMEM_MID),
    )(x, nc, o, g_attn_out, w_o, g_ffn)


def _load_weights(i, pairs, sem):
    @pl.when(i == 0)
    def _():
        cps = [pltpu.make_async_copy(src, dst, sem.at[n]) for n, (src, dst) in enumerate(pairs)]
        for cp in cps:
            cp.start()
        for cp in cps:
            cp.wait()


def _ff_cols(j):
    return (slice(j * FF_CHUNK, (j + 1) * FF_CHUNK), slice(D_FF + j * FF_CHUNK, D_FF + (j + 1) * FF_CHUNK))


def _fwd_ffn(h2, x2, tgt, w_up, w_ffn_conv, w_dn, g_final, tm):
    s = x2.shape[0]

    def body(h2_ref, x2_ref, tgt_ref, wfc_ref, gfin_ref, wup_any, wdn_any,
             u0_ref, uc_ref, act_ref, dx3_ref, dx3b_ref, loss_ref, dgfin_ref,
             wup, wdn, carry, sem):
        i = pl.program_id(0)
        _load_weights(i, [(wup_any, wup), (wdn_any, wdn)], sem)

        @pl.when(i == 0)
        def _():
            carry[...] = jnp.zeros_like(carry)
            loss_ref[...] = jnp.zeros_like(loss_ref)
            dgfin_ref[...] = jnp.zeros_like(dgfin_ref)

        hb = h2_ref[...]

        def up(j):
            return [_dot(hb, wup[:, cc]) for cc in _ff_cols(j)]

        nxt = up(0)
        down = None
        for j in range(N_FF_CHUNKS):
            cur = nxt
            if j + 1 < N_FF_CHUNKS:
                nxt = up(j + 1)
            parts = []
            for cc, u0 in zip(_ff_cols(j), cur):
                u0_ref[:, cc] = u0.astype(BF)
                uu, _, _ = _conv3(u0, carry[:, cc], wfc_ref[:, cc])
                carry[:, cc] = u0[tm - 8:, :]
                uc_ref[:, cc] = uu.astype(BF)
                parts.append(uu)
            ua, ug = parts
            act = (ug * jax.nn.sigmoid(ug) * ua).astype(BF)
            ca = _ff_cols(j)[0]
            act_ref[:, ca] = act
            part = _dot(act, wdn[ca, :])
            down = part if down is None else down + part

        x3 = x2_ref[...] + down
        r3 = _rms(x3)
        gfin = gfin_ref[...]
        xn = x3 * r3
        diff = xn * gfin - tgt_ref[...]
        loss_ref[...] += jnp.sum(jnp.sum(diff * diff, axis=-1, keepdims=True), axis=0, keepdims=True) * (0.5 / D_MODEL)
        dy = diff * (1.0 / D_MODEL)
        dgfin_ref[...] += jnp.sum(dy * xn, axis=0, keepdims=True)
        dx3 = _rms_bwd(x3, r3, dy * gfin)
        dx3_ref[...] = dx3
        dx3b_ref[...] = dx3.astype(BF)

    blk = lambda c: pl.BlockSpec((tm, c), lambda i: (i, 0))
    any_ = pl.BlockSpec(memory_space=pl.ANY)
    return pl.pallas_call(
        body, name="fwd_ffn", grid=(s // tm,),
        in_specs=[blk(D_MODEL), blk(D_MODEL), blk(D_MODEL), _full((8, 2 * D_FF)), _full((1, D_MODEL)), any_, any_],
        out_specs=[blk(2 * D_FF), blk(2 * D_FF), blk(D_FF), blk(D_MODEL), blk(D_MODEL), _full((1, 1)),
                   _full((1, D_MODEL))],
        out_shape=[jax.ShapeDtypeStruct((s, 2 * D_FF), BF), jax.ShapeDtypeStruct((s, 2 * D_FF), BF),
                   jax.ShapeDtypeStruct((s, D_FF), BF),
                   jax.ShapeDtypeStruct((s, D_MODEL), F32), jax.ShapeDtypeStruct((s, D_MODEL), BF),
                   jax.ShapeDtypeStruct((1, 1), F32), jax.ShapeDtypeStruct((1, D_MODEL), F32)],
        scratch_shapes=[pltpu.VMEM(w_up.shape, BF), pltpu.VMEM(w_dn.shape, BF), pltpu.VMEM((8, 2 * D_FF), F32),
                        pltpu.SemaphoreType.DMA((2,))],
        compiler_params=_params(VMEM_BIG),
    )(h2, x2, tgt, w_ffn_conv, g_final, w_up, w_dn)


def _bwd_ffn(dx3, dx3b, u0, uc, x2, w_up, w_ffn_conv, w_dn, g_ffn, tm):
    s = x2.shape[0]
    nb = s // tm

    def body(dx3_ref, dx3b_ref, u0_ref, uc_ref, x2_ref, wfc_ref, gf_ref, wup_any, wdn_any,
             du0_ref, dx2_ref, dx2b_ref, dwfc_ref, dgf_ref,
             wup, wdn, carry, sem):
        i = pl.program_id(0)
        _load_weights(i, [(wup_any, wup), (wdn_any, wdn)], sem)

        @pl.when(i == 0)
        def _():
            carry[...] = jnp.zeros_like(carry)
            dwfc_ref[...] = jnp.zeros_like(dwfc_ref)
            dgf_ref[...] = jnp.zeros_like(dgf_ref)

        db = dx3b_ref[...]

        def dact_of(j):
            return _dot(db, wdn[:, _ff_cols(j)[0]])

        nxt = dact_of(0)
        dh2 = None
        for j in range(N_FF_CHUNKS):
            ca, cg = _ff_cols(j)
            dact = nxt
            if j + 1 < N_FF_CHUNKS:
                nxt = dact_of(j + 1)
            ua = uc_ref[:, ca].astype(F32)
            ug = uc_ref[:, cg].astype(F32)
            sg = jax.nn.sigmoid(ug)
            da = dact * (ug * sg)
            dg = dact * ua * (sg * (1.0 + ug * (1.0 - sg)))
            for cc, d in ((ca, da), (cg, dg)):
                nxt_rows = carry[:, cc]
                d1 = _shift_up(d, nxt_rows, 1)
                d2 = _shift_up(d, nxt_rows, 2)
                u0c = u0_ref[:, cc].astype(F32)
                w = wfc_ref[:, cc]
                dwfc_ref[0:1, cc] += jnp.sum(d2 * u0c, axis=0, keepdims=True)
                dwfc_ref[1:2, cc] += jnp.sum(d1 * u0c, axis=0, keepdims=True)
                dwfc_ref[2:3, cc] += jnp.sum(d * u0c, axis=0, keepdims=True)
                du0 = (w[2:3, :] * d + w[1:2, :] * d1 + w[0:1, :] * d2).astype(BF)
                carry[:, cc] = d[:8, :]
                du0_ref[:, cc] = du0
                part = _dot(du0, wup[cc, :])
                dh2 = part if dh2 is None else dh2 + part

        x2v = x2_ref[...]
        r2 = _rms(x2v)
        dgf_ref[...] += jnp.sum(dh2 * (x2v * r2), axis=0, keepdims=True)
        dx2 = dx3_ref[...] + _rms_bwd(x2v, r2, dh2 * gf_ref[...])
        dx2_ref[...] = dx2
        dx2b_ref[...] = dx2.astype(BF)

    blk = lambda c: pl.BlockSpec((tm, c), lambda i: (nb - 1 - i, 0))
    any_ = pl.BlockSpec(memory_space=pl.ANY)
    return pl.pallas_call(
        body, name="bwd_ffn", grid=(nb,),
        in_specs=[blk(D_MODEL), blk(D_MODEL), blk(2 * D_FF), blk(2 * D_FF), blk(D_MODEL), _full((8, 2 * D_FF)),
                  _full((1, D_MODEL)), any_, any_],
        out_specs=[blk(2 * D_FF), blk(D_MODEL), blk(D_MODEL), _full((8, 2 * D_FF)), _full((1, D_MODEL))],
        out_shape=[jax.ShapeDtypeStruct((s, 2 * D_FF), BF), jax.ShapeDtypeStruct((s, D_MODEL), F32),
                   jax.ShapeDtypeStruct((s, D_MODEL), BF), jax.ShapeDtypeStruct((8, 2 * D_FF), F32),
                   jax.ShapeDtypeStruct((1, D_MODEL), F32)],
        scratch_shapes=[pltpu.VMEM(w_up.shape, BF), pltpu.VMEM(w_dn.shape, BF), pltpu.VMEM((8, 2 * D_FF), F32),
                        pltpu.SemaphoreType.DMA((2,))],
        compiler_params=_params(VMEM_BIG),
    )(dx3, dx3b, u0, uc, x2, w_ffn_conv, g_ffn, w_up, w_dn)


def _bwd_out(dx2b, o, zc, fcol, lcol, qa, w_o, g_attn_out, g_conv_out, w_conv, tm):
    s = o.shape[0]
    nb = s // tm
    aw = N_HEADS * LANES

    def body(dx2b_ref, o_ref, zc_ref, halo_ref, fcol_ref, lcol_ref, qa_ref, wo_ref, ga_ref, gco_ref, wc_ref, sc_ref,
             dzc_ref, doa_ref, dot_ref, qab_ref, dwc_ref, dga_ref, dgco_ref, carry):
        i = pl.program_id(0)
        rb = nb - 1 - i

        @pl.when(i == 0)
        def _():
            carry[...] = jnp.zeros_like(carry)
            dwc_ref[...] = jnp.zeros_like(dwc_ref)
            dga_ref[...] = jnp.zeros_like(dga_ref)
            dgco_ref[...] = jnp.zeros_like(dgco_ref)

        dmix = _dot_nt(dx2b_ref[...], wo_ref[...])
        dnc, dna = dmix[:, :CONV_CH], dmix[:, CONV_CH:]

        ov = o_ref[...]
        ra = _rms(ov)
        dga_ref[...] += jnp.sum(dna * (ov * ra), axis=0, keepdims=True)
        do = _rms_bwd(ov, ra, dna * ga_ref[...])
        dob = do.astype(BF)
        dot_ref[0] = do.T.astype(BF)
        sel = (_rows((ATTN_W, LANES)) // HEAD_DIM == _cols((ATTN_W, LANES))).astype(F32)
        delta = jnp.dot(do * ov, sel, precision=HI, preferred_element_type=F32)
        featc = _dot(_split_parts(fcol_ref[...] - lcol_ref[...], 1.0), sc_ref[...]).astype(BF)
        featd = _dot(_split_parts(-delta, 0.0), sc_ref[...]).astype(BF)
        lane = _cols((tm, LANES))
        for h in range(N_HEADS):
            hp, hh = divmod(h, 2)
            own = (lane >= HEAD_DIM * hh) & (lane < HEAD_DIM * (hh + 1))
            hs = slice(LANES * h, LANES * (h + 1))
            qab_ref[:, hs] = jnp.where(own, qa_ref[:, hs], featc[:, hs])
            doa_ref[:, hs] = jnp.where(own, dob[:, LANES * hp:LANES * (hp + 1)], featd[:, hs])

        zc_v = zc_ref[...]
        gb, gc, xc = zc_v[:, :CONV_CH], zc_v[:, CONV_CH:2 * CONV_CH], zc_v[:, 2 * CONV_CH:]
        hal = halo_ref[...] * (rb > 0).astype(F32)
        u = gc * xc
        prev = hal[:, CONV_CH:2 * CONV_CH] * hal[:, 2 * CONV_CH:]
        wc = wc_ref[...]
        cv, u1, u2 = _conv3(u, prev, wc)
        y = gb * cv
        rc = _rms(y)
        dgco_ref[...] += jnp.sum(dnc * (y * rc), axis=0, keepdims=True)
        dy = _rms_bwd(y, rc, dnc * gco_ref[...])
        dcv = dy * gb
        dwc_ref[0:1, :] += jnp.sum(dcv * u2, axis=0, keepdims=True)
        dwc_ref[1:2, :] += jnp.sum(dcv * u1, axis=0, keepdims=True)
        dwc_ref[2:3, :] += jnp.sum(dcv * u, axis=0, keepdims=True)
        du = _conv3_bwd(dcv, carry[...], wc)
        carry[...] = dcv[:8, :]
        dzc_ref[:, :CONV_CH] = (dy * cv).astype(BF)
        dzc_ref[:, CONV_CH:2 * CONV_CH] = (du * xc).astype(BF)
        dzc_ref[:, 2 * CONV_CH:] = (du * gc).astype(BF)

    blk = lambda c: pl.BlockSpec((tm, c), lambda i: (nb - 1 - i, 0))
    halo = pl.BlockSpec((8, 3 * CONV_CH), lambda i: (jnp.maximum((nb - 1 - i) * (tm // 8) - 1, 0), 0))
    tr = pl.BlockSpec((1, ATTN_W, tm), lambda i: (nb - 1 - i, 0, 0))
    return pl.pallas_call(
        body, name="bwd_out", grid=(nb,),
        in_specs=[blk(D_MODEL), blk(ATTN_W), blk(3 * CONV_CH), halo, blk(LANES), blk(LANES), blk(aw),
                  _full(w_o.shape), _full((1, ATTN_W)), _full((1, CONV_CH)), _full((8, CONV_CH)), _full((LANES, aw))],
        out_specs=[blk(3 * CONV_CH), blk(aw), tr, blk(aw),
                   _full((8, CONV_CH)), _full((1, ATTN_W)), _full((1, CONV_CH))],
        out_shape=[jax.ShapeDtypeStruct((s, 3 * CONV_CH), BF),
                   jax.ShapeDtypeStruct((s, aw), BF), jax.ShapeDtypeStruct((nb, ATTN_W, tm), BF),
                   jax.ShapeDtypeStruct((s, aw), BF),
                   jax.ShapeDtypeStruct((8, CONV_CH), F32), jax.ShapeDtypeStruct((1, ATTN_W), F32),
                   jax.ShapeDtypeStruct((1, CONV_CH), F32)],
        scratch_shapes=[pltpu.VMEM((8, CONV_CH), F32)],
        compiler_params=_params(VMEM_MID),
    )(dx2b, o, zc, zc, fcol, lcol, qa, w_o, g_attn_out, g_conv_out, w_conv, _bias_scatter()[:, :aw])


def _attn_bwd(qab, doa, ka, qkv, qt, dot, t, blocks):
    s = qab.shape[0]
    nb = s // t
    npair = N_HEADS // 2
    n = len(blocks)

    def body(ka_ref, v_ref, qab_ref, doa_ref, qt_ref, dot_ref, *refs):
        block_refs, (dk_ref, dv_ref, dfk_ref, dqa_ref, dq_ref) = refs[:n], refs[n:n + 5]
        received = refs[n + 5:2 * n + 5]
        acck, accv = refs[2 * n + 5:2 * n + 7]
        hp = pl.program_id(0)
        j = pl.program_id(1)
        start, finish = _scatter_phases(block_refs, received, *refs[2 * n + 7:])
        pl.when((hp == 0) & (j == 0))(start)

        @pl.when(j == 0)
        def _():
            dqa_ref[...] = jnp.zeros_like(dqa_ref)

        causal = _cols((t, t)) <= _rows((t, t))
        row = _rows((LANES, t))
        lane = _cols((t, LANES))
        acck[...] = jnp.zeros(acck.shape, F32)
        accv[...] = jnp.zeros(accv.shape, F32)
        v2 = v_ref[...]
        heads = []
        for hh in range(2):
            aux = HEAD_DIM * (1 - hh)
            ownl = (lane >= HEAD_DIM * hh) & (lane < HEAD_DIM * (hh + 1))
            ones3 = jnp.where((lane >= aux) & (lane < aux + 3), 1.0, 0.0).astype(BF)
            heads.append(dict(hh=hh, hs=slice(hh * LANES, (hh + 1) * LANES), aux=aux,
                              own=(row >= HEAD_DIM * hh) & (row < HEAD_DIM * (hh + 1)),
                              fill=jnp.where(row == aux, 1.0, 0.0).astype(BF),
                              kh=ka_ref[:, hh * LANES:(hh + 1) * LANES], vh=jnp.where(ownl, v2, ones3)))

        def tile(qb, masked):
            rs = pl.ds(pl.multiple_of(qb * t, t), t)
            qt2 = qt_ref[qb]
            dot2 = dot_ref[qb]
            es = [_dot_nt(qab_ref[rs, hd["hs"]], hd["kh"]) for hd in heads]
            dps = [_dot_nt(doa_ref[rs, hd["hs"]], hd["vh"]) for hd in heads]
            aks = [acck[hd["hh"]] for hd in heads]
            avs = [accv[hd["hh"]] for hd in heads]
            dqs = [dqa_ref[rs, hd["hs"]] for hd in heads]
            outs = []
            for hd, e, dp, ak, av, dq in zip(heads, es, dps, aks, avs, dqs):
                if masked:
                    e = jnp.where(causal, e, NEG)
                p = jnp.exp(e)
                ds = (p * dp).astype(BF)
                outs.append((ak + _dot(jnp.where(hd["own"], qt2, hd["fill"]), ds), av + _dot(dot2, p.astype(BF)),
                             dq + _dot(ds, hd["kh"])))
            for hd, (ak, av, dq) in zip(heads, outs):
                acck[hd["hh"]] = ak
                accv[hd["hh"]] = av
                dqa_ref[rs, hd["hs"]] = dq

        def loop_body(qb, carry):
            tile(qb, False)
            return carry

        tile(j, True)
        lax.fori_loop(j + 1, nb, loop_body, 0)
        dk_pair = dv_pair = None
        dfrows = jnp.zeros((LANES, t), F32)
        for hd in heads:
            ak, av = acck[hd["hh"]], accv[hd["hh"]]
            dfrows = jnp.where(row == 2 * hp + hd["hh"], -ak[hd["aux"]:hd["aux"] + 1, :], dfrows)
            dk_pair = ak if hd["hh"] == 0 else jnp.where(row < HEAD_DIM, dk_pair, ak)
            dv_pair = av if hd["hh"] == 0 else jnp.where(row < HEAD_DIM, dv_pair, av)
        dk_ref[...] = dk_pair.T.astype(BF)
        dv_ref[...] = dv_pair.T.astype(BF)
        dfk_ref[0] = dfrows.T

        @pl.when(j == nb - 1)
        def _():
            for r in range(nb):
                rows = slice(r * t, (r + 1) * t)
                dq_ref[rows, :] = (jnp.where(lane < HEAD_DIM, dqa_ref[rows, :LANES], dqa_ref[rows, LANES:])
                                   * 0.125).astype(BF)

        pl.when((hp == npair - 1) & (j == nb - 1))(finish)

    pair_cols = pl.BlockSpec((s, 2 * LANES), lambda hp, j: (0, hp))
    pair_rows = pl.BlockSpec((nb, LANES, t), lambda hp, j: (0, hp, 0))
    any_ = pl.BlockSpec(memory_space=pl.ANY)
    return pl.pallas_call(
        body, name="attn_bwd", grid=(npair, nb),
        in_specs=[pl.BlockSpec((t, 2 * LANES), lambda hp, j: (j, hp)),
                  pl.BlockSpec((t, LANES), lambda hp, j: (j, 2 * npair + hp)),
                  pair_cols, pair_cols, pair_rows, pair_rows] + [any_] * n,
        out_specs=[pl.BlockSpec((t, LANES), lambda hp, j: (j, hp)), pl.BlockSpec((t, LANES), lambda hp, j: (j, hp)),
                   pl.BlockSpec((1, t, LANES), lambda hp, j: (hp, j, 0)), pair_cols,
                   pl.BlockSpec((s, LANES), lambda hp, j: (0, hp))] + [any_] * n,
        out_shape=[jax.ShapeDtypeStruct((s, ATTN_W), BF), jax.ShapeDtypeStruct((s, ATTN_W), BF),
                   jax.ShapeDtypeStruct((npair, s, LANES), F32), jax.ShapeDtypeStruct((s, N_HEADS * LANES), F32),
                   jax.ShapeDtypeStruct((s, ATTN_W), BF)]
        + [jax.ShapeDtypeStruct(b.shape, b.dtype) for b in blocks],
        scratch_shapes=[pltpu.VMEM((2, LANES, t), F32), pltpu.VMEM((2, LANES, t), F32)] + _exchange_sems(n),
        compiler_params=_params(VMEM_BIG),
    )(ka, qkv, qab, doa, qt, dot, *blocks)


def _bwd_in(x, dx2, dzc, dq, dqa, dk, dv, dfk, fpre, w_zc, w_qkv, w_f, g_mix, tm, blocks):
    s = x.shape[0]
    nb = s // tm
    n = len(blocks)

    def body(x_ref, dx2_ref, dzc_ref, dq_ref, dqa_ref, dk_ref, dv_ref, dfk_ref, fpre_ref, wzc_ref, wqkv_ref, wf_ref,
             gm_ref, *refs):
        block_refs, (gx_ref, dfb_ref, dgm_ref, dbf_ref), received = refs[:n], refs[n:n + 4], refs[n + 4:2 * n + 4]
        carry = refs[2 * n + 4]
        i = pl.program_id(0)
        start, finish = _scatter_phases(block_refs, received, *refs[2 * n + 5:])
        pl.when(i == 0)(start)

        @pl.when(i == 0)
        def _():
            carry[...] = jnp.zeros_like(carry)
            dgm_ref[...] = jnp.zeros_like(dgm_ref)
            dbf_ref[...] = jnp.zeros_like(dbf_ref)

        lane = _cols((tm, LANES))
        dfq = jnp.zeros((tm, LANES), F32)
        for h in range(N_HEADS):
            aux = LANES * h + HEAD_DIM * (1 - h % 2)
            dfq = jnp.where(lane == h, dqa_ref[:, aux:aux + 1], dfq)

        triu = (_rows((tm, tm)) <= _cols((tm, tm))).astype(F32)
        df_cum = dfq + ((dfk_ref[0] + dfk_ref[1]) + (dfk_ref[2] + dfk_ref[3]))
        dlogf = jnp.dot(triu, df_cum, precision=HI, preferred_element_type=F32) + carry[...]
        carry[...] = dlogf[0:1, :]
        fpre = fpre_ref[...]
        df = jnp.where(_cols(fpre.shape) < N_HEADS, dlogf / (1.0 + jnp.exp(fpre)), 0.0)
        dbf_ref[...] += jnp.sum(df, axis=0, keepdims=True)
        dfb = df.astype(BF)
        dfb_ref[...] = dfb

        dh1 = _dot_nt(dzc_ref[...], wzc_ref[...])
        dh1 += _dot_nt(dq_ref[...], wqkv_ref[:, :ATTN_W])
        dh1 += _dot_nt(dk_ref[...], wqkv_ref[:, ATTN_W:2 * ATTN_W])
        dh1 += _dot_nt(dv_ref[...], wqkv_ref[:, 2 * ATTN_W:])
        dh1 += _dot_nt(dfb, wf_ref[...])
        xv = x_ref[...]
        r1 = _rms(xv)
        dgm_ref[...] += jnp.sum(dh1 * (xv * r1), axis=0, keepdims=True)
        gx_ref[...] = dx2_ref[...] + _rms_bwd(xv, r1, dh1 * gm_ref[...])
        pl.when(i == nb - 1)(finish)

    blk = lambda c: pl.BlockSpec((tm, c), lambda i: (nb - 1 - i, 0))
    any_ = pl.BlockSpec(memory_space=pl.ANY)
    return pl.pallas_call(
        body, name="bwd_in", grid=(nb,),
        in_specs=[blk(D_MODEL), blk(D_MODEL), blk(3 * CONV_CH), blk(ATTN_W), blk(N_HEADS * LANES), blk(ATTN_W),
                  blk(ATTN_W), pl.BlockSpec((N_HEADS // 2, tm, LANES), lambda i: (0, nb - 1 - i, 0)), blk(LANES),
                  _full(w_zc.shape), _full(w_qkv.shape), _full(w_f.shape), _full((1, D_MODEL))] + [any_] * n,
        out_specs=[blk(D_MODEL), blk(LANES), _full((1, D_MODEL)), _full((1, LANES))] + [any_] * n,
        out_shape=[jax.ShapeDtypeStruct((s, D_MODEL), F32), jax.ShapeDtypeStruct((s, LANES), BF),
                   jax.ShapeDtypeStruct((1, D_MODEL), F32), jax.ShapeDtypeStruct((1, LANES), F32)]
        + [jax.ShapeDtypeStruct(b.shape, b.dtype) for b in blocks],
        scratch_shapes=[pltpu.VMEM((1, LANES), F32)] + _exchange_sems(n),
        compiler_params=_params(VMEM_MID),
    )(x, dx2, dzc, dq, dqa, dk, dv, dfk, fpre, w_zc, w_qkv, w_f, g_mix, *blocks)


def _matmul_tn(a, b, bn, bk, name):
    s, m = a.shape
    n = b.shape[1]

    def body(a_ref, b_ref, o_ref):
        @pl.when(pl.program_id(1) == 0)
        def _():
            o_ref[...] = jnp.zeros_like(o_ref)

        o_ref[...] += _dot_tn(a_ref[...], b_ref[...])

    return pl.pallas_call(
        body, name=name, grid=(n // bn, s // bk),
        in_specs=[pl.BlockSpec((bk, m), lambda jn, k: (k, 0)), pl.BlockSpec((bk, bn), lambda jn, k: (k, jn))],
        out_specs=pl.BlockSpec((m, bn), lambda jn, k: (0, jn)),
        out_shape=jax.ShapeDtypeStruct((m, n), F32),
        compiler_params=_params(VMEM_MID),
    )(a, b)


def _matmul_tn_multi(a, bs, bk, name):
    s, m = a.shape
    nb = len(bs)

    def body(a_ref, *refs):
        b_refs, o_refs = refs[:nb], refs[nb:]

        @pl.when(pl.program_id(0) == 0)
        def _():
            for o_ref in o_refs:
                o_ref[...] = jnp.zeros_like(o_ref)

        at = a_ref[...].T
        for b_ref, o_ref in zip(b_refs, o_refs):
            o_ref[...] += _dot(at, b_ref[...])

    return pl.pallas_call(
        body, name=name, grid=(s // bk,),
        in_specs=[pl.BlockSpec((bk, m), lambda k: (k, 0))] + [pl.BlockSpec((bk, b.shape[1]), lambda k: (k, 0)) for b in bs],
        out_specs=[_full((m, b.shape[1])) for b in bs],
        out_shape=[jax.ShapeDtypeStruct((m, b.shape[1]), F32) for b in bs],
        compiler_params=_params(VMEM_BIG),
    )(a, *bs)


def _flip(v, bit):
    return 1 - v if bit else v


def _all_gather(shards):
    n = len(shards)

    def body(*refs):
        start, forward, finish = _gather_phases(refs[:n], refs[n:2 * n], *refs[2 * n:])
        start()
        forward()
        finish()

    any_ = pl.BlockSpec(memory_space=pl.ANY)
    return pl.pallas_call(
        body, name="all_gather_weights",
        in_specs=[any_] * n, out_specs=[any_] * n,
        out_shape=_gathered_shapes(shards), scratch_shapes=_exchange_sems(n),
    )(*shards)


def _gathered_shapes(shards):
    return [jax.ShapeDtypeStruct((8,) + sh.shape, sh.dtype) for sh in shards]


def _exchange_sems(n):
    return [pltpu.SemaphoreType.DMA((7 * n,)), pltpu.SemaphoreType.DMA((7 * n,)), pltpu.SemaphoreType.DMA((n,))]


def _gather_phases(src, out, send_sems, recv_sems, loc_sems):
    n = len(src)
    x, y, c = lax.axis_index("x"), lax.axis_index("y"), lax.axis_index("c")
    me, sibling = (x, y, c), (x, y, 1 - c)
    chips = [(1 - x, y), (x, 1 - y), (1 - x, 1 - y)]

    def slot(a, px, py, pc):
        return out[a].at[4 * px + 2 * py + pc]

    def copy(a, k, block, to, from_src=False):
        return pltpu.make_async_remote_copy(
            src_ref=src[a] if from_src else slot(a, *block), dst_ref=slot(a, *block),
            send_sem=send_sems.at[7 * a + k], recv_sem=recv_sems.at[7 * a + k],
            device_id=to, device_id_type=MESH)

    def local(a):
        return pltpu.make_async_copy(src[a], slot(a, *me), loc_sems.at[a])

    def first(a):
        return [copy(a, 0, me, sibling, True)] + [copy(a, 1 + j, me, (*chip, c), True)
                                                  for j, chip in enumerate(chips)]

    def passed(a, j):
        return copy(a, 4 + j, (*chips[j], c), sibling)

    def start():
        for a in range(n):
            local(a).start()
            for cp in first(a):
                cp.start()

    def forward():
        for a in range(n):
            for j, chip in enumerate(chips):
                copy(a, 1 + j, (*chip, c), me).wait_recv()
                passed(a, j).start()

    def finish():
        for a in range(n):
            copy(a, 0, sibling, me).wait_recv()
            for j, chip in enumerate(chips):
                copy(a, 4 + j, (*chip, 1 - c), me).wait_recv()
        for a in range(n):
            for cp in first(a) + [passed(a, j) for j in range(3)]:
                cp.wait_send()
            local(a).wait()

    return start, forward, finish


def _scatter_phases(src, out, send_sems, recv_sems, loc_sems):
    n = len(src)
    masks = [((k >> 2) & 1, (k >> 1) & 1, k & 1) for k in range(1, 8)]
    x, y, c = lax.axis_index("x"), lax.axis_index("y"), lax.axis_index("c")
    me = 4 * x + 2 * y + c

    def copies():
        cps = []
        for a in range(n):
            cps.append(pltpu.make_async_copy(src[a].at[me], out[a].at[me], loc_sems.at[a]))
            for k, (mx, my, mc) in enumerate(masks):
                px, py, pc = _flip(x, mx), _flip(y, my), _flip(c, mc)
                cps.append(pltpu.make_async_remote_copy(
                    src_ref=src[a].at[4 * px + 2 * py + pc], dst_ref=out[a].at[me],
                    send_sem=send_sems.at[7 * a + k], recv_sem=recv_sems.at[7 * a + k],
                    device_id=(px, py, pc), device_id_type=MESH))
        return cps

    def start():
        for cp in copies():
            cp.start()

    def finish():
        for cp in copies():
            cp.wait()

    return start, finish


def _all_to_all(blocks, name):
    n = len(blocks)

    def body(*refs):
        start, finish = _scatter_phases(refs[:n], refs[n:2 * n], *refs[2 * n:])
        start()
        finish()

    any_ = pl.BlockSpec(memory_space=pl.ANY)
    return pl.pallas_call(
        body, name=name,
        in_specs=[any_] * n, out_specs=[any_] * n,
        out_shape=[jax.ShapeDtypeStruct(b.shape, b.dtype) for b in blocks], scratch_shapes=_exchange_sems(n),
    )(*blocks)


def _adamw(parts, w, m, v, br, name, tail=None):
    g8, r, c = parts.shape
    c1 = 1.0 - ADAM_B1 ** ADAM_STEP
    c2 = 1.0 - ADAM_B2 ** ADAM_STEP
    extra = [] if tail is None else [tail]

    def body(p_ref, w_ref, m_ref, v_ref, *refs):
        g_ref, d_ref, m2_ref, v2_ref = refs[len(extra):]
        g = p_ref[0].astype(F32)
        for d in range(1, g8):
            g = g + p_ref[d].astype(F32)
        if tail is not None:
            k = tail.shape[1]
            place = (_rows((k, c)) + (c - k) == _cols((k, c))).astype(F32)
            g = g + jnp.dot(refs[0][...], place, precision=HI, preferred_element_type=F32)
        m2 = ADAM_B1 * m_ref[...] + (1.0 - ADAM_B1) * g
        v2 = ADAM_B2 * v_ref[...] + (1.0 - ADAM_B2) * (g * g)
        g_ref[...] = g
        m2_ref[...] = m2
        v2_ref[...] = v2
        d_ref[...] = -ADAM_LR * ((m2 / c1) / (jnp.sqrt(v2 / c2) + ADAM_EPS) + ADAM_WD * w_ref[...])

    blk = pl.BlockSpec((br, c), lambda i: (i, 0))
    out = jax.ShapeDtypeStruct((r, c), F32)
    return pl.pallas_call(
        body, name=name, grid=(r // br,),
        in_specs=[pl.BlockSpec((g8, br, c), lambda i: (0, i, 0)), blk, blk, blk]
        + [pl.BlockSpec((br, e.shape[1]), lambda i: (i, 0)) for e in extra],
        out_specs=[blk] * 4, out_shape=[out] * 4,
        compiler_params=_params(VMEM_MID),
    )(parts, w, m, v, *extra)


def _pad_rows(a, rows):
    return jnp.pad(a, ((0, rows - a.shape[0]), (0, 0)))


_SMALL = (("g_mix", 1024), ("b_f", 8), ("g_conv_out", 512), ("g_attn_out", 512), ("g_ffn", 1024), ("g_final", 1024))
_SMALL_LEN = sum(n for _, n in _SMALL)
_SMALL_ROWS = 104


def _pack_small(vals, extra=None):
    parts = [vals[k].reshape(-1) for k, _ in _SMALL]
    if extra is not None:
        parts.append(extra.reshape(-1))
    flat = jnp.concatenate(parts)
    return jnp.pad(flat, (0, _SMALL_ROWS * LANES - flat.shape[0])).reshape(_SMALL_ROWS, LANES)


def _unpack_small(packed, shapes):
    flat = packed.reshape(-1)
    out, off = {}, 0
    for k, n in _SMALL:
        out[k] = flat[off:off + n].reshape(shapes[k])
        off += n
    return out


def _col_blocks(a):
    r, c8 = a.shape
    return jnp.transpose(a.reshape(r, 8, c8 // 8), (1, 0, 2))


def _from_col_blocks(a):
    g, r, c = a.shape
    return jnp.transpose(a, (1, 0, 2)).reshape(r, g * c)


def kernel(x, g_mix, w_in, b_f, w_conv, g_conv_out, g_attn_out, w_o, g_ffn, w_up, w_ffn_conv, w_down, g_final, loss_target, m_g_mix, m_w_in, m_b_f, m_w_conv, m_g_conv_out, m_g_attn_out, m_w_o, m_g_ffn, m_w_up, m_w_ffn_conv, m_w_down, m_g_final, v_g_mix, v_w_in, v_b_f, v_w_conv, v_g_conv_out, v_g_attn_out, v_w_o, v_g_ffn, v_w_up, v_w_ffn_conv, v_w_down, v_g_final):
    w = dict(g_mix=g_mix, w_in=w_in[0], b_f=b_f, w_conv=w_conv[0], g_conv_out=g_conv_out, g_attn_out=g_attn_out,
             w_o=w_o[0], g_ffn=g_ffn, w_up=w_up[0], w_ffn_conv=w_ffn_conv[0], w_down=w_down[0], g_final=g_final)
    m = dict(g_mix=m_g_mix, w_in=m_w_in[0], b_f=m_b_f, w_conv=m_w_conv[0], g_conv_out=m_g_conv_out,
             g_attn_out=m_g_attn_out, w_o=m_w_o[0], g_ffn=m_g_ffn, w_up=m_w_up[0], w_ffn_conv=m_w_ffn_conv[0],
             w_down=m_w_down[0], g_final=m_g_final)
    v = dict(g_mix=v_g_mix, w_in=v_w_in[0], b_f=v_b_f, w_conv=v_w_conv[0], g_conv_out=v_g_conv_out,
             g_attn_out=v_g_attn_out, w_o=v_w_o[0], g_ffn=v_g_ffn, w_up=v_w_up[0], w_ffn_conv=v_w_ffn_conv[0],
             w_down=v_w_down[0], g_final=v_g_final)
    shapes = dict(g_mix=g_mix.shape, w_in=w_in.shape, b_f=b_f.shape, w_conv=w_conv.shape,
                  g_conv_out=g_conv_out.shape, g_attn_out=g_attn_out.shape, w_o=w_o.shape, g_ffn=g_ffn.shape,
                  w_up=w_up.shape, w_ffn_conv=w_ffn_conv.shape, w_down=w_down.shape, g_final=g_final.shape)

    tm = t = 512
    tf = 256
    xs, tgt = x[0], loss_target[0]
    seq = xs.shape[0]
    assert seq % tm == 0 and seq % tf == 0
    bk = 1024 if seq % 1024 == 0 else 512

    g_in, g_conv = _all_gather([w["w_in"].astype(BF), w["w_conv"]])
    w_in_f = _from_col_blocks(g_in)
    w_zc = w_in_f[:, :3 * CONV_CH]
    w_qkv = w_in_f[:, 3 * CONV_CH:3 * CONV_CH + 3 * ATTN_W]
    w_f = jnp.pad(w_in_f[:, 3 * CONV_CH + 3 * ATTN_W:], ((0, 0), (0, LANES - N_HEADS)))
    b_f_p = jnp.pad(b_f, ((0, 0), (0, LANES - N_HEADS)))
    w_conv_p = _pad_rows(_from_col_blocks(g_conv), 8)

    h1, zc, qkv, fpre, fcol, nc, qa, ka, qt, vt = _fwd_in(
        xs, g_mix, w_zc, w_qkv, w_f, b_f_p, w_conv_p, g_conv_out, tm)
    o, lcol, g_o, g_up, g_dn, g_fc = _attn_fwd(
        qa, ka, vt, t, [w["w_o"].astype(BF), w["w_up"].astype(BF), w["w_down"].astype(BF), w["w_ffn_conv"]])
    w_o_f = g_o.reshape(D_MODEL, D_MODEL)
    w_up_f = _from_col_blocks(g_up)
    w_dn_f = g_dn.reshape(D_FF, D_MODEL)
    w_fc_p = _pad_rows(_from_col_blocks(g_fc), 8)
    x2, h2, mix = _fwd_out(xs, nc, o, g_attn_out, w_o_f, g_ffn, tm)
    u0, uc, act, dx3, dx3b, loss, dg_final = _fwd_ffn(
        h2, x2, tgt, w_up_f, w_fc_p, w_dn_f, g_final.reshape(1, D_MODEL), tf)
    loss_part = loss

    du0, dx2, dx2b, dw_fc, dg_ffn = _bwd_ffn(dx3, dx3b, u0, uc, x2, w_up_f.T, w_fc_p, w_dn_f.T, g_ffn, tf)
    dw_up = _matmul_tn(h2, du0, 1408, 2 * bk if seq % (2 * bk) == 0 else bk, "dw_up")
    dw_dn = _matmul_tn(act, dx3b, 512, bk, "dw_down")
    dw_o = _matmul_tn(mix, dx2b, 512, bk, "dw_o")
    dzc, doa, dot, qab, dw_conv, dg_attn, dg_conv = _bwd_out(
        dx2b, o, zc, fcol, lcol, qa, w_o_f, g_attn_out, g_conv_out, w_conv_p, t)
    dk, dv, dfk, dqa, dq, r_up, r_dn, r_fc, r_o = _attn_bwd(
        qab, doa, ka, qkv, qt, dot, t,
        [_col_blocks(dw_up), dw_dn.reshape(8, D_FF // 8, D_MODEL), _col_blocks(dw_fc[:3]),
         dw_o.reshape(8, D_MODEL // 8, D_MODEL)])
    dw_zc, dw_q, dw_k, dw_v = _matmul_tn_multi(h1, [dzc, dq, dk, dv], bk, "dw_in")
    dw_main = jnp.concatenate([dw_zc, dw_q, dw_k, dw_v, jnp.zeros((D_MODEL, N_HEADS), F32)], axis=1)
    gx, dfb, dg_mix, db_f, r_in = _bwd_in(xs, dx2, dzc, dq, dqa, dk, dv, dfk, fpre, w_zc, w_qkv, w_f, g_mix, tm,
                                          [_col_blocks(dw_main).astype(BF)])
    dw_f = _matmul_tn(h1, dfb, LANES, bk, "dw_in_f")
    small = dict(g_mix=dg_mix, b_f=db_f[:, :N_HEADS], g_conv_out=dg_conv, g_attn_out=dg_attn, g_ffn=dg_ffn,
                 g_final=dg_final)
    riders = jnp.concatenate([loss_part.reshape(-1), dw_f[:, :N_HEADS].reshape(-1)])
    r_conv, r_small = _all_to_all(
        [_col_blocks(dw_conv[:3]), jnp.broadcast_to(_pack_small(small, riders), (8, _SMALL_ROWS, LANES))],
        "all_to_all_grads")

    g_out, d_out, m_out, v_out = {}, {}, {}, {}
    res = _adamw(r_small, _pack_small(w), _pack_small(m), _pack_small(v), _SMALL_ROWS, "adamw_gains")
    for dst, packed in zip((g_out, d_out, m_out, v_out), res):
        dst.update(_unpack_small(packed, shapes))
    summed = res[0].reshape(-1)
    loss = summed[_SMALL_LEN]
    is_last = (4 * lax.axis_index("x") + 2 * lax.axis_index("y") + lax.axis_index("c") == 7).astype(F32)
    dw_f_sum = summed[_SMALL_LEN + 1:_SMALL_LEN + 1 + D_MODEL * N_HEADS].reshape(D_MODEL, N_HEADS) * is_last

    big = ("w_in", "w_o", "w_up", "w_down", "w_conv", "w_ffn_conv")
    recv = [r_in, r_o, r_up, r_dn, r_conv, r_fc]
    rows = dict(w_in=256, w_o=128, w_up=256, w_down=176, w_conv=3, w_ffn_conv=3)
    for a, k in enumerate(big):
        res = _adamw(recv[a], w[k], m[k], v[k], rows[k], "adamw_" + k, tail=dw_f_sum if k == "w_in" else None)
        g_out[k], d_out[k], m_out[k], v_out[k] = [r.reshape(shapes[k]) for r in res]

    order = ("g_mix", "w_in", "b_f", "w_conv", "g_conv_out", "g_attn_out", "w_o", "g_ffn", "w_up", "w_ffn_conv",
             "w_down", "g_final")
    return (loss, gx.reshape(x.shape), *[g_out[k] for k in order], *[d_out[k] for k in order],
            *[m_out[k] for k in order], *[v_out[k] for k in order])
```

```python
import jax
import jax.numpy as jnp
import numpy as np
from jax import lax
from jax.experimental import pallas as pl
from jax.experimental.pallas import tpu as pltpu

F32 = jnp.float32
BF = jnp.bfloat16
HI = lax.Precision.HIGHEST
MESH = pl.DeviceIdType.MESH

D_MODEL = 1024
CONV_CH = 512
ATTN_W = 512
N_HEADS = 8
HEAD_DIM = 64
D_FF = 2816
FF_CHUNK = 256
N_FF_CHUNKS = D_FF // FF_CHUNK
EPS = 1e-6
NEG = -1e30
LANES = 128
VMEM_BIG = 56 * 1024 * 1024
VMEM_MID = 40 * 1024 * 1024

ADAM_LR = 0.001
ADAM_B1 = 0.9
ADAM_B2 = 0.999
ADAM_EPS = 1e-08
ADAM_WD = 0.01
ADAM_STEP = 10

NT = (((1,), (1,)), ((), ()))
TN = (((0,), (0,)), ((), ()))


def _dot(a, b):
    return jnp.dot(a, b, preferred_element_type=F32)


def _dot_nt(a, b):
    return lax.dot_general(a, b, NT, preferred_element_type=F32)


def _dot_tn(a, b):
    return lax.dot_general(a, b, TN, preferred_element_type=F32)


def _params(vmem):
    return pltpu.CompilerParams(vmem_limit_bytes=vmem)


def _rows(shape):
    return lax.broadcasted_iota(jnp.int32, shape, 0)


def _cols(shape):
    return lax.broadcasted_iota(jnp.int32, shape, 1)


def _shift_down(u, prev, k):
    n = prev.shape[0]
    out = pltpu.roll(u, k, 0)
    row = _rows(u.shape)
    for r in range(k):
        out = jnp.where(row == r, prev[n - k + r:n - k + r + 1, :].astype(u.dtype), out)
    return out


def _shift_up(u, nxt, k):
    tm = u.shape[0]
    out = pltpu.roll(u, tm - k, 0)
    row = _rows(u.shape)
    for r in range(k):
        out = jnp.where(row == tm - k + r, nxt[r:r + 1, :], out)
    return out


def _conv3(u, prev, w):
    u1 = _shift_down(u, prev, 1)
    u2 = _shift_down(u, prev, 2)
    return w[0:1, :] * u2 + w[1:2, :] * u1 + w[2:3, :] * u, u1, u2


def _conv3_bwd(d, nxt, w):
    return w[2:3, :] * d + w[1:2, :] * _shift_up(d, nxt, 1) + w[0:1, :] * _shift_up(d, nxt, 2)


def _rms(x):
    return lax.rsqrt(jnp.mean(x * x, axis=-1, keepdims=True) + EPS)


def _rms_bwd(x, r, dyg):
    return r * dyg - x * (r * r * r) * jnp.mean(dyg * x, axis=-1, keepdims=True)


def _full(shape):
    nd = len(shape)
    return pl.BlockSpec(shape, lambda i, _n=nd: (0,) * _n)


ONES_LANE = 24


def _bias_scatter():
    sc = np.zeros((LANES, 2 * N_HEADS * LANES), np.float32)
    koff = N_HEADS * LANES
    for h in range(N_HEADS):
        aux = HEAD_DIM * (1 - h % 2)
        for j in range(3):
            sc[8 * j + h, LANES * h + aux + j] = 1.0
            sc[ONES_LANE, koff + LANES * h + aux + j] = 1.0
            sc[ONES_LANE, LANES * h + aux + 3 + j] = 1.0
            sc[8 * j + h, koff + LANES * h + aux + 3 + j] = -1.0
    return jnp.asarray(sc, BF)


def _split_parts(v, one):
    hi = v.astype(BF).astype(F32)
    rest = v - hi
    mid = rest.astype(BF).astype(F32)
    lo = (rest - mid).astype(BF).astype(F32)
    parts = hi + pltpu.roll(mid, 8, 1) + pltpu.roll(lo, 16, 1)
    return jnp.where(_cols(parts.shape) == ONES_LANE, one, parts).astype(BF)


def _fwd_in(x, g_mix, w_zc, w_qkv, w_f, b_f, w_conv, g_conv_out, tm):
    s = x.shape[0]
    nb = s // tm
    aw = N_HEADS * LANES

    def body(x_ref, gm_ref, wzc_ref, wqkv_ref, wf_ref, bf_ref, wc_ref, gco_ref, sc_ref,
             h1_ref, zc_ref, qkv_ref, fpre_ref, fcol_ref, nc_ref, qa_ref, ka_ref, qt_ref, vt_ref, cu_ref, cf_ref):
        i = pl.program_id(0)

        @pl.when(i == 0)
        def _():
            cu_ref[...] = jnp.zeros_like(cu_ref)
            cf_ref[...] = jnp.zeros_like(cf_ref)

        xv = x_ref[...]
        hb = (xv * _rms(xv) * gm_ref[...]).astype(BF)
        h1_ref[...] = hb
        zc = _dot(hb, wzc_ref[...])
        zc_ref[...] = zc
        qkv = _dot(hb, wqkv_ref[...])
        qkv = jnp.where(_cols(qkv.shape) < ATTN_W, qkv * 0.125, qkv)
        qkvb = qkv.astype(BF)
        qkv_ref[...] = qkvb
        qt_ref[0] = qkv[:, :ATTN_W].T.astype(BF)
        vt_ref[0] = qkv[:, 2 * ATTN_W:].T.astype(BF)

        gb, gc, xc = zc[:, :CONV_CH], zc[:, CONV_CH:2 * CONV_CH], zc[:, 2 * CONV_CH:]
        u = gc * xc
        cv, _, _ = _conv3(u, cu_ref[...], wc_ref[...])
        cu_ref[...] = u[tm - 8:, :]
        y = gb * cv
        nc_ref[...] = (y * _rms(y) * gco_ref[...]).astype(BF)

        fpre = _dot(hb, wf_ref[...]) + bf_ref[...]
        fpre_ref[...] = fpre
        logf = jnp.minimum(fpre, 0.0) - jnp.log1p(jnp.exp(-jnp.abs(fpre)))
        logf = jnp.where(_cols(logf.shape) < N_HEADS, logf, 0.0)
        tri = (_rows((tm, tm)) >= _cols((tm, tm))).astype(F32)
        fcol = jnp.dot(tri, logf, precision=HI, preferred_element_type=F32) + cf_ref[...]
        cf_ref[...] = fcol[tm - 1:tm, :]
        fcol_ref[...] = fcol

        feat = _dot(_split_parts(fcol, 1.0), sc_ref[...]).astype(BF)
        lane = _cols((tm, LANES))
        for h in range(N_HEADS):
            hp, hh = divmod(h, 2)
            own = (lane >= HEAD_DIM * hh) & (lane < HEAD_DIM * (hh + 1))
            hs = slice(LANES * h, LANES * (h + 1))
            qa_ref[:, hs] = jnp.where(own, qkvb[:, LANES * hp:LANES * (hp + 1)], feat[:, hs])
            ka_ref[:, hs] = jnp.where(own, qkvb[:, ATTN_W + LANES * hp:ATTN_W + LANES * (hp + 1)],
                                      feat[:, aw + LANES * h:aw + LANES * (h + 1)])

    blk = lambda c: pl.BlockSpec((tm, c), lambda i: (i, 0))
    return pl.pallas_call(
        body, name="fwd_in", grid=(nb,),
        in_specs=[blk(D_MODEL), _full((1, D_MODEL)), _full(w_zc.shape), _full(w_qkv.shape), _full(w_f.shape),
                  _full((1, LANES)), _full((8, CONV_CH)), _full((1, CONV_CH)), _full((LANES, 2 * aw))],
        out_specs=[blk(D_MODEL), blk(3 * CONV_CH), blk(3 * ATTN_W), blk(LANES), blk(LANES),
                   blk(CONV_CH), blk(aw), blk(aw)] + [pl.BlockSpec((1, ATTN_W, tm), lambda i: (i, 0, 0))] * 2,
        out_shape=[jax.ShapeDtypeStruct((s, D_MODEL), BF), jax.ShapeDtypeStruct((s, 3 * CONV_CH), F32),
                   jax.ShapeDtypeStruct((s, 3 * ATTN_W), BF), jax.ShapeDtypeStruct((s, LANES), F32),
                   jax.ShapeDtypeStruct((s, LANES), F32),
                   jax.ShapeDtypeStruct((s, CONV_CH), BF), jax.ShapeDtypeStruct((s, aw), BF),
                   jax.ShapeDtypeStruct((s, aw), BF)] + [jax.ShapeDtypeStruct((nb, ATTN_W, tm), BF)] * 2,
        scratch_shapes=[pltpu.VMEM((8, CONV_CH), F32), pltpu.VMEM((1, LANES), F32)],
        compiler_params=_params(VMEM_MID),
    )(x, g_mix, w_zc, w_qkv, w_f, b_f, w_conv, g_conv_out, _bias_scatter())


def _pipeline_masked_last(last, produce, consume, buf_a, buf_b):
    produce(0, buf_a)

    def two_blocks(j, carry):
        blk = 2 * j
        produce(blk + 1, buf_b)
        consume(blk, buf_a, False)
        produce(blk + 2, buf_a)
        consume(blk + 1, buf_b, False)
        return carry

    lax.fori_loop(0, last // 2, two_blocks, 0)

    @pl.when(last % 2 == 0)
    def _():
        consume(last, buf_a, True)

    @pl.when(last % 2 == 1)
    def _():
        produce(last, buf_b)
        consume(last - 1, buf_a, False)
        consume(last, buf_b, True)


def _attn_fwd(qa, ka, vt, t, shards):
    s = qa.shape[0]
    nb = s // t
    n = len(shards)

    def body(qa_ref, ka_any, vt_any, *refs):
        shard_refs, (o_ref, l_ref), gathered = refs[:n], refs[n:n + 2], refs[n + 2:2 * n + 2]
        ka_scr, vt_scr, m_scr, acc_scr, ea_scr, eb_scr, sem = refs[2 * n + 2:2 * n + 9]
        i = pl.program_id(0)
        start, forward, finish = _gather_phases(shard_refs, gathered, *refs[2 * n + 9:])
        pl.when(i == 0)(start)
        pl.when(i == nb // 2)(forward)
        _load_weights(i, [(ka_any, ka_scr), (vt_any, vt_scr)], sem)

        causal_t = _rows((t, t)) <= _cols((t, t))
        row = _rows((LANES, t))
        lrows = jnp.zeros((LANES, t), F32)
        for hp in range(N_HEADS // 2):
            ps = slice(hp * LANES, (hp + 1) * LANES)
            m_scr[...] = jnp.full(m_scr.shape, NEG, F32)
            acc_scr[...] = jnp.zeros(acc_scr.shape, F32)
            heads = [dict(hh=hh, hs=slice((2 * hp + hh) * LANES, (2 * hp + hh + 1) * LANES),
                          aux=HEAD_DIM * (1 - hh), own=(row >= HEAD_DIM * hh) & (row < HEAD_DIM * (hh + 1)),
                          fill=jnp.where(row == HEAD_DIM * (1 - hh), 1.0, 0.0).astype(BF),
                          qh=qa_ref[:, (2 * hp + hh) * LANES:(2 * hp + hh + 1) * LANES]) for hh in range(2)]

            def scores(kb, dst, heads=heads):
                rs = pl.ds(pl.multiple_of(kb * t, t), t)
                for hd in heads:
                    dst[hd["hh"]] = _dot_nt(ka_scr[rs, hd["hs"]], hd["qh"])

            def consume(kb, src, masked, ps=ps, heads=heads):
                vt2 = vt_scr[kb, ps, :]
                m_olds = [m_scr[hd["hh"]:hd["hh"] + 1, :] for hd in heads]
                accs = [acc_scr[hd["hh"]] for hd in heads]
                m_news, acc_news = [], []
                for hd, m_old, acc in zip(heads, m_olds, accs):
                    e = src[hd["hh"]]
                    if masked:
                        e = jnp.where(causal_t, e, NEG)
                    m_new = jnp.maximum(m_old, jnp.max(e, axis=0, keepdims=True))
                    p = jnp.exp(e - m_new).astype(BF)
                    vta = jnp.where(hd["own"], vt2, hd["fill"])
                    acc_news.append(jnp.exp(m_old - m_new) * acc + _dot(vta, p))
                    m_news.append(m_new)
                for hd, m_new, acc in zip(heads, m_news, acc_news):
                    acc_scr[hd["hh"]] = acc
                    m_scr[hd["hh"]:hd["hh"] + 1, :] = m_new

            _pipeline_masked_last(i, scores, consume, ea_scr, eb_scr)

            o_pair = None
            for hd in heads:
                hh = hd["hh"]
                acc = acc_scr[hh]
                denom = acc[hd["aux"]:hd["aux"] + 1, :]
                o_h = acc / denom
                lrows = jnp.where(row == 2 * hp + hh, m_scr[hh:hh + 1, :] + jnp.log(denom), lrows)
                o_pair = o_h if hh == 0 else jnp.where(row < HEAD_DIM, o_pair, o_h)
            o_ref[:, ps] = o_pair.T
        l_ref[...] = lrows.T
        pl.when(i == nb - 1)(finish)

    any_ = pl.BlockSpec(memory_space=pl.ANY)
    return pl.pallas_call(
        body, name="attn_fwd", grid=(nb,),
        in_specs=[pl.BlockSpec((t, N_HEADS * LANES), lambda i: (i, 0)), any_, any_] + [any_] * n,
        out_specs=[pl.BlockSpec((t, ATTN_W), lambda i: (i, 0)), pl.BlockSpec((t, LANES), lambda i: (i, 0))]
        + [any_] * n,
        out_shape=[jax.ShapeDtypeStruct((s, ATTN_W), F32), jax.ShapeDtypeStruct((s, LANES), F32)]
        + _gathered_shapes(shards),
        scratch_shapes=[pltpu.VMEM(ka.shape, BF), pltpu.VMEM(vt.shape, BF), pltpu.VMEM((2, t), F32),
                        pltpu.VMEM((2, LANES, t), F32), pltpu.VMEM((2, t, t), F32), pltpu.VMEM((2, t, t), F32),
                        pltpu.SemaphoreType.DMA((2,))] + _exchange_sems(n),
        compiler_params=_params(VMEM_BIG),
    )(qa, ka, vt, *shards)


def _fwd_out(x, nc, o, g_attn_out, w_o, g_ffn, tm):
    s = x.shape[0]

    def body(x_ref, nc_ref, o_ref, ga_ref, wo_ref, gf_ref, x2_ref, h2_ref, mix_ref):
        ov = o_ref[...]
        na = (ov * _rms(ov) * ga_ref[...]).astype(BF)
        ncv = nc_ref[...]
        mix_ref[:, :CONV_CH] = ncv
        mix_ref[:, CONV_CH:] = na
        x2 = x_ref[...] + _dot(ncv, wo_ref[:CONV_CH, :]) + _dot(na, wo_ref[CONV_CH:, :])
        x2_ref[...] = x2
        h2_ref[...] = (x2 * _rms(x2) * gf_ref[...]).astype(BF)

    blk = lambda c: pl.BlockSpec((tm, c), lambda i: (i, 0))
    return pl.pallas_call(
        body, name="fwd_out", grid=(s // tm,),
        in_specs=[blk(D_MODEL), blk(CONV_CH), blk(ATTN_W), _full((1, ATTN_W)), _full(w_o.shape), _full((1, D_MODEL))],
        out_specs=[blk(D_MODEL), blk(D_MODEL), blk(D_MODEL)],
        out_shape=[jax.ShapeDtypeStruct((s, D_MODEL), F32), jax.ShapeDtypeStruct((s, D_MODEL), BF),
                   jax.ShapeDtypeStruct((s, D_MODEL), BF)],
        compiler_params=_params(VMEM_MID),
    )(x, nc, o, g_attn_out, w_o, g_ffn)


def _load_weights(i, pairs, sem):
    @pl.when(i == 0)
    def _():
        cps = [pltpu.make_async_copy(src, dst, sem.at[n]) for n, (src, dst) in enumerate(pairs)]
        for cp in cps:
            cp.start()
        for cp in cps:
            cp.wait()


def _ff_cols(j):
    return (slice(j * FF_CHUNK, (j + 1) * FF_CHUNK), slice(D_FF + j * FF_CHUNK, D_FF + (j + 1) * FF_CHUNK))


def _fwd_ffn(h2, x2, tgt, w_up, w_ffn_conv, w_dn, g_final, tm):
    s = x2.shape[0]

    def body(h2_ref, x2_ref, tgt_ref, wfc_ref, gfin_ref, wup_any, wdn_any,
             u0_ref, uc_ref, act_ref, dx3_ref, dx3b_ref, loss_ref, dgfin_ref,
             wup, wdn, carry, sem):
        i = pl.program_id(0)
        _load_weights(i, [(wup_any, wup), (wdn_any, wdn)], sem)

        @pl.when(i == 0)
        def _():
            carry[...] = jnp.zeros_like(carry)
            loss_ref[...] = jnp.zeros_like(loss_ref)
            dgfin_ref[...] = jnp.zeros_like(dgfin_ref)

        hb = h2_ref[...]

        def up(j):
            return [_dot(hb, wup[:, cc]) for cc in _ff_cols(j)]

        nxt = up(0)
        down = None
        for j in range(N_FF_CHUNKS):
            cur = nxt
            if j + 1 < N_FF_CHUNKS:
                nxt = up(j + 1)
            parts = []
            for cc, u0 in zip(_ff_cols(j), cur):
                u0_ref[:, cc] = u0.astype(BF)
                uu, _, _ = _conv3(u0, carry[:, cc], wfc_ref[:, cc])
                carry[:, cc] = u0[tm - 8:, :]
                uc_ref[:, cc] = uu.astype(BF)
                parts.append(uu)
            ua, ug = parts
            act = (ug * jax.nn.sigmoid(ug) * ua).astype(BF)
            ca = _ff_cols(j)[0]
            act_ref[:, ca] = act
            part = _dot(act, wdn[ca, :])
            down = part if down is None else down + part

        x3 = x2_ref[...] + down
        r3 = _rms(x3)
        gfin = gfin_ref[...]
        xn = x3 * r3
        diff = xn * gfin - tgt_ref[...]
        loss_ref[...] += jnp.sum(jnp.sum(diff * diff, axis=-1, keepdims=True), axis=0, keepdims=True) * (0.5 / D_MODEL)
        dy = diff * (1.0 / D_MODEL)
        dgfin_ref[...] += jnp.sum(dy * xn, axis=0, keepdims=True)
        dx3 = _rms_bwd(x3, r3, dy * gfin)
        dx3_ref[...] = dx3
        dx3b_ref[...] = dx3.astype(BF)

    blk = lambda c: pl.BlockSpec((tm, c), lambda i: (i, 0))
    any_ = pl.BlockSpec(memory_space=pl.ANY)
    return pl.pallas_call(
        body, name="fwd_ffn", grid=(s // tm,),
        in_specs=[blk(D_MODEL), blk(D_MODEL), blk(D_MODEL), _full((8, 2 * D_FF)), _full((1, D_MODEL)), any_, any_],
        out_specs=[blk(2 * D_FF), blk(2 * D_FF), blk(D_FF), blk(D_MODEL), blk(D_MODEL), _full((1, 1)),
                   _full((1, D_MODEL))],
        out_shape=[jax.ShapeDtypeStruct((s, 2 * D_FF), BF), jax.ShapeDtypeStruct((s, 2 * D_FF), BF),
                   jax.ShapeDtypeStruct((s, D_FF), BF),
                   jax.ShapeDtypeStruct((s, D_MODEL), F32), jax.ShapeDtypeStruct((s, D_MODEL), BF),
                   jax.ShapeDtypeStruct((1, 1), F32), jax.ShapeDtypeStruct((1, D_MODEL), F32)],
        scratch_shapes=[pltpu.VMEM(w_up.shape, BF), pltpu.VMEM(w_dn.shape, BF), pltpu.VMEM((8, 2 * D_FF), F32),
                        pltpu.SemaphoreType.DMA((2,))],
        compiler_params=_params(VMEM_BIG),
    )(h2, x2, tgt, w_ffn_conv, g_final, w_up, w_dn)


def _bwd_ffn(dx3, dx3b, u0, uc, x2, w_up, w_ffn_conv, w_dn, g_ffn, tm):
    s = x2.shape[0]
    nb = s // tm

    def body(dx3_ref, dx3b_ref, u0_ref, uc_ref, x2_ref, wfc_ref, gf_ref, wup_any, wdn_any,
             du0_ref, dx2_ref, dx2b_ref, dwfc_ref, dgf_ref,
             wup, wdn, carry, sem):
        i = pl.program_id(0)
        _load_weights(i, [(wup_any, wup), (wdn_any, wdn)], sem)

        @pl.when(i == 0)
        def _():
            carry[...] = jnp.zeros_like(carry)
            dwfc_ref[...] = jnp.zeros_like(dwfc_ref)
            dgf_ref[...] = jnp.zeros_like(dgf_ref)

        db = dx3b_ref[...]

        def dact_of(j):
            return _dot(db, wdn[:, _ff_cols(j)[0]])

        nxt = dact_of(0)
        dh2 = None
        for j in range(N_FF_CHUNKS):
            ca, cg = _ff_cols(j)
            dact = nxt
            if j + 1 < N_FF_CHUNKS:
                nxt = dact_of(j + 1)
            ua = uc_ref[:, ca].astype(F32)
            ug = uc_ref[:, cg].astype(F32)
            sg = jax.nn.sigmoid(ug)
            da = dact * (ug * sg)
            dg = dact * ua * (sg * (1.0 + ug * (1.0 - sg)))
            for cc, d in ((ca, da), (cg, dg)):
                nxt_rows = carry[:, cc]
                d1 = _shift_up(d, nxt_rows, 1)
                d2 = _shift_up(d, nxt_rows, 2)
                u0c = u0_ref[:, cc].astype(F32)
                w = wfc_ref[:, cc]
                dwfc_ref[0:1, cc] += jnp.sum(d2 * u0c, axis=0, keepdims=True)
                dwfc_ref[1:2, cc] += jnp.sum(d1 * u0c, axis=0, keepdims=True)
                dwfc_ref[2:3, cc] += jnp.sum(d * u0c, axis=0, keepdims=True)
                du0 = (w[2:3, :] * d + w[1:2, :] * d1 + w[0:1, :] * d2).astype(BF)
                carry[:, cc] = d[:8, :]
                du0_ref[:, cc] = du0
                part = _dot(du0, wup[cc, :])
                dh2 = part if dh2 is None else dh2 + part

        x2v = x2_ref[...]
        r2 = _rms(x2v)
        dgf_ref[...] += jnp.sum(dh2 * (x2v * r2), axis=0, keepdims=True)
        dx2 = dx3_ref[...] + _rms_bwd(x2v, r2, dh2 * gf_ref[...])
        dx2_ref[...] = dx2
        dx2b_ref[...] = dx2.astype(BF)

    blk = lambda c: pl.BlockSpec((tm, c), lambda i: (nb - 1 - i, 0))
    any_ = pl.BlockSpec(memory_space=pl.ANY)
    return pl.pallas_call(
        body, name="bwd_ffn", grid=(nb,),
        in_specs=[blk(D_MODEL), blk(D_MODEL), blk(2 * D_FF), blk(2 * D_FF), blk(D_MODEL), _full((8, 2 * D_FF)),
                  _full((1, D_MODEL)), any_, any_],
        out_specs=[blk(2 * D_FF), blk(D_MODEL), blk(D_MODEL), _full((8, 2 * D_FF)), _full((1, D_MODEL))],
        out_shape=[jax.ShapeDtypeStruct((s, 2 * D_FF), BF), jax.ShapeDtypeStruct((s, D_MODEL), F32),
                   jax.ShapeDtypeStruct((s, D_MODEL), BF), jax.ShapeDtypeStruct((8, 2 * D_FF), F32),
                   jax.ShapeDtypeStruct((1, D_MODEL), F32)],
        scratch_shapes=[pltpu.VMEM(w_up.shape, BF), pltpu.VMEM(w_dn.shape, BF), pltpu.VMEM((8, 2 * D_FF), F32),
                        pltpu.SemaphoreType.DMA((2,))],
        compiler_params=_params(VMEM_BIG),
    )(dx3, dx3b, u0, uc, x2, w_ffn_conv, g_ffn, w_up, w_dn)


def _bwd_out(dx2b, o, zc, fcol, lcol, qa, w_o, g_attn_out, g_conv_out, w_conv, tm):
    s = o.shape[0]
    nb = s // tm
    aw = N_HEADS * LANES

    def body(dx2b_ref, o_ref, zc_ref, halo_ref, fcol_ref, lcol_ref, qa_ref, wo_ref, ga_ref, gco_ref, wc_ref, sc_ref,
             dzc_ref, doa_ref, dot_ref, qab_ref, dwc_ref, dga_ref, dgco_ref, carry):
        i = pl.program_id(0)
        rb = nb - 1 - i

        @pl.when(i == 0)
        def _():
            carry[...] = jnp.zeros_like(carry)
            dwc_ref[...] = jnp.zeros_like(dwc_ref)
            dga_ref[...] = jnp.zeros_like(dga_ref)
            dgco_ref[...] = jnp.zeros_like(dgco_ref)

        dmix = _dot_nt(dx2b_ref[...], wo_ref[...])
        dnc, dna = dmix[:, :CONV_CH], dmix[:, CONV_CH:]

        ov = o_ref[...]
        ra = _rms(ov)
        dga_ref[...] += jnp.sum(dna * (ov * ra), axis=0, keepdims=True)
        do = _rms_bwd(ov, ra, dna * ga_ref[...])
        dob = do.astype(BF)
        dot_ref[0] = do.T.astype(BF)
        sel = (_rows((ATTN_W, LANES)) // HEAD_DIM == _cols((ATTN_W, LANES))).astype(F32)
        delta = jnp.dot(do * ov, sel, precision=HI, preferred_element_type=F32)
        featc = _dot(_split_parts(fcol_ref[...] - lcol_ref[...], 1.0), sc_ref[...]).astype(BF)
        featd = _dot(_split_parts(-delta, 0.0), sc_ref[...]).astype(BF)
        lane = _cols((tm, LANES))
        for h in range(N_HEADS):
            hp, hh = divmod(h, 2)
            own = (lane >= HEAD_DIM * hh) & (lane < HEAD_DIM * (hh + 1))
            hs = slice(LANES * h, LANES * (h + 1))
            qab_ref[:, hs] = jnp.where(own, qa_ref[:, hs], featc[:, hs])
            doa_ref[:, hs] = jnp.where(own, dob[:, LANES * hp:LANES * (hp + 1)], featd[:, hs])

        zc_v = zc_ref[...]
        gb, gc, xc = zc_v[:, :CONV_CH], zc_v[:, CONV_CH:2 * CONV_CH], zc_v[:, 2 * CONV_CH:]
        hal = halo_ref[...] * (rb > 0).astype(F32)
        u = gc * xc
        prev = hal[:, CONV_CH:2 * CONV_CH] * hal[:, 2 * CONV_CH:]
        wc = wc_ref[...]
        cv, u1, u2 = _conv3(u, prev, wc)
        y = gb * cv
        rc = _rms(y)
        dgco_ref[...] += jnp.sum(dnc * (y * rc), axis=0, keepdims=True)
        dy = _rms_bwd(y, rc, dnc * gco_ref[...])
        dcv = dy * gb
        dwc_ref[0:1, :] += jnp.sum(dcv * u2, axis=0, keepdims=True)
        dwc_ref[1:2, :] += jnp.sum(dcv * u1, axis=0, keepdims=True)
        dwc_ref[2:3, :] += jnp.sum(dcv * u, axis=0, keepdims=True)
        du = _conv3_bwd(dcv, carry[...], wc)
        carry[...] = dcv[:8, :]
        dzc_ref[:, :CONV_CH] = (dy * cv).astype(BF)
        dzc_ref[:, CONV_CH:2 * CONV_CH] = (du * xc).astype(BF)
        dzc_ref[:, 2 * CONV_CH:] = (du * gc).astype(BF)

    blk = lambda c: pl.BlockSpec((tm, c), lambda i: (nb - 1 - i, 0))
    halo = pl.BlockSpec((8, 3 * CONV_CH), lambda i: (jnp.maximum((nb - 1 - i) * (tm // 8) - 1, 0), 0))
    tr = pl.BlockSpec((1, ATTN_W, tm), lambda i: (nb - 1 - i, 0, 0))
    return pl.pallas_call(
        body, name="bwd_out", grid=(nb,),
        in_specs=[blk(D_MODEL), blk(ATTN_W), blk(3 * CONV_CH), halo, blk(LANES), blk(LANES), blk(aw),
                  _full(w_o.shape), _full((1, ATTN_W)), _full((1, CONV_CH)), _full((8, CONV_CH)), _full((LANES, aw))],
        out_specs=[blk(3 * CONV_CH), blk(aw), tr, blk(aw),
                   _full((8, CONV_CH)), _full((1, ATTN_W)), _full((1, CONV_CH))],
        out_shape=[jax.ShapeDtypeStruct((s, 3 * CONV_CH), BF),
                   jax.ShapeDtypeStruct((s, aw), BF), jax.ShapeDtypeStruct((nb, ATTN_W, tm), BF),
                   jax.ShapeDtypeStruct((s, aw), BF),
                   jax.ShapeDtypeStruct((8, CONV_CH), F32), jax.ShapeDtypeStruct((1, ATTN_W), F32),
                   jax.ShapeDtypeStruct((1, CONV_CH), F32)],
        scratch_shapes=[pltpu.VMEM((8, CONV_CH), F32)],
        compiler_params=_params(VMEM_MID),
    )(dx2b, o, zc, zc, fcol, lcol, qa, w_o, g_attn_out, g_conv_out, w_conv, _bias_scatter()[:, :aw])


def _attn_bwd(qab, doa, ka, qkv, qt, dot, t, blocks):
    s = qab.shape[0]
    nb = s // t
    npair = N_HEADS // 2
    n = len(blocks)

    def body(ka_ref, v_ref, qab_ref, doa_ref, qt_ref, dot_ref, *refs):
        block_refs, (dk_ref, dv_ref, dfk_ref, dqa_ref, dq_ref) = refs[:n], refs[n:n + 5]
        received = refs[n + 5:2 * n + 5]
        acck, accv = refs[2 * n + 5:2 * n + 7]
        hp = pl.program_id(0)
        j = pl.program_id(1)
        start, finish = _scatter_phases(block_refs, received, *refs[2 * n + 7:])
        pl.when((hp == 0) & (j == 0))(start)

        @pl.when(j == 0)
        def _():
            dqa_ref[...] = jnp.zeros_like(dqa_ref)

        causal = _cols((t, t)) <= _rows((t, t))
        row = _rows((LANES, t))
        lane = _cols((t, LANES))
        acck[...] = jnp.zeros(acck.shape, F32)
        accv[...] = jnp.zeros(accv.shape, F32)
        v2 = v_ref[...]
        heads = []
        for hh in range(2):
            aux = HEAD_DIM * (1 - hh)
            ownl = (lane >= HEAD_DIM * hh) & (lane < HEAD_DIM * (hh + 1))
            ones3 = jnp.where((lane >= aux) & (lane < aux + 3), 1.0, 0.0).astype(BF)
            heads.append(dict(hh=hh, hs=slice(hh * LANES, (hh + 1) * LANES), aux=aux,
                              own=(row >= HEAD_DIM * hh) & (row < HEAD_DIM * (hh + 1)),
                              fill=jnp.where(row == aux, 1.0, 0.0).astype(BF),
                              kh=ka_ref[:, hh * LANES:(hh + 1) * LANES], vh=jnp.where(ownl, v2, ones3)))

        def tile(qb, masked):
            rs = pl.ds(pl.multiple_of(qb * t, t), t)
            qt2 = qt_ref[qb]
            dot2 = dot_ref[qb]
            es = [_dot_nt(qab_ref[rs, hd["hs"]], hd["kh"]) for hd in heads]
            dps = [_dot_nt(doa_ref[rs, hd["hs"]], hd["vh"]) for hd in heads]
            aks = [acck[hd["hh"]] for hd in heads]
            avs = [accv[hd["hh"]] for hd in heads]
            dqs = [dqa_ref[rs, hd["hs"]] for hd in heads]
            outs = []
            for hd, e, dp, ak, av, dq in zip(heads, es, dps, aks, avs, dqs):
                if masked:
                    e = jnp.where(causal, e, NEG)
                p = jnp.exp(e)
                ds = (p * dp).astype(BF)
                outs.append((ak + _dot(jnp.where(hd["own"], qt2, hd["fill"]), ds), av + _dot(dot2, p.astype(BF)),
                             dq + _dot(ds, hd["kh"])))
            for hd, (ak, av, dq) in zip(heads, outs):
                acck[hd["hh"]] = ak
                accv[hd["hh"]] = av
                dqa_ref[rs, hd["hs"]] = dq

        def loop_body(qb, carry):
            tile(qb, False)
            return carry

        tile(j, True)
        lax.fori_loop(j + 1, nb, loop_body, 0)
        dk_pair = dv_pair = None
        dfrows = jnp.zeros((LANES, t), F32)
        for hd in heads:
            ak, av = acck[hd["hh"]], accv[hd["hh"]]
            dfrows = jnp.where(row == 2 * hp + hd["hh"], -ak[hd["aux"]:hd["aux"] + 1, :], dfrows)
            dk_pair = ak if hd["hh"] == 0 else jnp.where(row < HEAD_DIM, dk_pair, ak)
            dv_pair = av if hd["hh"] == 0 else jnp.where(row < HEAD_DIM, dv_pair, av)
        dk_ref[...] = dk_pair.T.astype(BF)
        dv_ref[...] = dv_pair.T.astype(BF)
        dfk_ref[0] = dfrows.T

        @pl.when(j == nb - 1)
        def _():
            for r in range(nb):
                rows = slice(r * t, (r + 1) * t)
                dq_ref[rows, :] = (jnp.where(lane < HEAD_DIM, dqa_ref[rows, :LANES], dqa_ref[rows, LANES:])
                                   * 0.125).astype(BF)

        pl.when((hp == npair - 1) & (j == nb - 1))(finish)

    pair_cols = pl.BlockSpec((s, 2 * LANES), lambda hp, j: (0, hp))
    pair_rows = pl.BlockSpec((nb, LANES, t), lambda hp, j: (0, hp, 0))
    any_ = pl.BlockSpec(memory_space=pl.ANY)
    return pl.pallas_call(
        body, name="attn_bwd", grid=(npair, nb),
        in_specs=[pl.BlockSpec((t, 2 * LANES), lambda hp, j: (j, hp)),
                  pl.BlockSpec((t, LANES), lambda hp, j: (j, 2 * npair + hp)),
                  pair_cols, pair_cols, pair_rows, pair_rows] + [any_] * n,
        out_specs=[pl.BlockSpec((t, LANES), lambda hp, j: (j, hp)), pl.BlockSpec((t, LANES), lambda hp, j: (j, hp)),
                   pl.BlockSpec((1, t, LANES), lambda hp, j: (hp, j, 0)), pair_cols,
                   pl.BlockSpec((s, LANES), lambda hp, j: (0, hp))] + [any_] * n,
        out_shape=[jax.ShapeDtypeStruct((s, ATTN_W), BF), jax.ShapeDtypeStruct((s, ATTN_W), BF),
                   jax.ShapeDtypeStruct((npair, s, LANES), F32), jax.ShapeDtypeStruct((s, N_HEADS * LANES), F32),
                   jax.ShapeDtypeStruct((s, ATTN_W), BF)]
        + [jax.ShapeDtypeStruct(b.shape, b.dtype) for b in blocks],
        scratch_shapes=[pltpu.VMEM((2, LANES, t), F32), pltpu.VMEM((2, LANES, t), F32)] + _exchange_sems(n),
        compiler_params=_params(VMEM_BIG),
    )(ka, qkv, qab, doa, qt, dot, *blocks)


def _bwd_in(x, dx2, dzc, dq, dqa, dk, dv, dfk, fpre, w_zc, w_qkv, w_f, g_mix, tm, blocks):
    s = x.shape[0]
    nb = s // tm
    n = len(blocks)

    def body(x_ref, dx2_ref, dzc_ref, dq_ref, dqa_ref, dk_ref, dv_ref, dfk_ref, fpre_ref, wzc_ref, wqkv_ref, wf_ref,
             gm_ref, *refs):
        block_refs, (gx_ref, dfb_ref, dgm_ref, dbf_ref), received = refs[:n], refs[n:n + 4], refs[n + 4:2 * n + 4]
        carry = refs[2 * n + 4]
        i = pl.program_id(0)
        start, finish = _scatter_phases(block_refs, received, *refs[2 * n + 5:])
        pl.when(i == 0)(start)

        @pl.when(i == 0)
        def _():
            carry[...] = jnp.zeros_like(carry)
            dgm_ref[...] = jnp.zeros_like(dgm_ref)
            dbf_ref[...] = jnp.zeros_like(dbf_ref)

        lane = _cols((tm, LANES))
        dfq = jnp.zeros((tm, LANES), F32)
        for h in range(N_HEADS):
            aux = LANES * h + HEAD_DIM * (1 - h % 2)
            dfq = jnp.where(lane == h, dqa_ref[:, aux:aux + 1], dfq)

        triu = (_rows((tm, tm)) <= _cols((tm, tm))).astype(F32)
        df_cum = dfq + ((dfk_ref[0] + dfk_ref[1]) + (dfk_ref[2] + dfk_ref[3]))
        dlogf = jnp.dot(triu, df_cum, precision=HI, preferred_element_type=F32) + carry[...]
        carry[...] = dlogf[0:1, :]
        fpre = fpre_ref[...]
        df = jnp.where(_cols(fpre.shape) < N_HEADS, dlogf / (1.0 + jnp.exp(fpre)), 0.0)
        dbf_ref[...] += jnp.sum(df, axis=0, keepdims=True)
        dfb = df.astype(BF)
        dfb_ref[...] = dfb

        dh1 = _dot_nt(dzc_ref[...], wzc_ref[...])
        dh1 += _dot_nt(dq_ref[...], wqkv_ref[:, :ATTN_W])
        dh1 += _dot_nt(dk_ref[...], wqkv_ref[:, ATTN_W:2 * ATTN_W])
        dh1 += _dot_nt(dv_ref[...], wqkv_ref[:, 2 * ATTN_W:])
        dh1 += _dot_nt(dfb, wf_ref[...])
        xv = x_ref[...]
        r1 = _rms(xv)
        dgm_ref[...] += jnp.sum(dh1 * (xv * r1), axis=0, keepdims=True)
        gx_ref[...] = dx2_ref[...] + _rms_bwd(xv, r1, dh1 * gm_ref[...])
        pl.when(i == nb - 1)(finish)

    blk = lambda c: pl.BlockSpec((tm, c), lambda i: (nb - 1 - i, 0))
    any_ = pl.BlockSpec(memory_space=pl.ANY)
    return pl.pallas_call(
        body, name="bwd_in", grid=(nb,),
        in_specs=[blk(D_MODEL), blk(D_MODEL), blk(3 * CONV_CH), blk(ATTN_W), blk(N_HEADS * LANES), blk(ATTN_W),
                  blk(ATTN_W), pl.BlockSpec((N_HEADS // 2, tm, LANES), lambda i: (0, nb - 1 - i, 0)), blk(LANES),
                  _full(w_zc.shape), _full(w_qkv.shape), _full(w_f.shape), _full((1, D_MODEL))] + [any_] * n,
        out_specs=[blk(D_MODEL), blk(LANES), _full((1, D_MODEL)), _full((1, LANES))] + [any_] * n,
        out_shape=[jax.ShapeDtypeStruct((s, D_MODEL), F32), jax.ShapeDtypeStruct((s, LANES), BF),
                   jax.ShapeDtypeStruct((1, D_MODEL), F32), jax.ShapeDtypeStruct((1, LANES), F32)]
        + [jax.ShapeDtypeStruct(b.shape, b.dtype) for b in blocks],
        scratch_shapes=[pltpu.VMEM((1, LANES), F32)] + _exchange_sems(n),
        compiler_params=_params(VMEM_MID),
    )(x, dx2, dzc, dq, dqa, dk, dv, dfk, fpre, w_zc, w_qkv, w_f, g_mix, *blocks)


def _matmul_tn(a, b, bn, bk, name):
    s, m = a.shape
    n = b.shape[1]

    def body(a_ref, b_ref, o_ref):
        @pl.when(pl.program_id(1) == 0)
        def _():
            o_ref[...] = jnp.zeros_like(o_ref)

        o_ref[...] += _dot_tn(a_ref[...], b_ref[...])

    return pl.pallas_call(
        body, name=name, grid=(n // bn, s // bk),
        in_specs=[pl.BlockSpec((bk, m), lambda jn, k: (k, 0)), pl.BlockSpec((bk, bn), lambda jn, k: (k, jn))],
        out_specs=pl.BlockSpec((m, bn), lambda jn, k: (0, jn)),
        out_shape=jax.ShapeDtypeStruct((m, n), F32),
        compiler_params=_params(VMEM_MID),
    )(a, b)


def _matmul_tn_multi(a, bs, bk, name):
    s, m = a.shape
    nb = len(bs)

    def body(a_ref, *refs):
        b_refs, o_refs = refs[:nb], refs[nb:]

        @pl.when(pl.program_id(0) == 0)
        def _():
            for o_ref in o_refs:
                o_ref[...] = jnp.zeros_like(o_ref)

        at = a_ref[...].T
        for b_ref, o_ref in zip(b_refs, o_refs):
            o_ref[...] += _dot(at, b_ref[...])

    return pl.pallas_call(
        body, name=name, grid=(s // bk,),
        in_specs=[pl.BlockSpec((bk, m), lambda k: (k, 0))]
        + [pl.BlockSpec((bk, b.shape[1]), lambda k: (k, 0)) for b in bs],
        out_specs=[_full((m, b.shape[1])) for b in bs],
        out_shape=[jax.ShapeDtypeStruct((m, b.shape[1]), F32) for b in bs],
        compiler_params=_params(VMEM_BIG),
    )(a, *bs)


def _flip(v, bit):
    return 1 - v if bit else v


def _all_gather(shards):
    n = len(shards)

    def body(*refs):
        start, forward, finish = _gather_phases(refs[:n], refs[n:2 * n], *refs[2 * n:])
        start()
        forward()
        finish()

    any_ = pl.BlockSpec(memory_space=pl.ANY)
    return pl.pallas_call(
        body, name="all_gather_weights",
        in_specs=[any_] * n, out_specs=[any_] * n,
        out_shape=_gathered_shapes(shards), scratch_shapes=_exchange_sems(n),
    )(*shards)


def _gathered_shapes(shards):
    return [jax.ShapeDtypeStruct((8,) + sh.shape, sh.dtype) for sh in shards]


def _exchange_sems(n):
    return [pltpu.SemaphoreType.DMA((7 * n,)), pltpu.SemaphoreType.DMA((7 * n,)), pltpu.SemaphoreType.DMA((n,))]


def _gather_phases(src, out, send_sems, recv_sems, loc_sems):
    n = len(src)
    x, y, c = lax.axis_index("x"), lax.axis_index("y"), lax.axis_index("c")
    me, sibling = (x, y, c), (x, y, 1 - c)
    chips = [(1 - x, y), (x, 1 - y), (1 - x, 1 - y)]

    def slot(a, px, py, pc):
        return out[a].at[4 * px + 2 * py + pc]

    def copy(a, k, block, to, from_src=False):
        return pltpu.make_async_remote_copy(
            src_ref=src[a] if from_src else slot(a, *block), dst_ref=slot(a, *block),
            send_sem=send_sems.at[7 * a + k], recv_sem=recv_sems.at[7 * a + k],
            device_id=to, device_id_type=MESH)

    def local(a):
        return pltpu.make_async_copy(src[a], slot(a, *me), loc_sems.at[a])

    def first(a):
        return [copy(a, 0, me, sibling, True)] + [copy(a, 1 + j, me, (*chip, c), True)
                                                  for j, chip in enumerate(chips)]

    def passed(a, j):
        return copy(a, 4 + j, (*chips[j], c), sibling)

    def start():
        for a in range(n):
            local(a).start()
            for cp in first(a):
                cp.start()

    def forward():
        for a in range(n):
            for j, chip in enumerate(chips):
                copy(a, 1 + j, (*chip, c), me).wait_recv()
                passed(a, j).start()

    def finish():
        for a in range(n):
            copy(a, 0, sibling, me).wait_recv()
            for j, chip in enumerate(chips):
                copy(a, 4 + j, (*chip, 1 - c), me).wait_recv()
        for a in range(n):
            for cp in first(a) + [passed(a, j) for j in range(3)]:
                cp.wait_send()
            local(a).wait()

    return start, forward, finish


def _scatter_phases(src, out, send_sems, recv_sems, loc_sems):
    n = len(src)
    masks = [((k >> 2) & 1, (k >> 1) & 1, k & 1) for k in range(1, 8)]
    x, y, c = lax.axis_index("x"), lax.axis_index("y"), lax.axis_index("c")
    me = 4 * x + 2 * y + c

    def copies():
        cps = []
        for a in range(n):
            cps.append(pltpu.make_async_copy(src[a].at[me], out[a].at[me], loc_sems.at[a]))
            for k, (mx, my, mc) in enumerate(masks):
                px, py, pc = _flip(x, mx), _flip(y, my), _flip(c, mc)
                cps.append(pltpu.make_async_remote_copy(
                    src_ref=src[a].at[4 * px + 2 * py + pc], dst_ref=out[a].at[me],
                    send_sem=send_sems.at[7 * a + k], recv_sem=recv_sems.at[7 * a + k],
                    device_id=(px, py, pc), device_id_type=MESH))
        return cps

    def start():
        for cp in copies():
            cp.start()

    def finish():
        for cp in copies():
            cp.wait()

    return start, finish


def _all_to_all(blocks, name):
    n = len(blocks)

    def body(*refs):
        start, finish = _scatter_phases(refs[:n], refs[n:2 * n], *refs[2 * n:])
        start()
        finish()

    any_ = pl.BlockSpec(memory_space=pl.ANY)
    return pl.pallas_call(
        body, name=name,
        in_specs=[any_] * n, out_specs=[any_] * n,
        out_shape=[jax.ShapeDtypeStruct(b.shape, b.dtype) for b in blocks], scratch_shapes=_exchange_sems(n),
    )(*blocks)


def _adamw(parts, w, m, v, br, name, tail=None):
    g8, r, c = parts.shape
    c1 = 1.0 - ADAM_B1 ** ADAM_STEP
    c2 = 1.0 - ADAM_B2 ** ADAM_STEP
    extra = [] if tail is None else [tail]

    def body(p_ref, w_ref, m_ref, v_ref, *refs):
        g_ref, d_ref, m2_ref, v2_ref = refs[len(extra):]
        g = p_ref[0].astype(F32)
        for d in range(1, g8):
            g = g + p_ref[d].astype(F32)
        if tail is not None:
            k = tail.shape[1]
            place = (_rows((k, c)) + (c - k) == _cols((k, c))).astype(F32)
            g = g + jnp.dot(refs[0][...], place, precision=HI, preferred_element_type=F32)
        m2 = ADAM_B1 * m_ref[...] + (1.0 - ADAM_B1) * g
        v2 = ADAM_B2 * v_ref[...] + (1.0 - ADAM_B2) * (g * g)
        g_ref[...] = g
        m2_ref[...] = m2
        v2_ref[...] = v2
        d_ref[...] = -ADAM_LR * ((m2 / c1) / (jnp.sqrt(v2 / c2) + ADAM_EPS) + ADAM_WD * w_ref[...])

    blk = pl.BlockSpec((br, c), lambda i: (i, 0))
    out = jax.ShapeDtypeStruct((r, c), F32)
    return pl.pallas_call(
        body, name=name, grid=(r // br,),
        in_specs=[pl.BlockSpec((g8, br, c), lambda i: (0, i, 0)), blk, blk, blk]
        + [pl.BlockSpec((br, e.shape[1]), lambda i: (i, 0)) for e in extra],
        out_specs=[blk] * 4, out_shape=[out] * 4,
        compiler_params=_params(VMEM_MID),
    )(parts, w, m, v, *extra)


def _pad_rows(a, rows):
    return jnp.pad(a, ((0, rows - a.shape[0]), (0, 0)))


_SMALL = (("g_mix", 1024), ("b_f", 8), ("g_conv_out", 512), ("g_attn_out", 512), ("g_ffn", 1024), ("g_final", 1024))
_SMALL_LEN = sum(n for _, n in _SMALL)
_SMALL_ROWS = 104


def _pack_small(vals, extra=None):
    parts = [vals[k].reshape(-1) for k, _ in _SMALL]
    if extra is not None:
        parts.append(extra.reshape(-1))
    flat = jnp.concatenate(parts)
    return jnp.pad(flat, (0, _SMALL_ROWS * LANES - flat.shape[0])).reshape(_SMALL_ROWS, LANES)


def _unpack_small(packed, shapes):
    flat = packed.reshape(-1)
    out, off = {}, 0
    for k, n in _SMALL:
        out[k] = flat[off:off + n].reshape(shapes[k])
        off += n
    return out


def _col_blocks(a):
    r, c8 = a.shape
    return jnp.transpose(a.reshape(r, 8, c8 // 8), (1, 0, 2))


def _from_col_blocks(a):
    g, r, c = a.shape
    return jnp.transpose(a, (1, 0, 2)).reshape(r, g * c)


def kernel(x, g_mix, w_in, b_f, w_conv, g_conv_out, g_attn_out, w_o, g_ffn, w_up, w_ffn_conv, w_down, g_final, loss_target, m_g_mix, m_w_in, m_b_f, m_w_conv, m_g_conv_out, m_g_attn_out, m_w_o, m_g_ffn, m_w_up, m_w_ffn_conv, m_w_down, m_g_final, v_g_mix, v_w_in, v_b_f, v_w_conv, v_g_conv_out, v_g_attn_out, v_w_o, v_g_ffn, v_w_up, v_w_ffn_conv, v_w_down, v_g_final):
    w = dict(g_mix=g_mix, w_in=w_in[0], b_f=b_f, w_conv=w_conv[0], g_conv_out=g_conv_out, g_attn_out=g_attn_out,
             w_o=w_o[0], g_ffn=g_ffn, w_up=w_up[0], w_ffn_conv=w_ffn_conv[0], w_down=w_down[0], g_final=g_final)
    m = dict(g_mix=m_g_mix, w_in=m_w_in[0], b_f=m_b_f, w_conv=m_w_conv[0], g_conv_out=m_g_conv_out,
             g_attn_out=m_g_attn_out, w_o=m_w_o[0], g_ffn=m_g_ffn, w_up=m_w_up[0], w_ffn_conv=m_w_ffn_conv[0],
             w_down=m_w_down[0], g_final=m_g_final)
    v = dict(g_mix=v_g_mix, w_in=v_w_in[0], b_f=v_b_f, w_conv=v_w_conv[0], g_conv_out=v_g_conv_out,
             g_attn_out=v_g_attn_out, w_o=v_w_o[0], g_ffn=v_g_ffn, w_up=v_w_up[0], w_ffn_conv=v_w_ffn_conv[0],
             w_down=v_w_down[0], g_final=v_g_final)
    shapes = dict(g_mix=g_mix.shape, w_in=w_in.shape, b_f=b_f.shape, w_conv=w_conv.shape,
                  g_conv_out=g_conv_out.shape, g_attn_out=g_attn_out.shape, w_o=w_o.shape, g_ffn=g_ffn.shape,
                  w_up=w_up.shape, w_ffn_conv=w_ffn_conv.shape, w_down=w_down.shape, g_final=g_final.shape)

    tm = t = 512
    tf = 256
    xs, tgt = x[0], loss_target[0]
    seq = xs.shape[0]
    assert seq % tm == 0 and seq % tf == 0
    bk = 1024 if seq % 1024 == 0 else 512

    g_in, g_conv = _all_gather([w["w_in"].astype(BF), w["w_conv"]])
    w_in_f = _from_col_blocks(g_in)
    w_zc = w_in_f[:, :3 * CONV_CH]
    w_qkv = w_in_f[:, 3 * CONV_CH:3 * CONV_CH + 3 * ATTN_W]
    w_f = jnp.pad(w_in_f[:, 3 * CONV_CH + 3 * ATTN_W:], ((0, 0), (0, LANES - N_HEADS)))
    b_f_p = jnp.pad(b_f, ((0, 0), (0, LANES - N_HEADS)))
    w_conv_p = _pad_rows(_from_col_blocks(g_conv), 8)

    h1, zc, qkv, fpre, fcol, nc, qa, ka, qt, vt = _fwd_in(
        xs, g_mix, w_zc, w_qkv, w_f, b_f_p, w_conv_p, g_conv_out, tm)
    o, lcol, g_o, g_up, g_dn, g_fc = _attn_fwd(
        qa, ka, vt, t, [w["w_o"].astype(BF), w["w_up"].astype(BF), w["w_down"].astype(BF), w["w_ffn_conv"]])
    w_o_f = g_o.reshape(D_MODEL, D_MODEL)
    w_up_f = _from_col_blocks(g_up)
    w_dn_f = g_dn.reshape(D_FF, D_MODEL)
    w_fc_p = _pad_rows(_from_col_blocks(g_fc), 8)
    x2, h2, mix = _fwd_out(xs, nc, o, g_attn_out, w_o_f, g_ffn, tm)
    u0, uc, act, dx3, dx3b, loss, dg_final = _fwd_ffn(
        h2, x2, tgt, w_up_f, w_fc_p, w_dn_f, g_final.reshape(1, D_MODEL), tf)
    loss_part = loss

    du0, dx2, dx2b, dw_fc, dg_ffn = _bwd_ffn(dx3, dx3b, u0, uc, x2, w_up_f.T, w_fc_p, w_dn_f.T, g_ffn, tf)
    dw_up = _matmul_tn(h2, du0, 1408, 2 * bk if seq % (2 * bk) == 0 else bk, "dw_up")
    dw_dn, = _matmul_tn_multi(act, [dx3b], bk, "dw_down")
    dw_o, = _matmul_tn_multi(mix, [dx2b], bk, "dw_o")
    dzc, doa, dot, qab, dw_conv, dg_attn, dg_conv = _bwd_out(
        dx2b, o, zc, fcol, lcol, qa, w_o_f, g_attn_out, g_conv_out, w_conv_p, t)
    dk, dv, dfk, dqa, dq, r_up, r_dn, r_fc, r_o = _attn_bwd(
        qab, doa, ka, qkv, qt, dot, t,
        [_col_blocks(dw_up), dw_dn.reshape(8, D_FF // 8, D_MODEL), _col_blocks(dw_fc[:3]),
         dw_o.reshape(8, D_MODEL // 8, D_MODEL)])
    dw_zc, dw_q, dw_k, dw_v = _matmul_tn_multi(h1, [dzc, dq, dk, dv], bk, "dw_in")
    dw_main = jnp.concatenate([dw_zc, dw_q, dw_k, dw_v, jnp.zeros((D_MODEL, N_HEADS), F32)], axis=1)
    gx, dfb, dg_mix, db_f, r_in = _bwd_in(xs, dx2, dzc, dq, dqa, dk, dv, dfk, fpre, w_zc, w_qkv, w_f, g_mix, tm,
                                          [_col_blocks(dw_main).astype(BF)])
    dw_f = _matmul_tn(h1, dfb, LANES, bk, "dw_in_f")
    small = dict(g_mix=dg_mix, b_f=db_f[:, :N_HEADS], g_conv_out=dg_conv, g_attn_out=dg_attn, g_ffn=dg_ffn,
                 g_final=dg_final)
    riders = jnp.concatenate([loss_part.reshape(-1), dw_f[:, :N_HEADS].reshape(-1)])
    r_conv, r_small = _all_to_all(
        [_col_blocks(dw_conv[:3]), jnp.broadcast_to(_pack_small(small, riders), (8, _SMALL_ROWS, LANES))],
        "all_to_all_grads")

    g_out, d_out, m_out, v_out = {}, {}, {}, {}
    res = _adamw(r_small, _pack_small(w), _pack_small(m), _pack_small(v), _SMALL_ROWS, "adamw_gains")
    for dst, packed in zip((g_out, d_out, m_out, v_out), res):
        dst.update(_unpack_small(packed, shapes))
    summed = res[0].reshape(-1)
    loss = summed[_SMALL_LEN]
    is_last = (4 * lax.axis_index("x") + 2 * lax.axis_index("y") + lax.axis_index("c") == 7).astype(F32)
    dw_f_sum = summed[_SMALL_LEN + 1:_SMALL_LEN + 1 + D_MODEL * N_HEADS].reshape(D_MODEL, N_HEADS) * is_last

    big = ("w_in", "w_o", "w_up", "w_down", "w_conv", "w_ffn_conv")
    recv = [r_in, r_o, r_up, r_dn, r_conv, r_fc]
    rows = dict(w_in=256, w_o=128, w_up=256, w_down=176, w_conv=3, w_ffn_conv=3)
    for a, k in enumerate(big):
        res = _adamw(recv[a], w[k], m[k], v[k], rows[k], "adamw_" + k, tail=dw_f_sum if k == "w_in" else None)
        g_out[k], d_out[k], m_out[k], v_out[k] = [r.reshape(shapes[k]) for r in res]

    order = ("g_mix", "w_in", "b_f", "w_conv", "g_conv_out", "g_attn_out", "w_o", "g_ffn", "w_up", "w_ffn_conv",
             "w_down", "g_final")
    return (loss, gx.reshape(x.shape), *[g_out[k] for k in order], *[d_out[k] for k in order],
            *[m_out[k] for k in order], *[v_out[k] for k in order])
```

```python
import jax
import jax.numpy as jnp
import numpy as np
from jax import lax
from jax.experimental import pallas as pl
from jax.experimental.pallas import tpu as pltpu

F32 = jnp.float32
BF = jnp.bfloat16
HI = lax.Precision.HIGHEST
MESH = pl.DeviceIdType.MESH

D_MODEL = 1024
CONV_CH = 512
ATTN_W = 512
N_HEADS = 8
HEAD_DIM = 64
D_FF = 2816
FF_CHUNK = 256
N_FF_CHUNKS = D_FF // FF_CHUNK
EPS = 1e-6
NEG = -1e30
LANES = 128
VMEM_BIG = 56 * 1024 * 1024
VMEM_MID = 40 * 1024 * 1024

ADAM_LR = 0.001
ADAM_B1 = 0.9
ADAM_B2 = 0.999
ADAM_EPS = 1e-08
ADAM_WD = 0.01
ADAM_STEP = 10

NT = (((1,), (1,)), ((), ()))
TN = (((0,), (0,)), ((), ()))


def _dot(a, b):
    return jnp.dot(a, b, preferred_element_type=F32)


def _dot_nt(a, b):
    return lax.dot_general(a, b, NT, preferred_element_type=F32)


def _dot_tn(a, b):
    return lax.dot_general(a, b, TN, preferred_element_type=F32)


def _params(vmem):
    return pltpu.CompilerParams(vmem_limit_bytes=vmem)


def _rows(shape):
    return lax.broadcasted_iota(jnp.int32, shape, 0)


def _cols(shape):
    return lax.broadcasted_iota(jnp.int32, shape, 1)


def _shift_down(u, prev, k):
    n = prev.shape[0]
    out = pltpu.roll(u, k, 0)
    row = _rows(u.shape)
    for r in range(k):
        out = jnp.where(row == r, prev[n - k + r:n - k + r + 1, :].astype(u.dtype), out)
    return out


def _shift_up(u, nxt, k):
    tm = u.shape[0]
    out = pltpu.roll(u, tm - k, 0)
    row = _rows(u.shape)
    for r in range(k):
        out = jnp.where(row == tm - k + r, nxt[r:r + 1, :], out)
    return out


def _conv3(u, prev, w):
    u1 = _shift_down(u, prev, 1)
    u2 = _shift_down(u, prev, 2)
    return w[0:1, :] * u2 + w[1:2, :] * u1 + w[2:3, :] * u, u1, u2


def _conv3_bwd(d, nxt, w):
    return w[2:3, :] * d + w[1:2, :] * _shift_up(d, nxt, 1) + w[0:1, :] * _shift_up(d, nxt, 2)


def _rms(x):
    return lax.rsqrt(jnp.mean(x * x, axis=-1, keepdims=True) + EPS)


def _rms_bwd(x, r, dyg):
    return r * dyg - x * (r * r * r) * jnp.mean(dyg * x, axis=-1, keepdims=True)


def _full(shape):
    nd = len(shape)
    return pl.BlockSpec(shape, lambda i, _n=nd: (0,) * _n)


ONES_LANE = 24


def _bias_scatter():
    sc = np.zeros((LANES, 2 * N_HEADS * LANES), np.float32)
    koff = N_HEADS * LANES
    for h in range(N_HEADS):
        aux = HEAD_DIM * (1 - h % 2)
        for j in range(3):
            sc[8 * j + h, LANES * h + aux + j] = 1.0
            sc[ONES_LANE, koff + LANES * h + aux + j] = 1.0
            sc[ONES_LANE, LANES * h + aux + 3 + j] = 1.0
            sc[8 * j + h, koff + LANES * h + aux + 3 + j] = -1.0
    return jnp.asarray(sc, BF)


def _split_parts(v, one):
    hi = v.astype(BF).astype(F32)
    rest = v - hi
    mid = rest.astype(BF).astype(F32)
    lo = (rest - mid).astype(BF).astype(F32)
    parts = hi + pltpu.roll(mid, 8, 1) + pltpu.roll(lo, 16, 1)
    return jnp.where(_cols(parts.shape) == ONES_LANE, one, parts).astype(BF)


def _fwd_in(x, g_mix, w_zc, w_qkv, w_f, b_f, w_conv, g_conv_out, tm):
    s = x.shape[0]
    nb = s // tm
    aw = N_HEADS * LANES

    def body(x_ref, gm_ref, wzc_ref, wqkv_ref, wf_ref, bf_ref, wc_ref, gco_ref, sc_ref,
             h1_ref, zc_ref, qkv_ref, fpre_ref, fcol_ref, nc_ref, qa_ref, ka_ref, qt_ref, vt_ref, cu_ref, cf_ref):
        i = pl.program_id(0)

        @pl.when(i == 0)
        def _():
            cu_ref[...] = jnp.zeros_like(cu_ref)
            cf_ref[...] = jnp.zeros_like(cf_ref)

        xv = x_ref[...]
        hb = (xv * _rms(xv) * gm_ref[...]).astype(BF)
        h1_ref[...] = hb
        zc = _dot(hb, wzc_ref[...])
        zc_ref[...] = zc
        qkv = _dot(hb, wqkv_ref[...])
        qkv = jnp.where(_cols(qkv.shape) < ATTN_W, qkv * 0.125, qkv)
        qkvb = qkv.astype(BF)
        qkv_ref[...] = qkvb
        qt_ref[0] = qkv[:, :ATTN_W].T.astype(BF)
        vt_ref[0] = qkv[:, 2 * ATTN_W:].T.astype(BF)

        gb, gc, xc = zc[:, :CONV_CH], zc[:, CONV_CH:2 * CONV_CH], zc[:, 2 * CONV_CH:]
        u = gc * xc
        cv, _, _ = _conv3(u, cu_ref[...], wc_ref[...])
        cu_ref[...] = u[tm - 8:, :]
        y = gb * cv
        nc_ref[...] = (y * _rms(y) * gco_ref[...]).astype(BF)

        fpre = _dot(hb, wf_ref[...]) + bf_ref[...]
        fpre_ref[...] = fpre
        logf = jnp.minimum(fpre, 0.0) - jnp.log1p(jnp.exp(-jnp.abs(fpre)))
        logf = jnp.where(_cols(logf.shape) < N_HEADS, logf, 0.0)
        tri = (_rows((tm, tm)) >= _cols((tm, tm))).astype(F32)
        fcol = jnp.dot(tri, logf, precision=HI, preferred_element_type=F32) + cf_ref[...]
        cf_ref[...] = fcol[tm - 1:tm, :]
        fcol_ref[...] = fcol

        feat = _dot(_split_parts(fcol, 1.0), sc_ref[...]).astype(BF)
        lane = _cols((tm, LANES))
        for h in range(N_HEADS):
            hp, hh = divmod(h, 2)
            own = (lane >= HEAD_DIM * hh) & (lane < HEAD_DIM * (hh + 1))
            hs = slice(LANES * h, LANES * (h + 1))
            qa_ref[:, hs] = jnp.where(own, qkvb[:, LANES * hp:LANES * (hp + 1)], feat[:, hs])
            ka_ref[:, hs] = jnp.where(own, qkvb[:, ATTN_W + LANES * hp:ATTN_W + LANES * (hp + 1)],
                                      feat[:, aw + LANES * h:aw + LANES * (h + 1)])

    blk = lambda c: pl.BlockSpec((tm, c), lambda i: (i, 0))
    return pl.pallas_call(
        body, name="fwd_in", grid=(nb,),
        in_specs=[blk(D_MODEL), _full((1, D_MODEL)), _full(w_zc.shape), _full(w_qkv.shape), _full(w_f.shape),
                  _full((1, LANES)), _full((8, CONV_CH)), _full((1, CONV_CH)), _full((LANES, 2 * aw))],
        out_specs=[blk(D_MODEL), blk(3 * CONV_CH), blk(3 * ATTN_W), blk(LANES), blk(LANES),
                   blk(CONV_CH), blk(aw), blk(aw)] + [pl.BlockSpec((1, ATTN_W, tm), lambda i: (i, 0, 0))] * 2,
        out_shape=[jax.ShapeDtypeStruct((s, D_MODEL), BF), jax.ShapeDtypeStruct((s, 3 * CONV_CH), F32),
                   jax.ShapeDtypeStruct((s, 3 * ATTN_W), BF), jax.ShapeDtypeStruct((s, LANES), F32),
                   jax.ShapeDtypeStruct((s, LANES), F32),
                   jax.ShapeDtypeStruct((s, CONV_CH), BF), jax.ShapeDtypeStruct((s, aw), BF),
                   jax.ShapeDtypeStruct((s, aw), BF)] + [jax.ShapeDtypeStruct((nb, ATTN_W, tm), BF)] * 2,
        scratch_shapes=[pltpu.VMEM((8, CONV_CH), F32), pltpu.VMEM((1, LANES), F32)],
        compiler_params=_params(VMEM_MID),
    )(x, g_mix, w_zc, w_qkv, w_f, b_f, w_conv, g_conv_out, _bias_scatter())


def _pipeline_masked_last(last, produce, consume, buf_a, buf_b):
    produce(0, buf_a)

    def two_blocks(j, carry):
        blk = 2 * j
        produce(blk + 1, buf_b)
        consume(blk, buf_a, False)
        produce(blk + 2, buf_a)
        consume(blk + 1, buf_b, False)
        return carry

    lax.fori_loop(0, last // 2, two_blocks, 0)

    @pl.when(last % 2 == 0)
    def _():
        consume(last, buf_a, True)

    @pl.when(last % 2 == 1)
    def _():
        produce(last, buf_b)
        consume(last - 1, buf_a, False)
        consume(last, buf_b, True)


def _attn_fwd(qa, ka, vt, t, shards):
    s = qa.shape[0]
    nb = s // t
    n = len(shards)

    def body(qa_ref, ka_any, vt_any, *refs):
        shard_refs, (o_ref, l_ref), gathered = refs[:n], refs[n:n + 2], refs[n + 2:2 * n + 2]
        ka_scr, vt_scr, m_scr, acc_scr, ea_scr, eb_scr, sem = refs[2 * n + 2:2 * n + 9]
        i = pl.program_id(0)
        start, forward, finish = _gather_phases(shard_refs, gathered, *refs[2 * n + 9:])
        pl.when(i == 0)(start)
        pl.when(i == nb // 2)(forward)
        _load_weights(i, [(ka_any, ka_scr), (vt_any, vt_scr)], sem)

        causal_t = _rows((t, t)) <= _cols((t, t))
        row = _rows((LANES, t))
        lrows = jnp.zeros((LANES, t), F32)
        for hp in range(N_HEADS // 2):
            ps = slice(hp * LANES, (hp + 1) * LANES)
            m_scr[...] = jnp.full(m_scr.shape, NEG, F32)
            acc_scr[...] = jnp.zeros(acc_scr.shape, F32)
            heads = [dict(hh=hh, hs=slice((2 * hp + hh) * LANES, (2 * hp + hh + 1) * LANES),
                          aux=HEAD_DIM * (1 - hh), own=(row >= HEAD_DIM * hh) & (row < HEAD_DIM * (hh + 1)),
                          fill=jnp.where(row == HEAD_DIM * (1 - hh), 1.0, 0.0).astype(BF),
                          qh=qa_ref[:, (2 * hp + hh) * LANES:(2 * hp + hh + 1) * LANES]) for hh in range(2)]

            def scores(kb, dst, heads=heads):
                rs = pl.ds(pl.multiple_of(kb * t, t), t)
                for hd in heads:
                    dst[hd["hh"]] = _dot_nt(ka_scr[rs, hd["hs"]], hd["qh"])

            def consume(kb, src, masked, ps=ps, heads=heads):
                vt2 = vt_scr[kb, ps, :]
                m_olds = [m_scr[hd["hh"]:hd["hh"] + 1, :] for hd in heads]
                accs = [acc_scr[hd["hh"]] for hd in heads]
                m_news, acc_news = [], []
                for hd, m_old, acc in zip(heads, m_olds, accs):
                    e = src[hd["hh"]]
                    if masked:
                        e = jnp.where(causal_t, e, NEG)
                    m_new = jnp.maximum(m_old, jnp.max(e, axis=0, keepdims=True))
                    p = jnp.exp(e - m_new).astype(BF)
                    vta = jnp.where(hd["own"], vt2, hd["fill"])
                    acc_news.append(jnp.exp(m_old - m_new) * acc + _dot(vta, p))
                    m_news.append(m_new)
                for hd, m_new, acc in zip(heads, m_news, acc_news):
                    acc_scr[hd["hh"]] = acc
                    m_scr[hd["hh"]:hd["hh"] + 1, :] = m_new

            _pipeline_masked_last(i, scores, consume, ea_scr, eb_scr)

            o_pair = None
            for hd in heads:
                hh = hd["hh"]
                acc = acc_scr[hh]
                denom = acc[hd["aux"]:hd["aux"] + 1, :]
                o_h = acc / denom
                lrows = jnp.where(row == 2 * hp + hh, m_scr[hh:hh + 1, :] + jnp.log(denom), lrows)
                o_pair = o_h if hh == 0 else jnp.where(row < HEAD_DIM, o_pair, o_h)
            o_ref[:, ps] = o_pair.T
        l_ref[...] = lrows.T
        pl.when(i == nb - 1)(finish)

    any_ = pl.BlockSpec(memory_space=pl.ANY)
    return pl.pallas_call(
        body, name="attn_fwd", grid=(nb,),
        in_specs=[pl.BlockSpec((t, N_HEADS * LANES), lambda i: (i, 0)), any_, any_] + [any_] * n,
        out_specs=[pl.BlockSpec((t, ATTN_W), lambda i: (i, 0)), pl.BlockSpec((t, LANES), lambda i: (i, 0))]
        + [any_] * n,
        out_shape=[jax.ShapeDtypeStruct((s, ATTN_W), F32), jax.ShapeDtypeStruct((s, LANES), F32)]
        + _gathered_shapes(shards),
        scratch_shapes=[pltpu.VMEM(ka.shape, BF), pltpu.VMEM(vt.shape, BF), pltpu.VMEM((2, t), F32),
                        pltpu.VMEM((2, LANES, t), F32), pltpu.VMEM((2, t, t), F32), pltpu.VMEM((2, t, t), F32),
                        pltpu.SemaphoreType.DMA((2,))] + _exchange_sems(n),
        compiler_params=_params(VMEM_BIG),
    )(qa, ka, vt, *shards)


def _fwd_out(x, nc, o, g_attn_out, w_o, g_ffn, tm):
    s = x.shape[0]

    def body(x_ref, nc_ref, o_ref, ga_ref, wo_ref, gf_ref, x2_ref, h2_ref, mix_ref):
        ov = o_ref[...]
        na = (ov * _rms(ov) * ga_ref[...]).astype(BF)
        ncv = nc_ref[...]
        mix_ref[:, :CONV_CH] = ncv
        mix_ref[:, CONV_CH:] = na
        x2 = x_ref[...] + _dot(ncv, wo_ref[:CONV_CH, :]) + _dot(na, wo_ref[CONV_CH:, :])
        x2_ref[...] = x2
        h2_ref[...] = (x2 * _rms(x2) * gf_ref[...]).astype(BF)

    blk = lambda c: pl.BlockSpec((tm, c), lambda i: (i, 0))
    return pl.pallas_call(
        body, name="fwd_out", grid=(s // tm,),
        in_specs=[blk(D_MODEL), blk(CONV_CH), blk(ATTN_W), _full((1, ATTN_W)), _full(w_o.shape), _full((1, D_MODEL))],
        out_specs=[blk(D_MODEL), blk(D_MODEL), blk(D_MODEL)],
        out_shape=[jax.ShapeDtypeStruct((s, D_MODEL), F32), jax.ShapeDtypeStruct((s, D_MODEL), BF),
                   jax.ShapeDtypeStruct((s, D_MODEL), BF)],
        compiler_params=_params(VMEM_MID),
    )(x, nc, o, g_attn_out, w_o, g_ffn)


def _load_weights(i, pairs, sem):
    @pl.when(i == 0)
    def _():
        cps = [pltpu.make_async_copy(src, dst, sem.at[n]) for n, (src, dst) in enumerate(pairs)]
        for cp in cps:
            cp.start()
        for cp in cps:
            cp.wait()


def _ff_cols(j):
    return (slice(j * FF_CHUNK, (j + 1) * FF_CHUNK), slice(D_FF + j * FF_CHUNK, D_FF + (j + 1) * FF_CHUNK))


def _fwd_ffn(h2, x2, tgt, w_up, w_ffn_conv, w_dn, g_final, tm):
    s = x2.shape[0]

    def body(h2_ref, x2_ref, tgt_ref, wfc_ref, gfin_ref, wup_any, wdn_any,
             u0_ref, uc_ref, act_ref, dx3_ref, dx3b_ref, loss_ref, dgfin_ref,
             wup, wdn, carry, sem):
        i = pl.program_id(0)
        _load_weights(i, [(wup_any, wup), (wdn_any, wdn)], sem)

        @pl.when(i == 0)
        def _():
            carry[...] = jnp.zeros_like(carry)
            loss_ref[...] = jnp.zeros_like(loss_ref)
            dgfin_ref[...] = jnp.zeros_like(dgfin_ref)

        hb = h2_ref[...]

        def up(j):
            return [_dot(hb, wup[:, cc]) for cc in _ff_cols(j)]

        nxt = up(0)
        down = None
        for j in range(N_FF_CHUNKS):
            cur = nxt
            if j + 1 < N_FF_CHUNKS:
                nxt = up(j + 1)
            parts = []
            for cc, u0 in zip(_ff_cols(j), cur):
                u0_ref[:, cc] = u0.astype(BF)
                uu, _, _ = _conv3(u0, carry[:, cc], wfc_ref[:, cc])
                carry[:, cc] = u0[tm - 8:, :]
                uc_ref[:, cc] = uu.astype(BF)
                parts.append(uu)
            ua, ug = parts
            act = (ug * jax.nn.sigmoid(ug) * ua).astype(BF)
            ca = _ff_cols(j)[0]
            act_ref[:, ca] = act
            part = _dot(act, wdn[ca, :])
            down = part if down is None else down + part

        x3 = x2_ref[...] + down
        r3 = _rms(x3)
        gfin = gfin_ref[...]
        xn = x3 * r3
        diff = xn * gfin - tgt_ref[...]
        loss_ref[...] += jnp.sum(jnp.sum(diff * diff, axis=-1, keepdims=True), axis=0, keepdims=True) * (0.5 / D_MODEL)
        dy = diff * (1.0 / D_MODEL)
        dgfin_ref[...] += jnp.sum(dy * xn, axis=0, keepdims=True)
        dx3 = _rms_bwd(x3, r3, dy * gfin)
        dx3_ref[...] = dx3
        dx3b_ref[...] = dx3.astype(BF)

    blk = lambda c: pl.BlockSpec((tm, c), lambda i: (i, 0))
    any_ = pl.BlockSpec(memory_space=pl.ANY)
    return pl.pallas_call(
        body, name="fwd_ffn", grid=(s // tm,),
        in_specs=[blk(D_MODEL), blk(D_MODEL), blk(D_MODEL), _full((8, 2 * D_FF)), _full((1, D_MODEL)), any_, any_],
        out_specs=[blk(2 * D_FF), blk(2 * D_FF), blk(D_FF), blk(D_MODEL), blk(D_MODEL), _full((1, 1)),
                   _full((1, D_MODEL))],
        out_shape=[jax.ShapeDtypeStruct((s, 2 * D_FF), BF), jax.ShapeDtypeStruct((s, 2 * D_FF), BF),
                   jax.ShapeDtypeStruct((s, D_FF), BF),
                   jax.ShapeDtypeStruct((s, D_MODEL), F32), jax.ShapeDtypeStruct((s, D_MODEL), BF),
                   jax.ShapeDtypeStruct((1, 1), F32), jax.ShapeDtypeStruct((1, D_MODEL), F32)],
        scratch_shapes=[pltpu.VMEM(w_up.shape, BF), pltpu.VMEM(w_dn.shape, BF), pltpu.VMEM((8, 2 * D_FF), F32),
                        pltpu.SemaphoreType.DMA((2,))],
        compiler_params=_params(VMEM_BIG),
    )(h2, x2, tgt, w_ffn_conv, g_final, w_up, w_dn)


def _bwd_ffn(dx3, dx3b, u0, uc, x2, w_up, w_ffn_conv, w_dn, g_ffn, tm):
    s = x2.shape[0]
    nb = s // tm

    def body(dx3_ref, dx3b_ref, u0_ref, uc_ref, x2_ref, wfc_ref, gf_ref, wup_any, wdn_any,
             du0_ref, dx2_ref, dx2b_ref, dwfc_ref, dgf_ref,
             wup, wdn, carry, sem):
        i = pl.program_id(0)
        _load_weights(i, [(wup_any, wup), (wdn_any, wdn)], sem)

        @pl.when(i == 0)
        def _():
            carry[...] = jnp.zeros_like(carry)
            dwfc_ref[...] = jnp.zeros_like(dwfc_ref)
            dgf_ref[...] = jnp.zeros_like(dgf_ref)

        db = dx3b_ref[...]

        def dact_of(j):
            return _dot(db, wdn[:, _ff_cols(j)[0]])

        nxt = dact_of(0)
        dh2 = None
        for j in range(N_FF_CHUNKS):
            ca, cg = _ff_cols(j)
            dact = nxt
            if j + 1 < N_FF_CHUNKS:
                nxt = dact_of(j + 1)
            ua = uc_ref[:, ca].astype(F32)
            ug = uc_ref[:, cg].astype(F32)
            sg = jax.nn.sigmoid(ug)
            da = dact * (ug * sg)
            dg = dact * ua * (sg * (1.0 + ug * (1.0 - sg)))
            for cc, d in ((ca, da), (cg, dg)):
                nxt_rows = carry[:, cc]
                d1 = _shift_up(d, nxt_rows, 1)
                d2 = _shift_up(d, nxt_rows, 2)
                u0c = u0_ref[:, cc].astype(F32)
                w = wfc_ref[:, cc]
                dwfc_ref[0:1, cc] += jnp.sum(d2 * u0c, axis=0, keepdims=True)
                dwfc_ref[1:2, cc] += jnp.sum(d1 * u0c, axis=0, keepdims=True)
                dwfc_ref[2:3, cc] += jnp.sum(d * u0c, axis=0, keepdims=True)
                du0 = (w[2:3, :] * d + w[1:2, :] * d1 + w[0:1, :] * d2).astype(BF)
                carry[:, cc] = d[:8, :]
                du0_ref[:, cc] = du0
                part = _dot(du0, wup[cc, :])
                dh2 = part if dh2 is None else dh2 + part

        x2v = x2_ref[...]
        r2 = _rms(x2v)
        dgf_ref[...] += jnp.sum(dh2 * (x2v * r2), axis=0, keepdims=True)
        dx2 = dx3_ref[...] + _rms_bwd(x2v, r2, dh2 * gf_ref[...])
        dx2_ref[...] = dx2
        dx2b_ref[...] = dx2.astype(BF)

    blk = lambda c: pl.BlockSpec((tm, c), lambda i: (nb - 1 - i, 0))
    any_ = pl.BlockSpec(memory_space=pl.ANY)
    return pl.pallas_call(
        body, name="bwd_ffn", grid=(nb,),
        in_specs=[blk(D_MODEL), blk(D_MODEL), blk(2 * D_FF), blk(2 * D_FF), blk(D_MODEL), _full((8, 2 * D_FF)),
                  _full((1, D_MODEL)), any_, any_],
        out_specs=[blk(2 * D_FF), blk(D_MODEL), blk(D_MODEL), _full((8, 2 * D_FF)), _full((1, D_MODEL))],
        out_shape=[jax.ShapeDtypeStruct((s, 2 * D_FF), BF), jax.ShapeDtypeStruct((s, D_MODEL), F32),
                   jax.ShapeDtypeStruct((s, D_MODEL), BF), jax.ShapeDtypeStruct((8, 2 * D_FF), F32),
                   jax.ShapeDtypeStruct((1, D_MODEL), F32)],
        scratch_shapes=[pltpu.VMEM(w_up.shape, BF), pltpu.VMEM(w_dn.shape, BF), pltpu.VMEM((8, 2 * D_FF), F32),
                        pltpu.SemaphoreType.DMA((2,))],
        compiler_params=_params(VMEM_BIG),
    )(dx3, dx3b, u0, uc, x2, w_ffn_conv, g_ffn, w_up, w_dn)


def _bwd_out(dx2b, o, zc, fcol, lcol, qa, w_o, g_attn_out, g_conv_out, w_conv, tm):
    s = o.shape[0]
    nb = s // tm
    aw = N_HEADS * LANES

    def body(dx2b_ref, o_ref, zc_ref, halo_ref, fcol_ref, lcol_ref, qa_ref, wo_ref, ga_ref, gco_ref, wc_ref, sc_ref,
             dzc_ref, doa_ref, dot_ref, qab_ref, dwc_ref, dga_ref, dgco_ref, carry):
        i = pl.program_id(0)
        rb = nb - 1 - i

        @pl.when(i == 0)
        def _():
            carry[...] = jnp.zeros_like(carry)
            dwc_ref[...] = jnp.zeros_like(dwc_ref)
            dga_ref[...] = jnp.zeros_like(dga_ref)
            dgco_ref[...] = jnp.zeros_like(dgco_ref)

        dmix = _dot_nt(dx2b_ref[...], wo_ref[...])
        dnc, dna = dmix[:, :CONV_CH], dmix[:, CONV_CH:]

        ov = o_ref[...]
        ra = _rms(ov)
        dga_ref[...] += jnp.sum(dna * (ov * ra), axis=0, keepdims=True)
        do = _rms_bwd(ov, ra, dna * ga_ref[...])
        dob = do.astype(BF)
        dot_ref[0] = do.T.astype(BF)
        sel = (_rows((ATTN_W, LANES)) // HEAD_DIM == _cols((ATTN_W, LANES))).astype(F32)
        delta = jnp.dot(do * ov, sel, precision=HI, preferred_element_type=F32)
        featc = _dot(_split_parts(fcol_ref[...] - lcol_ref[...], 1.0), sc_ref[...]).astype(BF)
        featd = _dot(_split_parts(-delta, 0.0), sc_ref[...]).astype(BF)
        lane = _cols((tm, LANES))
        for h in range(N_HEADS):
            hp, hh = divmod(h, 2)
            own = (lane >= HEAD_DIM * hh) & (lane < HEAD_DIM * (hh + 1))
            hs = slice(LANES * h, LANES * (h + 1))
            qab_ref[:, hs] = jnp.where(own, qa_ref[:, hs], featc[:, hs])
            doa_ref[:, hs] = jnp.where(own, dob[:, LANES * hp:LANES * (hp + 1)], featd[:, hs])

        zc_v = zc_ref[...]
        gb, gc, xc = zc_v[:, :CONV_CH], zc_v[:, CONV_CH:2 * CONV_CH], zc_v[:, 2 * CONV_CH:]
        hal = halo_ref[...] * (rb > 0).astype(F32)
        u = gc * xc
        prev = hal[:, CONV_CH:2 * CONV_CH] * hal[:, 2 * CONV_CH:]
        wc = wc_ref[...]
        cv, u1, u2 = _conv3(u, prev, wc)
        y = gb * cv
        rc = _rms(y)
        dgco_ref[...] += jnp.sum(dnc * (y * rc), axis=0, keepdims=True)
        dy = _rms_bwd(y, rc, dnc * gco_ref[...])
        dcv = dy * gb
        dwc_ref[0:1, :] += jnp.sum(dcv * u2, axis=0, keepdims=True)
        dwc_ref[1:2, :] += jnp.sum(dcv * u1, axis=0, keepdims=True)
        dwc_ref[2:3, :] += jnp.sum(dcv * u, axis=0, keepdims=True)
        du = _conv3_bwd(dcv, carry[...], wc)
        carry[...] = dcv[:8, :]
        dzc_ref[:, :CONV_CH] = (dy * cv).astype(BF)
        dzc_ref[:, CONV_CH:2 * CONV_CH] = (du * xc).astype(BF)
        dzc_ref[:, 2 * CONV_CH:] = (du * gc).astype(BF)

    blk = lambda c: pl.BlockSpec((tm, c), lambda i: (nb - 1 - i, 0))
    halo = pl.BlockSpec((8, 3 * CONV_CH), lambda i: (jnp.maximum((nb - 1 - i) * (tm // 8) - 1, 0), 0))
    tr = pl.BlockSpec((1, ATTN_W, tm), lambda i: (nb - 1 - i, 0, 0))
    return pl.pallas_call(
        body, name="bwd_out", grid=(nb,),
        in_specs=[blk(D_MODEL), blk(ATTN_W), blk(3 * CONV_CH), halo, blk(LANES), blk(LANES), blk(aw),
                  _full(w_o.shape), _full((1, ATTN_W)), _full((1, CONV_CH)), _full((8, CONV_CH)), _full((LANES, aw))],
        out_specs=[blk(3 * CONV_CH), blk(aw), tr, blk(aw),
                   _full((8, CONV_CH)), _full((1, ATTN_W)), _full((1, CONV_CH))],
        out_shape=[jax.ShapeDtypeStruct((s, 3 * CONV_CH), BF),
                   jax.ShapeDtypeStruct((s, aw), BF), jax.ShapeDtypeStruct((nb, ATTN_W, tm), BF),
                   jax.ShapeDtypeStruct((s, aw), BF),
                   jax.ShapeDtypeStruct((8, CONV_CH), F32), jax.ShapeDtypeStruct((1, ATTN_W), F32),
                   jax.ShapeDtypeStruct((1, CONV_CH), F32)],
        scratch_shapes=[pltpu.VMEM((8, CONV_CH), F32)],
        compiler_params=_params(VMEM_MID),
    )(dx2b, o, zc, zc, fcol, lcol, qa, w_o, g_attn_out, g_conv_out, w_conv, _bias_scatter()[:, :aw])


def _attn_bwd(qab, doa, ka, qkv, qt, dot, t, blocks):
    s = qab.shape[0]
    nb = s // t
    npair = N_HEADS // 2
    n = len(blocks)

    def body(ka_ref, v_ref, qab_ref, doa_ref, qt_ref, dot_ref, *refs):
        block_refs, (dk_ref, dv_ref, dfk_ref, dqa_ref, dq_ref) = refs[:n], refs[n:n + 5]
        received = refs[n + 5:2 * n + 5]
        acck, accv = refs[2 * n + 5:2 * n + 7]
        hp = pl.program_id(0)
        j = pl.program_id(1)
        start, finish = _scatter_phases(block_refs, received, *refs[2 * n + 7:])
        pl.when((hp == 0) & (j == 0))(start)

        @pl.when(j == 0)
        def _():
            dqa_ref[...] = jnp.zeros_like(dqa_ref)

        causal = _cols((t, t)) <= _rows((t, t))
        row = _rows((LANES, t))
        lane = _cols((t, LANES))
        acck[...] = jnp.zeros(acck.shape, F32)
        accv[...] = jnp.zeros(accv.shape, F32)
        v2 = v_ref[...]
        heads = []
        for hh in range(2):
            aux = HEAD_DIM * (1 - hh)
            ownl = (lane >= HEAD_DIM * hh) & (lane < HEAD_DIM * (hh + 1))
            ones3 = jnp.where((lane >= aux) & (lane < aux + 3), 1.0, 0.0).astype(BF)
            heads.append(dict(hh=hh, hs=slice(hh * LANES, (hh + 1) * LANES), aux=aux,
                              own=(row >= HEAD_DIM * hh) & (row < HEAD_DIM * (hh + 1)),
                              fill=jnp.where(row == aux, 1.0, 0.0).astype(BF),
                              kh=ka_ref[:, hh * LANES:(hh + 1) * LANES], vh=jnp.where(ownl, v2, ones3)))

        def tile(qb, masked):
            rs = pl.ds(pl.multiple_of(qb * t, t), t)
            qt2 = qt_ref[qb]
            dot2 = dot_ref[qb]
            es = [_dot_nt(qab_ref[rs, hd["hs"]], hd["kh"]) for hd in heads]
            dps = [_dot_nt(doa_ref[rs, hd["hs"]], hd["vh"]) for hd in heads]
            aks = [acck[hd["hh"]] for hd in heads]
            avs = [accv[hd["hh"]] for hd in heads]
            dqs = [dqa_ref[rs, hd["hs"]] for hd in heads]
            outs = []
            for hd, e, dp, ak, av, dq in zip(heads, es, dps, aks, avs, dqs):
                if masked:
                    e = jnp.where(causal, e, NEG)
                p = jnp.exp(e)
                ds = (p * dp).astype(BF)
                outs.append((ak + _dot(jnp.where(hd["own"], qt2, hd["fill"]), ds), av + _dot(dot2, p.astype(BF)),
                             dq + _dot(ds, hd["kh"])))
            for hd, (ak, av, dq) in zip(heads, outs):
                acck[hd["hh"]] = ak
                accv[hd["hh"]] = av
                dqa_ref[rs, hd["hs"]] = dq

        def loop_body(qb, carry):
            tile(qb, False)
            return carry

        tile(j, True)
        lax.fori_loop(j + 1, nb, loop_body, 0)
        dk_pair = dv_pair = None
        dfrows = jnp.zeros((LANES, t), F32)
        for hd in heads:
            ak, av = acck[hd["hh"]], accv[hd["hh"]]
            dfrows = jnp.where(row == 2 * hp + hd["hh"], -ak[hd["aux"]:hd["aux"] + 1, :], dfrows)
            dk_pair = ak if hd["hh"] == 0 else jnp.where(row < HEAD_DIM, dk_pair, ak)
            dv_pair = av if hd["hh"] == 0 else jnp.where(row < HEAD_DIM, dv_pair, av)
        dk_ref[...] = dk_pair.T.astype(BF)
        dv_ref[...] = dv_pair.T.astype(BF)
        dfk_ref[0] = dfrows.T

        @pl.when(j == nb - 1)
        def _():
            for r in range(nb):
                rows = slice(r * t, (r + 1) * t)
                dq_ref[rows, :] = (jnp.where(lane < HEAD_DIM, dqa_ref[rows, :LANES], dqa_ref[rows, LANES:])
                                   * 0.125).astype(BF)

        pl.when((hp == npair - 1) & (j == nb - 1))(finish)

    pair_cols = pl.BlockSpec((s, 2 * LANES), lambda hp, j: (0, hp))
    pair_rows = pl.BlockSpec((nb, LANES, t), lambda hp, j: (0, hp, 0))
    any_ = pl.BlockSpec(memory_space=pl.ANY)
    return pl.pallas_call(
        body, name="attn_bwd", grid=(npair, nb),
        in_specs=[pl.BlockSpec((t, 2 * LANES), lambda hp, j: (j, hp)),
                  pl.BlockSpec((t, LANES), lambda hp, j: (j, 2 * npair + hp)),
                  pair_cols, pair_cols, pair_rows, pair_rows] + [any_] * n,
        out_specs=[pl.BlockSpec((t, LANES), lambda hp, j: (j, hp)), pl.BlockSpec((t, LANES), lambda hp, j: (j, hp)),
                   pl.BlockSpec((1, t, LANES), lambda hp, j: (hp, j, 0)), pair_cols,
                   pl.BlockSpec((s, LANES), lambda hp, j: (0, hp))] + [any_] * n,
        out_shape=[jax.ShapeDtypeStruct((s, ATTN_W), BF), jax.ShapeDtypeStruct((s, ATTN_W), BF),
                   jax.ShapeDtypeStruct((npair, s, LANES), F32), jax.ShapeDtypeStruct((s, N_HEADS * LANES), F32),
                   jax.ShapeDtypeStruct((s, ATTN_W), BF)]
        + [jax.ShapeDtypeStruct(b.shape, b.dtype) for b in blocks],
        scratch_shapes=[pltpu.VMEM((2, LANES, t), F32), pltpu.VMEM((2, LANES, t), F32)] + _exchange_sems(n),
        compiler_params=_params(VMEM_BIG),
    )(ka, qkv, qab, doa, qt, dot, *blocks)


def _bwd_in(x, dx2, dzc, dq, dqa, dk, dv, dfk, fpre, h1, w_zc, w_qkv, w_f, g_mix, tm, blocks):
    s = x.shape[0]
    nb = s // tm
    n = len(blocks)

    def body(x_ref, dx2_ref, dzc_ref, dq_ref, dqa_ref, dk_ref, dv_ref, dfk_ref, fpre_ref, h1_ref, wzc_ref, wqkv_ref,
             wf_ref, gm_ref, *refs):
        block_refs, (gx_ref, dwf_ref, dgm_ref, dbf_ref), received = refs[:n], refs[n:n + 4], refs[n + 4:2 * n + 4]
        carry = refs[2 * n + 4]
        i = pl.program_id(0)
        start, finish = _scatter_phases(block_refs, received, *refs[2 * n + 5:])
        pl.when(i == 0)(start)

        @pl.when(i == 0)
        def _():
            carry[...] = jnp.zeros_like(carry)
            dwf_ref[...] = jnp.zeros_like(dwf_ref)
            dgm_ref[...] = jnp.zeros_like(dgm_ref)
            dbf_ref[...] = jnp.zeros_like(dbf_ref)

        lane = _cols((tm, LANES))
        dfq = jnp.zeros((tm, LANES), F32)
        for h in range(N_HEADS):
            aux = LANES * h + HEAD_DIM * (1 - h % 2)
            dfq = jnp.where(lane == h, dqa_ref[:, aux:aux + 1], dfq)

        triu = (_rows((tm, tm)) <= _cols((tm, tm))).astype(F32)
        df_cum = dfq + ((dfk_ref[0] + dfk_ref[1]) + (dfk_ref[2] + dfk_ref[3]))
        dlogf = jnp.dot(triu, df_cum, precision=HI, preferred_element_type=F32) + carry[...]
        carry[...] = dlogf[0:1, :]
        fpre = fpre_ref[...]
        df = jnp.where(_cols(fpre.shape) < N_HEADS, dlogf / (1.0 + jnp.exp(fpre)), 0.0)
        dbf_ref[...] += jnp.sum(df, axis=0, keepdims=True)
        dfb = df.astype(BF)
        dwf_ref[...] += _dot_tn(h1_ref[...], dfb)

        dh1 = _dot_nt(dzc_ref[...], wzc_ref[...])
        dh1 += _dot_nt(dq_ref[...], wqkv_ref[:, :ATTN_W])
        dh1 += _dot_nt(dk_ref[...], wqkv_ref[:, ATTN_W:2 * ATTN_W])
        dh1 += _dot_nt(dv_ref[...], wqkv_ref[:, 2 * ATTN_W:])
        dh1 += _dot_nt(dfb, wf_ref[...])
        xv = x_ref[...]
        r1 = _rms(xv)
        dgm_ref[...] += jnp.sum(dh1 * (xv * r1), axis=0, keepdims=True)
        gx_ref[...] = dx2_ref[...] + _rms_bwd(xv, r1, dh1 * gm_ref[...])
        pl.when(i == nb - 1)(finish)

    blk = lambda c: pl.BlockSpec((tm, c), lambda i: (nb - 1 - i, 0))
    any_ = pl.BlockSpec(memory_space=pl.ANY)
    return pl.pallas_call(
        body, name="bwd_in", grid=(nb,),
        in_specs=[blk(D_MODEL), blk(D_MODEL), blk(3 * CONV_CH), blk(ATTN_W), blk(N_HEADS * LANES), blk(ATTN_W),
                  blk(ATTN_W), pl.BlockSpec((N_HEADS // 2, tm, LANES), lambda i: (0, nb - 1 - i, 0)), blk(LANES),
                  blk(D_MODEL), _full(w_zc.shape), _full(w_qkv.shape), _full(w_f.shape), _full((1, D_MODEL))]
        + [any_] * n,
        out_specs=[blk(D_MODEL), _full((D_MODEL, LANES)), _full((1, D_MODEL)), _full((1, LANES))] + [any_] * n,
        out_shape=[jax.ShapeDtypeStruct((s, D_MODEL), F32), jax.ShapeDtypeStruct((D_MODEL, LANES), F32),
                   jax.ShapeDtypeStruct((1, D_MODEL), F32), jax.ShapeDtypeStruct((1, LANES), F32)]
        + [jax.ShapeDtypeStruct(b.shape, b.dtype) for b in blocks],
        scratch_shapes=[pltpu.VMEM((1, LANES), F32)] + _exchange_sems(n),
        compiler_params=_params(VMEM_MID),
    )(x, dx2, dzc, dq, dqa, dk, dv, dfk, fpre, h1, w_zc, w_qkv, w_f, g_mix, *blocks)


def _matmul_tn(a, b, bn, bk, name):
    s, m = a.shape
    n = b.shape[1]

    def body(a_ref, b_ref, o_ref):
        @pl.when(pl.program_id(1) == 0)
        def _():
            o_ref[...] = jnp.zeros_like(o_ref)

        o_ref[...] += _dot_tn(a_ref[...], b_ref[...])

    return pl.pallas_call(
        body, name=name, grid=(n // bn, s // bk),
        in_specs=[pl.BlockSpec((bk, m), lambda jn, k: (k, 0)), pl.BlockSpec((bk, bn), lambda jn, k: (k, jn))],
        out_specs=pl.BlockSpec((m, bn), lambda jn, k: (0, jn)),
        out_shape=jax.ShapeDtypeStruct((m, n), F32),
        compiler_params=_params(VMEM_MID),
    )(a, b)


def _matmul_tn_multi(a, bs, bk, name):
    s, m = a.shape
    nb = len(bs)

    def body(a_ref, *refs):
        b_refs, o_refs = refs[:nb], refs[nb:]

        @pl.when(pl.program_id(0) == 0)
        def _():
            for o_ref in o_refs:
                o_ref[...] = jnp.zeros_like(o_ref)

        at = a_ref[...].T
        for b_ref, o_ref in zip(b_refs, o_refs):
            o_ref[...] += _dot(at, b_ref[...])

    return pl.pallas_call(
        body, name=name, grid=(s // bk,),
        in_specs=[pl.BlockSpec((bk, m), lambda k: (k, 0))]
        + [pl.BlockSpec((bk, b.shape[1]), lambda k: (k, 0)) for b in bs],
        out_specs=[_full((m, b.shape[1])) for b in bs],
        out_shape=[jax.ShapeDtypeStruct((m, b.shape[1]), F32) for b in bs],
        compiler_params=_params(VMEM_BIG),
    )(a, *bs)


def _flip(v, bit):
    return 1 - v if bit else v


def _all_gather(shards):
    n = len(shards)

    def body(*refs):
        start, forward, finish = _gather_phases(refs[:n], refs[n:2 * n], *refs[2 * n:])
        start()
        forward()
        finish()

    any_ = pl.BlockSpec(memory_space=pl.ANY)
    return pl.pallas_call(
        body, name="all_gather_weights",
        in_specs=[any_] * n, out_specs=[any_] * n,
        out_shape=_gathered_shapes(shards), scratch_shapes=_exchange_sems(n),
    )(*shards)


def _gathered_shapes(shards):
    return [jax.ShapeDtypeStruct((8,) + sh.shape, sh.dtype) for sh in shards]


def _exchange_sems(n):
    return [pltpu.SemaphoreType.DMA((7 * n,)), pltpu.SemaphoreType.DMA((7 * n,)), pltpu.SemaphoreType.DMA((n,))]


def _gather_phases(src, out, send_sems, recv_sems, loc_sems):
    n = len(src)
    x, y, c = lax.axis_index("x"), lax.axis_index("y"), lax.axis_index("c")
    me, sibling = (x, y, c), (x, y, 1 - c)
    chips = [(1 - x, y), (x, 1 - y), (1 - x, 1 - y)]

    def slot(a, px, py, pc):
        return out[a].at[4 * px + 2 * py + pc]

    def copy(a, k, block, to, from_src=False):
        return pltpu.make_async_remote_copy(
            src_ref=src[a] if from_src else slot(a, *block), dst_ref=slot(a, *block),
            send_sem=send_sems.at[7 * a + k], recv_sem=recv_sems.at[7 * a + k],
            device_id=to, device_id_type=MESH)

    def local(a):
        return pltpu.make_async_copy(src[a], slot(a, *me), loc_sems.at[a])

    def first(a):
        return [copy(a, 0, me, sibling, True)] + [copy(a, 1 + j, me, (*chip, c), True)
                                                  for j, chip in enumerate(chips)]

    def passed(a, j):
        return copy(a, 4 + j, (*chips[j], c), sibling)

    def start():
        for a in range(n):
            local(a).start()
            for cp in first(a):
                cp.start()

    def forward():
        for a in range(n):
            for j, chip in enumerate(chips):
                copy(a, 1 + j, (*chip, c), me).wait_recv()
                passed(a, j).start()

    def finish():
        for a in range(n):
            copy(a, 0, sibling, me).wait_recv()
            for j, chip in enumerate(chips):
                copy(a, 4 + j, (*chip, 1 - c), me).wait_recv()
        for a in range(n):
            for cp in first(a) + [passed(a, j) for j in range(3)]:
                cp.wait_send()
            local(a).wait()

    return start, forward, finish


def _scatter_phases(src, out, send_sems, recv_sems, loc_sems):
    n = len(src)
    masks = [((k >> 2) & 1, (k >> 1) & 1, k & 1) for k in range(1, 8)]
    x, y, c = lax.axis_index("x"), lax.axis_index("y"), lax.axis_index("c")
    me = 4 * x + 2 * y + c

    def copies():
        cps = []
        for a in range(n):
            cps.append(pltpu.make_async_copy(src[a].at[me], out[a].at[me], loc_sems.at[a]))
            for k, (mx, my, mc) in enumerate(masks):
                px, py, pc = _flip(x, mx), _flip(y, my), _flip(c, mc)
                cps.append(pltpu.make_async_remote_copy(
                    src_ref=src[a].at[4 * px + 2 * py + pc], dst_ref=out[a].at[me],
                    send_sem=send_sems.at[7 * a + k], recv_sem=recv_sems.at[7 * a + k],
                    device_id=(px, py, pc), device_id_type=MESH))
        return cps

    def start():
        for cp in copies():
            cp.start()

    def finish():
        for cp in copies():
            cp.wait()

    return start, finish


def _all_to_all(blocks, name):
    n = len(blocks)

    def body(*refs):
        start, finish = _scatter_phases(refs[:n], refs[n:2 * n], *refs[2 * n:])
        start()
        finish()

    any_ = pl.BlockSpec(memory_space=pl.ANY)
    return pl.pallas_call(
        body, name=name,
        in_specs=[any_] * n, out_specs=[any_] * n,
        out_shape=[jax.ShapeDtypeStruct(b.shape, b.dtype) for b in blocks], scratch_shapes=_exchange_sems(n),
    )(*blocks)


def _adamw(parts, w, m, v, br, name, tail=None):
    g8, r, c = parts.shape
    c1 = 1.0 - ADAM_B1 ** ADAM_STEP
    c2 = 1.0 - ADAM_B2 ** ADAM_STEP
    extra = [] if tail is None else [tail]

    def body(p_ref, w_ref, m_ref, v_ref, *refs):
        g_ref, d_ref, m2_ref, v2_ref = refs[len(extra):]
        g = p_ref[0].astype(F32)
        for d in range(1, g8):
            g = g + p_ref[d].astype(F32)
        if tail is not None:
            k = tail.shape[1]
            place = (_rows((k, c)) + (c - k) == _cols((k, c))).astype(F32)
            g = g + jnp.dot(refs[0][...], place, precision=HI, preferred_element_type=F32)
        m2 = ADAM_B1 * m_ref[...] + (1.0 - ADAM_B1) * g
        v2 = ADAM_B2 * v_ref[...] + (1.0 - ADAM_B2) * (g * g)
        g_ref[...] = g
        m2_ref[...] = m2
        v2_ref[...] = v2
        d_ref[...] = -ADAM_LR * ((m2 / c1) / (jnp.sqrt(v2 / c2) + ADAM_EPS) + ADAM_WD * w_ref[...])

    blk = pl.BlockSpec((br, c), lambda i: (i, 0))
    out = jax.ShapeDtypeStruct((r, c), F32)
    return pl.pallas_call(
        body, name=name, grid=(r // br,),
        in_specs=[pl.BlockSpec((g8, br, c), lambda i: (0, i, 0)), blk, blk, blk]
        + [pl.BlockSpec((br, e.shape[1]), lambda i: (i, 0)) for e in extra],
        out_specs=[blk] * 4, out_shape=[out] * 4,
        compiler_params=_params(VMEM_MID),
    )(parts, w, m, v, *extra)


def _pad_rows(a, rows):
    return jnp.pad(a, ((0, rows - a.shape[0]), (0, 0)))


_SMALL = (("g_mix", 1024), ("b_f", 8), ("g_conv_out", 512), ("g_attn_out", 512), ("g_ffn", 1024), ("g_final", 1024))
_SMALL_LEN = sum(n for _, n in _SMALL)
_SMALL_ROWS = 104


def _pack_small(vals, extra=None):
    parts = [vals[k].reshape(-1) for k, _ in _SMALL]
    if extra is not None:
        parts.append(extra.reshape(-1))
    flat = jnp.concatenate(parts)
    return jnp.pad(flat, (0, _SMALL_ROWS * LANES - flat.shape[0])).reshape(_SMALL_ROWS, LANES)


def _unpack_small(packed, shapes):
    flat = packed.reshape(-1)
    out, off = {}, 0
    for k, n in _SMALL:
        out[k] = flat[off:off + n].reshape(shapes[k])
        off += n
    return out


def _col_blocks(a):
    r, c8 = a.shape
    return jnp.transpose(a.reshape(r, 8, c8 // 8), (1, 0, 2))


def _from_col_blocks(a):
    g, r, c = a.shape
    return jnp.transpose(a, (1, 0, 2)).reshape(r, g * c)


def kernel(x, g_mix, w_in, b_f, w_conv, g_conv_out, g_attn_out, w_o, g_ffn, w_up, w_ffn_conv, w_down, g_final, loss_target, m_g_mix, m_w_in, m_b_f, m_w_conv, m_g_conv_out, m_g_attn_out, m_w_o, m_g_ffn, m_w_up, m_w_ffn_conv, m_w_down, m_g_final, v_g_mix, v_w_in, v_b_f, v_w_conv, v_g_conv_out, v_g_attn_out, v_w_o, v_g_ffn, v_w_up, v_w_ffn_conv, v_w_down, v_g_final):
    w = dict(g_mix=g_mix, w_in=w_in[0], b_f=b_f, w_conv=w_conv[0], g_conv_out=g_conv_out, g_attn_out=g_attn_out,
             w_o=w_o[0], g_ffn=g_ffn, w_up=w_up[0], w_ffn_conv=w_ffn_conv[0], w_down=w_down[0], g_final=g_final)
    m = dict(g_mix=m_g_mix, w_in=m_w_in[0], b_f=m_b_f, w_conv=m_w_conv[0], g_conv_out=m_g_conv_out,
             g_attn_out=m_g_attn_out, w_o=m_w_o[0], g_ffn=m_g_ffn, w_up=m_w_up[0], w_ffn_conv=m_w_ffn_conv[0],
             w_down=m_w_down[0], g_final=m_g_final)
    v = dict(g_mix=v_g_mix, w_in=v_w_in[0], b_f=v_b_f, w_conv=v_w_conv[0], g_conv_out=v_g_conv_out,
             g_attn_out=v_g_attn_out, w_o=v_w_o[0], g_ffn=v_g_ffn, w_up=v_w_up[0], w_ffn_conv=v_w_ffn_conv[0],
             w_down=v_w_down[0], g_final=v_g_final)
    shapes = dict(g_mix=g_mix.shape, w_in=w_in.shape, b_f=b_f.shape, w_conv=w_conv.shape,
                  g_conv_out=g_conv_out.shape, g_attn_out=g_attn_out.shape, w_o=w_o.shape, g_ffn=g_ffn.shape,
                  w_up=w_up.shape, w_ffn_conv=w_ffn_conv.shape, w_down=w_down.shape, g_final=g_final.shape)

    tm = t = 512
    tf = 256
    xs, tgt = x[0], loss_target[0]
    seq = xs.shape[0]
    assert seq % tm == 0 and seq % tf == 0
    bk = 1024 if seq % 1024 == 0 else 512

    g_in, g_conv = _all_gather([w["w_in"].astype(BF), w["w_conv"]])
    w_in_f = _from_col_blocks(g_in)
    w_zc = w_in_f[:, :3 * CONV_CH]
    w_qkv = w_in_f[:, 3 * CONV_CH:3 * CONV_CH + 3 * ATTN_W]
    w_f = jnp.pad(w_in_f[:, 3 * CONV_CH + 3 * ATTN_W:], ((0, 0), (0, LANES - N_HEADS)))
    b_f_p = jnp.pad(b_f, ((0, 0), (0, LANES - N_HEADS)))
    w_conv_p = _pad_rows(_from_col_blocks(g_conv), 8)

    h1, zc, qkv, fpre, fcol, nc, qa, ka, qt, vt = _fwd_in(
        xs, g_mix, w_zc, w_qkv, w_f, b_f_p, w_conv_p, g_conv_out, tm)
    o, lcol, g_o, g_up, g_dn, g_fc = _attn_fwd(
        qa, ka, vt, t, [w["w_o"].astype(BF), w["w_up"].astype(BF), w["w_down"].astype(BF), w["w_ffn_conv"]])
    w_o_f = g_o.reshape(D_MODEL, D_MODEL)
    w_up_f = _from_col_blocks(g_up)
    w_dn_f = g_dn.reshape(D_FF, D_MODEL)
    w_fc_p = _pad_rows(_from_col_blocks(g_fc), 8)
    x2, h2, mix = _fwd_out(xs, nc, o, g_attn_out, w_o_f, g_ffn, tm)
    u0, uc, act, dx3, dx3b, loss, dg_final = _fwd_ffn(
        h2, x2, tgt, w_up_f, w_fc_p, w_dn_f, g_final.reshape(1, D_MODEL), tf)
    loss_part = loss

    du0, dx2, dx2b, dw_fc, dg_ffn = _bwd_ffn(dx3, dx3b, u0, uc, x2, w_up_f.T, w_fc_p, w_dn_f.T, g_ffn, tf)
    dw_up = _matmul_tn(h2, du0, 1408, 2 * bk if seq % (2 * bk) == 0 else bk, "dw_up")
    dw_dn, = _matmul_tn_multi(act, [dx3b], bk, "dw_down")
    dw_o, = _matmul_tn_multi(mix, [dx2b], bk, "dw_o")
    dzc, doa, dot, qab, dw_conv, dg_attn, dg_conv = _bwd_out(
        dx2b, o, zc, fcol, lcol, qa, w_o_f, g_attn_out, g_conv_out, w_conv_p, t)
    dk, dv, dfk, dqa, dq, r_up, r_dn, r_fc, r_o = _attn_bwd(
        qab, doa, ka, qkv, qt, dot, t,
        [_col_blocks(dw_up), dw_dn.reshape(8, D_FF // 8, D_MODEL), _col_blocks(dw_fc[:3]),
         dw_o.reshape(8, D_MODEL // 8, D_MODEL)])
    dw_zc, dw_q, dw_k, dw_v = _matmul_tn_multi(h1, [dzc, dq, dk, dv], bk, "dw_in")
    dw_main = jnp.concatenate([dw_zc, dw_q, dw_k, dw_v, jnp.zeros((D_MODEL, N_HEADS), F32)], axis=1)
    gx, dw_f, dg_mix, db_f, r_in = _bwd_in(xs, dx2, dzc, dq, dqa, dk, dv, dfk, fpre, h1, w_zc, w_qkv, w_f, g_mix, tm,
                                           [_col_blocks(dw_main).astype(BF)])
    small = dict(g_mix=dg_mix, b_f=db_f[:, :N_HEADS], g_conv_out=dg_conv, g_attn_out=dg_attn, g_ffn=dg_ffn,
                 g_final=dg_final)
    riders = jnp.concatenate([loss_part.reshape(-1), dw_f[:, :N_HEADS].reshape(-1)])
    r_conv, r_small = _all_to_all(
        [_col_blocks(dw_conv[:3]), jnp.broadcast_to(_pack_small(small, riders), (8, _SMALL_ROWS, LANES))],
        "all_to_all_grads")

    g_out, d_out, m_out, v_out = {}, {}, {}, {}
    res = _adamw(r_small, _pack_small(w), _pack_small(m), _pack_small(v), _SMALL_ROWS, "adamw_gains")
    for dst, packed in zip((g_out, d_out, m_out, v_out), res):
        dst.update(_unpack_small(packed, shapes))
    summed = res[0].reshape(-1)
    loss = summed[_SMALL_LEN]
    is_last = (4 * lax.axis_index("x") + 2 * lax.axis_index("y") + lax.axis_index("c") == 7).astype(F32)
    dw_f_sum = summed[_SMALL_LEN + 1:_SMALL_LEN + 1 + D_MODEL * N_HEADS].reshape(D_MODEL, N_HEADS) * is_last

    big = ("w_in", "w_o", "w_up", "w_down", "w_conv", "w_ffn_conv")
    recv = [r_in, r_o, r_up, r_dn, r_conv, r_fc]
    rows = dict(w_in=256, w_o=128, w_up=256, w_down=176, w_conv=3, w_ffn_conv=3)
    for a, k in enumerate(big):
        res = _adamw(recv[a], w[k], m[k], v[k], rows[k], "adamw_" + k, tail=dw_f_sum if k == "w_in" else None)
        g_out[k], d_out[k], m_out[k], v_out[k] = [r.reshape(shapes[k]) for r in res]

    order = ("g_mix", "w_in", "b_f", "w_conv", "g_conv_out", "g_attn_out", "w_o", "g_ffn", "w_up", "w_ffn_conv",
             "w_down", "g_final")
    return (loss, gx.reshape(x.shape), *[g_out[k] for k in order], *[d_out[k] for k in order],
            *[m_out[k] for k in order], *[v_out[k] for k in order])
```

```python
import jax
import jax.numpy as jnp
import numpy as np
from jax import lax
from jax.experimental import pallas as pl
from jax.experimental.pallas import tpu as pltpu

F32 = jnp.float32
BF = jnp.bfloat16
HI = lax.Precision.HIGHEST
MESH = pl.DeviceIdType.MESH

D_MODEL = 1024
CONV_CH = 512
ATTN_W = 512
N_HEADS = 8
HEAD_DIM = 64
D_FF = 2816
FF_CHUNK = 256
N_FF_CHUNKS = D_FF // FF_CHUNK
EPS = 1e-6
NEG = -1e30
LANES = 128
VMEM_BIG = 56 * 1024 * 1024
VMEM_MID = 40 * 1024 * 1024

ADAM_LR = 0.001
ADAM_B1 = 0.9
ADAM_B2 = 0.999
ADAM_EPS = 1e-08
ADAM_WD = 0.01
ADAM_STEP = 10

NT = (((1,), (1,)), ((), ()))
TN = (((0,), (0,)), ((), ()))


def _dot(a, b):
    return jnp.dot(a, b, preferred_element_type=F32)


def _dot_nt(a, b):
    return lax.dot_general(a, b, NT, preferred_element_type=F32)


def _dot_tn(a, b):
    return lax.dot_general(a, b, TN, preferred_element_type=F32)


def _params(vmem):
    return pltpu.CompilerParams(vmem_limit_bytes=vmem)


def _rows(shape):
    return lax.broadcasted_iota(jnp.int32, shape, 0)


def _cols(shape):
    return lax.broadcasted_iota(jnp.int32, shape, 1)


def _shift_down(u, prev, k):
    n = prev.shape[0]
    out = pltpu.roll(u, k, 0)
    row = _rows(u.shape)
    for r in range(k):
        out = jnp.where(row == r, prev[n - k + r:n - k + r + 1, :].astype(u.dtype), out)
    return out


def _shift_up(u, nxt, k):
    tm = u.shape[0]
    out = pltpu.roll(u, tm - k, 0)
    row = _rows(u.shape)
    for r in range(k):
        out = jnp.where(row == tm - k + r, nxt[r:r + 1, :], out)
    return out


def _conv3(u, prev, w):
    u1 = _shift_down(u, prev, 1)
    u2 = _shift_down(u, prev, 2)
    return w[0:1, :] * u2 + w[1:2, :] * u1 + w[2:3, :] * u, u1, u2


def _conv3_bwd(d, nxt, w):
    return w[2:3, :] * d + w[1:2, :] * _shift_up(d, nxt, 1) + w[0:1, :] * _shift_up(d, nxt, 2)


def _rms(x):
    return lax.rsqrt(jnp.mean(x * x, axis=-1, keepdims=True) + EPS)


def _rms_bwd(x, r, dyg):
    return r * dyg - x * (r * r * r) * jnp.mean(dyg * x, axis=-1, keepdims=True)


def _full(shape):
    nd = len(shape)
    return pl.BlockSpec(shape, lambda i, _n=nd: (0,) * _n)


ONES_LANE = 24


def _bias_scatter():
    sc = np.zeros((LANES, 2 * N_HEADS * LANES), np.float32)
    koff = N_HEADS * LANES
    for h in range(N_HEADS):
        aux = HEAD_DIM * (1 - h % 2)
        for j in range(3):
            sc[8 * j + h, LANES * h + aux + j] = 1.0
            sc[ONES_LANE, koff + LANES * h + aux + j] = 1.0
            sc[ONES_LANE, LANES * h + aux + 3 + j] = 1.0
            sc[8 * j + h, koff + LANES * h + aux + 3 + j] = -1.0
    return jnp.asarray(sc, BF)


def _split_parts(v, one):
    hi = v.astype(BF).astype(F32)
    rest = v - hi
    mid = rest.astype(BF).astype(F32)
    lo = (rest - mid).astype(BF).astype(F32)
    parts = hi + pltpu.roll(mid, 8, 1) + pltpu.roll(lo, 16, 1)
    return jnp.where(_cols(parts.shape) == ONES_LANE, one, parts).astype(BF)


def _fwd_in(x, g_mix, w_zc, w_qkv, w_f, b_f, w_conv, g_conv_out, tm):
    s = x.shape[0]
    nb = s // tm
    aw = N_HEADS * LANES

    def body(x_ref, gm_ref, wzc_ref, wqkv_ref, wf_ref, bf_ref, wc_ref, gco_ref, sc_ref,
             h1_ref, zc_ref, qkv_ref, fpre_ref, fcol_ref, nc_ref, qa_ref, ka_ref, qt_ref, vt_ref, cu_ref, cf_ref):
        i = pl.program_id(0)

        @pl.when(i == 0)
        def _():
            cu_ref[...] = jnp.zeros_like(cu_ref)
            cf_ref[...] = jnp.zeros_like(cf_ref)

        xv = x_ref[...]
        hb = (xv * _rms(xv) * gm_ref[...]).astype(BF)
        h1_ref[...] = hb
        zc = _dot(hb, wzc_ref[...])
        zc_ref[...] = zc
        qkv = _dot(hb, wqkv_ref[...])
        qkv = jnp.where(_cols(qkv.shape) < ATTN_W, qkv * 0.125, qkv)
        qkvb = qkv.astype(BF)
        qkv_ref[...] = qkvb
        qt_ref[0] = qkv[:, :ATTN_W].T.astype(BF)
        vt_ref[0] = qkv[:, 2 * ATTN_W:].T.astype(BF)

        gb, gc, xc = zc[:, :CONV_CH], zc[:, CONV_CH:2 * CONV_CH], zc[:, 2 * CONV_CH:]
        u = gc * xc
        cv, _, _ = _conv3(u, cu_ref[...], wc_ref[...])
        cu_ref[...] = u[tm - 8:, :]
        y = gb * cv
        nc_ref[...] = (y * _rms(y) * gco_ref[...]).astype(BF)

        fpre = _dot(hb, wf_ref[...]) + bf_ref[...]
        fpre_ref[...] = fpre
        logf = jnp.minimum(fpre, 0.0) - jnp.log1p(jnp.exp(-jnp.abs(fpre)))
        logf = jnp.where(_cols(logf.shape) < N_HEADS, logf, 0.0)
        tri = (_rows((tm, tm)) >= _cols((tm, tm))).astype(F32)
        fcol = jnp.dot(tri, logf, precision=HI, preferred_element_type=F32) + cf_ref[...]
        cf_ref[...] = fcol[tm - 1:tm, :]
        fcol_ref[...] = fcol

        feat = _dot(_split_parts(fcol, 1.0), sc_ref[...]).astype(BF)
        lane = _cols((tm, LANES))
        for h in range(N_HEADS):
            hp, hh = divmod(h, 2)
            own = (lane >= HEAD_DIM * hh) & (lane < HEAD_DIM * (hh + 1))
            hs = slice(LANES * h, LANES * (h + 1))
            qa_ref[:, hs] = jnp.where(own, qkvb[:, LANES * hp:LANES * (hp + 1)], feat[:, hs])
            ka_ref[:, hs] = jnp.where(own, qkvb[:, ATTN_W + LANES * hp:ATTN_W + LANES * (hp + 1)],
                                      feat[:, aw + LANES * h:aw + LANES * (h + 1)])

    blk = lambda c: pl.BlockSpec((tm, c), lambda i: (i, 0))
    return pl.pallas_call(
        body, name="fwd_in", grid=(nb,),
        in_specs=[blk(D_MODEL), _full((1, D_MODEL)), _full(w_zc.shape), _full(w_qkv.shape), _full(w_f.shape),
                  _full((1, LANES)), _full((8, CONV_CH)), _full((1, CONV_CH)), _full((LANES, 2 * aw))],
        out_specs=[blk(D_MODEL), blk(3 * CONV_CH), blk(3 * ATTN_W), blk(LANES), blk(LANES),
                   blk(CONV_CH), blk(aw), blk(aw)] + [pl.BlockSpec((1, ATTN_W, tm), lambda i: (i, 0, 0))] * 2,
        out_shape=[jax.ShapeDtypeStruct((s, D_MODEL), BF), jax.ShapeDtypeStruct((s, 3 * CONV_CH), F32),
                   jax.ShapeDtypeStruct((s, 3 * ATTN_W), BF), jax.ShapeDtypeStruct((s, LANES), F32),
                   jax.ShapeDtypeStruct((s, LANES), F32),
                   jax.ShapeDtypeStruct((s, CONV_CH), BF), jax.ShapeDtypeStruct((s, aw), BF),
                   jax.ShapeDtypeStruct((s, aw), BF)] + [jax.ShapeDtypeStruct((nb, ATTN_W, tm), BF)] * 2,
        scratch_shapes=[pltpu.VMEM((8, CONV_CH), F32), pltpu.VMEM((1, LANES), F32)],
        compiler_params=_params(VMEM_MID),
    )(x, g_mix, w_zc, w_qkv, w_f, b_f, w_conv, g_conv_out, _bias_scatter())


def _pipeline_masked_last(last, produce, consume, buf_a, buf_b):
    produce(0, buf_a)

    def two_blocks(j, carry):
        blk = 2 * j
        produce(blk + 1, buf_b)
        consume(blk, buf_a, False)
        produce(blk + 2, buf_a)
        consume(blk + 1, buf_b, False)
        return carry

    lax.fori_loop(0, last // 2, two_blocks, 0)

    @pl.when(last % 2 == 0)
    def _():
        consume(last, buf_a, True)

    @pl.when(last % 2 == 1)
    def _():
        produce(last, buf_b)
        consume(last - 1, buf_a, False)
        consume(last, buf_b, True)


def _attn_fwd(qa, ka, vt, t, shards):
    s = qa.shape[0]
    nb = s // t
    n = len(shards)

    def body(qa_ref, ka_any, vt_any, *refs):
        shard_refs, (o_ref, l_ref), gathered = refs[:n], refs[n:n + 2], refs[n + 2:2 * n + 2]
        ka_scr, vt_scr, m_scr, acc_scr, ea_scr, eb_scr, sem = refs[2 * n + 2:2 * n + 9]
        i = pl.program_id(0)
        start, forward, finish = _gather_phases(shard_refs, gathered, *refs[2 * n + 9:])
        pl.when(i == 0)(start)
        pl.when(i == nb // 2)(forward)
        _load_weights(i, [(ka_any, ka_scr), (vt_any, vt_scr)], sem)

        causal_t = _rows((t, t)) <= _cols((t, t))
        row = _rows((LANES, t))
        lrows = jnp.zeros((LANES, t), F32)
        for hp in range(N_HEADS // 2):
            ps = slice(hp * LANES, (hp + 1) * LANES)
            m_scr[...] = jnp.full(m_scr.shape, NEG, F32)
            acc_scr[...] = jnp.zeros(acc_scr.shape, F32)
            heads = [dict(hh=hh, hs=slice((2 * hp + hh) * LANES, (2 * hp + hh + 1) * LANES),
                          aux=HEAD_DIM * (1 - hh), own=(row >= HEAD_DIM * hh) & (row < HEAD_DIM * (hh + 1)),
                          fill=jnp.where(row == HEAD_DIM * (1 - hh), 1.0, 0.0).astype(BF),
                          qh=qa_ref[:, (2 * hp + hh) * LANES:(2 * hp + hh + 1) * LANES]) for hh in range(2)]

            def scores(kb, dst, heads=heads):
                rs = pl.ds(pl.multiple_of(kb * t, t), t)
                for hd in heads:
                    dst[hd["hh"]] = _dot_nt(ka_scr[rs, hd["hs"]], hd["qh"])

            def consume(kb, src, masked, ps=ps, heads=heads):
                vt2 = vt_scr[kb, ps, :]
                m_olds = [m_scr[hd["hh"]:hd["hh"] + 1, :] for hd in heads]
                accs = [acc_scr[hd["hh"]] for hd in heads]
                m_news, acc_news = [], []
                for hd, m_old, acc in zip(heads, m_olds, accs):
                    e = src[hd["hh"]]
                    if masked:
                        e = jnp.where(causal_t, e, NEG)
                    m_new = jnp.maximum(m_old, jnp.max(e, axis=0, keepdims=True))
                    p = jnp.exp(e - m_new).astype(BF)
                    vta = jnp.where(hd["own"], vt2, hd["fill"])
                    acc_news.append(jnp.exp(m_old - m_new) * acc + _dot(vta, p))
                    m_news.append(m_new)
                for hd, m_new, acc in zip(heads, m_news, acc_news):
                    acc_scr[hd["hh"]] = acc
                    m_scr[hd["hh"]:hd["hh"] + 1, :] = m_new

            _pipeline_masked_last(i, scores, consume, ea_scr, eb_scr)

            o_pair = None
            for hd in heads:
                hh = hd["hh"]
                acc = acc_scr[hh]
                denom = acc[hd["aux"]:hd["aux"] + 1, :]
                o_h = acc / denom
                lrows = jnp.where(row == 2 * hp + hh, m_scr[hh:hh + 1, :] + jnp.log(denom), lrows)
                o_pair = o_h if hh == 0 else jnp.where(row < HEAD_DIM, o_pair, o_h)
            o_ref[:, ps] = o_pair.T
        l_ref[...] = lrows.T
        pl.when(i == nb - 1)(finish)

    any_ = pl.BlockSpec(memory_space=pl.ANY)
    return pl.pallas_call(
        body, name="attn_fwd", grid=(nb,),
        in_specs=[pl.BlockSpec((t, N_HEADS * LANES), lambda i: (i, 0)), any_, any_] + [any_] * n,
        out_specs=[pl.BlockSpec((t, ATTN_W), lambda i: (i, 0)), pl.BlockSpec((t, LANES), lambda i: (i, 0))]
        + [any_] * n,
        out_shape=[jax.ShapeDtypeStruct((s, ATTN_W), F32), jax.ShapeDtypeStruct((s, LANES), F32)]
        + _gathered_shapes(shards),
        scratch_shapes=[pltpu.VMEM(ka.shape, BF), pltpu.VMEM(vt.shape, BF), pltpu.VMEM((2, t), F32),
                        pltpu.VMEM((2, LANES, t), F32), pltpu.VMEM((2, t, t), F32), pltpu.VMEM((2, t, t), F32),
                        pltpu.SemaphoreType.DMA((2,))] + _exchange_sems(n),
        compiler_params=_params(VMEM_BIG),
    )(qa, ka, vt, *shards)


def _fwd_out(x, nc, o, g_attn_out, w_o, g_ffn, tm):
    s = x.shape[0]

    def body(x_ref, nc_ref, o_ref, ga_ref, wo_ref, gf_ref, x2_ref, h2_ref, mix_ref):
        ov = o_ref[...]
        na = (ov * _rms(ov) * ga_ref[...]).astype(BF)
        ncv = nc_ref[...]
        mix_ref[:, :CONV_CH] = ncv
        mix_ref[:, CONV_CH:] = na
        x2 = x_ref[...] + _dot(ncv, wo_ref[:CONV_CH, :]) + _dot(na, wo_ref[CONV_CH:, :])
        x2_ref[...] = x2
        h2_ref[...] = (x2 * _rms(x2) * gf_ref[...]).astype(BF)

    blk = lambda c: pl.BlockSpec((tm, c), lambda i: (i, 0))
    return pl.pallas_call(
        body, name="fwd_out", grid=(s // tm,),
        in_specs=[blk(D_MODEL), blk(CONV_CH), blk(ATTN_W), _full((1, ATTN_W)), _full(w_o.shape), _full((1, D_MODEL))],
        out_specs=[blk(D_MODEL), blk(D_MODEL), blk(D_MODEL)],
        out_shape=[jax.ShapeDtypeStruct((s, D_MODEL), F32), jax.ShapeDtypeStruct((s, D_MODEL), BF),
                   jax.ShapeDtypeStruct((s, D_MODEL), BF)],
        compiler_params=_params(VMEM_MID),
    )(x, nc, o, g_attn_out, w_o, g_ffn)


def _load_weights(i, pairs, sem):
    @pl.when(i == 0)
    def _():
        cps = [pltpu.make_async_copy(src, dst, sem.at[n]) for n, (src, dst) in enumerate(pairs)]
        for cp in cps:
            cp.start()
        for cp in cps:
            cp.wait()


def _ff_cols(j):
    return (slice(j * FF_CHUNK, (j + 1) * FF_CHUNK), slice(D_FF + j * FF_CHUNK, D_FF + (j + 1) * FF_CHUNK))


def _fwd_ffn(h2, x2, tgt, w_up, w_ffn_conv, w_dn, g_final, tm):
    s = x2.shape[0]

    def body(h2_ref, x2_ref, tgt_ref, wfc_ref, gfin_ref, wup_any, wdn_any,
             u0_ref, uc_ref, act_ref, dx3_ref, dx3b_ref, loss_ref, dgfin_ref,
             wup, wdn, carry, sem):
        i = pl.program_id(0)
        _load_weights(i, [(wup_any, wup), (wdn_any, wdn)], sem)

        @pl.when(i == 0)
        def _():
            carry[...] = jnp.zeros_like(carry)
            loss_ref[...] = jnp.zeros_like(loss_ref)
            dgfin_ref[...] = jnp.zeros_like(dgfin_ref)

        hb = h2_ref[...]

        def up(j):
            return [_dot(hb, wup[:, cc]) for cc in _ff_cols(j)]

        nxt = up(0)
        down = None
        for j in range(N_FF_CHUNKS):
            cur = nxt
            if j + 1 < N_FF_CHUNKS:
                nxt = up(j + 1)
            parts = []
            for cc, u0 in zip(_ff_cols(j), cur):
                u0_ref[:, cc] = u0.astype(BF)
                uu, _, _ = _conv3(u0, carry[:, cc], wfc_ref[:, cc])
                carry[:, cc] = u0[tm - 8:, :]
                uc_ref[:, cc] = uu.astype(BF)
                parts.append(uu)
            ua, ug = parts
            act = (ug * jax.nn.sigmoid(ug) * ua).astype(BF)
            ca = _ff_cols(j)[0]
            act_ref[:, ca] = act
            part = _dot(act, wdn[ca, :])
            down = part if down is None else down + part

        x3 = x2_ref[...] + down
        r3 = _rms(x3)
        gfin = gfin_ref[...]
        xn = x3 * r3
        diff = xn * gfin - tgt_ref[...]
        loss_ref[...] += jnp.sum(jnp.sum(diff * diff, axis=-1, keepdims=True), axis=0, keepdims=True) * (0.5 / D_MODEL)
        dy = diff * (1.0 / D_MODEL)
        dgfin_ref[...] += jnp.sum(dy * xn, axis=0, keepdims=True)
        dx3 = _rms_bwd(x3, r3, dy * gfin)
        dx3_ref[...] = dx3
        dx3b_ref[...] = dx3.astype(BF)

    blk = lambda c: pl.BlockSpec((tm, c), lambda i: (i, 0))
    any_ = pl.BlockSpec(memory_space=pl.ANY)
    return pl.pallas_call(
        body, name="fwd_ffn", grid=(s // tm,),
        in_specs=[blk(D_MODEL), blk(D_MODEL), blk(D_MODEL), _full((8, 2 * D_FF)), _full((1, D_MODEL)), any_, any_],
        out_specs=[blk(2 * D_FF), blk(2 * D_FF), blk(D_FF), blk(D_MODEL), blk(D_MODEL), _full((1, 1)),
                   _full((1, D_MODEL))],
        out_shape=[jax.ShapeDtypeStruct((s, 2 * D_FF), BF), jax.ShapeDtypeStruct((s, 2 * D_FF), BF),
                   jax.ShapeDtypeStruct((s, D_FF), BF),
                   jax.ShapeDtypeStruct((s, D_MODEL), F32), jax.ShapeDtypeStruct((s, D_MODEL), BF),
                   jax.ShapeDtypeStruct((1, 1), F32), jax.ShapeDtypeStruct((1, D_MODEL), F32)],
        scratch_shapes=[pltpu.VMEM(w_up.shape, BF), pltpu.VMEM(w_dn.shape, BF), pltpu.VMEM((8, 2 * D_FF), F32),
                        pltpu.SemaphoreType.DMA((2,))],
        compiler_params=_params(VMEM_BIG),
    )(h2, x2, tgt, w_ffn_conv, g_final, w_up, w_dn)


def _bwd_ffn(dx3, dx3b, u0, uc, x2, w_up, w_ffn_conv, w_dn, g_ffn, tm):
    s = x2.shape[0]
    nb = s // tm

    def body(dx3_ref, dx3b_ref, u0_ref, uc_ref, x2_ref, wfc_ref, gf_ref, wup_any, wdn_any,
             du0_ref, dx2_ref, dx2b_ref, dwfc_ref, dgf_ref,
             wup, wdn, carry, sem):
        i = pl.program_id(0)
        _load_weights(i, [(wup_any, wup), (wdn_any, wdn)], sem)

        @pl.when(i == 0)
        def _():
            carry[...] = jnp.zeros_like(carry)
            dwfc_ref[...] = jnp.zeros_like(dwfc_ref)
            dgf_ref[...] = jnp.zeros_like(dgf_ref)

        db = dx3b_ref[...]

        def dact_of(j):
            return _dot(db, wdn[:, _ff_cols(j)[0]])

        nxt = dact_of(0)
        dh2 = None
        for j in range(N_FF_CHUNKS):
            ca, cg = _ff_cols(j)
            dact = nxt
            if j + 1 < N_FF_CHUNKS:
                nxt = dact_of(j + 1)
            ua = uc_ref[:, ca].astype(F32)
            ug = uc_ref[:, cg].astype(F32)
            sg = jax.nn.sigmoid(ug)
            da = dact * (ug * sg)
            dg = dact * ua * (sg * (1.0 + ug * (1.0 - sg)))
            for cc, d in ((ca, da), (cg, dg)):
                nxt_rows = carry[:, cc]
                d1 = _shift_up(d, nxt_rows, 1)
                d2 = _shift_up(d, nxt_rows, 2)
                u0c = u0_ref[:, cc].astype(F32)
                w = wfc_ref[:, cc]
                dwfc_ref[0:1, cc] += jnp.sum(d2 * u0c, axis=0, keepdims=True)
                dwfc_ref[1:2, cc] += jnp.sum(d1 * u0c, axis=0, keepdims=True)
                dwfc_ref[2:3, cc] += jnp.sum(d * u0c, axis=0, keepdims=True)
                du0 = (w[2:3, :] * d + w[1:2, :] * d1 + w[0:1, :] * d2).astype(BF)
                carry[:, cc] = d[:8, :]
                du0_ref[:, cc] = du0
                part = _dot(du0, wup[cc, :])
                dh2 = part if dh2 is None else dh2 + part

        x2v = x2_ref[...]
        r2 = _rms(x2v)
        dgf_ref[...] += jnp.sum(dh2 * (x2v * r2), axis=0, keepdims=True)
        dx2 = dx3_ref[...] + _rms_bwd(x2v, r2, dh2 * gf_ref[...])
        dx2_ref[...] = dx2
        dx2b_ref[...] = dx2.astype(BF)

    blk = lambda c: pl.BlockSpec((tm, c), lambda i: (nb - 1 - i, 0))
    any_ = pl.BlockSpec(memory_space=pl.ANY)
    return pl.pallas_call(
        body, name="bwd_ffn", grid=(nb,),
        in_specs=[blk(D_MODEL), blk(D_MODEL), blk(2 * D_FF), blk(2 * D_FF), blk(D_MODEL), _full((8, 2 * D_FF)),
                  _full((1, D_MODEL)), any_, any_],
        out_specs=[blk(2 * D_FF), blk(D_MODEL), blk(D_MODEL), _full((8, 2 * D_FF)), _full((1, D_MODEL))],
        out_shape=[jax.ShapeDtypeStruct((s, 2 * D_FF), BF), jax.ShapeDtypeStruct((s, D_MODEL), F32),
                   jax.ShapeDtypeStruct((s, D_MODEL), BF), jax.ShapeDtypeStruct((8, 2 * D_FF), F32),
                   jax.ShapeDtypeStruct((1, D_MODEL), F32)],
        scratch_shapes=[pltpu.VMEM(w_up.shape, BF), pltpu.VMEM(w_dn.shape, BF), pltpu.VMEM((8, 2 * D_FF), F32),
                        pltpu.SemaphoreType.DMA((2,))],
        compiler_params=_params(VMEM_BIG),
    )(dx3, dx3b, u0, uc, x2, w_ffn_conv, g_ffn, w_up, w_dn)


def _bwd_out(dx2b, o, zc, fcol, lcol, qa, w_o, g_attn_out, g_conv_out, w_conv, tm):
    s = o.shape[0]
    nb = s // tm
    aw = N_HEADS * LANES

    def body(dx2b_ref, o_ref, zc_ref, halo_ref, fcol_ref, lcol_ref, qa_ref, wo_ref, ga_ref, gco_ref, wc_ref, sc_ref,
             dzc_ref, doa_ref, dot_ref, qab_ref, dwc_ref, dga_ref, dgco_ref, carry):
        i = pl.program_id(0)
        rb = nb - 1 - i

        @pl.when(i == 0)
        def _():
            carry[...] = jnp.zeros_like(carry)
            dwc_ref[...] = jnp.zeros_like(dwc_ref)
            dga_ref[...] = jnp.zeros_like(dga_ref)
            dgco_ref[...] = jnp.zeros_like(dgco_ref)

        dmix = _dot_nt(dx2b_ref[...], wo_ref[...])
        dnc, dna = dmix[:, :CONV_CH], dmix[:, CONV_CH:]

        ov = o_ref[...]
        ra = _rms(ov)
        dga_ref[...] += jnp.sum(dna * (ov * ra), axis=0, keepdims=True)
        do = _rms_bwd(ov, ra, dna * ga_ref[...])
        dob = do.astype(BF)
        dot_ref[0] = do.T.astype(BF)
        sel = (_rows((ATTN_W, LANES)) // HEAD_DIM == _cols((ATTN_W, LANES))).astype(F32)
        delta = jnp.dot(do * ov, sel, precision=HI, preferred_element_type=F32)
        featc = _dot(_split_parts(fcol_ref[...] - lcol_ref[...], 1.0), sc_ref[...]).astype(BF)
        featd = _dot(_split_parts(-delta, 0.0), sc_ref[...]).astype(BF)
        lane = _cols((tm, LANES))
        for h in range(N_HEADS):
            hp, hh = divmod(h, 2)
            own = (lane >= HEAD_DIM * hh) & (lane < HEAD_DIM * (hh + 1))
            hs = slice(LANES * h, LANES * (h + 1))
            qab_ref[:, hs] = jnp.where(own, qa_ref[:, hs], featc[:, hs])
            doa_ref[:, hs] = jnp.where(own, dob[:, LANES * hp:LANES * (hp + 1)], featd[:, hs])

        zc_v = zc_ref[...]
        gb, gc, xc = zc_v[:, :CONV_CH], zc_v[:, CONV_CH:2 * CONV_CH], zc_v[:, 2 * CONV_CH:]
        hal = halo_ref[...] * (rb > 0).astype(F32)
        u = gc * xc
        prev = hal[:, CONV_CH:2 * CONV_CH] * hal[:, 2 * CONV_CH:]
        wc = wc_ref[...]
        cv, u1, u2 = _conv3(u, prev, wc)
        y = gb * cv
        rc = _rms(y)
        dgco_ref[...] += jnp.sum(dnc * (y * rc), axis=0, keepdims=True)
        dy = _rms_bwd(y, rc, dnc * gco_ref[...])
        dcv = dy * gb
        dwc_ref[0:1, :] += jnp.sum(dcv * u2, axis=0, keepdims=True)
        dwc_ref[1:2, :] += jnp.sum(dcv * u1, axis=0, keepdims=True)
        dwc_ref[2:3, :] += jnp.sum(dcv * u, axis=0, keepdims=True)
        du = _conv3_bwd(dcv, carry[...], wc)
        carry[...] = dcv[:8, :]
        dzc_ref[:, :CONV_CH] = (dy * cv).astype(BF)
        dzc_ref[:, CONV_CH:2 * CONV_CH] = (du * xc).astype(BF)
        dzc_ref[:, 2 * CONV_CH:] = (du * gc).astype(BF)

    blk = lambda c: pl.BlockSpec((tm, c), lambda i: (nb - 1 - i, 0))
    halo = pl.BlockSpec((8, 3 * CONV_CH), lambda i: (jnp.maximum((nb - 1 - i) * (tm // 8) - 1, 0), 0))
    tr = pl.BlockSpec((1, ATTN_W, tm), lambda i: (nb - 1 - i, 0, 0))
    return pl.pallas_call(
        body, name="bwd_out", grid=(nb,),
        in_specs=[blk(D_MODEL), blk(ATTN_W), blk(3 * CONV_CH), halo, blk(LANES), blk(LANES), blk(aw),
                  _full(w_o.shape), _full((1, ATTN_W)), _full((1, CONV_CH)), _full((8, CONV_CH)), _full((LANES, aw))],
        out_specs=[blk(3 * CONV_CH), blk(aw), tr, blk(aw),
                   _full((8, CONV_CH)), _full((1, ATTN_W)), _full((1, CONV_CH))],
        out_shape=[jax.ShapeDtypeStruct((s, 3 * CONV_CH), BF),
                   jax.ShapeDtypeStruct((s, aw), BF), jax.ShapeDtypeStruct((nb, ATTN_W, tm), BF),
                   jax.ShapeDtypeStruct((s, aw), BF),
                   jax.ShapeDtypeStruct((8, CONV_CH), F32), jax.ShapeDtypeStruct((1, ATTN_W), F32),
                   jax.ShapeDtypeStruct((1, CONV_CH), F32)],
        scratch_shapes=[pltpu.VMEM((8, CONV_CH), F32)],
        compiler_params=_params(VMEM_MID),
    )(dx2b, o, zc, zc, fcol, lcol, qa, w_o, g_attn_out, g_conv_out, w_conv, _bias_scatter()[:, :aw])


def _attn_bwd(qab, doa, ka, qkv, qt, dot, t, blocks):
    s = qab.shape[0]
    nb = s // t
    npair = N_HEADS // 2
    n = len(blocks)

    def body(ka_ref, v_ref, qab_ref, doa_ref, qt_ref, dot_ref, *refs):
        block_refs, (dk_ref, dv_ref, dfk_ref, dqa_ref, dq_ref) = refs[:n], refs[n:n + 5]
        received = refs[n + 5:2 * n + 5]
        acck, accv = refs[2 * n + 5:2 * n + 7]
        hp = pl.program_id(0)
        j = pl.program_id(1)
        start, finish = _scatter_phases(block_refs, received, *refs[2 * n + 7:])
        pl.when((hp == 0) & (j == 0))(start)

        @pl.when(j == 0)
        def _():
            dqa_ref[...] = jnp.zeros_like(dqa_ref)

        causal = _cols((t, t)) <= _rows((t, t))
        row = _rows((LANES, t))
        lane = _cols((t, LANES))
        acck[...] = jnp.zeros(acck.shape, F32)
        accv[...] = jnp.zeros(accv.shape, F32)
        v2 = v_ref[...]
        heads = []
        for hh in range(2):
            aux = HEAD_DIM * (1 - hh)
            ownl = (lane >= HEAD_DIM * hh) & (lane < HEAD_DIM * (hh + 1))
            ones3 = jnp.where((lane >= aux) & (lane < aux + 3), 1.0, 0.0).astype(BF)
            heads.append(dict(hh=hh, hs=slice(hh * LANES, (hh + 1) * LANES), aux=aux,
                              own=(row >= HEAD_DIM * hh) & (row < HEAD_DIM * (hh + 1)),
                              fill=jnp.where(row == aux, 1.0, 0.0).astype(BF),
                              kh=ka_ref[:, hh * LANES:(hh + 1) * LANES], vh=jnp.where(ownl, v2, ones3)))

        def tile(qb, masked):
            rs = pl.ds(pl.multiple_of(qb * t, t), t)
            qt2 = qt_ref[qb]
            dot2 = dot_ref[qb]
            es = [_dot_nt(qab_ref[rs, hd["hs"]], hd["kh"]) for hd in heads]
            dps = [_dot_nt(doa_ref[rs, hd["hs"]], hd["vh"]) for hd in heads]
            aks = [acck[hd["hh"]] for hd in heads]
            avs = [accv[hd["hh"]] for hd in heads]
            dqs = [dqa_ref[rs, hd["hs"]] for hd in heads]
            outs = []
            for hd, e, dp, ak, av, dq in zip(heads, es, dps, aks, avs, dqs):
                if masked:
                    e = jnp.where(causal, e, NEG)
                p = jnp.exp(e)
                ds = (p * dp).astype(BF)
                outs.append((ak + _dot(jnp.where(hd["own"], qt2, hd["fill"]), ds), av + _dot(dot2, p.astype(BF)),
                             dq + _dot(ds, hd["kh"])))
            for hd, (ak, av, dq) in zip(heads, outs):
                acck[hd["hh"]] = ak
                accv[hd["hh"]] = av
                dqa_ref[rs, hd["hs"]] = dq

        def loop_body(qb, carry):
            tile(qb, False)
            return carry

        tile(j, True)
        lax.fori_loop(j + 1, nb, loop_body, 0)
        dk_pair = dv_pair = None
        dfrows = jnp.zeros((LANES, t), F32)
        for hd in heads:
            ak, av = acck[hd["hh"]], accv[hd["hh"]]
            dfrows = jnp.where(row == 2 * hp + hd["hh"], -ak[hd["aux"]:hd["aux"] + 1, :], dfrows)
            dk_pair = ak if hd["hh"] == 0 else jnp.where(row < HEAD_DIM, dk_pair, ak)
            dv_pair = av if hd["hh"] == 0 else jnp.where(row < HEAD_DIM, dv_pair, av)
        dk_ref[...] = dk_pair.T.astype(BF)
        dv_ref[...] = dv_pair.T.astype(BF)
        dfk_ref[0] = dfrows.T

        @pl.when(j == nb - 1)
        def _():
            for r in range(nb):
                rows = slice(r * t, (r + 1) * t)
                dq_ref[rows, :] = (jnp.where(lane < HEAD_DIM, dqa_ref[rows, :LANES], dqa_ref[rows, LANES:])
                                   * 0.125).astype(BF)

        pl.when((hp == npair - 1) & (j == nb - 1))(finish)

    pair_cols = pl.BlockSpec((s, 2 * LANES), lambda hp, j: (0, hp))
    pair_rows = pl.BlockSpec((nb, LANES, t), lambda hp, j: (0, hp, 0))
    any_ = pl.BlockSpec(memory_space=pl.ANY)
    return pl.pallas_call(
        body, name="attn_bwd", grid=(npair, nb),
        in_specs=[pl.BlockSpec((t, 2 * LANES), lambda hp, j: (j, hp)),
                  pl.BlockSpec((t, LANES), lambda hp, j: (j, 2 * npair + hp)),
                  pair_cols, pair_cols, pair_rows, pair_rows] + [any_] * n,
        out_specs=[pl.BlockSpec((t, LANES), lambda hp, j: (j, hp)), pl.BlockSpec((t, LANES), lambda hp, j: (j, hp)),
                   pl.BlockSpec((1, t, LANES), lambda hp, j: (hp, j, 0)), pair_cols,
                   pl.BlockSpec((s, LANES), lambda hp, j: (0, hp))] + [any_] * n,
        out_shape=[jax.ShapeDtypeStruct((s, ATTN_W), BF), jax.ShapeDtypeStruct((s, ATTN_W), BF),
                   jax.ShapeDtypeStruct((npair, s, LANES), F32), jax.ShapeDtypeStruct((s, N_HEADS * LANES), F32),
                   jax.ShapeDtypeStruct((s, ATTN_W), BF)]
        + [jax.ShapeDtypeStruct(b.shape, b.dtype) for b in blocks],
        scratch_shapes=[pltpu.VMEM((2, LANES, t), F32), pltpu.VMEM((2, LANES, t), F32)] + _exchange_sems(n),
        compiler_params=_params(VMEM_BIG),
    )(ka, qkv, qab, doa, qt, dot, *blocks)


def _bwd_in(x, dx2, dzc, dq, dqa, dk, dv, dfk, fpre, w_zc, w_qkv, w_f, g_mix, tm, blocks):
    s = x.shape[0]
    nb = s // tm
    n = len(blocks)

    def body(x_ref, dx2_ref, dzc_ref, dq_ref, dqa_ref, dk_ref, dv_ref, dfk_ref, fpre_ref, wzc_ref, wqkv_ref, wf_ref,
             gm_ref, *refs):
        block_refs, (gx_ref, dfb_ref, dgm_ref, dbf_ref), received = refs[:n], refs[n:n + 4], refs[n + 4:2 * n + 4]
        carry = refs[2 * n + 4]
        i = pl.program_id(0)
        start, finish = _scatter_phases(block_refs, received, *refs[2 * n + 5:])
        pl.when(i == 0)(start)

        @pl.when(i == 0)
        def _():
            carry[...] = jnp.zeros_like(carry)
            dgm_ref[...] = jnp.zeros_like(dgm_ref)
            dbf_ref[...] = jnp.zeros_like(dbf_ref)

        lane = _cols((tm, LANES))
        dfq = jnp.zeros((tm, LANES), F32)
        for h in range(N_HEADS):
            aux = LANES * h + HEAD_DIM * (1 - h % 2)
            dfq = jnp.where(lane == h, dqa_ref[:, aux:aux + 1], dfq)

        triu = (_rows((tm, tm)) <= _cols((tm, tm))).astype(F32)
        df_cum = dfq + ((dfk_ref[0] + dfk_ref[1]) + (dfk_ref[2] + dfk_ref[3]))
        dlogf = jnp.dot(triu, df_cum, precision=HI, preferred_element_type=F32) + carry[...]
        carry[...] = dlogf[0:1, :]
        fpre = fpre_ref[...]
        df = jnp.where(_cols(fpre.shape) < N_HEADS, dlogf / (1.0 + jnp.exp(fpre)), 0.0)
        dbf_ref[...] += jnp.sum(df, axis=0, keepdims=True)
        dfb = df.astype(BF)
        dfb_ref[...] = dfb

        dh1 = _dot_nt(dzc_ref[...], wzc_ref[...])
        dh1 += _dot_nt(dq_ref[...], wqkv_ref[:, :ATTN_W])
        dh1 += _dot_nt(dk_ref[...], wqkv_ref[:, ATTN_W:2 * ATTN_W])
        dh1 += _dot_nt(dv_ref[...], wqkv_ref[:, 2 * ATTN_W:])
        dh1 += _dot_nt(dfb, wf_ref[...])
        xv = x_ref[...]
        r1 = _rms(xv)
        dgm_ref[...] += jnp.sum(dh1 * (xv * r1), axis=0, keepdims=True)
        gx_ref[...] = dx2_ref[...] + _rms_bwd(xv, r1, dh1 * gm_ref[...])
        pl.when(i == nb - 1)(finish)

    blk = lambda c: pl.BlockSpec((tm, c), lambda i: (nb - 1 - i, 0))
    any_ = pl.BlockSpec(memory_space=pl.ANY)
    return pl.pallas_call(
        body, name="bwd_in", grid=(nb,),
        in_specs=[blk(D_MODEL), blk(D_MODEL), blk(3 * CONV_CH), blk(ATTN_W), blk(N_HEADS * LANES), blk(ATTN_W),
                  blk(ATTN_W), pl.BlockSpec((N_HEADS // 2, tm, LANES), lambda i: (0, nb - 1 - i, 0)), blk(LANES),
                  _full(w_zc.shape), _full(w_qkv.shape), _full(w_f.shape), _full((1, D_MODEL))] + [any_] * n,
        out_specs=[blk(D_MODEL), blk(LANES), _full((1, D_MODEL)), _full((1, LANES))] + [any_] * n,
        out_shape=[jax.ShapeDtypeStruct((s, D_MODEL), F32), jax.ShapeDtypeStruct((s, LANES), BF),
                   jax.ShapeDtypeStruct((1, D_MODEL), F32), jax.ShapeDtypeStruct((1, LANES), F32)]
        + [jax.ShapeDtypeStruct(b.shape, b.dtype) for b in blocks],
        scratch_shapes=[pltpu.VMEM((1, LANES), F32)] + _exchange_sems(n),
        compiler_params=_params(VMEM_MID),
    )(x, dx2, dzc, dq, dqa, dk, dv, dfk, fpre, w_zc, w_qkv, w_f, g_mix, *blocks)


def _matmul_tn(a, b, bn, bk, name):
    s, m = a.shape
    n = b.shape[1]

    def body(a_ref, b_ref, o_ref):
        @pl.when(pl.program_id(1) == 0)
        def _():
            o_ref[...] = jnp.zeros_like(o_ref)

        o_ref[...] += _dot_tn(a_ref[...], b_ref[...])

    return pl.pallas_call(
        body, name=name, grid=(n // bn, s // bk),
        in_specs=[pl.BlockSpec((bk, m), lambda jn, k: (k, 0)), pl.BlockSpec((bk, bn), lambda jn, k: (k, jn))],
        out_specs=pl.BlockSpec((m, bn), lambda jn, k: (0, jn)),
        out_shape=jax.ShapeDtypeStruct((m, n), F32),
        compiler_params=_params(VMEM_MID),
    )(a, b)


def _matmul_tn_multi(a, bs, bk, name):
    s, m = a.shape
    nb = len(bs)

    def body(a_ref, *refs):
        b_refs, o_refs = refs[:nb], refs[nb:]

        @pl.when(pl.program_id(0) == 0)
        def _():
            for o_ref in o_refs:
                o_ref[...] = jnp.zeros_like(o_ref)

        at = a_ref[...].T
        for b_ref, o_ref in zip(b_refs, o_refs):
            o_ref[...] += _dot(at, b_ref[...])

    return pl.pallas_call(
        body, name=name, grid=(s // bk,),
        in_specs=[pl.BlockSpec((bk, m), lambda k: (k, 0))]
        + [pl.BlockSpec((bk, b.shape[1]), lambda k: (k, 0)) for b in bs],
        out_specs=[_full((m, b.shape[1])) for b in bs],
        out_shape=[jax.ShapeDtypeStruct((m, b.shape[1]), F32) for b in bs],
        compiler_params=_params(VMEM_BIG),
    )(a, *bs)


def _flip(v, bit):
    return 1 - v if bit else v


def _all_gather(shards):
    n = len(shards)

    def body(*refs):
        start, forward, finish = _gather_phases(refs[:n], refs[n:2 * n], *refs[2 * n:])
        start()
        forward()
        finish()

    any_ = pl.BlockSpec(memory_space=pl.ANY)
    return pl.pallas_call(
        body, name="all_gather_weights",
        in_specs=[any_] * n, out_specs=[any_] * n,
        out_shape=_gathered_shapes(shards), scratch_shapes=_exchange_sems(n),
    )(*shards)


def _gathered_shapes(shards):
    return [jax.ShapeDtypeStruct((8,) + sh.shape, sh.dtype) for sh in shards]


def _exchange_sems(n):
    return [pltpu.SemaphoreType.DMA((7 * n,)), pltpu.SemaphoreType.DMA((7 * n,)), pltpu.SemaphoreType.DMA((n,))]


def _gather_phases(src, out, send_sems, recv_sems, loc_sems):
    n = len(src)
    x, y, c = lax.axis_index("x"), lax.axis_index("y"), lax.axis_index("c")
    me, sibling = (x, y, c), (x, y, 1 - c)
    chips = [(1 - x, y), (x, 1 - y), (1 - x, 1 - y)]

    def slot(a, px, py, pc):
        return out[a].at[4 * px + 2 * py + pc]

    def copy(a, k, block, to, from_src=False):
        return pltpu.make_async_remote_copy(
            src_ref=src[a] if from_src else slot(a, *block), dst_ref=slot(a, *block),
            send_sem=send_sems.at[7 * a + k], recv_sem=recv_sems.at[7 * a + k],
            device_id=to, device_id_type=MESH)

    def local(a):
        return pltpu.make_async_copy(src[a], slot(a, *me), loc_sems.at[a])

    def first(a):
        return [copy(a, 0, me, sibling, True)] + [copy(a, 1 + j, me, (*chip, c), True)
                                                  for j, chip in enumerate(chips)]

    def passed(a, j):
        return copy(a, 4 + j, (*chips[j], c), sibling)

    def start():
        for a in range(n):
            local(a).start()
            for cp in first(a):
                cp.start()

    def forward():
        for a in range(n):
            for j, chip in enumerate(chips):
                copy(a, 1 + j, (*chip, c), me).wait_recv()
                passed(a, j).start()

    def finish():
        for a in range(n):
            copy(a, 0, sibling, me).wait_recv()
            for j, chip in enumerate(chips):
                copy(a, 4 + j, (*chip, 1 - c), me).wait_recv()
        for a in range(n):
            for cp in first(a) + [passed(a, j) for j in range(3)]:
                cp.wait_send()
            local(a).wait()

    return start, forward, finish


def _scatter_phases(src, out, send_sems, recv_sems, loc_sems):
    n = len(src)
    masks = [((k >> 2) & 1, (k >> 1) & 1, k & 1) for k in range(1, 8)]
    x, y, c = lax.axis_index("x"), lax.axis_index("y"), lax.axis_index("c")
    me = 4 * x + 2 * y + c

    def copies():
        cps = []
        for a in range(n):
            cps.append(pltpu.make_async_copy(src[a].at[me], out[a].at[me], loc_sems.at[a]))
            for k, (mx, my, mc) in enumerate(masks):
                px, py, pc = _flip(x, mx), _flip(y, my), _flip(c, mc)
                cps.append(pltpu.make_async_remote_copy(
                    src_ref=src[a].at[4 * px + 2 * py + pc], dst_ref=out[a].at[me],
                    send_sem=send_sems.at[7 * a + k], recv_sem=recv_sems.at[7 * a + k],
                    device_id=(px, py, pc), device_id_type=MESH))
        return cps

    def start():
        for cp in copies():
            cp.start()

    def finish():
        for cp in copies():
            cp.wait()

    return start, finish


def _all_to_all(blocks, name):
    n = len(blocks)

    def body(*refs):
        start, finish = _scatter_phases(refs[:n], refs[n:2 * n], *refs[2 * n:])
        start()
        finish()

    any_ = pl.BlockSpec(memory_space=pl.ANY)
    return pl.pallas_call(
        body, name=name,
        in_specs=[any_] * n, out_specs=[any_] * n,
        out_shape=[jax.ShapeDtypeStruct(b.shape, b.dtype) for b in blocks], scratch_shapes=_exchange_sems(n),
    )(*blocks)


def _adamw(parts, w, m, v, br, name, tail=None):
    g8, r, c = parts.shape
    c1 = 1.0 - ADAM_B1 ** ADAM_STEP
    c2 = 1.0 - ADAM_B2 ** ADAM_STEP
    extra = [] if tail is None else [tail]

    def body(p_ref, w_ref, m_ref, v_ref, *refs):
        g_ref, d_ref, m2_ref, v2_ref = refs[len(extra):]
        g = p_ref[0].astype(F32)
        for d in range(1, g8):
            g = g + p_ref[d].astype(F32)
        if tail is not None:
            k = tail.shape[1]
            place = (_rows((k, c)) + (c - k) == _cols((k, c))).astype(F32)
            g = g + jnp.dot(refs[0][...], place, precision=HI, preferred_element_type=F32)
        m2 = ADAM_B1 * m_ref[...] + (1.0 - ADAM_B1) * g
        v2 = ADAM_B2 * v_ref[...] + (1.0 - ADAM_B2) * (g * g)
        g_ref[...] = g
        m2_ref[...] = m2
        v2_ref[...] = v2
        d_ref[...] = -ADAM_LR * ((m2 / c1) / (jnp.sqrt(v2 / c2) + ADAM_EPS) + ADAM_WD * w_ref[...])

    blk = pl.BlockSpec((br, c), lambda i: (i, 0))
    out = jax.ShapeDtypeStruct((r, c), F32)
    return pl.pallas_call(
        body, name=name, grid=(r // br,),
        in_specs=[pl.BlockSpec((g8, br, c), lambda i: (0, i, 0)), blk, blk, blk]
        + [pl.BlockSpec((br, e.shape[1]), lambda i: (i, 0)) for e in extra],
        out_specs=[blk] * 4, out_shape=[out] * 4,
        compiler_params=_params(VMEM_MID),
    )(parts, w, m, v, *extra)


def _pad_rows(a, rows):
    return jnp.pad(a, ((0, rows - a.shape[0]), (0, 0)))


_SMALL = (("g_mix", 1024), ("b_f", 8), ("g_conv_out", 512), ("g_attn_out", 512), ("g_ffn", 1024), ("g_final", 1024))
_SMALL_LEN = sum(n for _, n in _SMALL)
_SMALL_ROWS = 104


def _pack_small(vals, extra=None):
    parts = [vals[k].reshape(-1) for k, _ in _SMALL]
    if extra is not None:
        parts.append(extra.reshape(-1))
    flat = jnp.concatenate(parts)
    return jnp.pad(flat, (0, _SMALL_ROWS * LANES - flat.shape[0])).reshape(_SMALL_ROWS, LANES)


def _unpack_small(packed, shapes):
    flat = packed.reshape(-1)
    out, off = {}, 0
    for k, n in _SMALL:
        out[k] = flat[off:off + n].reshape(shapes[k])
        off += n
    return out


def _col_blocks(a):
    r, c8 = a.shape
    return jnp.transpose(a.reshape(r, 8, c8 // 8), (1, 0, 2))


def _from_col_blocks(a):
    g, r, c = a.shape
    return jnp.transpose(a, (1, 0, 2)).reshape(r, g * c)


def kernel(x, g_mix, w_in, b_f, w_conv, g_conv_out, g_attn_out, w_o, g_ffn, w_up, w_ffn_conv, w_down, g_final, loss_target, m_g_mix, m_w_in, m_b_f, m_w_conv, m_g_conv_out, m_g_attn_out, m_w_o, m_g_ffn, m_w_up, m_w_ffn_conv, m_w_down, m_g_final, v_g_mix, v_w_in, v_b_f, v_w_conv, v_g_conv_out, v_g_attn_out, v_w_o, v_g_ffn, v_w_up, v_w_ffn_conv, v_w_down, v_g_final):
    w = dict(g_mix=g_mix, w_in=w_in[0], b_f=b_f, w_conv=w_conv[0], g_conv_out=g_conv_out, g_attn_out=g_attn_out,
             w_o=w_o[0], g_ffn=g_ffn, w_up=w_up[0], w_ffn_conv=w_ffn_conv[0], w_down=w_down[0], g_final=g_final)
    m = dict(g_mix=m_g_mix, w_in=m_w_in[0], b_f=m_b_f, w_conv=m_w_conv[0], g_conv_out=m_g_conv_out,
             g_attn_out=m_g_attn_out, w_o=m_w_o[0], g_ffn=m_g_ffn, w_up=m_w_up[0], w_ffn_conv=m_w_ffn_conv[0],
             w_down=m_w_down[0], g_final=m_g_final)
    v = dict(g_mix=v_g_mix, w_in=v_w_in[0], b_f=v_b_f, w_conv=v_w_conv[0], g_conv_out=v_g_conv_out,
             g_attn_out=v_g_attn_out, w_o=v_w_o[0], g_ffn=v_g_ffn, w_up=v_w_up[0], w_ffn_conv=v_w_ffn_conv[0],
             w_down=v_w_down[0], g_final=v_g_final)
    shapes = dict(g_mix=g_mix.shape, w_in=w_in.shape, b_f=b_f.shape, w_conv=w_conv.shape,
                  g_conv_out=g_conv_out.shape, g_attn_out=g_attn_out.shape, w_o=w_o.shape, g_ffn=g_ffn.shape,
                  w_up=w_up.shape, w_ffn_conv=w_ffn_conv.shape, w_down=w_down.shape, g_final=g_final.shape)

    tm = t = 512
    tf = 256
    xs, tgt = x[0], loss_target[0]
    seq = xs.shape[0]
    assert seq % tm == 0 and seq % tf == 0
    bk = 1024 if seq % 1024 == 0 else 512

    g_in, g_conv = _all_gather([w["w_in"].astype(BF), w["w_conv"]])
    w_in_f = _from_col_blocks(g_in)
    w_zc = w_in_f[:, :3 * CONV_CH]
    w_qkv = w_in_f[:, 3 * CONV_CH:3 * CONV_CH + 3 * ATTN_W]
    w_f = jnp.pad(w_in_f[:, 3 * CONV_CH + 3 * ATTN_W:], ((0, 0), (0, LANES - N_HEADS)))
    b_f_p = jnp.pad(b_f, ((0, 0), (0, LANES - N_HEADS)))
    w_conv_p = _pad_rows(_from_col_blocks(g_conv), 8)

    h1, zc, qkv, fpre, fcol, nc, qa, ka, qt, vt = _fwd_in(
        xs, g_mix, w_zc, w_qkv, w_f, b_f_p, w_conv_p, g_conv_out, tm)
    o, lcol, g_o, g_up, g_dn, g_fc = _attn_fwd(
        qa, ka, vt, t, [w["w_o"].astype(BF), w["w_up"].astype(BF), w["w_down"].astype(BF), w["w_ffn_conv"]])
    w_o_f = g_o.reshape(D_MODEL, D_MODEL)
    w_up_f = _from_col_blocks(g_up)
    w_dn_f = g_dn.reshape(D_FF, D_MODEL)
    w_fc_p = _pad_rows(_from_col_blocks(g_fc), 8)
    x2, h2, mix = _fwd_out(xs, nc, o, g_attn_out, w_o_f, g_ffn, tm)
    u0, uc, act, dx3, dx3b, loss, dg_final = _fwd_ffn(
        h2, x2, tgt, w_up_f, w_fc_p, w_dn_f, g_final.reshape(1, D_MODEL), tf)
    loss_part = loss

    du0, dx2, dx2b, dw_fc, dg_ffn = _bwd_ffn(dx3, dx3b, u0, uc, x2, w_up_f.T, w_fc_p, w_dn_f.T, g_ffn, tf)
    dw_up = _matmul_tn(h2, du0, 1408, 2 * bk if seq % (2 * bk) == 0 else bk, "dw_up")
    dw_dn, = _matmul_tn_multi(act, [dx3b], bk, "dw_down")
    dw_o, = _matmul_tn_multi(mix, [dx2b], bk, "dw_o")
    dzc, doa, dot, qab, dw_conv, dg_attn, dg_conv = _bwd_out(
        dx2b, o, zc, fcol, lcol, qa, w_o_f, g_attn_out, g_conv_out, w_conv_p, t)
    dk, dv, dfk, dqa, dq, r_up, r_dn, r_fc, r_o = _attn_bwd(
        qab, doa, ka, qkv, qt, dot, t,
        [_col_blocks(dw_up).astype(BF), dw_dn.reshape(8, D_FF // 8, D_MODEL).astype(BF), _col_blocks(dw_fc[:3]),
         dw_o.reshape(8, D_MODEL // 8, D_MODEL).astype(BF)])
    dw_zc, dw_q, dw_k, dw_v = _matmul_tn_multi(h1, [dzc, dq, dk, dv], bk, "dw_in")
    dw_main = jnp.concatenate([dw_zc, dw_q, dw_k, dw_v, jnp.zeros((D_MODEL, N_HEADS), F32)], axis=1)
    gx, dfb, dg_mix, db_f, r_in = _bwd_in(xs, dx2, dzc, dq, dqa, dk, dv, dfk, fpre, w_zc, w_qkv, w_f, g_mix, tm,
                                          [_col_blocks(dw_main).astype(BF)])
    dw_f = _matmul_tn(h1, dfb, LANES, bk, "dw_in_f")
    small = dict(g_mix=dg_mix, b_f=db_f[:, :N_HEADS], g_conv_out=dg_conv, g_attn_out=dg_attn, g_ffn=dg_ffn,
                 g_final=dg_final)
    riders = jnp.concatenate([loss_part.reshape(-1), dw_f[:, :N_HEADS].reshape(-1)])
    r_conv, r_small = _all_to_all(
        [_col_blocks(dw_conv[:3]), jnp.broadcast_to(_pack_small(small, riders), (8, _SMALL_ROWS, LANES))],
        "all_to_all_grads")

    g_out, d_out, m_out, v_out = {}, {}, {}, {}
    res = _adamw(r_small, _pack_small(w), _pack_small(m), _pack_small(v), _SMALL_ROWS, "adamw_gains")
    for dst, packed in zip((g_out, d_out, m_out, v_out), res):
        dst.update(_unpack_small(packed, shapes))
    summed = res[0].reshape(-1)
    loss = summed[_SMALL_LEN]
    is_last = (4 * lax.axis_index("x") + 2 * lax.axis_index("y") + lax.axis_index("c") == 7).astype(F32)
    dw_f_sum = summed[_SMALL_LEN + 1:_SMALL_LEN + 1 + D_MODEL * N_HEADS].reshape(D_MODEL, N_HEADS) * is_last

    big = ("w_in", "w_o", "w_up", "w_down", "w_conv", "w_ffn_conv")
    recv = [r_in, r_o, r_up, r_dn, r_conv, r_fc]
    rows = dict(w_in=256, w_o=128, w_up=256, w_down=176, w_conv=3, w_ffn_conv=3)
    for a, k in enumerate(big):
        res = _adamw(recv[a], w[k], m[k], v[k], rows[k], "adamw_" + k, tail=dw_f_sum if k == "w_in" else None)
        g_out[k], d_out[k], m_out[k], v_out[k] = [r.reshape(shapes[k]) for r in res]

    order = ("g_mix", "w_in", "b_f", "w_conv", "g_conv_out", "g_attn_out", "w_o", "g_ffn", "w_up", "w_ffn_conv",
             "w_down", "g_final")
    return (loss, gx.reshape(x.shape), *[g_out[k] for k in order], *[d_out[k] for k in order],
            *[m_out[k] for k in order], *[v_out[k] for k in order])
```

```python
import jax
import jax.numpy as jnp
import numpy as np
from jax import lax
from jax.experimental import pallas as pl
from jax.experimental.pallas import tpu as pltpu

F32 = jnp.float32
BF = jnp.bfloat16
HI = lax.Precision.HIGHEST
MESH = pl.DeviceIdType.MESH

D_MODEL = 1024
CONV_CH = 512
ATTN_W = 512
N_HEADS = 8
HEAD_DIM = 64
D_FF = 2816
FF_CHUNK = 256
N_FF_CHUNKS = D_FF // FF_CHUNK
EPS = 1e-6
NEG = -1e30
LANES = 128
VMEM_BIG = 56 * 1024 * 1024
VMEM_MID = 40 * 1024 * 1024

ADAM_LR = 0.001
ADAM_B1 = 0.9
ADAM_B2 = 0.999
ADAM_EPS = 1e-08
ADAM_WD = 0.01
ADAM_STEP = 10

NT = (((1,), (1,)), ((), ()))
TN = (((0,), (0,)), ((), ()))


def _dot(a, b):
    return jnp.dot(a, b, preferred_element_type=F32)


def _dot_nt(a, b):
    return lax.dot_general(a, b, NT, preferred_element_type=F32)


def _dot_tn(a, b):
    return lax.dot_general(a, b, TN, preferred_element_type=F32)


def _params(vmem):
    return pltpu.CompilerParams(vmem_limit_bytes=vmem)


def _rows(shape):
    return lax.broadcasted_iota(jnp.int32, shape, 0)


def _cols(shape):
    return lax.broadcasted_iota(jnp.int32, shape, 1)


def _shift_down(u, prev, k):
    n = prev.shape[0]
    out = pltpu.roll(u, k, 0)
    row = _rows(u.shape)
    for r in range(k):
        out = jnp.where(row == r, prev[n - k + r:n - k + r + 1, :].astype(u.dtype), out)
    return out


def _shift_up(u, nxt, k):
    tm = u.shape[0]
    out = pltpu.roll(u, tm - k, 0)
    row = _rows(u.shape)
    for r in range(k):
        out = jnp.where(row == tm - k + r, nxt[r:r + 1, :], out)
    return out


def _conv3(u, prev, w):
    u1 = _shift_down(u, prev, 1)
    u2 = _shift_down(u, prev, 2)
    return w[0:1, :] * u2 + w[1:2, :] * u1 + w[2:3, :] * u, u1, u2


def _conv3_bwd(d, nxt, w):
    return w[2:3, :] * d + w[1:2, :] * _shift_up(d, nxt, 1) + w[0:1, :] * _shift_up(d, nxt, 2)


def _rms(x):
    return lax.rsqrt(jnp.mean(x * x, axis=-1, keepdims=True) + EPS)


def _rms_bwd(x, r, dyg):
    return r * dyg - x * (r * r * r) * jnp.mean(dyg * x, axis=-1, keepdims=True)


def _full(shape):
    nd = len(shape)
    return pl.BlockSpec(shape, lambda i, _n=nd: (0,) * _n)


ONES_LANE = 24


def _bias_scatter():
    sc = np.zeros((LANES, 2 * N_HEADS * LANES), np.float32)
    koff = N_HEADS * LANES
    for h in range(N_HEADS):
        aux = HEAD_DIM * (1 - h % 2)
        for j in range(3):
            sc[8 * j + h, LANES * h + aux + j] = 1.0
            sc[ONES_LANE, koff + LANES * h + aux + j] = 1.0
            sc[ONES_LANE, LANES * h + aux + 3 + j] = 1.0
            sc[8 * j + h, koff + LANES * h + aux + 3 + j] = -1.0
    return jnp.asarray(sc, BF)


def _split_parts(v, one):
    hi = v.astype(BF).astype(F32)
    rest = v - hi
    mid = rest.astype(BF).astype(F32)
    lo = (rest - mid).astype(BF).astype(F32)
    parts = hi + pltpu.roll(mid, 8, 1) + pltpu.roll(lo, 16, 1)
    return jnp.where(_cols(parts.shape) == ONES_LANE, one, parts).astype(BF)


def _fwd_in(x, g_mix, w_zc, w_qkv, w_f, b_f, w_conv, g_conv_out, tm):
    s = x.shape[0]
    nb = s // tm
    aw = N_HEADS * LANES

    def body(x_ref, gm_ref, wzc_ref, wqkv_ref, wf_ref, bf_ref, wc_ref, gco_ref, sc_ref,
             h1_ref, zc_ref, qkv_ref, fpre_ref, fcol_ref, nc_ref, qa_ref, ka_ref, qt_ref, vt_ref, cu_ref, cf_ref):
        i = pl.program_id(0)

        @pl.when(i == 0)
        def _():
            cu_ref[...] = jnp.zeros_like(cu_ref)
            cf_ref[...] = jnp.zeros_like(cf_ref)

        xv = x_ref[...]
        hb = (xv * _rms(xv) * gm_ref[...]).astype(BF)
        h1_ref[...] = hb
        zc = _dot(hb, wzc_ref[...])
        zc_ref[...] = zc
        qkv = _dot(hb, wqkv_ref[...])
        qkv = jnp.where(_cols(qkv.shape) < ATTN_W, qkv * 0.125, qkv)
        qkvb = qkv.astype(BF)
        qkv_ref[...] = qkvb
        qt_ref[0] = qkv[:, :ATTN_W].T.astype(BF)
        vt_ref[0] = qkv[:, 2 * ATTN_W:].T.astype(BF)

        gb, gc, xc = zc[:, :CONV_CH], zc[:, CONV_CH:2 * CONV_CH], zc[:, 2 * CONV_CH:]
        u = gc * xc
        cv, _, _ = _conv3(u, cu_ref[...], wc_ref[...])
        cu_ref[...] = u[tm - 8:, :]
        y = gb * cv
        nc_ref[...] = (y * _rms(y) * gco_ref[...]).astype(BF)

        fpre = _dot(hb, wf_ref[...]) + bf_ref[...]
        fpre_ref[...] = fpre
        logf = jnp.minimum(fpre, 0.0) - jnp.log1p(jnp.exp(-jnp.abs(fpre)))
        logf = jnp.where(_cols(logf.shape) < N_HEADS, logf, 0.0)
        tri = (_rows((LANES, LANES)) >= _cols((LANES, LANES))).astype(F32)
        offset = cf_ref[...]
        pieces = []
        for r in range(tm // LANES):
            piece = jnp.dot(tri, logf[r * LANES:(r + 1) * LANES, :], precision=HI, preferred_element_type=F32) + offset
            offset = piece[LANES - 1:LANES, :]
            pieces.append(piece)
        fcol = jnp.concatenate(pieces, axis=0)
        cf_ref[...] = offset
        fcol_ref[...] = fcol

        feat = _dot(_split_parts(fcol, 1.0), sc_ref[...]).astype(BF)
        lane = _cols((tm, LANES))
        for h in range(N_HEADS):
            hp, hh = divmod(h, 2)
            own = (lane >= HEAD_DIM * hh) & (lane < HEAD_DIM * (hh + 1))
            hs = slice(LANES * h, LANES * (h + 1))
            qa_ref[:, hs] = jnp.where(own, qkvb[:, LANES * hp:LANES * (hp + 1)], feat[:, hs])
            ka_ref[:, hs] = jnp.where(own, qkvb[:, ATTN_W + LANES * hp:ATTN_W + LANES * (hp + 1)],
                                      feat[:, aw + LANES * h:aw + LANES * (h + 1)])

    blk = lambda c: pl.BlockSpec((tm, c), lambda i: (i, 0))
    return pl.pallas_call(
        body, name="fwd_in", grid=(nb,),
        in_specs=[blk(D_MODEL), _full((1, D_MODEL)), _full(w_zc.shape), _full(w_qkv.shape), _full(w_f.shape),
                  _full((1, LANES)), _full((8, CONV_CH)), _full((1, CONV_CH)), _full((LANES, 2 * aw))],
        out_specs=[blk(D_MODEL), blk(3 * CONV_CH), blk(3 * ATTN_W), blk(LANES), blk(LANES),
                   blk(CONV_CH), blk(aw), blk(aw)] + [pl.BlockSpec((1, ATTN_W, tm), lambda i: (i, 0, 0))] * 2,
        out_shape=[jax.ShapeDtypeStruct((s, D_MODEL), BF), jax.ShapeDtypeStruct((s, 3 * CONV_CH), F32),
                   jax.ShapeDtypeStruct((s, 3 * ATTN_W), BF), jax.ShapeDtypeStruct((s, LANES), F32),
                   jax.ShapeDtypeStruct((s, LANES), F32),
                   jax.ShapeDtypeStruct((s, CONV_CH), BF), jax.ShapeDtypeStruct((s, aw), BF),
                   jax.ShapeDtypeStruct((s, aw), BF)] + [jax.ShapeDtypeStruct((nb, ATTN_W, tm), BF)] * 2,
        scratch_shapes=[pltpu.VMEM((8, CONV_CH), F32), pltpu.VMEM((1, LANES), F32)],
        compiler_params=_params(VMEM_MID),
    )(x, g_mix, w_zc, w_qkv, w_f, b_f, w_conv, g_conv_out, _bias_scatter())


def _pipeline_masked_last(last, produce, consume, buf_a, buf_b):
    produce(0, buf_a)

    def two_blocks(j, carry):
        blk = 2 * j
        produce(blk + 1, buf_b)
        consume(blk, buf_a, False)
        produce(blk + 2, buf_a)
        consume(blk + 1, buf_b, False)
        return carry

    lax.fori_loop(0, last // 2, two_blocks, 0)

    @pl.when(last % 2 == 0)
    def _():
        consume(last, buf_a, True)

    @pl.when(last % 2 == 1)
    def _():
        produce(last, buf_b)
        consume(last - 1, buf_a, False)
        consume(last, buf_b, True)


def _attn_fwd(qa, ka, vt, t, shards):
    s = qa.shape[0]
    nb = s // t
    n = len(shards)

    def body(qa_ref, ka_any, vt_any, *refs):
        shard_refs, (o_ref, l_ref), gathered = refs[:n], refs[n:n + 2], refs[n + 2:2 * n + 2]
        ka_scr, vt_scr, m_scr, acc_scr, ea_scr, eb_scr, sem = refs[2 * n + 2:2 * n + 9]
        i = pl.program_id(0)
        start, forward, finish = _gather_phases(shard_refs, gathered, *refs[2 * n + 9:])
        pl.when(i == 0)(start)
        pl.when(i == nb // 2)(forward)
        _load_weights(i, [(ka_any, ka_scr), (vt_any, vt_scr)], sem)

        causal_t = _rows((t, t)) <= _cols((t, t))
        row = _rows((LANES, t))
        lrows = jnp.zeros((LANES, t), F32)
        for hp in range(N_HEADS // 2):
            ps = slice(hp * LANES, (hp + 1) * LANES)
            m_scr[...] = jnp.full(m_scr.shape, NEG, F32)
            acc_scr[...] = jnp.zeros(acc_scr.shape, F32)
            heads = [dict(hh=hh, hs=slice((2 * hp + hh) * LANES, (2 * hp + hh + 1) * LANES),
                          aux=HEAD_DIM * (1 - hh), own=(row >= HEAD_DIM * hh) & (row < HEAD_DIM * (hh + 1)),
                          fill=jnp.where(row == HEAD_DIM * (1 - hh), 1.0, 0.0).astype(BF),
                          qh=qa_ref[:, (2 * hp + hh) * LANES:(2 * hp + hh + 1) * LANES]) for hh in range(2)]

            def scores(kb, dst, heads=heads):
                rs = pl.ds(pl.multiple_of(kb * t, t), t)
                for hd in heads:
                    dst[hd["hh"]] = _dot_nt(ka_scr[rs, hd["hs"]], hd["qh"])

            def consume(kb, src, masked, ps=ps, heads=heads):
                vt2 = vt_scr[kb, ps, :]
                m_olds = [m_scr[hd["hh"]:hd["hh"] + 1, :] for hd in heads]
                accs = [acc_scr[hd["hh"]] for hd in heads]
                m_news, acc_news = [], []
                for hd, m_old, acc in zip(heads, m_olds, accs):
                    e = src[hd["hh"]]
                    if masked:
                        e = jnp.where(causal_t, e, NEG)
                    m_new = jnp.maximum(m_old, jnp.max(e, axis=0, keepdims=True))
                    p = jnp.exp(e - m_new).astype(BF)
                    vta = jnp.where(hd["own"], vt2, hd["fill"])
                    acc_news.append(jnp.exp(m_old - m_new) * acc + _dot(vta, p))
                    m_news.append(m_new)
                for hd, m_new, acc in zip(heads, m_news, acc_news):
                    acc_scr[hd["hh"]] = acc
                    m_scr[hd["hh"]:hd["hh"] + 1, :] = m_new

            _pipeline_masked_last(i, scores, consume, ea_scr, eb_scr)

            o_pair = None
            for hd in heads:
                hh = hd["hh"]
                acc = acc_scr[hh]
                denom = acc[hd["aux"]:hd["aux"] + 1, :]
                o_h = acc / denom
                lrows = jnp.where(row == 2 * hp + hh, m_scr[hh:hh + 1, :] + jnp.log(denom), lrows)
                o_pair = o_h if hh == 0 else jnp.where(row < HEAD_DIM, o_pair, o_h)
            o_ref[:, ps] = o_pair.T
        l_ref[...] = lrows.T
        pl.when(i == nb - 1)(finish)

    any_ = pl.BlockSpec(memory_space=pl.ANY)
    return pl.pallas_call(
        body, name="attn_fwd", grid=(nb,),
        in_specs=[pl.BlockSpec((t, N_HEADS * LANES), lambda i: (i, 0)), any_, any_] + [any_] * n,
        out_specs=[pl.BlockSpec((t, ATTN_W), lambda i: (i, 0)), pl.BlockSpec((t, LANES), lambda i: (i, 0))]
        + [any_] * n,
        out_shape=[jax.ShapeDtypeStruct((s, ATTN_W), F32), jax.ShapeDtypeStruct((s, LANES), F32)]
        + _gathered_shapes(shards),
        scratch_shapes=[pltpu.VMEM(ka.shape, BF), pltpu.VMEM(vt.shape, BF), pltpu.VMEM((2, t), F32),
                        pltpu.VMEM((2, LANES, t), F32), pltpu.VMEM((2, t, t), F32), pltpu.VMEM((2, t, t), F32),
                        pltpu.SemaphoreType.DMA((2,))] + _exchange_sems(n),
        compiler_params=_params(VMEM_BIG),
    )(qa, ka, vt, *shards)


def _fwd_out(x, nc, o, g_attn_out, w_o, g_ffn, tm):
    s = x.shape[0]

    def body(x_ref, nc_ref, o_ref, ga_ref, wo_ref, gf_ref, x2_ref, h2_ref, mix_ref):
        ov = o_ref[...]
        na = (ov * _rms(ov) * ga_ref[...]).astype(BF)
        ncv = nc_ref[...]
        mix_ref[:, :CONV_CH] = ncv
        mix_ref[:, CONV_CH:] = na
        x2 = x_ref[...] + _dot(ncv, wo_ref[:CONV_CH, :]) + _dot(na, wo_ref[CONV_CH:, :])
        x2_ref[...] = x2
        h2_ref[...] = (x2 * _rms(x2) * gf_ref[...]).astype(BF)

    blk = lambda c: pl.BlockSpec((tm, c), lambda i: (i, 0))
    return pl.pallas_call(
        body, name="fwd_out", grid=(s // tm,),
        in_specs=[blk(D_MODEL), blk(CONV_CH), blk(ATTN_W), _full((1, ATTN_W)), _full(w_o.shape), _full((1, D_MODEL))],
        out_specs=[blk(D_MODEL), blk(D_MODEL), blk(D_MODEL)],
        out_shape=[jax.ShapeDtypeStruct((s, D_MODEL), F32), jax.ShapeDtypeStruct((s, D_MODEL), BF),
                   jax.ShapeDtypeStruct((s, D_MODEL), BF)],
        compiler_params=_params(VMEM_MID),
    )(x, nc, o, g_attn_out, w_o, g_ffn)


def _load_weights(i, pairs, sem):
    @pl.when(i == 0)
    def _():
        cps = [pltpu.make_async_copy(src, dst, sem.at[n]) for n, (src, dst) in enumerate(pairs)]
        for cp in cps:
            cp.start()
        for cp in cps:
            cp.wait()


def _ff_cols(j):
    return (slice(j * FF_CHUNK, (j + 1) * FF_CHUNK), slice(D_FF + j * FF_CHUNK, D_FF + (j + 1) * FF_CHUNK))


def _fwd_ffn(h2, x2, tgt, w_up, w_ffn_conv, w_dn, g_final, tm):
    s = x2.shape[0]

    def body(h2_ref, x2_ref, tgt_ref, wfc_ref, gfin_ref, wup_any, wdn_any,
             u0_ref, uc_ref, act_ref, dx3_ref, dx3b_ref, loss_ref, dgfin_ref,
             wup, wdn, carry, sem):
        i = pl.program_id(0)
        _load_weights(i, [(wup_any, wup), (wdn_any, wdn)], sem)

        @pl.when(i == 0)
        def _():
            carry[...] = jnp.zeros_like(carry)
            loss_ref[...] = jnp.zeros_like(loss_ref)
            dgfin_ref[...] = jnp.zeros_like(dgfin_ref)

        hb = h2_ref[...]

        def up(j):
            return [_dot(hb, wup[:, cc]) for cc in _ff_cols(j)]

        nxt = up(0)
        down = None
        for j in range(N_FF_CHUNKS):
            cur = nxt
            if j + 1 < N_FF_CHUNKS:
                nxt = up(j + 1)
            parts = []
            for cc, u0 in zip(_ff_cols(j), cur):
                u0_ref[:, cc] = u0.astype(BF)
                uu, _, _ = _conv3(u0, carry[:, cc], wfc_ref[:, cc])
                carry[:, cc] = u0[tm - 8:, :]
                uc_ref[:, cc] = uu.astype(BF)
                parts.append(uu)
            ua, ug = parts
            act = (ug * jax.nn.sigmoid(ug) * ua).astype(BF)
            ca = _ff_cols(j)[0]
            act_ref[:, ca] = act
            part = _dot(act, wdn[ca, :])
            down = part if down is None else down + part

        x3 = x2_ref[...] + down
        r3 = _rms(x3)
        gfin = gfin_ref[...]
        xn = x3 * r3
        diff = xn * gfin - tgt_ref[...]
        loss_ref[...] += jnp.sum(jnp.sum(diff * diff, axis=-1, keepdims=True), axis=0, keepdims=True) * (0.5 / D_MODEL)
        dy = diff * (1.0 / D_MODEL)
        dgfin_ref[...] += jnp.sum(dy * xn, axis=0, keepdims=True)
        dx3 = _rms_bwd(x3, r3, dy * gfin)
        dx3_ref[...] = dx3
        dx3b_ref[...] = dx3.astype(BF)

    blk = lambda c: pl.BlockSpec((tm, c), lambda i: (i, 0))
    any_ = pl.BlockSpec(memory_space=pl.ANY)
    return pl.pallas_call(
        body, name="fwd_ffn", grid=(s // tm,),
        in_specs=[blk(D_MODEL), blk(D_MODEL), blk(D_MODEL), _full((8, 2 * D_FF)), _full((1, D_MODEL)), any_, any_],
        out_specs=[blk(2 * D_FF), blk(2 * D_FF), blk(D_FF), blk(D_MODEL), blk(D_MODEL), _full((1, 1)),
                   _full((1, D_MODEL))],
        out_shape=[jax.ShapeDtypeStruct((s, 2 * D_FF), BF), jax.ShapeDtypeStruct((s, 2 * D_FF), BF),
                   jax.ShapeDtypeStruct((s, D_FF), BF),
                   jax.ShapeDtypeStruct((s, D_MODEL), F32), jax.ShapeDtypeStruct((s, D_MODEL), BF),
                   jax.ShapeDtypeStruct((1, 1), F32), jax.ShapeDtypeStruct((1, D_MODEL), F32)],
        scratch_shapes=[pltpu.VMEM(w_up.shape, BF), pltpu.VMEM(w_dn.shape, BF), pltpu.VMEM((8, 2 * D_FF), F32),
                        pltpu.SemaphoreType.DMA((2,))],
        compiler_params=_params(VMEM_BIG),
    )(h2, x2, tgt, w_ffn_conv, g_final, w_up, w_dn)


def _bwd_ffn(dx3, dx3b, u0, uc, x2, w_up, w_ffn_conv, w_dn, g_ffn, tm):
    s = x2.shape[0]
    nb = s // tm

    def body(dx3_ref, dx3b_ref, u0_ref, uc_ref, x2_ref, wfc_ref, gf_ref, wup_any, wdn_any,
             du0_ref, dx2_ref, dx2b_ref, dwfc_ref, dgf_ref,
             wup, wdn, carry, sem):
        i = pl.program_id(0)
        _load_weights(i, [(wup_any, wup), (wdn_any, wdn)], sem)

        @pl.when(i == 0)
        def _():
            carry[...] = jnp.zeros_like(carry)
            dwfc_ref[...] = jnp.zeros_like(dwfc_ref)
            dgf_ref[...] = jnp.zeros_like(dgf_ref)

        db = dx3b_ref[...]

        def dact_of(j):
            return _dot(db, wdn[:, _ff_cols(j)[0]])

        nxt = dact_of(0)
        dh2 = None
        for j in range(N_FF_CHUNKS):
            ca, cg = _ff_cols(j)
            dact = nxt
            if j + 1 < N_FF_CHUNKS:
                nxt = dact_of(j + 1)
            ua = uc_ref[:, ca].astype(F32)
            ug = uc_ref[:, cg].astype(F32)
            sg = jax.nn.sigmoid(ug)
            da = dact * (ug * sg)
            dg = dact * ua * (sg * (1.0 + ug * (1.0 - sg)))
            for cc, d in ((ca, da), (cg, dg)):
                nxt_rows = carry[:, cc]
                d1 = _shift_up(d, nxt_rows, 1)
                d2 = _shift_up(d, nxt_rows, 2)
                u0c = u0_ref[:, cc].astype(F32)
                w = wfc_ref[:, cc]
                dwfc_ref[0:1, cc] += jnp.sum(d2 * u0c, axis=0, keepdims=True)
                dwfc_ref[1:2, cc] += jnp.sum(d1 * u0c, axis=0, keepdims=True)
                dwfc_ref[2:3, cc] += jnp.sum(d * u0c, axis=0, keepdims=True)
                du0 = (w[2:3, :] * d + w[1:2, :] * d1 + w[0:1, :] * d2).astype(BF)
                carry[:, cc] = d[:8, :]
                du0_ref[:, cc] = du0
                part = _dot(du0, wup[cc, :])
                dh2 = part if dh2 is None else dh2 + part

        x2v = x2_ref[...]
        r2 = _rms(x2v)
        dgf_ref[...] += jnp.sum(dh2 * (x2v * r2), axis=0, keepdims=True)
        dx2 = dx3_ref[...] + _rms_bwd(x2v, r2, dh2 * gf_ref[...])
        dx2_ref[...] = dx2
        dx2b_ref[...] = dx2.astype(BF)

    blk = lambda c: pl.BlockSpec((tm, c), lambda i: (nb - 1 - i, 0))
    any_ = pl.BlockSpec(memory_space=pl.ANY)
    return pl.pallas_call(
        body, name="bwd_ffn", grid=(nb,),
        in_specs=[blk(D_MODEL), blk(D_MODEL), blk(2 * D_FF), blk(2 * D_FF), blk(D_MODEL), _full((8, 2 * D_FF)),
                  _full((1, D_MODEL)), any_, any_],
        out_specs=[blk(2 * D_FF), blk(D_MODEL), blk(D_MODEL), _full((8, 2 * D_FF)), _full((1, D_MODEL))],
        out_shape=[jax.ShapeDtypeStruct((s, 2 * D_FF), BF), jax.ShapeDtypeStruct((s, D_MODEL), F32),
                   jax.ShapeDtypeStruct((s, D_MODEL), BF), jax.ShapeDtypeStruct((8, 2 * D_FF), F32),
                   jax.ShapeDtypeStruct((1, D_MODEL), F32)],
        scratch_shapes=[pltpu.VMEM(w_up.shape, BF), pltpu.VMEM(w_dn.shape, BF), pltpu.VMEM((8, 2 * D_FF), F32),
                        pltpu.SemaphoreType.DMA((2,))],
        compiler_params=_params(VMEM_BIG),
    )(dx3, dx3b, u0, uc, x2, w_ffn_conv, g_ffn, w_up, w_dn)


def _bwd_out(dx2b, o, zc, fcol, lcol, qa, w_o, g_attn_out, g_conv_out, w_conv, tm):
    s = o.shape[0]
    nb = s // tm
    aw = N_HEADS * LANES

    def body(dx2b_ref, o_ref, zc_ref, halo_ref, fcol_ref, lcol_ref, qa_ref, wo_ref, ga_ref, gco_ref, wc_ref, sc_ref,
             dzc_ref, doa_ref, dot_ref, qab_ref, dwc_ref, dga_ref, dgco_ref, carry):
        i = pl.program_id(0)
        rb = nb - 1 - i

        @pl.when(i == 0)
        def _():
            carry[...] = jnp.zeros_like(carry)
            dwc_ref[...] = jnp.zeros_like(dwc_ref)
            dga_ref[...] = jnp.zeros_like(dga_ref)
            dgco_ref[...] = jnp.zeros_like(dgco_ref)

        dmix = _dot_nt(dx2b_ref[...], wo_ref[...])
        dnc, dna = dmix[:, :CONV_CH], dmix[:, CONV_CH:]

        ov = o_ref[...]
        ra = _rms(ov)
        dga_ref[...] += jnp.sum(dna * (ov * ra), axis=0, keepdims=True)
        do = _rms_bwd(ov, ra, dna * ga_ref[...])
        dob = do.astype(BF)
        dot_ref[0] = do.T.astype(BF)
        sel = (_rows((ATTN_W, LANES)) // HEAD_DIM == _cols((ATTN_W, LANES))).astype(F32)
        delta = jnp.dot(do * ov, sel, precision=lax.Precision.HIGH, preferred_element_type=F32)
        featc = _dot(_split_parts(fcol_ref[...] - lcol_ref[...], 1.0), sc_ref[...]).astype(BF)
        featd = _dot(_split_parts(-delta, 0.0), sc_ref[...]).astype(BF)
        lane = _cols((tm, LANES))
        for h in range(N_HEADS):
            hp, hh = divmod(h, 2)
            own = (lane >= HEAD_DIM * hh) & (lane < HEAD_DIM * (hh + 1))
            hs = slice(LANES * h, LANES * (h + 1))
            qab_ref[:, hs] = jnp.where(own, qa_ref[:, hs], featc[:, hs])
            doa_ref[:, hs] = jnp.where(own, dob[:, LANES * hp:LANES * (hp + 1)], featd[:, hs])

        zc_v = zc_ref[...]
        gb, gc, xc = zc_v[:, :CONV_CH], zc_v[:, CONV_CH:2 * CONV_CH], zc_v[:, 2 * CONV_CH:]
        hal = halo_ref[...] * (rb > 0).astype(F32)
        u = gc * xc
        prev = hal[:, CONV_CH:2 * CONV_CH] * hal[:, 2 * CONV_CH:]
        wc = wc_ref[...]
        cv, u1, u2 = _conv3(u, prev, wc)
        y = gb * cv
        rc = _rms(y)
        dgco_ref[...] += jnp.sum(dnc * (y * rc), axis=0, keepdims=True)
        dy = _rms_bwd(y, rc, dnc * gco_ref[...])
        dcv = dy * gb
        dwc_ref[0:1, :] += jnp.sum(dcv * u2, axis=0, keepdims=True)
        dwc_ref[1:2, :] += jnp.sum(dcv * u1, axis=0, keepdims=True)
        dwc_ref[2:3, :] += jnp.sum(dcv * u, axis=0, keepdims=True)
        du = _conv3_bwd(dcv, carry[...], wc)
        carry[...] = dcv[:8, :]
        dzc_ref[:, :CONV_CH] = (dy * cv).astype(BF)
        dzc_ref[:, CONV_CH:2 * CONV_CH] = (du * xc).astype(BF)
        dzc_ref[:, 2 * CONV_CH:] = (du * gc).astype(BF)

    blk = lambda c: pl.BlockSpec((tm, c), lambda i: (nb - 1 - i, 0))
    halo = pl.BlockSpec((8, 3 * CONV_CH), lambda i: (jnp.maximum((nb - 1 - i) * (tm // 8) - 1, 0), 0))
    tr = pl.BlockSpec((1, ATTN_W, tm), lambda i: (nb - 1 - i, 0, 0))
    return pl.pallas_call(
        body, name="bwd_out", grid=(nb,),
        in_specs=[blk(D_MODEL), blk(ATTN_W), blk(3 * CONV_CH), halo, blk(LANES), blk(LANES), blk(aw),
                  _full(w_o.shape), _full((1, ATTN_W)), _full((1, CONV_CH)), _full((8, CONV_CH)), _full((LANES, aw))],
        out_specs=[blk(3 * CONV_CH), blk(aw), tr, blk(aw),
                   _full((8, CONV_CH)), _full((1, ATTN_W)), _full((1, CONV_CH))],
        out_shape=[jax.ShapeDtypeStruct((s, 3 * CONV_CH), BF),
                   jax.ShapeDtypeStruct((s, aw), BF), jax.ShapeDtypeStruct((nb, ATTN_W, tm), BF),
                   jax.ShapeDtypeStruct((s, aw), BF),
                   jax.ShapeDtypeStruct((8, CONV_CH), F32), jax.ShapeDtypeStruct((1, ATTN_W), F32),
                   jax.ShapeDtypeStruct((1, CONV_CH), F32)],
        scratch_shapes=[pltpu.VMEM((8, CONV_CH), F32)],
        compiler_params=_params(VMEM_MID),
    )(dx2b, o, zc, zc, fcol, lcol, qa, w_o, g_attn_out, g_conv_out, w_conv, _bias_scatter()[:, :aw])


def _attn_bwd(qab, doa, ka, qkv, qt, dot, t, blocks):
    s = qab.shape[0]
    nb = s // t
    npair = N_HEADS // 2
    n = len(blocks)

    def body(ka_ref, v_ref, qab_ref, doa_ref, qt_ref, dot_ref, *refs):
        block_refs, (dk_ref, dv_ref, dfk_ref, dqa_ref, dq_ref) = refs[:n], refs[n:n + 5]
        received = refs[n + 5:2 * n + 5]
        acck, accv = refs[2 * n + 5:2 * n + 7]
        hp = pl.program_id(0)
        j = pl.program_id(1)
        start, finish = _scatter_phases(block_refs, received, *refs[2 * n + 7:])
        pl.when((hp == 0) & (j == 0))(start)

        @pl.when(j == 0)
        def _():
            dqa_ref[...] = jnp.zeros_like(dqa_ref)

        causal = _cols((t, t)) <= _rows((t, t))
        row = _rows((LANES, t))
        lane = _cols((t, LANES))
        acck[...] = jnp.zeros(acck.shape, F32)
        accv[...] = jnp.zeros(accv.shape, F32)
        v2 = v_ref[...]
        heads = []
        for hh in range(2):
            aux = HEAD_DIM * (1 - hh)
            ownl = (lane >= HEAD_DIM * hh) & (lane < HEAD_DIM * (hh + 1))
            ones3 = jnp.where((lane >= aux) & (lane < aux + 3), 1.0, 0.0).astype(BF)
            heads.append(dict(hh=hh, hs=slice(hh * LANES, (hh + 1) * LANES), aux=aux,
                              own=(row >= HEAD_DIM * hh) & (row < HEAD_DIM * (hh + 1)),
                              fill=jnp.where(row == aux, 1.0, 0.0).astype(BF),
                              kh=ka_ref[:, hh * LANES:(hh + 1) * LANES], vh=jnp.where(ownl, v2, ones3)))

        def tile(qb, masked):
            rs = pl.ds(pl.multiple_of(qb * t, t), t)
            qt2 = qt_ref[qb]
            dot2 = dot_ref[qb]
            es = [_dot_nt(qab_ref[rs, hd["hs"]], hd["kh"]) for hd in heads]
            dps = [_dot_nt(doa_ref[rs, hd["hs"]], hd["vh"]) for hd in heads]
            aks = [acck[hd["hh"]] for hd in heads]
            avs = [accv[hd["hh"]] for hd in heads]
            dqs = [dqa_ref[rs, hd["hs"]] for hd in heads]
            outs = []
            for hd, e, dp, ak, av, dq in zip(heads, es, dps, aks, avs, dqs):
                if masked:
                    e = jnp.where(causal, e, NEG)
                p = jnp.exp(e)
                ds = (p * dp).astype(BF)
                outs.append((ak + _dot(jnp.where(hd["own"], qt2, hd["fill"]), ds), av + _dot(dot2, p.astype(BF)),
                             dq + _dot(ds, hd["kh"])))
            for hd, (ak, av, dq) in zip(heads, outs):
                acck[hd["hh"]] = ak
                accv[hd["hh"]] = av
                dqa_ref[rs, hd["hs"]] = dq

        def loop_body(qb, carry):
            tile(qb, False)
            return carry

        tile(j, True)
        lax.fori_loop(j + 1, nb, loop_body, 0)
        dk_pair = dv_pair = None
        dfrows = jnp.zeros((LANES, t), F32)
        for hd in heads:
            ak, av = acck[hd["hh"]], accv[hd["hh"]]
            dfrows = jnp.where(row == 2 * hp + hd["hh"], -ak[hd["aux"]:hd["aux"] + 1, :], dfrows)
            dk_pair = ak if hd["hh"] == 0 else jnp.where(row < HEAD_DIM, dk_pair, ak)
            dv_pair = av if hd["hh"] == 0 else jnp.where(row < HEAD_DIM, dv_pair, av)
        dk_ref[...] = dk_pair.T.astype(BF)
        dv_ref[...] = dv_pair.T.astype(BF)
        dfk_ref[0] = dfrows.T

        @pl.when(j == nb - 1)
        def _():
            for r in range(nb):
                rows = slice(r * t, (r + 1) * t)
                dq_ref[rows, :] = (jnp.where(lane < HEAD_DIM, dqa_ref[rows, :LANES], dqa_ref[rows, LANES:])
                                   * 0.125).astype(BF)

        pl.when((hp == npair - 1) & (j == nb - 1))(finish)

    pair_cols = pl.BlockSpec((s, 2 * LANES), lambda hp, j: (0, hp))
    pair_rows = pl.BlockSpec((nb, LANES, t), lambda hp, j: (0, hp, 0))
    any_ = pl.BlockSpec(memory_space=pl.ANY)
    return pl.pallas_call(
        body, name="attn_bwd", grid=(npair, nb),
        in_specs=[pl.BlockSpec((t, 2 * LANES), lambda hp, j: (j, hp)),
                  pl.BlockSpec((t, LANES), lambda hp, j: (j, 2 * npair + hp)),
                  pair_cols, pair_cols, pair_rows, pair_rows] + [any_] * n,
        out_specs=[pl.BlockSpec((t, LANES), lambda hp, j: (j, hp)), pl.BlockSpec((t, LANES), lambda hp, j: (j, hp)),
                   pl.BlockSpec((1, t, LANES), lambda hp, j: (hp, j, 0)), pair_cols,
                   pl.BlockSpec((s, LANES), lambda hp, j: (0, hp))] + [any_] * n,
        out_shape=[jax.ShapeDtypeStruct((s, ATTN_W), BF), jax.ShapeDtypeStruct((s, ATTN_W), BF),
                   jax.ShapeDtypeStruct((npair, s, LANES), F32), jax.ShapeDtypeStruct((s, N_HEADS * LANES), F32),
                   jax.ShapeDtypeStruct((s, ATTN_W), BF)]
        + [jax.ShapeDtypeStruct(b.shape, b.dtype) for b in blocks],
        scratch_shapes=[pltpu.VMEM((2, LANES, t), F32), pltpu.VMEM((2, LANES, t), F32)] + _exchange_sems(n),
        compiler_params=_params(VMEM_BIG),
    )(ka, qkv, qab, doa, qt, dot, *blocks)


def _bwd_in(x, dx2, dzc, dq, dqa, dk, dv, dfk, fpre, w_zc, w_qkv, w_f, g_mix, tm, blocks):
    s = x.shape[0]
    nb = s // tm
    n = len(blocks)

    def body(x_ref, dx2_ref, dzc_ref, dq_ref, dqa_ref, dk_ref, dv_ref, dfk_ref, fpre_ref, wzc_ref, wqkv_ref, wf_ref,
             gm_ref, *refs):
        block_refs, (gx_ref, dfb_ref, dgm_ref, dbf_ref), received = refs[:n], refs[n:n + 4], refs[n + 4:2 * n + 4]
        carry = refs[2 * n + 4]
        i = pl.program_id(0)
        start, finish = _scatter_phases(block_refs, received, *refs[2 * n + 5:])
        pl.when(i == 0)(start)

        @pl.when(i == 0)
        def _():
            carry[...] = jnp.zeros_like(carry)
            dgm_ref[...] = jnp.zeros_like(dgm_ref)
            dbf_ref[...] = jnp.zeros_like(dbf_ref)

        lane = _cols((tm, LANES))
        dfq = jnp.zeros((tm, LANES), F32)
        for h in range(N_HEADS):
            aux = LANES * h + HEAD_DIM * (1 - h % 2)
            dfq = jnp.where(lane == h, dqa_ref[:, aux:aux + 1], dfq)

        triu = (_rows((tm, tm)) <= _cols((tm, tm))).astype(F32)
        df_cum = dfq + ((dfk_ref[0] + dfk_ref[1]) + (dfk_ref[2] + dfk_ref[3]))
        dlogf = jnp.dot(triu, df_cum, precision=HI, preferred_element_type=F32) + carry[...]
        carry[...] = dlogf[0:1, :]
        fpre = fpre_ref[...]
        df = jnp.where(_cols(fpre.shape) < N_HEADS, dlogf / (1.0 + jnp.exp(fpre)), 0.0)
        dbf_ref[...] += jnp.sum(df, axis=0, keepdims=True)
        dfb = df.astype(BF)
        dfb_ref[...] = dfb

        dh1 = _dot_nt(dzc_ref[...], wzc_ref[...])
        dh1 += _dot_nt(dq_ref[...], wqkv_ref[:, :ATTN_W])
        dh1 += _dot_nt(dk_ref[...], wqkv_ref[:, ATTN_W:2 * ATTN_W])
        dh1 += _dot_nt(dv_ref[...], wqkv_ref[:, 2 * ATTN_W:])
        dh1 += _dot_nt(dfb, wf_ref[...])
        xv = x_ref[...]
        r1 = _rms(xv)
        dgm_ref[...] += jnp.sum(dh1 * (xv * r1), axis=0, keepdims=True)
        gx_ref[...] = dx2_ref[...] + _rms_bwd(xv, r1, dh1 * gm_ref[...])
        pl.when(i == nb - 1)(finish)

    blk = lambda c: pl.BlockSpec((tm, c), lambda i: (nb - 1 - i, 0))
    any_ = pl.BlockSpec(memory_space=pl.ANY)
    return pl.pallas_call(
        body, name="bwd_in", grid=(nb,),
        in_specs=[blk(D_MODEL), blk(D_MODEL), blk(3 * CONV_CH), blk(ATTN_W), blk(N_HEADS * LANES), blk(ATTN_W),
                  blk(ATTN_W), pl.BlockSpec((N_HEADS // 2, tm, LANES), lambda i: (0, nb - 1 - i, 0)), blk(LANES),
                  _full(w_zc.shape), _full(w_qkv.shape), _full(w_f.shape), _full((1, D_MODEL))] + [any_] * n,
        out_specs=[blk(D_MODEL), blk(LANES), _full((1, D_MODEL)), _full((1, LANES))] + [any_] * n,
        out_shape=[jax.ShapeDtypeStruct((s, D_MODEL), F32), jax.ShapeDtypeStruct((s, LANES), BF),
                   jax.ShapeDtypeStruct((1, D_MODEL), F32), jax.ShapeDtypeStruct((1, LANES), F32)]
        + [jax.ShapeDtypeStruct(b.shape, b.dtype) for b in blocks],
        scratch_shapes=[pltpu.VMEM((1, LANES), F32)] + _exchange_sems(n),
        compiler_params=_params(VMEM_MID),
    )(x, dx2, dzc, dq, dqa, dk, dv, dfk, fpre, w_zc, w_qkv, w_f, g_mix, *blocks)


def _matmul_tn(a, b, bn, bk, name):
    s, m = a.shape
    n = b.shape[1]

    def body(a_ref, b_ref, o_ref):
        @pl.when(pl.program_id(1) == 0)
        def _():
            o_ref[...] = jnp.zeros_like(o_ref)

        o_ref[...] += _dot_tn(a_ref[...], b_ref[...])

    return pl.pallas_call(
        body, name=name, grid=(n // bn, s // bk),
        in_specs=[pl.BlockSpec((bk, m), lambda jn, k: (k, 0)), pl.BlockSpec((bk, bn), lambda jn, k: (k, jn))],
        out_specs=pl.BlockSpec((m, bn), lambda jn, k: (0, jn)),
        out_shape=jax.ShapeDtypeStruct((m, n), F32),
        compiler_params=_params(VMEM_MID),
    )(a, b)


def _matmul_tn_multi(a, bs, bk, name):
    s, m = a.shape
    nb = len(bs)

    def body(a_ref, *refs):
        b_refs, o_refs = refs[:nb], refs[nb:]

        @pl.when(pl.program_id(0) == 0)
        def _():
            for o_ref in o_refs:
                o_ref[...] = jnp.zeros_like(o_ref)

        at = a_ref[...].T
        for b_ref, o_ref in zip(b_refs, o_refs):
            o_ref[...] += _dot(at, b_ref[...])

    return pl.pallas_call(
        body, name=name, grid=(s // bk,),
        in_specs=[pl.BlockSpec((bk, m), lambda k: (k, 0))]
        + [pl.BlockSpec((bk, b.shape[1]), lambda k: (k, 0)) for b in bs],
        out_specs=[_full((m, b.shape[1])) for b in bs],
        out_shape=[jax.ShapeDtypeStruct((m, b.shape[1]), F32) for b in bs],
        compiler_params=_params(VMEM_BIG),
    )(a, *bs)


def _flip(v, bit):
    return 1 - v if bit else v


def _all_gather(shards):
    n = len(shards)

    def body(*refs):
        start, forward, finish = _gather_phases(refs[:n], refs[n:2 * n], *refs[2 * n:])
        start()
        forward()
        finish()

    any_ = pl.BlockSpec(memory_space=pl.ANY)
    return pl.pallas_call(
        body, name="all_gather_weights",
        in_specs=[any_] * n, out_specs=[any_] * n,
        out_shape=_gathered_shapes(shards), scratch_shapes=_exchange_sems(n),
    )(*shards)


def _gathered_shapes(shards):
    return [jax.ShapeDtypeStruct((8,) + sh.shape, sh.dtype) for sh in shards]


def _exchange_sems(n):
    return [pltpu.SemaphoreType.DMA((7 * n,)), pltpu.SemaphoreType.DMA((7 * n,)), pltpu.SemaphoreType.DMA((n,))]


def _gather_phases(src, out, send_sems, recv_sems, loc_sems):
    n = len(src)
    x, y, c = lax.axis_index("x"), lax.axis_index("y"), lax.axis_index("c")
    me, sibling = (x, y, c), (x, y, 1 - c)
    chips = [(1 - x, y), (x, 1 - y), (1 - x, 1 - y)]

    def slot(a, px, py, pc):
        return out[a].at[4 * px + 2 * py + pc]

    def copy(a, k, block, to, from_src=False):
        return pltpu.make_async_remote_copy(
            src_ref=src[a] if from_src else slot(a, *block), dst_ref=slot(a, *block),
            send_sem=send_sems.at[7 * a + k], recv_sem=recv_sems.at[7 * a + k],
            device_id=to, device_id_type=MESH)

    def local(a):
        return pltpu.make_async_copy(src[a], slot(a, *me), loc_sems.at[a])

    def first(a):
        return [copy(a, 0, me, sibling, True)] + [copy(a, 1 + j, me, (*chip, c), True)
                                                  for j, chip in enumerate(chips)]

    def passed(a, j):
        return copy(a, 4 + j, (*chips[j], c), sibling)

    def start():
        for a in range(n):
            local(a).start()
            for cp in first(a):
                cp.start()

    def forward():
        for a in range(n):
            for j, chip in enumerate(chips):
                copy(a, 1 + j, (*chip, c), me).wait_recv()
                passed(a, j).start()

    def finish():
        for a in range(n):
            copy(a, 0, sibling, me).wait_recv()
            for j, chip in enumerate(chips):
                copy(a, 4 + j, (*chip, 1 - c), me).wait_recv()
        for a in range(n):
            for cp in first(a) + [passed(a, j) for j in range(3)]:
                cp.wait_send()
            local(a).wait()

    return start, forward, finish


def _scatter_phases(src, out, send_sems, recv_sems, loc_sems):
    n = len(src)
    masks = [((k >> 2) & 1, (k >> 1) & 1, k & 1) for k in range(1, 8)]
    x, y, c = lax.axis_index("x"), lax.axis_index("y"), lax.axis_index("c")
    me = 4 * x + 2 * y + c

    def copies():
        cps = []
        for a in range(n):
            cps.append(pltpu.make_async_copy(src[a].at[me], out[a].at[me], loc_sems.at[a]))
            for k, (mx, my, mc) in enumerate(masks):
                px, py, pc = _flip(x, mx), _flip(y, my), _flip(c, mc)
                cps.append(pltpu.make_async_remote_copy(
                    src_ref=src[a].at[4 * px + 2 * py + pc], dst_ref=out[a].at[me],
                    send_sem=send_sems.at[7 * a + k], recv_sem=recv_sems.at[7 * a + k],
                    device_id=(px, py, pc), device_id_type=MESH))
        return cps

    def start():
        for cp in copies():
            cp.start()

    def finish():
        for cp in copies():
            cp.wait()

    return start, finish


def _all_to_all(blocks, name):
    n = len(blocks)

    def body(*refs):
        start, finish = _scatter_phases(refs[:n], refs[n:2 * n], *refs[2 * n:])
        start()
        finish()

    any_ = pl.BlockSpec(memory_space=pl.ANY)
    return pl.pallas_call(
        body, name=name,
        in_specs=[any_] * n, out_specs=[any_] * n,
        out_shape=[jax.ShapeDtypeStruct(b.shape, b.dtype) for b in blocks], scratch_shapes=_exchange_sems(n),
    )(*blocks)


def _adamw(parts, w, m, v, br, name, tail=None):
    g8, r, c = parts.shape
    c1 = 1.0 - ADAM_B1 ** ADAM_STEP
    c2 = 1.0 - ADAM_B2 ** ADAM_STEP
    extra = [] if tail is None else [tail]

    def body(p_ref, w_ref, m_ref, v_ref, *refs):
        g_ref, d_ref, m2_ref, v2_ref = refs[len(extra):]
        g = p_ref[0].astype(F32)
        for d in range(1, g8):
            g = g + p_ref[d].astype(F32)
        if tail is not None:
            k = tail.shape[1]
            place = (_rows((k, c)) + (c - k) == _cols((k, c))).astype(F32)
            g = g + jnp.dot(refs[0][...], place, precision=HI, preferred_element_type=F32)
        m2 = ADAM_B1 * m_ref[...] + (1.0 - ADAM_B1) * g
        v2 = ADAM_B2 * v_ref[...] + (1.0 - ADAM_B2) * (g * g)
        g_ref[...] = g
        m2_ref[...] = m2
        v2_ref[...] = v2
        d_ref[...] = -ADAM_LR * ((m2 / c1) / (jnp.sqrt(v2 / c2) + ADAM_EPS) + ADAM_WD * w_ref[...])

    blk = pl.BlockSpec((br, c), lambda i: (i, 0))
    out = jax.ShapeDtypeStruct((r, c), F32)
    return pl.pallas_call(
        body, name=name, grid=(r // br,),
        in_specs=[pl.BlockSpec((g8, br, c), lambda i: (0, i, 0)), blk, blk, blk]
        + [pl.BlockSpec((br, e.shape[1]), lambda i: (i, 0)) for e in extra],
        out_specs=[blk] * 4, out_shape=[out] * 4,
        compiler_params=_params(VMEM_MID),
    )(parts, w, m, v, *extra)


def _pad_rows(a, rows):
    return jnp.pad(a, ((0, rows - a.shape[0]), (0, 0)))


_SMALL = (("g_mix", 1024), ("b_f", 8), ("g_conv_out", 512), ("g_attn_out", 512), ("g_ffn", 1024), ("g_final", 1024))
_SMALL_LEN = sum(n for _, n in _SMALL)
_SMALL_ROWS = 104


def _pack_small(vals, extra=None):
    parts = [vals[k].reshape(-1) for k, _ in _SMALL]
    if extra is not None:
        parts.append(extra.reshape(-1))
    flat = jnp.concatenate(parts)
    return jnp.pad(flat, (0, _SMALL_ROWS * LANES - flat.shape[0])).reshape(_SMALL_ROWS, LANES)


def _unpack_small(packed, shapes):
    flat = packed.reshape(-1)
    out, off = {}, 0
    for k, n in _SMALL:
        out[k] = flat[off:off + n].reshape(shapes[k])
        off += n
    return out


def _col_blocks(a):
    r, c8 = a.shape
    return jnp.transpose(a.reshape(r, 8, c8 // 8), (1, 0, 2))


def _from_col_blocks(a):
    g, r, c = a.shape
    return jnp.transpose(a, (1, 0, 2)).reshape(r, g * c)


def kernel(x, g_mix, w_in, b_f, w_conv, g_conv_out, g_attn_out, w_o, g_ffn, w_up, w_ffn_conv, w_down, g_final, loss_target, m_g_mix, m_w_in, m_b_f, m_w_conv, m_g_conv_out, m_g_attn_out, m_w_o, m_g_ffn, m_w_up, m_w_ffn_conv, m_w_down, m_g_final, v_g_mix, v_w_in, v_b_f, v_w_conv, v_g_conv_out, v_g_attn_out, v_w_o, v_g_ffn, v_w_up, v_w_ffn_conv, v_w_down, v_g_final):
    w = dict(g_mix=g_mix, w_in=w_in[0], b_f=b_f, w_conv=w_conv[0], g_conv_out=g_conv_out, g_attn_out=g_attn_out,
             w_o=w_o[0], g_ffn=g_ffn, w_up=w_up[0], w_ffn_conv=w_ffn_conv[0], w_down=w_down[0], g_final=g_final)
    m = dict(g_mix=m_g_mix, w_in=m_w_in[0], b_f=m_b_f, w_conv=m_w_conv[0], g_conv_out=m_g_conv_out,
             g_attn_out=m_g_attn_out, w_o=m_w_o[0], g_ffn=m_g_ffn, w_up=m_w_up[0], w_ffn_conv=m_w_ffn_conv[0],
             w_down=m_w_down[0], g_final=m_g_final)
    v = dict(g_mix=v_g_mix, w_in=v_w_in[0], b_f=v_b_f, w_conv=v_w_conv[0], g_conv_out=v_g_conv_out,
             g_attn_out=v_g_attn_out, w_o=v_w_o[0], g_ffn=v_g_ffn, w_up=v_w_up[0], w_ffn_conv=v_w_ffn_conv[0],
             w_down=v_w_down[0], g_final=v_g_final)
    shapes = dict(g_mix=g_mix.shape, w_in=w_in.shape, b_f=b_f.shape, w_conv=w_conv.shape,
                  g_conv_out=g_conv_out.shape, g_attn_out=g_attn_out.shape, w_o=w_o.shape, g_ffn=g_ffn.shape,
                  w_up=w_up.shape, w_ffn_conv=w_ffn_conv.shape, w_down=w_down.shape, g_final=g_final.shape)

    tm = t = 512
    tf = 256
    xs, tgt = x[0], loss_target[0]
    seq = xs.shape[0]
    assert seq % tm == 0 and seq % tf == 0
    bk = 1024 if seq % 1024 == 0 else 512

    g_in, g_conv = _all_gather([w["w_in"].astype(BF), w["w_conv"]])
    w_in_f = _from_col_blocks(g_in)
    w_zc = w_in_f[:, :3 * CONV_CH]
    w_qkv = w_in_f[:, 3 * CONV_CH:3 * CONV_CH + 3 * ATTN_W]
    w_f = jnp.pad(w_in_f[:, 3 * CONV_CH + 3 * ATTN_W:], ((0, 0), (0, LANES - N_HEADS)))
    b_f_p = jnp.pad(b_f, ((0, 0), (0, LANES - N_HEADS)))
    w_conv_p = _pad_rows(_from_col_blocks(g_conv), 8)

    h1, zc, qkv, fpre, fcol, nc, qa, ka, qt, vt = _fwd_in(
        xs, g_mix, w_zc, w_qkv, w_f, b_f_p, w_conv_p, g_conv_out, tm)
    o, lcol, g_o, g_up, g_dn, g_fc = _attn_fwd(
        qa, ka, vt, t, [w["w_o"].astype(BF), w["w_up"].astype(BF), w["w_down"].astype(BF), w["w_ffn_conv"]])
    w_o_f = g_o.reshape(D_MODEL, D_MODEL)
    w_up_f = _from_col_blocks(g_up)
    w_dn_f = g_dn.reshape(D_FF, D_MODEL)
    w_fc_p = _pad_rows(_from_col_blocks(g_fc), 8)
    x2, h2, mix = _fwd_out(xs, nc, o, g_attn_out, w_o_f, g_ffn, tm)
    u0, uc, act, dx3, dx3b, loss, dg_final = _fwd_ffn(
        h2, x2, tgt, w_up_f, w_fc_p, w_dn_f, g_final.reshape(1, D_MODEL), tf)
    loss_part = loss

    du0, dx2, dx2b, dw_fc, dg_ffn = _bwd_ffn(dx3, dx3b, u0, uc, x2, w_up_f.T, w_fc_p, w_dn_f.T, g_ffn, tf)
    dw_up = _matmul_tn(h2, du0, 1408, 2 * bk if seq % (2 * bk) == 0 else bk, "dw_up")
    dw_dn, = _matmul_tn_multi(act, [dx3b], bk, "dw_down")
    dw_o, = _matmul_tn_multi(mix, [dx2b], bk, "dw_o")
    dzc, doa, dot, qab, dw_conv, dg_attn, dg_conv = _bwd_out(
        dx2b, o, zc, fcol, lcol, qa, w_o_f, g_attn_out, g_conv_out, w_conv_p, t)
    dk, dv, dfk, dqa, dq, r_up, r_dn, r_fc, r_o = _attn_bwd(
        qab, doa, ka, qkv, qt, dot, t,
        [_col_blocks(dw_up), dw_dn.reshape(8, D_FF // 8, D_MODEL), _col_blocks(dw_fc[:3]),
         dw_o.reshape(8, D_MODEL // 8, D_MODEL)])
    dw_zc, dw_q, dw_k, dw_v = _matmul_tn_multi(h1, [dzc, dq, dk, dv], bk, "dw_in")
    dw_main = jnp.concatenate([dw_zc, dw_q, dw_k, dw_v, jnp.zeros((D_MODEL, N_HEADS), F32)], axis=1)
    gx, dfb, dg_mix, db_f, r_in = _bwd_in(xs, dx2, dzc, dq, dqa, dk, dv, dfk, fpre, w_zc, w_qkv, w_f, g_mix, tm,
                                          [_col_blocks(dw_main).astype(BF)])
    dw_f = _matmul_tn(h1, dfb, LANES, bk, "dw_in_f")
    small = dict(g_mix=dg_mix, b_f=db_f[:, :N_HEADS], g_conv_out=dg_conv, g_attn_out=dg_attn, g_ffn=dg_ffn,
                 g_final=dg_final)
    riders = jnp.concatenate([loss_part.reshape(-1), dw_f[:, :N_HEADS].reshape(-1)])
    r_conv, r_small = _all_to_all(
        [_col_blocks(dw_conv[:3]), jnp.broadcast_to(_pack_small(small, riders), (8, _SMALL_ROWS, LANES))],
        "all_to_all_grads")

    g_out, d_out, m_out, v_out = {}, {}, {}, {}
    res = _adamw(r_small, _pack_small(w), _pack_small(m), _pack_small(v), _SMALL_ROWS, "adamw_gains")
    for dst, packed in zip((g_out, d_out, m_out, v_out), res):
        dst.update(_unpack_small(packed, shapes))
    summed = res[0].reshape(-1)
    loss = summed[_SMALL_LEN]
    is_last = (4 * lax.axis_index("x") + 2 * lax.axis_index("y") + lax.axis_index("c") == 7).astype(F32)
    dw_f_sum = summed[_SMALL_LEN + 1:_SMALL_LEN + 1 + D_MODEL * N_HEADS].reshape(D_MODEL, N_HEADS) * is_last

    big = ("w_in", "w_o", "w_up", "w_down", "w_conv", "w_ffn_conv")
    recv = [r_in, r_o, r_up, r_dn, r_conv, r_fc]
    rows = dict(w_in=256, w_o=128, w_up=256, w_down=176, w_conv=3, w_ffn_conv=3)
    for a, k in enumerate(big):
        res = _adamw(recv[a], w[k], m[k], v[k], rows[k], "adamw_" + k, tail=dw_f_sum if k == "w_in" else None)
        g_out[k], d_out[k], m_out[k], v_out[k] = [r.reshape(shapes[k]) for r in res]

    order = ("g_mix", "w_in", "b_f", "w_conv", "g_conv_out", "g_attn_out", "w_o", "g_ffn", "w_up", "w_ffn_conv",
             "w_down", "g_final")
    return (loss, gx.reshape(x.shape), *[g_out[k] for k in order], *[d_out[k] for k in order],
            *[m_out[k] for k in order], *[v_out[k] for k in order])
```

```python
import jax
import jax.numpy as jnp
import numpy as np
from jax import lax
from jax.experimental import pallas as pl
from jax.experimental.pallas import tpu as pltpu

F32 = jnp.float32
BF = jnp.bfloat16
HI = lax.Precision.HIGHEST
MESH = pl.DeviceIdType.MESH

D_MODEL = 1024
CONV_CH = 512
ATTN_W = 512
N_HEADS = 8
HEAD_DIM = 64
D_FF = 2816
FF_CHUNK = 256
N_FF_CHUNKS = D_FF // FF_CHUNK
EPS = 1e-6
NEG = -1e30
LANES = 128
VMEM_BIG = 56 * 1024 * 1024
VMEM_MID = 40 * 1024 * 1024

ADAM_LR = 0.001
ADAM_B1 = 0.9
ADAM_B2 = 0.999
ADAM_EPS = 1e-08
ADAM_WD = 0.01
ADAM_STEP = 10

NT = (((1,), (1,)), ((), ()))
TN = (((0,), (0,)), ((), ()))


def _dot(a, b):
    return jnp.dot(a, b, preferred_element_type=F32)


def _dot_nt(a, b):
    return lax.dot_general(a, b, NT, preferred_element_type=F32)


def _dot_tn(a, b):
    return lax.dot_general(a, b, TN, preferred_element_type=F32)


def _params(vmem):
    return pltpu.CompilerParams(vmem_limit_bytes=vmem)


def _rows(shape):
    return lax.broadcasted_iota(jnp.int32, shape, 0)


def _cols(shape):
    return lax.broadcasted_iota(jnp.int32, shape, 1)


def _shift_down(u, prev, k):
    n = prev.shape[0]
    out = pltpu.roll(u, k, 0)
    row = _rows(u.shape)
    for r in range(k):
        out = jnp.where(row == r, prev[n - k + r:n - k + r + 1, :].astype(u.dtype), out)
    return out


def _shift_up(u, nxt, k):
    tm = u.shape[0]
    out = pltpu.roll(u, tm - k, 0)
    row = _rows(u.shape)
    for r in range(k):
        out = jnp.where(row == tm - k + r, nxt[r:r + 1, :], out)
    return out


def _conv3(u, prev, w):
    u1 = _shift_down(u, prev, 1)
    u2 = _shift_down(u, prev, 2)
    return w[0:1, :] * u2 + w[1:2, :] * u1 + w[2:3, :] * u, u1, u2


def _conv3_bwd(d, nxt, w):
    return w[2:3, :] * d + w[1:2, :] * _shift_up(d, nxt, 1) + w[0:1, :] * _shift_up(d, nxt, 2)


def _rms(x):
    return lax.rsqrt(jnp.mean(x * x, axis=-1, keepdims=True) + EPS)


def _rms_bwd(x, r, dyg):
    return r * dyg - x * (r * r * r) * jnp.mean(dyg * x, axis=-1, keepdims=True)


def _full(shape):
    nd = len(shape)
    return pl.BlockSpec(shape, lambda i, _n=nd: (0,) * _n)


ONES_LANE = 24


def _bias_scatter():
    sc = np.zeros((LANES, 2 * N_HEADS * LANES), np.float32)
    koff = N_HEADS * LANES
    for h in range(N_HEADS):
        aux = HEAD_DIM * (1 - h % 2)
        for j in range(3):
            sc[8 * j + h, LANES * h + aux + j] = 1.0
            sc[ONES_LANE, koff + LANES * h + aux + j] = 1.0
            sc[ONES_LANE, LANES * h + aux + 3 + j] = 1.0
            sc[8 * j + h, koff + LANES * h + aux + 3 + j] = -1.0
    return jnp.asarray(sc, BF)


def _split_parts(v, one):
    hi = v.astype(BF).astype(F32)
    rest = v - hi
    mid = rest.astype(BF).astype(F32)
    lo = (rest - mid).astype(BF).astype(F32)
    parts = hi + pltpu.roll(mid, 8, 1) + pltpu.roll(lo, 16, 1)
    return jnp.where(_cols(parts.shape) == ONES_LANE, one, parts).astype(BF)


def _fwd_in(x, g_mix, w_zc, w_qkv, w_f, b_f, w_conv, g_conv_out, tm):
    s = x.shape[0]
    nb = s // tm
    aw = N_HEADS * LANES

    def body(x_ref, gm_ref, wzc_ref, wqkv_ref, wf_ref, bf_ref, wc_ref, gco_ref, sc_ref,
             h1_ref, zc_ref, qkv_ref, fpre_ref, fcol_ref, nc_ref, qa_ref, ka_ref, qt_ref, vt_ref, cu_ref, cf_ref):
        i = pl.program_id(0)

        @pl.when(i == 0)
        def _():
            cu_ref[...] = jnp.zeros_like(cu_ref)
            cf_ref[...] = jnp.zeros_like(cf_ref)

        xv = x_ref[...]
        hb = (xv * _rms(xv) * gm_ref[...]).astype(BF)
        h1_ref[...] = hb
        zc = _dot(hb, wzc_ref[...])
        zc_ref[...] = zc
        qkv = _dot(hb, wqkv_ref[...])
        qkv = jnp.where(_cols(qkv.shape) < ATTN_W, qkv * 0.125, qkv)
        qkvb = qkv.astype(BF)
        qkv_ref[...] = qkvb
        qt_ref[0] = qkv[:, :ATTN_W].T.astype(BF)
        vt_ref[0] = qkv[:, 2 * ATTN_W:].T.astype(BF)

        gb, gc, xc = zc[:, :CONV_CH], zc[:, CONV_CH:2 * CONV_CH], zc[:, 2 * CONV_CH:]
        u = gc * xc
        cv, _, _ = _conv3(u, cu_ref[...], wc_ref[...])
        cu_ref[...] = u[tm - 8:, :]
        y = gb * cv
        nc_ref[...] = (y * _rms(y) * gco_ref[...]).astype(BF)

        fpre = _dot(hb, wf_ref[...]) + bf_ref[...]
        fpre_ref[...] = fpre
        logf = jnp.minimum(fpre, 0.0) - jnp.log1p(jnp.exp(-jnp.abs(fpre)))
        logf = jnp.where(_cols(logf.shape) < N_HEADS, logf, 0.0)
        tri = (_rows((LANES, LANES)) >= _cols((LANES, LANES))).astype(F32)
        offset = cf_ref[...]
        pieces = []
        for r in range(tm // LANES):
            piece = jnp.dot(tri, logf[r * LANES:(r + 1) * LANES, :], precision=HI, preferred_element_type=F32) + offset
            offset = piece[LANES - 1:LANES, :]
            pieces.append(piece)
        fcol = jnp.concatenate(pieces, axis=0)
        cf_ref[...] = offset
        fcol_ref[...] = fcol

        feat = _dot(_split_parts(fcol, 1.0), sc_ref[...]).astype(BF)
        lane = _cols((tm, LANES))
        for h in range(N_HEADS):
            hp, hh = divmod(h, 2)
            own = (lane >= HEAD_DIM * hh) & (lane < HEAD_DIM * (hh + 1))
            hs = slice(LANES * h, LANES * (h + 1))
            qa_ref[:, hs] = jnp.where(own, qkvb[:, LANES * hp:LANES * (hp + 1)], feat[:, hs])
            ka_ref[:, hs] = jnp.where(own, qkvb[:, ATTN_W + LANES * hp:ATTN_W + LANES * (hp + 1)],
                                      feat[:, aw + LANES * h:aw + LANES * (h + 1)])

    blk = lambda c: pl.BlockSpec((tm, c), lambda i: (i, 0))
    return pl.pallas_call(
        body, name="fwd_in", grid=(nb,),
        in_specs=[blk(D_MODEL), _full((1, D_MODEL)), _full(w_zc.shape), _full(w_qkv.shape), _full(w_f.shape),
                  _full((1, LANES)), _full((8, CONV_CH)), _full((1, CONV_CH)), _full((LANES, 2 * aw))],
        out_specs=[blk(D_MODEL), blk(3 * CONV_CH), blk(3 * ATTN_W), blk(LANES), blk(LANES),
                   blk(CONV_CH), blk(aw), blk(aw)] + [pl.BlockSpec((1, ATTN_W, tm), lambda i: (i, 0, 0))] * 2,
        out_shape=[jax.ShapeDtypeStruct((s, D_MODEL), BF), jax.ShapeDtypeStruct((s, 3 * CONV_CH), F32),
                   jax.ShapeDtypeStruct((s, 3 * ATTN_W), BF), jax.ShapeDtypeStruct((s, LANES), F32),
                   jax.ShapeDtypeStruct((s, LANES), F32),
                   jax.ShapeDtypeStruct((s, CONV_CH), BF), jax.ShapeDtypeStruct((s, aw), BF),
                   jax.ShapeDtypeStruct((s, aw), BF)] + [jax.ShapeDtypeStruct((nb, ATTN_W, tm), BF)] * 2,
        scratch_shapes=[pltpu.VMEM((8, CONV_CH), F32), pltpu.VMEM((1, LANES), F32)],
        compiler_params=_params(VMEM_MID),
    )(x, g_mix, w_zc, w_qkv, w_f, b_f, w_conv, g_conv_out, _bias_scatter())


def _pipeline_masked_last(last, produce, consume, buf_a, buf_b):
    produce(0, buf_a)

    def two_blocks(j, carry):
        blk = 2 * j
        produce(blk + 1, buf_b)
        consume(blk, buf_a, False)
        produce(blk + 2, buf_a)
        consume(blk + 1, buf_b, False)
        return carry

    lax.fori_loop(0, last // 2, two_blocks, 0)

    @pl.when(last % 2 == 0)
    def _():
        consume(last, buf_a, True)

    @pl.when(last % 2 == 1)
    def _():
        produce(last, buf_b)
        consume(last - 1, buf_a, False)
        consume(last, buf_b, True)


def _attn_fwd(qa, ka, vt, t, shards):
    s = qa.shape[0]
    nb = s // t
    n = len(shards)

    def body(qa_ref, ka_any, vt_any, *refs):
        shard_refs, (o_ref, l_ref), gathered = refs[:n], refs[n:n + 2], refs[n + 2:2 * n + 2]
        ka_scr, vt_scr, m_scr, acc_scr, ea_scr, eb_scr, sem = refs[2 * n + 2:2 * n + 9]
        i = pl.program_id(0)
        start, forward, finish = _gather_phases(shard_refs, gathered, *refs[2 * n + 9:])
        pl.when(i == 0)(start)
        pl.when(i == nb // 2)(forward)
        _load_weights(i, [(ka_any, ka_scr), (vt_any, vt_scr)], sem)

        causal_t = _rows((t, t)) <= _cols((t, t))
        row = _rows((LANES, t))
        lrows = jnp.zeros((LANES, t), F32)
        for hp in range(N_HEADS // 2):
            ps = slice(hp * LANES, (hp + 1) * LANES)
            m_scr[...] = jnp.full(m_scr.shape, NEG, F32)
            acc_scr[...] = jnp.zeros(acc_scr.shape, F32)
            heads = [dict(hh=hh, hs=slice((2 * hp + hh) * LANES, (2 * hp + hh + 1) * LANES),
                          aux=HEAD_DIM * (1 - hh), own=(row >= HEAD_DIM * hh) & (row < HEAD_DIM * (hh + 1)),
                          fill=jnp.where(row == HEAD_DIM * (1 - hh), 1.0, 0.0).astype(BF),
                          qh=qa_ref[:, (2 * hp + hh) * LANES:(2 * hp + hh + 1) * LANES]) for hh in range(2)]

            def scores(kb, dst, heads=heads):
                rs = pl.ds(pl.multiple_of(kb * t, t), t)
                for hd in heads:
                    dst[hd["hh"]] = _dot_nt(ka_scr[rs, hd["hs"]], hd["qh"])

            def consume(kb, src, masked, ps=ps, heads=heads):
                vt2 = vt_scr[kb, ps, :]
                m_olds = [m_scr[hd["hh"]:hd["hh"] + 1, :] for hd in heads]
                accs = [acc_scr[hd["hh"]] for hd in heads]
                m_news, acc_news = [], []
                for hd, m_old, acc in zip(heads, m_olds, accs):
                    e = src[hd["hh"]]
                    if masked:
                        e = jnp.where(causal_t, e, NEG)
                    m_new = jnp.maximum(m_old, jnp.max(e, axis=0, keepdims=True))
                    p = jnp.exp(e - m_new).astype(BF)
                    vta = jnp.where(hd["own"], vt2, hd["fill"])
                    acc_news.append(jnp.exp(m_old - m_new) * acc + _dot(vta, p))
                    m_news.append(m_new)
                for hd, m_new, acc in zip(heads, m_news, acc_news):
                    acc_scr[hd["hh"]] = acc
                    m_scr[hd["hh"]:hd["hh"] + 1, :] = m_new

            _pipeline_masked_last(i, scores, consume, ea_scr, eb_scr)

            o_pair = None
            for hd in heads:
                hh = hd["hh"]
                acc = acc_scr[hh]
                denom = acc[hd["aux"]:hd["aux"] + 1, :]
                o_h = acc / denom
                lrows = jnp.where(row == 2 * hp + hh, m_scr[hh:hh + 1, :] + jnp.log(denom), lrows)
                o_pair = o_h if hh == 0 else jnp.where(row < HEAD_DIM, o_pair, o_h)
            o_ref[:, ps] = o_pair.T
        l_ref[...] = lrows.T
        pl.when(i == nb - 1)(finish)

    any_ = pl.BlockSpec(memory_space=pl.ANY)
    return pl.pallas_call(
        body, name="attn_fwd", grid=(nb,),
        in_specs=[pl.BlockSpec((t, N_HEADS * LANES), lambda i: (i, 0)), any_, any_] + [any_] * n,
        out_specs=[pl.BlockSpec((t, ATTN_W), lambda i: (i, 0)), pl.BlockSpec((t, LANES), lambda i: (i, 0))]
        + [any_] * n,
        out_shape=[jax.ShapeDtypeStruct((s, ATTN_W), F32), jax.ShapeDtypeStruct((s, LANES), F32)]
        + _gathered_shapes(shards),
        scratch_shapes=[pltpu.VMEM(ka.shape, BF), pltpu.VMEM(vt.shape, BF), pltpu.VMEM((2, t), F32),
                        pltpu.VMEM((2, LANES, t), F32), pltpu.VMEM((2, t, t), F32), pltpu.VMEM((2, t, t), F32),
                        pltpu.SemaphoreType.DMA((2,))] + _exchange_sems(n),
        compiler_params=_params(VMEM_BIG),
    )(qa, ka, vt, *shards)


def _fwd_out(x, nc, o, g_attn_out, w_o, g_ffn, tm):
    s = x.shape[0]

    def body(x_ref, nc_ref, o_ref, ga_ref, wo_ref, gf_ref, x2_ref, h2_ref, mix_ref):
        ov = o_ref[...]
        na = (ov * _rms(ov) * ga_ref[...]).astype(BF)
        ncv = nc_ref[...]
        mix_ref[:, :CONV_CH] = ncv
        mix_ref[:, CONV_CH:] = na
        x2 = x_ref[...] + _dot(ncv, wo_ref[:CONV_CH, :]) + _dot(na, wo_ref[CONV_CH:, :])
        x2_ref[...] = x2
        h2_ref[...] = (x2 * _rms(x2) * gf_ref[...]).astype(BF)

    blk = lambda c: pl.BlockSpec((tm, c), lambda i: (i, 0))
    return pl.pallas_call(
        body, name="fwd_out", grid=(s // tm,),
        in_specs=[blk(D_MODEL), blk(CONV_CH), blk(ATTN_W), _full((1, ATTN_W)), _full(w_o.shape), _full((1, D_MODEL))],
        out_specs=[blk(D_MODEL), blk(D_MODEL), blk(D_MODEL)],
        out_shape=[jax.ShapeDtypeStruct((s, D_MODEL), F32), jax.ShapeDtypeStruct((s, D_MODEL), BF),
                   jax.ShapeDtypeStruct((s, D_MODEL), BF)],
        compiler_params=_params(VMEM_MID),
    )(x, nc, o, g_attn_out, w_o, g_ffn)


def _load_weights(i, pairs, sem):
    @pl.when(i == 0)
    def _():
        cps = [pltpu.make_async_copy(src, dst, sem.at[n]) for n, (src, dst) in enumerate(pairs)]
        for cp in cps:
            cp.start()
        for cp in cps:
            cp.wait()


def _ff_cols(j):
    return (slice(j * FF_CHUNK, (j + 1) * FF_CHUNK), slice(D_FF + j * FF_CHUNK, D_FF + (j + 1) * FF_CHUNK))


def _fwd_ffn(h2, x2, tgt, w_up, w_ffn_conv, w_dn, g_final, tm):
    s = x2.shape[0]

    def body(h2_ref, x2_ref, tgt_ref, wfc_ref, gfin_ref, wup_any, wdn_any,
             u0_ref, uc_ref, act_ref, dx3_ref, dx3b_ref, loss_ref, dgfin_ref,
             wup, wdn, carry, sem):
        i = pl.program_id(0)
        _load_weights(i, [(wup_any, wup), (wdn_any, wdn)], sem)

        @pl.when(i == 0)
        def _():
            carry[...] = jnp.zeros_like(carry)
            loss_ref[...] = jnp.zeros_like(loss_ref)
            dgfin_ref[...] = jnp.zeros_like(dgfin_ref)

        hb = h2_ref[...]

        def up(j):
            return [_dot(hb, wup[:, cc]) for cc in _ff_cols(j)]

        nxt = up(0)
        down = None
        for j in range(N_FF_CHUNKS):
            cur = nxt
            if j + 1 < N_FF_CHUNKS:
                nxt = up(j + 1)
            parts = []
            for cc, u0 in zip(_ff_cols(j), cur):
                u0_ref[:, cc] = u0.astype(BF)
                uu, _, _ = _conv3(u0, carry[:, cc], wfc_ref[:, cc])
                carry[:, cc] = u0[tm - 8:, :]
                uc_ref[:, cc] = uu.astype(BF)
                parts.append(uu)
            ua, ug = parts
            act = (ug * jax.nn.sigmoid(ug) * ua).astype(BF)
            ca = _ff_cols(j)[0]
            act_ref[:, ca] = act
            part = _dot(act, wdn[ca, :])
            down = part if down is None else down + part

        x3 = x2_ref[...] + down
        r3 = _rms(x3)
        gfin = gfin_ref[...]
        xn = x3 * r3
        diff = xn * gfin - tgt_ref[...]
        loss_ref[...] += jnp.sum(jnp.sum(diff * diff, axis=-1, keepdims=True), axis=0, keepdims=True) * (0.5 / D_MODEL)
        dy = diff * (1.0 / D_MODEL)
        dgfin_ref[...] += jnp.sum(dy * xn, axis=0, keepdims=True)
        dx3 = _rms_bwd(x3, r3, dy * gfin)
        dx3_ref[...] = dx3
        dx3b_ref[...] = dx3.astype(BF)

    blk = lambda c: pl.BlockSpec((tm, c), lambda i: (i, 0))
    any_ = pl.BlockSpec(memory_space=pl.ANY)
    return pl.pallas_call(
        body, name="fwd_ffn", grid=(s // tm,),
        in_specs=[blk(D_MODEL), blk(D_MODEL), blk(D_MODEL), _full((8, 2 * D_FF)), _full((1, D_MODEL)), any_, any_],
        out_specs=[blk(2 * D_FF), blk(2 * D_FF), blk(D_FF), blk(D_MODEL), blk(D_MODEL), _full((1, 1)),
                   _full((1, D_MODEL))],
        out_shape=[jax.ShapeDtypeStruct((s, 2 * D_FF), BF), jax.ShapeDtypeStruct((s, 2 * D_FF), BF),
                   jax.ShapeDtypeStruct((s, D_FF), BF),
                   jax.ShapeDtypeStruct((s, D_MODEL), F32), jax.ShapeDtypeStruct((s, D_MODEL), BF),
                   jax.ShapeDtypeStruct((1, 1), F32), jax.ShapeDtypeStruct((1, D_MODEL), F32)],
        scratch_shapes=[pltpu.VMEM(w_up.shape, BF), pltpu.VMEM(w_dn.shape, BF), pltpu.VMEM((8, 2 * D_FF), F32),
                        pltpu.SemaphoreType.DMA((2,))],
        compiler_params=_params(VMEM_BIG),
    )(h2, x2, tgt, w_ffn_conv, g_final, w_up, w_dn)


def _bwd_ffn(dx3, dx3b, u0, uc, x2, w_up, w_ffn_conv, w_dn, g_ffn, tm):
    s = x2.shape[0]
    nb = s // tm

    def body(dx3_ref, dx3b_ref, u0_ref, uc_ref, x2_ref, wfc_ref, gf_ref, wup_any, wdn_any,
             du0_ref, dx2_ref, dx2b_ref, dwfc_ref, dgf_ref,
             wup, wdn, carry, sem):
        i = pl.program_id(0)
        _load_weights(i, [(wup_any, wup), (wdn_any, wdn)], sem)

        @pl.when(i == 0)
        def _():
            carry[...] = jnp.zeros_like(carry)
            dwfc_ref[...] = jnp.zeros_like(dwfc_ref)
            dgf_ref[...] = jnp.zeros_like(dgf_ref)

        db = dx3b_ref[...]

        def dact_of(j):
            return _dot(db, wdn[:, _ff_cols(j)[0]])

        nxt = dact_of(0)
        dh2 = None
        for j in range(N_FF_CHUNKS):
            ca, cg = _ff_cols(j)
            dact = nxt
            if j + 1 < N_FF_CHUNKS:
                nxt = dact_of(j + 1)
            ua = uc_ref[:, ca].astype(F32)
            ug = uc_ref[:, cg].astype(F32)
            sg = jax.nn.sigmoid(ug)
            da = dact * (ug * sg)
            dg = dact * ua * (sg * (1.0 + ug * (1.0 - sg)))
            for cc, d in ((ca, da), (cg, dg)):
                nxt_rows = carry[:, cc]
                d1 = _shift_up(d, nxt_rows, 1)
                d2 = _shift_up(d, nxt_rows, 2)
                u0c = u0_ref[:, cc].astype(F32)
                w = wfc_ref[:, cc]
                dwfc_ref[0:1, cc] += jnp.sum(d2 * u0c, axis=0, keepdims=True)
                dwfc_ref[1:2, cc] += jnp.sum(d1 * u0c, axis=0, keepdims=True)
                dwfc_ref[2:3, cc] += jnp.sum(d * u0c, axis=0, keepdims=True)
                du0 = (w[2:3, :] * d + w[1:2, :] * d1 + w[0:1, :] * d2).astype(BF)
                carry[:, cc] = d[:8, :]
                du0_ref[:, cc] = du0
                part = _dot(du0, wup[cc, :])
                dh2 = part if dh2 is None else dh2 + part

        x2v = x2_ref[...]
        r2 = _rms(x2v)
        dgf_ref[...] += jnp.sum(dh2 * (x2v * r2), axis=0, keepdims=True)
        dx2 = dx3_ref[...] + _rms_bwd(x2v, r2, dh2 * gf_ref[...])
        dx2_ref[...] = dx2
        dx2b_ref[...] = dx2.astype(BF)

    blk = lambda c: pl.BlockSpec((tm, c), lambda i: (nb - 1 - i, 0))
    any_ = pl.BlockSpec(memory_space=pl.ANY)
    return pl.pallas_call(
        body, name="bwd_ffn", grid=(nb,),
        in_specs=[blk(D_MODEL), blk(D_MODEL), blk(2 * D_FF), blk(2 * D_FF), blk(D_MODEL), _full((8, 2 * D_FF)),
                  _full((1, D_MODEL)), any_, any_],
        out_specs=[blk(2 * D_FF), blk(D_MODEL), blk(D_MODEL), _full((8, 2 * D_FF)), _full((1, D_MODEL))],
        out_shape=[jax.ShapeDtypeStruct((s, 2 * D_FF), BF), jax.ShapeDtypeStruct((s, D_MODEL), F32),
                   jax.ShapeDtypeStruct((s, D_MODEL), BF), jax.ShapeDtypeStruct((8, 2 * D_FF), F32),
                   jax.ShapeDtypeStruct((1, D_MODEL), F32)],
        scratch_shapes=[pltpu.VMEM(w_up.shape, BF), pltpu.VMEM(w_dn.shape, BF), pltpu.VMEM((8, 2 * D_FF), F32),
                        pltpu.SemaphoreType.DMA((2,))],
        compiler_params=_params(VMEM_BIG),
    )(dx3, dx3b, u0, uc, x2, w_ffn_conv, g_ffn, w_up, w_dn)


def _bwd_out(dx2b, o, zc, fcol, lcol, qa, w_o, g_attn_out, g_conv_out, w_conv, tm):
    s = o.shape[0]
    nb = s // tm
    aw = N_HEADS * LANES

    def body(dx2b_ref, o_ref, zc_ref, halo_ref, fcol_ref, lcol_ref, qa_ref, wo_ref, ga_ref, gco_ref, wc_ref, sc_ref,
             dzc_ref, doa_ref, dot_ref, qab_ref, dwc_ref, dga_ref, dgco_ref, carry):
        i = pl.program_id(0)
        rb = nb - 1 - i

        @pl.when(i == 0)
        def _():
            carry[...] = jnp.zeros_like(carry)
            dwc_ref[...] = jnp.zeros_like(dwc_ref)
            dga_ref[...] = jnp.zeros_like(dga_ref)
            dgco_ref[...] = jnp.zeros_like(dgco_ref)

        dmix = _dot_nt(dx2b_ref[...], wo_ref[...])
        dnc, dna = dmix[:, :CONV_CH], dmix[:, CONV_CH:]

        ov = o_ref[...]
        ra = _rms(ov)
        dga_ref[...] += jnp.sum(dna * (ov * ra), axis=0, keepdims=True)
        do = _rms_bwd(ov, ra, dna * ga_ref[...])
        dob = do.astype(BF)
        dot_ref[0] = do.T.astype(BF)
        sel = (_rows((ATTN_W, LANES)) // HEAD_DIM == _cols((ATTN_W, LANES))).astype(F32)
        delta = jnp.dot(do * ov, sel, precision=lax.Precision.HIGH, preferred_element_type=F32)
        featc = _dot(_split_parts(fcol_ref[...] - lcol_ref[...], 1.0), sc_ref[...]).astype(BF)
        featd = _dot(_split_parts(-delta, 0.0), sc_ref[...]).astype(BF)
        lane = _cols((tm, LANES))
        for h in range(N_HEADS):
            hp, hh = divmod(h, 2)
            own = (lane >= HEAD_DIM * hh) & (lane < HEAD_DIM * (hh + 1))
            hs = slice(LANES * h, LANES * (h + 1))
            qab_ref[:, hs] = jnp.where(own, qa_ref[:, hs], featc[:, hs])
            doa_ref[:, hs] = jnp.where(own, dob[:, LANES * hp:LANES * (hp + 1)], featd[:, hs])

        zc_v = zc_ref[...]
        gb, gc, xc = zc_v[:, :CONV_CH], zc_v[:, CONV_CH:2 * CONV_CH], zc_v[:, 2 * CONV_CH:]
        hal = halo_ref[...] * (rb > 0).astype(F32)
        u = gc * xc
        prev = hal[:, CONV_CH:2 * CONV_CH] * hal[:, 2 * CONV_CH:]
        wc = wc_ref[...]
        cv, u1, u2 = _conv3(u, prev, wc)
        y = gb * cv
        rc = _rms(y)
        dgco_ref[...] += jnp.sum(dnc * (y * rc), axis=0, keepdims=True)
        dy = _rms_bwd(y, rc, dnc * gco_ref[...])
        dcv = dy * gb
        dwc_ref[0:1, :] += jnp.sum(dcv * u2, axis=0, keepdims=True)
        dwc_ref[1:2, :] += jnp.sum(dcv * u1, axis=0, keepdims=True)
        dwc_ref[2:3, :] += jnp.sum(dcv * u, axis=0, keepdims=True)
        du = _conv3_bwd(dcv, carry[...], wc)
        carry[...] = dcv[:8, :]
        dzc_ref[:, :CONV_CH] = (dy * cv).astype(BF)
        dzc_ref[:, CONV_CH:2 * CONV_CH] = (du * xc).astype(BF)
        dzc_ref[:, 2 * CONV_CH:] = (du * gc).astype(BF)

    blk = lambda c: pl.BlockSpec((tm, c), lambda i: (nb - 1 - i, 0))
    halo = pl.BlockSpec((8, 3 * CONV_CH), lambda i: (jnp.maximum((nb - 1 - i) * (tm // 8) - 1, 0), 0))
    tr = pl.BlockSpec((1, ATTN_W, tm), lambda i: (nb - 1 - i, 0, 0))
    return pl.pallas_call(
        body, name="bwd_out", grid=(nb,),
        in_specs=[blk(D_MODEL), blk(ATTN_W), blk(3 * CONV_CH), halo, blk(LANES), blk(LANES), blk(aw),
                  _full(w_o.shape), _full((1, ATTN_W)), _full((1, CONV_CH)), _full((8, CONV_CH)), _full((LANES, aw))],
        out_specs=[blk(3 * CONV_CH), blk(aw), tr, blk(aw),
                   _full((8, CONV_CH)), _full((1, ATTN_W)), _full((1, CONV_CH))],
        out_shape=[jax.ShapeDtypeStruct((s, 3 * CONV_CH), BF),
                   jax.ShapeDtypeStruct((s, aw), BF), jax.ShapeDtypeStruct((nb, ATTN_W, tm), BF),
                   jax.ShapeDtypeStruct((s, aw), BF),
                   jax.ShapeDtypeStruct((8, CONV_CH), F32), jax.ShapeDtypeStruct((1, ATTN_W), F32),
                   jax.ShapeDtypeStruct((1, CONV_CH), F32)],
        scratch_shapes=[pltpu.VMEM((8, CONV_CH), F32)],
        compiler_params=_params(VMEM_MID),
    )(dx2b, o, zc, zc, fcol, lcol, qa, w_o, g_attn_out, g_conv_out, w_conv, _bias_scatter()[:, :aw])


def _attn_bwd(qab, doa, ka, qkv, qt, dot, t, blocks):
    s = qab.shape[0]
    nb = s // t
    npair = N_HEADS // 2
    n = len(blocks)

    def body(ka_ref, v_ref, qab_ref, doa_ref, qt_ref, dot_ref, *refs):
        block_refs, (dk_ref, dv_ref, dfk_ref, dqa_ref, dq_ref) = refs[:n], refs[n:n + 5]
        received = refs[n + 5:2 * n + 5]
        acck, accv = refs[2 * n + 5:2 * n + 7]
        hp = pl.program_id(0)
        j = pl.program_id(1)
        start, finish = _scatter_phases(block_refs, received, *refs[2 * n + 7:])
        pl.when((hp == 0) & (j == 0))(start)

        @pl.when(j == 0)
        def _():
            dqa_ref[...] = jnp.zeros_like(dqa_ref)

        causal = _cols((t, t)) <= _rows((t, t))
        row = _rows((LANES, t))
        lane = _cols((t, LANES))
        acck[...] = jnp.zeros(acck.shape, F32)
        accv[...] = jnp.zeros(accv.shape, F32)
        v2 = v_ref[...]
        heads = []
        for hh in range(2):
            aux = HEAD_DIM * (1 - hh)
            ownl = (lane >= HEAD_DIM * hh) & (lane < HEAD_DIM * (hh + 1))
            ones3 = jnp.where((lane >= aux) & (lane < aux + 3), 1.0, 0.0).astype(BF)
            heads.append(dict(hh=hh, hs=slice(hh * LANES, (hh + 1) * LANES), aux=aux,
                              own=(row >= HEAD_DIM * hh) & (row < HEAD_DIM * (hh + 1)),
                              fill=jnp.where(row == aux, 1.0, 0.0).astype(BF),
                              kh=ka_ref[:, hh * LANES:(hh + 1) * LANES], vh=jnp.where(ownl, v2, ones3)))

        def tile(qb, masked):
            rs = pl.ds(pl.multiple_of(qb * t, t), t)
            qt2 = qt_ref[qb]
            dot2 = dot_ref[qb]
            es = [_dot_nt(qab_ref[rs, hd["hs"]], hd["kh"]) for hd in heads]
            dps = [_dot_nt(doa_ref[rs, hd["hs"]], hd["vh"]) for hd in heads]
            aks = [acck[hd["hh"]] for hd in heads]
            avs = [accv[hd["hh"]] for hd in heads]
            dqs = [dqa_ref[rs, hd["hs"]] for hd in heads]
            outs = []
            for hd, e, dp, ak, av, dq in zip(heads, es, dps, aks, avs, dqs):
                if masked:
                    e = jnp.where(causal, e, NEG)
                p = jnp.exp(e)
                ds = (p * dp).astype(BF)
                outs.append((ak + _dot(jnp.where(hd["own"], qt2, hd["fill"]), ds), av + _dot(dot2, p.astype(BF)),
                             dq + _dot(ds, hd["kh"])))
            for hd, (ak, av, dq) in zip(heads, outs):
                acck[hd["hh"]] = ak
                accv[hd["hh"]] = av
                dqa_ref[rs, hd["hs"]] = dq

        def loop_body(qb, carry):
            tile(qb, False)
            return carry

        tile(j, True)
        lax.fori_loop(j + 1, nb, loop_body, 0)
        dk_pair = dv_pair = None
        dfrows = jnp.zeros((LANES, t), F32)
        for hd in heads:
            ak, av = acck[hd["hh"]], accv[hd["hh"]]
            dfrows = jnp.where(row == 2 * hp + hd["hh"], -ak[hd["aux"]:hd["aux"] + 1, :], dfrows)
            dk_pair = ak if hd["hh"] == 0 else jnp.where(row < HEAD_DIM, dk_pair, ak)
            dv_pair = av if hd["hh"] == 0 else jnp.where(row < HEAD_DIM, dv_pair, av)
        dk_ref[...] = dk_pair.T.astype(BF)
        dv_ref[...] = dv_pair.T.astype(BF)
        dfk_ref[0] = dfrows.T

        @pl.when(j == nb - 1)
        def _():
            for r in range(nb):
                rows = slice(r * t, (r + 1) * t)
                dq_ref[rows, :] = (jnp.where(lane < HEAD_DIM, dqa_ref[rows, :LANES], dqa_ref[rows, LANES:])
                                   * 0.125).astype(BF)

        pl.when((hp == npair - 1) & (j == nb - 1))(finish)

    pair_cols = pl.BlockSpec((s, 2 * LANES), lambda hp, j: (0, hp))
    pair_rows = pl.BlockSpec((nb, LANES, t), lambda hp, j: (0, hp, 0))
    any_ = pl.BlockSpec(memory_space=pl.ANY)
    return pl.pallas_call(
        body, name="attn_bwd", grid=(npair, nb),
        in_specs=[pl.BlockSpec((t, 2 * LANES), lambda hp, j: (j, hp)),
                  pl.BlockSpec((t, LANES), lambda hp, j: (j, 2 * npair + hp)),
                  pair_cols, pair_cols, pair_rows, pair_rows] + [any_] * n,
        out_specs=[pl.BlockSpec((t, LANES), lambda hp, j: (j, hp)), pl.BlockSpec((t, LANES), lambda hp, j: (j, hp)),
                   pl.BlockSpec((1, t, LANES), lambda hp, j: (hp, j, 0)), pair_cols,
                   pl.BlockSpec((s, LANES), lambda hp, j: (0, hp))] + [any_] * n,
        out_shape=[jax.ShapeDtypeStruct((s, ATTN_W), BF), jax.ShapeDtypeStruct((s, ATTN_W), BF),
                   jax.ShapeDtypeStruct((npair, s, LANES), F32), jax.ShapeDtypeStruct((s, N_HEADS * LANES), F32),
                   jax.ShapeDtypeStruct((s, ATTN_W), BF)]
        + [jax.ShapeDtypeStruct(b.shape, b.dtype) for b in blocks],
        scratch_shapes=[pltpu.VMEM((2, LANES, t), F32), pltpu.VMEM((2, LANES, t), F32)] + _exchange_sems(n),
        compiler_params=_params(VMEM_BIG),
    )(ka, qkv, qab, doa, qt, dot, *blocks)


def _bwd_in(x, dx2, dzc, dq, dqa, dk, dv, dfk, fpre, w_zc, w_qkv, w_f, g_mix, tm, blocks):
    s = x.shape[0]
    nb = s // tm
    n = len(blocks)

    def body(x_ref, dx2_ref, dzc_ref, dq_ref, dqa_ref, dk_ref, dv_ref, dfk_ref, fpre_ref, wzc_ref, wqkv_ref, wf_ref,
             gm_ref, *refs):
        block_refs, (gx_ref, dfb_ref, dgm_ref, dbf_ref), received = refs[:n], refs[n:n + 4], refs[n + 4:2 * n + 4]
        carry = refs[2 * n + 4]
        i = pl.program_id(0)
        start, finish = _scatter_phases(block_refs, received, *refs[2 * n + 5:])
        pl.when(i == 0)(start)

        @pl.when(i == 0)
        def _():
            carry[...] = jnp.zeros_like(carry)
            dgm_ref[...] = jnp.zeros_like(dgm_ref)
            dbf_ref[...] = jnp.zeros_like(dbf_ref)

        lane = _cols((tm, LANES))
        dfq = jnp.zeros((tm, LANES), F32)
        for h in range(N_HEADS):
            aux = LANES * h + HEAD_DIM * (1 - h % 2)
            dfq = jnp.where(lane == h, dqa_ref[:, aux:aux + 1], dfq)

        df_cum = dfq + ((dfk_ref[0] + dfk_ref[1]) + (dfk_ref[2] + dfk_ref[3]))
        triu = (_rows((LANES, LANES)) <= _cols((LANES, LANES))).astype(F32)
        offset = carry[...]
        pieces = []
        for r in reversed(range(tm // LANES)):
            piece = jnp.dot(triu, df_cum[r * LANES:(r + 1) * LANES, :], precision=HI,
                            preferred_element_type=F32) + offset
            offset = piece[0:1, :]
            pieces.append(piece)
        dlogf = jnp.concatenate(pieces[::-1], axis=0)
        carry[...] = offset
        fpre = fpre_ref[...]
        df = jnp.where(_cols(fpre.shape) < N_HEADS, dlogf / (1.0 + jnp.exp(fpre)), 0.0)
        dbf_ref[...] += jnp.sum(df, axis=0, keepdims=True)
        dfb = df.astype(BF)
        dfb_ref[...] = dfb

        dh1 = _dot_nt(dzc_ref[...], wzc_ref[...])
        dh1 += _dot_nt(dq_ref[...], wqkv_ref[:, :ATTN_W])
        dh1 += _dot_nt(dk_ref[...], wqkv_ref[:, ATTN_W:2 * ATTN_W])
        dh1 += _dot_nt(dv_ref[...], wqkv_ref[:, 2 * ATTN_W:])
        dh1 += _dot_nt(dfb, wf_ref[...])
        xv = x_ref[...]
        r1 = _rms(xv)
        dgm_ref[...] += jnp.sum(dh1 * (xv * r1), axis=0, keepdims=True)
        gx_ref[...] = dx2_ref[...] + _rms_bwd(xv, r1, dh1 * gm_ref[...])
        pl.when(i == nb - 1)(finish)

    blk = lambda c: pl.BlockSpec((tm, c), lambda i: (nb - 1 - i, 0))
    any_ = pl.BlockSpec(memory_space=pl.ANY)
    return pl.pallas_call(
        body, name="bwd_in", grid=(nb,),
        in_specs=[blk(D_MODEL), blk(D_MODEL), blk(3 * CONV_CH), blk(ATTN_W), blk(N_HEADS * LANES), blk(ATTN_W),
                  blk(ATTN_W), pl.BlockSpec((N_HEADS // 2, tm, LANES), lambda i: (0, nb - 1 - i, 0)), blk(LANES),
                  _full(w_zc.shape), _full(w_qkv.shape), _full(w_f.shape), _full((1, D_MODEL))] + [any_] * n,
        out_specs=[blk(D_MODEL), blk(LANES), _full((1, D_MODEL)), _full((1, LANES))] + [any_] * n,
        out_shape=[jax.ShapeDtypeStruct((s, D_MODEL), F32), jax.ShapeDtypeStruct((s, LANES), BF),
                   jax.ShapeDtypeStruct((1, D_MODEL), F32), jax.ShapeDtypeStruct((1, LANES), F32)]
        + [jax.ShapeDtypeStruct(b.shape, b.dtype) for b in blocks],
        scratch_shapes=[pltpu.VMEM((1, LANES), F32)] + _exchange_sems(n),
        compiler_params=_params(VMEM_MID),
    )(x, dx2, dzc, dq, dqa, dk, dv, dfk, fpre, w_zc, w_qkv, w_f, g_mix, *blocks)


def _matmul_tn(a, b, bn, bk, name):
    s, m = a.shape
    n = b.shape[1]

    def body(a_ref, b_ref, o_ref):
        @pl.when(pl.program_id(1) == 0)
        def _():
            o_ref[...] = jnp.zeros_like(o_ref)

        o_ref[...] += _dot_tn(a_ref[...], b_ref[...])

    return pl.pallas_call(
        body, name=name, grid=(n // bn, s // bk),
        in_specs=[pl.BlockSpec((bk, m), lambda jn, k: (k, 0)), pl.BlockSpec((bk, bn), lambda jn, k: (k, jn))],
        out_specs=pl.BlockSpec((m, bn), lambda jn, k: (0, jn)),
        out_shape=jax.ShapeDtypeStruct((m, n), F32),
        compiler_params=_params(VMEM_MID),
    )(a, b)


def _matmul_tn_multi(a, bs, bk, name):
    s, m = a.shape
    nb = len(bs)

    def body(a_ref, *refs):
        b_refs, o_refs = refs[:nb], refs[nb:]

        @pl.when(pl.program_id(0) == 0)
        def _():
            for o_ref in o_refs:
                o_ref[...] = jnp.zeros_like(o_ref)

        at = a_ref[...].T
        for b_ref, o_ref in zip(b_refs, o_refs):
            o_ref[...] += _dot(at, b_ref[...])

    return pl.pallas_call(
        body, name=name, grid=(s // bk,),
        in_specs=[pl.BlockSpec((bk, m), lambda k: (k, 0))]
        + [pl.BlockSpec((bk, b.shape[1]), lambda k: (k, 0)) for b in bs],
        out_specs=[_full((m, b.shape[1])) for b in bs],
        out_shape=[jax.ShapeDtypeStruct((m, b.shape[1]), F32) for b in bs],
        compiler_params=_params(VMEM_BIG),
    )(a, *bs)


def _flip(v, bit):
    return 1 - v if bit else v


def _all_gather(shards):
    n = len(shards)

    def body(*refs):
        start, forward, finish = _gather_phases(refs[:n], refs[n:2 * n], *refs[2 * n:])
        start()
        forward()
        finish()

    any_ = pl.BlockSpec(memory_space=pl.ANY)
    return pl.pallas_call(
        body, name="all_gather_weights",
        in_specs=[any_] * n, out_specs=[any_] * n,
        out_shape=_gathered_shapes(shards), scratch_shapes=_exchange_sems(n),
    )(*shards)


def _gathered_shapes(shards):
    return [jax.ShapeDtypeStruct((8,) + sh.shape, sh.dtype) for sh in shards]


def _exchange_sems(n):
    return [pltpu.SemaphoreType.DMA((7 * n,)), pltpu.SemaphoreType.DMA((7 * n,)), pltpu.SemaphoreType.DMA((n,))]


def _gather_phases(src, out, send_sems, recv_sems, loc_sems):
    n = len(src)
    x, y, c = lax.axis_index("x"), lax.axis_index("y"), lax.axis_index("c")
    me, sibling = (x, y, c), (x, y, 1 - c)
    chips = [(1 - x, y), (x, 1 - y), (1 - x, 1 - y)]

    def slot(a, px, py, pc):
        return out[a].at[4 * px + 2 * py + pc]

    def copy(a, k, block, to, from_src=False):
        return pltpu.make_async_remote_copy(
            src_ref=src[a] if from_src else slot(a, *block), dst_ref=slot(a, *block),
            send_sem=send_sems.at[7 * a + k], recv_sem=recv_sems.at[7 * a + k],
            device_id=to, device_id_type=MESH)

    def local(a):
        return pltpu.make_async_copy(src[a], slot(a, *me), loc_sems.at[a])

    def first(a):
        return [copy(a, 0, me, sibling, True)] + [copy(a, 1 + j, me, (*chip, c), True)
                                                  for j, chip in enumerate(chips)]

    def passed(a, j):
        return copy(a, 4 + j, (*chips[j], c), sibling)

    def start():
        for a in range(n):
            local(a).start()
            for cp in first(a):
                cp.start()

    def forward():
        for a in range(n):
            for j, chip in enumerate(chips):
                copy(a, 1 + j, (*chip, c), me).wait_recv()
                passed(a, j).start()

    def finish():
        for a in range(n):
            copy(a, 0, sibling, me).wait_recv()
            for j, chip in enumerate(chips):
                copy(a, 4 + j, (*chip, 1 - c), me).wait_recv()
        for a in range(n):
            for cp in first(a) + [passed(a, j) for j in range(3)]:
                cp.wait_send()
            local(a).wait()

    return start, forward, finish


def _scatter_phases(src, out, send_sems, recv_sems, loc_sems):
    n = len(src)
    masks = [((k >> 2) & 1, (k >> 1) & 1, k & 1) for k in range(1, 8)]
    x, y, c = lax.axis_index("x"), lax.axis_index("y"), lax.axis_index("c")
    me = 4 * x + 2 * y + c

    def copies():
        cps = []
        for a in range(n):
            cps.append(pltpu.make_async_copy(src[a].at[me], out[a].at[me], loc_sems.at[a]))
            for k, (mx, my, mc) in enumerate(masks):
                px, py, pc = _flip(x, mx), _flip(y, my), _flip(c, mc)
                cps.append(pltpu.make_async_remote_copy(
                    src_ref=src[a].at[4 * px + 2 * py + pc], dst_ref=out[a].at[me],
                    send_sem=send_sems.at[7 * a + k], recv_sem=recv_sems.at[7 * a + k],
                    device_id=(px, py, pc), device_id_type=MESH))
        return cps

    def start():
        for cp in copies():
            cp.start()

    def finish():
        for cp in copies():
            cp.wait()

    return start, finish


def _all_to_all(blocks, name):
    n = len(blocks)

    def body(*refs):
        start, finish = _scatter_phases(refs[:n], refs[n:2 * n], *refs[2 * n:])
        start()
        finish()

    any_ = pl.BlockSpec(memory_space=pl.ANY)
    return pl.pallas_call(
        body, name=name,
        in_specs=[any_] * n, out_specs=[any_] * n,
        out_shape=[jax.ShapeDtypeStruct(b.shape, b.dtype) for b in blocks], scratch_shapes=_exchange_sems(n),
    )(*blocks)


def _adamw(parts, w, m, v, br, name, tail=None):
    g8, r, c = parts.shape
    c1 = 1.0 - ADAM_B1 ** ADAM_STEP
    c2 = 1.0 - ADAM_B2 ** ADAM_STEP
    extra = [] if tail is None else [tail]

    def body(p_ref, w_ref, m_ref, v_ref, *refs):
        g_ref, d_ref, m2_ref, v2_ref = refs[len(extra):]
        g = p_ref[0].astype(F32)
        for d in range(1, g8):
            g = g + p_ref[d].astype(F32)
        if tail is not None:
            k = tail.shape[1]
            place = (_rows((k, c)) + (c - k) == _cols((k, c))).astype(F32)
            g = g + jnp.dot(refs[0][...], place, precision=HI, preferred_element_type=F32)
        m2 = ADAM_B1 * m_ref[...] + (1.0 - ADAM_B1) * g
        v2 = ADAM_B2 * v_ref[...] + (1.0 - ADAM_B2) * (g * g)
        g_ref[...] = g
        m2_ref[...] = m2
        v2_ref[...] = v2
        d_ref[...] = -ADAM_LR * ((m2 / c1) / (jnp.sqrt(v2 / c2) + ADAM_EPS) + ADAM_WD * w_ref[...])

    blk = pl.BlockSpec((br, c), lambda i: (i, 0))
    out = jax.ShapeDtypeStruct((r, c), F32)
    return pl.pallas_call(
        body, name=name, grid=(r // br,),
        in_specs=[pl.BlockSpec((g8, br, c), lambda i: (0, i, 0)), blk, blk, blk]
        + [pl.BlockSpec((br, e.shape[1]), lambda i: (i, 0)) for e in extra],
        out_specs=[blk] * 4, out_shape=[out] * 4,
        compiler_params=_params(VMEM_MID),
    )(parts, w, m, v, *extra)


def _pad_rows(a, rows):
    return jnp.pad(a, ((0, rows - a.shape[0]), (0, 0)))


_SMALL = (("g_mix", 1024), ("b_f", 8), ("g_conv_out", 512), ("g_attn_out", 512), ("g_ffn", 1024), ("g_final", 1024))
_SMALL_LEN = sum(n for _, n in _SMALL)
_SMALL_ROWS = 104


def _pack_small(vals, extra=None):
    parts = [vals[k].reshape(-1) for k, _ in _SMALL]
    if extra is not None:
        parts.append(extra.reshape(-1))
    flat = jnp.concatenate(parts)
    return jnp.pad(flat, (0, _SMALL_ROWS * LANES - flat.shape[0])).reshape(_SMALL_ROWS, LANES)


def _unpack_small(packed, shapes):
    flat = packed.reshape(-1)
    out, off = {}, 0
    for k, n in _SMALL:
        out[k] = flat[off:off + n].reshape(shapes[k])
        off += n
    return out


def _col_blocks(a):
    r, c8 = a.shape
    return jnp.transpose(a.reshape(r, 8, c8 // 8), (1, 0, 2))


def _from_col_blocks(a):
    g, r, c = a.shape
    return jnp.transpose(a, (1, 0, 2)).reshape(r, g * c)


def kernel(x, g_mix, w_in, b_f, w_conv, g_conv_out, g_attn_out, w_o, g_ffn, w_up, w_ffn_conv, w_down, g_final, loss_target, m_g_mix, m_w_in, m_b_f, m_w_conv, m_g_conv_out, m_g_attn_out, m_w_o, m_g_ffn, m_w_up, m_w_ffn_conv, m_w_down, m_g_final, v_g_mix, v_w_in, v_b_f, v_w_conv, v_g_conv_out, v_g_attn_out, v_w_o, v_g_ffn, v_w_up, v_w_ffn_conv, v_w_down, v_g_final):
    w = dict(g_mix=g_mix, w_in=w_in[0], b_f=b_f, w_conv=w_conv[0], g_conv_out=g_conv_out, g_attn_out=g_attn_out,
             w_o=w_o[0], g_ffn=g_ffn, w_up=w_up[0], w_ffn_conv=w_ffn_conv[0], w_down=w_down[0], g_final=g_final)
    m = dict(g_mix=m_g_mix, w_in=m_w_in[0], b_f=m_b_f, w_conv=m_w_conv[0], g_conv_out=m_g_conv_out,
             g_attn_out=m_g_attn_out, w_o=m_w_o[0], g_ffn=m_g_ffn, w_up=m_w_up[0], w_ffn_conv=m_w_ffn_conv[0],
             w_down=m_w_down[0], g_final=m_g_final)
    v = dict(g_mix=v_g_mix, w_in=v_w_in[0], b_f=v_b_f, w_conv=v_w_conv[0], g_conv_out=v_g_conv_out,
             g_attn_out=v_g_attn_out, w_o=v_w_o[0], g_ffn=v_g_ffn, w_up=v_w_up[0], w_ffn_conv=v_w_ffn_conv[0],
             w_down=v_w_down[0], g_final=v_g_final)
    shapes = dict(g_mix=g_mix.shape, w_in=w_in.shape, b_f=b_f.shape, w_conv=w_conv.shape,
                  g_conv_out=g_conv_out.shape, g_attn_out=g_attn_out.shape, w_o=w_o.shape, g_ffn=g_ffn.shape,
                  w_up=w_up.shape, w_ffn_conv=w_ffn_conv.shape, w_down=w_down.shape, g_final=g_final.shape)

    tm = t = 512
    tf = 256
    xs, tgt = x[0], loss_target[0]
    seq = xs.shape[0]
    assert seq % tm == 0 and seq % tf == 0
    bk = 1024 if seq % 1024 == 0 else 512

    g_in, g_conv = _all_gather([w["w_in"].astype(BF), w["w_conv"]])
    w_in_f = _from_col_blocks(g_in)
    w_zc = w_in_f[:, :3 * CONV_CH]
    w_qkv = w_in_f[:, 3 * CONV_CH:3 * CONV_CH + 3 * ATTN_W]
    w_f = jnp.pad(w_in_f[:, 3 * CONV_CH + 3 * ATTN_W:], ((0, 0), (0, LANES - N_HEADS)))
    b_f_p = jnp.pad(b_f, ((0, 0), (0, LANES - N_HEADS)))
    w_conv_p = _pad_rows(_from_col_blocks(g_conv), 8)

    h1, zc, qkv, fpre, fcol, nc, qa, ka, qt, vt = _fwd_in(
        xs, g_mix, w_zc, w_qkv, w_f, b_f_p, w_conv_p, g_conv_out, tm)
    o, lcol, g_o, g_up, g_dn, g_fc = _attn_fwd(
        qa, ka, vt, t, [w["w_o"].astype(BF), w["w_up"].astype(BF), w["w_down"].astype(BF), w["w_ffn_conv"]])
    w_o_f = g_o.reshape(D_MODEL, D_MODEL)
    w_up_f = _from_col_blocks(g_up)
    w_dn_f = g_dn.reshape(D_FF, D_MODEL)
    w_fc_p = _pad_rows(_from_col_blocks(g_fc), 8)
    x2, h2, mix = _fwd_out(xs, nc, o, g_attn_out, w_o_f, g_ffn, tm)
    u0, uc, act, dx3, dx3b, loss, dg_final = _fwd_ffn(
        h2, x2, tgt, w_up_f, w_fc_p, w_dn_f, g_final.reshape(1, D_MODEL), tf)
    loss_part = loss

    du0, dx2, dx2b, dw_fc, dg_ffn = _bwd_ffn(dx3, dx3b, u0, uc, x2, w_up_f.T, w_fc_p, w_dn_f.T, g_ffn, tf)
    dw_up = _matmul_tn(h2, du0, 1408, 2 * bk if seq % (2 * bk) == 0 else bk, "dw_up")
    dw_dn, = _matmul_tn_multi(act, [dx3b], bk, "dw_down")
    dw_o, = _matmul_tn_multi(mix, [dx2b], bk, "dw_o")
    dzc, doa, dot, qab, dw_conv, dg_attn, dg_conv = _bwd_out(
        dx2b, o, zc, fcol, lcol, qa, w_o_f, g_attn_out, g_conv_out, w_conv_p, t)
    dk, dv, dfk, dqa, dq, r_up, r_dn, r_fc, r_o = _attn_bwd(
        qab, doa, ka, qkv, qt, dot, t,
        [_col_blocks(dw_up), dw_dn.reshape(8, D_FF // 8, D_MODEL), _col_blocks(dw_fc[:3]),
         dw_o.reshape(8, D_MODEL // 8, D_MODEL)])
    dw_zc, dw_q, dw_k, dw_v = _matmul_tn_multi(h1, [dzc, dq, dk, dv], bk, "dw_in")
    dw_main = jnp.concatenate([dw_zc, dw_q, dw_k, dw_v, jnp.zeros((D_MODEL, N_HEADS), F32)], axis=1)
    gx, dfb, dg_mix, db_f, r_in = _bwd_in(xs, dx2, dzc, dq, dqa, dk, dv, dfk, fpre, w_zc, w_qkv, w_f, g_mix, tm,
                                          [_col_blocks(dw_main).astype(BF)])
    dw_f = _matmul_tn(h1, dfb, LANES, bk, "dw_in_f")
    small = dict(g_mix=dg_mix, b_f=db_f[:, :N_HEADS], g_conv_out=dg_conv, g_attn_out=dg_attn, g_ffn=dg_ffn,
                 g_final=dg_final)
    riders = jnp.concatenate([loss_part.reshape(-1), dw_f[:, :N_HEADS].reshape(-1)])
    r_conv, r_small = _all_to_all(
        [_col_blocks(dw_conv[:3]), jnp.broadcast_to(_pack_small(small, riders), (8, _SMALL_ROWS, LANES))],
        "all_to_all_grads")

    g_out, d_out, m_out, v_out = {}, {}, {}, {}
    res = _adamw(r_small, _pack_small(w), _pack_small(m), _pack_small(v), _SMALL_ROWS, "adamw_gains")
    for dst, packed in zip((g_out, d_out, m_out, v_out), res):
        dst.update(_unpack_small(packed, shapes))
    summed = res[0].reshape(-1)
    loss = summed[_SMALL_LEN]
    is_last = (4 * lax.axis_index("x") + 2 * lax.axis_index("y") + lax.axis_index("c") == 7).astype(F32)
    dw_f_sum = summed[_SMALL_LEN + 1:_SMALL_LEN + 1 + D_MODEL * N_HEADS].reshape(D_MODEL, N_HEADS) * is_last

    big = ("w_in", "w_o", "w_up", "w_down", "w_conv", "w_ffn_conv")
    recv = [r_in, r_o, r_up, r_dn, r_conv, r_fc]
    rows = dict(w_in=256, w_o=128, w_up=256, w_down=176, w_conv=3, w_ffn_conv=3)
    for a, k in enumerate(big):
        res = _adamw(recv[a], w[k], m[k], v[k], rows[k], "adamw_" + k, tail=dw_f_sum if k == "w_in" else None)
        g_out[k], d_out[k], m_out[k], v_out[k] = [r.reshape(shapes[k]) for r in res]

    order = ("g_mix", "w_in", "b_f", "w_conv", "g_conv_out", "g_attn_out", "w_o", "g_ffn", "w_up", "w_ffn_conv",
             "w_down", "g_final")
    return (loss, gx.reshape(x.shape), *[g_out[k] for k in order], *[d_out[k] for k in order],
            *[m_out[k] for k in order], *[v_out[k] for k in order])
```

```python
import jax
import jax.numpy as jnp
import numpy as np
from jax import lax
from jax.experimental import pallas as pl
from jax.experimental.pallas import tpu as pltpu

F32 = jnp.float32
BF = jnp.bfloat16
HI = lax.Precision.HIGHEST
MESH = pl.DeviceIdType.MESH

D_MODEL = 1024
CONV_CH = 512
ATTN_W = 512
N_HEADS = 8
HEAD_DIM = 64
D_FF = 2816
FF_CHUNK = 256
N_FF_CHUNKS = D_FF // FF_CHUNK
EPS = 1e-6
NEG = -1e30
LANES = 128
VMEM_BIG = 56 * 1024 * 1024
VMEM_MID = 40 * 1024 * 1024

ADAM_LR = 0.001
ADAM_B1 = 0.9
ADAM_B2 = 0.999
ADAM_EPS = 1e-08
ADAM_WD = 0.01
ADAM_STEP = 10

NT = (((1,), (1,)), ((), ()))
TN = (((0,), (0,)), ((), ()))


def _dot(a, b):
    return jnp.dot(a, b, preferred_element_type=F32)


def _dot_nt(a, b):
    return lax.dot_general(a, b, NT, preferred_element_type=F32)


def _dot_tn(a, b):
    return lax.dot_general(a, b, TN, preferred_element_type=F32)


def _params(vmem):
    return pltpu.CompilerParams(vmem_limit_bytes=vmem)


def _rows(shape):
    return lax.broadcasted_iota(jnp.int32, shape, 0)


def _cols(shape):
    return lax.broadcasted_iota(jnp.int32, shape, 1)


def _shift_down(u, prev, k):
    n = prev.shape[0]
    out = pltpu.roll(u, k, 0)
    row = _rows(u.shape)
    for r in range(k):
        out = jnp.where(row == r, prev[n - k + r:n - k + r + 1, :].astype(u.dtype), out)
    return out


def _shift_up(u, nxt, k):
    tm = u.shape[0]
    out = pltpu.roll(u, tm - k, 0)
    row = _rows(u.shape)
    for r in range(k):
        out = jnp.where(row == tm - k + r, nxt[r:r + 1, :], out)
    return out


def _conv3(u, prev, w):
    u1 = _shift_down(u, prev, 1)
    u2 = _shift_down(u, prev, 2)
    return w[0:1, :] * u2 + w[1:2, :] * u1 + w[2:3, :] * u, u1, u2


def _conv3_bwd(d, nxt, w):
    return w[2:3, :] * d + w[1:2, :] * _shift_up(d, nxt, 1) + w[0:1, :] * _shift_up(d, nxt, 2)


def _rms(x):
    return lax.rsqrt(jnp.mean(x * x, axis=-1, keepdims=True) + EPS)


def _rms_bwd(x, r, dyg):
    return r * dyg - x * (r * r * r) * jnp.mean(dyg * x, axis=-1, keepdims=True)


def _full(shape):
    nd = len(shape)
    return pl.BlockSpec(shape, lambda i, _n=nd: (0,) * _n)


ONES_LANE = 24


def _bias_scatter():
    sc = np.zeros((LANES, 2 * N_HEADS * LANES), np.float32)
    koff = N_HEADS * LANES
    for h in range(N_HEADS):
        aux = HEAD_DIM * (1 - h % 2)
        for j in range(3):
            sc[8 * j + h, LANES * h + aux + j] = 1.0
            sc[ONES_LANE, koff + LANES * h + aux + j] = 1.0
            sc[ONES_LANE, LANES * h + aux + 3 + j] = 1.0
            sc[8 * j + h, koff + LANES * h + aux + 3 + j] = -1.0
    return jnp.asarray(sc, BF)


def _split_parts(v, one):
    hi = v.astype(BF).astype(F32)
    rest = v - hi
    mid = rest.astype(BF).astype(F32)
    lo = (rest - mid).astype(BF).astype(F32)
    parts = hi + pltpu.roll(mid, 8, 1) + pltpu.roll(lo, 16, 1)
    return jnp.where(_cols(parts.shape) == ONES_LANE, one, parts).astype(BF)


def _fwd_in(x, g_mix, w_zc, w_qkv, w_f, b_f, w_conv, g_conv_out, tm):
    s = x.shape[0]
    nb = s // tm
    aw = N_HEADS * LANES

    def body(x_ref, gm_ref, wzc_ref, wqkv_ref, wf_ref, bf_ref, wc_ref, gco_ref, sc_ref,
             h1_ref, zc_ref, qkv_ref, fpre_ref, fcol_ref, nc_ref, qa_ref, ka_ref, qt_ref, vt_ref, cu_ref, cf_ref):
        i = pl.program_id(0)

        @pl.when(i == 0)
        def _():
            cu_ref[...] = jnp.zeros_like(cu_ref)
            cf_ref[...] = jnp.zeros_like(cf_ref)

        xv = x_ref[...]
        hb = (xv * _rms(xv) * gm_ref[...]).astype(BF)
        h1_ref[...] = hb
        zc = _dot(hb, wzc_ref[...])
        zc_ref[...] = zc
        qkv = _dot(hb, wqkv_ref[...])
        qkv = jnp.where(_cols(qkv.shape) < ATTN_W, qkv * 0.125, qkv)
        qkvb = qkv.astype(BF)
        qkv_ref[...] = qkvb
        qt_ref[0] = qkv[:, :ATTN_W].T.astype(BF)
        vt_ref[0] = qkv[:, 2 * ATTN_W:].T.astype(BF)

        gb, gc, xc = zc[:, :CONV_CH], zc[:, CONV_CH:2 * CONV_CH], zc[:, 2 * CONV_CH:]
        u = gc * xc
        cv, _, _ = _conv3(u, cu_ref[...], wc_ref[...])
        cu_ref[...] = u[tm - 8:, :]
        y = gb * cv
        nc_ref[...] = (y * _rms(y) * gco_ref[...]).astype(BF)

        fpre = _dot(hb, wf_ref[...]) + bf_ref[...]
        fpre_ref[...] = fpre
        logf = jnp.minimum(fpre, 0.0) - jnp.log1p(jnp.exp(-jnp.abs(fpre)))
        logf = jnp.where(_cols(logf.shape) < N_HEADS, logf, 0.0)
        tri = (_rows((LANES, LANES)) >= _cols((LANES, LANES))).astype(F32)
        offset = cf_ref[...]
        pieces = []
        for r in range(tm // LANES):
            piece = jnp.dot(tri, logf[r * LANES:(r + 1) * LANES, :], precision=HI, preferred_element_type=F32) + offset
            offset = piece[LANES - 1:LANES, :]
            pieces.append(piece)
        fcol = jnp.concatenate(pieces, axis=0)
        cf_ref[...] = offset
        fcol_ref[...] = fcol

        feat = _dot(_split_parts(fcol, 1.0), sc_ref[...]).astype(BF)
        lane = _cols((tm, LANES))
        for h in range(N_HEADS):
            hp, hh = divmod(h, 2)
            own = (lane >= HEAD_DIM * hh) & (lane < HEAD_DIM * (hh + 1))
            hs = slice(LANES * h, LANES * (h + 1))
            qa_ref[:, hs] = jnp.where(own, qkvb[:, LANES * hp:LANES * (hp + 1)], feat[:, hs])
            ka_ref[:, hs] = jnp.where(own, qkvb[:, ATTN_W + LANES * hp:ATTN_W + LANES * (hp + 1)],
                                      feat[:, aw + LANES * h:aw + LANES * (h + 1)])

    blk = lambda c: pl.BlockSpec((tm, c), lambda i: (i, 0))
    return pl.pallas_call(
        body, name="fwd_in", grid=(nb,),
        in_specs=[blk(D_MODEL), _full((1, D_MODEL)), _full(w_zc.shape), _full(w_qkv.shape), _full(w_f.shape),
                  _full((1, LANES)), _full((8, CONV_CH)), _full((1, CONV_CH)), _full((LANES, 2 * aw))],
        out_specs=[blk(D_MODEL), blk(3 * CONV_CH), blk(3 * ATTN_W), blk(LANES), blk(LANES),
                   blk(CONV_CH), blk(aw), blk(aw)] + [pl.BlockSpec((1, ATTN_W, tm), lambda i: (i, 0, 0))] * 2,
        out_shape=[jax.ShapeDtypeStruct((s, D_MODEL), BF), jax.ShapeDtypeStruct((s, 3 * CONV_CH), F32),
                   jax.ShapeDtypeStruct((s, 3 * ATTN_W), BF), jax.ShapeDtypeStruct((s, LANES), F32),
                   jax.ShapeDtypeStruct((s, LANES), F32),
                   jax.ShapeDtypeStruct((s, CONV_CH), BF), jax.ShapeDtypeStruct((s, aw), BF),
                   jax.ShapeDtypeStruct((s, aw), BF)] + [jax.ShapeDtypeStruct((nb, ATTN_W, tm), BF)] * 2,
        scratch_shapes=[pltpu.VMEM((8, CONV_CH), F32), pltpu.VMEM((1, LANES), F32)],
        compiler_params=_params(VMEM_MID),
    )(x, g_mix, w_zc, w_qkv, w_f, b_f, w_conv, g_conv_out, _bias_scatter())


def _pipeline_masked_last(last, produce, consume, buf_a, buf_b):
    produce(0, buf_a)

    def two_blocks(j, carry):
        blk = 2 * j
        produce(blk + 1, buf_b)
        consume(blk, buf_a, False)
        produce(blk + 2, buf_a)
        consume(blk + 1, buf_b, False)
        return carry

    lax.fori_loop(0, last // 2, two_blocks, 0)

    @pl.when(last % 2 == 0)
    def _():
        consume(last, buf_a, True)

    @pl.when(last % 2 == 1)
    def _():
        produce(last, buf_b)
        consume(last - 1, buf_a, False)
        consume(last, buf_b, True)


def _attn_fwd(qa, ka, vt, t, shards):
    s = qa.shape[0]
    nb = s // t
    n = len(shards)

    def body(qa_ref, ka_any, vt_any, *refs):
        shard_refs, (o_ref, l_ref), gathered = refs[:n], refs[n:n + 2], refs[n + 2:2 * n + 2]
        ka_scr, vt_scr, m_scr, acc_scr, ea_scr, eb_scr, sem = refs[2 * n + 2:2 * n + 9]
        i = pl.program_id(0)
        start, forward, finish = _gather_phases(shard_refs, gathered, *refs[2 * n + 9:])
        pl.when(i == 0)(start)
        pl.when(i == (3 * nb) // 4)(forward)
        _load_weights(i, [(ka_any, ka_scr), (vt_any, vt_scr)], sem)

        causal_t = _rows((t, t)) <= _cols((t, t))
        row = _rows((LANES, t))
        lrows = jnp.zeros((LANES, t), F32)
        for hp in range(N_HEADS // 2):
            ps = slice(hp * LANES, (hp + 1) * LANES)
            m_scr[...] = jnp.full(m_scr.shape, NEG, F32)
            acc_scr[...] = jnp.zeros(acc_scr.shape, F32)
            heads = [dict(hh=hh, hs=slice((2 * hp + hh) * LANES, (2 * hp + hh + 1) * LANES),
                          aux=HEAD_DIM * (1 - hh), own=(row >= HEAD_DIM * hh) & (row < HEAD_DIM * (hh + 1)),
                          fill=jnp.where(row == HEAD_DIM * (1 - hh), 1.0, 0.0).astype(BF),
                          qh=qa_ref[:, (2 * hp + hh) * LANES:(2 * hp + hh + 1) * LANES]) for hh in range(2)]

            def scores(kb, dst, heads=heads):
                rs = pl.ds(pl.multiple_of(kb * t, t), t)
                for hd in heads:
                    dst[hd["hh"]] = _dot_nt(ka_scr[rs, hd["hs"]], hd["qh"])

            def consume(kb, src, masked, ps=ps, heads=heads):
                vt2 = vt_scr[kb, ps, :]
                m_olds = [m_scr[hd["hh"]:hd["hh"] + 1, :] for hd in heads]
                accs = [acc_scr[hd["hh"]] for hd in heads]
                m_news, acc_news = [], []
                for hd, m_old, acc in zip(heads, m_olds, accs):
                    e = src[hd["hh"]]
                    if masked:
                        e = jnp.where(causal_t, e, NEG)
                    m_new = jnp.maximum(m_old, jnp.max(e, axis=0, keepdims=True))
                    p = jnp.exp(e - m_new).astype(BF)
                    vta = jnp.where(hd["own"], vt2, hd["fill"])
                    acc_news.append(jnp.exp(m_old - m_new) * acc + _dot(vta, p))
                    m_news.append(m_new)
                for hd, m_new, acc in zip(heads, m_news, acc_news):
                    acc_scr[hd["hh"]] = acc
                    m_scr[hd["hh"]:hd["hh"] + 1, :] = m_new

            _pipeline_masked_last(i, scores, consume, ea_scr, eb_scr)

            o_pair = None
            for hd in heads:
                hh = hd["hh"]
                acc = acc_scr[hh]
                denom = acc[hd["aux"]:hd["aux"] + 1, :]
                o_h = acc / denom
                lrows = jnp.where(row == 2 * hp + hh, m_scr[hh:hh + 1, :] + jnp.log(denom), lrows)
                o_pair = o_h if hh == 0 else jnp.where(row < HEAD_DIM, o_pair, o_h)
            o_ref[:, ps] = o_pair.T
        l_ref[...] = lrows.T
        pl.when(i == nb - 1)(finish)

    any_ = pl.BlockSpec(memory_space=pl.ANY)
    return pl.pallas_call(
        body, name="attn_fwd", grid=(nb,),
        in_specs=[pl.BlockSpec((t, N_HEADS * LANES), lambda i: (i, 0)), any_, any_] + [any_] * n,
        out_specs=[pl.BlockSpec((t, ATTN_W), lambda i: (i, 0)), pl.BlockSpec((t, LANES), lambda i: (i, 0))]
        + [any_] * n,
        out_shape=[jax.ShapeDtypeStruct((s, ATTN_W), F32), jax.ShapeDtypeStruct((s, LANES), F32)]
        + _gathered_shapes(shards),
        scratch_shapes=[pltpu.VMEM(ka.shape, BF), pltpu.VMEM(vt.shape, BF), pltpu.VMEM((2, t), F32),
                        pltpu.VMEM((2, LANES, t), F32), pltpu.VMEM((2, t, t), F32), pltpu.VMEM((2, t, t), F32),
                        pltpu.SemaphoreType.DMA((2,))] + _exchange_sems(n),
        compiler_params=_params(VMEM_BIG),
    )(qa, ka, vt, *shards)


def _fwd_out(x, nc, o, g_attn_out, w_o, g_ffn, tm):
    s = x.shape[0]

    def body(x_ref, nc_ref, o_ref, ga_ref, wo_ref, gf_ref, x2_ref, h2_ref, mix_ref):
        ov = o_ref[...]
        na = (ov * _rms(ov) * ga_ref[...]).astype(BF)
        ncv = nc_ref[...]
        mix_ref[:, :CONV_CH] = ncv
        mix_ref[:, CONV_CH:] = na
        x2 = x_ref[...] + _dot(ncv, wo_ref[:CONV_CH, :]) + _dot(na, wo_ref[CONV_CH:, :])
        x2_ref[...] = x2
        h2_ref[...] = (x2 * _rms(x2) * gf_ref[...]).astype(BF)

    blk = lambda c: pl.BlockSpec((tm, c), lambda i: (i, 0))
    return pl.pallas_call(
        body, name="fwd_out", grid=(s // tm,),
        in_specs=[blk(D_MODEL), blk(CONV_CH), blk(ATTN_W), _full((1, ATTN_W)), _full(w_o.shape), _full((1, D_MODEL))],
        out_specs=[blk(D_MODEL), blk(D_MODEL), blk(D_MODEL)],
        out_shape=[jax.ShapeDtypeStruct((s, D_MODEL), F32), jax.ShapeDtypeStruct((s, D_MODEL), BF),
                   jax.ShapeDtypeStruct((s, D_MODEL), BF)],
        compiler_params=_params(VMEM_MID),
    )(x, nc, o, g_attn_out, w_o, g_ffn)


def _load_weights(i, pairs, sem):
    @pl.when(i == 0)
    def _():
        cps = [pltpu.make_async_copy(src, dst, sem.at[n]) for n, (src, dst) in enumerate(pairs)]
        for cp in cps:
            cp.start()
        for cp in cps:
            cp.wait()


def _ff_cols(j):
    return (slice(j * FF_CHUNK, (j + 1) * FF_CHUNK), slice(D_FF + j * FF_CHUNK, D_FF + (j + 1) * FF_CHUNK))


def _fwd_ffn(h2, x2, tgt, w_up, w_ffn_conv, w_dn, g_final, tm):
    s = x2.shape[0]

    def body(h2_ref, x2_ref, tgt_ref, wfc_ref, gfin_ref, wup_any, wdn_any,
             u0_ref, uc_ref, act_ref, dx3_ref, dx3b_ref, loss_ref, dgfin_ref,
             wup, wdn, carry, sem):
        i = pl.program_id(0)
        _load_weights(i, [(wup_any, wup), (wdn_any, wdn)], sem)

        @pl.when(i == 0)
        def _():
            carry[...] = jnp.zeros_like(carry)
            loss_ref[...] = jnp.zeros_like(loss_ref)
            dgfin_ref[...] = jnp.zeros_like(dgfin_ref)

        hb = h2_ref[...]

        def up(j):
            return [_dot(hb, wup[:, cc]) for cc in _ff_cols(j)]

        nxt = up(0)
        down = None
        for j in range(N_FF_CHUNKS):
            cur = nxt
            if j + 1 < N_FF_CHUNKS:
                nxt = up(j + 1)
            parts = []
            for cc, u0 in zip(_ff_cols(j), cur):
                u0_ref[:, cc] = u0.astype(BF)
                uu, _, _ = _conv3(u0, carry[:, cc], wfc_ref[:, cc])
                carry[:, cc] = u0[tm - 8:, :]
                uc_ref[:, cc] = uu.astype(BF)
                parts.append(uu)
            ua, ug = parts
            act = (ug * jax.nn.sigmoid(ug) * ua).astype(BF)
            ca = _ff_cols(j)[0]
            act_ref[:, ca] = act
            part = _dot(act, wdn[ca, :])
            down = part if down is None else down + part

        x3 = x2_ref[...] + down
        r3 = _rms(x3)
        gfin = gfin_ref[...]
        xn = x3 * r3
        diff = xn * gfin - tgt_ref[...]
        loss_ref[...] += jnp.sum(jnp.sum(diff * diff, axis=-1, keepdims=True), axis=0, keepdims=True) * (0.5 / D_MODEL)
        dy = diff * (1.0 / D_MODEL)
        dgfin_ref[...] += jnp.sum(dy * xn, axis=0, keepdims=True)
        dx3 = _rms_bwd(x3, r3, dy * gfin)
        dx3_ref[...] = dx3
        dx3b_ref[...] = dx3.astype(BF)

    blk = lambda c: pl.BlockSpec((tm, c), lambda i: (i, 0))
    any_ = pl.BlockSpec(memory_space=pl.ANY)
    return pl.pallas_call(
        body, name="fwd_ffn", grid=(s // tm,),
        in_specs=[blk(D_MODEL), blk(D_MODEL), blk(D_MODEL), _full((8, 2 * D_FF)), _full((1, D_MODEL)), any_, any_],
        out_specs=[blk(2 * D_FF), blk(2 * D_FF), blk(D_FF), blk(D_MODEL), blk(D_MODEL), _full((1, 1)),
                   _full((1, D_MODEL))],
        out_shape=[jax.ShapeDtypeStruct((s, 2 * D_FF), BF), jax.ShapeDtypeStruct((s, 2 * D_FF), BF),
                   jax.ShapeDtypeStruct((s, D_FF), BF),
                   jax.ShapeDtypeStruct((s, D_MODEL), F32), jax.ShapeDtypeStruct((s, D_MODEL), BF),
                   jax.ShapeDtypeStruct((1, 1), F32), jax.ShapeDtypeStruct((1, D_MODEL), F32)],
        scratch_shapes=[pltpu.VMEM(w_up.shape, BF), pltpu.VMEM(w_dn.shape, BF), pltpu.VMEM((8, 2 * D_FF), F32),
                        pltpu.SemaphoreType.DMA((2,))],
        compiler_params=_params(VMEM_BIG),
    )(h2, x2, tgt, w_ffn_conv, g_final, w_up, w_dn)


def _bwd_ffn(dx3, dx3b, u0, uc, x2, w_up, w_ffn_conv, w_dn, g_ffn, tm):
    s = x2.shape[0]
    nb = s // tm

    def body(dx3_ref, dx3b_ref, u0_ref, uc_ref, x2_ref, wfc_ref, gf_ref, wup_any, wdn_any,
             du0_ref, dx2_ref, dx2b_ref, dwfc_ref, dgf_ref,
             wup, wdn, carry, sem):
        i = pl.program_id(0)
        _load_weights(i, [(wup_any, wup), (wdn_any, wdn)], sem)

        @pl.when(i == 0)
        def _():
            carry[...] = jnp.zeros_like(carry)
            dwfc_ref[...] = jnp.zeros_like(dwfc_ref)
            dgf_ref[...] = jnp.zeros_like(dgf_ref)

        db = dx3b_ref[...]

        def dact_of(j):
            return _dot(db, wdn[:, _ff_cols(j)[0]])

        nxt = dact_of(0)
        dh2 = None
        for j in range(N_FF_CHUNKS):
            ca, cg = _ff_cols(j)
            dact = nxt
            if j + 1 < N_FF_CHUNKS:
                nxt = dact_of(j + 1)
            ua = uc_ref[:, ca].astype(F32)
            ug = uc_ref[:, cg].astype(F32)
            sg = jax.nn.sigmoid(ug)
            da = dact * (ug * sg)
            dg = dact * ua * (sg * (1.0 + ug * (1.0 - sg)))
            for cc, d in ((ca, da), (cg, dg)):
                nxt_rows = carry[:, cc]
                d1 = _shift_up(d, nxt_rows, 1)
                d2 = _shift_up(d, nxt_rows, 2)
                u0c = u0_ref[:, cc].astype(F32)
                w = wfc_ref[:, cc]
                dwfc_ref[0:1, cc] += jnp.sum(d2 * u0c, axis=0, keepdims=True)
                dwfc_ref[1:2, cc] += jnp.sum(d1 * u0c, axis=0, keepdims=True)
                dwfc_ref[2:3, cc] += jnp.sum(d * u0c, axis=0, keepdims=True)
                du0 = (w[2:3, :] * d + w[1:2, :] * d1 + w[0:1, :] * d2).astype(BF)
                carry[:, cc] = d[:8, :]
                du0_ref[:, cc] = du0
                part = _dot(du0, wup[cc, :])
                dh2 = part if dh2 is None else dh2 + part

        x2v = x2_ref[...]
        r2 = _rms(x2v)
        dgf_ref[...] += jnp.sum(dh2 * (x2v * r2), axis=0, keepdims=True)
        dx2 = dx3_ref[...] + _rms_bwd(x2v, r2, dh2 * gf_ref[...])
        dx2_ref[...] = dx2
        dx2b_ref[...] = dx2.astype(BF)

    blk = lambda c: pl.BlockSpec((tm, c), lambda i: (nb - 1 - i, 0))
    any_ = pl.BlockSpec(memory_space=pl.ANY)
    return pl.pallas_call(
        body, name="bwd_ffn", grid=(nb,),
        in_specs=[blk(D_MODEL), blk(D_MODEL), blk(2 * D_FF), blk(2 * D_FF), blk(D_MODEL), _full((8, 2 * D_FF)),
                  _full((1, D_MODEL)), any_, any_],
        out_specs=[blk(2 * D_FF), blk(D_MODEL), blk(D_MODEL), _full((8, 2 * D_FF)), _full((1, D_MODEL))],
        out_shape=[jax.ShapeDtypeStruct((s, 2 * D_FF), BF), jax.ShapeDtypeStruct((s, D_MODEL), F32),
                   jax.ShapeDtypeStruct((s, D_MODEL), BF), jax.ShapeDtypeStruct((8, 2 * D_FF), F32),
                   jax.ShapeDtypeStruct((1, D_MODEL), F32)],
        scratch_shapes=[pltpu.VMEM(w_up.shape, BF), pltpu.VMEM(w_dn.shape, BF), pltpu.VMEM((8, 2 * D_FF), F32),
                        pltpu.SemaphoreType.DMA((2,))],
        compiler_params=_params(VMEM_BIG),
    )(dx3, dx3b, u0, uc, x2, w_ffn_conv, g_ffn, w_up, w_dn)


def _bwd_out(dx2b, o, zc, fcol, lcol, qa, w_o, g_attn_out, g_conv_out, w_conv, tm):
    s = o.shape[0]
    nb = s // tm
    aw = N_HEADS * LANES

    def body(dx2b_ref, o_ref, zc_ref, halo_ref, fcol_ref, lcol_ref, qa_ref, wo_ref, ga_ref, gco_ref, wc_ref, sc_ref,
             dzc_ref, doa_ref, dot_ref, qab_ref, dwc_ref, dga_ref, dgco_ref, carry):
        i = pl.program_id(0)
        rb = nb - 1 - i

        @pl.when(i == 0)
        def _():
            carry[...] = jnp.zeros_like(carry)
            dwc_ref[...] = jnp.zeros_like(dwc_ref)
            dga_ref[...] = jnp.zeros_like(dga_ref)
            dgco_ref[...] = jnp.zeros_like(dgco_ref)

        dmix = _dot_nt(dx2b_ref[...], wo_ref[...])
        dnc, dna = dmix[:, :CONV_CH], dmix[:, CONV_CH:]

        ov = o_ref[...]
        ra = _rms(ov)
        dga_ref[...] += jnp.sum(dna * (ov * ra), axis=0, keepdims=True)
        do = _rms_bwd(ov, ra, dna * ga_ref[...])
        dob = do.astype(BF)
        dot_ref[0] = do.T.astype(BF)
        sel = (_rows((ATTN_W, LANES)) // HEAD_DIM == _cols((ATTN_W, LANES))).astype(F32)
        delta = jnp.dot(do * ov, sel, precision=lax.Precision.HIGH, preferred_element_type=F32)
        featc = _dot(_split_parts(fcol_ref[...] - lcol_ref[...], 1.0), sc_ref[...]).astype(BF)
        featd = _dot(_split_parts(-delta, 0.0), sc_ref[...]).astype(BF)
        lane = _cols((tm, LANES))
        for h in range(N_HEADS):
            hp, hh = divmod(h, 2)
            own = (lane >= HEAD_DIM * hh) & (lane < HEAD_DIM * (hh + 1))
            hs = slice(LANES * h, LANES * (h + 1))
            qab_ref[:, hs] = jnp.where(own, qa_ref[:, hs], featc[:, hs])
            doa_ref[:, hs] = jnp.where(own, dob[:, LANES * hp:LANES * (hp + 1)], featd[:, hs])

        zc_v = zc_ref[...]
        gb, gc, xc = zc_v[:, :CONV_CH], zc_v[:, CONV_CH:2 * CONV_CH], zc_v[:, 2 * CONV_CH:]
        hal = halo_ref[...] * (rb > 0).astype(F32)
        u = gc * xc
        prev = hal[:, CONV_CH:2 * CONV_CH] * hal[:, 2 * CONV_CH:]
        wc = wc_ref[...]
        cv, u1, u2 = _conv3(u, prev, wc)
        y = gb * cv
        rc = _rms(y)
        dgco_ref[...] += jnp.sum(dnc * (y * rc), axis=0, keepdims=True)
        dy = _rms_bwd(y, rc, dnc * gco_ref[...])
        dcv = dy * gb
        dwc_ref[0:1, :] += jnp.sum(dcv * u2, axis=0, keepdims=True)
        dwc_ref[1:2, :] += jnp.sum(dcv * u1, axis=0, keepdims=True)
        dwc_ref[2:3, :] += jnp.sum(dcv * u, axis=0, keepdims=True)
        du = _conv3_bwd(dcv, carry[...], wc)
        carry[...] = dcv[:8, :]
        dzc_ref[:, :CONV_CH] = (dy * cv).astype(BF)
        dzc_ref[:, CONV_CH:2 * CONV_CH] = (du * xc).astype(BF)
        dzc_ref[:, 2 * CONV_CH:] = (du * gc).astype(BF)

    blk = lambda c: pl.BlockSpec((tm, c), lambda i: (nb - 1 - i, 0))
    halo = pl.BlockSpec((8, 3 * CONV_CH), lambda i: (jnp.maximum((nb - 1 - i) * (tm // 8) - 1, 0), 0))
    tr = pl.BlockSpec((1, ATTN_W, tm), lambda i: (nb - 1 - i, 0, 0))
    return pl.pallas_call(
        body, name="bwd_out", grid=(nb,),
        in_specs=[blk(D_MODEL), blk(ATTN_W), blk(3 * CONV_CH), halo, blk(LANES), blk(LANES), blk(aw),
                  _full(w_o.shape), _full((1, ATTN_W)), _full((1, CONV_CH)), _full((8, CONV_CH)), _full((LANES, aw))],
        out_specs=[blk(3 * CONV_CH), blk(aw), tr, blk(aw),
                   _full((8, CONV_CH)), _full((1, ATTN_W)), _full((1, CONV_CH))],
        out_shape=[jax.ShapeDtypeStruct((s, 3 * CONV_CH), BF),
                   jax.ShapeDtypeStruct((s, aw), BF), jax.ShapeDtypeStruct((nb, ATTN_W, tm), BF),
                   jax.ShapeDtypeStruct((s, aw), BF),
                   jax.ShapeDtypeStruct((8, CONV_CH), F32), jax.ShapeDtypeStruct((1, ATTN_W), F32),
                   jax.ShapeDtypeStruct((1, CONV_CH), F32)],
        scratch_shapes=[pltpu.VMEM((8, CONV_CH), F32)],
        compiler_params=_params(VMEM_MID),
    )(dx2b, o, zc, zc, fcol, lcol, qa, w_o, g_attn_out, g_conv_out, w_conv, _bias_scatter()[:, :aw])


def _attn_bwd(qab, doa, ka, qkv, qt, dot, t, blocks):
    s = qab.shape[0]
    nb = s // t
    npair = N_HEADS // 2
    n = len(blocks)

    def body(ka_ref, v_ref, qab_ref, doa_ref, qt_ref, dot_ref, *refs):
        block_refs, (dk_ref, dv_ref, dfk_ref, dqa_ref, dq_ref) = refs[:n], refs[n:n + 5]
        received = refs[n + 5:2 * n + 5]
        acck, accv = refs[2 * n + 5:2 * n + 7]
        hp = pl.program_id(0)
        j = pl.program_id(1)
        start, finish = _scatter_phases(block_refs, received, *refs[2 * n + 7:])
        pl.when((hp == 0) & (j == 0))(start)

        @pl.when(j == 0)
        def _():
            dqa_ref[...] = jnp.zeros_like(dqa_ref)

        causal = _cols((t, t)) <= _rows((t, t))
        row = _rows((LANES, t))
        lane = _cols((t, LANES))
        acck[...] = jnp.zeros(acck.shape, F32)
        accv[...] = jnp.zeros(accv.shape, F32)
        v2 = v_ref[...]
        heads = []
        for hh in range(2):
            aux = HEAD_DIM * (1 - hh)
            ownl = (lane >= HEAD_DIM * hh) & (lane < HEAD_DIM * (hh + 1))
            ones3 = jnp.where((lane >= aux) & (lane < aux + 3), 1.0, 0.0).astype(BF)
            heads.append(dict(hh=hh, hs=slice(hh * LANES, (hh + 1) * LANES), aux=aux,
                              own=(row >= HEAD_DIM * hh) & (row < HEAD_DIM * (hh + 1)),
                              fill=jnp.where(row == aux, 1.0, 0.0).astype(BF),
                              kh=ka_ref[:, hh * LANES:(hh + 1) * LANES], vh=jnp.where(ownl, v2, ones3)))

        def tile(qb, masked):
            rs = pl.ds(pl.multiple_of(qb * t, t), t)
            qt2 = qt_ref[qb]
            dot2 = dot_ref[qb]
            es = [_dot_nt(qab_ref[rs, hd["hs"]], hd["kh"]) for hd in heads]
            dps = [_dot_nt(doa_ref[rs, hd["hs"]], hd["vh"]) for hd in heads]
            aks = [acck[hd["hh"]] for hd in heads]
            avs = [accv[hd["hh"]] for hd in heads]
            dqs = [dqa_ref[rs, hd["hs"]] for hd in heads]
            outs = []
            for hd, e, dp, ak, av, dq in zip(heads, es, dps, aks, avs, dqs):
                if masked:
                    e = jnp.where(causal, e, NEG)
                p = jnp.exp(e)
                ds = (p * dp).astype(BF)
                outs.append((ak + _dot(jnp.where(hd["own"], qt2, hd["fill"]), ds), av + _dot(dot2, p.astype(BF)),
                             dq + _dot(ds, hd["kh"])))
            for hd, (ak, av, dq) in zip(heads, outs):
                acck[hd["hh"]] = ak
                accv[hd["hh"]] = av
                dqa_ref[rs, hd["hs"]] = dq

        def loop_body(qb, carry):
            tile(qb, False)
            return carry

        tile(j, True)
        lax.fori_loop(j + 1, nb, loop_body, 0)
        dk_pair = dv_pair = None
        dfrows = jnp.zeros((LANES, t), F32)
        for hd in heads:
            ak, av = acck[hd["hh"]], accv[hd["hh"]]
            dfrows = jnp.where(row == 2 * hp + hd["hh"], -ak[hd["aux"]:hd["aux"] + 1, :], dfrows)
            dk_pair = ak if hd["hh"] == 0 else jnp.where(row < HEAD_DIM, dk_pair, ak)
            dv_pair = av if hd["hh"] == 0 else jnp.where(row < HEAD_DIM, dv_pair, av)
        dk_ref[...] = dk_pair.T.astype(BF)
        dv_ref[...] = dv_pair.T.astype(BF)
        dfk_ref[0] = dfrows.T

        @pl.when(j == nb - 1)
        def _():
            for r in range(nb):
                rows = slice(r * t, (r + 1) * t)
                dq_ref[rows, :] = (jnp.where(lane < HEAD_DIM, dqa_ref[rows, :LANES], dqa_ref[rows, LANES:])
                                   * 0.125).astype(BF)

        pl.when((hp == npair - 1) & (j == nb - 1))(finish)

    pair_cols = pl.BlockSpec((s, 2 * LANES), lambda hp, j: (0, hp))
    pair_rows = pl.BlockSpec((nb, LANES, t), lambda hp, j: (0, hp, 0))
    any_ = pl.BlockSpec(memory_space=pl.ANY)
    return pl.pallas_call(
        body, name="attn_bwd", grid=(npair, nb),
        in_specs=[pl.BlockSpec((t, 2 * LANES), lambda hp, j: (j, hp)),
                  pl.BlockSpec((t, LANES), lambda hp, j: (j, 2 * npair + hp)),
                  pair_cols, pair_cols, pair_rows, pair_rows] + [any_] * n,
        out_specs=[pl.BlockSpec((t, LANES), lambda hp, j: (j, hp)), pl.BlockSpec((t, LANES), lambda hp, j: (j, hp)),
                   pl.BlockSpec((1, t, LANES), lambda hp, j: (hp, j, 0)), pair_cols,
                   pl.BlockSpec((s, LANES), lambda hp, j: (0, hp))] + [any_] * n,
        out_shape=[jax.ShapeDtypeStruct((s, ATTN_W), BF), jax.ShapeDtypeStruct((s, ATTN_W), BF),
                   jax.ShapeDtypeStruct((npair, s, LANES), F32), jax.ShapeDtypeStruct((s, N_HEADS * LANES), F32),
                   jax.ShapeDtypeStruct((s, ATTN_W), BF)]
        + [jax.ShapeDtypeStruct(b.shape, b.dtype) for b in blocks],
        scratch_shapes=[pltpu.VMEM((2, LANES, t), F32), pltpu.VMEM((2, LANES, t), F32)] + _exchange_sems(n),
        compiler_params=_params(VMEM_BIG),
    )(ka, qkv, qab, doa, qt, dot, *blocks)


def _bwd_in(x, dx2, dzc, dq, dqa, dk, dv, dfk, fpre, w_zc, w_qkv, w_f, g_mix, tm, blocks):
    s = x.shape[0]
    nb = s // tm
    n = len(blocks)

    def body(x_ref, dx2_ref, dzc_ref, dq_ref, dqa_ref, dk_ref, dv_ref, dfk_ref, fpre_ref, wzc_ref, wqkv_ref, wf_ref,
             gm_ref, *refs):
        block_refs, (gx_ref, dfb_ref, dgm_ref, dbf_ref), received = refs[:n], refs[n:n + 4], refs[n + 4:2 * n + 4]
        carry = refs[2 * n + 4]
        i = pl.program_id(0)
        start, finish = _scatter_phases(block_refs, received, *refs[2 * n + 5:])
        pl.when(i == 0)(start)

        @pl.when(i == 0)
        def _():
            carry[...] = jnp.zeros_like(carry)
            dgm_ref[...] = jnp.zeros_like(dgm_ref)
            dbf_ref[...] = jnp.zeros_like(dbf_ref)

        lane = _cols((tm, LANES))
        dfq = jnp.zeros((tm, LANES), F32)
        for h in range(N_HEADS):
            aux = LANES * h + HEAD_DIM * (1 - h % 2)
            dfq = jnp.where(lane == h, dqa_ref[:, aux:aux + 1], dfq)

        df_cum = dfq + ((dfk_ref[0] + dfk_ref[1]) + (dfk_ref[2] + dfk_ref[3]))
        triu = (_rows((LANES, LANES)) <= _cols((LANES, LANES))).astype(F32)
        offset = carry[...]
        pieces = []
        for r in reversed(range(tm // LANES)):
            piece = jnp.dot(triu, df_cum[r * LANES:(r + 1) * LANES, :], precision=HI,
                            preferred_element_type=F32) + offset
            offset = piece[0:1, :]
            pieces.append(piece)
        dlogf = jnp.concatenate(pieces[::-1], axis=0)
        carry[...] = offset
        fpre = fpre_ref[...]
        df = jnp.where(_cols(fpre.shape) < N_HEADS, dlogf / (1.0 + jnp.exp(fpre)), 0.0)
        dbf_ref[...] += jnp.sum(df, axis=0, keepdims=True)
        dfb = df.astype(BF)
        dfb_ref[...] = dfb

        dh1 = _dot_nt(dzc_ref[...], wzc_ref[...])
        dh1 += _dot_nt(dq_ref[...], wqkv_ref[:, :ATTN_W])
        dh1 += _dot_nt(dk_ref[...], wqkv_ref[:, ATTN_W:2 * ATTN_W])
        dh1 += _dot_nt(dv_ref[...], wqkv_ref[:, 2 * ATTN_W:])
        dh1 += _dot_nt(dfb, wf_ref[...])
        xv = x_ref[...]
        r1 = _rms(xv)
        dgm_ref[...] += jnp.sum(dh1 * (xv * r1), axis=0, keepdims=True)
        gx_ref[...] = dx2_ref[...] + _rms_bwd(xv, r1, dh1 * gm_ref[...])
        pl.when(i == nb - 1)(finish)

    blk = lambda c: pl.BlockSpec((tm, c), lambda i: (nb - 1 - i, 0))
    any_ = pl.BlockSpec(memory_space=pl.ANY)
    return pl.pallas_call(
        body, name="bwd_in", grid=(nb,),
        in_specs=[blk(D_MODEL), blk(D_MODEL), blk(3 * CONV_CH), blk(ATTN_W), blk(N_HEADS * LANES), blk(ATTN_W),
                  blk(ATTN_W), pl.BlockSpec((N_HEADS // 2, tm, LANES), lambda i: (0, nb - 1 - i, 0)), blk(LANES),
                  _full(w_zc.shape), _full(w_qkv.shape), _full(w_f.shape), _full((1, D_MODEL))] + [any_] * n,
        out_specs=[blk(D_MODEL), blk(LANES), _full((1, D_MODEL)), _full((1, LANES))] + [any_] * n,
        out_shape=[jax.ShapeDtypeStruct((s, D_MODEL), F32), jax.ShapeDtypeStruct((s, LANES), BF),
                   jax.ShapeDtypeStruct((1, D_MODEL), F32), jax.ShapeDtypeStruct((1, LANES), F32)]
        + [jax.ShapeDtypeStruct(b.shape, b.dtype) for b in blocks],
        scratch_shapes=[pltpu.VMEM((1, LANES), F32)] + _exchange_sems(n),
        compiler_params=_params(VMEM_MID),
    )(x, dx2, dzc, dq, dqa, dk, dv, dfk, fpre, w_zc, w_qkv, w_f, g_mix, *blocks)


def _matmul_tn(a, b, bn, bk, name):
    s, m = a.shape
    n = b.shape[1]

    def body(a_ref, b_ref, o_ref):
        @pl.when(pl.program_id(1) == 0)
        def _():
            o_ref[...] = jnp.zeros_like(o_ref)

        o_ref[...] += _dot_tn(a_ref[...], b_ref[...])

    return pl.pallas_call(
        body, name=name, grid=(n // bn, s // bk),
        in_specs=[pl.BlockSpec((bk, m), lambda jn, k: (k, 0)), pl.BlockSpec((bk, bn), lambda jn, k: (k, jn))],
        out_specs=pl.BlockSpec((m, bn), lambda jn, k: (0, jn)),
        out_shape=jax.ShapeDtypeStruct((m, n), F32),
        compiler_params=_params(VMEM_MID),
    )(a, b)


def _matmul_tn_multi(a, bs, bk, name):
    s, m = a.shape
    nb = len(bs)

    def body(a_ref, *refs):
        b_refs, o_refs = refs[:nb], refs[nb:]

        @pl.when(pl.program_id(0) == 0)
        def _():
            for o_ref in o_refs:
                o_ref[...] = jnp.zeros_like(o_ref)

        at = a_ref[...].T
        for b_ref, o_ref in zip(b_refs, o_refs):
            o_ref[...] += _dot(at, b_ref[...])

    return pl.pallas_call(
        body, name=name, grid=(s // bk,),
        in_specs=[pl.BlockSpec((bk, m), lambda k: (k, 0))]
        + [pl.BlockSpec((bk, b.shape[1]), lambda k: (k, 0)) for b in bs],
        out_specs=[_full((m, b.shape[1])) for b in bs],
        out_shape=[jax.ShapeDtypeStruct((m, b.shape[1]), F32) for b in bs],
        compiler_params=_params(VMEM_BIG),
    )(a, *bs)


def _flip(v, bit):
    return 1 - v if bit else v


def _all_gather(shards):
    n = len(shards)

    def body(*refs):
        start, forward, finish = _gather_phases(refs[:n], refs[n:2 * n], *refs[2 * n:])
        start()
        forward()
        finish()

    any_ = pl.BlockSpec(memory_space=pl.ANY)
    return pl.pallas_call(
        body, name="all_gather_weights",
        in_specs=[any_] * n, out_specs=[any_] * n,
        out_shape=_gathered_shapes(shards), scratch_shapes=_exchange_sems(n),
    )(*shards)


def _gathered_shapes(shards):
    return [jax.ShapeDtypeStruct((8,) + sh.shape, sh.dtype) for sh in shards]


def _exchange_sems(n):
    return [pltpu.SemaphoreType.DMA((7 * n,)), pltpu.SemaphoreType.DMA((7 * n,)), pltpu.SemaphoreType.DMA((n,))]


def _gather_phases(src, out, send_sems, recv_sems, loc_sems):
    n = len(src)
    x, y, c = lax.axis_index("x"), lax.axis_index("y"), lax.axis_index("c")
    me, sibling = (x, y, c), (x, y, 1 - c)
    chips = [(1 - x, y), (x, 1 - y), (1 - x, 1 - y)]

    def slot(a, px, py, pc):
        return out[a].at[4 * px + 2 * py + pc]

    def copy(a, k, block, to, from_src=False):
        return pltpu.make_async_remote_copy(
            src_ref=src[a] if from_src else slot(a, *block), dst_ref=slot(a, *block),
            send_sem=send_sems.at[7 * a + k], recv_sem=recv_sems.at[7 * a + k],
            device_id=to, device_id_type=MESH)

    def local(a):
        return pltpu.make_async_copy(src[a], slot(a, *me), loc_sems.at[a])

    def first(a):
        return [copy(a, 0, me, sibling, True)] + [copy(a, 1 + j, me, (*chip, c), True)
                                                  for j, chip in enumerate(chips)]

    def passed(a, j):
        return copy(a, 4 + j, (*chips[j], c), sibling)

    def start():
        for a in range(n):
            local(a).start()
            for cp in first(a):
                cp.start()

    def forward():
        for a in range(n):
            for j, chip in enumerate(chips):
                copy(a, 1 + j, (*chip, c), me).wait_recv()
                passed(a, j).start()

    def finish():
        for a in range(n):
            copy(a, 0, sibling, me).wait_recv()
            for j, chip in enumerate(chips):
                copy(a, 4 + j, (*chip, 1 - c), me).wait_recv()
        for a in range(n):
            for cp in first(a) + [passed(a, j) for j in range(3)]:
                cp.wait_send()
            local(a).wait()

    return start, forward, finish


def _scatter_phases(src, out, send_sems, recv_sems, loc_sems):
    n = len(src)
    masks = [((k >> 2) & 1, (k >> 1) & 1, k & 1) for k in range(1, 8)]
    x, y, c = lax.axis_index("x"), lax.axis_index("y"), lax.axis_index("c")
    me = 4 * x + 2 * y + c

    def copies():
        cps = []
        for a in range(n):
            cps.append(pltpu.make_async_copy(src[a].at[me], out[a].at[me], loc_sems.at[a]))
            for k, (mx, my, mc) in enumerate(masks):
                px, py, pc = _flip(x, mx), _flip(y, my), _flip(c, mc)
                cps.append(pltpu.make_async_remote_copy(
                    src_ref=src[a].at[4 * px + 2 * py + pc], dst_ref=out[a].at[me],
                    send_sem=send_sems.at[7 * a + k], recv_sem=recv_sems.at[7 * a + k],
                    device_id=(px, py, pc), device_id_type=MESH))
        return cps

    def start():
        for cp in copies():
            cp.start()

    def finish():
        for cp in copies():
            cp.wait()

    return start, finish


def _all_to_all(blocks, name):
    n = len(blocks)

    def body(*refs):
        start, finish = _scatter_phases(refs[:n], refs[n:2 * n], *refs[2 * n:])
        start()
        finish()

    any_ = pl.BlockSpec(memory_space=pl.ANY)
    return pl.pallas_call(
        body, name=name,
        in_specs=[any_] * n, out_specs=[any_] * n,
        out_shape=[jax.ShapeDtypeStruct(b.shape, b.dtype) for b in blocks], scratch_shapes=_exchange_sems(n),
    )(*blocks)


def _adamw(parts, w, m, v, br, name, tail=None):
    g8, r, c = parts.shape
    c1 = 1.0 - ADAM_B1 ** ADAM_STEP
    c2 = 1.0 - ADAM_B2 ** ADAM_STEP
    extra = [] if tail is None else [tail]

    def body(p_ref, w_ref, m_ref, v_ref, *refs):
        g_ref, d_ref, m2_ref, v2_ref = refs[len(extra):]
        g = p_ref[0].astype(F32)
        for d in range(1, g8):
            g = g + p_ref[d].astype(F32)
        if tail is not None:
            k = tail.shape[1]
            place = (_rows((k, c)) + (c - k) == _cols((k, c))).astype(F32)
            g = g + jnp.dot(refs[0][...], place, precision=HI, preferred_element_type=F32)
        m2 = ADAM_B1 * m_ref[...] + (1.0 - ADAM_B1) * g
        v2 = ADAM_B2 * v_ref[...] + (1.0 - ADAM_B2) * (g * g)
        g_ref[...] = g
        m2_ref[...] = m2
        v2_ref[...] = v2
        d_ref[...] = -ADAM_LR * ((m2 / c1) / (jnp.sqrt(v2 / c2) + ADAM_EPS) + ADAM_WD * w_ref[...])

    blk = pl.BlockSpec((br, c), lambda i: (i, 0))
    out = jax.ShapeDtypeStruct((r, c), F32)
    return pl.pallas_call(
        body, name=name, grid=(r // br,),
        in_specs=[pl.BlockSpec((g8, br, c), lambda i: (0, i, 0)), blk, blk, blk]
        + [pl.BlockSpec((br, e.shape[1]), lambda i: (i, 0)) for e in extra],
        out_specs=[blk] * 4, out_shape=[out] * 4,
        compiler_params=_params(VMEM_MID),
    )(parts, w, m, v, *extra)


def _pad_rows(a, rows):
    return jnp.pad(a, ((0, rows - a.shape[0]), (0, 0)))


_SMALL = (("g_mix", 1024), ("b_f", 8), ("g_conv_out", 512), ("g_attn_out", 512), ("g_ffn", 1024), ("g_final", 1024))
_SMALL_LEN = sum(n for _, n in _SMALL)
_SMALL_ROWS = 104


def _pack_small(vals, extra=None):
    parts = [vals[k].reshape(-1) for k, _ in _SMALL]
    if extra is not None:
        parts.append(extra.reshape(-1))
    flat = jnp.concatenate(parts)
    return jnp.pad(flat, (0, _SMALL_ROWS * LANES - flat.shape[0])).reshape(_SMALL_ROWS, LANES)


def _unpack_small(packed, shapes):
    flat = packed.reshape(-1)
    out, off = {}, 0
    for k, n in _SMALL:
        out[k] = flat[off:off + n].reshape(shapes[k])
        off += n
    return out


def _col_blocks(a):
    r, c8 = a.shape
    return jnp.transpose(a.reshape(r, 8, c8 // 8), (1, 0, 2))


def _from_col_blocks(a):
    g, r, c = a.shape
    return jnp.transpose(a, (1, 0, 2)).reshape(r, g * c)


def kernel(x, g_mix, w_in, b_f, w_conv, g_conv_out, g_attn_out, w_o, g_ffn, w_up, w_ffn_conv, w_down, g_final, loss_target, m_g_mix, m_w_in, m_b_f, m_w_conv, m_g_conv_out, m_g_attn_out, m_w_o, m_g_ffn, m_w_up, m_w_ffn_conv, m_w_down, m_g_final, v_g_mix, v_w_in, v_b_f, v_w_conv, v_g_conv_out, v_g_attn_out, v_w_o, v_g_ffn, v_w_up, v_w_ffn_conv, v_w_down, v_g_final):
    w = dict(g_mix=g_mix, w_in=w_in[0], b_f=b_f, w_conv=w_conv[0], g_conv_out=g_conv_out, g_attn_out=g_attn_out,
             w_o=w_o[0], g_ffn=g_ffn, w_up=w_up[0], w_ffn_conv=w_ffn_conv[0], w_down=w_down[0], g_final=g_final)
    m = dict(g_mix=m_g_mix, w_in=m_w_in[0], b_f=m_b_f, w_conv=m_w_conv[0], g_conv_out=m_g_conv_out,
             g_attn_out=m_g_attn_out, w_o=m_w_o[0], g_ffn=m_g_ffn, w_up=m_w_up[0], w_ffn_conv=m_w_ffn_conv[0],
             w_down=m_w_down[0], g_final=m_g_final)
    v = dict(g_mix=v_g_mix, w_in=v_w_in[0], b_f=v_b_f, w_conv=v_w_conv[0], g_conv_out=v_g_conv_out,
             g_attn_out=v_g_attn_out, w_o=v_w_o[0], g_ffn=v_g_ffn, w_up=v_w_up[0], w_ffn_conv=v_w_ffn_conv[0],
             w_down=v_w_down[0], g_final=v_g_final)
    shapes = dict(g_mix=g_mix.shape, w_in=w_in.shape, b_f=b_f.shape, w_conv=w_conv.shape,
                  g_conv_out=g_conv_out.shape, g_attn_out=g_attn_out.shape, w_o=w_o.shape, g_ffn=g_ffn.shape,
                  w_up=w_up.shape, w_ffn_conv=w_ffn_conv.shape, w_down=w_down.shape, g_final=g_final.shape)

    tm = t = 512
    tf = 256
    xs, tgt = x[0], loss_target[0]
    seq = xs.shape[0]
    assert seq % tm == 0 and seq % tf == 0
    bk = 1024 if seq % 1024 == 0 else 512

    g_in, g_conv = _all_gather([w["w_in"].astype(BF), w["w_conv"]])
    w_in_f = _from_col_blocks(g_in)
    w_zc = w_in_f[:, :3 * CONV_CH]
    w_qkv = w_in_f[:, 3 * CONV_CH:3 * CONV_CH + 3 * ATTN_W]
    w_f = jnp.pad(w_in_f[:, 3 * CONV_CH + 3 * ATTN_W:], ((0, 0), (0, LANES - N_HEADS)))
    b_f_p = jnp.pad(b_f, ((0, 0), (0, LANES - N_HEADS)))
    w_conv_p = _pad_rows(_from_col_blocks(g_conv), 8)

    h1, zc, qkv, fpre, fcol, nc, qa, ka, qt, vt = _fwd_in(
        xs, g_mix, w_zc, w_qkv, w_f, b_f_p, w_conv_p, g_conv_out, tm)
    o, lcol, g_o, g_up, g_dn, g_fc = _attn_fwd(
        qa, ka, vt, t, [w["w_o"].astype(BF), w["w_up"].astype(BF), w["w_down"].astype(BF), w["w_ffn_conv"]])
    w_o_f = g_o.reshape(D_MODEL, D_MODEL)
    w_up_f = _from_col_blocks(g_up)
    w_dn_f = g_dn.reshape(D_FF, D_MODEL)
    w_fc_p = _pad_rows(_from_col_blocks(g_fc), 8)
    x2, h2, mix = _fwd_out(xs, nc, o, g_attn_out, w_o_f, g_ffn, tm)
    u0, uc, act, dx3, dx3b, loss, dg_final = _fwd_ffn(
        h2, x2, tgt, w_up_f, w_fc_p, w_dn_f, g_final.reshape(1, D_MODEL), tf)
    loss_part = loss

    du0, dx2, dx2b, dw_fc, dg_ffn = _bwd_ffn(dx3, dx3b, u0, uc, x2, w_up_f.T, w_fc_p, w_dn_f.T, g_ffn, tf)
    dw_up = _matmul_tn(h2, du0, 1408, 2 * bk if seq % (2 * bk) == 0 else bk, "dw_up")
    dw_dn, = _matmul_tn_multi(act, [dx3b], bk, "dw_down")
    dw_o, = _matmul_tn_multi(mix, [dx2b], bk, "dw_o")
    dzc, doa, dot, qab, dw_conv, dg_attn, dg_conv = _bwd_out(
        dx2b, o, zc, fcol, lcol, qa, w_o_f, g_attn_out, g_conv_out, w_conv_p, t)
    dk, dv, dfk, dqa, dq, r_up, r_dn, r_fc, r_o = _attn_bwd(
        qab, doa, ka, qkv, qt, dot, t,
        [_col_blocks(dw_up), dw_dn.reshape(8, D_FF // 8, D_MODEL), _col_blocks(dw_fc[:3]),
         dw_o.reshape(8, D_MODEL // 8, D_MODEL)])
    dw_zc, dw_q, dw_k, dw_v = _matmul_tn_multi(h1, [dzc, dq, dk, dv], bk, "dw_in")
    dw_main = jnp.concatenate([dw_zc, dw_q, dw_k, dw_v, jnp.zeros((D_MODEL, N_HEADS), F32)], axis=1)
    gx, dfb, dg_mix, db_f, r_in = _bwd_in(xs, dx2, dzc, dq, dqa, dk, dv, dfk, fpre, w_zc, w_qkv, w_f, g_mix, tm,
                                          [_col_blocks(dw_main).astype(BF)])
    dw_f = _matmul_tn(h1, dfb, LANES, bk, "dw_in_f")
    small = dict(g_mix=dg_mix, b_f=db_f[:, :N_HEADS], g_conv_out=dg_conv, g_attn_out=dg_attn, g_ffn=dg_ffn,
                 g_final=dg_final)
    riders = jnp.concatenate([loss_part.reshape(-1), dw_f[:, :N_HEADS].reshape(-1)])
    r_conv, r_small = _all_to_all(
        [_col_blocks(dw_conv[:3]), jnp.broadcast_to(_pack_small(small, riders), (8, _SMALL_ROWS, LANES))],
        "all_to_all_grads")

    g_out, d_out, m_out, v_out = {}, {}, {}, {}
    res = _adamw(r_small, _pack_small(w), _pack_small(m), _pack_small(v), _SMALL_ROWS, "adamw_gains")
    for dst, packed in zip((g_out, d_out, m_out, v_out), res):
        dst.update(_unpack_small(packed, shapes))
    summed = res[0].reshape(-1)
    loss = summed[_SMALL_LEN]
    is_last = (4 * lax.axis_index("x") + 2 * lax.axis_index("y") + lax.axis_index("c") == 7).astype(F32)
    dw_f_sum = summed[_SMALL_LEN + 1:_SMALL_LEN + 1 + D_MODEL * N_HEADS].reshape(D_MODEL, N_HEADS) * is_last

    big = ("w_in", "w_o", "w_up", "w_down", "w_conv", "w_ffn_conv")
    recv = [r_in, r_o, r_up, r_dn, r_conv, r_fc]
    rows = dict(w_in=256, w_o=128, w_up=256, w_down=176, w_conv=3, w_ffn_conv=3)
    for a, k in enumerate(big):
        res = _adamw(recv[a], w[k], m[k], v[k], rows[k], "adamw_" + k, tail=dw_f_sum if k == "w_in" else None)
        g_out[k], d_out[k], m_out[k], v_out[k] = [r.reshape(shapes[k]) for r in res]

    order = ("g_mix", "w_in", "b_f", "w_conv", "g_conv_out", "g_attn_out", "w_o", "g_ffn", "w_up", "w_ffn_conv",
             "w_down", "g_final")
    return (loss, gx.reshape(x.shape), *[g_out[k] for k in order], *[d_out[k] for k in order],
            *[m_out[k] for k in order], *[v_out[k] for k in order])
```

```python
import jax
import jax.numpy as jnp
import numpy as np
from jax import lax
from jax.experimental import pallas as pl
from jax.experimental.pallas import tpu as pltpu

F32 = jnp.float32
BF = jnp.bfloat16
HI = lax.Precision.HIGHEST
MESH = pl.DeviceIdType.MESH

D_MODEL = 1024
CONV_CH = 512
ATTN_W = 512
N_HEADS = 8
HEAD_DIM = 64
D_FF = 2816
FF_CHUNK = 256
N_FF_CHUNKS = D_FF // FF_CHUNK
EPS = 1e-6
NEG = -1e30
LANES = 128
VMEM_BIG = 56 * 1024 * 1024
VMEM_MID = 40 * 1024 * 1024

ADAM_LR = 0.001
ADAM_B1 = 0.9
ADAM_B2 = 0.999
ADAM_EPS = 1e-08
ADAM_WD = 0.01
ADAM_STEP = 10

NT = (((1,), (1,)), ((), ()))
TN = (((0,), (0,)), ((), ()))


def _dot(a, b):
    return jnp.dot(a, b, preferred_element_type=F32)


def _dot_nt(a, b):
    return lax.dot_general(a, b, NT, preferred_element_type=F32)


def _dot_tn(a, b):
    return lax.dot_general(a, b, TN, preferred_element_type=F32)


def _params(vmem):
    return pltpu.CompilerParams(vmem_limit_bytes=vmem)


def _rows(shape):
    return lax.broadcasted_iota(jnp.int32, shape, 0)


def _cols(shape):
    return lax.broadcasted_iota(jnp.int32, shape, 1)


def _shift_down(u, prev, k):
    n = prev.shape[0]
    out = pltpu.roll(u, k, 0)
    row = _rows(u.shape)
    for r in range(k):
        out = jnp.where(row == r, prev[n - k + r:n - k + r + 1, :].astype(u.dtype), out)
    return out


def _shift_up(u, nxt, k):
    tm = u.shape[0]
    out = pltpu.roll(u, tm - k, 0)
    row = _rows(u.shape)
    for r in range(k):
        out = jnp.where(row == tm - k + r, nxt[r:r + 1, :], out)
    return out


def _conv3(u, prev, w):
    u1 = _shift_down(u, prev, 1)
    u2 = _shift_down(u, prev, 2)
    return w[0:1, :] * u2 + w[1:2, :] * u1 + w[2:3, :] * u, u1, u2


def _conv3_bwd(d, nxt, w):
    return w[2:3, :] * d + w[1:2, :] * _shift_up(d, nxt, 1) + w[0:1, :] * _shift_up(d, nxt, 2)


def _rms(x):
    return lax.rsqrt(jnp.mean(x * x, axis=-1, keepdims=True) + EPS)


def _rms_bwd(x, r, dyg):
    return r * dyg - x * (r * r * r) * jnp.mean(dyg * x, axis=-1, keepdims=True)


def _full(shape):
    nd = len(shape)
    return pl.BlockSpec(shape, lambda i, _n=nd: (0,) * _n)


ONES_LANE = 24


def _bias_scatter():
    sc = np.zeros((LANES, 2 * N_HEADS * LANES), np.float32)
    koff = N_HEADS * LANES
    for h in range(N_HEADS):
        aux = HEAD_DIM * (1 - h % 2)
        for j in range(3):
            sc[8 * j + h, LANES * h + aux + j] = 1.0
            sc[ONES_LANE, koff + LANES * h + aux + j] = 1.0
            sc[ONES_LANE, LANES * h + aux + 3 + j] = 1.0
            sc[8 * j + h, koff + LANES * h + aux + 3 + j] = -1.0
    return jnp.asarray(sc, BF)


def _split_parts(v, one):
    hi = v.astype(BF).astype(F32)
    rest = v - hi
    mid = rest.astype(BF).astype(F32)
    lo = (rest - mid).astype(BF).astype(F32)
    parts = hi + pltpu.roll(mid, 8, 1) + pltpu.roll(lo, 16, 1)
    return jnp.where(_cols(parts.shape) == ONES_LANE, one, parts).astype(BF)


def _fwd_in(x, g_mix, w_zc, w_qkv, w_f, b_f, w_conv, g_conv_out, tm):
    s = x.shape[0]
    nb = s // tm
    aw = N_HEADS * LANES

    def body(x_ref, gm_ref, wzc_ref, wqkv_ref, wf_ref, bf_ref, wc_ref, gco_ref, sc_ref,
             h1_ref, zc_ref, qkv_ref, fpre_ref, fcol_ref, nc_ref, qa_ref, ka_ref, qt_ref, vt_ref, cu_ref, cf_ref):
        i = pl.program_id(0)

        @pl.when(i == 0)
        def _():
            cu_ref[...] = jnp.zeros_like(cu_ref)
            cf_ref[...] = jnp.zeros_like(cf_ref)

        xv = x_ref[...]
        hb = (xv * _rms(xv) * gm_ref[...]).astype(BF)
        h1_ref[...] = hb
        zc = _dot(hb, wzc_ref[...])
        zc_ref[...] = zc
        qkv = _dot(hb, wqkv_ref[...])
        qkv = jnp.where(_cols(qkv.shape) < ATTN_W, qkv * 0.125, qkv)
        qkvb = qkv.astype(BF)
        qkv_ref[...] = qkvb
        qt_ref[0] = qkv[:, :ATTN_W].T.astype(BF)
        vt_ref[0] = qkv[:, 2 * ATTN_W:].T.astype(BF)

        gb, gc, xc = zc[:, :CONV_CH], zc[:, CONV_CH:2 * CONV_CH], zc[:, 2 * CONV_CH:]
        u = gc * xc
        cv, _, _ = _conv3(u, cu_ref[...], wc_ref[...])
        cu_ref[...] = u[tm - 8:, :]
        y = gb * cv
        nc_ref[...] = (y * _rms(y) * gco_ref[...]).astype(BF)

        fpre = _dot(hb, wf_ref[...]) + bf_ref[...]
        fpre_ref[...] = fpre
        logf = jnp.minimum(fpre, 0.0) - jnp.log1p(jnp.exp(-jnp.abs(fpre)))
        logf = jnp.where(_cols(logf.shape) < N_HEADS, logf, 0.0)
        tri = (_rows((LANES, LANES)) >= _cols((LANES, LANES))).astype(F32)
        offset = cf_ref[...]
        pieces = []
        for r in range(tm // LANES):
            piece = jnp.dot(tri, logf[r * LANES:(r + 1) * LANES, :], precision=HI, preferred_element_type=F32) + offset
            offset = piece[LANES - 1:LANES, :]
            pieces.append(piece)
        fcol = jnp.concatenate(pieces, axis=0)
        cf_ref[...] = offset
        fcol_ref[...] = fcol

        feat = _dot(_split_parts(fcol, 1.0), sc_ref[...]).astype(BF)
        lane = _cols((tm, LANES))
        for h in range(N_HEADS):
            hp, hh = divmod(h, 2)
            own = (lane >= HEAD_DIM * hh) & (lane < HEAD_DIM * (hh + 1))
            hs = slice(LANES * h, LANES * (h + 1))
            qa_ref[:, hs] = jnp.where(own, qkvb[:, LANES * hp:LANES * (hp + 1)], feat[:, hs])
            ka_ref[:, hs] = jnp.where(own, qkvb[:, ATTN_W + LANES * hp:ATTN_W + LANES * (hp + 1)],
                                      feat[:, aw + LANES * h:aw + LANES * (h + 1)])

    blk = lambda c: pl.BlockSpec((tm, c), lambda i: (i, 0))
    return pl.pallas_call(
        body, name="fwd_in", grid=(nb,),
        in_specs=[blk(D_MODEL), _full((1, D_MODEL)), _full(w_zc.shape), _full(w_qkv.shape), _full(w_f.shape),
                  _full((1, LANES)), _full((8, CONV_CH)), _full((1, CONV_CH)), _full((LANES, 2 * aw))],
        out_specs=[blk(D_MODEL), blk(3 * CONV_CH), blk(3 * ATTN_W), blk(LANES), blk(LANES),
                   blk(CONV_CH), blk(aw), blk(aw)] + [pl.BlockSpec((1, ATTN_W, tm), lambda i: (i, 0, 0))] * 2,
        out_shape=[jax.ShapeDtypeStruct((s, D_MODEL), BF), jax.ShapeDtypeStruct((s, 3 * CONV_CH), F32),
                   jax.ShapeDtypeStruct((s, 3 * ATTN_W), BF), jax.ShapeDtypeStruct((s, LANES), F32),
                   jax.ShapeDtypeStruct((s, LANES), F32),
                   jax.ShapeDtypeStruct((s, CONV_CH), BF), jax.ShapeDtypeStruct((s, aw), BF),
                   jax.ShapeDtypeStruct((s, aw), BF)] + [jax.ShapeDtypeStruct((nb, ATTN_W, tm), BF)] * 2,
        scratch_shapes=[pltpu.VMEM((8, CONV_CH), F32), pltpu.VMEM((1, LANES), F32)],
        compiler_params=_params(VMEM_MID),
    )(x, g_mix, w_zc, w_qkv, w_f, b_f, w_conv, g_conv_out, _bias_scatter())


def _pipeline_masked_last(last, produce, consume, buf_a, buf_b):
    produce(0, buf_a)

    def two_blocks(j, carry):
        blk = 2 * j
        produce(blk + 1, buf_b)
        consume(blk, buf_a, False)
        produce(blk + 2, buf_a)
        consume(blk + 1, buf_b, False)
        return carry

    lax.fori_loop(0, last // 2, two_blocks, 0)

    @pl.when(last % 2 == 0)
    def _():
        consume(last, buf_a, True)

    @pl.when(last % 2 == 1)
    def _():
        produce(last, buf_b)
        consume(last - 1, buf_a, False)
        consume(last, buf_b, True)


def _attn_fwd(qa, ka, vt, t, shards):
    s = qa.shape[0]
    nb = s // t
    n = len(shards)

    def body(qa_ref, ka_any, vt_any, *refs):
        shard_refs, (o_ref, l_ref), gathered = refs[:n], refs[n:n + 2], refs[n + 2:2 * n + 2]
        ka_scr, vt_scr, m_scr, acc_scr, ea_scr, eb_scr, sem = refs[2 * n + 2:2 * n + 9]
        i = pl.program_id(0)
        start, forward, finish = _gather_phases(shard_refs, gathered, *refs[2 * n + 9:])
        pl.when(i == 0)(start)
        pl.when(i == nb // 2)(forward)
        _load_weights(i, [(ka_any, ka_scr), (vt_any, vt_scr)], sem)

        causal_t = _rows((t, t)) <= _cols((t, t))
        row = _rows((LANES, t))
        lrows = jnp.zeros((LANES, t), F32)
        for hp in range(N_HEADS // 2):
            ps = slice(hp * LANES, (hp + 1) * LANES)
            m_scr[...] = jnp.full(m_scr.shape, NEG, F32)
            acc_scr[...] = jnp.zeros(acc_scr.shape, F32)
            heads = [dict(hh=hh, hs=slice((2 * hp + hh) * LANES, (2 * hp + hh + 1) * LANES),
                          aux=HEAD_DIM * (1 - hh), own=(row >= HEAD_DIM * hh) & (row < HEAD_DIM * (hh + 1)),
                          fill=jnp.where(row == HEAD_DIM * (1 - hh), 1.0, 0.0).astype(BF),
                          qh=qa_ref[:, (2 * hp + hh) * LANES:(2 * hp + hh + 1) * LANES]) for hh in range(2)]

            def scores(kb, dst, heads=heads):
                rs = pl.ds(pl.multiple_of(kb * t, t), t)
                for hd in heads:
                    dst[hd["hh"]] = _dot_nt(ka_scr[rs, hd["hs"]], hd["qh"])

            def consume(kb, src, masked, ps=ps, heads=heads):
                vt2 = vt_scr[kb, ps, :]
                m_olds = [m_scr[hd["hh"]:hd["hh"] + 1, :] for hd in heads]
                accs = [acc_scr[hd["hh"]] for hd in heads]
                m_news, acc_news = [], []
                for hd, m_old, acc in zip(heads, m_olds, accs):
                    e = src[hd["hh"]]
                    if masked:
                        e = jnp.where(causal_t, e, NEG)
                    m_new = jnp.maximum(m_old, jnp.max(e, axis=0, keepdims=True))
                    p = jnp.exp(e - m_new).astype(BF)
                    vta = jnp.where(hd["own"], vt2, hd["fill"])
                    acc_news.append(jnp.exp(m_old - m_new) * acc + _dot(vta, p))
                    m_news.append(m_new)
                for hd, m_new, acc in zip(heads, m_news, acc_news):
                    acc_scr[hd["hh"]] = acc
                    m_scr[hd["hh"]:hd["hh"] + 1, :] = m_new

            _pipeline_masked_last(i, scores, consume, ea_scr, eb_scr)

            o_pair = None
            for hd in heads:
                hh = hd["hh"]
                acc = acc_scr[hh]
                denom = acc[hd["aux"]:hd["aux"] + 1, :]
                o_h = acc / denom
                lrows = jnp.where(row == 2 * hp + hh, m_scr[hh:hh + 1, :] + jnp.log(denom), lrows)
                o_pair = o_h if hh == 0 else jnp.where(row < HEAD_DIM, o_pair, o_h)
            o_ref[:, ps] = o_pair.T
        l_ref[...] = lrows.T
        pl.when(i == nb - 1)(finish)

    any_ = pl.BlockSpec(memory_space=pl.ANY)
    return pl.pallas_call(
        body, name="attn_fwd", grid=(nb,),
        in_specs=[pl.BlockSpec((t, N_HEADS * LANES), lambda i: (i, 0)), any_, any_] + [any_] * n,
        out_specs=[pl.BlockSpec((t, ATTN_W), lambda i: (i, 0)), pl.BlockSpec((t, LANES), lambda i: (i, 0))]
        + [any_] * n,
        out_shape=[jax.ShapeDtypeStruct((s, ATTN_W), F32), jax.ShapeDtypeStruct((s, LANES), F32)]
        + _gathered_shapes(shards),
        scratch_shapes=[pltpu.VMEM(ka.shape, BF), pltpu.VMEM(vt.shape, BF), pltpu.VMEM((2, t), F32),
                        pltpu.VMEM((2, LANES, t), F32), pltpu.VMEM((2, t, t), F32), pltpu.VMEM((2, t, t), F32),
                        pltpu.SemaphoreType.DMA((2,))] + _exchange_sems(n),
        compiler_params=_params(VMEM_BIG),
    )(qa, ka, vt, *shards)


def _load_weights(i, pairs, sem):
    @pl.when(i == 0)
    def _():
        cps = [pltpu.make_async_copy(src, dst, sem.at[n]) for n, (src, dst) in enumerate(pairs)]
        for cp in cps:
            cp.start()
        for cp in cps:
            cp.wait()


def _ff_cols(j):
    return (slice(j * FF_CHUNK, (j + 1) * FF_CHUNK), slice(D_FF + j * FF_CHUNK, D_FF + (j + 1) * FF_CHUNK))


def _fwd_ffn(x, nc, o, tgt, g_attn_out, w_o, g_ffn, w_up, w_ffn_conv, w_dn, g_final, tm):
    s = x.shape[0]

    def body(x_ref, nc_ref, o_ref, tgt_ref, ga_ref, gf_ref, wfc_ref, gfin_ref, wo_any, wup_any, wdn_any,
             x2_ref, h2_ref, mix_ref, u0_ref, uc_ref, act_ref, dx3_ref, dx3b_ref, loss_ref, dgfin_ref,
             wo, wup, wdn, carry, sem):
        i = pl.program_id(0)
        _load_weights(i, [(wo_any, wo), (wup_any, wup), (wdn_any, wdn)], sem)

        @pl.when(i == 0)
        def _():
            carry[...] = jnp.zeros_like(carry)
            loss_ref[...] = jnp.zeros_like(loss_ref)
            dgfin_ref[...] = jnp.zeros_like(dgfin_ref)

        ov = o_ref[...]
        na = (ov * _rms(ov) * ga_ref[...]).astype(BF)
        ncv = nc_ref[...]
        mix_ref[:, :CONV_CH] = ncv
        mix_ref[:, CONV_CH:] = na
        x2 = x_ref[...] + _dot(ncv, wo[:CONV_CH, :]) + _dot(na, wo[CONV_CH:, :])
        x2_ref[...] = x2
        hb = (x2 * _rms(x2) * gf_ref[...]).astype(BF)
        h2_ref[...] = hb

        def up(j):
            return [_dot(hb, wup[:, cc]) for cc in _ff_cols(j)]

        nxt = up(0)
        down = None
        for j in range(N_FF_CHUNKS):
            cur = nxt
            if j + 1 < N_FF_CHUNKS:
                nxt = up(j + 1)
            parts = []
            for cc, u0 in zip(_ff_cols(j), cur):
                u0_ref[:, cc] = u0.astype(BF)
                uu, _, _ = _conv3(u0, carry[:, cc], wfc_ref[:, cc])
                carry[:, cc] = u0[tm - 8:, :]
                uc_ref[:, cc] = uu.astype(BF)
                parts.append(uu)
            ua, ug = parts
            act = (ug * jax.nn.sigmoid(ug) * ua).astype(BF)
            ca = _ff_cols(j)[0]
            act_ref[:, ca] = act
            part = _dot(act, wdn[ca, :])
            down = part if down is None else down + part

        x3 = x2_ref[...] + down
        r3 = _rms(x3)
        gfin = gfin_ref[...]
        xn = x3 * r3
        diff = xn * gfin - tgt_ref[...]
        loss_ref[...] += jnp.sum(jnp.sum(diff * diff, axis=-1, keepdims=True), axis=0, keepdims=True) * (0.5 / D_MODEL)
        dy = diff * (1.0 / D_MODEL)
        dgfin_ref[...] += jnp.sum(dy * xn, axis=0, keepdims=True)
        dx3 = _rms_bwd(x3, r3, dy * gfin)
        dx3_ref[...] = dx3
        dx3b_ref[...] = dx3.astype(BF)

    blk = lambda c: pl.BlockSpec((tm, c), lambda i: (i, 0))
    any_ = pl.BlockSpec(memory_space=pl.ANY)
    return pl.pallas_call(
        body, name="fwd_ffn", grid=(s // tm,),
        in_specs=[blk(D_MODEL), blk(CONV_CH), blk(ATTN_W), blk(D_MODEL), _full((1, ATTN_W)), _full((1, D_MODEL)),
                  _full((8, 2 * D_FF)), _full((1, D_MODEL)), any_, any_, any_],
        out_specs=[blk(D_MODEL), blk(D_MODEL), blk(D_MODEL), blk(2 * D_FF), blk(2 * D_FF), blk(D_FF), blk(D_MODEL),
                   blk(D_MODEL), _full((1, 1)), _full((1, D_MODEL))],
        out_shape=[jax.ShapeDtypeStruct((s, D_MODEL), F32), jax.ShapeDtypeStruct((s, D_MODEL), BF),
                   jax.ShapeDtypeStruct((s, D_MODEL), BF),
                   jax.ShapeDtypeStruct((s, 2 * D_FF), BF), jax.ShapeDtypeStruct((s, 2 * D_FF), BF),
                   jax.ShapeDtypeStruct((s, D_FF), BF),
                   jax.ShapeDtypeStruct((s, D_MODEL), F32), jax.ShapeDtypeStruct((s, D_MODEL), BF),
                   jax.ShapeDtypeStruct((1, 1), F32), jax.ShapeDtypeStruct((1, D_MODEL), F32)],
        scratch_shapes=[pltpu.VMEM(w_o.shape, BF), pltpu.VMEM(w_up.shape, BF), pltpu.VMEM(w_dn.shape, BF),
                        pltpu.VMEM((8, 2 * D_FF), F32), pltpu.SemaphoreType.DMA((3,))],
        compiler_params=_params(VMEM_BIG),
    )(x, nc, o, tgt, g_attn_out, g_ffn, w_ffn_conv, g_final, w_o, w_up, w_dn)


def _bwd_ffn(dx3, dx3b, u0, uc, x2, w_up, w_ffn_conv, w_dn, g_ffn, tm):
    s = x2.shape[0]
    nb = s // tm

    def body(dx3_ref, dx3b_ref, u0_ref, uc_ref, x2_ref, wfc_ref, gf_ref, wup_any, wdn_any,
             du0_ref, dx2_ref, dx2b_ref, dwfc_ref, dgf_ref,
             wup, wdn, carry, sem):
        i = pl.program_id(0)
        _load_weights(i, [(wup_any, wup), (wdn_any, wdn)], sem)

        @pl.when(i == 0)
        def _():
            carry[...] = jnp.zeros_like(carry)
            dwfc_ref[...] = jnp.zeros_like(dwfc_ref)
            dgf_ref[...] = jnp.zeros_like(dgf_ref)

        db = dx3b_ref[...]

        def dact_of(j):
            return _dot(db, wdn[:, _ff_cols(j)[0]])

        nxt = dact_of(0)
        dh2 = None
        for j in range(N_FF_CHUNKS):
            ca, cg = _ff_cols(j)
            dact = nxt
            if j + 1 < N_FF_CHUNKS:
                nxt = dact_of(j + 1)
            ua = uc_ref[:, ca].astype(F32)
            ug = uc_ref[:, cg].astype(F32)
            sg = jax.nn.sigmoid(ug)
            da = dact * (ug * sg)
            dg = dact * ua * (sg * (1.0 + ug * (1.0 - sg)))
            for cc, d in ((ca, da), (cg, dg)):
                nxt_rows = carry[:, cc]
                d1 = _shift_up(d, nxt_rows, 1)
                d2 = _shift_up(d, nxt_rows, 2)
                u0c = u0_ref[:, cc].astype(F32)
                w = wfc_ref[:, cc]
                dwfc_ref[0:1, cc] += jnp.sum(d2 * u0c, axis=0, keepdims=True)
                dwfc_ref[1:2, cc] += jnp.sum(d1 * u0c, axis=0, keepdims=True)
                dwfc_ref[2:3, cc] += jnp.sum(d * u0c, axis=0, keepdims=True)
                du0 = (w[2:3, :] * d + w[1:2, :] * d1 + w[0:1, :] * d2).astype(BF)
                carry[:, cc] = d[:8, :]
                du0_ref[:, cc] = du0
                part = _dot(du0, wup[cc, :])
                dh2 = part if dh2 is None else dh2 + part

        x2v = x2_ref[...]
        r2 = _rms(x2v)
        dgf_ref[...] += jnp.sum(dh2 * (x2v * r2), axis=0, keepdims=True)
        dx2 = dx3_ref[...] + _rms_bwd(x2v, r2, dh2 * gf_ref[...])
        dx2_ref[...] = dx2
        dx2b_ref[...] = dx2.astype(BF)

    blk = lambda c: pl.BlockSpec((tm, c), lambda i: (nb - 1 - i, 0))
    any_ = pl.BlockSpec(memory_space=pl.ANY)
    return pl.pallas_call(
        body, name="bwd_ffn", grid=(nb,),
        in_specs=[blk(D_MODEL), blk(D_MODEL), blk(2 * D_FF), blk(2 * D_FF), blk(D_MODEL), _full((8, 2 * D_FF)),
                  _full((1, D_MODEL)), any_, any_],
        out_specs=[blk(2 * D_FF), blk(D_MODEL), blk(D_MODEL), _full((8, 2 * D_FF)), _full((1, D_MODEL))],
        out_shape=[jax.ShapeDtypeStruct((s, 2 * D_FF), BF), jax.ShapeDtypeStruct((s, D_MODEL), F32),
                   jax.ShapeDtypeStruct((s, D_MODEL), BF), jax.ShapeDtypeStruct((8, 2 * D_FF), F32),
                   jax.ShapeDtypeStruct((1, D_MODEL), F32)],
        scratch_shapes=[pltpu.VMEM(w_up.shape, BF), pltpu.VMEM(w_dn.shape, BF), pltpu.VMEM((8, 2 * D_FF), F32),
                        pltpu.SemaphoreType.DMA((2,))],
        compiler_params=_params(VMEM_BIG),
    )(dx3, dx3b, u0, uc, x2, w_ffn_conv, g_ffn, w_up, w_dn)


def _bwd_out(dx2b, o, zc, fcol, lcol, qa, w_o, g_attn_out, g_conv_out, w_conv, tm):
    s = o.shape[0]
    nb = s // tm
    aw = N_HEADS * LANES

    def body(dx2b_ref, o_ref, zc_ref, halo_ref, fcol_ref, lcol_ref, qa_ref, wo_ref, ga_ref, gco_ref, wc_ref, sc_ref,
             dzc_ref, doa_ref, dot_ref, qab_ref, dwc_ref, dga_ref, dgco_ref, carry):
        i = pl.program_id(0)
        rb = nb - 1 - i

        @pl.when(i == 0)
        def _():
            carry[...] = jnp.zeros_like(carry)
            dwc_ref[...] = jnp.zeros_like(dwc_ref)
            dga_ref[...] = jnp.zeros_like(dga_ref)
            dgco_ref[...] = jnp.zeros_like(dgco_ref)

        dmix = _dot_nt(dx2b_ref[...], wo_ref[...])
        dnc, dna = dmix[:, :CONV_CH], dmix[:, CONV_CH:]

        ov = o_ref[...]
        ra = _rms(ov)
        dga_ref[...] += jnp.sum(dna * (ov * ra), axis=0, keepdims=True)
        do = _rms_bwd(ov, ra, dna * ga_ref[...])
        dob = do.astype(BF)
        dot_ref[0] = do.T.astype(BF)
        sel = (_rows((ATTN_W, LANES)) // HEAD_DIM == _cols((ATTN_W, LANES))).astype(F32)
        delta = jnp.dot(do * ov, sel, precision=lax.Precision.HIGH, preferred_element_type=F32)
        featc = _dot(_split_parts(fcol_ref[...] - lcol_ref[...], 1.0), sc_ref[...]).astype(BF)
        featd = _dot(_split_parts(-delta, 0.0), sc_ref[...]).astype(BF)
        lane = _cols((tm, LANES))
        for h in range(N_HEADS):
            hp, hh = divmod(h, 2)
            own = (lane >= HEAD_DIM * hh) & (lane < HEAD_DIM * (hh + 1))
            hs = slice(LANES * h, LANES * (h + 1))
            qab_ref[:, hs] = jnp.where(own, qa_ref[:, hs], featc[:, hs])
            doa_ref[:, hs] = jnp.where(own, dob[:, LANES * hp:LANES * (hp + 1)], featd[:, hs])

        zc_v = zc_ref[...]
        gb, gc, xc = zc_v[:, :CONV_CH], zc_v[:, CONV_CH:2 * CONV_CH], zc_v[:, 2 * CONV_CH:]
        hal = halo_ref[...] * (rb > 0).astype(F32)
        u = gc * xc
        prev = hal[:, CONV_CH:2 * CONV_CH] * hal[:, 2 * CONV_CH:]
        wc = wc_ref[...]
        cv, u1, u2 = _conv3(u, prev, wc)
        y = gb * cv
        rc = _rms(y)
        dgco_ref[...] += jnp.sum(dnc * (y * rc), axis=0, keepdims=True)
        dy = _rms_bwd(y, rc, dnc * gco_ref[...])
        dcv = dy * gb
        dwc_ref[0:1, :] += jnp.sum(dcv * u2, axis=0, keepdims=True)
        dwc_ref[1:2, :] += jnp.sum(dcv * u1, axis=0, keepdims=True)
        dwc_ref[2:3, :] += jnp.sum(dcv * u, axis=0, keepdims=True)
        du = _conv3_bwd(dcv, carry[...], wc)
        carry[...] = dcv[:8, :]
        dzc_ref[:, :CONV_CH] = (dy * cv).astype(BF)
        dzc_ref[:, CONV_CH:2 * CONV_CH] = (du * xc).astype(BF)
        dzc_ref[:, 2 * CONV_CH:] = (du * gc).astype(BF)

    blk = lambda c: pl.BlockSpec((tm, c), lambda i: (nb - 1 - i, 0))
    halo = pl.BlockSpec((8, 3 * CONV_CH), lambda i: (jnp.maximum((nb - 1 - i) * (tm // 8) - 1, 0), 0))
    tr = pl.BlockSpec((1, ATTN_W, tm), lambda i: (nb - 1 - i, 0, 0))
    return pl.pallas_call(
        body, name="bwd_out", grid=(nb,),
        in_specs=[blk(D_MODEL), blk(ATTN_W), blk(3 * CONV_CH), halo, blk(LANES), blk(LANES), blk(aw),
                  _full(w_o.shape), _full((1, ATTN_W)), _full((1, CONV_CH)), _full((8, CONV_CH)), _full((LANES, aw))],
        out_specs=[blk(3 * CONV_CH), blk(aw), tr, blk(aw),
                   _full((8, CONV_CH)), _full((1, ATTN_W)), _full((1, CONV_CH))],
        out_shape=[jax.ShapeDtypeStruct((s, 3 * CONV_CH), BF),
                   jax.ShapeDtypeStruct((s, aw), BF), jax.ShapeDtypeStruct((nb, ATTN_W, tm), BF),
                   jax.ShapeDtypeStruct((s, aw), BF),
                   jax.ShapeDtypeStruct((8, CONV_CH), F32), jax.ShapeDtypeStruct((1, ATTN_W), F32),
                   jax.ShapeDtypeStruct((1, CONV_CH), F32)],
        scratch_shapes=[pltpu.VMEM((8, CONV_CH), F32)],
        compiler_params=_params(VMEM_MID),
    )(dx2b, o, zc, zc, fcol, lcol, qa, w_o, g_attn_out, g_conv_out, w_conv, _bias_scatter()[:, :aw])


def _attn_bwd(qab, doa, ka, qkv, qt, dot, t, blocks):
    s = qab.shape[0]
    nb = s // t
    npair = N_HEADS // 2
    n = len(blocks)

    def body(ka_ref, v_ref, qab_ref, doa_ref, qt_ref, dot_ref, *refs):
        block_refs, (dk_ref, dv_ref, dfk_ref, dqa_ref, dq_ref) = refs[:n], refs[n:n + 5]
        received = refs[n + 5:2 * n + 5]
        acck, accv = refs[2 * n + 5:2 * n + 7]
        hp = pl.program_id(0)
        j = pl.program_id(1)
        start, finish = _scatter_phases(block_refs, received, *refs[2 * n + 7:])
        pl.when((hp == 0) & (j == 0))(start)

        @pl.when(j == 0)
        def _():
            dqa_ref[...] = jnp.zeros_like(dqa_ref)

        causal = _cols((t, t)) <= _rows((t, t))
        row = _rows((LANES, t))
        lane = _cols((t, LANES))
        acck[...] = jnp.zeros(acck.shape, F32)
        accv[...] = jnp.zeros(accv.shape, F32)
        v2 = v_ref[...]
        heads = []
        for hh in range(2):
            aux = HEAD_DIM * (1 - hh)
            ownl = (lane >= HEAD_DIM * hh) & (lane < HEAD_DIM * (hh + 1))
            ones3 = jnp.where((lane >= aux) & (lane < aux + 3), 1.0, 0.0).astype(BF)
            heads.append(dict(hh=hh, hs=slice(hh * LANES, (hh + 1) * LANES), aux=aux,
                              own=(row >= HEAD_DIM * hh) & (row < HEAD_DIM * (hh + 1)),
                              fill=jnp.where(row == aux, 1.0, 0.0).astype(BF),
                              kh=ka_ref[:, hh * LANES:(hh + 1) * LANES], vh=jnp.where(ownl, v2, ones3)))

        def tile(qb, masked):
            rs = pl.ds(pl.multiple_of(qb * t, t), t)
            qt2 = qt_ref[qb]
            dot2 = dot_ref[qb]
            es = [_dot_nt(qab_ref[rs, hd["hs"]], hd["kh"]) for hd in heads]
            dps = [_dot_nt(doa_ref[rs, hd["hs"]], hd["vh"]) for hd in heads]
            aks = [acck[hd["hh"]] for hd in heads]
            avs = [accv[hd["hh"]] for hd in heads]
            dqs = [dqa_ref[rs, hd["hs"]] for hd in heads]
            outs = []
            for hd, e, dp, ak, av, dq in zip(heads, es, dps, aks, avs, dqs):
                if masked:
                    e = jnp.where(causal, e, NEG)
                p = jnp.exp(e)
                ds = (p * dp).astype(BF)
                outs.append((ak + _dot(jnp.where(hd["own"], qt2, hd["fill"]), ds), av + _dot(dot2, p.astype(BF)),
                             dq + _dot(ds, hd["kh"])))
            for hd, (ak, av, dq) in zip(heads, outs):
                acck[hd["hh"]] = ak
                accv[hd["hh"]] = av
                dqa_ref[rs, hd["hs"]] = dq

        def loop_body(qb, carry):
            tile(qb, False)
            return carry

        tile(j, True)
        lax.fori_loop(j + 1, nb, loop_body, 0)
        dk_pair = dv_pair = None
        dfrows = jnp.zeros((LANES, t), F32)
        for hd in heads:
            ak, av = acck[hd["hh"]], accv[hd["hh"]]
            dfrows = jnp.where(row == 2 * hp + hd["hh"], -ak[hd["aux"]:hd["aux"] + 1, :], dfrows)
            dk_pair = ak if hd["hh"] == 0 else jnp.where(row < HEAD_DIM, dk_pair, ak)
            dv_pair = av if hd["hh"] == 0 else jnp.where(row < HEAD_DIM, dv_pair, av)
        dk_ref[...] = dk_pair.T.astype(BF)
        dv_ref[...] = dv_pair.T.astype(BF)
        dfk_ref[0] = dfrows.T

        @pl.when(j == nb - 1)
        def _():
            for r in range(nb):
                rows = slice(r * t, (r + 1) * t)
                dq_ref[rows, :] = (jnp.where(lane < HEAD_DIM, dqa_ref[rows, :LANES], dqa_ref[rows, LANES:])
                                   * 0.125).astype(BF)

        pl.when((hp == npair - 1) & (j == nb - 1))(finish)

    pair_cols = pl.BlockSpec((s, 2 * LANES), lambda hp, j: (0, hp))
    pair_rows = pl.BlockSpec((nb, LANES, t), lambda hp, j: (0, hp, 0))
    any_ = pl.BlockSpec(memory_space=pl.ANY)
    return pl.pallas_call(
        body, name="attn_bwd", grid=(npair, nb),
        in_specs=[pl.BlockSpec((t, 2 * LANES), lambda hp, j: (j, hp)),
                  pl.BlockSpec((t, LANES), lambda hp, j: (j, 2 * npair + hp)),
                  pair_cols, pair_cols, pair_rows, pair_rows] + [any_] * n,
        out_specs=[pl.BlockSpec((t, LANES), lambda hp, j: (j, hp)), pl.BlockSpec((t, LANES), lambda hp, j: (j, hp)),
                   pl.BlockSpec((1, t, LANES), lambda hp, j: (hp, j, 0)), pair_cols,
                   pl.BlockSpec((s, LANES), lambda hp, j: (0, hp))] + [any_] * n,
        out_shape=[jax.ShapeDtypeStruct((s, ATTN_W), BF), jax.ShapeDtypeStruct((s, ATTN_W), BF),
                   jax.ShapeDtypeStruct((npair, s, LANES), F32), jax.ShapeDtypeStruct((s, N_HEADS * LANES), F32),
                   jax.ShapeDtypeStruct((s, ATTN_W), BF)]
        + [jax.ShapeDtypeStruct(b.shape, b.dtype) for b in blocks],
        scratch_shapes=[pltpu.VMEM((2, LANES, t), F32), pltpu.VMEM((2, LANES, t), F32)] + _exchange_sems(n),
        compiler_params=_params(VMEM_BIG),
    )(ka, qkv, qab, doa, qt, dot, *blocks)


def _bwd_in(x, dx2, dzc, dq, dqa, dk, dv, dfk, fpre, w_zc, w_qkv, w_f, g_mix, tm, blocks):
    s = x.shape[0]
    nb = s // tm
    n = len(blocks)

    def body(x_ref, dx2_ref, dzc_ref, dq_ref, dqa_ref, dk_ref, dv_ref, dfk_ref, fpre_ref, wzc_ref, wqkv_ref, wf_ref,
             gm_ref, *refs):
        block_refs, (gx_ref, dfb_ref, dgm_ref, dbf_ref), received = refs[:n], refs[n:n + 4], refs[n + 4:2 * n + 4]
        carry = refs[2 * n + 4]
        i = pl.program_id(0)
        start, finish = _scatter_phases(block_refs, received, *refs[2 * n + 5:])
        pl.when(i == 0)(start)

        @pl.when(i == 0)
        def _():
            carry[...] = jnp.zeros_like(carry)
            dgm_ref[...] = jnp.zeros_like(dgm_ref)
            dbf_ref[...] = jnp.zeros_like(dbf_ref)

        lane = _cols((tm, LANES))
        dfq = jnp.zeros((tm, LANES), F32)
        for h in range(N_HEADS):
            aux = LANES * h + HEAD_DIM * (1 - h % 2)
            dfq = jnp.where(lane == h, dqa_ref[:, aux:aux + 1], dfq)

        df_cum = dfq + ((dfk_ref[0] + dfk_ref[1]) + (dfk_ref[2] + dfk_ref[3]))
        triu = (_rows((LANES, LANES)) <= _cols((LANES, LANES))).astype(F32)
        offset = carry[...]
        pieces = []
        for r in reversed(range(tm // LANES)):
            piece = jnp.dot(triu, df_cum[r * LANES:(r + 1) * LANES, :], precision=HI,
                            preferred_element_type=F32) + offset
            offset = piece[0:1, :]
            pieces.append(piece)
        dlogf = jnp.concatenate(pieces[::-1], axis=0)
        carry[...] = offset
        fpre = fpre_ref[...]
        df = jnp.where(_cols(fpre.shape) < N_HEADS, dlogf / (1.0 + jnp.exp(fpre)), 0.0)
        dbf_ref[...] += jnp.sum(df, axis=0, keepdims=True)
        dfb = df.astype(BF)
        dfb_ref[...] = dfb

        dh1 = _dot_nt(dzc_ref[...], wzc_ref[...])
        dh1 += _dot_nt(dq_ref[...], wqkv_ref[:, :ATTN_W])
        dh1 += _dot_nt(dk_ref[...], wqkv_ref[:, ATTN_W:2 * ATTN_W])
        dh1 += _dot_nt(dv_ref[...], wqkv_ref[:, 2 * ATTN_W:])
        dh1 += _dot_nt(dfb, wf_ref[...])
        xv = x_ref[...]
        r1 = _rms(xv)
        dgm_ref[...] += jnp.sum(dh1 * (xv * r1), axis=0, keepdims=True)
        gx_ref[...] = dx2_ref[...] + _rms_bwd(xv, r1, dh1 * gm_ref[...])
        pl.when(i == nb - 1)(finish)

    blk = lambda c: pl.BlockSpec((tm, c), lambda i: (nb - 1 - i, 0))
    any_ = pl.BlockSpec(memory_space=pl.ANY)
    return pl.pallas_call(
        body, name="bwd_in", grid=(nb,),
        in_specs=[blk(D_MODEL), blk(D_MODEL), blk(3 * CONV_CH), blk(ATTN_W), blk(N_HEADS * LANES), blk(ATTN_W),
                  blk(ATTN_W), pl.BlockSpec((N_HEADS // 2, tm, LANES), lambda i: (0, nb - 1 - i, 0)), blk(LANES),
                  _full(w_zc.shape), _full(w_qkv.shape), _full(w_f.shape), _full((1, D_MODEL))] + [any_] * n,
        out_specs=[blk(D_MODEL), blk(LANES), _full((1, D_MODEL)), _full((1, LANES))] + [any_] * n,
        out_shape=[jax.ShapeDtypeStruct((s, D_MODEL), F32), jax.ShapeDtypeStruct((s, LANES), BF),
                   jax.ShapeDtypeStruct((1, D_MODEL), F32), jax.ShapeDtypeStruct((1, LANES), F32)]
        + [jax.ShapeDtypeStruct(b.shape, b.dtype) for b in blocks],
        scratch_shapes=[pltpu.VMEM((1, LANES), F32)] + _exchange_sems(n),
        compiler_params=_params(VMEM_MID),
    )(x, dx2, dzc, dq, dqa, dk, dv, dfk, fpre, w_zc, w_qkv, w_f, g_mix, *blocks)


def _matmul_tn(a, b, bn, bk, name):
    s, m = a.shape
    n = b.shape[1]

    def body(a_ref, b_ref, o_ref):
        @pl.when(pl.program_id(1) == 0)
        def _():
            o_ref[...] = jnp.zeros_like(o_ref)

        o_ref[...] += _dot_tn(a_ref[...], b_ref[...])

    return pl.pallas_call(
        body, name=name, grid=(n // bn, s // bk),
        in_specs=[pl.BlockSpec((bk, m), lambda jn, k: (k, 0)), pl.BlockSpec((bk, bn), lambda jn, k: (k, jn))],
        out_specs=pl.BlockSpec((m, bn), lambda jn, k: (0, jn)),
        out_shape=jax.ShapeDtypeStruct((m, n), F32),
        compiler_params=_params(VMEM_MID),
    )(a, b)


def _matmul_tn_multi(a, bs, bk, name):
    s, m = a.shape
    nb = len(bs)

    def body(a_ref, *refs):
        b_refs, o_refs = refs[:nb], refs[nb:]

        @pl.when(pl.program_id(0) == 0)
        def _():
            for o_ref in o_refs:
                o_ref[...] = jnp.zeros_like(o_ref)

        at = a_ref[...].T
        for b_ref, o_ref in zip(b_refs, o_refs):
            o_ref[...] += _dot(at, b_ref[...])

    return pl.pallas_call(
        body, name=name, grid=(s // bk,),
        in_specs=[pl.BlockSpec((bk, m), lambda k: (k, 0))]
        + [pl.BlockSpec((bk, b.shape[1]), lambda k: (k, 0)) for b in bs],
        out_specs=[_full((m, b.shape[1])) for b in bs],
        out_shape=[jax.ShapeDtypeStruct((m, b.shape[1]), F32) for b in bs],
        compiler_params=_params(VMEM_BIG),
    )(a, *bs)


def _flip(v, bit):
    return 1 - v if bit else v


def _all_gather(shards):
    n = len(shards)

    def body(*refs):
        start, forward, finish = _gather_phases(refs[:n], refs[n:2 * n], *refs[2 * n:])
        start()
        forward()
        finish()

    any_ = pl.BlockSpec(memory_space=pl.ANY)
    return pl.pallas_call(
        body, name="all_gather_weights",
        in_specs=[any_] * n, out_specs=[any_] * n,
        out_shape=_gathered_shapes(shards), scratch_shapes=_exchange_sems(n),
    )(*shards)


def _gathered_shapes(shards):
    return [jax.ShapeDtypeStruct((8,) + sh.shape, sh.dtype) for sh in shards]


def _exchange_sems(n):
    return [pltpu.SemaphoreType.DMA((7 * n,)), pltpu.SemaphoreType.DMA((7 * n,)), pltpu.SemaphoreType.DMA((n,))]


def _gather_phases(src, out, send_sems, recv_sems, loc_sems):
    n = len(src)
    x, y, c = lax.axis_index("x"), lax.axis_index("y"), lax.axis_index("c")
    me, sibling = (x, y, c), (x, y, 1 - c)
    chips = [(1 - x, y), (x, 1 - y), (1 - x, 1 - y)]

    def slot(a, px, py, pc):
        return out[a].at[4 * px + 2 * py + pc]

    def copy(a, k, block, to, from_src=False):
        return pltpu.make_async_remote_copy(
            src_ref=src[a] if from_src else slot(a, *block), dst_ref=slot(a, *block),
            send_sem=send_sems.at[7 * a + k], recv_sem=recv_sems.at[7 * a + k],
            device_id=to, device_id_type=MESH)

    def local(a):
        return pltpu.make_async_copy(src[a], slot(a, *me), loc_sems.at[a])

    def first(a):
        return [copy(a, 0, me, sibling, True)] + [copy(a, 1 + j, me, (*chip, c), True)
                                                  for j, chip in enumerate(chips)]

    def passed(a, j):
        return copy(a, 4 + j, (*chips[j], c), sibling)

    def start():
        for a in range(n):
            local(a).start()
            for cp in first(a):
                cp.start()

    def forward():
        for a in range(n):
            for j, chip in enumerate(chips):
                copy(a, 1 + j, (*chip, c), me).wait_recv()
                passed(a, j).start()

    def finish():
        for a in range(n):
            copy(a, 0, sibling, me).wait_recv()
            for j, chip in enumerate(chips):
                copy(a, 4 + j, (*chip, 1 - c), me).wait_recv()
        for a in range(n):
            for cp in first(a) + [passed(a, j) for j in range(3)]:
                cp.wait_send()
            local(a).wait()

    return start, forward, finish


def _scatter_phases(src, out, send_sems, recv_sems, loc_sems):
    n = len(src)
    masks = [((k >> 2) & 1, (k >> 1) & 1, k & 1) for k in range(1, 8)]
    x, y, c = lax.axis_index("x"), lax.axis_index("y"), lax.axis_index("c")
    me = 4 * x + 2 * y + c

    def copies():
        cps = []
        for a in range(n):
            cps.append(pltpu.make_async_copy(src[a].at[me], out[a].at[me], loc_sems.at[a]))
            for k, (mx, my, mc) in enumerate(masks):
                px, py, pc = _flip(x, mx), _flip(y, my), _flip(c, mc)
                cps.append(pltpu.make_async_remote_copy(
                    src_ref=src[a].at[4 * px + 2 * py + pc], dst_ref=out[a].at[me],
                    send_sem=send_sems.at[7 * a + k], recv_sem=recv_sems.at[7 * a + k],
                    device_id=(px, py, pc), device_id_type=MESH))
        return cps

    def start():
        for cp in copies():
            cp.start()

    def finish():
        for cp in copies():
            cp.wait()

    return start, finish


def _all_to_all(blocks, name):
    n = len(blocks)

    def body(*refs):
        start, finish = _scatter_phases(refs[:n], refs[n:2 * n], *refs[2 * n:])
        start()
        finish()

    any_ = pl.BlockSpec(memory_space=pl.ANY)
    return pl.pallas_call(
        body, name=name,
        in_specs=[any_] * n, out_specs=[any_] * n,
        out_shape=[jax.ShapeDtypeStruct(b.shape, b.dtype) for b in blocks], scratch_shapes=_exchange_sems(n),
    )(*blocks)


def _adamw(parts, w, m, v, br, name, tail=None):
    g8, r, c = parts.shape
    c1 = 1.0 - ADAM_B1 ** ADAM_STEP
    c2 = 1.0 - ADAM_B2 ** ADAM_STEP
    extra = [] if tail is None else [tail]

    def body(p_ref, w_ref, m_ref, v_ref, *refs):
        g_ref, d_ref, m2_ref, v2_ref = refs[len(extra):]
        g = p_ref[0].astype(F32)
        for d in range(1, g8):
            g = g + p_ref[d].astype(F32)
        if tail is not None:
            k = tail.shape[1]
            place = (_rows((k, c)) + (c - k) == _cols((k, c))).astype(F32)
            g = g + jnp.dot(refs[0][...], place, precision=HI, preferred_element_type=F32)
        m2 = ADAM_B1 * m_ref[...] + (1.0 - ADAM_B1) * g
        v2 = ADAM_B2 * v_ref[...] + (1.0 - ADAM_B2) * (g * g)
        g_ref[...] = g
        m2_ref[...] = m2
        v2_ref[...] = v2
        d_ref[...] = -ADAM_LR * ((m2 / c1) / (jnp.sqrt(v2 / c2) + ADAM_EPS) + ADAM_WD * w_ref[...])

    blk = pl.BlockSpec((br, c), lambda i: (i, 0))
    out = jax.ShapeDtypeStruct((r, c), F32)
    return pl.pallas_call(
        body, name=name, grid=(r // br,),
        in_specs=[pl.BlockSpec((g8, br, c), lambda i: (0, i, 0)), blk, blk, blk]
        + [pl.BlockSpec((br, e.shape[1]), lambda i: (i, 0)) for e in extra],
        out_specs=[blk] * 4, out_shape=[out] * 4,
        compiler_params=_params(VMEM_MID),
    )(parts, w, m, v, *extra)


def _pad_rows(a, rows):
    return jnp.pad(a, ((0, rows - a.shape[0]), (0, 0)))


_SMALL = (("g_mix", 1024), ("b_f", 8), ("g_conv_out", 512), ("g_attn_out", 512), ("g_ffn", 1024), ("g_final", 1024))
_SMALL_LEN = sum(n for _, n in _SMALL)
_SMALL_ROWS = 104


def _pack_small(vals, extra=None):
    parts = [vals[k].reshape(-1) for k, _ in _SMALL]
    if extra is not None:
        parts.append(extra.reshape(-1))
    flat = jnp.concatenate(parts)
    return jnp.pad(flat, (0, _SMALL_ROWS * LANES - flat.shape[0])).reshape(_SMALL_ROWS, LANES)


def _unpack_small(packed, shapes):
    flat = packed.reshape(-1)
    out, off = {}, 0
    for k, n in _SMALL:
        out[k] = flat[off:off + n].reshape(shapes[k])
        off += n
    return out


def _col_blocks(a):
    r, c8 = a.shape
    return jnp.transpose(a.reshape(r, 8, c8 // 8), (1, 0, 2))


def _from_col_blocks(a):
    g, r, c = a.shape
    return jnp.transpose(a, (1, 0, 2)).reshape(r, g * c)


def kernel(x, g_mix, w_in, b_f, w_conv, g_conv_out, g_attn_out, w_o, g_ffn, w_up, w_ffn_conv, w_down, g_final, loss_target, m_g_mix, m_w_in, m_b_f, m_w_conv, m_g_conv_out, m_g_attn_out, m_w_o, m_g_ffn, m_w_up, m_w_ffn_conv, m_w_down, m_g_final, v_g_mix, v_w_in, v_b_f, v_w_conv, v_g_conv_out, v_g_attn_out, v_w_o, v_g_ffn, v_w_up, v_w_ffn_conv, v_w_down, v_g_final):
    w = dict(g_mix=g_mix, w_in=w_in[0], b_f=b_f, w_conv=w_conv[0], g_conv_out=g_conv_out, g_attn_out=g_attn_out,
             w_o=w_o[0], g_ffn=g_ffn, w_up=w_up[0], w_ffn_conv=w_ffn_conv[0], w_down=w_down[0], g_final=g_final)
    m = dict(g_mix=m_g_mix, w_in=m_w_in[0], b_f=m_b_f, w_conv=m_w_conv[0], g_conv_out=m_g_conv_out,
             g_attn_out=m_g_attn_out, w_o=m_w_o[0], g_ffn=m_g_ffn, w_up=m_w_up[0], w_ffn_conv=m_w_ffn_conv[0],
             w_down=m_w_down[0], g_final=m_g_final)
    v = dict(g_mix=v_g_mix, w_in=v_w_in[0], b_f=v_b_f, w_conv=v_w_conv[0], g_conv_out=v_g_conv_out,
             g_attn_out=v_g_attn_out, w_o=v_w_o[0], g_ffn=v_g_ffn, w_up=v_w_up[0], w_ffn_conv=v_w_ffn_conv[0],
             w_down=v_w_down[0], g_final=v_g_final)
    shapes = dict(g_mix=g_mix.shape, w_in=w_in.shape, b_f=b_f.shape, w_conv=w_conv.shape,
                  g_conv_out=g_conv_out.shape, g_attn_out=g_attn_out.shape, w_o=w_o.shape, g_ffn=g_ffn.shape,
                  w_up=w_up.shape, w_ffn_conv=w_ffn_conv.shape, w_down=w_down.shape, g_final=g_final.shape)

    tm = t = 512
    tf = 256
    xs, tgt = x[0], loss_target[0]
    seq = xs.shape[0]
    assert seq % tm == 0 and seq % tf == 0
    bk = 1024 if seq % 1024 == 0 else 512

    g_in, g_conv = _all_gather([w["w_in"].astype(BF), w["w_conv"]])
    w_in_f = _from_col_blocks(g_in)
    w_zc = w_in_f[:, :3 * CONV_CH]
    w_qkv = w_in_f[:, 3 * CONV_CH:3 * CONV_CH + 3 * ATTN_W]
    w_f = jnp.pad(w_in_f[:, 3 * CONV_CH + 3 * ATTN_W:], ((0, 0), (0, LANES - N_HEADS)))
    b_f_p = jnp.pad(b_f, ((0, 0), (0, LANES - N_HEADS)))
    w_conv_p = _pad_rows(_from_col_blocks(g_conv), 8)

    h1, zc, qkv, fpre, fcol, nc, qa, ka, qt, vt = _fwd_in(
        xs, g_mix, w_zc, w_qkv, w_f, b_f_p, w_conv_p, g_conv_out, tm)
    o, lcol, g_o, g_up, g_dn, g_fc = _attn_fwd(
        qa, ka, vt, t, [w["w_o"].astype(BF), w["w_up"].astype(BF), w["w_down"].astype(BF), w["w_ffn_conv"]])
    w_o_f = g_o.reshape(D_MODEL, D_MODEL)
    w_up_f = _from_col_blocks(g_up)
    w_dn_f = g_dn.reshape(D_FF, D_MODEL)
    w_fc_p = _pad_rows(_from_col_blocks(g_fc), 8)
    x2, h2, mix, u0, uc, act, dx3, dx3b, loss, dg_final = _fwd_ffn(
        xs, nc, o, tgt, g_attn_out, w_o_f, g_ffn, w_up_f, w_fc_p, w_dn_f, g_final.reshape(1, D_MODEL), tf)
    loss_part = loss

    du0, dx2, dx2b, dw_fc, dg_ffn = _bwd_ffn(dx3, dx3b, u0, uc, x2, w_up_f.T, w_fc_p, w_dn_f.T, g_ffn, tf)
    dw_up = _matmul_tn(h2, du0, 1408, 2 * bk if seq % (2 * bk) == 0 else bk, "dw_up")
    dw_dn, = _matmul_tn_multi(act, [dx3b], bk, "dw_down")
    dw_o, = _matmul_tn_multi(mix, [dx2b], bk, "dw_o")
    dzc, doa, dot, qab, dw_conv, dg_attn, dg_conv = _bwd_out(
        dx2b, o, zc, fcol, lcol, qa, w_o_f, g_attn_out, g_conv_out, w_conv_p, t)
    dk, dv, dfk, dqa, dq, r_up, r_dn, r_fc, r_o = _attn_bwd(
        qab, doa, ka, qkv, qt, dot, t,
        [_col_blocks(dw_up), dw_dn.reshape(8, D_FF // 8, D_MODEL), _col_blocks(dw_fc[:3]),
         dw_o.reshape(8, D_MODEL // 8, D_MODEL)])
    dw_zc, dw_q, dw_k, dw_v = _matmul_tn_multi(h1, [dzc, dq, dk, dv], bk, "dw_in")
    dw_main = jnp.concatenate([dw_zc, dw_q, dw_k, dw_v, jnp.zeros((D_MODEL, N_HEADS), F32)], axis=1)
    gx, dfb, dg_mix, db_f, r_in = _bwd_in(xs, dx2, dzc, dq, dqa, dk, dv, dfk, fpre, w_zc, w_qkv, w_f, g_mix, tm,
                                          [_col_blocks(dw_main).astype(BF)])
    dw_f = _matmul_tn(h1, dfb, LANES, bk, "dw_in_f")
    small = dict(g_mix=dg_mix, b_f=db_f[:, :N_HEADS], g_conv_out=dg_conv, g_attn_out=dg_attn, g_ffn=dg_ffn,
                 g_final=dg_final)
    riders = jnp.concatenate([loss_part.reshape(-1), dw_f[:, :N_HEADS].reshape(-1)])
    r_conv, r_small = _all_to_all(
        [_col_blocks(dw_conv[:3]), jnp.broadcast_to(_pack_small(small, riders), (8, _SMALL_ROWS, LANES))],
        "all_to_all_grads")

    g_out, d_out, m_out, v_out = {}, {}, {}, {}
    res = _adamw(r_small, _pack_small(w), _pack_small(m), _pack_small(v), _SMALL_ROWS, "adamw_gains")
    for dst, packed in zip((g_out, d_out, m_out, v_out), res):
        dst.update(_unpack_small(packed, shapes))
    summed = res[0].reshape(-1)
    loss = summed[_SMALL_LEN]
    is_last = (4 * lax.axis_index("x") + 2 * lax.axis_index("y") + lax.axis_index("c") == 7).astype(F32)
    dw_f_sum = summed[_SMALL_LEN + 1:_SMALL_LEN + 1 + D_MODEL * N_HEADS].reshape(D_MODEL, N_HEADS) * is_last

    big = ("w_in", "w_o", "w_up", "w_down", "w_conv", "w_ffn_conv")
    recv = [r_in, r_o, r_up, r_dn, r_conv, r_fc]
    rows = dict(w_in=256, w_o=128, w_up=256, w_down=176, w_conv=3, w_ffn_conv=3)
    for a, k in enumerate(big):
        res = _adamw(recv[a], w[k], m[k], v[k], rows[k], "adamw_" + k, tail=dw_f_sum if k == "w_in" else None)
        g_out[k], d_out[k], m_out[k], v_out[k] = [r.reshape(shapes[k]) for r in res]

    order = ("g_mix", "w_in", "b_f", "w_conv", "g_conv_out", "g_attn_out", "w_o", "g_ffn", "w_up", "w_ffn_conv",
             "w_down", "g_final")
    return (loss, gx.reshape(x.shape), *[g_out[k] for k in order], *[d_out[k] for k in order],
            *[m_out[k] for k in order], *[v_out[k] for k in order])
```

```python
import jax
import jax.numpy as jnp
import numpy as np
from jax import lax
from jax.experimental import pallas as pl
from jax.experimental.pallas import tpu as pltpu

F32 = jnp.float32
BF = jnp.bfloat16
HI = lax.Precision.HIGHEST
MESH = pl.DeviceIdType.MESH

D_MODEL = 1024
CONV_CH = 512
ATTN_W = 512
N_HEADS = 8
HEAD_DIM = 64
D_FF = 2816
FF_CHUNK = 256
N_FF_CHUNKS = D_FF // FF_CHUNK
EPS = 1e-6
NEG = -1e30
LANES = 128
VMEM_BIG = 56 * 1024 * 1024
VMEM_MID = 40 * 1024 * 1024

ADAM_LR = 0.001
ADAM_B1 = 0.9
ADAM_B2 = 0.999
ADAM_EPS = 1e-08
ADAM_WD = 0.01
ADAM_STEP = 10

NT = (((1,), (1,)), ((), ()))
TN = (((0,), (0,)), ((), ()))


def _dot(a, b):
    return jnp.dot(a, b, preferred_element_type=F32)


def _dot_nt(a, b):
    return lax.dot_general(a, b, NT, preferred_element_type=F32)


def _dot_tn(a, b):
    return lax.dot_general(a, b, TN, preferred_element_type=F32)


def _params(vmem):
    return pltpu.CompilerParams(vmem_limit_bytes=vmem)


def _rows(shape):
    return lax.broadcasted_iota(jnp.int32, shape, 0)


def _cols(shape):
    return lax.broadcasted_iota(jnp.int32, shape, 1)


def _shift_down(u, prev, k):
    n = prev.shape[0]
    out = pltpu.roll(u, k, 0)
    row = _rows(u.shape)
    for r in range(k):
        out = jnp.where(row == r, prev[n - k + r:n - k + r + 1, :].astype(u.dtype), out)
    return out


def _shift_up(u, nxt, k):
    tm = u.shape[0]
    out = pltpu.roll(u, tm - k, 0)
    row = _rows(u.shape)
    for r in range(k):
        out = jnp.where(row == tm - k + r, nxt[r:r + 1, :], out)
    return out


def _conv3(u, prev, w):
    u1 = _shift_down(u, prev, 1)
    u2 = _shift_down(u, prev, 2)
    return w[0:1, :] * u2 + w[1:2, :] * u1 + w[2:3, :] * u, u1, u2


def _conv3_bwd(d, nxt, w):
    return w[2:3, :] * d + w[1:2, :] * _shift_up(d, nxt, 1) + w[0:1, :] * _shift_up(d, nxt, 2)


def _rms(x):
    return lax.rsqrt(jnp.mean(x * x, axis=-1, keepdims=True) + EPS)


def _rms_bwd(x, r, dyg):
    return r * dyg - x * (r * r * r) * jnp.mean(dyg * x, axis=-1, keepdims=True)


def _full(shape):
    nd = len(shape)
    return pl.BlockSpec(shape, lambda i, _n=nd: (0,) * _n)


ONES_LANE = 24


def _bias_scatter():
    sc = np.zeros((LANES, 2 * N_HEADS * LANES), np.float32)
    koff = N_HEADS * LANES
    for h in range(N_HEADS):
        aux = HEAD_DIM * (1 - h % 2)
        for j in range(3):
            sc[8 * j + h, LANES * h + aux + j] = 1.0
            sc[ONES_LANE, koff + LANES * h + aux + j] = 1.0
            sc[ONES_LANE, LANES * h + aux + 3 + j] = 1.0
            sc[8 * j + h, koff + LANES * h + aux + 3 + j] = -1.0
    return jnp.asarray(sc, BF)


def _split_parts(v, one):
    hi = v.astype(BF).astype(F32)
    rest = v - hi
    mid = rest.astype(BF).astype(F32)
    lo = (rest - mid).astype(BF).astype(F32)
    parts = hi + pltpu.roll(mid, 8, 1) + pltpu.roll(lo, 16, 1)
    return jnp.where(_cols(parts.shape) == ONES_LANE, one, parts).astype(BF)


def _fwd_in(x, g_mix, w_zc, w_qkv, w_f, b_f, w_conv, g_conv_out, tm):
    s = x.shape[0]
    nb = s // tm
    aw = N_HEADS * LANES

    def body(x_ref, gm_ref, wzc_ref, wqkv_ref, wf_ref, bf_ref, wc_ref, gco_ref, sc_ref,
             h1_ref, zc_ref, qkv_ref, fpre_ref, fcol_ref, nc_ref, qa_ref, ka_ref, qt_ref, vt_ref, cu_ref, cf_ref):
        i = pl.program_id(0)

        @pl.when(i == 0)
        def _():
            cu_ref[...] = jnp.zeros_like(cu_ref)
            cf_ref[...] = jnp.zeros_like(cf_ref)

        xv = x_ref[...]
        hb = (xv * _rms(xv) * gm_ref[...]).astype(BF)
        h1_ref[...] = hb
        zc = _dot(hb, wzc_ref[...])
        zc_ref[...] = zc
        qkv = _dot(hb, wqkv_ref[...])
        qkv = jnp.where(_cols(qkv.shape) < ATTN_W, qkv * 0.125, qkv)
        qkvb = qkv.astype(BF)
        qkv_ref[...] = qkvb
        qt_ref[0] = qkv[:, :ATTN_W].T.astype(BF)
        vt_ref[0] = qkv[:, 2 * ATTN_W:].T.astype(BF)

        gb, gc, xc = zc[:, :CONV_CH], zc[:, CONV_CH:2 * CONV_CH], zc[:, 2 * CONV_CH:]
        u = gc * xc
        cv, _, _ = _conv3(u, cu_ref[...], wc_ref[...])
        cu_ref[...] = u[tm - 8:, :]
        y = gb * cv
        nc_ref[...] = (y * _rms(y) * gco_ref[...]).astype(BF)

        fpre = _dot(hb, wf_ref[...]) + bf_ref[...]
        fpre_ref[...] = fpre
        logf = jnp.minimum(fpre, 0.0) - jnp.log1p(jnp.exp(-jnp.abs(fpre)))
        logf = jnp.where(_cols(logf.shape) < N_HEADS, logf, 0.0)
        tri = (_rows((LANES, LANES)) >= _cols((LANES, LANES))).astype(F32)
        offset = cf_ref[...]
        pieces = []
        for r in range(tm // LANES):
            piece = jnp.dot(tri, logf[r * LANES:(r + 1) * LANES, :], precision=HI, preferred_element_type=F32) + offset
            offset = piece[LANES - 1:LANES, :]
            pieces.append(piece)
        fcol = jnp.concatenate(pieces, axis=0)
        cf_ref[...] = offset
        fcol_ref[...] = fcol

        feat = _dot(_split_parts(fcol, 1.0), sc_ref[...]).astype(BF)
        lane = _cols((tm, LANES))
        for h in range(N_HEADS):
            hp, hh = divmod(h, 2)
            own = (lane >= HEAD_DIM * hh) & (lane < HEAD_DIM * (hh + 1))
            hs = slice(LANES * h, LANES * (h + 1))
            qa_ref[:, hs] = jnp.where(own, qkvb[:, LANES * hp:LANES * (hp + 1)], feat[:, hs])
            ka_ref[:, hs] = jnp.where(own, qkvb[:, ATTN_W + LANES * hp:ATTN_W + LANES * (hp + 1)],
                                      feat[:, aw + LANES * h:aw + LANES * (h + 1)])

    blk = lambda c: pl.BlockSpec((tm, c), lambda i: (i, 0))
    return pl.pallas_call(
        body, name="fwd_in", grid=(nb,),
        in_specs=[blk(D_MODEL), _full((1, D_MODEL)), _full(w_zc.shape), _full(w_qkv.shape), _full(w_f.shape),
                  _full((1, LANES)), _full((8, CONV_CH)), _full((1, CONV_CH)), _full((LANES, 2 * aw))],
        out_specs=[blk(D_MODEL), blk(3 * CONV_CH), blk(3 * ATTN_W), blk(LANES), blk(LANES),
                   blk(CONV_CH), blk(aw), blk(aw)] + [pl.BlockSpec((1, ATTN_W, tm), lambda i: (i, 0, 0))] * 2,
        out_shape=[jax.ShapeDtypeStruct((s, D_MODEL), BF), jax.ShapeDtypeStruct((s, 3 * CONV_CH), F32),
                   jax.ShapeDtypeStruct((s, 3 * ATTN_W), BF), jax.ShapeDtypeStruct((s, LANES), F32),
                   jax.ShapeDtypeStruct((s, LANES), F32),
                   jax.ShapeDtypeStruct((s, CONV_CH), BF), jax.ShapeDtypeStruct((s, aw), BF),
                   jax.ShapeDtypeStruct((s, aw), BF)] + [jax.ShapeDtypeStruct((nb, ATTN_W, tm), BF)] * 2,
        scratch_shapes=[pltpu.VMEM((8, CONV_CH), F32), pltpu.VMEM((1, LANES), F32)],
        compiler_params=_params(VMEM_MID),
    )(x, g_mix, w_zc, w_qkv, w_f, b_f, w_conv, g_conv_out, _bias_scatter())


def _pipeline_masked_last(last, produce, consume, buf_a, buf_b):
    produce(0, buf_a)

    def two_blocks(j, carry):
        blk = 2 * j
        produce(blk + 1, buf_b)
        consume(blk, buf_a, False)
        produce(blk + 2, buf_a)
        consume(blk + 1, buf_b, False)
        return carry

    lax.fori_loop(0, last // 2, two_blocks, 0)

    @pl.when(last % 2 == 0)
    def _():
        consume(last, buf_a, True)

    @pl.when(last % 2 == 1)
    def _():
        produce(last, buf_b)
        consume(last - 1, buf_a, False)
        consume(last, buf_b, True)


def _attn_fwd(qa, ka, vt, t, shards):
    s = qa.shape[0]
    nb = s // t
    n = len(shards)

    def body(qa_ref, ka_any, vt_any, *refs):
        shard_refs, (o_ref, l_ref), gathered = refs[:n], refs[n:n + 2], refs[n + 2:2 * n + 2]
        ka_scr, vt_scr, m_scr, acc_scr, ea_scr, eb_scr, sem = refs[2 * n + 2:2 * n + 9]
        i = pl.program_id(0)
        start, forward, finish = _gather_phases(shard_refs, gathered, *refs[2 * n + 9:])
        pl.when(i == 0)(start)
        pl.when(i == nb // 2)(forward)

        def block_copies(b):
            rows = pl.ds(b * t if isinstance(b, int) else pl.multiple_of(b * t, t), t)
            return (pltpu.make_async_copy(ka_any.at[rows], ka_scr.at[rows], sem.at[0, b]),
                    pltpu.make_async_copy(vt_any.at[b], vt_scr.at[b], sem.at[1, b]))

        @pl.when(i == 0)
        def _():
            for b in range(nb):
                for cp in block_copies(b):
                    cp.start()

        for cp in block_copies(i):
            cp.wait()

        causal_t = _rows((t, t)) <= _cols((t, t))
        row = _rows((LANES, t))
        lrows = jnp.zeros((LANES, t), F32)
        for hp in range(N_HEADS // 2):
            ps = slice(hp * LANES, (hp + 1) * LANES)
            m_scr[...] = jnp.full(m_scr.shape, NEG, F32)
            acc_scr[...] = jnp.zeros(acc_scr.shape, F32)
            heads = [dict(hh=hh, hs=slice((2 * hp + hh) * LANES, (2 * hp + hh + 1) * LANES),
                          aux=HEAD_DIM * (1 - hh), own=(row >= HEAD_DIM * hh) & (row < HEAD_DIM * (hh + 1)),
                          fill=jnp.where(row == HEAD_DIM * (1 - hh), 1.0, 0.0).astype(BF),
                          qh=qa_ref[:, (2 * hp + hh) * LANES:(2 * hp + hh + 1) * LANES]) for hh in range(2)]

            def scores(kb, dst, heads=heads):
                rs = pl.ds(pl.multiple_of(kb * t, t), t)
                for hd in heads:
                    dst[hd["hh"]] = _dot_nt(ka_scr[rs, hd["hs"]], hd["qh"])

            def consume(kb, src, masked, ps=ps, heads=heads):
                vt2 = vt_scr[kb, ps, :]
                m_olds = [m_scr[hd["hh"]:hd["hh"] + 1, :] for hd in heads]
                accs = [acc_scr[hd["hh"]] for hd in heads]
                m_news, acc_news = [], []
                for hd, m_old, acc in zip(heads, m_olds, accs):
                    e = src[hd["hh"]]
                    if masked:
                        e = jnp.where(causal_t, e, NEG)
                    m_new = jnp.maximum(m_old, jnp.max(e, axis=0, keepdims=True))
                    p = jnp.exp(e - m_new).astype(BF)
                    vta = jnp.where(hd["own"], vt2, hd["fill"])
                    acc_news.append(jnp.exp(m_old - m_new) * acc + _dot(vta, p))
                    m_news.append(m_new)
                for hd, m_new, acc in zip(heads, m_news, acc_news):
                    acc_scr[hd["hh"]] = acc
                    m_scr[hd["hh"]:hd["hh"] + 1, :] = m_new

            _pipeline_masked_last(i, scores, consume, ea_scr, eb_scr)

            o_pair = None
            for hd in heads:
                hh = hd["hh"]
                acc = acc_scr[hh]
                denom = acc[hd["aux"]:hd["aux"] + 1, :]
                o_h = acc / denom
                lrows = jnp.where(row == 2 * hp + hh, m_scr[hh:hh + 1, :] + jnp.log(denom), lrows)
                o_pair = o_h if hh == 0 else jnp.where(row < HEAD_DIM, o_pair, o_h)
            o_ref[:, ps] = o_pair.T
        l_ref[...] = lrows.T
        pl.when(i == nb - 1)(finish)

    any_ = pl.BlockSpec(memory_space=pl.ANY)
    return pl.pallas_call(
        body, name="attn_fwd", grid=(nb,),
        in_specs=[pl.BlockSpec((t, N_HEADS * LANES), lambda i: (i, 0)), any_, any_] + [any_] * n,
        out_specs=[pl.BlockSpec((t, ATTN_W), lambda i: (i, 0)), pl.BlockSpec((t, LANES), lambda i: (i, 0))]
        + [any_] * n,
        out_shape=[jax.ShapeDtypeStruct((s, ATTN_W), F32), jax.ShapeDtypeStruct((s, LANES), F32)]
        + _gathered_shapes(shards),
        scratch_shapes=[pltpu.VMEM(ka.shape, BF), pltpu.VMEM(vt.shape, BF), pltpu.VMEM((2, t), F32),
                        pltpu.VMEM((2, LANES, t), F32), pltpu.VMEM((2, t, t), F32), pltpu.VMEM((2, t, t), F32),
                        pltpu.SemaphoreType.DMA((2, nb))] + _exchange_sems(n),
        compiler_params=_params(VMEM_BIG),
    )(qa, ka, vt, *shards)


def _load_weights(i, pairs, sem):
    @pl.when(i == 0)
    def _():
        cps = [pltpu.make_async_copy(src, dst, sem.at[n]) for n, (src, dst) in enumerate(pairs)]
        for cp in cps:
            cp.start()
        for cp in cps:
            cp.wait()


def _ff_cols(j):
    return (slice(j * FF_CHUNK, (j + 1) * FF_CHUNK), slice(D_FF + j * FF_CHUNK, D_FF + (j + 1) * FF_CHUNK))


def _fwd_ffn(x, nc, o, tgt, g_attn_out, w_o, g_ffn, w_up, w_ffn_conv, w_dn, g_final, tm):
    s = x.shape[0]

    def body(x_ref, nc_ref, o_ref, tgt_ref, ga_ref, gf_ref, wfc_ref, gfin_ref, wo_any, wup_any, wdn_any,
             x2_ref, h2_ref, mix_ref, u0_ref, uc_ref, act_ref, dx3_ref, dx3b_ref, loss_ref, dgfin_ref,
             wo, wup, wdn, carry, sem):
        i = pl.program_id(0)
        _load_weights(i, [(wo_any, wo), (wup_any, wup), (wdn_any, wdn)], sem)

        @pl.when(i == 0)
        def _():
            carry[...] = jnp.zeros_like(carry)
            loss_ref[...] = jnp.zeros_like(loss_ref)
            dgfin_ref[...] = jnp.zeros_like(dgfin_ref)

        ov = o_ref[...]
        na = (ov * _rms(ov) * ga_ref[...]).astype(BF)
        ncv = nc_ref[...]
        mix_ref[:, :CONV_CH] = ncv
        mix_ref[:, CONV_CH:] = na
        x2 = x_ref[...] + _dot(ncv, wo[:CONV_CH, :]) + _dot(na, wo[CONV_CH:, :])
        x2_ref[...] = x2
        hb = (x2 * _rms(x2) * gf_ref[...]).astype(BF)
        h2_ref[...] = hb

        def up(j):
            return [_dot(hb, wup[:, cc]) for cc in _ff_cols(j)]

        nxt = up(0)
        down = None
        for j in range(N_FF_CHUNKS):
            cur = nxt
            if j + 1 < N_FF_CHUNKS:
                nxt = up(j + 1)
            parts = []
            for cc, u0 in zip(_ff_cols(j), cur):
                u0_ref[:, cc] = u0.astype(BF)
                uu, _, _ = _conv3(u0, carry[:, cc], wfc_ref[:, cc])
                carry[:, cc] = u0[tm - 8:, :]
                uc_ref[:, cc] = uu.astype(BF)
                parts.append(uu)
            ua, ug = parts
            act = (ug * jax.nn.sigmoid(ug) * ua).astype(BF)
            ca = _ff_cols(j)[0]
            act_ref[:, ca] = act
            part = _dot(act, wdn[ca, :])
            down = part if down is None else down + part

        x3 = x2_ref[...] + down
        r3 = _rms(x3)
        gfin = gfin_ref[...]
        xn = x3 * r3
        diff = xn * gfin - tgt_ref[...]
        loss_ref[...] += jnp.sum(jnp.sum(diff * diff, axis=-1, keepdims=True), axis=0, keepdims=True) * (0.5 / D_MODEL)
        dy = diff * (1.0 / D_MODEL)
        dgfin_ref[...] += jnp.sum(dy * xn, axis=0, keepdims=True)
        dx3 = _rms_bwd(x3, r3, dy * gfin)
        dx3_ref[...] = dx3
        dx3b_ref[...] = dx3.astype(BF)

    blk = lambda c: pl.BlockSpec((tm, c), lambda i: (i, 0))
    any_ = pl.BlockSpec(memory_space=pl.ANY)
    return pl.pallas_call(
        body, name="fwd_ffn", grid=(s // tm,),
        in_specs=[blk(D_MODEL), blk(CONV_CH), blk(ATTN_W), blk(D_MODEL), _full((1, ATTN_W)), _full((1, D_MODEL)),
                  _full((8, 2 * D_FF)), _full((1, D_MODEL)), any_, any_, any_],
        out_specs=[blk(D_MODEL), blk(D_MODEL), blk(D_MODEL), blk(2 * D_FF), blk(2 * D_FF), blk(D_FF), blk(D_MODEL),
                   blk(D_MODEL), _full((1, 1)), _full((1, D_MODEL))],
        out_shape=[jax.ShapeDtypeStruct((s, D_MODEL), F32), jax.ShapeDtypeStruct((s, D_MODEL), BF),
                   jax.ShapeDtypeStruct((s, D_MODEL), BF),
                   jax.ShapeDtypeStruct((s, 2 * D_FF), BF), jax.ShapeDtypeStruct((s, 2 * D_FF), BF),
                   jax.ShapeDtypeStruct((s, D_FF), BF),
                   jax.ShapeDtypeStruct((s, D_MODEL), F32), jax.ShapeDtypeStruct((s, D_MODEL), BF),
                   jax.ShapeDtypeStruct((1, 1), F32), jax.ShapeDtypeStruct((1, D_MODEL), F32)],
        scratch_shapes=[pltpu.VMEM(w_o.shape, BF), pltpu.VMEM(w_up.shape, BF), pltpu.VMEM(w_dn.shape, BF),
                        pltpu.VMEM((8, 2 * D_FF), F32), pltpu.SemaphoreType.DMA((3,))],
        compiler_params=_params(VMEM_BIG),
    )(x, nc, o, tgt, g_attn_out, g_ffn, w_ffn_conv, g_final, w_o, w_up, w_dn)


def _bwd_ffn(dx3, dx3b, u0, uc, x2, w_up, w_ffn_conv, w_dn, g_ffn, tm):
    s = x2.shape[0]
    nb = s // tm

    def body(dx3_ref, dx3b_ref, u0_ref, uc_ref, x2_ref, wfc_ref, gf_ref, wup_any, wdn_any,
             du0_ref, dx2_ref, dx2b_ref, dwfc_ref, dgf_ref,
             wup, wdn, carry, sem):
        i = pl.program_id(0)
        _load_weights(i, [(wup_any, wup), (wdn_any, wdn)], sem)

        @pl.when(i == 0)
        def _():
            carry[...] = jnp.zeros_like(carry)
            dwfc_ref[...] = jnp.zeros_like(dwfc_ref)
            dgf_ref[...] = jnp.zeros_like(dgf_ref)

        db = dx3b_ref[...]

        def dact_of(j):
            return _dot(db, wdn[:, _ff_cols(j)[0]])

        nxt = dact_of(0)
        dh2 = None
        for j in range(N_FF_CHUNKS):
            ca, cg = _ff_cols(j)
            dact = nxt
            if j + 1 < N_FF_CHUNKS:
                nxt = dact_of(j + 1)
            ua = uc_ref[:, ca].astype(F32)
            ug = uc_ref[:, cg].astype(F32)
            sg = jax.nn.sigmoid(ug)
            da = dact * (ug * sg)
            dg = dact * ua * (sg * (1.0 + ug * (1.0 - sg)))
            for cc, d in ((ca, da), (cg, dg)):
                nxt_rows = carry[:, cc]
                d1 = _shift_up(d, nxt_rows, 1)
                d2 = _shift_up(d, nxt_rows, 2)
                u0c = u0_ref[:, cc].astype(F32)
                w = wfc_ref[:, cc]
                dwfc_ref[0:1, cc] += jnp.sum(d2 * u0c, axis=0, keepdims=True)
                dwfc_ref[1:2, cc] += jnp.sum(d1 * u0c, axis=0, keepdims=True)
                dwfc_ref[2:3, cc] += jnp.sum(d * u0c, axis=0, keepdims=True)
                du0 = (w[2:3, :] * d + w[1:2, :] * d1 + w[0:1, :] * d2).astype(BF)
                carry[:, cc] = d[:8, :]
                du0_ref[:, cc] = du0
                part = _dot(du0, wup[cc, :])
                dh2 = part if dh2 is None else dh2 + part

        x2v = x2_ref[...]
        r2 = _rms(x2v)
        dgf_ref[...] += jnp.sum(dh2 * (x2v * r2), axis=0, keepdims=True)
        dx2 = dx3_ref[...] + _rms_bwd(x2v, r2, dh2 * gf_ref[...])
        dx2_ref[...] = dx2
        dx2b_ref[...] = dx2.astype(BF)

    blk = lambda c: pl.BlockSpec((tm, c), lambda i: (nb - 1 - i, 0))
    any_ = pl.BlockSpec(memory_space=pl.ANY)
    return pl.pallas_call(
        body, name="bwd_ffn", grid=(nb,),
        in_specs=[blk(D_MODEL), blk(D_MODEL), blk(2 * D_FF), blk(2 * D_FF), blk(D_MODEL), _full((8, 2 * D_FF)),
                  _full((1, D_MODEL)), any_, any_],
        out_specs=[blk(2 * D_FF), blk(D_MODEL), blk(D_MODEL), _full((8, 2 * D_FF)), _full((1, D_MODEL))],
        out_shape=[jax.ShapeDtypeStruct((s, 2 * D_FF), BF), jax.ShapeDtypeStruct((s, D_MODEL), F32),
                   jax.ShapeDtypeStruct((s, D_MODEL), BF), jax.ShapeDtypeStruct((8, 2 * D_FF), F32),
                   jax.ShapeDtypeStruct((1, D_MODEL), F32)],
        scratch_shapes=[pltpu.VMEM(w_up.shape, BF), pltpu.VMEM(w_dn.shape, BF), pltpu.VMEM((8, 2 * D_FF), F32),
                        pltpu.SemaphoreType.DMA((2,))],
        compiler_params=_params(VMEM_BIG),
    )(dx3, dx3b, u0, uc, x2, w_ffn_conv, g_ffn, w_up, w_dn)


def _bwd_out(dx2b, o, zc, fcol, lcol, qa, w_o, g_attn_out, g_conv_out, w_conv, tm):
    s = o.shape[0]
    nb = s // tm
    aw = N_HEADS * LANES

    def body(dx2b_ref, o_ref, zc_ref, halo_ref, fcol_ref, lcol_ref, qa_ref, wo_ref, ga_ref, gco_ref, wc_ref, sc_ref,
             dzc_ref, doa_ref, dot_ref, qab_ref, dwc_ref, dga_ref, dgco_ref, carry):
        i = pl.program_id(0)
        rb = nb - 1 - i

        @pl.when(i == 0)
        def _():
            carry[...] = jnp.zeros_like(carry)
            dwc_ref[...] = jnp.zeros_like(dwc_ref)
            dga_ref[...] = jnp.zeros_like(dga_ref)
            dgco_ref[...] = jnp.zeros_like(dgco_ref)

        dmix = _dot_nt(dx2b_ref[...], wo_ref[...])
        dnc, dna = dmix[:, :CONV_CH], dmix[:, CONV_CH:]

        ov = o_ref[...]
        ra = _rms(ov)
        dga_ref[...] += jnp.sum(dna * (ov * ra), axis=0, keepdims=True)
        do = _rms_bwd(ov, ra, dna * ga_ref[...])
        dob = do.astype(BF)
        dot_ref[0] = do.T.astype(BF)
        sel = (_rows((ATTN_W, LANES)) // HEAD_DIM == _cols((ATTN_W, LANES))).astype(F32)
        delta = jnp.dot(do * ov, sel, precision=lax.Precision.HIGH, preferred_element_type=F32)
        featc = _dot(_split_parts(fcol_ref[...] - lcol_ref[...], 1.0), sc_ref[...]).astype(BF)
        featd = _dot(_split_parts(-delta, 0.0), sc_ref[...]).astype(BF)
        lane = _cols((tm, LANES))
        for h in range(N_HEADS):
            hp, hh = divmod(h, 2)
            own = (lane >= HEAD_DIM * hh) & (lane < HEAD_DIM * (hh + 1))
            hs = slice(LANES * h, LANES * (h + 1))
            qab_ref[:, hs] = jnp.where(own, qa_ref[:, hs], featc[:, hs])
            doa_ref[:, hs] = jnp.where(own, dob[:, LANES * hp:LANES * (hp + 1)], featd[:, hs])

        zc_v = zc_ref[...]
        gb, gc, xc = zc_v[:, :CONV_CH], zc_v[:, CONV_CH:2 * CONV_CH], zc_v[:, 2 * CONV_CH:]
        hal = halo_ref[...] * (rb > 0).astype(F32)
        u = gc * xc
        prev = hal[:, CONV_CH:2 * CONV_CH] * hal[:, 2 * CONV_CH:]
        wc = wc_ref[...]
        cv, u1, u2 = _conv3(u, prev, wc)
        y = gb * cv
        rc = _rms(y)
        dgco_ref[...] += jnp.sum(dnc * (y * rc), axis=0, keepdims=True)
        dy = _rms_bwd(y, rc, dnc * gco_ref[...])
        dcv = dy * gb
        dwc_ref[0:1, :] += jnp.sum(dcv * u2, axis=0, keepdims=True)
        dwc_ref[1:2, :] += jnp.sum(dcv * u1, axis=0, keepdims=True)
        dwc_ref[2:3, :] += jnp.sum(dcv * u, axis=0, keepdims=True)
        du = _conv3_bwd(dcv, carry[...], wc)
        carry[...] = dcv[:8, :]
        dzc_ref[:, :CONV_CH] = (dy * cv).astype(BF)
        dzc_ref[:, CONV_CH:2 * CONV_CH] = (du * xc).astype(BF)
        dzc_ref[:, 2 * CONV_CH:] = (du * gc).astype(BF)

    blk = lambda c: pl.BlockSpec((tm, c), lambda i: (nb - 1 - i, 0))
    halo = pl.BlockSpec((8, 3 * CONV_CH), lambda i: (jnp.maximum((nb - 1 - i) * (tm // 8) - 1, 0), 0))
    tr = pl.BlockSpec((1, ATTN_W, tm), lambda i: (nb - 1 - i, 0, 0))
    return pl.pallas_call(
        body, name="bwd_out", grid=(nb,),
        in_specs=[blk(D_MODEL), blk(ATTN_W), blk(3 * CONV_CH), halo, blk(LANES), blk(LANES), blk(aw),
                  _full(w_o.shape), _full((1, ATTN_W)), _full((1, CONV_CH)), _full((8, CONV_CH)), _full((LANES, aw))],
        out_specs=[blk(3 * CONV_CH), blk(aw), tr, blk(aw),
                   _full((8, CONV_CH)), _full((1, ATTN_W)), _full((1, CONV_CH))],
        out_shape=[jax.ShapeDtypeStruct((s, 3 * CONV_CH), BF),
                   jax.ShapeDtypeStruct((s, aw), BF), jax.ShapeDtypeStruct((nb, ATTN_W, tm), BF),
                   jax.ShapeDtypeStruct((s, aw), BF),
                   jax.ShapeDtypeStruct((8, CONV_CH), F32), jax.ShapeDtypeStruct((1, ATTN_W), F32),
                   jax.ShapeDtypeStruct((1, CONV_CH), F32)],
        scratch_shapes=[pltpu.VMEM((8, CONV_CH), F32)],
        compiler_params=_params(VMEM_MID),
    )(dx2b, o, zc, zc, fcol, lcol, qa, w_o, g_attn_out, g_conv_out, w_conv, _bias_scatter()[:, :aw])


def _attn_bwd(qab, doa, ka, qkv, qt, dot, t, blocks):
    s = qab.shape[0]
    nb = s // t
    npair = N_HEADS // 2
    n = len(blocks)

    def body(ka_ref, v_ref, qab_ref, doa_ref, qt_ref, dot_ref, *refs):
        block_refs, (dk_ref, dv_ref, dfk_ref, dqa_ref, dq_ref) = refs[:n], refs[n:n + 5]
        received = refs[n + 5:2 * n + 5]
        acck, accv = refs[2 * n + 5:2 * n + 7]
        hp = pl.program_id(0)
        j = pl.program_id(1)
        start, finish = _scatter_phases(block_refs, received, *refs[2 * n + 7:])
        pl.when((hp == 0) & (j == 0))(start)

        @pl.when(j == 0)
        def _():
            dqa_ref[...] = jnp.zeros_like(dqa_ref)

        causal = _cols((t, t)) <= _rows((t, t))
        row = _rows((LANES, t))
        lane = _cols((t, LANES))
        acck[...] = jnp.zeros(acck.shape, F32)
        accv[...] = jnp.zeros(accv.shape, F32)
        v2 = v_ref[...]
        heads = []
        for hh in range(2):
            aux = HEAD_DIM * (1 - hh)
            ownl = (lane >= HEAD_DIM * hh) & (lane < HEAD_DIM * (hh + 1))
            ones3 = jnp.where((lane >= aux) & (lane < aux + 3), 1.0, 0.0).astype(BF)
            heads.append(dict(hh=hh, hs=slice(hh * LANES, (hh + 1) * LANES), aux=aux,
                              own=(row >= HEAD_DIM * hh) & (row < HEAD_DIM * (hh + 1)),
                              fill=jnp.where(row == aux, 1.0, 0.0).astype(BF),
                              kh=ka_ref[:, hh * LANES:(hh + 1) * LANES], vh=jnp.where(ownl, v2, ones3)))

        def tile(qb, masked):
            rs = pl.ds(pl.multiple_of(qb * t, t), t)
            qt2 = qt_ref[qb]
            dot2 = dot_ref[qb]
            es = [_dot_nt(qab_ref[rs, hd["hs"]], hd["kh"]) for hd in heads]
            dps = [_dot_nt(doa_ref[rs, hd["hs"]], hd["vh"]) for hd in heads]
            aks = [acck[hd["hh"]] for hd in heads]
            avs = [accv[hd["hh"]] for hd in heads]
            dqs = [dqa_ref[rs, hd["hs"]] for hd in heads]
            outs = []
            for hd, e, dp, ak, av, dq in zip(heads, es, dps, aks, avs, dqs):
                if masked:
                    e = jnp.where(causal, e, NEG)
                p = jnp.exp(e)
                ds = (p * dp).astype(BF)
                outs.append((ak + _dot(jnp.where(hd["own"], qt2, hd["fill"]), ds), av + _dot(dot2, p.astype(BF)),
                             dq + _dot(ds, hd["kh"])))
            for hd, (ak, av, dq) in zip(heads, outs):
                acck[hd["hh"]] = ak
                accv[hd["hh"]] = av
                dqa_ref[rs, hd["hs"]] = dq

        def loop_body(qb, carry):
            tile(qb, False)
            return carry

        tile(j, True)
        lax.fori_loop(j + 1, nb, loop_body, 0)
        dk_pair = dv_pair = None
        dfrows = jnp.zeros((LANES, t), F32)
        for hd in heads:
            ak, av = acck[hd["hh"]], accv[hd["hh"]]
            dfrows = jnp.where(row == 2 * hp + hd["hh"], -ak[hd["aux"]:hd["aux"] + 1, :], dfrows)
            dk_pair = ak if hd["hh"] == 0 else jnp.where(row < HEAD_DIM, dk_pair, ak)
            dv_pair = av if hd["hh"] == 0 else jnp.where(row < HEAD_DIM, dv_pair, av)
        dk_ref[...] = dk_pair.T.astype(BF)
        dv_ref[...] = dv_pair.T.astype(BF)
        dfk_ref[0] = dfrows.T

        @pl.when(j == nb - 1)
        def _():
            for r in range(nb):
                rows = slice(r * t, (r + 1) * t)
                dq_ref[rows, :] = (jnp.where(lane < HEAD_DIM, dqa_ref[rows, :LANES], dqa_ref[rows, LANES:])
                                   * 0.125).astype(BF)

        pl.when((hp == npair - 1) & (j == nb - 1))(finish)

    pair_cols = pl.BlockSpec((s, 2 * LANES), lambda hp, j: (0, hp))
    pair_rows = pl.BlockSpec((nb, LANES, t), lambda hp, j: (0, hp, 0))
    any_ = pl.BlockSpec(memory_space=pl.ANY)
    return pl.pallas_call(
        body, name="attn_bwd", grid=(npair, nb),
        in_specs=[pl.BlockSpec((t, 2 * LANES), lambda hp, j: (j, hp)),
                  pl.BlockSpec((t, LANES), lambda hp, j: (j, 2 * npair + hp)),
                  pair_cols, pair_cols, pair_rows, pair_rows] + [any_] * n,
        out_specs=[pl.BlockSpec((t, LANES), lambda hp, j: (j, hp)), pl.BlockSpec((t, LANES), lambda hp, j: (j, hp)),
                   pl.BlockSpec((1, t, LANES), lambda hp, j: (hp, j, 0)), pair_cols,
                   pl.BlockSpec((s, LANES), lambda hp, j: (0, hp))] + [any_] * n,
        out_shape=[jax.ShapeDtypeStruct((s, ATTN_W), BF), jax.ShapeDtypeStruct((s, ATTN_W), BF),
                   jax.ShapeDtypeStruct((npair, s, LANES), F32), jax.ShapeDtypeStruct((s, N_HEADS * LANES), F32),
                   jax.ShapeDtypeStruct((s, ATTN_W), BF)]
        + [jax.ShapeDtypeStruct(b.shape, b.dtype) for b in blocks],
        scratch_shapes=[pltpu.VMEM((2, LANES, t), F32), pltpu.VMEM((2, LANES, t), F32)] + _exchange_sems(n),
        compiler_params=_params(VMEM_BIG),
    )(ka, qkv, qab, doa, qt, dot, *blocks)


def _bwd_in(x, dx2, dzc, dq, dqa, dk, dv, dfk, fpre, w_zc, w_qkv, w_f, g_mix, tm, blocks):
    s = x.shape[0]
    nb = s // tm
    n = len(blocks)

    def body(x_ref, dx2_ref, dzc_ref, dq_ref, dqa_ref, dk_ref, dv_ref, dfk_ref, fpre_ref, wzc_ref, wqkv_ref, wf_ref,
             gm_ref, *refs):
        block_refs, (gx_ref, dfb_ref, dgm_ref, dbf_ref), received = refs[:n], refs[n:n + 4], refs[n + 4:2 * n + 4]
        carry = refs[2 * n + 4]
        i = pl.program_id(0)
        start, finish = _scatter_phases(block_refs, received, *refs[2 * n + 5:])
        pl.when(i == 0)(start)

        @pl.when(i == 0)
        def _():
            carry[...] = jnp.zeros_like(carry)
            dgm_ref[...] = jnp.zeros_like(dgm_ref)
            dbf_ref[...] = jnp.zeros_like(dbf_ref)

        lane = _cols((tm, LANES))
        dfq = jnp.zeros((tm, LANES), F32)
        for h in range(N_HEADS):
            aux = LANES * h + HEAD_DIM * (1 - h % 2)
            dfq = jnp.where(lane == h, dqa_ref[:, aux:aux + 1], dfq)

        df_cum = dfq + ((dfk_ref[0] + dfk_ref[1]) + (dfk_ref[2] + dfk_ref[3]))
        triu = (_rows((LANES, LANES)) <= _cols((LANES, LANES))).astype(F32)
        offset = carry[...]
        pieces = []
        for r in reversed(range(tm // LANES)):
            piece = jnp.dot(triu, df_cum[r * LANES:(r + 1) * LANES, :], precision=HI,
                            preferred_element_type=F32) + offset
            offset = piece[0:1, :]
            pieces.append(piece)
        dlogf = jnp.concatenate(pieces[::-1], axis=0)
        carry[...] = offset
        fpre = fpre_ref[...]
        df = jnp.where(_cols(fpre.shape) < N_HEADS, dlogf / (1.0 + jnp.exp(fpre)), 0.0)
        dbf_ref[...] += jnp.sum(df, axis=0, keepdims=True)
        dfb = df.astype(BF)
        dfb_ref[...] = dfb

        dh1 = _dot_nt(dzc_ref[...], wzc_ref[...])
        dh1 += _dot_nt(dq_ref[...], wqkv_ref[:, :ATTN_W])
        dh1 += _dot_nt(dk_ref[...], wqkv_ref[:, ATTN_W:2 * ATTN_W])
        dh1 += _dot_nt(dv_ref[...], wqkv_ref[:, 2 * ATTN_W:])
        dh1 += _dot_nt(dfb, wf_ref[...])
        xv = x_ref[...]
        r1 = _rms(xv)
        dgm_ref[...] += jnp.sum(dh1 * (xv * r1), axis=0, keepdims=True)
        gx_ref[...] = dx2_ref[...] + _rms_bwd(xv, r1, dh1 * gm_ref[...])
        pl.when(i == nb - 1)(finish)

    blk = lambda c: pl.BlockSpec((tm, c), lambda i: (nb - 1 - i, 0))
    any_ = pl.BlockSpec(memory_space=pl.ANY)
    return pl.pallas_call(
        body, name="bwd_in", grid=(nb,),
        in_specs=[blk(D_MODEL), blk(D_MODEL), blk(3 * CONV_CH), blk(ATTN_W), blk(N_HEADS * LANES), blk(ATTN_W),
                  blk(ATTN_W), pl.BlockSpec((N_HEADS // 2, tm, LANES), lambda i: (0, nb - 1 - i, 0)), blk(LANES),
                  _full(w_zc.shape), _full(w_qkv.shape), _full(w_f.shape), _full((1, D_MODEL))] + [any_] * n,
        out_specs=[blk(D_MODEL), blk(LANES), _full((1, D_MODEL)), _full((1, LANES))] + [any_] * n,
        out_shape=[jax.ShapeDtypeStruct((s, D_MODEL), F32), jax.ShapeDtypeStruct((s, LANES), BF),
                   jax.ShapeDtypeStruct((1, D_MODEL), F32), jax.ShapeDtypeStruct((1, LANES), F32)]
        + [jax.ShapeDtypeStruct(b.shape, b.dtype) for b in blocks],
        scratch_shapes=[pltpu.VMEM((1, LANES), F32)] + _exchange_sems(n),
        compiler_params=_params(VMEM_MID),
    )(x, dx2, dzc, dq, dqa, dk, dv, dfk, fpre, w_zc, w_qkv, w_f, g_mix, *blocks)


def _matmul_tn(a, b, bn, bk, name):
    s, m = a.shape
    n = b.shape[1]

    def body(a_ref, b_ref, o_ref):
        @pl.when(pl.program_id(1) == 0)
        def _():
            o_ref[...] = jnp.zeros_like(o_ref)

        o_ref[...] += _dot_tn(a_ref[...], b_ref[...])

    return pl.pallas_call(
        body, name=name, grid=(n // bn, s // bk),
        in_specs=[pl.BlockSpec((bk, m), lambda jn, k: (k, 0)), pl.BlockSpec((bk, bn), lambda jn, k: (k, jn))],
        out_specs=pl.BlockSpec((m, bn), lambda jn, k: (0, jn)),
        out_shape=jax.ShapeDtypeStruct((m, n), F32),
        compiler_params=_params(VMEM_MID),
    )(a, b)


def _matmul_tn_multi(a, bs, bk, name):
    s, m = a.shape
    nb = len(bs)

    def body(a_ref, *refs):
        b_refs, o_refs = refs[:nb], refs[nb:]

        @pl.when(pl.program_id(0) == 0)
        def _():
            for o_ref in o_refs:
                o_ref[...] = jnp.zeros_like(o_ref)

        at = a_ref[...].T
        for b_ref, o_ref in zip(b_refs, o_refs):
            o_ref[...] += _dot(at, b_ref[...])

    return pl.pallas_call(
        body, name=name, grid=(s // bk,),
        in_specs=[pl.BlockSpec((bk, m), lambda k: (k, 0))]
        + [pl.BlockSpec((bk, b.shape[1]), lambda k: (k, 0)) for b in bs],
        out_specs=[_full((m, b.shape[1])) for b in bs],
        out_shape=[jax.ShapeDtypeStruct((m, b.shape[1]), F32) for b in bs],
        compiler_params=_params(VMEM_BIG),
    )(a, *bs)


def _flip(v, bit):
    return 1 - v if bit else v


def _all_gather(shards):
    n = len(shards)

    def body(*refs):
        start, forward, finish = _gather_phases(refs[:n], refs[n:2 * n], *refs[2 * n:])
        start()
        forward()
        finish()

    any_ = pl.BlockSpec(memory_space=pl.ANY)
    return pl.pallas_call(
        body, name="all_gather_weights",
        in_specs=[any_] * n, out_specs=[any_] * n,
        out_shape=_gathered_shapes(shards), scratch_shapes=_exchange_sems(n),
    )(*shards)


def _gathered_shapes(shards):
    return [jax.ShapeDtypeStruct((8,) + sh.shape, sh.dtype) for sh in shards]


def _exchange_sems(n):
    return [pltpu.SemaphoreType.DMA((7 * n,)), pltpu.SemaphoreType.DMA((7 * n,)), pltpu.SemaphoreType.DMA((n,))]


def _gather_phases(src, out, send_sems, recv_sems, loc_sems):
    n = len(src)
    x, y, c = lax.axis_index("x"), lax.axis_index("y"), lax.axis_index("c")
    me, sibling = (x, y, c), (x, y, 1 - c)
    chips = [(1 - x, y), (x, 1 - y), (1 - x, 1 - y)]

    def slot(a, px, py, pc):
        return out[a].at[4 * px + 2 * py + pc]

    def copy(a, k, block, to, from_src=False):
        return pltpu.make_async_remote_copy(
            src_ref=src[a] if from_src else slot(a, *block), dst_ref=slot(a, *block),
            send_sem=send_sems.at[7 * a + k], recv_sem=recv_sems.at[7 * a + k],
            device_id=to, device_id_type=MESH)

    def local(a):
        return pltpu.make_async_copy(src[a], slot(a, *me), loc_sems.at[a])

    def first(a):
        return [copy(a, 0, me, sibling, True)] + [copy(a, 1 + j, me, (*chip, c), True)
                                                  for j, chip in enumerate(chips)]

    def passed(a, j):
        return copy(a, 4 + j, (*chips[j], c), sibling)

    def start():
        for a in range(n):
            local(a).start()
            for cp in first(a):
                cp.start()

    def forward():
        for a in range(n):
            for j, chip in enumerate(chips):
                copy(a, 1 + j, (*chip, c), me).wait_recv()
                passed(a, j).start()

    def finish():
        for a in range(n):
            copy(a, 0, sibling, me).wait_recv()
            for j, chip in enumerate(chips):
                copy(a, 4 + j, (*chip, 1 - c), me).wait_recv()
        for a in range(n):
            for cp in first(a) + [passed(a, j) for j in range(3)]:
                cp.wait_send()
            local(a).wait()

    return start, forward, finish


def _scatter_phases(src, out, send_sems, recv_sems, loc_sems):
    n = len(src)
    masks = [((k >> 2) & 1, (k >> 1) & 1, k & 1) for k in range(1, 8)]
    x, y, c = lax.axis_index("x"), lax.axis_index("y"), lax.axis_index("c")
    me = 4 * x + 2 * y + c

    def copies():
        cps = []
        for a in range(n):
            cps.append(pltpu.make_async_copy(src[a].at[me], out[a].at[me], loc_sems.at[a]))
            for k, (mx, my, mc) in enumerate(masks):
                px, py, pc = _flip(x, mx), _flip(y, my), _flip(c, mc)
                cps.append(pltpu.make_async_remote_copy(
                    src_ref=src[a].at[4 * px + 2 * py + pc], dst_ref=out[a].at[me],
                    send_sem=send_sems.at[7 * a + k], recv_sem=recv_sems.at[7 * a + k],
                    device_id=(px, py, pc), device_id_type=MESH))
        return cps

    def start():
        for cp in copies():
            cp.start()

    def finish():
        for cp in copies():
            cp.wait()

    return start, finish


def _all_to_all(blocks, name):
    n = len(blocks)

    def body(*refs):
        start, finish = _scatter_phases(refs[:n], refs[n:2 * n], *refs[2 * n:])
        start()
        finish()

    any_ = pl.BlockSpec(memory_space=pl.ANY)
    return pl.pallas_call(
        body, name=name,
        in_specs=[any_] * n, out_specs=[any_] * n,
        out_shape=[jax.ShapeDtypeStruct(b.shape, b.dtype) for b in blocks], scratch_shapes=_exchange_sems(n),
    )(*blocks)


def _adamw(parts, w, m, v, br, name, tail=None):
    g8, r, c = parts.shape
    c1 = 1.0 - ADAM_B1 ** ADAM_STEP
    c2 = 1.0 - ADAM_B2 ** ADAM_STEP
    extra = [] if tail is None else [tail]

    def body(p_ref, w_ref, m_ref, v_ref, *refs):
        g_ref, d_ref, m2_ref, v2_ref = refs[len(extra):]
        g = p_ref[0].astype(F32)
        for d in range(1, g8):
            g = g + p_ref[d].astype(F32)
        if tail is not None:
            k = tail.shape[1]
            place = (_rows((k, c)) + (c - k) == _cols((k, c))).astype(F32)
            g = g + jnp.dot(refs[0][...], place, precision=HI, preferred_element_type=F32)
        m2 = ADAM_B1 * m_ref[...] + (1.0 - ADAM_B1) * g
        v2 = ADAM_B2 * v_ref[...] + (1.0 - ADAM_B2) * (g * g)
        g_ref[...] = g
        m2_ref[...] = m2
        v2_ref[...] = v2
        d_ref[...] = -ADAM_LR * ((m2 / c1) / (jnp.sqrt(v2 / c2) + ADAM_EPS) + ADAM_WD * w_ref[...])

    blk = pl.BlockSpec((br, c), lambda i: (i, 0))
    out = jax.ShapeDtypeStruct((r, c), F32)
    return pl.pallas_call(
        body, name=name, grid=(r // br,),
        in_specs=[pl.BlockSpec((g8, br, c), lambda i: (0, i, 0)), blk, blk, blk]
        + [pl.BlockSpec((br, e.shape[1]), lambda i: (i, 0)) for e in extra],
        out_specs=[blk] * 4, out_shape=[out] * 4,
        compiler_params=_params(VMEM_MID),
    )(parts, w, m, v, *extra)


def _pad_rows(a, rows):
    return jnp.pad(a, ((0, rows - a.shape[0]), (0, 0)))


_SMALL = (("g_mix", 1024), ("b_f", 8), ("g_conv_out", 512), ("g_attn_out", 512), ("g_ffn", 1024), ("g_final", 1024))
_SMALL_LEN = sum(n for _, n in _SMALL)
_SMALL_ROWS = 104


def _pack_small(vals, extra=None):
    parts = [vals[k].reshape(-1) for k, _ in _SMALL]
    if extra is not None:
        parts.append(extra.reshape(-1))
    flat = jnp.concatenate(parts)
    return jnp.pad(flat, (0, _SMALL_ROWS * LANES - flat.shape[0])).reshape(_SMALL_ROWS, LANES)


def _unpack_small(packed, shapes):
    flat = packed.reshape(-1)
    out, off = {}, 0
    for k, n in _SMALL:
        out[k] = flat[off:off + n].reshape(shapes[k])
        off += n
    return out


def _col_blocks(a):
    r, c8 = a.shape
    return jnp.transpose(a.reshape(r, 8, c8 // 8), (1, 0, 2))


def _from_col_blocks(a):
    g, r, c = a.shape
    return jnp.transpose(a, (1, 0, 2)).reshape(r, g * c)


def kernel(x, g_mix, w_in, b_f, w_conv, g_conv_out, g_attn_out, w_o, g_ffn, w_up, w_ffn_conv, w_down, g_final, loss_target, m_g_mix, m_w_in, m_b_f, m_w_conv, m_g_conv_out, m_g_attn_out, m_w_o, m_g_ffn, m_w_up, m_w_ffn_conv, m_w_down, m_g_final, v_g_mix, v_w_in, v_b_f, v_w_conv, v_g_conv_out, v_g_attn_out, v_w_o, v_g_ffn, v_w_up, v_w_ffn_conv, v_w_down, v_g_final):
    w = dict(g_mix=g_mix, w_in=w_in[0], b_f=b_f, w_conv=w_conv[0], g_conv_out=g_conv_out, g_attn_out=g_attn_out,
             w_o=w_o[0], g_ffn=g_ffn, w_up=w_up[0], w_ffn_conv=w_ffn_conv[0], w_down=w_down[0], g_final=g_final)
    m = dict(g_mix=m_g_mix, w_in=m_w_in[0], b_f=m_b_f, w_conv=m_w_conv[0], g_conv_out=m_g_conv_out,
             g_attn_out=m_g_attn_out, w_o=m_w_o[0], g_ffn=m_g_ffn, w_up=m_w_up[0], w_ffn_conv=m_w_ffn_conv[0],
             w_down=m_w_down[0], g_final=m_g_final)
    v = dict(g_mix=v_g_mix, w_in=v_w_in[0], b_f=v_b_f, w_conv=v_w_conv[0], g_conv_out=v_g_conv_out,
             g_attn_out=v_g_attn_out, w_o=v_w_o[0], g_ffn=v_g_ffn, w_up=v_w_up[0], w_ffn_conv=v_w_ffn_conv[0],
             w_down=v_w_down[0], g_final=v_g_final)
    shapes = dict(g_mix=g_mix.shape, w_in=w_in.shape, b_f=b_f.shape, w_conv=w_conv.shape,
                  g_conv_out=g_conv_out.shape, g_attn_out=g_attn_out.shape, w_o=w_o.shape, g_ffn=g_ffn.shape,
                  w_up=w_up.shape, w_ffn_conv=w_ffn_conv.shape, w_down=w_down.shape, g_final=g_final.shape)

    tm = t = 512
    tf = 256
    xs, tgt = x[0], loss_target[0]
    seq = xs.shape[0]
    assert seq % tm == 0 and seq % tf == 0
    bk = 1024 if seq % 1024 == 0 else 512

    g_in, g_conv = _all_gather([w["w_in"].astype(BF), w["w_conv"]])
    w_in_f = _from_col_blocks(g_in)
    w_zc = w_in_f[:, :3 * CONV_CH]
    w_qkv = w_in_f[:, 3 * CONV_CH:3 * CONV_CH + 3 * ATTN_W]
    w_f = jnp.pad(w_in_f[:, 3 * CONV_CH + 3 * ATTN_W:], ((0, 0), (0, LANES - N_HEADS)))
    b_f_p = jnp.pad(b_f, ((0, 0), (0, LANES - N_HEADS)))
    w_conv_p = _pad_rows(_from_col_blocks(g_conv), 8)

    h1, zc, qkv, fpre, fcol, nc, qa, ka, qt, vt = _fwd_in(
        xs, g_mix, w_zc, w_qkv, w_f, b_f_p, w_conv_p, g_conv_out, tm)
    o, lcol, g_o, g_up, g_dn, g_fc = _attn_fwd(
        qa, ka, vt, t, [w["w_o"].astype(BF), w["w_up"].astype(BF), w["w_down"].astype(BF), w["w_ffn_conv"]])
    w_o_f = g_o.reshape(D_MODEL, D_MODEL)
    w_up_f = _from_col_blocks(g_up)
    w_dn_f = g_dn.reshape(D_FF, D_MODEL)
    w_fc_p = _pad_rows(_from_col_blocks(g_fc), 8)
    x2, h2, mix, u0, uc, act, dx3, dx3b, loss, dg_final = _fwd_ffn(
        xs, nc, o, tgt, g_attn_out, w_o_f, g_ffn, w_up_f, w_fc_p, w_dn_f, g_final.reshape(1, D_MODEL), tf)
    loss_part = loss

    du0, dx2, dx2b, dw_fc, dg_ffn = _bwd_ffn(dx3, dx3b, u0, uc, x2, w_up_f.T, w_fc_p, w_dn_f.T, g_ffn, tf)
    dw_up = _matmul_tn(h2, du0, 1408, 2 * bk if seq % (2 * bk) == 0 else bk, "dw_up")
    dw_dn, = _matmul_tn_multi(act, [dx3b], bk, "dw_down")
    dw_o, = _matmul_tn_multi(mix, [dx2b], bk, "dw_o")
    dzc, doa, dot, qab, dw_conv, dg_attn, dg_conv = _bwd_out(
        dx2b, o, zc, fcol, lcol, qa, w_o_f, g_attn_out, g_conv_out, w_conv_p, t)
    dk, dv, dfk, dqa, dq, r_up, r_dn, r_fc, r_o = _attn_bwd(
        qab, doa, ka, qkv, qt, dot, t,
        [_col_blocks(dw_up), dw_dn.reshape(8, D_FF // 8, D_MODEL), _col_blocks(dw_fc[:3]),
         dw_o.reshape(8, D_MODEL // 8, D_MODEL)])
    dw_zc, dw_q, dw_k, dw_v = _matmul_tn_multi(h1, [dzc, dq, dk, dv], bk, "dw_in")
    dw_main = jnp.concatenate([dw_zc, dw_q, dw_k, dw_v, jnp.zeros((D_MODEL, N_HEADS), F32)], axis=1)
    gx, dfb, dg_mix, db_f, r_in = _bwd_in(xs, dx2, dzc, dq, dqa, dk, dv, dfk, fpre, w_zc, w_qkv, w_f, g_mix, tm,
                                          [_col_blocks(dw_main).astype(BF)])
    dw_f = _matmul_tn(h1, dfb, LANES, bk, "dw_in_f")
    small = dict(g_mix=dg_mix, b_f=db_f[:, :N_HEADS], g_conv_out=dg_conv, g_attn_out=dg_attn, g_ffn=dg_ffn,
                 g_final=dg_final)
    riders = jnp.concatenate([loss_part.reshape(-1), dw_f[:, :N_HEADS].reshape(-1)])
    r_conv, r_small = _all_to_all(
        [_col_blocks(dw_conv[:3]), jnp.broadcast_to(_pack_small(small, riders), (8, _SMALL_ROWS, LANES))],
        "all_to_all_grads")

    g_out, d_out, m_out, v_out = {}, {}, {}, {}
    res = _adamw(r_small, _pack_small(w), _pack_small(m), _pack_small(v), _SMALL_ROWS, "adamw_gains")
    for dst, packed in zip((g_out, d_out, m_out, v_out), res):
        dst.update(_unpack_small(packed, shapes))
    summed = res[0].reshape(-1)
    loss = summed[_SMALL_LEN]
    is_last = (4 * lax.axis_index("x") + 2 * lax.axis_index("y") + lax.axis_index("c") == 7).astype(F32)
    dw_f_sum = summed[_SMALL_LEN + 1:_SMALL_LEN + 1 + D_MODEL * N_HEADS].reshape(D_MODEL, N_HEADS) * is_last

    big = ("w_in", "w_o", "w_up", "w_down", "w_conv", "w_ffn_conv")
    recv = [r_in, r_o, r_up, r_dn, r_conv, r_fc]
    rows = dict(w_in=256, w_o=128, w_up=256, w_down=176, w_conv=3, w_ffn_conv=3)
    for a, k in enumerate(big):
        res = _adamw(recv[a], w[k], m[k], v[k], rows[k], "adamw_" + k, tail=dw_f_sum if k == "w_in" else None)
        g_out[k], d_out[k], m_out[k], v_out[k] = [r.reshape(shapes[k]) for r in res]

    order = ("g_mix", "w_in", "b_f", "w_conv", "g_conv_out", "g_attn_out", "w_o", "g_ffn", "w_up", "w_ffn_conv",
             "w_down", "g_final")
    return (loss, gx.reshape(x.shape), *[g_out[k] for k in order], *[d_out[k] for k in order],
            *[m_out[k] for k in order], *[v_out[k] for k in order])
```

```python
import jax
import jax.numpy as jnp
import numpy as np
from jax import lax
from jax.experimental import pallas as pl
from jax.experimental.pallas import tpu as pltpu

F32 = jnp.float32
BF = jnp.bfloat16
HI = lax.Precision.HIGHEST
MESH = pl.DeviceIdType.MESH

D_MODEL = 1024
CONV_CH = 512
ATTN_W = 512
N_HEADS = 8
HEAD_DIM = 64
D_FF = 2816
FF_CHUNK = 256
N_FF_CHUNKS = D_FF // FF_CHUNK
EPS = 1e-6
NEG = -1e30
LANES = 128
VMEM_BIG = 56 * 1024 * 1024
VMEM_MID = 40 * 1024 * 1024

ADAM_LR = 0.001
ADAM_B1 = 0.9
ADAM_B2 = 0.999
ADAM_EPS = 1e-08
ADAM_WD = 0.01
ADAM_STEP = 10

NT = (((1,), (1,)), ((), ()))
TN = (((0,), (0,)), ((), ()))


def _dot(a, b):
    return jnp.dot(a, b, preferred_element_type=F32)


def _dot_nt(a, b):
    return lax.dot_general(a, b, NT, preferred_element_type=F32)


def _dot_tn(a, b):
    return lax.dot_general(a, b, TN, preferred_element_type=F32)


def _params(vmem):
    return pltpu.CompilerParams(vmem_limit_bytes=vmem)


def _rows(shape):
    return lax.broadcasted_iota(jnp.int32, shape, 0)


def _cols(shape):
    return lax.broadcasted_iota(jnp.int32, shape, 1)


def _shift_down(u, prev, k):
    n = prev.shape[0]
    out = pltpu.roll(u, k, 0)
    row = _rows(u.shape)
    for r in range(k):
        out = jnp.where(row == r, prev[n - k + r:n - k + r + 1, :].astype(u.dtype), out)
    return out


def _shift_up(u, nxt, k):
    tm = u.shape[0]
    out = pltpu.roll(u, tm - k, 0)
    row = _rows(u.shape)
    for r in range(k):
        out = jnp.where(row == tm - k + r, nxt[r:r + 1, :], out)
    return out


def _conv3(u, prev, w):
    u1 = _shift_down(u, prev, 1)
    u2 = _shift_down(u, prev, 2)
    return w[0:1, :] * u2 + w[1:2, :] * u1 + w[2:3, :] * u, u1, u2


def _conv3_bwd(d, nxt, w):
    return w[2:3, :] * d + w[1:2, :] * _shift_up(d, nxt, 1) + w[0:1, :] * _shift_up(d, nxt, 2)


def _rms(x):
    return lax.rsqrt(jnp.mean(x * x, axis=-1, keepdims=True) + EPS)


def _rms_bwd(x, r, dyg):
    return r * dyg - x * (r * r * r) * jnp.mean(dyg * x, axis=-1, keepdims=True)


def _full(shape):
    nd = len(shape)
    return pl.BlockSpec(shape, lambda i, _n=nd: (0,) * _n)


ONES_LANE = 24


def _bias_scatter():
    sc = np.zeros((LANES, 2 * N_HEADS * LANES), np.float32)
    koff = N_HEADS * LANES
    for h in range(N_HEADS):
        aux = HEAD_DIM * (1 - h % 2)
        for j in range(3):
            sc[8 * j + h, LANES * h + aux + j] = 1.0
            sc[ONES_LANE, koff + LANES * h + aux + j] = 1.0
            sc[ONES_LANE, LANES * h + aux + 3 + j] = 1.0
            sc[8 * j + h, koff + LANES * h + aux + 3 + j] = -1.0
    return jnp.asarray(sc, BF)


def _split_parts(v, one):
    hi = v.astype(BF).astype(F32)
    rest = v - hi
    mid = rest.astype(BF).astype(F32)
    lo = (rest - mid).astype(BF).astype(F32)
    parts = hi + pltpu.roll(mid, 8, 1) + pltpu.roll(lo, 16, 1)
    return jnp.where(_cols(parts.shape) == ONES_LANE, one, parts).astype(BF)


def _fwd_in(x, g_mix, w_zc, w_qkv, w_f, b_f, w_conv, g_conv_out, tm):
    s = x.shape[0]
    nb = s // tm
    aw = N_HEADS * LANES

    def body(x_ref, gm_ref, wzc_ref, wqkv_ref, wf_ref, bf_ref, wc_ref, gco_ref, sc_ref,
             h1_ref, zc_ref, qkv_ref, fpre_ref, fcol_ref, nc_ref, qa_ref, ka_ref, qt_ref, vt_ref, cu_ref, cf_ref):
        i = pl.program_id(0)

        @pl.when(i == 0)
        def _():
            cu_ref[...] = jnp.zeros_like(cu_ref)
            cf_ref[...] = jnp.zeros_like(cf_ref)

        xv = x_ref[...]
        hb = (xv * _rms(xv) * gm_ref[...]).astype(BF)
        h1_ref[...] = hb
        zc = _dot(hb, wzc_ref[...])
        zc_ref[...] = zc
        qkv = _dot(hb, wqkv_ref[...])
        qkv = jnp.where(_cols(qkv.shape) < ATTN_W, qkv * 0.125, qkv)
        qkvb = qkv.astype(BF)
        qkv_ref[...] = qkvb
        qt_ref[0] = qkv[:, :ATTN_W].T.astype(BF)
        vt_ref[0] = qkv[:, 2 * ATTN_W:].T.astype(BF)

        gb, gc, xc = zc[:, :CONV_CH], zc[:, CONV_CH:2 * CONV_CH], zc[:, 2 * CONV_CH:]
        u = gc * xc
        cv, _, _ = _conv3(u, cu_ref[...], wc_ref[...])
        cu_ref[...] = u[tm - 8:, :]
        y = gb * cv
        nc_ref[...] = (y * _rms(y) * gco_ref[...]).astype(BF)

        fpre = _dot(hb, wf_ref[...]) + bf_ref[...]
        fpre_ref[...] = fpre
        logf = jnp.minimum(fpre, 0.0) - jnp.log1p(jnp.exp(-jnp.abs(fpre)))
        logf = jnp.where(_cols(logf.shape) < N_HEADS, logf, 0.0)
        tri = (_rows((LANES, LANES)) >= _cols((LANES, LANES))).astype(F32)
        offset = cf_ref[...]
        pieces = []
        for r in range(tm // LANES):
            piece = jnp.dot(tri, logf[r * LANES:(r + 1) * LANES, :], precision=HI, preferred_element_type=F32) + offset
            offset = piece[LANES - 1:LANES, :]
            pieces.append(piece)
        fcol = jnp.concatenate(pieces, axis=0)
        cf_ref[...] = offset
        fcol_ref[...] = fcol

        feat = _dot(_split_parts(fcol, 1.0), sc_ref[...]).astype(BF)
        lane = _cols((tm, LANES))
        for h in range(N_HEADS):
            hp, hh = divmod(h, 2)
            own = (lane >= HEAD_DIM * hh) & (lane < HEAD_DIM * (hh + 1))
            hs = slice(LANES * h, LANES * (h + 1))
            qa_ref[:, hs] = jnp.where(own, qkvb[:, LANES * hp:LANES * (hp + 1)], feat[:, hs])
            ka_ref[:, hs] = jnp.where(own, qkvb[:, ATTN_W + LANES * hp:ATTN_W + LANES * (hp + 1)],
                                      feat[:, aw + LANES * h:aw + LANES * (h + 1)])

    blk = lambda c: pl.BlockSpec((tm, c), lambda i: (i, 0))
    return pl.pallas_call(
        body, name="fwd_in", grid=(nb,),
        in_specs=[blk(D_MODEL), _full((1, D_MODEL)), _full(w_zc.shape), _full(w_qkv.shape), _full(w_f.shape),
                  _full((1, LANES)), _full((8, CONV_CH)), _full((1, CONV_CH)), _full((LANES, 2 * aw))],
        out_specs=[blk(D_MODEL), blk(3 * CONV_CH), blk(3 * ATTN_W), blk(LANES), blk(LANES),
                   blk(CONV_CH), blk(aw), blk(aw)] + [pl.BlockSpec((1, ATTN_W, tm), lambda i: (i, 0, 0))] * 2,
        out_shape=[jax.ShapeDtypeStruct((s, D_MODEL), BF), jax.ShapeDtypeStruct((s, 3 * CONV_CH), F32),
                   jax.ShapeDtypeStruct((s, 3 * ATTN_W), BF), jax.ShapeDtypeStruct((s, LANES), F32),
                   jax.ShapeDtypeStruct((s, LANES), F32),
                   jax.ShapeDtypeStruct((s, CONV_CH), BF), jax.ShapeDtypeStruct((s, aw), BF),
                   jax.ShapeDtypeStruct((s, aw), BF)] + [jax.ShapeDtypeStruct((nb, ATTN_W, tm), BF)] * 2,
        scratch_shapes=[pltpu.VMEM((8, CONV_CH), F32), pltpu.VMEM((1, LANES), F32)],
        compiler_params=_params(VMEM_MID),
    )(x, g_mix, w_zc, w_qkv, w_f, b_f, w_conv, g_conv_out, _bias_scatter())


def _pipeline_masked_last(last, produce, consume, buf_a, buf_b):
    produce(0, buf_a)

    def two_blocks(j, carry):
        blk = 2 * j
        produce(blk + 1, buf_b)
        consume(blk, buf_a, False)
        produce(blk + 2, buf_a)
        consume(blk + 1, buf_b, False)
        return carry

    lax.fori_loop(0, last // 2, two_blocks, 0)

    @pl.when(last % 2 == 0)
    def _():
        consume(last, buf_a, True)

    @pl.when(last % 2 == 1)
    def _():
        produce(last, buf_b)
        consume(last - 1, buf_a, False)
        consume(last, buf_b, True)


def _attn_fwd(qa, ka, vt, t, shards):
    s = qa.shape[0]
    nb = s // t
    n = len(shards)

    def body(qa_ref, ka_any, vt_any, *refs):
        shard_refs, (o_ref, l_ref), gathered = refs[:n], refs[n:n + 2], refs[n + 2:2 * n + 2]
        ka_scr, vt_scr, m_scr, acc_scr, ea_scr, eb_scr, sem = refs[2 * n + 2:2 * n + 9]
        i = pl.program_id(0)
        start, forward, finish = _gather_phases(shard_refs, gathered, *refs[2 * n + 9:])
        pl.when(i == 0)(start)
        pl.when(i == nb // 2)(forward)
        _load_weights(i, [(ka_any, ka_scr), (vt_any, vt_scr)], sem)

        causal_t = _rows((t, t)) <= _cols((t, t))
        row = _rows((LANES, t))
        lrows = jnp.zeros((LANES, t), F32)
        for hp in range(N_HEADS // 2):
            ps = slice(hp * LANES, (hp + 1) * LANES)
            m_scr[...] = jnp.full(m_scr.shape, NEG, F32)
            acc_scr[...] = jnp.zeros(acc_scr.shape, F32)
            heads = [dict(hh=hh, hs=slice((2 * hp + hh) * LANES, (2 * hp + hh + 1) * LANES),
                          aux=HEAD_DIM * (1 - hh), own=(row >= HEAD_DIM * hh) & (row < HEAD_DIM * (hh + 1)),
                          fill=jnp.where(row == HEAD_DIM * (1 - hh), 1.0, 0.0).astype(BF),
                          qh=qa_ref[:, (2 * hp + hh) * LANES:(2 * hp + hh + 1) * LANES]) for hh in range(2)]

            def scores(kb, dst, heads=heads):
                rs = pl.ds(pl.multiple_of(kb * t, t), t)
                for hd in heads:
                    dst[hd["hh"]] = _dot_nt(ka_scr[rs, hd["hs"]], hd["qh"])

            def consume(kb, src, masked, ps=ps, heads=heads):
                vt2 = vt_scr[kb, ps, :]
                m_olds = [m_scr[hd["hh"]:hd["hh"] + 1, :] for hd in heads]
                accs = [acc_scr[hd["hh"]] for hd in heads]
                m_news, acc_news = [], []
                for hd, m_old, acc in zip(heads, m_olds, accs):
                    e = src[hd["hh"]]
                    if masked:
                        e = jnp.where(causal_t, e, NEG)
                    m_new = jnp.maximum(m_old, jnp.max(e, axis=0, keepdims=True))
                    p = jnp.exp(e - m_new).astype(BF)
                    vta = jnp.where(hd["own"], vt2, hd["fill"])
                    acc_news.append(jnp.exp(m_old - m_new) * acc + _dot(vta, p))
                    m_news.append(m_new)
                for hd, m_new, acc in zip(heads, m_news, acc_news):
                    acc_scr[hd["hh"]] = acc
                    m_scr[hd["hh"]:hd["hh"] + 1, :] = m_new

            _pipeline_masked_last(i, scores, consume, ea_scr, eb_scr)

            o_pair = None
            for hd in heads:
                hh = hd["hh"]
                acc = acc_scr[hh]
                denom = acc[hd["aux"]:hd["aux"] + 1, :]
                o_h = acc / denom
                lrows = jnp.where(row == 2 * hp + hh, m_scr[hh:hh + 1, :] + jnp.log(denom), lrows)
                o_pair = o_h if hh == 0 else jnp.where(row < HEAD_DIM, o_pair, o_h)
            o_ref[:, ps] = o_pair.T
        l_ref[...] = lrows.T
        pl.when(i == nb - 1)(finish)

    any_ = pl.BlockSpec(memory_space=pl.ANY)
    return pl.pallas_call(
        body, name="attn_fwd", grid=(nb,),
        in_specs=[pl.BlockSpec((t, N_HEADS * LANES), lambda i: (i, 0)), any_, any_] + [any_] * n,
        out_specs=[pl.BlockSpec((t, ATTN_W), lambda i: (i, 0)), pl.BlockSpec((t, LANES), lambda i: (i, 0))]
        + [any_] * n,
        out_shape=[jax.ShapeDtypeStruct((s, ATTN_W), F32), jax.ShapeDtypeStruct((s, LANES), F32)]
        + _gathered_shapes(shards),
        scratch_shapes=[pltpu.VMEM(ka.shape, BF), pltpu.VMEM(vt.shape, BF), pltpu.VMEM((2, t), F32),
                        pltpu.VMEM((2, LANES, t), F32), pltpu.VMEM((2, t, t), F32), pltpu.VMEM((2, t, t), F32),
                        pltpu.SemaphoreType.DMA((2,))] + _exchange_sems(n),
        compiler_params=_params(VMEM_BIG),
    )(qa, ka, vt, *shards)


def _load_weights(i, pairs, sem):
    @pl.when(i == 0)
    def _():
        cps = [pltpu.make_async_copy(src, dst, sem.at[n]) for n, (src, dst) in enumerate(pairs)]
        for cp in cps:
            cp.start()
        for cp in cps:
            cp.wait()


def _ff_cols(j):
    return (slice(j * FF_CHUNK, (j + 1) * FF_CHUNK), slice(D_FF + j * FF_CHUNK, D_FF + (j + 1) * FF_CHUNK))


def _fwd_ffn(x, nc, o, tgt, g_attn_out, w_o, g_ffn, w_up, w_ffn_conv, w_dn, g_final, tm):
    s = x.shape[0]

    def body(x_ref, nc_ref, o_ref, tgt_ref, ga_ref, gf_ref, wfc_ref, gfin_ref, wo_any, wup_any, wdn_any,
             x2_ref, h2_ref, mix_ref, u0_ref, uc_ref, act_ref, dx3_ref, dx3b_ref, loss_ref, dgfin_ref,
             wo, wup, wdn, carry, sem):
        i = pl.program_id(0)
        _load_weights(i, [(wo_any, wo), (wup_any, wup), (wdn_any, wdn)], sem)

        @pl.when(i == 0)
        def _():
            carry[...] = jnp.zeros_like(carry)
            loss_ref[...] = jnp.zeros_like(loss_ref)
            dgfin_ref[...] = jnp.zeros_like(dgfin_ref)

        ov = o_ref[...]
        na = (ov * _rms(ov) * ga_ref[...]).astype(BF)
        ncv = nc_ref[...]
        mix_ref[:, :CONV_CH] = ncv
        mix_ref[:, CONV_CH:] = na
        x2 = x_ref[...] + _dot(ncv, wo[:CONV_CH, :]) + _dot(na, wo[CONV_CH:, :])
        x2_ref[...] = x2
        hb = (x2 * _rms(x2) * gf_ref[...]).astype(BF)
        h2_ref[...] = hb

        def up(j):
            return [_dot(hb, wup[:, cc]) for cc in _ff_cols(j)]

        nxt = up(0)
        down = None
        for j in range(N_FF_CHUNKS):
            cur = nxt
            if j + 1 < N_FF_CHUNKS:
                nxt = up(j + 1)
            parts = []
            for cc, u0 in zip(_ff_cols(j), cur):
                u0_ref[:, cc] = u0.astype(BF)
                uu, _, _ = _conv3(u0, carry[:, cc], wfc_ref[:, cc])
                carry[:, cc] = u0[tm - 8:, :]
                uc_ref[:, cc] = uu.astype(BF)
                parts.append(uu)
            ua, ug = parts
            act = (ug * jax.nn.sigmoid(ug) * ua).astype(BF)
            ca = _ff_cols(j)[0]
            act_ref[:, ca] = act
            part = _dot(act, wdn[ca, :])
            down = part if down is None else down + part

        x3 = x2_ref[...] + down
        r3 = _rms(x3)
        gfin = gfin_ref[...]
        xn = x3 * r3
        diff = xn * gfin - tgt_ref[...]
        loss_ref[...] += jnp.sum(jnp.sum(diff * diff, axis=-1, keepdims=True), axis=0, keepdims=True) * (0.5 / D_MODEL)
        dy = diff * (1.0 / D_MODEL)
        dgfin_ref[...] += jnp.sum(dy * xn, axis=0, keepdims=True)
        dx3 = _rms_bwd(x3, r3, dy * gfin)
        dx3_ref[...] = dx3
        dx3b_ref[...] = dx3.astype(BF)

    blk = lambda c: pl.BlockSpec((tm, c), lambda i: (i, 0))
    any_ = pl.BlockSpec(memory_space=pl.ANY)
    return pl.pallas_call(
        body, name="fwd_ffn", grid=(s // tm,),
        in_specs=[blk(D_MODEL), blk(CONV_CH), blk(ATTN_W), blk(D_MODEL), _full((1, ATTN_W)), _full((1, D_MODEL)),
                  _full((8, 2 * D_FF)), _full((1, D_MODEL)), any_, any_, any_],
        out_specs=[blk(D_MODEL), blk(D_MODEL), blk(D_MODEL), blk(2 * D_FF), blk(2 * D_FF), blk(D_FF), blk(D_MODEL),
                   blk(D_MODEL), _full((1, 1)), _full((1, D_MODEL))],
        out_shape=[jax.ShapeDtypeStruct((s, D_MODEL), F32), jax.ShapeDtypeStruct((s, D_MODEL), BF),
                   jax.ShapeDtypeStruct((s, D_MODEL), BF),
                   jax.ShapeDtypeStruct((s, 2 * D_FF), BF), jax.ShapeDtypeStruct((s, 2 * D_FF), BF),
                   jax.ShapeDtypeStruct((s, D_FF), BF),
                   jax.ShapeDtypeStruct((s, D_MODEL), F32), jax.ShapeDtypeStruct((s, D_MODEL), BF),
                   jax.ShapeDtypeStruct((1, 1), F32), jax.ShapeDtypeStruct((1, D_MODEL), F32)],
        scratch_shapes=[pltpu.VMEM(w_o.shape, BF), pltpu.VMEM(w_up.shape, BF), pltpu.VMEM(w_dn.shape, BF),
                        pltpu.VMEM((8, 2 * D_FF), F32), pltpu.SemaphoreType.DMA((3,))],
        compiler_params=_params(VMEM_BIG),
    )(x, nc, o, tgt, g_attn_out, g_ffn, w_ffn_conv, g_final, w_o, w_up, w_dn)


def _bwd_ffn(dx3, dx3b, u0, uc, x2, w_up, w_ffn_conv, w_dn, g_ffn, tm):
    s = x2.shape[0]
    nb = s // tm

    def body(dx3_ref, dx3b_ref, u0_ref, uc_ref, x2_ref, wfc_ref, gf_ref, wup_any, wdn_any,
             du0_ref, dx2_ref, dx2b_ref, dwfc_ref, dgf_ref,
             wup, wdn, carry, sem):
        i = pl.program_id(0)
        _load_weights(i, [(wup_any, wup), (wdn_any, wdn)], sem)

        @pl.when(i == 0)
        def _():
            carry[...] = jnp.zeros_like(carry)
            dwfc_ref[...] = jnp.zeros_like(dwfc_ref)
            dgf_ref[...] = jnp.zeros_like(dgf_ref)

        db = dx3b_ref[...]

        def dact_of(j):
            return _dot(db, wdn[:, _ff_cols(j)[0]])

        nxt = dact_of(0)
        dh2 = None
        for j in range(N_FF_CHUNKS):
            ca, cg = _ff_cols(j)
            dact = nxt
            if j + 1 < N_FF_CHUNKS:
                nxt = dact_of(j + 1)
            ua = uc_ref[:, ca].astype(F32)
            ug = uc_ref[:, cg].astype(F32)
            sg = jax.nn.sigmoid(ug)
            da = dact * (ug * sg)
            dg = dact * ua * (sg * (1.0 + ug * (1.0 - sg)))
            for cc, d in ((ca, da), (cg, dg)):
                nxt_rows = carry[:, cc]
                d1 = _shift_up(d, nxt_rows, 1)
                d2 = _shift_up(d, nxt_rows, 2)
                u0c = u0_ref[:, cc].astype(F32)
                w = wfc_ref[:, cc]
                dwfc_ref[0:1, cc] += jnp.sum(d2 * u0c, axis=0, keepdims=True)
                dwfc_ref[1:2, cc] += jnp.sum(d1 * u0c, axis=0, keepdims=True)
                dwfc_ref[2:3, cc] += jnp.sum(d * u0c, axis=0, keepdims=True)
                du0 = (w[2:3, :] * d + w[1:2, :] * d1 + w[0:1, :] * d2).astype(BF)
                carry[:, cc] = d[:8, :]
                du0_ref[:, cc] = du0
                part = _dot(du0, wup[cc, :])
                dh2 = part if dh2 is None else dh2 + part

        x2v = x2_ref[...]
        r2 = _rms(x2v)
        dgf_ref[...] += jnp.sum(dh2 * (x2v * r2), axis=0, keepdims=True)
        dx2 = dx3_ref[...] + _rms_bwd(x2v, r2, dh2 * gf_ref[...])
        dx2_ref[...] = dx2
        dx2b_ref[...] = dx2.astype(BF)

    blk = lambda c: pl.BlockSpec((tm, c), lambda i: (nb - 1 - i, 0))
    any_ = pl.BlockSpec(memory_space=pl.ANY)
    return pl.pallas_call(
        body, name="bwd_ffn", grid=(nb,),
        in_specs=[blk(D_MODEL), blk(D_MODEL), blk(2 * D_FF), blk(2 * D_FF), blk(D_MODEL), _full((8, 2 * D_FF)),
                  _full((1, D_MODEL)), any_, any_],
        out_specs=[blk(2 * D_FF), blk(D_MODEL), blk(D_MODEL), _full((8, 2 * D_FF)), _full((1, D_MODEL))],
        out_shape=[jax.ShapeDtypeStruct((s, 2 * D_FF), BF), jax.ShapeDtypeStruct((s, D_MODEL), F32),
                   jax.ShapeDtypeStruct((s, D_MODEL), BF), jax.ShapeDtypeStruct((8, 2 * D_FF), F32),
                   jax.ShapeDtypeStruct((1, D_MODEL), F32)],
        scratch_shapes=[pltpu.VMEM(w_up.shape, BF), pltpu.VMEM(w_dn.shape, BF), pltpu.VMEM((8, 2 * D_FF), F32),
                        pltpu.SemaphoreType.DMA((2,))],
        compiler_params=_params(VMEM_BIG),
    )(dx3, dx3b, u0, uc, x2, w_ffn_conv, g_ffn, w_up, w_dn)


def _bwd_out(dx2b, o, zc, fcol, lcol, qa, w_o, g_attn_out, g_conv_out, w_conv, tm):
    s = o.shape[0]
    nb = s // tm
    aw = N_HEADS * LANES

    def body(dx2b_ref, o_ref, zc_ref, halo_ref, fcol_ref, lcol_ref, qa_ref, wo_ref, ga_ref, gco_ref, wc_ref, sc_ref,
             dzc_ref, doa_ref, dot_ref, qab_ref, dwc_ref, dga_ref, dgco_ref, carry):
        i = pl.program_id(0)
        rb = nb - 1 - i

        @pl.when(i == 0)
        def _():
            carry[...] = jnp.zeros_like(carry)
            dwc_ref[...] = jnp.zeros_like(dwc_ref)
            dga_ref[...] = jnp.zeros_like(dga_ref)
            dgco_ref[...] = jnp.zeros_like(dgco_ref)

        dmix = _dot_nt(dx2b_ref[...], wo_ref[...])
        dnc, dna = dmix[:, :CONV_CH], dmix[:, CONV_CH:]

        ov = o_ref[...]
        ra = _rms(ov)
        dga_ref[...] += jnp.sum(dna * (ov * ra), axis=0, keepdims=True)
        do = _rms_bwd(ov, ra, dna * ga_ref[...])
        dob = do.astype(BF)
        dot_ref[0] = do.T.astype(BF)
        sel = (_rows((ATTN_W, LANES)) // HEAD_DIM == _cols((ATTN_W, LANES))).astype(F32)
        delta = jnp.dot(do * ov, sel, precision=lax.Precision.HIGH, preferred_element_type=F32)
        featc = _dot(_split_parts(fcol_ref[...] - lcol_ref[...], 1.0), sc_ref[...]).astype(BF)
        featd = _dot(_split_parts(-delta, 0.0), sc_ref[...]).astype(BF)
        lane = _cols((tm, LANES))
        for h in range(N_HEADS):
            hp, hh = divmod(h, 2)
            own = (lane >= HEAD_DIM * hh) & (lane < HEAD_DIM * (hh + 1))
            hs = slice(LANES * h, LANES * (h + 1))
            qab_ref[:, hs] = jnp.where(own, qa_ref[:, hs], featc[:, hs])
            doa_ref[:, hs] = jnp.where(own, dob[:, LANES * hp:LANES * (hp + 1)], featd[:, hs])

        zc_v = zc_ref[...]
        gb, gc, xc = zc_v[:, :CONV_CH], zc_v[:, CONV_CH:2 * CONV_CH], zc_v[:, 2 * CONV_CH:]
        hal = halo_ref[...] * (rb > 0).astype(F32)
        u = gc * xc
        prev = hal[:, CONV_CH:2 * CONV_CH] * hal[:, 2 * CONV_CH:]
        wc = wc_ref[...]
        cv, u1, u2 = _conv3(u, prev, wc)
        y = gb * cv
        rc = _rms(y)
        dgco_ref[...] += jnp.sum(dnc * (y * rc), axis=0, keepdims=True)
        dy = _rms_bwd(y, rc, dnc * gco_ref[...])
        dcv = dy * gb
        dwc_ref[0:1, :] += jnp.sum(dcv * u2, axis=0, keepdims=True)
        dwc_ref[1:2, :] += jnp.sum(dcv * u1, axis=0, keepdims=True)
        dwc_ref[2:3, :] += jnp.sum(dcv * u, axis=0, keepdims=True)
        du = _conv3_bwd(dcv, carry[...], wc)
        carry[...] = dcv[:8, :]
        dzc_ref[:, :CONV_CH] = (dy * cv).astype(BF)
        dzc_ref[:, CONV_CH:2 * CONV_CH] = (du * xc).astype(BF)
        dzc_ref[:, 2 * CONV_CH:] = (du * gc).astype(BF)

    blk = lambda c: pl.BlockSpec((tm, c), lambda i: (nb - 1 - i, 0))
    halo = pl.BlockSpec((8, 3 * CONV_CH), lambda i: (jnp.maximum((nb - 1 - i) * (tm // 8) - 1, 0), 0))
    tr = pl.BlockSpec((1, ATTN_W, tm), lambda i: (nb - 1 - i, 0, 0))
    return pl.pallas_call(
        body, name="bwd_out", grid=(nb,),
        in_specs=[blk(D_MODEL), blk(ATTN_W), blk(3 * CONV_CH), halo, blk(LANES), blk(LANES), blk(aw),
                  _full(w_o.shape), _full((1, ATTN_W)), _full((1, CONV_CH)), _full((8, CONV_CH)), _full((LANES, aw))],
        out_specs=[blk(3 * CONV_CH), blk(aw), tr, blk(aw),
                   _full((8, CONV_CH)), _full((1, ATTN_W)), _full((1, CONV_CH))],
        out_shape=[jax.ShapeDtypeStruct((s, 3 * CONV_CH), BF),
                   jax.ShapeDtypeStruct((s, aw), BF), jax.ShapeDtypeStruct((nb, ATTN_W, tm), BF),
                   jax.ShapeDtypeStruct((s, aw), BF),
                   jax.ShapeDtypeStruct((8, CONV_CH), F32), jax.ShapeDtypeStruct((1, ATTN_W), F32),
                   jax.ShapeDtypeStruct((1, CONV_CH), F32)],
        scratch_shapes=[pltpu.VMEM((8, CONV_CH), F32)],
        compiler_params=_params(VMEM_MID),
    )(dx2b, o, zc, zc, fcol, lcol, qa, w_o, g_attn_out, g_conv_out, w_conv, _bias_scatter()[:, :aw])


def _attn_bwd(qab, doa, ka, qkv, qt, dot, t, blocks):
    s = qab.shape[0]
    nb = s // t
    npair = N_HEADS // 2
    n = len(blocks)

    def body(ka_ref, v_ref, qab_ref, doa_ref, qt_ref, dot_ref, *refs):
        block_refs, (dk_ref, dv_ref, dfk_ref, dqa_ref, dq_ref) = refs[:n], refs[n:n + 5]
        received = refs[n + 5:2 * n + 5]
        acck, accv = refs[2 * n + 5:2 * n + 7]
        hp = pl.program_id(0)
        j = pl.program_id(1)
        start, finish = _scatter_phases(block_refs, received, *refs[2 * n + 7:])
        pl.when((hp == 0) & (j == 0))(start)

        @pl.when(j == 0)
        def _():
            dqa_ref[...] = jnp.zeros_like(dqa_ref)

        causal = _cols((t, t)) <= _rows((t, t))
        row = _rows((LANES, t))
        lane = _cols((t, LANES))
        acck[...] = jnp.zeros(acck.shape, F32)
        accv[...] = jnp.zeros(accv.shape, F32)
        v2 = v_ref[...]
        heads = []
        for hh in range(2):
            aux = HEAD_DIM * (1 - hh)
            ownl = (lane >= HEAD_DIM * hh) & (lane < HEAD_DIM * (hh + 1))
            ones3 = jnp.where((lane >= aux) & (lane < aux + 3), 1.0, 0.0).astype(BF)
            heads.append(dict(hh=hh, hs=slice(hh * LANES, (hh + 1) * LANES), aux=aux,
                              own=(row >= HEAD_DIM * hh) & (row < HEAD_DIM * (hh + 1)),
                              fill=jnp.where(row == aux, 1.0, 0.0).astype(BF),
                              kh=ka_ref[:, hh * LANES:(hh + 1) * LANES], vh=jnp.where(ownl, v2, ones3)))

        def tile(qb, masked):
            rs = pl.ds(pl.multiple_of(qb * t, t), t)
            qt2 = qt_ref[qb]
            dot2 = dot_ref[qb]
            es = [_dot_nt(qab_ref[rs, hd["hs"]], hd["kh"]) for hd in heads]
            dps = [_dot_nt(doa_ref[rs, hd["hs"]], hd["vh"]) for hd in heads]
            aks = [acck[hd["hh"]] for hd in heads]
            avs = [accv[hd["hh"]] for hd in heads]
            dqs = [dqa_ref[rs, hd["hs"]] for hd in heads]
            outs = []
            for hd, e, dp, ak, av, dq in zip(heads, es, dps, aks, avs, dqs):
                if masked:
                    e = jnp.where(causal, e, NEG)
                p = jnp.exp(e)
                ds = (p * dp).astype(BF)
                outs.append((ak + _dot(jnp.where(hd["own"], qt2, hd["fill"]), ds), av + _dot(dot2, p.astype(BF)),
                             dq + _dot(ds, hd["kh"])))
            for hd, (ak, av, dq) in zip(heads, outs):
                acck[hd["hh"]] = ak
                accv[hd["hh"]] = av
                dqa_ref[rs, hd["hs"]] = dq

        def loop_body(qb, carry):
            tile(qb, False)
            return carry

        tile(j, True)
        lax.fori_loop(j + 1, nb, loop_body, 0)
        dk_pair = dv_pair = None
        dfrows = jnp.zeros((LANES, t), F32)
        for hd in heads:
            ak, av = acck[hd["hh"]], accv[hd["hh"]]
            dfrows = jnp.where(row == 2 * hp + hd["hh"], -ak[hd["aux"]:hd["aux"] + 1, :], dfrows)
            dk_pair = ak if hd["hh"] == 0 else jnp.where(row < HEAD_DIM, dk_pair, ak)
            dv_pair = av if hd["hh"] == 0 else jnp.where(row < HEAD_DIM, dv_pair, av)
        dk_ref[...] = dk_pair.T.astype(BF)
        dv_ref[...] = dv_pair.T.astype(BF)
        dfk_ref[0] = dfrows.T

        @pl.when(j == nb - 1)
        def _():
            for r in range(nb):
                rows = slice(r * t, (r + 1) * t)
                dq_ref[rows, :] = (jnp.where(lane < HEAD_DIM, dqa_ref[rows, :LANES], dqa_ref[rows, LANES:])
                                   * 0.125).astype(BF)

        pl.when((hp == npair - 1) & (j == nb - 1))(finish)

    pair_cols = pl.BlockSpec((s, 2 * LANES), lambda hp, j: (0, hp))
    pair_rows = pl.BlockSpec((nb, LANES, t), lambda hp, j: (0, hp, 0))
    any_ = pl.BlockSpec(memory_space=pl.ANY)
    return pl.pallas_call(
        body, name="attn_bwd", grid=(npair, nb),
        in_specs=[pl.BlockSpec((t, 2 * LANES), lambda hp, j: (j, hp)),
                  pl.BlockSpec((t, LANES), lambda hp, j: (j, 2 * npair + hp)),
                  pair_cols, pair_cols, pair_rows, pair_rows] + [any_] * n,
        out_specs=[pl.BlockSpec((t, LANES), lambda hp, j: (j, hp)), pl.BlockSpec((t, LANES), lambda hp, j: (j, hp)),
                   pl.BlockSpec((1, t, LANES), lambda hp, j: (hp, j, 0)), pair_cols,
                   pl.BlockSpec((s, LANES), lambda hp, j: (0, hp))] + [any_] * n,
        out_shape=[jax.ShapeDtypeStruct((s, ATTN_W), BF), jax.ShapeDtypeStruct((s, ATTN_W), BF),
                   jax.ShapeDtypeStruct((npair, s, LANES), F32), jax.ShapeDtypeStruct((s, N_HEADS * LANES), F32),
                   jax.ShapeDtypeStruct((s, ATTN_W), BF)]
        + [jax.ShapeDtypeStruct(b.shape, b.dtype) for b in blocks],
        scratch_shapes=[pltpu.VMEM((2, LANES, t), F32), pltpu.VMEM((2, LANES, t), F32)] + _exchange_sems(n),
        compiler_params=_params(VMEM_BIG),
    )(ka, qkv, qab, doa, qt, dot, *blocks)


def _bwd_in(x, dx2, dzc, dq, dqa, dk, dv, dfk, fpre, w_zc, w_qkv, w_f, g_mix, tm, blocks):
    s = x.shape[0]
    nb = s // tm
    n = len(blocks)

    def body(x_ref, dx2_ref, dzc_ref, dq_ref, dqa_ref, dk_ref, dv_ref, dfk_ref, fpre_ref, wzc_ref, wqkv_ref, wf_ref,
             gm_ref, *refs):
        block_refs, (gx_ref, dfb_ref, dgm_ref, dbf_ref), received = refs[:n], refs[n:n + 4], refs[n + 4:2 * n + 4]
        carry = refs[2 * n + 4]
        i = pl.program_id(0)
        start, finish = _scatter_phases(block_refs, received, *refs[2 * n + 5:])
        pl.when(i == 0)(start)

        @pl.when(i == 0)
        def _():
            carry[...] = jnp.zeros_like(carry)
            dgm_ref[...] = jnp.zeros_like(dgm_ref)
            dbf_ref[...] = jnp.zeros_like(dbf_ref)

        lane = _cols((tm, LANES))
        dfq = jnp.zeros((tm, LANES), F32)
        for h in range(N_HEADS):
            aux = LANES * h + HEAD_DIM * (1 - h % 2)
            dfq = jnp.where(lane == h, dqa_ref[:, aux:aux + 1], dfq)

        df_cum = dfq + ((dfk_ref[0] + dfk_ref[1]) + (dfk_ref[2] + dfk_ref[3]))
        triu = (_rows((LANES, LANES)) <= _cols((LANES, LANES))).astype(F32)
        offset = carry[...]
        pieces = []
        for r in reversed(range(tm // LANES)):
            piece = jnp.dot(triu, df_cum[r * LANES:(r + 1) * LANES, :], precision=HI,
                            preferred_element_type=F32) + offset
            offset = piece[0:1, :]
            pieces.append(piece)
        dlogf = jnp.concatenate(pieces[::-1], axis=0)
        carry[...] = offset
        fpre = fpre_ref[...]
        df = jnp.where(_cols(fpre.shape) < N_HEADS, dlogf / (1.0 + jnp.exp(fpre)), 0.0)
        dbf_ref[...] += jnp.sum(df, axis=0, keepdims=True)
        dfb = df.astype(BF)
        dfb_ref[...] = dfb

        dh1 = _dot_nt(dzc_ref[...], wzc_ref[...])
        dh1 += _dot_nt(dq_ref[...], wqkv_ref[:, :ATTN_W])
        dh1 += _dot_nt(dk_ref[...], wqkv_ref[:, ATTN_W:2 * ATTN_W])
        dh1 += _dot_nt(dv_ref[...], wqkv_ref[:, 2 * ATTN_W:])
        dh1 += _dot_nt(dfb, wf_ref[...])
        xv = x_ref[...]
        r1 = _rms(xv)
        dgm_ref[...] += jnp.sum(dh1 * (xv * r1), axis=0, keepdims=True)
        gx_ref[...] = dx2_ref[...] + _rms_bwd(xv, r1, dh1 * gm_ref[...])
        pl.when(i == nb - 1)(finish)

    blk = lambda c: pl.BlockSpec((tm, c), lambda i: (nb - 1 - i, 0))
    any_ = pl.BlockSpec(memory_space=pl.ANY)
    return pl.pallas_call(
        body, name="bwd_in", grid=(nb,),
        in_specs=[blk(D_MODEL), blk(D_MODEL), blk(3 * CONV_CH), blk(ATTN_W), blk(N_HEADS * LANES), blk(ATTN_W),
                  blk(ATTN_W), pl.BlockSpec((N_HEADS // 2, tm, LANES), lambda i: (0, nb - 1 - i, 0)), blk(LANES),
                  _full(w_zc.shape), _full(w_qkv.shape), _full(w_f.shape), _full((1, D_MODEL))] + [any_] * n,
        out_specs=[blk(D_MODEL), blk(LANES), _full((1, D_MODEL)), _full((1, LANES))] + [any_] * n,
        out_shape=[jax.ShapeDtypeStruct((s, D_MODEL), F32), jax.ShapeDtypeStruct((s, LANES), BF),
                   jax.ShapeDtypeStruct((1, D_MODEL), F32), jax.ShapeDtypeStruct((1, LANES), F32)]
        + [jax.ShapeDtypeStruct(b.shape, b.dtype) for b in blocks],
        scratch_shapes=[pltpu.VMEM((1, LANES), F32)] + _exchange_sems(n),
        compiler_params=_params(VMEM_MID),
    )(x, dx2, dzc, dq, dqa, dk, dv, dfk, fpre, w_zc, w_qkv, w_f, g_mix, *blocks)


def _matmul_tn(a, b, bn, bk, name):
    s, m = a.shape
    n = b.shape[1]

    def body(a_ref, b_ref, o_ref):
        @pl.when(pl.program_id(1) == 0)
        def _():
            o_ref[...] = jnp.zeros_like(o_ref)

        o_ref[...] += _dot_tn(a_ref[...], b_ref[...])

    return pl.pallas_call(
        body, name=name, grid=(n // bn, s // bk),
        in_specs=[pl.BlockSpec((bk, m), lambda jn, k: (k, 0)), pl.BlockSpec((bk, bn), lambda jn, k: (k, jn))],
        out_specs=pl.BlockSpec((m, bn), lambda jn, k: (0, jn)),
        out_shape=jax.ShapeDtypeStruct((m, n), F32),
        compiler_params=_params(VMEM_MID),
    )(a, b)


def _matmul_tn_wide(a, b, bn, bk, name):
    s, m = a.shape
    n = b.shape[1]
    nk, nt = s // bk, n // bn

    def body(a_ref, b_ref, o_any, at_scr, acc, sem):
        k = pl.program_id(0)
        j = pl.program_id(1)

        @pl.when(j == 0)
        def _():
            at_scr[...] = a_ref[...].T

        part = _dot(at_scr[...], b_ref[...])

        @pl.when(k == 0)
        def _():
            acc[j] = part

        @pl.when(k > 0)
        def _():
            acc[j] += part

        @pl.when(k == nk - 1)
        def _():
            cp = pltpu.make_async_copy(acc.at[j], o_any.at[:, pl.ds(pl.multiple_of(j * bn, LANES), bn)], sem.at[0])
            cp.start()
            cp.wait()

    return pl.pallas_call(
        body, name=name, grid=(nk, nt),
        in_specs=[pl.BlockSpec((bk, m), lambda k, j: (k, 0)), pl.BlockSpec((bk, bn), lambda k, j: (k, j))],
        out_specs=pl.BlockSpec(memory_space=pl.ANY),
        out_shape=jax.ShapeDtypeStruct((m, n), F32),
        scratch_shapes=[pltpu.VMEM((m, bk), BF), pltpu.VMEM((nt, m, bn), F32), pltpu.SemaphoreType.DMA((1,))],
        compiler_params=_params(VMEM_BIG),
    )(a, b)


def _matmul_tn_multi(a, bs, bk, name):
    s, m = a.shape
    nb = len(bs)

    def body(a_ref, *refs):
        b_refs, o_refs = refs[:nb], refs[nb:]

        @pl.when(pl.program_id(0) == 0)
        def _():
            for o_ref in o_refs:
                o_ref[...] = jnp.zeros_like(o_ref)

        at = a_ref[...].T
        for b_ref, o_ref in zip(b_refs, o_refs):
            o_ref[...] += _dot(at, b_ref[...])

    return pl.pallas_call(
        body, name=name, grid=(s // bk,),
        in_specs=[pl.BlockSpec((bk, m), lambda k: (k, 0))]
        + [pl.BlockSpec((bk, b.shape[1]), lambda k: (k, 0)) for b in bs],
        out_specs=[_full((m, b.shape[1])) for b in bs],
        out_shape=[jax.ShapeDtypeStruct((m, b.shape[1]), F32) for b in bs],
        compiler_params=_params(VMEM_BIG),
    )(a, *bs)


def _flip(v, bit):
    return 1 - v if bit else v


def _all_gather(shards):
    n = len(shards)

    def body(*refs):
        start, forward, finish = _gather_phases(refs[:n], refs[n:2 * n], *refs[2 * n:])
        start()
        forward()
        finish()

    any_ = pl.BlockSpec(memory_space=pl.ANY)
    return pl.pallas_call(
        body, name="all_gather_weights",
        in_specs=[any_] * n, out_specs=[any_] * n,
        out_shape=_gathered_shapes(shards), scratch_shapes=_exchange_sems(n),
    )(*shards)


def _gathered_shapes(shards):
    return [jax.ShapeDtypeStruct((8,) + sh.shape, sh.dtype) for sh in shards]


def _exchange_sems(n):
    return [pltpu.SemaphoreType.DMA((7 * n,)), pltpu.SemaphoreType.DMA((7 * n,)), pltpu.SemaphoreType.DMA((n,))]


def _gather_phases(src, out, send_sems, recv_sems, loc_sems):
    n = len(src)
    x, y, c = lax.axis_index("x"), lax.axis_index("y"), lax.axis_index("c")
    me, sibling = (x, y, c), (x, y, 1 - c)
    chips = [(1 - x, y), (x, 1 - y), (1 - x, 1 - y)]

    def slot(a, px, py, pc):
        return out[a].at[4 * px + 2 * py + pc]

    def copy(a, k, block, to, from_src=False):
        return pltpu.make_async_remote_copy(
            src_ref=src[a] if from_src else slot(a, *block), dst_ref=slot(a, *block),
            send_sem=send_sems.at[7 * a + k], recv_sem=recv_sems.at[7 * a + k],
            device_id=to, device_id_type=MESH)

    def local(a):
        return pltpu.make_async_copy(src[a], slot(a, *me), loc_sems.at[a])

    def first(a):
        return [copy(a, 0, me, sibling, True)] + [copy(a, 1 + j, me, (*chip, c), True)
                                                  for j, chip in enumerate(chips)]

    def passed(a, j):
        return copy(a, 4 + j, (*chips[j], c), sibling)

    def start():
        for a in range(n):
            local(a).start()
            for cp in first(a):
                cp.start()

    def forward():
        for a in range(n):
            for j, chip in enumerate(chips):
                copy(a, 1 + j, (*chip, c), me).wait_recv()
                passed(a, j).start()

    def finish():
        for a in range(n):
            copy(a, 0, sibling, me).wait_recv()
            for j, chip in enumerate(chips):
                copy(a, 4 + j, (*chip, 1 - c), me).wait_recv()
        for a in range(n):
            for cp in first(a) + [passed(a, j) for j in range(3)]:
                cp.wait_send()
            local(a).wait()

    return start, forward, finish


def _scatter_phases(src, out, send_sems, recv_sems, loc_sems):
    n = len(src)
    masks = [((k >> 2) & 1, (k >> 1) & 1, k & 1) for k in range(1, 8)]
    x, y, c = lax.axis_index("x"), lax.axis_index("y"), lax.axis_index("c")
    me = 4 * x + 2 * y + c

    def copies():
        cps = []
        for a in range(n):
            cps.append(pltpu.make_async_copy(src[a].at[me], out[a].at[me], loc_sems.at[a]))
            for k, (mx, my, mc) in enumerate(masks):
                px, py, pc = _flip(x, mx), _flip(y, my), _flip(c, mc)
                cps.append(pltpu.make_async_remote_copy(
                    src_ref=src[a].at[4 * px + 2 * py + pc], dst_ref=out[a].at[me],
                    send_sem=send_sems.at[7 * a + k], recv_sem=recv_sems.at[7 * a + k],
                    device_id=(px, py, pc), device_id_type=MESH))
        return cps

    def start():
        for cp in copies():
            cp.start()

    def finish():
        for cp in copies():
            cp.wait()

    return start, finish


def _all_to_all(blocks, name):
    n = len(blocks)

    def body(*refs):
        start, finish = _scatter_phases(refs[:n], refs[n:2 * n], *refs[2 * n:])
        start()
        finish()

    any_ = pl.BlockSpec(memory_space=pl.ANY)
    return pl.pallas_call(
        body, name=name,
        in_specs=[any_] * n, out_specs=[any_] * n,
        out_shape=[jax.ShapeDtypeStruct(b.shape, b.dtype) for b in blocks], scratch_shapes=_exchange_sems(n),
    )(*blocks)


def _adamw(parts, w, m, v, br, name, tail=None):
    g8, r, c = parts.shape
    c1 = 1.0 - ADAM_B1 ** ADAM_STEP
    c2 = 1.0 - ADAM_B2 ** ADAM_STEP
    extra = [] if tail is None else [tail]

    def body(p_ref, w_ref, m_ref, v_ref, *refs):
        g_ref, d_ref, m2_ref, v2_ref = refs[len(extra):]
        g = p_ref[0].astype(F32)
        for d in range(1, g8):
            g = g + p_ref[d].astype(F32)
        if tail is not None:
            k = tail.shape[1]
            place = (_rows((k, c)) + (c - k) == _cols((k, c))).astype(F32)
            g = g + jnp.dot(refs[0][...], place, precision=HI, preferred_element_type=F32)
        m2 = ADAM_B1 * m_ref[...] + (1.0 - ADAM_B1) * g
        v2 = ADAM_B2 * v_ref[...] + (1.0 - ADAM_B2) * (g * g)
        g_ref[...] = g
        m2_ref[...] = m2
        v2_ref[...] = v2
        d_ref[...] = -ADAM_LR * ((m2 / c1) / (jnp.sqrt(v2 / c2) + ADAM_EPS) + ADAM_WD * w_ref[...])

    blk = pl.BlockSpec((br, c), lambda i: (i, 0))
    out = jax.ShapeDtypeStruct((r, c), F32)
    return pl.pallas_call(
        body, name=name, grid=(r // br,),
        in_specs=[pl.BlockSpec((g8, br, c), lambda i: (0, i, 0)), blk, blk, blk]
        + [pl.BlockSpec((br, e.shape[1]), lambda i: (i, 0)) for e in extra],
        out_specs=[blk] * 4, out_shape=[out] * 4,
        compiler_params=_params(VMEM_MID),
    )(parts, w, m, v, *extra)


def _pad_rows(a, rows):
    return jnp.pad(a, ((0, rows - a.shape[0]), (0, 0)))


_SMALL = (("g_mix", 1024), ("b_f", 8), ("g_conv_out", 512), ("g_attn_out", 512), ("g_ffn", 1024), ("g_final", 1024))
_SMALL_LEN = sum(n for _, n in _SMALL)
_SMALL_ROWS = 104


def _pack_small(vals, extra=None):
    parts = [vals[k].reshape(-1) for k, _ in _SMALL]
    if extra is not None:
        parts.append(extra.reshape(-1))
    flat = jnp.concatenate(parts)
    return jnp.pad(flat, (0, _SMALL_ROWS * LANES - flat.shape[0])).reshape(_SMALL_ROWS, LANES)


def _unpack_small(packed, shapes):
    flat = packed.reshape(-1)
    out, off = {}, 0
    for k, n in _SMALL:
        out[k] = flat[off:off + n].reshape(shapes[k])
        off += n
    return out


def _col_blocks(a):
    r, c8 = a.shape
    return jnp.transpose(a.reshape(r, 8, c8 // 8), (1, 0, 2))


def _from_col_blocks(a):
    g, r, c = a.shape
    return jnp.transpose(a, (1, 0, 2)).reshape(r, g * c)


def kernel(x, g_mix, w_in, b_f, w_conv, g_conv_out, g_attn_out, w_o, g_ffn, w_up, w_ffn_conv, w_down, g_final, loss_target, m_g_mix, m_w_in, m_b_f, m_w_conv, m_g_conv_out, m_g_attn_out, m_w_o, m_g_ffn, m_w_up, m_w_ffn_conv, m_w_down, m_g_final, v_g_mix, v_w_in, v_b_f, v_w_conv, v_g_conv_out, v_g_attn_out, v_w_o, v_g_ffn, v_w_up, v_w_ffn_conv, v_w_down, v_g_final):
    w = dict(g_mix=g_mix, w_in=w_in[0], b_f=b_f, w_conv=w_conv[0], g_conv_out=g_conv_out, g_attn_out=g_attn_out,
             w_o=w_o[0], g_ffn=g_ffn, w_up=w_up[0], w_ffn_conv=w_ffn_conv[0], w_down=w_down[0], g_final=g_final)
    m = dict(g_mix=m_g_mix, w_in=m_w_in[0], b_f=m_b_f, w_conv=m_w_conv[0], g_conv_out=m_g_conv_out,
             g_attn_out=m_g_attn_out, w_o=m_w_o[0], g_ffn=m_g_ffn, w_up=m_w_up[0], w_ffn_conv=m_w_ffn_conv[0],
             w_down=m_w_down[0], g_final=m_g_final)
    v = dict(g_mix=v_g_mix, w_in=v_w_in[0], b_f=v_b_f, w_conv=v_w_conv[0], g_conv_out=v_g_conv_out,
             g_attn_out=v_g_attn_out, w_o=v_w_o[0], g_ffn=v_g_ffn, w_up=v_w_up[0], w_ffn_conv=v_w_ffn_conv[0],
             w_down=v_w_down[0], g_final=v_g_final)
    shapes = dict(g_mix=g_mix.shape, w_in=w_in.shape, b_f=b_f.shape, w_conv=w_conv.shape,
                  g_conv_out=g_conv_out.shape, g_attn_out=g_attn_out.shape, w_o=w_o.shape, g_ffn=g_ffn.shape,
                  w_up=w_up.shape, w_ffn_conv=w_ffn_conv.shape, w_down=w_down.shape, g_final=g_final.shape)

    tm = t = 512
    tf = 256
    xs, tgt = x[0], loss_target[0]
    seq = xs.shape[0]
    assert seq % tm == 0 and seq % tf == 0
    bk = 1024 if seq % 1024 == 0 else 512

    g_in, g_conv = _all_gather([w["w_in"].astype(BF), w["w_conv"]])
    w_in_f = _from_col_blocks(g_in)
    w_zc = w_in_f[:, :3 * CONV_CH]
    w_qkv = w_in_f[:, 3 * CONV_CH:3 * CONV_CH + 3 * ATTN_W]
    w_f = jnp.pad(w_in_f[:, 3 * CONV_CH + 3 * ATTN_W:], ((0, 0), (0, LANES - N_HEADS)))
    b_f_p = jnp.pad(b_f, ((0, 0), (0, LANES - N_HEADS)))
    w_conv_p = _pad_rows(_from_col_blocks(g_conv), 8)

    h1, zc, qkv, fpre, fcol, nc, qa, ka, qt, vt = _fwd_in(
        xs, g_mix, w_zc, w_qkv, w_f, b_f_p, w_conv_p, g_conv_out, tm)
    o, lcol, g_o, g_up, g_dn, g_fc = _attn_fwd(
        qa, ka, vt, t, [w["w_o"].astype(BF), w["w_up"].astype(BF), w["w_down"].astype(BF), w["w_ffn_conv"]])
    w_o_f = g_o.reshape(D_MODEL, D_MODEL)
    w_up_f = _from_col_blocks(g_up)
    w_dn_f = g_dn.reshape(D_FF, D_MODEL)
    w_fc_p = _pad_rows(_from_col_blocks(g_fc), 8)
    x2, h2, mix, u0, uc, act, dx3, dx3b, loss, dg_final = _fwd_ffn(
        xs, nc, o, tgt, g_attn_out, w_o_f, g_ffn, w_up_f, w_fc_p, w_dn_f, g_final.reshape(1, D_MODEL), tf)
    loss_part = loss

    du0, dx2, dx2b, dw_fc, dg_ffn = _bwd_ffn(dx3, dx3b, u0, uc, x2, w_up_f.T, w_fc_p, w_dn_f.T, g_ffn, tf)
    dw_up = _matmul_tn_wide(h2, du0, 1408, 2 * bk if seq % (2 * bk) == 0 else bk, "dw_up")
    dw_dn, = _matmul_tn_multi(act, [dx3b], bk, "dw_down")
    dw_o, = _matmul_tn_multi(mix, [dx2b], bk, "dw_o")
    dzc, doa, dot, qab, dw_conv, dg_attn, dg_conv = _bwd_out(
        dx2b, o, zc, fcol, lcol, qa, w_o_f, g_attn_out, g_conv_out, w_conv_p, t)
    dk, dv, dfk, dqa, dq, r_up, r_dn, r_fc, r_o = _attn_bwd(
        qab, doa, ka, qkv, qt, dot, t,
        [_col_blocks(dw_up), dw_dn.reshape(8, D_FF // 8, D_MODEL), _col_blocks(dw_fc[:3]),
         dw_o.reshape(8, D_MODEL // 8, D_MODEL)])
    dw_zc, dw_q, dw_k, dw_v = _matmul_tn_multi(h1, [dzc, dq, dk, dv], bk, "dw_in")
    dw_main = jnp.concatenate([dw_zc, dw_q, dw_k, dw_v, jnp.zeros((D_MODEL, N_HEADS), F32)], axis=1)
    gx, dfb, dg_mix, db_f, r_in = _bwd_in(xs, dx2, dzc, dq, dqa, dk, dv, dfk, fpre, w_zc, w_qkv, w_f, g_mix, tm,
                                          [_col_blocks(dw_main).astype(BF)])
    dw_f = _matmul_tn(h1, dfb, LANES, bk, "dw_in_f")
    small = dict(g_mix=dg_mix, b_f=db_f[:, :N_HEADS], g_conv_out=dg_conv, g_attn_out=dg_attn, g_ffn=dg_ffn,
                 g_final=dg_final)
    riders = jnp.concatenate([loss_part.reshape(-1), dw_f[:, :N_HEADS].reshape(-1)])
    r_conv, r_small = _all_to_all(
        [_col_blocks(dw_conv[:3]), jnp.broadcast_to(_pack_small(small, riders), (8, _SMALL_ROWS, LANES))],
        "all_to_all_grads")

    g_out, d_out, m_out, v_out = {}, {}, {}, {}
    res = _adamw(r_small, _pack_small(w), _pack_small(m), _pack_small(v), _SMALL_ROWS, "adamw_gains")
    for dst, packed in zip((g_out, d_out, m_out, v_out), res):
        dst.update(_unpack_small(packed, shapes))
    summed = res[0].reshape(-1)
    loss = summed[_SMALL_LEN]
    is_last = (4 * lax.axis_index("x") + 2 * lax.axis_index("y") + lax.axis_index("c") == 7).astype(F32)
    dw_f_sum = summed[_SMALL_LEN + 1:_SMALL_LEN + 1 + D_MODEL * N_HEADS].reshape(D_MODEL, N_HEADS) * is_last

    big = ("w_in", "w_o", "w_up", "w_down", "w_conv", "w_ffn_conv")
    recv = [r_in, r_o, r_up, r_dn, r_conv, r_fc]
    rows = dict(w_in=256, w_o=128, w_up=256, w_down=176, w_conv=3, w_ffn_conv=3)
    for a, k in enumerate(big):
        res = _adamw(recv[a], w[k], m[k], v[k], rows[k], "adamw_" + k, tail=dw_f_sum if k == "w_in" else None)
        g_out[k], d_out[k], m_out[k], v_out[k] = [r.reshape(shapes[k]) for r in res]

    order = ("g_mix", "w_in", "b_f", "w_conv", "g_conv_out", "g_attn_out", "w_o", "g_ffn", "w_up", "w_ffn_conv",
             "w_down", "g_final")
    return (loss, gx.reshape(x.shape), *[g_out[k] for k in order], *[d_out[k] for k in order],
            *[m_out[k] for k in order], *[v_out[k] for k in order])
```
